```python
import math
import jax, jax.numpy as jnp
from jax import lax
import numpy as np

D_MODEL = 1024
BATCH = 8
SEQ = 2048
DEPTH = 1
DEC_BATCH = 128
DEC_SEQ = 4
PAST_LEN = 16384
PAGE_SIZE = 128

A_HEADS = 8
A_KV_HEADS = 2
A_GROUP = A_HEADS // A_KV_HEADS
A_HEAD_DIM = 64
A_WIDTH = A_HEADS * A_HEAD_DIM
A_KV_WIDTH = A_KV_HEADS * A_HEAD_DIM
WINDOW = 128
ROT_DIM = A_HEAD_DIM // 4
ROPE_THETA = 500000.0
DN_HEADS = 4
DN_HEAD_DIM = 128
DN_WIDTH = DN_HEADS * DN_HEAD_DIM
DN_CONV = 4
DN_CHUNK = 64
DT_MIN = 1e-3
DT_MAX = 1e-1
X_HEADS = 4
X_HEAD_DIM = 128
X_WIDTH = X_HEADS * X_HEAD_DIM
N_MEM = 256
D_FF = 2816
FFN_CONV = 3
EPS = 1e-6

IN_SPLITS = (A_WIDTH, A_KV_WIDTH, A_KV_WIDTH,
             DN_WIDTH, DN_WIDTH, DN_WIDTH, DN_WIDTH,
             DN_HEADS, DN_HEADS,
             D_MODEL, D_MODEL)
IN_WIDTH = sum(IN_SPLITS)

kernel_name = "hybrid_swa_gdn_xmem_convffn_step"

F32 = jnp.float32


def rmsnorm(x, g):
    xf = x.astype(F32)
    y = xf * lax.rsqrt(jnp.mean(xf * xf, axis=-1, keepdims=True) + EPS)
    return (y * g.astype(F32)).astype(x.dtype)


def l2norm(t):
    tf = t.astype(F32)
    return tf * lax.rsqrt(jnp.sum(tf * tf, axis=-1, keepdims=True) + EPS)


def split_cols(z, sizes):
    idx = [int(i) for i in np.cumsum(sizes)[:-1]]
    return jnp.split(z, idx, axis=-1)


def rope_partial(x, pos):
    half = ROT_DIM // 2
    inv = ROPE_THETA ** (-2.0 * jnp.arange(half, dtype=F32) / ROT_DIM)
    ang = pos.astype(F32)[:, None] * inv[None, :]
    cos = jnp.cos(ang)[None, :, None, :]
    sin = jnp.sin(ang)[None, :, None, :]
    xr = x[..., :ROT_DIM].astype(F32)
    x1, x2 = xr[..., :half], xr[..., half:]
    rot = jnp.concatenate([x1 * cos - x2 * sin, x2 * cos + x1 * sin], axis=-1).astype(x.dtype)
    return jnp.concatenate([rot, x[..., ROT_DIM:]], axis=-1)


def causal_dwconv(x, buf, w):
    width = w.shape[0]
    L = x.shape[1]
    xp = jnp.concatenate([buf.astype(x.dtype), x], axis=1)
    y = sum(xp[:, j:j + L] * w[j] for j in range(width))
    return y, xp[:, L:]


def window_mask(qpos, kpos):
    d = qpos[..., :, None] - kpos[..., None, :]
    return (d >= 0) & (d < WINDOW) & (kpos[..., None, :] >= 0)


def sink_attention(q, k, v, sinks, mask):
    s = jnp.einsum("bnqkgd,bnskd->bnkgqs", q, k, preferred_element_type=F32) * (A_HEAD_DIM ** -0.5)
    s = jnp.where(mask[None, :, None, None], s, -jnp.inf)
    sink = sinks.astype(F32).reshape(A_KV_HEADS, A_GROUP)[None, None, :, :, None, None]
    m = jnp.maximum(jnp.max(s, axis=-1, keepdims=True), sink)
    p = jnp.exp(s - m)
    p = p / (jnp.sum(p, axis=-1, keepdims=True) + jnp.exp(sink - m))
    return jnp.einsum("bnkgqs,bnskd->bnqkgd", p.astype(v.dtype), v)


def swa_prompt(q, k, v, sinks):
    B, L = q.shape[:2]
    nb = L // WINDOW
    qb = q.reshape(B, nb, WINDOW, A_KV_HEADS, A_GROUP, A_HEAD_DIM)

    def band(t):
        tb = t.reshape(B, nb, WINDOW, A_KV_HEADS, A_HEAD_DIM)
        prev = jnp.concatenate([jnp.zeros_like(tb[:, :1]), tb[:, :-1]], axis=1)
        return jnp.concatenate([prev, tb], axis=2)

    blk = jnp.arange(nb, dtype=jnp.int32)[:, None] * WINDOW
    qpos = blk + jnp.arange(WINDOW, dtype=jnp.int32)[None]
    kpos = blk + jnp.arange(-WINDOW, WINDOW, dtype=jnp.int32)[None]
    o = sink_attention(qb, band(k), band(v), sinks, window_mask(qpos, kpos))
    return o.reshape(B, L, A_WIDTH)


def swa_sample(q, k, v, sinks, buf_k, buf_v, pos):
    B, T = q.shape[:2]
    wb = buf_k.shape[1]
    kk = jnp.concatenate([buf_k.astype(k.dtype), k], axis=1)
    vv = jnp.concatenate([buf_v.astype(v.dtype), v], axis=1)
    qpos = pos[None]
    kpos = (pos[0] - wb + jnp.arange(wb + T, dtype=jnp.int32))[None]
    o = sink_attention(q.reshape(B, 1, T, A_KV_HEADS, A_GROUP, A_HEAD_DIM), kk[:, None], vv[:, None],
                       sinks, window_mask(qpos, kpos))
    return o.reshape(B, T, A_WIDTH), kk[:, -wb:], vv[:, -wb:]


def gated_delta(q, k, v, g, beta, S0, chunk):
    B, L, H, _ = q.shape
    dv = v.shape[-1]
    N = L // chunk

    def to_chunks(t):
        t = t.astype(F32).reshape((B, N, chunk) + t.shape[2:])
        return jnp.swapaxes(t, 2, 3)

    qc, kc, vc, gc, bc = [to_chunks(t) for t in (q, k, v, g, beta)]
    gcum = jnp.cumsum(gc, axis=-1)
    tril = jnp.tril(jnp.ones((chunk, chunk), bool))
    strict = jnp.tril(jnp.ones((chunk, chunk), bool), -1)
    decay = jnp.exp(jnp.where(tril, gcum[..., :, None] - gcum[..., None, :], -jnp.inf))
    kb = kc * bc[..., None]
    A = jnp.where(strict, jnp.einsum("bnhid,bnhjd->bnhij", kb, kc) * decay, 0.0)
    eye = jnp.eye(chunk, dtype=F32)
    rhs = jnp.concatenate([vc * bc[..., None], kb * jnp.exp(gcum)[..., None]], axis=-1)
    sol = lax.linalg.triangular_solve(A + eye, rhs, left_side=True, lower=True)
    u, w = sol[..., :dv], sol[..., dv:]
    qk = jnp.where(tril, jnp.einsum("bnhid,bnhjd->bnhij", qc, kc) * decay, 0.0)
    q_dec = qc * jnp.exp(gcum)[..., None]
    k_dec = kc * jnp.exp(gcum[..., -1:] - gcum)[..., None]
    g_last = jnp.exp(gcum[..., -1])

    def step(S, xs):
        u_i, w_i, qd_i, kd_i, qk_i, gl_i = xs
        v_new = u_i - jnp.einsum("bhck,bhkv->bhcv", w_i, S)
        o = jnp.einsum("bhck,bhkv->bhcv", qd_i, S) + jnp.einsum("bhij,bhjv->bhiv", qk_i, v_new)
        S = S * gl_i[..., None, None] + jnp.einsum("bhck,bhcv->bhkv", kd_i, v_new)
        return S, o

    xs = tuple(jnp.moveaxis(t, 1, 0) for t in (u, w, q_dec, k_dec, qk, g_last))
    S, o = lax.scan(step, S0.astype(F32), xs)
    o = jnp.swapaxes(jnp.moveaxis(o, 0, 1), 2, 3).reshape(B, L, H, dv)
    return o, S


def memory_kv(mem, g, w):
    B, M, _ = mem.shape
    mk, mv = jnp.split(rmsnorm(mem, g) @ w, 2, axis=-1)
    return mk.reshape(B, M, X_HEADS, X_HEAD_DIM), mv.reshape(B, M, X_HEADS, X_HEAD_DIM)


def trunk_layer(x, pos, mem_k, mem_v, win, dn_buf, dn_state, ffn_buf, p, chunk):
    B, L, _ = x.shape
    xn = rmsnorm(x, p["norm_mix_g"])
    aq, ak, av, dq, dk, dvv, dz, da, db, ga, gb = split_cols(xn @ p["w_in"], IN_SPLITS)
    aq = rope_partial(aq.reshape(B, L, A_HEADS, A_HEAD_DIM), pos)
    ak = rope_partial(ak.reshape(B, L, A_KV_HEADS, A_HEAD_DIM), pos)
    av = av.reshape(B, L, A_KV_HEADS, A_HEAD_DIM)
    if win is None:
        ya = swa_prompt(aq, ak, av, p["sinks"])
        wb = min(WINDOW, L)
        new_wk, new_wv = ak[:, L - wb:], av[:, L - wb:]
    else:
        ya, new_wk, new_wv = swa_sample(aq, ak, av, p["sinks"], win[0], win[1], pos)
    qkv, new_dn_buf = causal_dwconv(jnp.concatenate([dq, dk, dvv], axis=-1), dn_buf, p["dn_conv_w"])
    dq, dk, dvv = jnp.split(jax.nn.silu(qkv), 3, axis=-1)
    heads = lambda t: t.reshape(B, L, DN_HEADS, DN_HEAD_DIM)
    qh = l2norm(heads(dq)) * (DN_HEAD_DIM ** -0.5)
    kh = l2norm(heads(dk))
    g = -jnp.exp(p["dn_a_log"].astype(F32)) * jax.nn.softplus(da.astype(F32) + p["dn_dt_bias"].astype(F32))
    beta = jax.nn.sigmoid(db.astype(F32))
    o, new_S = gated_delta(qh, kh, heads(dvv), g, beta, dn_state, chunk)
    o = rmsnorm(o.astype(x.dtype), p["dn_norm_g"]) * jax.nn.silu(heads(dz))
    yb = o.reshape(B, L, DN_WIDTH)
    mix = jax.nn.sigmoid(ga) * (ya @ p["w_br_a"]) + jax.nn.sigmoid(gb) * (yb @ p["w_br_b"])
    h = x + mix @ p["w_mix_out"]
    hq = (rmsnorm(h, p["norm_x_g"]) @ p["w_xq"]).reshape(B, L, X_HEADS, X_HEAD_DIM)
    s = jnp.einsum("blhd,bmhd->bhlm", hq, mem_k.astype(hq.dtype), preferred_element_type=F32) * (X_HEAD_DIM ** -0.5)
    pr = jax.nn.softmax(s, axis=-1).astype(hq.dtype)
    xo = jnp.einsum("bhlm,bmhd->blhd", pr, mem_v.astype(hq.dtype)).reshape(B, L, X_WIDTH)
    h = h + xo @ p["w_xo"]
    u, gv = jnp.split(rmsnorm(h, p["norm_ffn_g"]) @ p["w_up"], 2, axis=-1)
    u, new_ffn_buf = causal_dwconv(u, ffn_buf, p["ffn_conv_w"])
    h = h + (jax.nn.silu(u) * gv) @ p["w_down"]
    return h, (new_wk, new_wv, new_dn_buf, new_S.astype(x.dtype), new_ffn_buf)


def setup_inputs(seed: int = 0) -> dict:
    key = jax.random.key(seed)
    ks = iter(jax.random.split(key, 40))
    nrm = lambda shape, scale=1.0: jax.random.normal(next(ks), shape, F32) * scale
    gain = lambda shape: 1.0 + 0.02 * jax.random.normal(next(ks), shape, F32)
    wb = min(WINDOW, PAST_LEN)
    u = jax.random.uniform(next(ks), (DEPTH, DN_HEADS), F32)
    dt = jnp.exp(u * (math.log(DT_MAX) - math.log(DT_MIN)) + math.log(DT_MIN))
    dn_dt_bias = dt + jnp.log(-jnp.expm1(-dt))
    dn_a_log = jnp.log(jax.random.uniform(next(ks), (DEPTH, DN_HEADS), F32, minval=1.0, maxval=16.0))
    return {
        "x_prompt": nrm((BATCH, SEQ, D_MODEL)),
        "x_sample": nrm((DEC_BATCH, DEC_SEQ, D_MODEL)),
        "mem_prompt": nrm((BATCH, N_MEM, D_MODEL)),
        "cache_win_k": nrm((DEPTH, DEC_BATCH, wb, A_KV_HEADS, A_HEAD_DIM)),
        "cache_win_v": nrm((DEPTH, DEC_BATCH, wb, A_KV_HEADS, A_HEAD_DIM)),
        "state_dn_conv": nrm((DEPTH, DEC_BATCH, DN_CONV - 1, 3 * DN_WIDTH)),
        "state_dn": nrm((DEPTH, DEC_BATCH, DN_HEADS, DN_HEAD_DIM, DN_HEAD_DIM), DN_HEAD_DIM ** -0.5),
        "cache_mem_k": nrm((DEPTH, DEC_BATCH, N_MEM, X_HEADS, X_HEAD_DIM)),
        "cache_mem_v": nrm((DEPTH, DEC_BATCH, N_MEM, X_HEADS, X_HEAD_DIM)),
        "state_ffn_conv": nrm((DEPTH, DEC_BATCH, FFN_CONV - 1, D_FF)),
        "norm_mix_g": gain((DEPTH, D_MODEL)),
        "w_in": nrm((DEPTH, D_MODEL, IN_WIDTH), D_MODEL ** -0.5),
        "dn_conv_w": nrm((DEPTH, DN_CONV, 3 * DN_WIDTH), DN_CONV ** -0.5),
        "dn_a_log": dn_a_log,
        "dn_dt_bias": dn_dt_bias,
        "dn_norm_g": gain((DEPTH, DN_HEAD_DIM)),
        "attn_sinks": nrm((DEPTH, A_HEADS), 0.5),
        "w_br_a": nrm((DEPTH, A_WIDTH, D_MODEL), A_WIDTH ** -0.5),
        "w_br_b": nrm((DEPTH, DN_WIDTH, D_MODEL), DN_WIDTH ** -0.5),
        "w_mix_out": nrm((DEPTH, D_MODEL, D_MODEL), D_MODEL ** -0.5),
        "norm_x_g": gain((DEPTH, D_MODEL)),
        "norm_mem_g": gain((DEPTH, D_MODEL)),
        "w_xq": nrm((DEPTH, D_MODEL, X_WIDTH), D_MODEL ** -0.5),
        "w_xkv": nrm((DEPTH, D_MODEL, 2 * X_WIDTH), D_MODEL ** -0.5),
        "w_xo": nrm((DEPTH, X_WIDTH, D_MODEL), X_WIDTH ** -0.5),
        "norm_ffn_g": gain((DEPTH, D_MODEL)),
        "w_up": nrm((DEPTH, D_MODEL, 2 * D_FF), D_MODEL ** -0.5),
        "ffn_conv_w": nrm((DEPTH, FFN_CONV, D_FF), FFN_CONV ** -0.5),
        "w_down": nrm((DEPTH, D_FF, D_MODEL), D_FF ** -0.5),
        "final_norm_g": gain((D_MODEL,)),
    }


def reference(x_prompt, x_sample, mem_prompt, cache_win_k, cache_win_v, state_dn_conv, state_dn,
              cache_mem_k, cache_mem_v, state_ffn_conv, norm_mix_g, w_in, dn_conv_w, dn_a_log,
              dn_dt_bias, dn_norm_g, attn_sinks, w_br_a, w_br_b, w_mix_out, norm_x_g, norm_mem_g,
              w_xq, w_xkv, w_xo, norm_ffn_g, w_up, ffn_conv_w, w_down, final_norm_g):
    Bp, Lp, _ = x_prompt.shape
    Ls = x_sample.shape[1]
    pos_p = jnp.arange(Lp, dtype=jnp.int32)
    pos_s = PAST_LEN + jnp.arange(Ls, dtype=jnp.int32)
    dt = x_prompt.dtype
    hp, hs = x_prompt, x_sample
    new_p = [[] for _ in range(5)]
    new_s = [[] for _ in range(5)]
    mem_k_list, mem_v_list = [], []
    for l in range(DEPTH):
        p = {"norm_mix_g": norm_mix_g[l], "w_in": w_in[l], "dn_conv_w": dn_conv_w[l],
             "dn_a_log": dn_a_log[l], "dn_dt_bias": dn_dt_bias[l], "dn_norm_g": dn_norm_g[l],
             "sinks": attn_sinks[l], "w_br_a": w_br_a[l], "w_br_b": w_br_b[l],
             "w_mix_out": w_mix_out[l], "norm_x_g": norm_x_g[l], "w_xq": w_xq[l], "w_xo": w_xo[l],
             "norm_ffn_g": norm_ffn_g[l], "w_up": w_up[l], "ffn_conv_w": ffn_conv_w[l],
             "w_down": w_down[l]}
        mk, mv = memory_kv(mem_prompt, norm_mem_g[l], w_xkv[l])
        hp, sp = trunk_layer(hp, pos_p, mk, mv, None,
                             jnp.zeros((Bp, DN_CONV - 1, 3 * DN_WIDTH), dt),
                             jnp.zeros((Bp, DN_HEADS, DN_HEAD_DIM, DN_HEAD_DIM), dt),
                             jnp.zeros((Bp, FFN_CONV - 1, D_FF), dt), p, min(DN_CHUNK, Lp))
        hs, ss = trunk_layer(hs, pos_s, cache_mem_k[l], cache_mem_v[l], (cache_win_k[l], cache_win_v[l]),
                             state_dn_conv[l], state_dn[l], state_ffn_conv[l], p, Ls)
        for lst, t in zip(new_p, sp):
            lst.append(t)
        for lst, t in zip(new_s, ss):
            lst.append(t)
        mem_k_list.append(mk)
        mem_v_list.append(mv)
    y_prompt = rmsnorm(hp, final_norm_g)
    y_sample = rmsnorm(hs, final_norm_g)
    p_win_k, p_win_v, p_dn_conv, p_dn_state, p_ffn_conv = [jnp.stack(t) for t in new_p]
    s_win_k, s_win_v, s_dn_conv, s_dn_state, s_ffn_conv = [jnp.stack(t) for t in new_s]
    p_mem_k = jnp.stack(mem_k_list)
    p_mem_v = jnp.stack(mem_v_list)
    return (y_prompt, y_sample, p_win_k, p_win_v, p_dn_conv, p_dn_state, p_mem_k, p_mem_v, p_ffn_conv,
            s_win_k, s_win_v, s_dn_conv, s_dn_state, s_ffn_conv)
```

```python
import functools
import math

import jax
import jax.numpy as jnp
from jax import lax
from jax.experimental import pallas as pl
from jax.experimental.pallas import tpu as pltpu

F32 = jnp.float32
BF16 = jnp.bfloat16

D_MODEL = 1024
A_HEADS = 8
A_KV_HEADS = 2
A_HEAD_DIM = 64
A_WIDTH = 512
A_KV_WIDTH = 128
WINDOW = 128
ROT_DIM = 16
ROPE_THETA = 500000.0
DN_HEADS = 4
DN_HEAD_DIM = 128
DN_WIDTH = 512
DN_CONV = 4
X_HEADS = 4
X_HEAD_DIM = 128
X_WIDTH = 512
D_FF = 2816
FFN_CONV = 3
EPS = 1e-6
PAST_LEN = 16384

LANES = 128
SUBLANES = 8
VMEM_LIMIT = 56 * 1024 * 1024
CHUNK = 128
SAMPLE_ROWS = 16


def _cparams(*sem):
    return pltpu.CompilerParams(dimension_semantics=sem, vmem_limit_bytes=VMEM_LIMIT)


def _resident(shape):
    return pl.BlockSpec(shape, lambda *_: (0,) * len(shape), pipeline_mode=pl.Buffered(1))


def _rms(x, g):
    return x * lax.rsqrt(jnp.mean(x * x, axis=-1, keepdims=True) + EPS) * g


def _dot(a, b):
    return jnp.dot(a, b, preferred_element_type=F32)


def _dot_nt(a, b):
    return lax.dot_general(a, b, (((1,), (1,)), ((), ())), preferred_element_type=F32)


def _silu(x):
    return x * jax.nn.sigmoid(x)


def _rope(seg, c, s1, s2):
    return seg * c + pltpu.roll(seg, LANES - 8, 1) * s1 + pltpu.roll(seg, 8, 1) * s2


def _proj_kernel(x_ref, g_ref, c_ref, s1_ref, s2_ref, wqkv_ref, wd_ref, wab_ref, wg_ref,
                 q_ref, k_ref, v_ref, dqkv_ref, dz_ref, ab_ref, ga_ref, gb_ref):
    xb = _rms(x_ref[...], g_ref[...]).astype(BF16)
    c, s1, s2 = c_ref[...], s1_ref[...], s2_ref[...]
    z = _dot(xb, wqkv_ref[...])
    for i in range(A_WIDTH // LANES):
        sl = slice(i * LANES, (i + 1) * LANES)
        q_ref[:, sl] = _rope(z[:, sl], c, s1, s2).astype(BF16)
    k_ref[...] = _rope(z[:, A_WIDTH:A_WIDTH + LANES], c, s1, s2)
    v_ref[...] = z[:, A_WIDTH + LANES:]
    z = _dot(xb, wd_ref[...])
    dqkv_ref[...] = z[:, :3 * DN_WIDTH]
    dz_ref[...] = z[:, 3 * DN_WIDTH:]
    ab_ref[...] = _dot(xb, wab_ref[...])
    z = _dot(xb, wg_ref[...])
    ga_ref[...] = z[:, :D_MODEL]
    gb_ref[...] = z[:, D_MODEL:]


def _proj(x, g, tabs, wqkv, wd, wab, wg, tm):
    m = x.shape[0]
    nt = tabs[0].shape[0] // tm
    row = lambda w: pl.BlockSpec((tm, w), lambda i: (i, 0))
    tab = pl.BlockSpec((tm, LANES), lambda i: (i % nt, 0))
    widths = (A_WIDTH, LANES, LANES, 3 * DN_WIDTH, DN_WIDTH, LANES, D_MODEL, D_MODEL)
    dts = (BF16, F32, F32, F32, F32, F32, F32, F32)
    return pl.pallas_call(
        _proj_kernel,
        grid=(m // tm,),
        in_specs=[row(D_MODEL), _resident((1, D_MODEL)), tab, tab, tab,
                  _resident(wqkv.shape), _resident(wd.shape), _resident(wab.shape), _resident(wg.shape)],
        out_specs=[row(w) for w in widths],
        out_shape=[jax.ShapeDtypeStruct((m, w), d) for w, d in zip(widths, dts)],
        compiler_params=_cparams("parallel"),
        name="in_proj",
    )(x, g, *tabs, wqkv, wd, wab, wg)


def _rope_tables(pos):
    half = ROT_DIM // 2
    inv = ROPE_THETA ** (-2.0 * jnp.arange(half, dtype=F32) / ROT_DIM)
    ang = pos.astype(F32)[:, None] * inv[None, :]
    c, s = jnp.cos(ang), jnp.sin(ang)
    n = pos.shape[0]
    one = jnp.ones((n, A_HEAD_DIM - ROT_DIM), F32)
    z8 = jnp.zeros((n, half), F32)
    z48 = jnp.zeros((n, A_HEAD_DIM - ROT_DIM), F32)
    ct = jnp.concatenate([c, c, one], axis=1)
    s1 = jnp.concatenate([-s, z8, z48], axis=1)
    s2 = jnp.concatenate([z8, s, z48], axis=1)
    two = lambda t: jnp.concatenate([t, t], axis=1)
    return two(ct), two(s1), two(s2)


def _both_halves(t, lane_lo):
    r = pltpu.roll(t, A_HEAD_DIM, 1)
    return jnp.where(lane_lo, t, r), jnp.where(lane_lo, r, t)


def _sink_softmax(s, valid, sink):
    s = jnp.where(valid, s, -jnp.inf)
    m = jnp.maximum(jnp.max(s, axis=-1, keepdims=True), sink)
    p = jnp.exp(s - m)
    den = jnp.sum(p, axis=-1, keepdims=True) + jnp.exp(sink - m)
    return p / den


def _swa_prompt_kernel(sink_ref, q_ref, kp_ref, kc_ref, vp_ref, vc_ref, o_ref):
    i = pl.program_id(1)
    w = WINDOW
    k = jnp.concatenate([kp_ref[...], kc_ref[...]], axis=0)
    v = jnp.concatenate([vp_ref[...], vc_ref[...]], axis=0)
    lane_lo_k = lax.broadcasted_iota(jnp.int32, k.shape, 1) < A_HEAD_DIM
    kk = [t.astype(BF16) for t in _both_halves(k, lane_lo_k)]
    vv = [t.astype(BF16) for t in _both_halves(v, lane_lo_k)]
    lane_lo = lax.broadcasted_iota(jnp.int32, (w, LANES), 1) < A_HEAD_DIM
    row = lax.broadcasted_iota(jnp.int32, (4 * w, 2 * w), 0)
    col = lax.broadcasted_iota(jnp.int32, (4 * w, 2 * w), 1)
    d = (row & (w - 1)) + w - col
    valid = (d >= 0) & (d < w) & jnp.logical_or(col >= w, i > 0)
    hrow = lax.broadcasted_iota(jnp.int32, (4 * w, 1), 0) // w
    zero = jnp.zeros((), BF16)
    for g in range(A_KV_HEADS):
        parts = []
        for sgm in range(2):
            seg = q_ref[:, (2 * g + sgm) * LANES:(2 * g + sgm + 1) * LANES]
            parts += [jnp.where(lane_lo, seg, zero), jnp.where(lane_lo, zero, seg)]
        qs = jnp.concatenate(parts, axis=0)
        sink = jnp.zeros((4 * w, 1), F32)
        for j in range(4):
            sink = jnp.where(hrow == j, sink_ref[4 * g + j], sink)
        s = _dot_nt(qs, kk[g]) * (A_HEAD_DIM ** -0.5)
        p = _sink_softmax(s, valid, sink).astype(BF16)
        o = _dot(p, vv[g])
        for sgm in range(2):
            o_ref[:, (2 * g + sgm) * LANES:(2 * g + sgm + 1) * LANES] = jnp.where(
                lane_lo, o[(2 * sgm) * w:(2 * sgm + 1) * w], o[(2 * sgm + 1) * w:(2 * sgm + 2) * w]
            ).astype(BF16)


def _swa_prompt(q, k, v, sinks, nseq, seqlen):
    w = WINDOW
    nb = seqlen // w
    cur = lambda b, i: (b * nb + i, 0)
    prev = lambda b, i: (b * nb + jnp.maximum(i - 1, 0), 0)
    kv = lambda im: pl.BlockSpec((w, A_KV_WIDTH), im)
    return pl.pallas_call(
        _swa_prompt_kernel,
        grid=(nseq, nb),
        in_specs=[pl.BlockSpec(memory_space=pltpu.SMEM), pl.BlockSpec((w, A_WIDTH), cur),
                  kv(prev), kv(cur), kv(prev), kv(cur)],
        out_specs=pl.BlockSpec((w, A_WIDTH), cur),
        out_shape=jax.ShapeDtypeStruct(q.shape, BF16),
        compiler_params=_cparams("parallel", "parallel"),
        name="swa_prompt",
    )(sinks, q, k, k, v, v)


def _swa_sample_kernel(sink_ref, q_ref, ck_ref, cv_ref, kn_ref, vn_ref, o_ref, *, nb, t):
    w = ck_ref.shape[1]
    nq = q_ref.shape[1]
    rows = nq // A_HEADS
    r = lax.broadcasted_iota(jnp.int32, (nq, w), 0)
    c = lax.broadcasted_iota(jnp.int32, (nq, w), 1)
    tq = r % rows
    valid_c = c > tq
    rn = lax.broadcasted_iota(jnp.int32, (nq, SAMPLE_ROWS), 0) % rows
    cn = lax.broadcasted_iota(jnp.int32, (nq, SAMPLE_ROWS), 1)
    valid_n = (cn <= rn) & (cn < t)
    hrow = lax.broadcasted_iota(jnp.int32, (nq, 1), 0) // rows
    sink = jnp.zeros((nq, 1), F32)
    for h in range(A_HEADS):
        sink = jnp.where(hrow == h, sink_ref[h], sink)
    for b in range(nb):
        q = q_ref[b]
        ck = ck_ref[b].astype(BF16)
        cv = cv_ref[b].astype(BF16)
        kn = kn_ref[b]
        vn = vn_ref[b]
        sc = jnp.where(valid_c, _dot_nt(q, ck) * (A_HEAD_DIM ** -0.5), -jnp.inf)
        sn = jnp.where(valid_n, _dot_nt(q, kn) * (A_HEAD_DIM ** -0.5), -jnp.inf)
        m = jnp.maximum(jnp.maximum(jnp.max(sc, -1, keepdims=True), jnp.max(sn, -1, keepdims=True)), sink)
        pc = jnp.exp(sc - m)
        pn = jnp.exp(sn - m)
        den = jnp.sum(pc, -1, keepdims=True) + jnp.sum(pn, -1, keepdims=True) + jnp.exp(sink - m)
        o_ref[b] = _dot((pc / den).astype(BF16), cv) + _dot((pn / den).astype(BF16), vn)


def _swa_sample(q8, ck, cv, kn, vn, sinks, t, nb=8):
    bsz, nq, _ = q8.shape
    w = ck.shape[1]
    blk = lambda r: pl.BlockSpec((nb, r, LANES), lambda i: (i, 0, 0))
    return pl.pallas_call(
        functools.partial(_swa_sample_kernel, nb=nb, t=t),
        grid=(bsz // nb,),
        in_specs=[pl.BlockSpec(memory_space=pltpu.SMEM), blk(nq), blk(w), blk(w),
                  blk(SAMPLE_ROWS), blk(SAMPLE_ROWS)],
        out_specs=blk(nq),
        out_shape=jax.ShapeDtypeStruct((bsz, nq, LANES), F32),
        compiler_params=_cparams("parallel"),
        name="swa_sample",
    )(sinks, q8, ck, cv, kn, vn)


def _lane_bcast(x, lane):
    return jnp.broadcast_to(x[:, lane:lane + 1], (x.shape[0], LANES))


def _cumsum_rows(x, block):
    row = lax.broadcasted_iota(jnp.int32, x.shape, 0) % block
    sh = 1
    while sh < block:
        x = x + jnp.where(row >= sh, pltpu.roll(x, sh, 0), 0.0)
        sh *= 2
    return x


def _l2n(t):
    return t * lax.rsqrt(jnp.sum(t * t, axis=-1, keepdims=True) + EPS)


def _gates(ab, alog, dtb):
    x = ab + dtb
    sp = jnp.maximum(x, 0.0) + jnp.log1p(jnp.exp(-jnp.abs(x)))
    return -jnp.exp(alog) * sp, jax.nn.sigmoid(ab)


def _merge_masks(c, top):
    rowi = lax.broadcasted_iota(jnp.int32, (c, c), 0)
    coli = lax.broadcasted_iota(jnp.int32, (c, c), 1)
    masks = []
    s = 1
    while s < top:
        rb, cb = rowi // s, coli // s
        masks.append(((rb // 2) == (cb // 2)) & ((rb % 2) == 1) & ((cb % 2) == 0))
        s *= 2
    return masks


def _chunk_local(q, k, v, gcol, grow, beta, strict, tril, merges):
    c = q.shape[0]
    decay = jnp.exp(jnp.where(tril, gcol - grow, -jnp.inf))
    kb = k * beta
    eg = jnp.exp(gcol)
    m = _dot_nt(jnp.concatenate([q, kb], axis=0).astype(BF16), k.astype(BF16))
    qk = jnp.where(tril, m[:c] * decay, 0.0)
    a = jnp.where(strict, m[c:] * decay, 0.0)
    n = -jnp.where(merges[0], a, 0.0) if merges else jnp.zeros_like(a)
    for off in merges[1:]:
        ao = jnp.where(off, a, 0.0)
        z = ao + _dot(ao.astype(BF16), n.astype(BF16))
        n = n - z - _dot(n.astype(BF16), z.astype(BF16))
    rhs = jnp.concatenate([v * beta, kb * eg], axis=1)
    uw = rhs + _dot(n.astype(BF16), rhs.astype(BF16))
    return uw[:, :LANES], uw[:, LANES:], qk, q * eg


def _gdn_prompt_kernel(z_ref, dz_ref, ab_ref, cw_ref, alog_ref, dtb_ref, ng_ref,
                       y_ref, sout_ref, xp_ref, s_ref, *, nc):
    c = CHUNK
    r = nc * c
    pad = SUBLANES

    @pl.when(pl.program_id(1) == 0)
    def _():
        xp_ref[0:pad, :] = jnp.zeros((pad, 3 * DN_WIDTH), F32)
        s_ref[...] = jnp.zeros_like(s_ref)

    xp_ref[pad:pad + r, :] = z_ref[...]
    cw = cw_ref[...]
    conv = xp_ref[pad:pad + r, :] * cw[3:4]
    for j in range(DN_CONV - 1):
        conv = conv + xp_ref[pad - 3 + j:pad - 3 + j + r, :] * cw[j:j + 1]
    xp_ref[0:pad, :] = xp_ref[r:r + pad, :]
    conv = _silu(conv)
    g, beta = _gates(ab_ref[...], alog_ref[...], dtb_ref[...])
    rowi = lax.broadcasted_iota(jnp.int32, (c, c), 0)
    coli = lax.broadcasted_iota(jnp.int32, (c, c), 1)
    strict, tril = rowi > coli, rowi >= coli
    merges = _merge_masks(c, c)
    ng = ng_ref[...]
    for ci in range(nc):
        rs = slice(ci * c, (ci + 1) * c)
        gc = _cumsum_rows(g[rs], c)
        gct = gc.T
        for h in range(DN_HEADS):
            hs = slice(h * LANES, (h + 1) * LANES)
            q = _l2n(conv[rs, hs]) * (DN_HEAD_DIM ** -0.5)
            k = _l2n(conv[rs, DN_WIDTH + h * LANES:DN_WIDTH + (h + 1) * LANES])
            v = conv[rs, 2 * DN_WIDTH + h * LANES:2 * DN_WIDTH + (h + 1) * LANES]
            gcol = _lane_bcast(gc, h)
            u, w, qk, qd = _chunk_local(q, k, v, gcol, gct[h:h + 1, :], _lane_bcast(beta[rs], DN_HEADS + h),
                                        strict, tril, merges)
            glast = gcol[c - 1:c, :]
            kd = k * jnp.exp(glast - gcol)
            s = s_ref[h]
            r2 = _dot(jnp.concatenate([w, qd], axis=0).astype(BF16), s.astype(BF16))
            vnew = u - r2[:c]
            r3 = _dot(jnp.concatenate([qk, kd.T], axis=0).astype(BF16), vnew.astype(BF16))
            s_ref[h] = s * jnp.exp(glast) + r3[c:]
            o = r2[c:] + r3[:c]
            y_ref[rs, hs] = (_rms(o, ng) * _silu(dz_ref[rs, hs])).astype(BF16)

    @pl.when(pl.program_id(1) == pl.num_programs(1) - 1)
    def _():
        sout_ref[0] = s_ref[...]


def _gdn_prompt(dqkv, dz, ab, cw, alog, dtb, ng, nseq, seqlen, nc=2):
    r = nc * CHUNK
    ns = seqlen // r
    row = lambda w: pl.BlockSpec((r, w), lambda b, s: (b * ns + s, 0))
    return pl.pallas_call(
        functools.partial(_gdn_prompt_kernel, nc=nc),
        grid=(nseq, ns),
        in_specs=[row(3 * DN_WIDTH), row(DN_WIDTH), row(LANES), _resident(cw.shape),
                  _resident((1, LANES)), _resident((1, LANES)), _resident((1, LANES))],
        out_specs=[row(DN_WIDTH),
                   pl.BlockSpec((1, DN_HEADS, DN_HEAD_DIM, DN_HEAD_DIM), lambda b, s: (b, 0, 0, 0))],
        out_shape=[jax.ShapeDtypeStruct((nseq * seqlen, DN_WIDTH), BF16),
                   jax.ShapeDtypeStruct((nseq, DN_HEADS, DN_HEAD_DIM, DN_HEAD_DIM), F32)],
        scratch_shapes=[pltpu.VMEM((r + SUBLANES, 3 * DN_WIDTH), F32),
                        pltpu.VMEM((DN_HEADS, DN_HEAD_DIM, DN_HEAD_DIM), F32)],
        compiler_params=_cparams("parallel", "arbitrary"),
        name="gdn_prompt",
    )(dqkv, dz, ab, cw, alog, dtb, ng)


def _gdn_sample_kernel(z_ref, dz_ref, ab_ref, s0_ref, cw_ref, alog_ref, dtb_ref, ng_ref,
                       y_ref, sout_ref, *, t):
    c = CHUNK
    sr = SAMPLE_ROWS
    nb = c // sr
    off = SUBLANES - (DN_CONV - 1)
    cw = cw_ref[...]
    conv = z_ref[0, off + 3:off + 3 + c, :] * cw[3:4]
    for j in range(DN_CONV - 1):
        conv = conv + z_ref[0, off + j:off + j + c, :] * cw[j:j + 1]
    conv = _silu(conv)
    live = (lax.broadcasted_iota(jnp.int32, (c, LANES), 0) % sr) < t
    g, beta = _gates(ab_ref[...], alog_ref[...], dtb_ref[...])
    g = jnp.where(live, g, 0.0)
    beta = jnp.where(live, beta, 0.0)
    gc = _cumsum_rows(g, sr)
    gct = gc.T
    rowi = lax.broadcasted_iota(jnp.int32, (c, c), 0)
    coli = lax.broadcasted_iota(jnp.int32, (c, c), 1)
    same = (rowi // sr) == (coli // sr)
    strict, tril = same & (rowi > coli), same & (rowi >= coli)
    merges = _merge_masks(c, pl.next_power_of_2(t))
    ng = ng_ref[...]
    rowb = lax.broadcasted_iota(jnp.int32, (c, LANES), 0) // sr
    for h in range(DN_HEADS):
        hs = slice(h * LANES, (h + 1) * LANES)
        q = _l2n(conv[:, hs]) * (DN_HEAD_DIM ** -0.5)
        k = _l2n(conv[:, DN_WIDTH + h * LANES:DN_WIDTH + (h + 1) * LANES])
        v = conv[:, 2 * DN_WIDTH + h * LANES:2 * DN_WIDTH + (h + 1) * LANES]
        gcol = _lane_bcast(gc, h)
        u, w, qk, qd = _chunk_local(q, k, v, gcol, gct[h:h + 1, :], _lane_bcast(beta, DN_HEADS + h),
                                    strict, tril, merges)
        wq = jnp.concatenate([w, qd], axis=1).astype(BF16)
        vnews, qss, sbs = [], [], []
        for b in range(nb):
            rs = slice(b * sr, (b + 1) * sr)
            s = s0_ref[b, h]
            sbs.append(s)
            r2 = _dot(jnp.concatenate([wq[rs, :LANES], wq[rs, LANES:]], axis=0), s.astype(BF16))
            vnews.append(u[rs] - r2[:sr])
            qss.append(r2[sr:])
        vnew = jnp.concatenate(vnews, axis=0)
        o = jnp.concatenate(qss, axis=0) + _dot(qk.astype(BF16), vnew.astype(BF16))
        y_ref[:, hs] = _rms(o, ng) * _silu(dz_ref[:, hs])
        glast = jnp.concatenate(
            [jnp.broadcast_to(gcol[b * sr + sr - 1:b * sr + sr, :], (sr, LANES)) for b in range(nb)], axis=0)
        kdt = (k * jnp.exp(glast - gcol)).T.astype(BF16)
        for b in range(nb):
            vb = jnp.where(rowb == b, vnew, 0.0).astype(BF16)
            sout_ref[b, h] = sbs[b] * jnp.exp(glast[b * sr:b * sr + 1, :]) + _dot(kdt, vb)


def _gdn_sample(zg, dz, ab, s0, cw, alog, dtb, ng, t):
    ngrp = zg.shape[0]
    nb = CHUNK // SAMPLE_ROWS
    row = lambda w: pl.BlockSpec((CHUNK, w), lambda i: (i, 0))
    st = pl.BlockSpec((nb, DN_HEADS, DN_HEAD_DIM, DN_HEAD_DIM), lambda i: (i, 0, 0, 0))
    return pl.pallas_call(
        functools.partial(_gdn_sample_kernel, t=t),
        grid=(ngrp,),
        in_specs=[pl.BlockSpec((1, CHUNK + SUBLANES, 3 * DN_WIDTH), lambda i: (i, 0, 0)),
                  row(DN_WIDTH), row(LANES), st, _resident(cw.shape),
                  _resident((1, LANES)), _resident((1, LANES)), _resident((1, LANES))],
        out_specs=[row(DN_WIDTH), st],
        out_shape=[jax.ShapeDtypeStruct((ngrp * CHUNK, DN_WIDTH), F32),
                   jax.ShapeDtypeStruct(s0.shape, F32)],
        compiler_params=_cparams("parallel"),
        name="gdn_sample",
    )(zg, dz, ab, s0, cw, alog, dtb, ng)


def _merge_kernel(x_ref, ya_ref, yb_ref, ga_ref, gb_ref, wa_ref, wb_ref, wo_ref, g_ref, wq_ref,
                  h_ref, hq_ref):
    mix = (jax.nn.sigmoid(ga_ref[...]) * _dot(ya_ref[...], wa_ref[...])
           + jax.nn.sigmoid(gb_ref[...]) * _dot(yb_ref[...], wb_ref[...]))
    h = x_ref[...] + _dot(mix.astype(BF16), wo_ref[...])
    h_ref[...] = h
    hq_ref[...] = _dot(_rms(h, g_ref[...]).astype(BF16), wq_ref[...]).astype(BF16)


def _merge(x, ya, yb, ga, gb, wa, wb, wo, g, wq, tm):
    m = x.shape[0]
    row = lambda w: pl.BlockSpec((tm, w), lambda i: (i, 0))
    return pl.pallas_call(
        _merge_kernel,
        grid=(m // tm,),
        in_specs=[row(D_MODEL), row(A_WIDTH), row(DN_WIDTH), row(D_MODEL), row(D_MODEL),
                  _resident(wa.shape), _resident(wb.shape), _resident(wo.shape),
                  _resident((1, D_MODEL)), _resident(wq.shape)],
        out_specs=[row(D_MODEL), row(X_WIDTH)],
        out_shape=[jax.ShapeDtypeStruct((m, D_MODEL), F32), jax.ShapeDtypeStruct((m, X_WIDTH), BF16)],
        compiler_params=_cparams("parallel"),
        name="merge",
    )(x, ya, yb, ga, gb, wa, wb, wo, g, wq)


def _memkv_kernel(m_ref, g_ref, w_ref, k_ref, v_ref):
    z = _dot(_rms(m_ref[...], g_ref[...]).astype(BF16), w_ref[...])
    k_ref[...] = z[:, :X_WIDTH]
    v_ref[...] = z[:, X_WIDTH:]


def _memkv(mem, g, w, tm):
    m = mem.shape[0]
    row = lambda wd: pl.BlockSpec((tm, wd), lambda i: (i, 0))
    return pl.pallas_call(
        _memkv_kernel,
        grid=(m // tm,),
        in_specs=[row(D_MODEL), _resident((1, D_MODEL)), _resident(w.shape)],
        out_specs=[row(X_WIDTH), row(X_WIDTH)],
        out_shape=[jax.ShapeDtypeStruct((m, X_WIDTH), F32)] * 2,
        compiler_params=_cparams("parallel"),
        name="mem_kv",
    )(mem, g, w)


def _xattn_heads(hq, mk, mv):
    outs = []
    for h in range(X_HEADS):
        hs = slice(h * LANES, (h + 1) * LANES)
        s = _dot_nt(hq[:, hs], mk[:, hs]) * (X_HEAD_DIM ** -0.5)
        e = jnp.exp(s - jnp.max(s, axis=-1, keepdims=True))
        p = e / jnp.sum(e, axis=-1, keepdims=True)
        outs.append(_dot(p.astype(BF16), mv[:, hs]))
    return jnp.concatenate(outs, axis=1)


def _xattn_prompt_kernel(hq_ref, mk_ref, mv_ref, o_ref):
    o_ref[...] = _xattn_heads(hq_ref[...], mk_ref[...].astype(BF16), mv_ref[...].astype(BF16)).astype(BF16)


def _xattn_prompt(hq, mk, mv, nseq, seqlen, tq):
    nt = seqlen // tq
    nm = mk.shape[0] // nseq
    row = pl.BlockSpec((tq, X_WIDTH), lambda b, i: (b * nt + i, 0))
    mem = pl.BlockSpec((nm, X_WIDTH), lambda b, i: (b, 0))
    return pl.pallas_call(
        _xattn_prompt_kernel,
        grid=(nseq, nt),
        in_specs=[row, mem, mem],
        out_specs=row,
        out_shape=jax.ShapeDtypeStruct(hq.shape, BF16),
        compiler_params=_cparams("parallel", "parallel"),
        name="xattn_prompt",
    )(hq, mk, mv)


def _xattn_sample_kernel(hq_ref, mk_ref, mv_ref, o_ref, *, nb):
    for b in range(nb):
        o_ref[b] = _xattn_heads(hq_ref[b], mk_ref[b].astype(BF16), mv_ref[b].astype(BF16)).astype(BF16)


def _xattn_sample(hq, mk, mv, nb=8):
    bsz, rows, _ = hq.shape
    nm = mk.shape[1]
    q = pl.BlockSpec((nb, rows, X_WIDTH), lambda i: (i, 0, 0))
    mem = pl.BlockSpec((nb, nm, X_WIDTH), lambda i: (i, 0, 0))
    return pl.pallas_call(
        functools.partial(_xattn_sample_kernel, nb=nb),
        grid=(bsz // nb,),
        in_specs=[q, mem, mem],
        out_specs=q,
        out_shape=jax.ShapeDtypeStruct(hq.shape, BF16),
        compiler_params=_cparams("parallel"),
        name="xattn_sample",
    )(hq, mk, mv)


def _ffn_kernel(h_ref, xo_ref, cin_ref, wxo_ref, g_ref, wu_ref, wv_ref, cw_ref, wd_ref, fg_ref,
                y_ref, tail_ref, xp_ref, acc_ref, *, stride, padc, fc):
    tm = h_ref.shape[0]

    @pl.when(pl.program_id(1) == 0)
    def _():
        tail_ref[...] = cin_ref[...]

    h = h_ref[...] + _dot(xo_ref[...], wxo_ref[...])
    acc_ref[...] = h
    hn = _rms(h, g_ref[...]).astype(BF16)
    for ci in range(D_FF // fc):
        cs = slice(ci * fc, (ci + 1) * fc)
        u = _dot(hn, wu_ref[:, cs])
        gate = _dot(hn, wv_ref[:, cs])
        xp_ref[0:padc, :] = tail_ref[0, :, cs]
        xp_ref[padc:padc + tm, :] = u
        cw = cw_ref[:, cs]
        uc = u * cw[2:3]
        for j in range(FFN_CONV - 1):
            o = padc - (FFN_CONV - 1 - j) * stride
            uc = uc + xp_ref[o:o + tm, :] * cw[j:j + 1]
        tail_ref[0, :, cs] = xp_ref[tm:tm + padc, :]
        acc_ref[...] += _dot((_silu(uc) * gate).astype(BF16), wd_ref[cs, :])
    y_ref[...] = _rms(acc_ref[...], fg_ref[...])


def _ffn(h, xo, cin, wxo, g, wu, wv, cw, wd, fg, tm, stride, fc=D_FF // 2):
    m = h.shape[0]
    nseq, padc, _ = cin.shape
    nt = m // nseq // tm
    row = lambda w: pl.BlockSpec((tm, w), lambda s, j: (s * nt + j, 0))
    car = pl.BlockSpec((1, padc, D_FF), lambda s, j: (s, 0, 0))
    return pl.pallas_call(
        functools.partial(_ffn_kernel, stride=stride, padc=padc, fc=fc),
        grid=(nseq, nt),
        in_specs=[row(D_MODEL), row(X_WIDTH), car, _resident(wxo.shape), _resident((1, D_MODEL)),
                  _resident(wu.shape), _resident(wv.shape), _resident(cw.shape), _resident(wd.shape),
                  _resident((1, D_MODEL))],
        out_specs=[row(D_MODEL), car],
        out_shape=[jax.ShapeDtypeStruct((m, D_MODEL), F32), jax.ShapeDtypeStruct(cin.shape, F32)],
        scratch_shapes=[pltpu.VMEM((tm + padc, fc), F32), pltpu.VMEM((tm, D_MODEL), F32)],
        compiler_params=_cparams("parallel", "arbitrary"),
        name="ffn",
    )(h, xo, cin, wxo, g, wu, wv, cw, wd, fg)


def _lane_row(vec, offset=0):
    return jnp.zeros((1, LANES), F32).at[0, offset:offset + vec.shape[0]].set(vec.astype(F32))


def _prep(p):
    w_in = p["w_in"]
    o = 0
    cuts = {}
    for name, wdt in (("qkv", A_WIDTH + 2 * A_KV_WIDTH), ("d", 4 * DN_WIDTH), ("ab", 2 * DN_HEADS),
                      ("g", 2 * D_MODEL)):
        cuts[name] = w_in[:, o:o + wdt]
        o += wdt
    w = {
        "wqkv": cuts["qkv"].astype(BF16),
        "wd": cuts["d"].astype(BF16),
        "wab": jnp.pad(cuts["ab"], ((0, 0), (0, LANES - 2 * DN_HEADS))).astype(BF16),
        "wg": cuts["g"].astype(BF16),
        "norm_mix_g": p["norm_mix_g"].reshape(1, D_MODEL),
        "dn_conv_w": p["dn_conv_w"],
        "alog": _lane_row(p["dn_a_log"]),
        "dtb": _lane_row(p["dn_dt_bias"]),
        "dn_norm_g": p["dn_norm_g"].reshape(1, DN_HEAD_DIM),
        "sinks": p["sinks"].astype(F32),
        "w_br_a": p["w_br_a"].astype(BF16),
        "w_br_b": p["w_br_b"].astype(BF16),
        "w_mix_out": p["w_mix_out"].astype(BF16),
        "norm_x_g": p["norm_x_g"].reshape(1, D_MODEL),
        "w_xq": p["w_xq"].astype(BF16),
        "w_xo": p["w_xo"].astype(BF16),
        "norm_ffn_g": p["norm_ffn_g"].reshape(1, D_MODEL),
        "w_up": p["w_up"][:, :D_FF].astype(BF16),
        "w_gate": p["w_up"][:, D_FF:].astype(BF16),
        "ffn_conv_w": p["ffn_conv_w"],
        "w_down": p["w_down"].astype(BF16),
        "final_norm_g": p["final_norm_g"].reshape(1, D_MODEL),
    }
    return w


def _tile(m, pref):
    return pref if m % pref == 0 else m


def _prompt_layer(x, mem, norm_mem_g, w_xkv, w):
    bsz, seqlen, _ = x.shape
    m = bsz * seqlen
    x2 = x.reshape(m, D_MODEL)
    tm = _tile(seqlen, 512)
    tabs = _rope_tables(jnp.arange(seqlen, dtype=jnp.int32))
    q, k, v, dqkv, dz, ab, ga, gb = _proj(x2, w["norm_mix_g"], tabs, w["wqkv"], w["wd"], w["wab"], w["wg"], tm)
    ya = _swa_prompt(q, k, v, w["sinks"], bsz, seqlen)
    yb, dn_state = _gdn_prompt(dqkv, dz, ab, w["dn_conv_w"], w["alog"], w["dtb"], w["dn_norm_g"], bsz, seqlen)
    h, hq = _merge(x2, ya, yb, ga, gb, w["w_br_a"], w["w_br_b"], w["w_mix_out"], w["norm_x_g"], w["w_xq"], tm)
    nm = mem.shape[1]
    mk, mv = _memkv(mem.reshape(bsz * nm, D_MODEL), norm_mem_g.reshape(1, D_MODEL), w_xkv.astype(BF16),
                    _tile(bsz * nm, 512))
    xo = _xattn_prompt(hq, mk, mv, bsz, seqlen, tm)
    cin = jnp.zeros((bsz, SUBLANES, D_FF), F32)
    y, tail = _ffn(h, xo, cin, w["w_xo"], w["norm_ffn_g"], w["w_up"], w["w_gate"], w["ffn_conv_w"],
                   w["w_down"], w["final_norm_g"], tm, 1)
    wb = min(WINDOW, seqlen)
    new = (
        k.reshape(bsz, seqlen, A_KV_HEADS, A_HEAD_DIM)[:, seqlen - wb:],
        v.reshape(bsz, seqlen, A_KV_HEADS, A_HEAD_DIM)[:, seqlen - wb:],
        dqkv.reshape(bsz, seqlen, 3 * DN_WIDTH)[:, seqlen - (DN_CONV - 1):],
        dn_state,
        mk.reshape(bsz, nm, X_HEADS, X_HEAD_DIM),
        mv.reshape(bsz, nm, X_HEADS, X_HEAD_DIM),
        tail[:, SUBLANES - (FFN_CONV - 1):],
    )
    return y.reshape(bsz, seqlen, D_MODEL), new


def _sample_layer(x, pos0, win_k, win_v, dn_buf, dn_state, mem_k, mem_v, ffn_buf, w):
    bsz, t, _ = x.shape
    m = bsz * t
    sr = SAMPLE_ROWS
    x2 = x.reshape(m, D_MODEL)
    tabs = _rope_tables(jnp.tile(pos0 + jnp.arange(t, dtype=jnp.int32), bsz))
    q, k, v, dqkv, dz, ab, ga, gb = _proj(x2, w["norm_mix_g"], tabs, w["wqkv"], w["wd"], w["wab"], w["wg"], m)

    wlen = win_k.shape[1]
    q5 = q.reshape(bsz, t, A_KV_HEADS, A_HEADS // A_KV_HEADS, A_HEAD_DIM)
    q5 = jnp.transpose(q5, (0, 2, 3, 1, 4))
    eye = jnp.eye(A_KV_HEADS, dtype=BF16)
    q8 = (q5[:, :, :, :, None, :] * eye[None, :, None, None, :, None]).reshape(bsz, A_HEADS * t, A_KV_WIDTH)
    padrows = lambda a: jnp.pad(a.reshape(bsz, t, -1), ((0, 0), (0, sr - t), (0, 0)))
    o8 = _swa_sample(q8, win_k.reshape(bsz, wlen, A_KV_WIDTH), win_v.reshape(bsz, wlen, A_KV_WIDTH),
                     padrows(k).astype(BF16), padrows(v).astype(BF16), w["sinks"], t)
    o8 = o8.reshape(bsz, A_KV_HEADS, A_HEADS // A_KV_HEADS, t, A_KV_HEADS, A_HEAD_DIM)
    o8 = jnp.stack([o8[:, g, :, :, g] for g in range(A_KV_HEADS)], axis=1)
    ya = jnp.transpose(o8, (0, 3, 1, 2, 4)).reshape(m, A_WIDTH).astype(BF16)
    new_wk = jnp.concatenate([win_k, k.reshape(bsz, t, A_KV_HEADS, A_HEAD_DIM)], axis=1)[:, -wlen:]
    new_wv = jnp.concatenate([win_v, v.reshape(bsz, t, A_KV_HEADS, A_HEAD_DIM)], axis=1)[:, -wlen:]

    nb = CHUNK // sr
    hist = jnp.concatenate([dn_buf, dqkv.reshape(bsz, t, 3 * DN_WIDTH)], axis=1)
    rec = jnp.pad(hist, ((0, 0), (0, sr - hist.shape[1]), (0, 0))).reshape(bsz // nb, CHUNK, 3 * DN_WIDTH)
    zg = jnp.pad(rec, ((0, 0), (SUBLANES - (DN_CONV - 1), DN_CONV - 1), (0, 0)))
    yb16, new_s = _gdn_sample(zg, padrows(dz).reshape(bsz * sr, DN_WIDTH), padrows(ab).reshape(bsz * sr, LANES),
                              dn_state, w["dn_conv_w"], w["alog"], w["dtb"], w["dn_norm_g"], t)
    yb = yb16.reshape(bsz, sr, DN_WIDTH)[:, :t].reshape(m, DN_WIDTH).astype(BF16)
    new_dn_buf = hist[:, -(DN_CONV - 1):]

    h, hq = _merge(x2, ya, yb, ga, gb, w["w_br_a"], w["w_br_b"], w["w_mix_out"], w["norm_x_g"], w["w_xq"], m)
    nm = mem_k.shape[1]
    xo = _xattn_sample(padrows(hq), mem_k.reshape(bsz, nm, X_WIDTH), mem_v.reshape(bsz, nm, X_WIDTH))
    xo = xo[:, :t]

    tmaj = lambda a: jnp.transpose(a.reshape(bsz, t, -1), (1, 0, 2)).reshape(m, -1)
    cin = jnp.transpose(ffn_buf, (1, 0, 2)).reshape(1, (FFN_CONV - 1) * bsz, D_FF)
    y, tail = _ffn(tmaj(h), tmaj(xo), cin, w["w_xo"], w["norm_ffn_g"], w["w_up"], w["w_gate"], w["ffn_conv_w"],
                   w["w_down"], w["final_norm_g"], m, bsz)
    y = jnp.transpose(y.reshape(t, bsz, D_MODEL), (1, 0, 2))
    new_ffn = jnp.transpose(tail.reshape(FFN_CONV - 1, bsz, D_FF), (1, 0, 2))
    return y, (new_wk, new_wv, new_dn_buf, new_s, new_ffn)


def kernel(x_prompt, x_sample, mem_prompt, cache_win_k, cache_win_v, state_dn_conv, state_dn, cache_mem_k, cache_mem_v, state_ffn_conv, norm_mix_g, w_in, dn_conv_w, dn_a_log, dn_dt_bias, dn_norm_g, attn_sinks, w_br_a, w_br_b, w_mix_out, norm_x_g, norm_mem_g, w_xq, w_xkv, w_xo, norm_ffn_g, w_up, ffn_conv_w, w_down, final_norm_g):
    p = {"norm_mix_g": norm_mix_g[0], "w_in": w_in[0], "dn_conv_w": dn_conv_w[0], "dn_a_log": dn_a_log[0],
         "dn_dt_bias": dn_dt_bias[0], "dn_norm_g": dn_norm_g[0], "sinks": attn_sinks[0], "w_br_a": w_br_a[0],
         "w_br_b": w_br_b[0], "w_mix_out": w_mix_out[0], "norm_x_g": norm_x_g[0], "w_xq": w_xq[0],
         "w_xo": w_xo[0], "norm_ffn_g": norm_ffn_g[0], "w_up": w_up[0], "ffn_conv_w": ffn_conv_w[0],
         "w_down": w_down[0], "final_norm_g": final_norm_g}
    w = _prep(p)
    yp, newp = _prompt_layer(x_prompt, mem_prompt, norm_mem_g[0], w_xkv[0], w)
    ys, news = _sample_layer(x_sample, PAST_LEN, cache_win_k[0], cache_win_v[0], state_dn_conv[0], state_dn[0],
                             cache_mem_k[0], cache_mem_v[0], state_ffn_conv[0], w)
    lead = lambda a: a[None]
    p_win_k, p_win_v, p_dn_conv, p_dn_state, p_mem_k, p_mem_v, p_ffn_conv = [lead(a) for a in newp]
    s_win_k, s_win_v, s_dn_conv, s_dn_state, s_ffn_conv = [lead(a) for a in news]
    return (yp, ys, p_win_k, p_win_v, p_dn_conv, p_dn_state, p_mem_k, p_mem_v, p_ffn_conv,
            s_win_k, s_win_v, s_dn_conv, s_dn_state, s_ffn_conv)
```

```python
import functools
import math

import jax
import jax.numpy as jnp
from jax import lax
from jax.experimental import pallas as pl
from jax.experimental.pallas import tpu as pltpu

F32 = jnp.float32
BF16 = jnp.bfloat16

D_MODEL = 1024
A_HEADS = 8
A_KV_HEADS = 2
A_HEAD_DIM = 64
A_WIDTH = 512
A_KV_WIDTH = 128
WINDOW = 128
ROT_DIM = 16
ROPE_THETA = 500000.0
DN_HEADS = 4
DN_HEAD_DIM = 128
DN_WIDTH = 512
DN_CONV = 4
X_HEADS = 4
X_HEAD_DIM = 128
X_WIDTH = 512
D_FF = 2816
FFN_CONV = 3
EPS = 1e-6
PAST_LEN = 16384

LANES = 128
SUBLANES = 8
VMEM_LIMIT = 56 * 1024 * 1024
CHUNK = 128
SAMPLE_ROWS = 16


def _cparams(*sem):
    return pltpu.CompilerParams(dimension_semantics=sem, vmem_limit_bytes=VMEM_LIMIT)


def _resident(shape):
    return pl.BlockSpec(shape, lambda *_: (0,) * len(shape), pipeline_mode=pl.Buffered(1))


def _rms(x, g):
    return x * lax.rsqrt(jnp.mean(x * x, axis=-1, keepdims=True) + EPS) * g


def _dot(a, b):
    return jnp.dot(a, b, preferred_element_type=F32)


def _dot_nt(a, b):
    return lax.dot_general(a, b, (((1,), (1,)), ((), ())), preferred_element_type=F32)


def _silu(x):
    return x * jax.nn.sigmoid(x)


def _rope(seg, c, s1, s2):
    return seg * c + pltpu.roll(seg, LANES - 8, 1) * s1 + pltpu.roll(seg, 8, 1) * s2


def _proj_kernel(x_ref, g_ref, c_ref, s1_ref, s2_ref, wqkv_ref, wd_ref, wab_ref, wg_ref,
                 q_ref, k_ref, v_ref, dqkv_ref, dz_ref, ab_ref, ga_ref, gb_ref):
    xb = _rms(x_ref[...], g_ref[...]).astype(BF16)
    c, s1, s2 = c_ref[...], s1_ref[...], s2_ref[...]
    z = _dot(xb, wqkv_ref[...])
    for i in range(A_WIDTH // LANES):
        sl = slice(i * LANES, (i + 1) * LANES)
        q_ref[:, sl] = _rope(z[:, sl], c, s1, s2).astype(BF16)
    k_ref[...] = _rope(z[:, A_WIDTH:A_WIDTH + LANES], c, s1, s2)
    v_ref[...] = z[:, A_WIDTH + LANES:]
    z = _dot(xb, wd_ref[...])
    dqkv_ref[...] = z[:, :3 * DN_WIDTH]
    dz_ref[...] = z[:, 3 * DN_WIDTH:]
    ab_ref[...] = _dot(xb, wab_ref[...])
    z = _dot(xb, wg_ref[...])
    ga_ref[...] = z[:, :D_MODEL]
    gb_ref[...] = z[:, D_MODEL:]


def _proj(x, g, tabs, wqkv, wd, wab, wg, tm):
    m = x.shape[0]
    nt = tabs[0].shape[0] // tm
    row = lambda w: pl.BlockSpec((tm, w), lambda i: (i, 0))
    tab = pl.BlockSpec((tm, LANES), lambda i: (i % nt, 0))
    widths = (A_WIDTH, LANES, LANES, 3 * DN_WIDTH, DN_WIDTH, LANES, D_MODEL, D_MODEL)
    dts = (BF16, F32, F32, F32, F32, F32, F32, F32)
    return pl.pallas_call(
        _proj_kernel,
        grid=(m // tm,),
        in_specs=[row(D_MODEL), _resident((1, D_MODEL)), tab, tab, tab,
                  _resident(wqkv.shape), _resident(wd.shape), _resident(wab.shape), _resident(wg.shape)],
        out_specs=[row(w) for w in widths],
        out_shape=[jax.ShapeDtypeStruct((m, w), d) for w, d in zip(widths, dts)],
        compiler_params=_cparams("parallel"),
        name="in_proj",
    )(x, g, *tabs, wqkv, wd, wab, wg)


def _rope_tables(pos):
    half = ROT_DIM // 2
    inv = ROPE_THETA ** (-2.0 * jnp.arange(half, dtype=F32) / ROT_DIM)
    ang = pos.astype(F32)[:, None] * inv[None, :]
    c, s = jnp.cos(ang), jnp.sin(ang)
    n = pos.shape[0]
    one = jnp.ones((n, A_HEAD_DIM - ROT_DIM), F32)
    z8 = jnp.zeros((n, half), F32)
    z48 = jnp.zeros((n, A_HEAD_DIM - ROT_DIM), F32)
    ct = jnp.concatenate([c, c, one], axis=1)
    s1 = jnp.concatenate([-s, z8, z48], axis=1)
    s2 = jnp.concatenate([z8, s, z48], axis=1)
    two = lambda t: jnp.concatenate([t, t], axis=1)
    return two(ct), two(s1), two(s2)


def _both_halves(t, lane_lo):
    r = pltpu.roll(t, A_HEAD_DIM, 1)
    return jnp.where(lane_lo, t, r), jnp.where(lane_lo, r, t)


def _sink_softmax(s, valid, sink):
    s = jnp.where(valid, s, -jnp.inf)
    m = jnp.maximum(jnp.max(s, axis=-1, keepdims=True), sink)
    p = jnp.exp(s - m)
    den = jnp.sum(p, axis=-1, keepdims=True) + jnp.exp(sink - m)
    return p / den


def _swa_prompt_kernel(sink_ref, q_ref, kp_ref, kc_ref, vp_ref, vc_ref, o_ref):
    i = pl.program_id(1)
    w = WINDOW
    k = jnp.concatenate([kp_ref[...], kc_ref[...]], axis=0)
    v = jnp.concatenate([vp_ref[...], vc_ref[...]], axis=0)
    lane_lo_k = lax.broadcasted_iota(jnp.int32, k.shape, 1) < A_HEAD_DIM
    kk = [t.astype(BF16) for t in _both_halves(k, lane_lo_k)]
    vv = [t.astype(BF16) for t in _both_halves(v, lane_lo_k)]
    lane_lo = lax.broadcasted_iota(jnp.int32, (w, LANES), 1) < A_HEAD_DIM
    row = lax.broadcasted_iota(jnp.int32, (4 * w, 2 * w), 0)
    col = lax.broadcasted_iota(jnp.int32, (4 * w, 2 * w), 1)
    d = (row & (w - 1)) + w - col
    valid = (d >= 0) & (d < w) & jnp.logical_or(col >= w, i > 0)
    hrow = lax.broadcasted_iota(jnp.int32, (4 * w, 1), 0) // w
    zero = jnp.zeros((), BF16)
    for g in range(A_KV_HEADS):
        parts = []
        for sgm in range(2):
            seg = q_ref[:, (2 * g + sgm) * LANES:(2 * g + sgm + 1) * LANES]
            parts += [jnp.where(lane_lo, seg, zero), jnp.where(lane_lo, zero, seg)]
        qs = jnp.concatenate(parts, axis=0)
        sink = jnp.zeros((4 * w, 1), F32)
        for j in range(4):
            sink = jnp.where(hrow == j, sink_ref[4 * g + j], sink)
        s = _dot_nt(qs, kk[g]) * (A_HEAD_DIM ** -0.5)
        p = _sink_softmax(s, valid, sink).astype(BF16)
        o = _dot(p, vv[g])
        for sgm in range(2):
            o_ref[:, (2 * g + sgm) * LANES:(2 * g + sgm + 1) * LANES] = jnp.where(
                lane_lo, o[(2 * sgm) * w:(2 * sgm + 1) * w], o[(2 * sgm + 1) * w:(2 * sgm + 2) * w]
            ).astype(BF16)


def _swa_prompt(q, k, v, sinks, nseq, seqlen):
    w = WINDOW
    nb = seqlen // w
    cur = lambda b, i: (b * nb + i, 0)
    prev = lambda b, i: (b * nb + jnp.maximum(i - 1, 0), 0)
    kv = lambda im: pl.BlockSpec((w, A_KV_WIDTH), im)
    return pl.pallas_call(
        _swa_prompt_kernel,
        grid=(nseq, nb),
        in_specs=[pl.BlockSpec(memory_space=pltpu.SMEM), pl.BlockSpec((w, A_WIDTH), cur),
                  kv(prev), kv(cur), kv(prev), kv(cur)],
        out_specs=pl.BlockSpec((w, A_WIDTH), cur),
        out_shape=jax.ShapeDtypeStruct(q.shape, BF16),
        compiler_params=_cparams("parallel", "parallel"),
        name="swa_prompt",
    )(sinks, q, k, k, v, v)


def _swa_sample_kernel(sink_ref, q_ref, ck_ref, cv_ref, kn_ref, vn_ref, o_ref, *, nb, t):
    w = ck_ref.shape[1]
    nq = q_ref.shape[1]
    rows = nq // A_HEADS
    r = lax.broadcasted_iota(jnp.int32, (nq, w), 0)
    c = lax.broadcasted_iota(jnp.int32, (nq, w), 1)
    tq = r % rows
    valid_c = c > tq
    rn = lax.broadcasted_iota(jnp.int32, (nq, SAMPLE_ROWS), 0) % rows
    cn = lax.broadcasted_iota(jnp.int32, (nq, SAMPLE_ROWS), 1)
    valid_n = (cn <= rn) & (cn < t)
    hrow = lax.broadcasted_iota(jnp.int32, (nq, 1), 0) // rows
    sink = jnp.zeros((nq, 1), F32)
    for h in range(A_HEADS):
        sink = jnp.where(hrow == h, sink_ref[h], sink)
    for b in range(nb):
        q = q_ref[b]
        ck = ck_ref[b].astype(BF16)
        cv = cv_ref[b].astype(BF16)
        kn = kn_ref[b]
        vn = vn_ref[b]
        sc = jnp.where(valid_c, _dot_nt(q, ck) * (A_HEAD_DIM ** -0.5), -jnp.inf)
        sn = jnp.where(valid_n, _dot_nt(q, kn) * (A_HEAD_DIM ** -0.5), -jnp.inf)
        m = jnp.maximum(jnp.maximum(jnp.max(sc, -1, keepdims=True), jnp.max(sn, -1, keepdims=True)), sink)
        pc = jnp.exp(sc - m)
        pn = jnp.exp(sn - m)
        den = jnp.sum(pc, -1, keepdims=True) + jnp.sum(pn, -1, keepdims=True) + jnp.exp(sink - m)
        o_ref[b] = _dot((pc / den).astype(BF16), cv) + _dot((pn / den).astype(BF16), vn)


def _swa_sample(q8, ck, cv, kn, vn, sinks, t, nb=8):
    bsz, nq, _ = q8.shape
    w = ck.shape[1]
    blk = lambda r: pl.BlockSpec((nb, r, LANES), lambda i: (i, 0, 0))
    return pl.pallas_call(
        functools.partial(_swa_sample_kernel, nb=nb, t=t),
        grid=(bsz // nb,),
        in_specs=[pl.BlockSpec(memory_space=pltpu.SMEM), blk(nq), blk(w), blk(w),
                  blk(SAMPLE_ROWS), blk(SAMPLE_ROWS)],
        out_specs=blk(nq),
        out_shape=jax.ShapeDtypeStruct((bsz, nq, LANES), F32),
        compiler_params=_cparams("parallel"),
        name="swa_sample",
    )(sinks, q8, ck, cv, kn, vn)


def _lane_bcast(x, lane):
    return jnp.broadcast_to(x[:, lane:lane + 1], (x.shape[0], LANES))


def _cumsum_rows(x, block):
    row = lax.broadcasted_iota(jnp.int32, x.shape, 0) % block
    sh = 1
    while sh < block:
        x = x + jnp.where(row >= sh, pltpu.roll(x, sh, 0), 0.0)
        sh *= 2
    return x


def _l2n(t):
    return t * lax.rsqrt(jnp.sum(t * t, axis=-1, keepdims=True) + EPS)


def _gates(ab, alog, dtb):
    x = ab + dtb
    sp = jnp.maximum(x, 0.0) + jnp.log1p(jnp.exp(-jnp.abs(x)))
    return -jnp.exp(alog) * sp, jax.nn.sigmoid(ab)


def _merge_masks(c, top):
    rowi = lax.broadcasted_iota(jnp.int32, (c, c), 0)
    coli = lax.broadcasted_iota(jnp.int32, (c, c), 1)
    masks = []
    s = 1
    while s < top:
        rb, cb = rowi // s, coli // s
        masks.append(((rb // 2) == (cb // 2)) & ((rb % 2) == 1) & ((cb % 2) == 0))
        s *= 2
    return masks


def _each(f, *lists):
    return [f(*t) for t in zip(*lists)]


def _chunk_local(qs, ks, vs, gcols, grows, betas, tril, merges):
    c = qs[0].shape[0]
    decays = _each(lambda gc, gr: jnp.exp(jnp.minimum(gc - gr, 0.0)), gcols, grows)
    kbs = _each(lambda k, b: k * b, ks, betas)
    ms = _each(lambda q, kb, k: _dot_nt(jnp.concatenate([q, kb], axis=0).astype(BF16), k.astype(BF16)),
               qs, kbs, ks)
    qks = _each(lambda m, d: jnp.where(tril, m[:c] * d, 0.0), ms, decays)
    a = _each(lambda m, d: m[c:] * d, ms, decays)
    ns = [-jnp.where(merges[0], x, 0.0) if merges else jnp.zeros_like(x) for x in a]
    for off in merges[1:]:
        aos = [jnp.where(off, x, 0.0) for x in a]
        zs = _each(lambda ao, n: ao + _dot(ao.astype(BF16), n.astype(BF16)), aos, ns)
        ns = _each(lambda n, z: n - z - _dot(n.astype(BF16), z.astype(BF16)), ns, zs)
    egs = [jnp.exp(gc) for gc in gcols]
    rhss = _each(lambda v, b, kb, eg: jnp.concatenate([v * b, kb * eg], axis=1), vs, betas, kbs, egs)
    uws = _each(lambda r, n: r + _dot(n.astype(BF16), r.astype(BF16)), rhss, ns)
    return ([x[:, :LANES] for x in uws], [x[:, LANES:] for x in uws], qks,
            _each(lambda q, eg: q * eg, qs, egs))


def _gdn_prompt_kernel(z_ref, dz_ref, ab_ref, cw_ref, alog_ref, dtb_ref, ng_ref,
                       y_ref, sout_ref, xp_ref, s_ref, *, nc):
    c = CHUNK
    r = nc * c
    pad = SUBLANES

    @pl.when(pl.program_id(1) == 0)
    def _():
        xp_ref[0:pad, :] = jnp.zeros((pad, 3 * DN_WIDTH), F32)
        s_ref[...] = jnp.zeros_like(s_ref)

    xp_ref[pad:pad + r, :] = z_ref[...]
    cw = cw_ref[...]
    conv = xp_ref[pad:pad + r, :] * cw[3:4]
    for j in range(DN_CONV - 1):
        conv = conv + xp_ref[pad - 3 + j:pad - 3 + j + r, :] * cw[j:j + 1]
    xp_ref[0:pad, :] = xp_ref[r:r + pad, :]
    conv = _silu(conv)
    g, beta = _gates(ab_ref[...], alog_ref[...], dtb_ref[...])
    rowi = lax.broadcasted_iota(jnp.int32, (c, c), 0)
    coli = lax.broadcasted_iota(jnp.int32, (c, c), 1)
    tril = rowi >= coli
    merges = _merge_masks(c, c)
    ng = ng_ref[...]
    heads = range(DN_HEADS)
    probs = [(ci, h) for ci in range(nc) for h in heads]
    rows = lambda ci: slice(ci * c, (ci + 1) * c)
    lanes = lambda part, h: slice(part * DN_WIDTH + h * LANES, part * DN_WIDTH + (h + 1) * LANES)
    gcs = [_cumsum_rows(g[rows(ci)], c) for ci in range(nc)]
    gcts = [gc.T for gc in gcs]
    qs = [_l2n(conv[rows(ci), lanes(0, h)]) * (DN_HEAD_DIM ** -0.5) for ci, h in probs]
    ks = [_l2n(conv[rows(ci), lanes(1, h)]) for ci, h in probs]
    vs = [conv[rows(ci), lanes(2, h)] for ci, h in probs]
    gcols = [_lane_bcast(gcs[ci], h) for ci, h in probs]
    grows = [gcts[ci][h:h + 1, :] for ci, h in probs]
    betas = [_lane_bcast(beta[rows(ci)], DN_HEADS + h) for ci, h in probs]
    us, ws, qks, qds = _chunk_local(qs, ks, vs, gcols, grows, betas, tril, merges)
    glasts = [gc[c - 1:c, :] for gc in gcols]
    kdts = _each(lambda k, gl, gc: (k * jnp.exp(gl - gc)).T, ks, glasts, gcols)
    wqs = _each(lambda w, qd: jnp.concatenate([w, qd], axis=0).astype(BF16), ws, qds)
    qkks = _each(lambda qk, kdt: jnp.concatenate([qk, kdt], axis=0).astype(BF16), qks, kdts)
    ss = [s_ref[h] for h in heads]
    for ci in range(nc):
        pr = [ci * DN_HEADS + h for h in heads]
        r2s = [_dot(wqs[p], s.astype(BF16)) for p, s in zip(pr, ss)]
        vnews = [us[p] - r2[:c] for p, r2 in zip(pr, r2s)]
        r3s = [_dot(qkks[p], vn.astype(BF16)) for p, vn in zip(pr, vnews)]
        ss = [s * jnp.exp(glasts[p]) + r3[c:] for p, s, r3 in zip(pr, ss, r3s)]
        for h in heads:
            o = r2s[h][c:] + r3s[h][:c]
            y_ref[rows(ci), lanes(0, h)] = (_rms(o, ng) * _silu(dz_ref[rows(ci), lanes(0, h)])).astype(BF16)
    for h in heads:
        s_ref[h] = ss[h]

    @pl.when(pl.program_id(1) == pl.num_programs(1) - 1)
    def _():
        sout_ref[0] = s_ref[...]


def _gdn_prompt(dqkv, dz, ab, cw, alog, dtb, ng, nseq, seqlen, nc=2):
    r = nc * CHUNK
    ns = seqlen // r
    row = lambda w: pl.BlockSpec((r, w), lambda b, s: (b * ns + s, 0))
    return pl.pallas_call(
        functools.partial(_gdn_prompt_kernel, nc=nc),
        grid=(nseq, ns),
        in_specs=[row(3 * DN_WIDTH), row(DN_WIDTH), row(LANES), _resident(cw.shape),
                  _resident((1, LANES)), _resident((1, LANES)), _resident((1, LANES))],
        out_specs=[row(DN_WIDTH),
                   pl.BlockSpec((1, DN_HEADS, DN_HEAD_DIM, DN_HEAD_DIM), lambda b, s: (b, 0, 0, 0))],
        out_shape=[jax.ShapeDtypeStruct((nseq * seqlen, DN_WIDTH), BF16),
                   jax.ShapeDtypeStruct((nseq, DN_HEADS, DN_HEAD_DIM, DN_HEAD_DIM), F32)],
        scratch_shapes=[pltpu.VMEM((r + SUBLANES, 3 * DN_WIDTH), F32),
                        pltpu.VMEM((DN_HEADS, DN_HEAD_DIM, DN_HEAD_DIM), F32)],
        compiler_params=_cparams("parallel", "arbitrary"),
        name="gdn_prompt",
    )(dqkv, dz, ab, cw, alog, dtb, ng)


def _gdn_sample_kernel(z_ref, dz_ref, ab_ref, s0_ref, cw_ref, alog_ref, dtb_ref, ng_ref,
                       y_ref, sout_ref, *, t):
    c = CHUNK
    sr = SAMPLE_ROWS
    nb = c // sr
    off = SUBLANES - (DN_CONV - 1)
    cw = cw_ref[...]
    conv = z_ref[0, off + 3:off + 3 + c, :] * cw[3:4]
    for j in range(DN_CONV - 1):
        conv = conv + z_ref[0, off + j:off + j + c, :] * cw[j:j + 1]
    conv = _silu(conv)
    live = (lax.broadcasted_iota(jnp.int32, (c, LANES), 0) % sr) < t
    g, beta = _gates(ab_ref[...], alog_ref[...], dtb_ref[...])
    g = jnp.where(live, g, 0.0)
    beta = jnp.where(live, beta, 0.0)
    gc = _cumsum_rows(g, sr)
    gct = gc.T
    rowi = lax.broadcasted_iota(jnp.int32, (c, c), 0)
    coli = lax.broadcasted_iota(jnp.int32, (c, c), 1)
    tril = ((rowi // sr) == (coli // sr)) & (rowi >= coli)
    merges = _merge_masks(c, pl.next_power_of_2(t))
    ng = ng_ref[...]
    rowb = lax.broadcasted_iota(jnp.int32, (c, LANES), 0) // sr
    heads = range(DN_HEADS)
    lanes = lambda part, h: slice(part * DN_WIDTH + h * LANES, part * DN_WIDTH + (h + 1) * LANES)
    ks = [_l2n(conv[:, lanes(1, h)]) for h in heads]
    gcols = [_lane_bcast(gc, h) for h in heads]
    us, ws, qks, qds = _chunk_local(
        [_l2n(conv[:, lanes(0, h)]) * (DN_HEAD_DIM ** -0.5) for h in heads], ks,
        [conv[:, lanes(2, h)] for h in heads], gcols, [gct[h:h + 1, :] for h in heads],
        [_lane_bcast(beta, DN_HEADS + h) for h in heads], tril, merges)
    for h in heads:
        hs = lanes(0, h)
        k, gcol, u, w, qk, qd = ks[h], gcols[h], us[h], ws[h], qks[h], qds[h]
        wq = jnp.concatenate([w, qd], axis=1).astype(BF16)
        vnews, qss, sbs = [], [], []
        for b in range(nb):
            rs = slice(b * sr, (b + 1) * sr)
            s = s0_ref[b, h]
            sbs.append(s)
            r2 = _dot(jnp.concatenate([wq[rs, :LANES], wq[rs, LANES:]], axis=0), s.astype(BF16))
            vnews.append(u[rs] - r2[:sr])
            qss.append(r2[sr:])
        vnew = jnp.concatenate(vnews, axis=0)
        o = jnp.concatenate(qss, axis=0) + _dot(qk.astype(BF16), vnew.astype(BF16))
        y_ref[:, hs] = _rms(o, ng) * _silu(dz_ref[:, hs])
        glast = jnp.concatenate(
            [jnp.broadcast_to(gcol[b * sr + sr - 1:b * sr + sr, :], (sr, LANES)) for b in range(nb)], axis=0)
        kdt = (k * jnp.exp(glast - gcol)).T.astype(BF16)
        for b in range(nb):
            vb = jnp.where(rowb == b, vnew, 0.0).astype(BF16)
            sout_ref[b, h] = sbs[b] * jnp.exp(glast[b * sr:b * sr + 1, :]) + _dot(kdt, vb)


def _gdn_sample(zg, dz, ab, s0, cw, alog, dtb, ng, t):
    ngrp = zg.shape[0]
    nb = CHUNK // SAMPLE_ROWS
    row = lambda w: pl.BlockSpec((CHUNK, w), lambda i: (i, 0))
    st = pl.BlockSpec((nb, DN_HEADS, DN_HEAD_DIM, DN_HEAD_DIM), lambda i: (i, 0, 0, 0))
    return pl.pallas_call(
        functools.partial(_gdn_sample_kernel, t=t),
        grid=(ngrp,),
        in_specs=[pl.BlockSpec((1, CHUNK + SUBLANES, 3 * DN_WIDTH), lambda i: (i, 0, 0)),
                  row(DN_WIDTH), row(LANES), st, _resident(cw.shape),
                  _resident((1, LANES)), _resident((1, LANES)), _resident((1, LANES))],
        out_specs=[row(DN_WIDTH), st],
        out_shape=[jax.ShapeDtypeStruct((ngrp * CHUNK, DN_WIDTH), F32),
                   jax.ShapeDtypeStruct(s0.shape, F32)],
        compiler_params=_cparams("parallel"),
        name="gdn_sample",
    )(zg, dz, ab, s0, cw, alog, dtb, ng)


def _merge_kernel(x_ref, ya_ref, yb_ref, ga_ref, gb_ref, wa_ref, wb_ref, wo_ref, g_ref, wq_ref,
                  h_ref, hq_ref):
    mix = (jax.nn.sigmoid(ga_ref[...]) * _dot(ya_ref[...], wa_ref[...])
           + jax.nn.sigmoid(gb_ref[...]) * _dot(yb_ref[...], wb_ref[...]))
    h = x_ref[...] + _dot(mix.astype(BF16), wo_ref[...])
    h_ref[...] = h
    hq_ref[...] = _dot(_rms(h, g_ref[...]).astype(BF16), wq_ref[...]).astype(BF16)


def _merge(x, ya, yb, ga, gb, wa, wb, wo, g, wq, tm):
    m = x.shape[0]
    row = lambda w: pl.BlockSpec((tm, w), lambda i: (i, 0))
    return pl.pallas_call(
        _merge_kernel,
        grid=(m // tm,),
        in_specs=[row(D_MODEL), row(A_WIDTH), row(DN_WIDTH), row(D_MODEL), row(D_MODEL),
                  _resident(wa.shape), _resident(wb.shape), _resident(wo.shape),
                  _resident((1, D_MODEL)), _resident(wq.shape)],
        out_specs=[row(D_MODEL), row(X_WIDTH)],
        out_shape=[jax.ShapeDtypeStruct((m, D_MODEL), F32), jax.ShapeDtypeStruct((m, X_WIDTH), BF16)],
        compiler_params=_cparams("parallel"),
        name="merge",
    )(x, ya, yb, ga, gb, wa, wb, wo, g, wq)


def _memkv_kernel(m_ref, g_ref, w_ref, k_ref, v_ref):
    z = _dot(_rms(m_ref[...], g_ref[...]).astype(BF16), w_ref[...])
    k_ref[...] = z[:, :X_WIDTH]
    v_ref[...] = z[:, X_WIDTH:]


def _memkv(mem, g, w, tm):
    m = mem.shape[0]
    row = lambda wd: pl.BlockSpec((tm, wd), lambda i: (i, 0))
    return pl.pallas_call(
        _memkv_kernel,
        grid=(m // tm,),
        in_specs=[row(D_MODEL), _resident((1, D_MODEL)), _resident(w.shape)],
        out_specs=[row(X_WIDTH), row(X_WIDTH)],
        out_shape=[jax.ShapeDtypeStruct((m, X_WIDTH), F32)] * 2,
        compiler_params=_cparams("parallel"),
        name="mem_kv",
    )(mem, g, w)


def _xattn_heads(hq, mk, mv):
    outs = []
    for h in range(X_HEADS):
        s = _dot_nt(hq[:, h * LANES:(h + 1) * LANES], mk(h).astype(BF16)) * (X_HEAD_DIM ** -0.5)
        e = jnp.exp(s - jnp.max(s, axis=-1, keepdims=True))
        p = e / jnp.sum(e, axis=-1, keepdims=True)
        outs.append(_dot(p.astype(BF16), mv(h).astype(BF16)))
    return jnp.concatenate(outs, axis=1)


def _xattn_prompt_kernel(hq_ref, mk_ref, mv_ref, o_ref):
    head = lambda ref: lambda h: ref[:, h * LANES:(h + 1) * LANES]
    o_ref[...] = _xattn_heads(hq_ref[...], head(mk_ref), head(mv_ref)).astype(BF16)


def _xattn_prompt(hq, mk, mv, nseq, seqlen, tq):
    nt = seqlen // tq
    nm = mk.shape[0] // nseq
    row = pl.BlockSpec((tq, X_WIDTH), lambda b, i: (b * nt + i, 0))
    mem = pl.BlockSpec((nm, X_WIDTH), lambda b, i: (b, 0))
    return pl.pallas_call(
        _xattn_prompt_kernel,
        grid=(nseq, nt),
        in_specs=[row, mem, mem],
        out_specs=row,
        out_shape=jax.ShapeDtypeStruct(hq.shape, BF16),
        compiler_params=_cparams("parallel", "parallel"),
        name="xattn_prompt",
    )(hq, mk, mv)


def _xattn_sample_kernel(hq_ref, mk_ref, mv_ref, o_ref, *, nb):
    for b in range(nb):
        head = lambda ref: lambda h: ref[b, :, h, :]
        o_ref[b] = _xattn_heads(hq_ref[b], head(mk_ref), head(mv_ref)).astype(BF16)


def _xattn_sample(hq, mk, mv, nb=4):
    bsz, rows, _ = hq.shape
    nm = mk.shape[1]
    q = pl.BlockSpec((nb, rows, X_WIDTH), lambda i: (i, 0, 0))
    mem = pl.BlockSpec((nb, nm, X_HEADS, X_HEAD_DIM), lambda i: (i, 0, 0, 0))
    return pl.pallas_call(
        functools.partial(_xattn_sample_kernel, nb=nb),
        grid=(bsz // nb,),
        in_specs=[q, mem, mem],
        out_specs=q,
        out_shape=jax.ShapeDtypeStruct(hq.shape, BF16),
        compiler_params=_cparams("parallel"),
        name="xattn_sample",
    )(hq, mk, mv)


def _ffn_kernel(h_ref, xo_ref, cin_ref, wxo_ref, g_ref, wu_ref, wv_ref, cw_ref, wd_ref, fg_ref,
                y_ref, tail_ref, xp_ref, acc_ref, *, stride, padc, fc):
    tm = h_ref.shape[0]

    @pl.when(pl.program_id(1) == 0)
    def _():
        tail_ref[...] = cin_ref[...]

    h = h_ref[...] + _dot(xo_ref[...], wxo_ref[...])
    acc_ref[...] = h
    hn = _rms(h, g_ref[...]).astype(BF16)
    for ci in range(D_FF // fc):
        cs = slice(ci * fc, (ci + 1) * fc)
        u = _dot(hn, wu_ref[:, cs])
        gate = _dot(hn, wv_ref[:, cs])
        xp_ref[0:padc, :] = tail_ref[0, :, cs]
        xp_ref[padc:padc + tm, :] = u
        cw = cw_ref[:, cs]
        uc = u * cw[2:3]
        for j in range(FFN_CONV - 1):
            o = padc - (FFN_CONV - 1 - j) * stride
            uc = uc + xp_ref[o:o + tm, :] * cw[j:j + 1]
        tail_ref[0, :, cs] = xp_ref[tm:tm + padc, :]
        acc_ref[...] += _dot((_silu(uc) * gate).astype(BF16), wd_ref[cs, :])
    y_ref[...] = _rms(acc_ref[...], fg_ref[...])


def _ffn(h, xo, cin, wxo, g, wu, wv, cw, wd, fg, tm, stride, fc=D_FF // 2):
    m = h.shape[0]
    nseq, padc, _ = cin.shape
    nt = m // nseq // tm
    row = lambda w: pl.BlockSpec((tm, w), lambda s, j: (s * nt + j, 0))
    car = pl.BlockSpec((1, padc, D_FF), lambda s, j: (s, 0, 0))
    return pl.pallas_call(
        functools.partial(_ffn_kernel, stride=stride, padc=padc, fc=fc),
        grid=(nseq, nt),
        in_specs=[row(D_MODEL), row(X_WIDTH), car, _resident(wxo.shape), _resident((1, D_MODEL)),
                  _resident(wu.shape), _resident(wv.shape), _resident(cw.shape), _resident(wd.shape),
                  _resident((1, D_MODEL))],
        out_specs=[row(D_MODEL), car],
        out_shape=[jax.ShapeDtypeStruct((m, D_MODEL), F32), jax.ShapeDtypeStruct(cin.shape, F32)],
        scratch_shapes=[pltpu.VMEM((tm + padc, fc), F32), pltpu.VMEM((tm, D_MODEL), F32)],
        compiler_params=_cparams("parallel", "arbitrary"),
        name="ffn",
    )(h, xo, cin, wxo, g, wu, wv, cw, wd, fg)


def _lane_row(vec, offset=0):
    return jnp.zeros((1, LANES), F32).at[0, offset:offset + vec.shape[0]].set(vec.astype(F32))


def _prep(p):
    w_in = p["w_in"]
    o = 0
    cuts = {}
    for name, wdt in (("qkv", A_WIDTH + 2 * A_KV_WIDTH), ("d", 4 * DN_WIDTH), ("ab", 2 * DN_HEADS),
                      ("g", 2 * D_MODEL)):
        cuts[name] = w_in[:, o:o + wdt]
        o += wdt
    w = {
        "wqkv": cuts["qkv"].astype(BF16),
        "wd": cuts["d"].astype(BF16),
        "wab": jnp.pad(cuts["ab"], ((0, 0), (0, LANES - 2 * DN_HEADS))).astype(BF16),
        "wg": cuts["g"].astype(BF16),
        "norm_mix_g": p["norm_mix_g"].reshape(1, D_MODEL),
        "dn_conv_w": p["dn_conv_w"],
        "alog": _lane_row(p["dn_a_log"]),
        "dtb": _lane_row(p["dn_dt_bias"]),
        "dn_norm_g": p["dn_norm_g"].reshape(1, DN_HEAD_DIM),
        "sinks": p["sinks"].astype(F32),
        "w_br_a": p["w_br_a"].astype(BF16),
        "w_br_b": p["w_br_b"].astype(BF16),
        "w_mix_out": p["w_mix_out"].astype(BF16),
        "norm_x_g": p["norm_x_g"].reshape(1, D_MODEL),
        "w_xq": p["w_xq"].astype(BF16),
        "w_xo": p["w_xo"].astype(BF16),
        "norm_ffn_g": p["norm_ffn_g"].reshape(1, D_MODEL),
        "w_up": p["w_up"][:, :D_FF].astype(BF16),
        "w_gate": p["w_up"][:, D_FF:].astype(BF16),
        "ffn_conv_w": p["ffn_conv_w"],
        "w_down": p["w_down"].astype(BF16),
        "final_norm_g": p["final_norm_g"].reshape(1, D_MODEL),
    }
    return w


def _tile(m, pref):
    return pref if m % pref == 0 else m


def _prompt_layer(x, mem, norm_mem_g, w_xkv, w):
    bsz, seqlen, _ = x.shape
    m = bsz * seqlen
    x2 = x.reshape(m, D_MODEL)
    tm = _tile(seqlen, 512)
    tabs = _rope_tables(jnp.arange(seqlen, dtype=jnp.int32))
    q, k, v, dqkv, dz, ab, ga, gb = _proj(x2, w["norm_mix_g"], tabs, w["wqkv"], w["wd"], w["wab"], w["wg"], tm)
    ya = _swa_prompt(q, k, v, w["sinks"], bsz, seqlen)
    yb, dn_state = _gdn_prompt(dqkv, dz, ab, w["dn_conv_w"], w["alog"], w["dtb"], w["dn_norm_g"], bsz, seqlen)
    h, hq = _merge(x2, ya, yb, ga, gb, w["w_br_a"], w["w_br_b"], w["w_mix_out"], w["norm_x_g"], w["w_xq"], tm)
    nm = mem.shape[1]
    mk, mv = _memkv(mem.reshape(bsz * nm, D_MODEL), norm_mem_g.reshape(1, D_MODEL), w_xkv.astype(BF16),
                    _tile(bsz * nm, 512))
    xo = _xattn_prompt(hq, mk, mv, bsz, seqlen, tm)
    cin = jnp.zeros((bsz, SUBLANES, D_FF), F32)
    y, tail = _ffn(h, xo, cin, w["w_xo"], w["norm_ffn_g"], w["w_up"], w["w_gate"], w["ffn_conv_w"],
                   w["w_down"], w["final_norm_g"], tm, 1)
    wb = min(WINDOW, seqlen)
    new = (
        k.reshape(bsz, seqlen, A_KV_HEADS, A_HEAD_DIM)[:, seqlen - wb:],
        v.reshape(bsz, seqlen, A_KV_HEADS, A_HEAD_DIM)[:, seqlen - wb:],
        dqkv.reshape(bsz, seqlen, 3 * DN_WIDTH)[:, seqlen - (DN_CONV - 1):],
        dn_state,
        mk.reshape(bsz, nm, X_HEADS, X_HEAD_DIM),
        mv.reshape(bsz, nm, X_HEADS, X_HEAD_DIM),
        tail[:, SUBLANES - (FFN_CONV - 1):],
    )
    return y.reshape(bsz, seqlen, D_MODEL), new


def _sample_layer(x, pos0, win_k, win_v, dn_buf, dn_state, mem_k, mem_v, ffn_buf, w):
    bsz, t, _ = x.shape
    m = bsz * t
    sr = SAMPLE_ROWS
    x2 = x.reshape(m, D_MODEL)
    tabs = _rope_tables(jnp.tile(pos0 + jnp.arange(t, dtype=jnp.int32), bsz))
    q, k, v, dqkv, dz, ab, ga, gb = _proj(x2, w["norm_mix_g"], tabs, w["wqkv"], w["wd"], w["wab"], w["wg"], m)

    wlen = win_k.shape[1]
    q5 = q.reshape(bsz, t, A_KV_HEADS, A_HEADS // A_KV_HEADS, A_HEAD_DIM)
    q5 = jnp.transpose(q5, (0, 2, 3, 1, 4))
    eye = jnp.eye(A_KV_HEADS, dtype=BF16)
    q8 = (q5[:, :, :, :, None, :] * eye[None, :, None, None, :, None]).reshape(bsz, A_HEADS * t, A_KV_WIDTH)
    padrows = lambda a: jnp.pad(a.reshape(bsz, t, -1), ((0, 0), (0, sr - t), (0, 0)))
    o8 = _swa_sample(q8, win_k.reshape(bsz, wlen, A_KV_WIDTH), win_v.reshape(bsz, wlen, A_KV_WIDTH),
                     padrows(k).astype(BF16), padrows(v).astype(BF16), w["sinks"], t)
    o8 = o8.reshape(bsz, A_KV_HEADS, A_HEADS // A_KV_HEADS, t, A_KV_HEADS, A_HEAD_DIM)
    o8 = jnp.stack([o8[:, g, :, :, g] for g in range(A_KV_HEADS)], axis=1)
    ya = jnp.transpose(o8, (0, 3, 1, 2, 4)).reshape(m, A_WIDTH).astype(BF16)
    new_wk = jnp.concatenate([win_k, k.reshape(bsz, t, A_KV_HEADS, A_HEAD_DIM)], axis=1)[:, -wlen:]
    new_wv = jnp.concatenate([win_v, v.reshape(bsz, t, A_KV_HEADS, A_HEAD_DIM)], axis=1)[:, -wlen:]

    nb = CHUNK // sr
    hist = jnp.concatenate([dn_buf, dqkv.reshape(bsz, t, 3 * DN_WIDTH)], axis=1)
    rec = jnp.pad(hist, ((0, 0), (0, sr - hist.shape[1]), (0, 0))).reshape(bsz // nb, CHUNK, 3 * DN_WIDTH)
    zg = jnp.pad(rec, ((0, 0), (SUBLANES - (DN_CONV - 1), DN_CONV - 1), (0, 0)))
    yb16, new_s = _gdn_sample(zg, padrows(dz).reshape(bsz * sr, DN_WIDTH), padrows(ab).reshape(bsz * sr, LANES),
                              dn_state, w["dn_conv_w"], w["alog"], w["dtb"], w["dn_norm_g"], t)
    yb = yb16.reshape(bsz, sr, DN_WIDTH)[:, :t].reshape(m, DN_WIDTH).astype(BF16)
    new_dn_buf = hist[:, -(DN_CONV - 1):]

    h, hq = _merge(x2, ya, yb, ga, gb, w["w_br_a"], w["w_br_b"], w["w_mix_out"], w["norm_x_g"], w["w_xq"], m)
    xo = _xattn_sample(padrows(hq), mem_k, mem_v)
    xo = xo[:, :t]

    tmaj = lambda a: jnp.transpose(a.reshape(bsz, t, -1), (1, 0, 2)).reshape(m, -1)
    cin = jnp.transpose(ffn_buf, (1, 0, 2)).reshape(1, (FFN_CONV - 1) * bsz, D_FF)
    y, tail = _ffn(tmaj(h), tmaj(xo), cin, w["w_xo"], w["norm_ffn_g"], w["w_up"], w["w_gate"], w["ffn_conv_w"],
                   w["w_down"], w["final_norm_g"], m, bsz)
    y = jnp.transpose(y.reshape(t, bsz, D_MODEL), (1, 0, 2))
    new_ffn = jnp.transpose(tail.reshape(FFN_CONV - 1, bsz, D_FF), (1, 0, 2))
    return y, (new_wk, new_wv, new_dn_buf, new_s, new_ffn)


def kernel(x_prompt, x_sample, mem_prompt, cache_win_k, cache_win_v, state_dn_conv, state_dn, cache_mem_k, cache_mem_v, state_ffn_conv, norm_mix_g, w_in, dn_conv_w, dn_a_log, dn_dt_bias, dn_norm_g, attn_sinks, w_br_a, w_br_b, w_mix_out, norm_x_g, norm_mem_g, w_xq, w_xkv, w_xo, norm_ffn_g, w_up, ffn_conv_w, w_down, final_norm_g):
    p = {"norm_mix_g": norm_mix_g[0], "w_in": w_in[0], "dn_conv_w": dn_conv_w[0], "dn_a_log": dn_a_log[0],
         "dn_dt_bias": dn_dt_bias[0], "dn_norm_g": dn_norm_g[0], "sinks": attn_sinks[0], "w_br_a": w_br_a[0],
         "w_br_b": w_br_b[0], "w_mix_out": w_mix_out[0], "norm_x_g": norm_x_g[0], "w_xq": w_xq[0],
         "w_xo": w_xo[0], "norm_ffn_g": norm_ffn_g[0], "w_up": w_up[0], "ffn_conv_w": ffn_conv_w[0],
         "w_down": w_down[0], "final_norm_g": final_norm_g}
    w = _prep(p)
    yp, newp = _prompt_layer(x_prompt, mem_prompt, norm_mem_g[0], w_xkv[0], w)
    ys, news = _sample_layer(x_sample, PAST_LEN, cache_win_k[0], cache_win_v[0], state_dn_conv[0], state_dn[0],
                             cache_mem_k[0], cache_mem_v[0], state_ffn_conv[0], w)
    lead = lambda a: a[None]
    p_win_k, p_win_v, p_dn_conv, p_dn_state, p_mem_k, p_mem_v, p_ffn_conv = [lead(a) for a in newp]
    s_win_k, s_win_v, s_dn_conv, s_dn_state, s_ffn_conv = [lead(a) for a in news]
    return (yp, ys, p_win_k, p_win_v, p_dn_conv, p_dn_state, p_mem_k, p_mem_v, p_ffn_conv,
            s_win_k, s_win_v, s_dn_conv, s_dn_state, s_ffn_conv)
```

```python
import functools
import math

import jax
import jax.numpy as jnp
from jax import lax
from jax.experimental import pallas as pl
from jax.experimental.pallas import tpu as pltpu

F32 = jnp.float32
BF16 = jnp.bfloat16

D_MODEL = 1024
A_HEADS = 8
A_KV_HEADS = 2
A_HEAD_DIM = 64
A_WIDTH = 512
A_KV_WIDTH = 128
WINDOW = 128
ROT_DIM = 16
ROPE_THETA = 500000.0
DN_HEADS = 4
DN_HEAD_DIM = 128
DN_WIDTH = 512
DN_CONV = 4
X_HEADS = 4
X_HEAD_DIM = 128
X_WIDTH = 512
D_FF = 2816
FFN_CONV = 3
EPS = 1e-6
PAST_LEN = 16384

LANES = 128
SUBLANES = 8
VMEM_LIMIT = 56 * 1024 * 1024
CHUNK = 128
SAMPLE_ROWS = 16


def _cparams(*sem):
    return pltpu.CompilerParams(dimension_semantics=sem, vmem_limit_bytes=VMEM_LIMIT)


def _resident(shape):
    return pl.BlockSpec(shape, lambda *_: (0,) * len(shape), pipeline_mode=pl.Buffered(1))


def _rms(x, g):
    return x * lax.rsqrt(jnp.mean(x * x, axis=-1, keepdims=True) + EPS) * g


def _dot(a, b):
    return jnp.dot(a, b, preferred_element_type=F32)


def _dot_nt(a, b):
    return lax.dot_general(a, b, (((1,), (1,)), ((), ())), preferred_element_type=F32)


def _silu(x):
    return x * jax.nn.sigmoid(x)


def _rope(seg, c, s1, s2):
    return seg * c + pltpu.roll(seg, LANES - 8, 1) * s1 + pltpu.roll(seg, 8, 1) * s2


def _proj_kernel(x_ref, g_ref, c_ref, s1_ref, s2_ref, wqkv_ref, wd_ref, wab_ref, wg_ref,
                 q_ref, k_ref, v_ref, dqkv_ref, dz_ref, ab_ref, ga_ref, gb_ref):
    xb = _rms(x_ref[...], g_ref[...]).astype(BF16)
    c, s1, s2 = c_ref[...], s1_ref[...], s2_ref[...]
    z = _dot(xb, wqkv_ref[...])
    for i in range(A_WIDTH // LANES):
        sl = slice(i * LANES, (i + 1) * LANES)
        q_ref[:, sl] = _rope(z[:, sl], c, s1, s2).astype(BF16)
    k_ref[...] = _rope(z[:, A_WIDTH:A_WIDTH + LANES], c, s1, s2)
    v_ref[...] = z[:, A_WIDTH + LANES:]
    z = _dot(xb, wd_ref[...])
    dz_ref[...] = z[:, 3 * DN_WIDTH:]
    dqkv_ref[...] = z[:, :3 * DN_WIDTH]
    ab_ref[...] = _dot(xb, wab_ref[...])
    z = _dot(xb, wg_ref[...])
    ga_ref[...] = z[:, :D_MODEL]
    gb_ref[...] = z[:, D_MODEL:]


def _proj(x, g, tabs, wqkv, wd, wab, wg, tm):
    m = x.shape[0]
    nt = tabs[0].shape[0] // tm
    row = lambda w: pl.BlockSpec((tm, w), lambda i: (i, 0))
    tab = pl.BlockSpec((tm, LANES), lambda i: (i % nt, 0))
    widths = (A_WIDTH, LANES, LANES, 3 * DN_WIDTH, DN_WIDTH, LANES, D_MODEL, D_MODEL)
    dts = (BF16, F32, F32, F32, F32, F32, F32, F32)
    return pl.pallas_call(
        _proj_kernel,
        grid=(m // tm,),
        in_specs=[row(D_MODEL), _resident((1, D_MODEL)), tab, tab, tab,
                  _resident(wqkv.shape), _resident(wd.shape), _resident(wab.shape), _resident(wg.shape)],
        out_specs=[row(w) for w in widths],
        out_shape=[jax.ShapeDtypeStruct((m, w), d) for w, d in zip(widths, dts)],
        compiler_params=_cparams("parallel"),
        name="in_proj",
    )(x, g, *tabs, wqkv, wd, wab, wg)


def _rope_tables(pos):
    half = ROT_DIM // 2
    inv = ROPE_THETA ** (-2.0 * jnp.arange(half, dtype=F32) / ROT_DIM)
    ang = pos.astype(F32)[:, None] * inv[None, :]
    c, s = jnp.cos(ang), jnp.sin(ang)
    n = pos.shape[0]
    one = jnp.ones((n, A_HEAD_DIM - ROT_DIM), F32)
    z8 = jnp.zeros((n, half), F32)
    z48 = jnp.zeros((n, A_HEAD_DIM - ROT_DIM), F32)
    ct = jnp.concatenate([c, c, one], axis=1)
    s1 = jnp.concatenate([-s, z8, z48], axis=1)
    s2 = jnp.concatenate([z8, s, z48], axis=1)
    two = lambda t: jnp.concatenate([t, t], axis=1)
    return two(ct), two(s1), two(s2)


def _both_halves(t, lane_lo):
    r = pltpu.roll(t, A_HEAD_DIM, 1)
    return jnp.where(lane_lo, t, r), jnp.where(lane_lo, r, t)


def _sink_softmax(s, valid, sink):
    s = jnp.where(valid, s, -jnp.inf)
    m = jnp.maximum(jnp.max(s, axis=-1, keepdims=True), sink)
    p = jnp.exp(s - m)
    den = jnp.sum(p, axis=-1, keepdims=True) + jnp.exp(sink - m)
    return p * (1.0 / den)


def _swa_prompt_kernel(sink_ref, q_ref, kp_ref, kc_ref, vp_ref, vc_ref, o_ref):
    i = pl.program_id(1)
    w = WINDOW
    k = jnp.concatenate([kp_ref[...], kc_ref[...]], axis=0)
    v = jnp.concatenate([vp_ref[...], vc_ref[...]], axis=0)
    lane_lo_k = lax.broadcasted_iota(jnp.int32, k.shape, 1) < A_HEAD_DIM
    kk = [t.astype(BF16) for t in _both_halves(k, lane_lo_k)]
    vv = [t.astype(BF16) for t in _both_halves(v, lane_lo_k)]
    lane_lo = lax.broadcasted_iota(jnp.int32, (w, LANES), 1) < A_HEAD_DIM
    row = lax.broadcasted_iota(jnp.int32, (4 * w, 2 * w), 0)
    col = lax.broadcasted_iota(jnp.int32, (4 * w, 2 * w), 1)
    d = (row & (w - 1)) + w - col
    valid = (d >= 0) & (d < w) & jnp.logical_or(col >= w, i > 0)
    hrow = lax.broadcasted_iota(jnp.int32, (4 * w, 1), 0) // w
    zero = jnp.zeros((), BF16)
    for g in range(A_KV_HEADS):
        parts = []
        for sgm in range(2):
            seg = q_ref[:, (2 * g + sgm) * LANES:(2 * g + sgm + 1) * LANES]
            parts += [jnp.where(lane_lo, seg, zero), jnp.where(lane_lo, zero, seg)]
        qs = jnp.concatenate(parts, axis=0)
        sink = jnp.zeros((4 * w, 1), F32)
        for j in range(4):
            sink = jnp.where(hrow == j, sink_ref[4 * g + j], sink)
        s = _dot_nt(qs, kk[g]) * (A_HEAD_DIM ** -0.5)
        p = _sink_softmax(s, valid, sink).astype(BF16)
        o = _dot(p, vv[g])
        for sgm in range(2):
            o_ref[:, (2 * g + sgm) * LANES:(2 * g + sgm + 1) * LANES] = jnp.where(
                lane_lo, o[(2 * sgm) * w:(2 * sgm + 1) * w], o[(2 * sgm + 1) * w:(2 * sgm + 2) * w]
            ).astype(BF16)


def _swa_prompt(q, k, v, sinks, nseq, seqlen):
    w = WINDOW
    nb = seqlen // w
    cur = lambda b, i: (b * nb + i, 0)
    prev = lambda b, i: (b * nb + jnp.maximum(i - 1, 0), 0)
    kv = lambda im: pl.BlockSpec((w, A_KV_WIDTH), im)
    return pl.pallas_call(
        _swa_prompt_kernel,
        grid=(nseq, nb),
        in_specs=[pl.BlockSpec(memory_space=pltpu.SMEM), pl.BlockSpec((w, A_WIDTH), cur),
                  kv(prev), kv(cur), kv(prev), kv(cur)],
        out_specs=pl.BlockSpec((w, A_WIDTH), cur),
        out_shape=jax.ShapeDtypeStruct(q.shape, BF16),
        compiler_params=_cparams("parallel", "parallel"),
        name="swa_prompt",
    )(sinks, q, k, k, v, v)


def _swa_sample_kernel(sink_ref, q_ref, ck_ref, cv_ref, kn_ref, vn_ref, o_ref, *, nb, t):
    w = ck_ref.shape[1]
    nq = q_ref.shape[1]
    rows = nq // A_HEADS
    r = lax.broadcasted_iota(jnp.int32, (nq, w), 0)
    c = lax.broadcasted_iota(jnp.int32, (nq, w), 1)
    tq = r % rows
    valid_c = c > tq
    rn = lax.broadcasted_iota(jnp.int32, (nq, SAMPLE_ROWS), 0) % rows
    cn = lax.broadcasted_iota(jnp.int32, (nq, SAMPLE_ROWS), 1)
    valid_n = (cn <= rn) & (cn < t)
    hrow = lax.broadcasted_iota(jnp.int32, (nq, 1), 0) // rows
    sink = jnp.zeros((nq, 1), F32)
    for h in range(A_HEADS):
        sink = jnp.where(hrow == h, sink_ref[h], sink)
    for b in range(nb):
        q = q_ref[b]
        ck = ck_ref[b].astype(BF16)
        cv = cv_ref[b].astype(BF16)
        kn = kn_ref[b]
        vn = vn_ref[b]
        sc = jnp.where(valid_c, _dot_nt(q, ck) * (A_HEAD_DIM ** -0.5), -jnp.inf)
        sn = jnp.where(valid_n, _dot_nt(q, kn) * (A_HEAD_DIM ** -0.5), -jnp.inf)
        m = jnp.maximum(jnp.maximum(jnp.max(sc, -1, keepdims=True), jnp.max(sn, -1, keepdims=True)), sink)
        pc = jnp.exp(sc - m)
        pn = jnp.exp(sn - m)
        den = jnp.sum(pc, -1, keepdims=True) + jnp.sum(pn, -1, keepdims=True) + jnp.exp(sink - m)
        o_ref[b] = _dot((pc / den).astype(BF16), cv) + _dot((pn / den).astype(BF16), vn)


def _swa_sample(q8, ck, cv, kn, vn, sinks, t, nb=8):
    bsz, nq, _ = q8.shape
    w = ck.shape[1]
    blk = lambda r: pl.BlockSpec((nb, r, LANES), lambda i: (i, 0, 0))
    return pl.pallas_call(
        functools.partial(_swa_sample_kernel, nb=nb, t=t),
        grid=(bsz // nb,),
        in_specs=[pl.BlockSpec(memory_space=pltpu.SMEM), blk(nq), blk(w), blk(w),
                  blk(SAMPLE_ROWS), blk(SAMPLE_ROWS)],
        out_specs=blk(nq),
        out_shape=jax.ShapeDtypeStruct((bsz, nq, LANES), F32),
        compiler_params=_cparams("parallel"),
        name="swa_sample",
    )(sinks, q8, ck, cv, kn, vn)


def _lane_bcast(x, lane):
    return jnp.broadcast_to(x[:, lane:lane + 1], (x.shape[0], LANES))


def _cumsum_rows(x, block):
    row = lax.broadcasted_iota(jnp.int32, x.shape, 0) % block
    sh = 1
    while sh < block:
        x = x + jnp.where(row >= sh, pltpu.roll(x, sh, 0), 0.0)
        sh *= 2
    return x


def _l2n(t):
    return t * lax.rsqrt(jnp.sum(t * t, axis=-1, keepdims=True) + EPS)


def _gates(ab, alog, dtb):
    x = ab + dtb
    sp = jnp.maximum(x, 0.0) + jnp.log1p(jnp.exp(-jnp.abs(x)))
    return -jnp.exp(alog) * sp, jax.nn.sigmoid(ab)


def _merge_masks(c, top):
    rowi = lax.broadcasted_iota(jnp.int32, (c, c), 0)
    coli = lax.broadcasted_iota(jnp.int32, (c, c), 1)
    masks = []
    s = 1
    while s < top:
        rb, cb = rowi // s, coli // s
        masks.append(((rb // 2) == (cb // 2)) & ((rb % 2) == 1) & ((cb % 2) == 0))
        s *= 2
    return masks


def _each(f, *lists):
    return [f(*t) for t in zip(*lists)]


def _chunk_local(qs, ks, vs, gcols, grows, betas, tril, merges):
    c = qs[0].shape[0]
    decays = _each(lambda gc, gr: jnp.exp(jnp.minimum(gc - gr, 0.0)), gcols, grows)
    kbs = _each(lambda k, b: k * b, ks, betas)
    ms = _each(lambda q, kb, k: _dot_nt(jnp.concatenate([q, kb], axis=0).astype(BF16), k.astype(BF16)),
               qs, kbs, ks)
    qks = _each(lambda m, d: jnp.where(tril, m[:c] * d, 0.0), ms, decays)
    a = _each(lambda m, d: m[c:] * d, ms, decays)
    ns = [-jnp.where(merges[0], x, 0.0) if merges else jnp.zeros_like(x) for x in a]
    for off in merges[1:]:
        aos = [jnp.where(off, x, 0.0) for x in a]
        zs = _each(lambda ao, n: ao + _dot(ao.astype(BF16), n.astype(BF16)), aos, ns)
        ns = _each(lambda n, z: n - z - _dot(n.astype(BF16), z.astype(BF16)), ns, zs)
    egs = [jnp.exp(gc) for gc in gcols]
    rhss = _each(lambda v, b, kb, eg: jnp.concatenate([v * b, kb * eg], axis=1), vs, betas, kbs, egs)
    uws = _each(lambda r, n: r + _dot(n.astype(BF16), r.astype(BF16)), rhss, ns)
    return ([x[:, :LANES] for x in uws], [x[:, LANES:] for x in uws], qks,
            _each(lambda q, eg: q * eg, qs, egs))


def _gdn_prompt_kernel(z_ref, dz_ref, ab_ref, cw_ref, alog_ref, dtb_ref, ng_ref,
                       y_ref, sout_ref, xp_ref, s_ref, *, nc):
    c = CHUNK
    r = nc * c
    pad = SUBLANES

    @pl.when(pl.program_id(1) == 0)
    def _():
        xp_ref[0:pad, :] = jnp.zeros((pad, 3 * DN_WIDTH), F32)
        s_ref[...] = jnp.zeros_like(s_ref)

    xp_ref[pad:pad + r, :] = z_ref[...]
    cw = cw_ref[...]
    conv = xp_ref[pad:pad + r, :] * cw[DN_CONV - 1:DN_CONV]
    for j in range(DN_CONV - 1):
        o = pad - (DN_CONV - 1) + j
        conv = conv + xp_ref[o:o + r, :] * cw[j:j + 1]
    xp_ref[0:pad, :] = xp_ref[r:r + pad, :]
    conv = _silu(conv)
    g, beta = _gates(ab_ref[...], alog_ref[...], dtb_ref[...])
    rowi = lax.broadcasted_iota(jnp.int32, (c, c), 0)
    coli = lax.broadcasted_iota(jnp.int32, (c, c), 1)
    tril = rowi >= coli
    merges = _merge_masks(c, c)
    ng = ng_ref[...]
    heads = range(DN_HEADS)
    probs = [(ci, h) for ci in range(nc) for h in heads]
    rows = lambda ci: slice(ci * c, (ci + 1) * c)
    lanes = lambda part, h: slice(part * DN_WIDTH + h * LANES, part * DN_WIDTH + (h + 1) * LANES)
    gcs = [_cumsum_rows(g[rows(ci)], c) for ci in range(nc)]
    gcts = [gc.T for gc in gcs]
    qs = [_l2n(conv[rows(ci), lanes(0, h)]) * (DN_HEAD_DIM ** -0.5) for ci, h in probs]
    ks = [_l2n(conv[rows(ci), lanes(1, h)]) for ci, h in probs]
    vs = [conv[rows(ci), lanes(2, h)] for ci, h in probs]
    gcols = [_lane_bcast(gcs[ci], h) for ci, h in probs]
    grows = [gcts[ci][h:h + 1, :] for ci, h in probs]
    betas = [_lane_bcast(beta[rows(ci)], DN_HEADS + h) for ci, h in probs]
    us, ws, qks, qds = _chunk_local(qs, ks, vs, gcols, grows, betas, tril, merges)
    glasts = [gc[c - 1:c, :] for gc in gcols]
    kdts = _each(lambda k, gl, gc: (k * jnp.exp(gl - gc)).T, ks, glasts, gcols)
    wqs = _each(lambda w, qd: jnp.concatenate([w, qd], axis=0).astype(BF16), ws, qds)
    qkks = _each(lambda qk, kdt: jnp.concatenate([qk, kdt], axis=0).astype(BF16), qks, kdts)
    ss = [s_ref[h] for h in heads]
    for ci in range(nc):
        pr = [ci * DN_HEADS + h for h in heads]
        r2s = [_dot(wqs[p], s.astype(BF16)) for p, s in zip(pr, ss)]
        vnews = [us[p] - r2[:c] for p, r2 in zip(pr, r2s)]
        r3s = [_dot(qkks[p], vn.astype(BF16)) for p, vn in zip(pr, vnews)]
        ss = [s * jnp.exp(glasts[p]) + r3[c:] for p, s, r3 in zip(pr, ss, r3s)]
        for h in heads:
            o = r2s[h][c:] + r3s[h][:c]
            y_ref[rows(ci), lanes(0, h)] = (_rms(o, ng) * _silu(dz_ref[rows(ci), lanes(0, h)])).astype(BF16)
    for h in heads:
        s_ref[h] = ss[h]

    @pl.when(pl.program_id(1) == pl.num_programs(1) - 1)
    def _():
        sout_ref[0] = s_ref[...]


def _gdn_prompt(dqkv, dz, ab, cw, alog, dtb, ng, nseq, seqlen, nc=4):
    r = nc * CHUNK
    ns = seqlen // r
    row = lambda w: pl.BlockSpec((r, w), lambda b, s: (b * ns + s, 0))
    return pl.pallas_call(
        functools.partial(_gdn_prompt_kernel, nc=nc),
        grid=(nseq, ns),
        in_specs=[row(3 * DN_WIDTH), row(DN_WIDTH), row(LANES), _resident(cw.shape),
                  _resident((1, LANES)), _resident((1, LANES)), _resident((1, LANES))],
        out_specs=[row(DN_WIDTH),
                   pl.BlockSpec((1, DN_HEADS, DN_HEAD_DIM, DN_HEAD_DIM), lambda b, s: (b, 0, 0, 0))],
        out_shape=[jax.ShapeDtypeStruct((nseq * seqlen, DN_WIDTH), BF16),
                   jax.ShapeDtypeStruct((nseq, DN_HEADS, DN_HEAD_DIM, DN_HEAD_DIM), F32)],
        scratch_shapes=[pltpu.VMEM((r + SUBLANES, 3 * DN_WIDTH), F32),
                        pltpu.VMEM((DN_HEADS, DN_HEAD_DIM, DN_HEAD_DIM), F32)],
        compiler_params=_cparams("parallel", "arbitrary"),
        name="gdn_prompt",
    )(dqkv, dz, ab, cw, alog, dtb, ng)


def _gdn_sample_kernel(z_ref, dz_ref, ab_ref, s0_ref, cw_ref, alog_ref, dtb_ref, ng_ref,
                       y_ref, sout_ref, *, t):
    c = CHUNK
    sr = SAMPLE_ROWS
    nb = c // sr
    off = SUBLANES - (DN_CONV - 1)
    cw = cw_ref[...]
    conv = z_ref[0, off + 3:off + 3 + c, :] * cw[3:4]
    for j in range(DN_CONV - 1):
        conv = conv + z_ref[0, off + j:off + j + c, :] * cw[j:j + 1]
    conv = _silu(conv)
    live = (lax.broadcasted_iota(jnp.int32, (c, LANES), 0) % sr) < t
    g, beta = _gates(ab_ref[...], alog_ref[...], dtb_ref[...])
    g = jnp.where(live, g, 0.0)
    beta = jnp.where(live, beta, 0.0)
    gc = _cumsum_rows(g, sr)
    gct = gc.T
    rowi = lax.broadcasted_iota(jnp.int32, (c, c), 0)
    coli = lax.broadcasted_iota(jnp.int32, (c, c), 1)
    tril = ((rowi // sr) == (coli // sr)) & (rowi >= coli)
    merges = _merge_masks(c, pl.next_power_of_2(t))
    ng = ng_ref[...]
    rowb = lax.broadcasted_iota(jnp.int32, (c, LANES), 0) // sr
    heads = range(DN_HEADS)
    lanes = lambda part, h: slice(part * DN_WIDTH + h * LANES, part * DN_WIDTH + (h + 1) * LANES)
    ks = [_l2n(conv[:, lanes(1, h)]) for h in heads]
    gcols = [_lane_bcast(gc, h) for h in heads]
    us, ws, qks, qds = _chunk_local(
        [_l2n(conv[:, lanes(0, h)]) * (DN_HEAD_DIM ** -0.5) for h in heads], ks,
        [conv[:, lanes(2, h)] for h in heads], gcols, [gct[h:h + 1, :] for h in heads],
        [_lane_bcast(beta, DN_HEADS + h) for h in heads], tril, merges)
    for h in heads:
        hs = lanes(0, h)
        k, gcol, u, w, qk, qd = ks[h], gcols[h], us[h], ws[h], qks[h], qds[h]
        wq = jnp.concatenate([w, qd], axis=1).astype(BF16)
        vnews, qss, sbs = [], [], []
        for b in range(nb):
            rs = slice(b * sr, (b + 1) * sr)
            s = s0_ref[b, h]
            sbs.append(s)
            r2 = _dot(jnp.concatenate([wq[rs, :LANES], wq[rs, LANES:]], axis=0), s.astype(BF16))
            vnews.append(u[rs] - r2[:sr])
            qss.append(r2[sr:])
        vnew = jnp.concatenate(vnews, axis=0)
        o = jnp.concatenate(qss, axis=0) + _dot(qk.astype(BF16), vnew.astype(BF16))
        y_ref[:, hs] = _rms(o, ng) * _silu(dz_ref[:, hs])
        glast = jnp.concatenate(
            [jnp.broadcast_to(gcol[b * sr + sr - 1:b * sr + sr, :], (sr, LANES)) for b in range(nb)], axis=0)
        kdt = (k * jnp.exp(glast - gcol)).T.astype(BF16)
        for b in range(nb):
            vb = jnp.where(rowb == b, vnew, 0.0).astype(BF16)
            sout_ref[b, h] = sbs[b] * jnp.exp(glast[b * sr:b * sr + 1, :]) + _dot(kdt, vb)


def _gdn_sample(zg, dz, ab, s0, cw, alog, dtb, ng, t):
    ngrp = zg.shape[0]
    nb = CHUNK // SAMPLE_ROWS
    row = lambda w: pl.BlockSpec((CHUNK, w), lambda i: (i, 0))
    st = pl.BlockSpec((nb, DN_HEADS, DN_HEAD_DIM, DN_HEAD_DIM), lambda i: (i, 0, 0, 0))
    return pl.pallas_call(
        functools.partial(_gdn_sample_kernel, t=t),
        grid=(ngrp,),
        in_specs=[pl.BlockSpec((1, CHUNK + SUBLANES, 3 * DN_WIDTH), lambda i: (i, 0, 0)),
                  row(DN_WIDTH), row(LANES), st, _resident(cw.shape),
                  _resident((1, LANES)), _resident((1, LANES)), _resident((1, LANES))],
        out_specs=[row(DN_WIDTH), st],
        out_shape=[jax.ShapeDtypeStruct((ngrp * CHUNK, DN_WIDTH), F32),
                   jax.ShapeDtypeStruct(s0.shape, F32)],
        compiler_params=_cparams("parallel"),
        name="gdn_sample",
    )(zg, dz, ab, s0, cw, alog, dtb, ng)


def _merge_kernel(x_ref, ya_ref, yb_ref, ga_ref, gb_ref, wa_ref, wb_ref, wo_ref, g_ref, wq_ref,
                  h_ref, hq_ref):
    mix = (jax.nn.sigmoid(ga_ref[...]) * _dot(ya_ref[...], wa_ref[...])
           + jax.nn.sigmoid(gb_ref[...]) * _dot(yb_ref[...], wb_ref[...]))
    h = x_ref[...] + _dot(mix.astype(BF16), wo_ref[...])
    h_ref[...] = h
    hq_ref[...] = _dot(_rms(h, g_ref[...]).astype(BF16), wq_ref[...]).astype(BF16)


def _merge(x, ya, yb, ga, gb, wa, wb, wo, g, wq, tm):
    m = x.shape[0]
    row = lambda w: pl.BlockSpec((tm, w), lambda i: (i, 0))
    return pl.pallas_call(
        _merge_kernel,
        grid=(m // tm,),
        in_specs=[row(D_MODEL), row(A_WIDTH), row(DN_WIDTH), row(D_MODEL), row(D_MODEL),
                  _resident(wa.shape), _resident(wb.shape), _resident(wo.shape),
                  _resident((1, D_MODEL)), _resident(wq.shape)],
        out_specs=[row(D_MODEL), row(X_WIDTH)],
        out_shape=[jax.ShapeDtypeStruct((m, D_MODEL), F32), jax.ShapeDtypeStruct((m, X_WIDTH), BF16)],
        compiler_params=_cparams("parallel"),
        name="merge",
    )(x, ya, yb, ga, gb, wa, wb, wo, g, wq)


def _memkv_kernel(m_ref, g_ref, w_ref, k_ref, v_ref):
    z = _dot(_rms(m_ref[...], g_ref[...]).astype(BF16), w_ref[...])
    k_ref[...] = z[:, :X_WIDTH]
    v_ref[...] = z[:, X_WIDTH:]


def _memkv(mem, g, w, tm):
    m = mem.shape[0]
    row = lambda wd: pl.BlockSpec((tm, wd), lambda i: (i, 0))
    return pl.pallas_call(
        _memkv_kernel,
        grid=(m // tm,),
        in_specs=[row(D_MODEL), _resident((1, D_MODEL)), _resident(w.shape)],
        out_specs=[row(X_WIDTH), row(X_WIDTH)],
        out_shape=[jax.ShapeDtypeStruct((m, X_WIDTH), F32)] * 2,
        compiler_params=_cparams("parallel"),
        name="mem_kv",
    )(mem, g, w)


def _softmax_rows(s):
    e = jnp.exp(s - jnp.max(s, axis=-1, keepdims=True))
    return e * (1.0 / jnp.sum(e, axis=-1, keepdims=True))


def _xattn_prompt_kernel(hq_ref, mk_ref, mv_ref, o_ref):
    for h in range(X_HEADS):
        hs = slice(h * LANES, (h + 1) * LANES)
        s = _dot_nt(hq_ref[:, hs], mk_ref[:, hs].astype(BF16)) * (X_HEAD_DIM ** -0.5)
        o_ref[:, hs] = _dot(_softmax_rows(s).astype(BF16), mv_ref[:, hs].astype(BF16)).astype(BF16)


def _xattn_prompt(hq, mk, mv, nseq, seqlen, tq):
    nt = seqlen // tq
    nm = mk.shape[0] // nseq
    row = pl.BlockSpec((tq, X_WIDTH), lambda b, i: (b * nt + i, 0))
    mem = pl.BlockSpec((nm, X_WIDTH), lambda b, i: (b, 0))
    return pl.pallas_call(
        _xattn_prompt_kernel,
        grid=(nseq, nt),
        in_specs=[row, mem, mem],
        out_specs=row,
        out_shape=jax.ShapeDtypeStruct(hq.shape, BF16),
        compiler_params=_cparams("parallel", "parallel"),
        name="xattn_prompt",
    )(hq, mk, mv)


def _xattn_sample_kernel(hq_ref, mk_ref, mv_ref, o_ref, *, nb):
    rows = hq_ref.shape[1]
    nmh = mk_ref.shape[1]
    qh = lax.broadcasted_iota(jnp.int32, (X_HEADS * rows, nmh), 0) // rows
    mh = lax.broadcasted_iota(jnp.int32, (X_HEADS * rows, nmh), 1) % X_HEADS
    same = qh == mh
    for b in range(nb):
        q = jnp.concatenate([hq_ref[b, :, h * LANES:(h + 1) * LANES] for h in range(X_HEADS)], axis=0)
        s = _dot_nt(q, mk_ref[b].astype(BF16)) * (X_HEAD_DIM ** -0.5)
        p = _softmax_rows(jnp.where(same, s, -jnp.inf))
        o = _dot(p.astype(BF16), mv_ref[b].astype(BF16))
        for h in range(X_HEADS):
            o_ref[b, :, h * LANES:(h + 1) * LANES] = o[h * rows:(h + 1) * rows].astype(BF16)


def _xattn_sample(hq, mk, mv, nb=8):
    bsz, rows, _ = hq.shape
    q = pl.BlockSpec((nb, rows, X_WIDTH), lambda i: (i, 0, 0))
    mem = pl.BlockSpec((nb, mk.shape[1], X_HEAD_DIM), lambda i: (i, 0, 0))
    return pl.pallas_call(
        functools.partial(_xattn_sample_kernel, nb=nb),
        grid=(bsz // nb,),
        in_specs=[q, mem, mem],
        out_specs=q,
        out_shape=jax.ShapeDtypeStruct(hq.shape, BF16),
        compiler_params=_cparams("parallel"),
        name="xattn_sample",
    )(hq, mk, mv)


def _ffn_kernel(h_ref, xo_ref, cin_ref, wxo_ref, g_ref, wu_ref, wv_ref, cw_ref, wd_ref, fg_ref,
                y_ref, tail_ref, xp_ref, acc_ref, *, stride, padc, fc):
    tm = h_ref.shape[0]

    @pl.when(pl.program_id(1) == 0)
    def _():
        tail_ref[...] = cin_ref[...]

    h = h_ref[...] + _dot(xo_ref[...], wxo_ref[...])
    acc_ref[...] = h
    hn = _rms(h, g_ref[...]).astype(BF16)
    for ci in range(D_FF // fc):
        cs = slice(ci * fc, (ci + 1) * fc)
        u = _dot(hn, wu_ref[:, cs])
        gate = _dot(hn, wv_ref[:, cs])
        xp_ref[0:padc, :] = tail_ref[0, :, cs]
        xp_ref[padc:padc + tm, :] = u
        cw = cw_ref[:, cs]
        uc = u * cw[2:3]
        for j in range(FFN_CONV - 1):
            o = padc - (FFN_CONV - 1 - j) * stride
            uc = uc + xp_ref[o:o + tm, :] * cw[j:j + 1]
        tail_ref[0, :, cs] = xp_ref[tm:tm + padc, :]
        acc_ref[...] += _dot((_silu(uc) * gate).astype(BF16), wd_ref[cs, :])
    y_ref[...] = _rms(acc_ref[...], fg_ref[...])


def _ffn(h, xo, cin, wxo, g, wu, wv, cw, wd, fg, tm, stride, fc=D_FF // 2):
    m = h.shape[0]
    nseq, padc, _ = cin.shape
    nt = m // nseq // tm
    row = lambda w: pl.BlockSpec((tm, w), lambda s, j: (s * nt + j, 0))
    car = pl.BlockSpec((1, padc, D_FF), lambda s, j: (s, 0, 0))
    return pl.pallas_call(
        functools.partial(_ffn_kernel, stride=stride, padc=padc, fc=fc),
        grid=(nseq, nt),
        in_specs=[row(D_MODEL), row(X_WIDTH), car, _resident(wxo.shape), _resident((1, D_MODEL)),
                  _resident(wu.shape), _resident(wv.shape), _resident(cw.shape), _resident(wd.shape),
                  _resident((1, D_MODEL))],
        out_specs=[row(D_MODEL), car],
        out_shape=[jax.ShapeDtypeStruct((m, D_MODEL), F32), jax.ShapeDtypeStruct(cin.shape, F32)],
        scratch_shapes=[pltpu.VMEM((tm + padc, fc), F32), pltpu.VMEM((tm, D_MODEL), F32)],
        compiler_params=_cparams("parallel", "arbitrary"),
        name="ffn",
    )(h, xo, cin, wxo, g, wu, wv, cw, wd, fg)


def _lane_row(vec, offset=0):
    return jnp.zeros((1, LANES), F32).at[0, offset:offset + vec.shape[0]].set(vec.astype(F32))


def _prep(p):
    w_in = p["w_in"]
    o = 0
    cuts = {}
    for name, wdt in (("qkv", A_WIDTH + 2 * A_KV_WIDTH), ("d", 4 * DN_WIDTH), ("ab", 2 * DN_HEADS),
                      ("g", 2 * D_MODEL)):
        cuts[name] = w_in[:, o:o + wdt]
        o += wdt
    w = {
        "wqkv": cuts["qkv"].astype(BF16),
        "wd": cuts["d"].astype(BF16),
        "wab": jnp.pad(cuts["ab"], ((0, 0), (0, LANES - 2 * DN_HEADS))).astype(BF16),
        "wg": cuts["g"].astype(BF16),
        "norm_mix_g": p["norm_mix_g"].reshape(1, D_MODEL),
        "dn_conv_w": p["dn_conv_w"],
        "alog": _lane_row(p["dn_a_log"]),
        "dtb": _lane_row(p["dn_dt_bias"]),
        "dn_norm_g": p["dn_norm_g"].reshape(1, DN_HEAD_DIM),
        "sinks": p["sinks"].astype(F32),
        "w_br_a": p["w_br_a"].astype(BF16),
        "w_br_b": p["w_br_b"].astype(BF16),
        "w_mix_out": p["w_mix_out"].astype(BF16),
        "norm_x_g": p["norm_x_g"].reshape(1, D_MODEL),
        "w_xq": p["w_xq"].astype(BF16),
        "w_xo": p["w_xo"].astype(BF16),
        "norm_ffn_g": p["norm_ffn_g"].reshape(1, D_MODEL),
        "w_up": p["w_up"][:, :D_FF].astype(BF16),
        "w_gate": p["w_up"][:, D_FF:].astype(BF16),
        "ffn_conv_w": p["ffn_conv_w"],
        "w_down": p["w_down"].astype(BF16),
        "final_norm_g": p["final_norm_g"].reshape(1, D_MODEL),
    }
    return w


def _tile(m, pref):
    return pref if m % pref == 0 else m


def _prompt_layer(x, mem, norm_mem_g, w_xkv, w):
    bsz, seqlen, _ = x.shape
    m = bsz * seqlen
    x2 = x.reshape(m, D_MODEL)
    tm = _tile(seqlen, 512)
    tabs = _rope_tables(jnp.arange(seqlen, dtype=jnp.int32))
    q, k, v, dqkv, dz, ab, ga, gb = _proj(x2, w["norm_mix_g"], tabs, w["wqkv"], w["wd"], w["wab"], w["wg"], tm)
    ya = _swa_prompt(q, k, v, w["sinks"], bsz, seqlen)
    yb, dn_state = _gdn_prompt(dqkv, dz, ab, w["dn_conv_w"], w["alog"], w["dtb"], w["dn_norm_g"], bsz, seqlen)
    h, hq = _merge(x2, ya, yb, ga, gb, w["w_br_a"], w["w_br_b"], w["w_mix_out"], w["norm_x_g"], w["w_xq"], tm)
    nm = mem.shape[1]
    mk, mv = _memkv(mem.reshape(bsz * nm, D_MODEL), norm_mem_g.reshape(1, D_MODEL), w_xkv.astype(BF16),
                    _tile(bsz * nm, 512))
    xo = _xattn_prompt(hq, mk, mv, bsz, seqlen, tm)
    cin = jnp.zeros((bsz, SUBLANES, D_FF), F32)
    y, tail = _ffn(h, xo, cin, w["w_xo"], w["norm_ffn_g"], w["w_up"], w["w_gate"], w["ffn_conv_w"],
                   w["w_down"], w["final_norm_g"], tm, 1)
    wb = min(WINDOW, seqlen)
    new = (
        k.reshape(bsz, seqlen, A_KV_HEADS, A_HEAD_DIM)[:, seqlen - wb:],
        v.reshape(bsz, seqlen, A_KV_HEADS, A_HEAD_DIM)[:, seqlen - wb:],
        dqkv.reshape(bsz, seqlen, 3 * DN_WIDTH)[:, seqlen - (DN_CONV - 1):],
        dn_state,
        mk.reshape(bsz, nm, X_HEADS, X_HEAD_DIM),
        mv.reshape(bsz, nm, X_HEADS, X_HEAD_DIM),
        tail[:, SUBLANES - (FFN_CONV - 1):],
    )
    return y.reshape(bsz, seqlen, D_MODEL), new


def _sample_layer(x, pos0, win_k, win_v, dn_buf, dn_state, mem_k, mem_v, ffn_buf, w):
    bsz, t, _ = x.shape
    m = bsz * t
    sr = SAMPLE_ROWS
    x2 = x.reshape(m, D_MODEL)
    tabs = _rope_tables(jnp.tile(pos0 + jnp.arange(t, dtype=jnp.int32), bsz))
    q, k, v, dqkv, dz, ab, ga, gb = _proj(x2, w["norm_mix_g"], tabs, w["wqkv"], w["wd"], w["wab"], w["wg"], m)

    wlen = win_k.shape[1]
    q5 = q.reshape(bsz, t, A_KV_HEADS, A_HEADS // A_KV_HEADS, A_HEAD_DIM)
    q5 = jnp.transpose(q5, (0, 2, 3, 1, 4))
    eye = jnp.eye(A_KV_HEADS, dtype=BF16)
    q8 = (q5[:, :, :, :, None, :] * eye[None, :, None, None, :, None]).reshape(bsz, A_HEADS * t, A_KV_WIDTH)
    padrows = lambda a: jnp.pad(a.reshape(bsz, t, -1), ((0, 0), (0, sr - t), (0, 0)))
    o8 = _swa_sample(q8, win_k.reshape(bsz, wlen, A_KV_WIDTH), win_v.reshape(bsz, wlen, A_KV_WIDTH),
                     padrows(k).astype(BF16), padrows(v).astype(BF16), w["sinks"], t)
    o8 = o8.reshape(bsz, A_KV_HEADS, A_HEADS // A_KV_HEADS, t, A_KV_HEADS, A_HEAD_DIM)
    o8 = jnp.stack([o8[:, g, :, :, g] for g in range(A_KV_HEADS)], axis=1)
    ya = jnp.transpose(o8, (0, 3, 1, 2, 4)).reshape(m, A_WIDTH).astype(BF16)
    new_wk = jnp.concatenate([win_k, k.reshape(bsz, t, A_KV_HEADS, A_HEAD_DIM)], axis=1)[:, -wlen:]
    new_wv = jnp.concatenate([win_v, v.reshape(bsz, t, A_KV_HEADS, A_HEAD_DIM)], axis=1)[:, -wlen:]

    nb = CHUNK // sr
    hist = jnp.concatenate([dn_buf, dqkv.reshape(bsz, t, 3 * DN_WIDTH)], axis=1)
    rec = jnp.pad(hist, ((0, 0), (0, sr - hist.shape[1]), (0, 0))).reshape(bsz // nb, CHUNK, 3 * DN_WIDTH)
    zg = jnp.pad(rec, ((0, 0), (SUBLANES - (DN_CONV - 1), DN_CONV - 1), (0, 0)))
    yb16, new_s = _gdn_sample(zg, padrows(dz).reshape(bsz * sr, DN_WIDTH), padrows(ab).reshape(bsz * sr, LANES),
                              dn_state, w["dn_conv_w"], w["alog"], w["dtb"], w["dn_norm_g"], t)
    yb = yb16.reshape(bsz, sr, DN_WIDTH)[:, :t].reshape(m, DN_WIDTH).astype(BF16)
    new_dn_buf = hist[:, -(DN_CONV - 1):]

    h, hq = _merge(x2, ya, yb, ga, gb, w["w_br_a"], w["w_br_b"], w["w_mix_out"], w["norm_x_g"], w["w_xq"], m)
    nm = mem_k.shape[1]
    xo = _xattn_sample(padrows(hq), mem_k.reshape(bsz, nm * X_HEADS, X_HEAD_DIM),
                       mem_v.reshape(bsz, nm * X_HEADS, X_HEAD_DIM))
    xo = xo[:, :t]

    tmaj = lambda a: jnp.transpose(a.reshape(bsz, t, -1), (1, 0, 2)).reshape(m, -1)
    cin = jnp.transpose(ffn_buf, (1, 0, 2)).reshape(1, (FFN_CONV - 1) * bsz, D_FF)
    y, tail = _ffn(tmaj(h), tmaj(xo), cin, w["w_xo"], w["norm_ffn_g"], w["w_up"], w["w_gate"], w["ffn_conv_w"],
                   w["w_down"], w["final_norm_g"], m, bsz)
    y = jnp.transpose(y.reshape(t, bsz, D_MODEL), (1, 0, 2))
    new_ffn = jnp.transpose(tail.reshape(FFN_CONV - 1, bsz, D_FF), (1, 0, 2))
    return y, (new_wk, new_wv, new_dn_buf, new_s, new_ffn)


def kernel(x_prompt, x_sample, mem_prompt, cache_win_k, cache_win_v, state_dn_conv, state_dn, cache_mem_k, cache_mem_v, state_ffn_conv, norm_mix_g, w_in, dn_conv_w, dn_a_log, dn_dt_bias, dn_norm_g, attn_sinks, w_br_a, w_br_b, w_mix_out, norm_x_g, norm_mem_g, w_xq, w_xkv, w_xo, norm_ffn_g, w_up, ffn_conv_w, w_down, final_norm_g):
    p = {"norm_mix_g": norm_mix_g[0], "w_in": w_in[0], "dn_conv_w": dn_conv_w[0], "dn_a_log": dn_a_log[0],
         "dn_dt_bias": dn_dt_bias[0], "dn_norm_g": dn_norm_g[0], "sinks": attn_sinks[0], "w_br_a": w_br_a[0],
         "w_br_b": w_br_b[0], "w_mix_out": w_mix_out[0], "norm_x_g": norm_x_g[0], "w_xq": w_xq[0],
         "w_xo": w_xo[0], "norm_ffn_g": norm_ffn_g[0], "w_up": w_up[0], "ffn_conv_w": ffn_conv_w[0],
         "w_down": w_down[0], "final_norm_g": final_norm_g}
    w = _prep(p)
    yp, newp = _prompt_layer(x_prompt, mem_prompt, norm_mem_g[0], w_xkv[0], w)
    ys, news = _sample_layer(x_sample, PAST_LEN, cache_win_k[0], cache_win_v[0], state_dn_conv[0], state_dn[0],
                             cache_mem_k[0], cache_mem_v[0], state_ffn_conv[0], w)
    lead = lambda a: a[None]
    p_win_k, p_win_v, p_dn_conv, p_dn_state, p_mem_k, p_mem_v, p_ffn_conv = [lead(a) for a in newp]
    s_win_k, s_win_v, s_dn_conv, s_dn_state, s_ffn_conv = [lead(a) for a in news]
    return (yp, ys, p_win_k, p_win_v, p_dn_conv, p_dn_state, p_mem_k, p_mem_v, p_ffn_conv,
            s_win_k, s_win_v, s_dn_conv, s_dn_state, s_ffn_conv)
```

```python
import functools
import math

import jax
import jax.numpy as jnp
from jax import lax
from jax.experimental import pallas as pl
from jax.experimental.pallas import tpu as pltpu

F32 = jnp.float32
BF16 = jnp.bfloat16

D_MODEL = 1024
A_HEADS = 8
A_KV_HEADS = 2
A_HEAD_DIM = 64
A_WIDTH = 512
A_KV_WIDTH = 128
WINDOW = 128
ROT_DIM = 16
ROPE_THETA = 500000.0
DN_HEADS = 4
DN_HEAD_DIM = 128
DN_WIDTH = 512
DN_CONV = 4
X_HEADS = 4
X_HEAD_DIM = 128
X_WIDTH = 512
D_FF = 2816
FFN_CONV = 3
EPS = 1e-6
PAST_LEN = 16384

LANES = 128
SUBLANES = 8
VMEM_LIMIT = 56 * 1024 * 1024
CHUNK = 128
SAMPLE_ROWS = 16


def _cparams(*sem):
    return pltpu.CompilerParams(dimension_semantics=sem, vmem_limit_bytes=VMEM_LIMIT)


def _resident(shape):
    return pl.BlockSpec(shape, lambda *_: (0,) * len(shape), pipeline_mode=pl.Buffered(1))


def _rms(x, g):
    return x * lax.rsqrt(jnp.mean(x * x, axis=-1, keepdims=True) + EPS) * g


def _dot(a, b):
    return jnp.dot(a, b, preferred_element_type=F32)


def _dot_nt(a, b):
    return lax.dot_general(a, b, (((1,), (1,)), ((), ())), preferred_element_type=F32)


def _silu(x):
    return x * jax.nn.sigmoid(x)


def _rope(seg, c, s1, s2):
    return seg * c + pltpu.roll(seg, LANES - 8, 1) * s1 + pltpu.roll(seg, 8, 1) * s2


def _proj_kernel(x_ref, g_ref, c_ref, s1_ref, s2_ref, wqkv_ref, wd_ref, wab_ref, wg_ref,
                 q_ref, k_ref, v_ref, dqkv_ref, dz_ref, ab_ref, ga_ref, gb_ref):
    xb = _rms(x_ref[...], g_ref[...]).astype(BF16)
    c, s1, s2 = c_ref[...], s1_ref[...], s2_ref[...]
    z = _dot(xb, wqkv_ref[...])
    for i in range(A_WIDTH // LANES):
        sl = slice(i * LANES, (i + 1) * LANES)
        q_ref[:, sl] = _rope(z[:, sl], c, s1, s2).astype(BF16)
    k_ref[...] = _rope(z[:, A_WIDTH:A_WIDTH + LANES], c, s1, s2)
    v_ref[...] = z[:, A_WIDTH + LANES:]
    z = _dot(xb, wd_ref[...])
    dz_ref[...] = z[:, 3 * DN_WIDTH:]
    dqkv_ref[...] = z[:, :3 * DN_WIDTH]
    ab_ref[...] = _dot(xb, wab_ref[...])
    z = _dot(xb, wg_ref[...])
    ga_ref[...] = z[:, :D_MODEL]
    gb_ref[...] = z[:, D_MODEL:]


def _proj(x, g, tabs, wqkv, wd, wab, wg, tm):
    m = x.shape[0]
    nt = tabs[0].shape[0] // tm
    row = lambda w: pl.BlockSpec((tm, w), lambda i: (i, 0))
    tab = pl.BlockSpec((tm, LANES), lambda i: (i % nt, 0))
    widths = (A_WIDTH, LANES, LANES, 3 * DN_WIDTH, DN_WIDTH, LANES, D_MODEL, D_MODEL)
    dts = (BF16, F32, F32, F32, F32, F32, F32, F32)
    return pl.pallas_call(
        _proj_kernel,
        grid=(m // tm,),
        in_specs=[row(D_MODEL), _resident((1, D_MODEL)), tab, tab, tab,
                  _resident(wqkv.shape), _resident(wd.shape), _resident(wab.shape), _resident(wg.shape)],
        out_specs=[row(w) for w in widths],
        out_shape=[jax.ShapeDtypeStruct((m, w), d) for w, d in zip(widths, dts)],
        compiler_params=_cparams("parallel"),
        name="in_proj",
    )(x, g, *tabs, wqkv, wd, wab, wg)


def _rope_tables(pos):
    half = ROT_DIM // 2
    inv = ROPE_THETA ** (-2.0 * jnp.arange(half, dtype=F32) / ROT_DIM)
    ang = pos.astype(F32)[:, None] * inv[None, :]
    c, s = jnp.cos(ang), jnp.sin(ang)
    n = pos.shape[0]
    one = jnp.ones((n, A_HEAD_DIM - ROT_DIM), F32)
    z8 = jnp.zeros((n, half), F32)
    z48 = jnp.zeros((n, A_HEAD_DIM - ROT_DIM), F32)
    ct = jnp.concatenate([c, c, one], axis=1)
    s1 = jnp.concatenate([-s, z8, z48], axis=1)
    s2 = jnp.concatenate([z8, s, z48], axis=1)
    two = lambda t: jnp.concatenate([t, t], axis=1)
    return two(ct), two(s1), two(s2)


def _both_halves(t, lane_lo):
    r = pltpu.roll(t, A_HEAD_DIM, 1)
    return jnp.where(lane_lo, t, r), jnp.where(lane_lo, r, t)


def _sink_softmax(s, valid, sink):
    s = jnp.where(valid, s, -jnp.inf)
    m = jnp.maximum(jnp.max(s, axis=-1, keepdims=True), sink)
    p = jnp.exp(s - m)
    den = jnp.sum(p, axis=-1, keepdims=True) + jnp.exp(sink - m)
    return p * (1.0 / den)


def _swa_tile(sink_ref, q_ref, k, v, first):
    w = WINDOW
    lane_lo_k = lax.broadcasted_iota(jnp.int32, k.shape, 1) < A_HEAD_DIM
    kk = [t.astype(BF16) for t in _both_halves(k, lane_lo_k)]
    vv = [t.astype(BF16) for t in _both_halves(v, lane_lo_k)]
    lane_lo = lax.broadcasted_iota(jnp.int32, (w, LANES), 1) < A_HEAD_DIM
    row = lax.broadcasted_iota(jnp.int32, (4 * w, 2 * w), 0)
    col = lax.broadcasted_iota(jnp.int32, (4 * w, 2 * w), 1)
    d = (row & (w - 1)) + w - col
    band = (d >= 0) & (d < w)
    band_first = band & (col >= jnp.where(first, w, 0))
    hrow = lax.broadcasted_iota(jnp.int32, (4 * w, 1), 0) // w
    sinks = []
    for g in range(A_KV_HEADS):
        sink = jnp.zeros((4 * w, 1), F32)
        for j in range(4):
            sink = jnp.where(hrow == j, sink_ref[4 * g + j], sink)
        sinks.append(sink)
    zero = jnp.zeros((), BF16)
    blocks = []
    for b in range(q_ref.shape[0] // w):
        segs = []
        for g in range(A_KV_HEADS):
            parts = []
            for sgm in range(2):
                seg = q_ref[b * w:(b + 1) * w, (2 * g + sgm) * LANES:(2 * g + sgm + 1) * LANES]
                parts += [jnp.where(lane_lo, seg, zero), jnp.where(lane_lo, zero, seg)]
            qs = jnp.concatenate(parts, axis=0)
            s = _dot_nt(qs, kk[g][b * w:(b + 2) * w]) * (A_HEAD_DIM ** -0.5)
            p = _sink_softmax(s, band_first if b == 0 else band, sinks[g]).astype(BF16)
            o = _dot(p, vv[g][b * w:(b + 2) * w])
            segs += [jnp.where(lane_lo, o[(2 * sgm) * w:(2 * sgm + 1) * w],
                               o[(2 * sgm + 1) * w:(2 * sgm + 2) * w]).astype(BF16) for sgm in range(2)]
        blocks.append(jnp.concatenate(segs, axis=1))
    return jnp.concatenate(blocks, axis=0)


def _swa_sample_kernel(sink_ref, q_ref, ck_ref, cv_ref, kn_ref, vn_ref, o_ref, *, nb, t):
    w = ck_ref.shape[1]
    nq = q_ref.shape[1]
    rows = nq // A_HEADS
    r = lax.broadcasted_iota(jnp.int32, (nq, w), 0)
    c = lax.broadcasted_iota(jnp.int32, (nq, w), 1)
    tq = r % rows
    valid_c = c > tq
    rn = lax.broadcasted_iota(jnp.int32, (nq, SAMPLE_ROWS), 0) % rows
    cn = lax.broadcasted_iota(jnp.int32, (nq, SAMPLE_ROWS), 1)
    valid_n = (cn <= rn) & (cn < t)
    hrow = lax.broadcasted_iota(jnp.int32, (nq, 1), 0) // rows
    sink = jnp.zeros((nq, 1), F32)
    for h in range(A_HEADS):
        sink = jnp.where(hrow == h, sink_ref[h], sink)
    for b in range(nb):
        q = q_ref[b]
        ck = ck_ref[b].astype(BF16)
        cv = cv_ref[b].astype(BF16)
        kn = kn_ref[b]
        vn = vn_ref[b]
        sc = jnp.where(valid_c, _dot_nt(q, ck) * (A_HEAD_DIM ** -0.5), -jnp.inf)
        sn = jnp.where(valid_n, _dot_nt(q, kn) * (A_HEAD_DIM ** -0.5), -jnp.inf)
        m = jnp.maximum(jnp.maximum(jnp.max(sc, -1, keepdims=True), jnp.max(sn, -1, keepdims=True)), sink)
        pc = jnp.exp(sc - m)
        pn = jnp.exp(sn - m)
        den = jnp.sum(pc, -1, keepdims=True) + jnp.sum(pn, -1, keepdims=True) + jnp.exp(sink - m)
        o_ref[b] = _dot((pc / den).astype(BF16), cv) + _dot((pn / den).astype(BF16), vn)


def _swa_sample(q8, ck, cv, kn, vn, sinks, t, nb=8):
    bsz, nq, _ = q8.shape
    w = ck.shape[1]
    blk = lambda r: pl.BlockSpec((nb, r, LANES), lambda i: (i, 0, 0))
    return pl.pallas_call(
        functools.partial(_swa_sample_kernel, nb=nb, t=t),
        grid=(bsz // nb,),
        in_specs=[pl.BlockSpec(memory_space=pltpu.SMEM), blk(nq), blk(w), blk(w),
                  blk(SAMPLE_ROWS), blk(SAMPLE_ROWS)],
        out_specs=blk(nq),
        out_shape=jax.ShapeDtypeStruct((bsz, nq, LANES), F32),
        compiler_params=_cparams("parallel"),
        name="swa_sample",
    )(sinks, q8, ck, cv, kn, vn)


def _lane_bcast(x, lane):
    return jnp.broadcast_to(x[:, lane:lane + 1], (x.shape[0], LANES))


def _cumsum_rows(x, block):
    row = lax.broadcasted_iota(jnp.int32, x.shape, 0) % block
    sh = 1
    while sh < block:
        x = x + jnp.where(row >= sh, pltpu.roll(x, sh, 0), 0.0)
        sh *= 2
    return x


def _l2n(t):
    return t * lax.rsqrt(jnp.sum(t * t, axis=-1, keepdims=True) + EPS)


def _gates(ab, alog, dtb):
    x = ab + dtb
    sp = jnp.maximum(x, 0.0) + jnp.log1p(jnp.exp(-jnp.abs(x)))
    return -jnp.exp(alog) * sp, jax.nn.sigmoid(ab)


def _merge_masks(c, top):
    rowi = lax.broadcasted_iota(jnp.int32, (c, c), 0)
    coli = lax.broadcasted_iota(jnp.int32, (c, c), 1)
    masks = []
    s = 1
    while s < top:
        rb, cb = rowi // s, coli // s
        masks.append(((rb // 2) == (cb // 2)) & ((rb % 2) == 1) & ((cb % 2) == 0))
        s *= 2
    return masks


def _each(f, *lists):
    return [f(*t) for t in zip(*lists)]


def _chunk_local(qs, ks, vs, gcols, grows, betas, tril, merges):
    c = qs[0].shape[0]
    decays = _each(lambda gc, gr: jnp.exp(jnp.minimum(gc - gr, 0.0)), gcols, grows)
    kbs = _each(lambda k, b: k * b, ks, betas)
    ms = _each(lambda q, kb, k: _dot_nt(jnp.concatenate([q, kb], axis=0).astype(BF16), k.astype(BF16)),
               qs, kbs, ks)
    qks = _each(lambda m, d: jnp.where(tril, m[:c] * d, 0.0), ms, decays)
    a = _each(lambda m, d: m[c:] * d, ms, decays)
    ns = [-jnp.where(merges[0], x, 0.0) if merges else jnp.zeros_like(x) for x in a]
    for off in merges[1:]:
        aos = [jnp.where(off, x, 0.0) for x in a]
        zs = _each(lambda ao, n: ao + _dot(ao.astype(BF16), n.astype(BF16)), aos, ns)
        ns = _each(lambda n, z: n - z - _dot(n.astype(BF16), z.astype(BF16)), ns, zs)
    egs = [jnp.exp(gc) for gc in gcols]
    rhss = _each(lambda v, b, kb, eg: jnp.concatenate([v * b, kb * eg], axis=1), vs, betas, kbs, egs)
    uws = _each(lambda r, n: r + _dot(n.astype(BF16), r.astype(BF16)), rhss, ns)
    return ([x[:, :LANES] for x in uws], [x[:, LANES:] for x in uws], qks,
            _each(lambda q, eg: q * eg, qs, egs))


def _gdn_prompt_kernel(z_ref, dz_ref, ab_ref, cw_ref, alog_ref, dtb_ref, ng_ref,
                       y_ref, sout_ref, xp_ref, s_ref, *, nc):
    c = CHUNK
    r = nc * c
    pad = SUBLANES

    @pl.when(pl.program_id(1) == 0)
    def _():
        xp_ref[0:pad, :] = jnp.zeros((pad, 3 * DN_WIDTH), F32)
        s_ref[...] = jnp.zeros_like(s_ref)

    xp_ref[pad:pad + r, :] = z_ref[...]
    cw = cw_ref[...]
    conv = xp_ref[pad:pad + r, :] * cw[DN_CONV - 1:DN_CONV]
    for j in range(DN_CONV - 1):
        o = pad - (DN_CONV - 1) + j
        conv = conv + xp_ref[o:o + r, :] * cw[j:j + 1]
    xp_ref[0:pad, :] = xp_ref[r:r + pad, :]
    conv = _silu(conv)
    g, beta = _gates(ab_ref[...], alog_ref[...], dtb_ref[...])
    rowi = lax.broadcasted_iota(jnp.int32, (c, c), 0)
    coli = lax.broadcasted_iota(jnp.int32, (c, c), 1)
    tril = rowi >= coli
    merges = _merge_masks(c, c)
    ng = ng_ref[...]
    heads = range(DN_HEADS)
    probs = [(ci, h) for ci in range(nc) for h in heads]
    rows = lambda ci: slice(ci * c, (ci + 1) * c)
    lanes = lambda part, h: slice(part * DN_WIDTH + h * LANES, part * DN_WIDTH + (h + 1) * LANES)
    gcs = [_cumsum_rows(g[rows(ci)], c) for ci in range(nc)]
    gcts = [gc.T for gc in gcs]
    qs = [_l2n(conv[rows(ci), lanes(0, h)]) * (DN_HEAD_DIM ** -0.5) for ci, h in probs]
    ks = [_l2n(conv[rows(ci), lanes(1, h)]) for ci, h in probs]
    vs = [conv[rows(ci), lanes(2, h)] for ci, h in probs]
    gcols = [_lane_bcast(gcs[ci], h) for ci, h in probs]
    grows = [gcts[ci][h:h + 1, :] for ci, h in probs]
    betas = [_lane_bcast(beta[rows(ci)], DN_HEADS + h) for ci, h in probs]
    us, ws, qks, qds = _chunk_local(qs, ks, vs, gcols, grows, betas, tril, merges)
    glasts = [gc[c - 1:c, :] for gc in gcols]
    kdts = _each(lambda k, gl, gc: (k * jnp.exp(gl - gc)).T, ks, glasts, gcols)
    wqs = _each(lambda w, qd: jnp.concatenate([w, qd], axis=0).astype(BF16), ws, qds)
    qkks = _each(lambda qk, kdt: jnp.concatenate([qk, kdt], axis=0).astype(BF16), qks, kdts)
    ss = [s_ref[h] for h in heads]
    for ci in range(nc):
        pr = [ci * DN_HEADS + h for h in heads]
        r2s = [_dot(wqs[p], s.astype(BF16)) for p, s in zip(pr, ss)]
        vnews = [us[p] - r2[:c] for p, r2 in zip(pr, r2s)]
        r3s = [_dot(qkks[p], vn.astype(BF16)) for p, vn in zip(pr, vnews)]
        ss = [s * jnp.exp(glasts[p]) + r3[c:] for p, s, r3 in zip(pr, ss, r3s)]
        for h in heads:
            o = r2s[h][c:] + r3s[h][:c]
            y_ref[rows(ci), lanes(0, h)] = (_rms(o, ng) * _silu(dz_ref[rows(ci), lanes(0, h)])).astype(BF16)
    for h in heads:
        s_ref[h] = ss[h]

    @pl.when(pl.program_id(1) == pl.num_programs(1) - 1)
    def _():
        sout_ref[0] = s_ref[...]


def _gdn_prompt(dqkv, dz, ab, cw, alog, dtb, ng, nseq, seqlen, nc=4):
    r = nc * CHUNK
    ns = seqlen // r
    row = lambda w: pl.BlockSpec((r, w), lambda b, s: (b * ns + s, 0))
    return pl.pallas_call(
        functools.partial(_gdn_prompt_kernel, nc=nc),
        grid=(nseq, ns),
        in_specs=[row(3 * DN_WIDTH), row(DN_WIDTH), row(LANES), _resident(cw.shape),
                  _resident((1, LANES)), _resident((1, LANES)), _resident((1, LANES))],
        out_specs=[row(DN_WIDTH),
                   pl.BlockSpec((1, DN_HEADS, DN_HEAD_DIM, DN_HEAD_DIM), lambda b, s: (b, 0, 0, 0))],
        out_shape=[jax.ShapeDtypeStruct((nseq * seqlen, DN_WIDTH), BF16),
                   jax.ShapeDtypeStruct((nseq, DN_HEADS, DN_HEAD_DIM, DN_HEAD_DIM), F32)],
        scratch_shapes=[pltpu.VMEM((r + SUBLANES, 3 * DN_WIDTH), F32),
                        pltpu.VMEM((DN_HEADS, DN_HEAD_DIM, DN_HEAD_DIM), F32)],
        compiler_params=_cparams("parallel", "arbitrary"),
        name="gdn_prompt",
    )(dqkv, dz, ab, cw, alog, dtb, ng)


def _gdn_sample_kernel(z_ref, dz_ref, ab_ref, s0_ref, cw_ref, alog_ref, dtb_ref, ng_ref,
                       y_ref, sout_ref, *, t):
    c = CHUNK
    sr = SAMPLE_ROWS
    nb = c // sr
    off = SUBLANES - (DN_CONV - 1)
    cw = cw_ref[...]
    conv = z_ref[0, off + 3:off + 3 + c, :] * cw[3:4]
    for j in range(DN_CONV - 1):
        conv = conv + z_ref[0, off + j:off + j + c, :] * cw[j:j + 1]
    conv = _silu(conv)
    live = (lax.broadcasted_iota(jnp.int32, (c, LANES), 0) % sr) < t
    g, beta = _gates(ab_ref[...], alog_ref[...], dtb_ref[...])
    g = jnp.where(live, g, 0.0)
    beta = jnp.where(live, beta, 0.0)
    gc = _cumsum_rows(g, sr)
    gct = gc.T
    rowi = lax.broadcasted_iota(jnp.int32, (c, c), 0)
    coli = lax.broadcasted_iota(jnp.int32, (c, c), 1)
    tril = ((rowi // sr) == (coli // sr)) & (rowi >= coli)
    merges = _merge_masks(c, pl.next_power_of_2(t))
    ng = ng_ref[...]
    rowb = lax.broadcasted_iota(jnp.int32, (c, LANES), 0) // sr
    heads = range(DN_HEADS)
    lanes = lambda part, h: slice(part * DN_WIDTH + h * LANES, part * DN_WIDTH + (h + 1) * LANES)
    ks = [_l2n(conv[:, lanes(1, h)]) for h in heads]
    gcols = [_lane_bcast(gc, h) for h in heads]
    us, ws, qks, qds = _chunk_local(
        [_l2n(conv[:, lanes(0, h)]) * (DN_HEAD_DIM ** -0.5) for h in heads], ks,
        [conv[:, lanes(2, h)] for h in heads], gcols, [gct[h:h + 1, :] for h in heads],
        [_lane_bcast(beta, DN_HEADS + h) for h in heads], tril, merges)
    for h in heads:
        hs = lanes(0, h)
        k, gcol, u, w, qk, qd = ks[h], gcols[h], us[h], ws[h], qks[h], qds[h]
        wq = jnp.concatenate([w, qd], axis=1).astype(BF16)
        vnews, qss, sbs = [], [], []
        for b in range(nb):
            rs = slice(b * sr, (b + 1) * sr)
            s = s0_ref[b, h]
            sbs.append(s)
            r2 = _dot(jnp.concatenate([wq[rs, :LANES], wq[rs, LANES:]], axis=0), s.astype(BF16))
            vnews.append(u[rs] - r2[:sr])
            qss.append(r2[sr:])
        vnew = jnp.concatenate(vnews, axis=0)
        o = jnp.concatenate(qss, axis=0) + _dot(qk.astype(BF16), vnew.astype(BF16))
        y_ref[:, hs] = _rms(o, ng) * _silu(dz_ref[:, hs])
        glast = jnp.concatenate(
            [jnp.broadcast_to(gcol[b * sr + sr - 1:b * sr + sr, :], (sr, LANES)) for b in range(nb)], axis=0)
        kdt = (k * jnp.exp(glast - gcol)).T.astype(BF16)
        for b in range(nb):
            vb = jnp.where(rowb == b, vnew, 0.0).astype(BF16)
            sout_ref[b, h] = sbs[b] * jnp.exp(glast[b * sr:b * sr + 1, :]) + _dot(kdt, vb)


def _gdn_sample(zg, dz, ab, s0, cw, alog, dtb, ng, t):
    ngrp = zg.shape[0]
    nb = CHUNK // SAMPLE_ROWS
    row = lambda w: pl.BlockSpec((CHUNK, w), lambda i: (i, 0))
    st = pl.BlockSpec((nb, DN_HEADS, DN_HEAD_DIM, DN_HEAD_DIM), lambda i: (i, 0, 0, 0))
    return pl.pallas_call(
        functools.partial(_gdn_sample_kernel, t=t),
        grid=(ngrp,),
        in_specs=[pl.BlockSpec((1, CHUNK + SUBLANES, 3 * DN_WIDTH), lambda i: (i, 0, 0)),
                  row(DN_WIDTH), row(LANES), st, _resident(cw.shape),
                  _resident((1, LANES)), _resident((1, LANES)), _resident((1, LANES))],
        out_specs=[row(DN_WIDTH), st],
        out_shape=[jax.ShapeDtypeStruct((ngrp * CHUNK, DN_WIDTH), F32),
                   jax.ShapeDtypeStruct(s0.shape, F32)],
        compiler_params=_cparams("parallel"),
        name="gdn_sample",
    )(zg, dz, ab, s0, cw, alog, dtb, ng)


def _merge_kernel(x_ref, *rest, attend):
    if attend:
        sink_ref, q_ref, kp_ref, kc_ref, vp_ref, vc_ref, *rest = rest
        ya = _swa_tile(sink_ref, q_ref, jnp.concatenate([kp_ref[...], kc_ref[...]], axis=0),
                       jnp.concatenate([vp_ref[...], vc_ref[...]], axis=0), pl.program_id(1) == 0)
    else:
        ya_ref, *rest = rest
        ya = ya_ref[...]
    yb_ref, ga_ref, gb_ref, wa_ref, wb_ref, wo_ref, g_ref, wq_ref, h_ref, hq_ref = rest
    mix = (jax.nn.sigmoid(ga_ref[...]) * _dot(ya, wa_ref[...])
           + jax.nn.sigmoid(gb_ref[...]) * _dot(yb_ref[...], wb_ref[...]))
    h = x_ref[...] + _dot(mix.astype(BF16), wo_ref[...])
    h_ref[...] = h
    hq_ref[...] = _dot(_rms(h, g_ref[...]).astype(BF16), wq_ref[...]).astype(BF16)


def _merge(x, attn, yb, ga, gb, wa, wb, wo, g, wq, nseq, tm):
    m = x.shape[0]
    nt = m // nseq // tm
    row = lambda w: pl.BlockSpec((tm, w), lambda s, j: (s * nt + j, 0))
    attend = isinstance(attn, tuple)
    if attend:
        sinks, q, k, v = attn
        per = tm // WINDOW
        prev = pl.BlockSpec((WINDOW, A_KV_WIDTH), lambda s, j: (jnp.maximum((s * nt + j) * per - 1, 0), 0))
        attn_args = [sinks, q, k, k, v, v]
        attn_specs = [pl.BlockSpec(memory_space=pltpu.SMEM), row(A_WIDTH), prev, row(A_KV_WIDTH),
                      prev, row(A_KV_WIDTH)]
    else:
        attn_args, attn_specs = [attn], [row(A_WIDTH)]
    return pl.pallas_call(
        functools.partial(_merge_kernel, attend=attend),
        grid=(nseq, nt),
        in_specs=[row(D_MODEL), *attn_specs, row(DN_WIDTH), row(D_MODEL), row(D_MODEL),
                  _resident(wa.shape), _resident(wb.shape), _resident(wo.shape),
                  _resident((1, D_MODEL)), _resident(wq.shape)],
        out_specs=[row(D_MODEL), row(X_WIDTH)],
        out_shape=[jax.ShapeDtypeStruct((m, D_MODEL), F32), jax.ShapeDtypeStruct((m, X_WIDTH), BF16)],
        compiler_params=_cparams("parallel", "parallel"),
        name="merge",
    )(x, *attn_args, yb, ga, gb, wa, wb, wo, g, wq)


def _memkv_kernel(m_ref, g_ref, w_ref, k_ref, v_ref):
    z = _dot(_rms(m_ref[...], g_ref[...]).astype(BF16), w_ref[...])
    k_ref[...] = z[:, :X_WIDTH]
    v_ref[...] = z[:, X_WIDTH:]


def _memkv(mem, g, w, tm):
    m = mem.shape[0]
    row = lambda wd: pl.BlockSpec((tm, wd), lambda i: (i, 0))
    return pl.pallas_call(
        _memkv_kernel,
        grid=(m // tm,),
        in_specs=[row(D_MODEL), _resident((1, D_MODEL)), _resident(w.shape)],
        out_specs=[row(X_WIDTH), row(X_WIDTH)],
        out_shape=[jax.ShapeDtypeStruct((m, X_WIDTH), F32)] * 2,
        compiler_params=_cparams("parallel"),
        name="mem_kv",
    )(mem, g, w)


def _softmax_rows(s):
    e = jnp.exp(s - jnp.max(s, axis=-1, keepdims=True))
    return e * (1.0 / jnp.sum(e, axis=-1, keepdims=True))


def _xattn_sample_kernel(hq_ref, mk_ref, mv_ref, o_ref, *, nb):
    rows = hq_ref.shape[1]
    nmh = mk_ref.shape[1]
    qh = lax.broadcasted_iota(jnp.int32, (X_HEADS * rows, nmh), 0) // rows
    mh = lax.broadcasted_iota(jnp.int32, (X_HEADS * rows, nmh), 1) % X_HEADS
    same = qh == mh
    for b in range(nb):
        q = jnp.concatenate([hq_ref[b, :, h * LANES:(h + 1) * LANES] for h in range(X_HEADS)], axis=0)
        s = _dot_nt(q, mk_ref[b].astype(BF16)) * (X_HEAD_DIM ** -0.5)
        p = _softmax_rows(jnp.where(same, s, -jnp.inf))
        o = _dot(p.astype(BF16), mv_ref[b].astype(BF16))
        for h in range(X_HEADS):
            o_ref[b, :, h * LANES:(h + 1) * LANES] = o[h * rows:(h + 1) * rows].astype(BF16)


def _xattn_sample(hq, mk, mv, nb=8):
    bsz, rows, _ = hq.shape
    q = pl.BlockSpec((nb, rows, X_WIDTH), lambda i: (i, 0, 0))
    mem = pl.BlockSpec((nb, mk.shape[1], X_HEAD_DIM), lambda i: (i, 0, 0))
    return pl.pallas_call(
        functools.partial(_xattn_sample_kernel, nb=nb),
        grid=(bsz // nb,),
        in_specs=[q, mem, mem],
        out_specs=q,
        out_shape=jax.ShapeDtypeStruct(hq.shape, BF16),
        compiler_params=_cparams("parallel"),
        name="xattn_sample",
    )(hq, mk, mv)


def _xattn_tile(hq_ref, mk_ref, mv_ref):
    outs = []
    for h in range(X_HEADS):
        hs = slice(h * LANES, (h + 1) * LANES)
        s = _dot_nt(hq_ref[:, hs], mk_ref[:, hs].astype(BF16)) * (X_HEAD_DIM ** -0.5)
        outs.append(_dot(_softmax_rows(s).astype(BF16), mv_ref[:, hs].astype(BF16)).astype(BF16))
    return jnp.concatenate(outs, axis=1)


def _ffn_kernel(h_ref, *rest, stride, padc, fc, attend):
    if attend:
        hq_ref, mk_ref, mv_ref, *rest = rest
        xo = _xattn_tile(hq_ref, mk_ref, mv_ref)
    else:
        xo_ref, *rest = rest
        xo = xo_ref[...]
    cin_ref, wxo_ref, g_ref, wu_ref, wv_ref, cw_ref, wd_ref, fg_ref, y_ref, tail_ref, xp_ref, acc_ref = rest
    tm = h_ref.shape[0]

    @pl.when(pl.program_id(1) == 0)
    def _():
        tail_ref[...] = cin_ref[...]

    h = h_ref[...] + _dot(xo, wxo_ref[...])
    acc_ref[...] = h
    hn = _rms(h, g_ref[...]).astype(BF16)
    for ci in range(D_FF // fc):
        cs = slice(ci * fc, (ci + 1) * fc)
        u = _dot(hn, wu_ref[:, cs])
        gate = _dot(hn, wv_ref[:, cs])
        xp_ref[0:padc, :] = tail_ref[0, :, cs]
        xp_ref[padc:padc + tm, :] = u
        cw = cw_ref[:, cs]
        uc = u * cw[2:3]
        for j in range(FFN_CONV - 1):
            o = padc - (FFN_CONV - 1 - j) * stride
            uc = uc + xp_ref[o:o + tm, :] * cw[j:j + 1]
        tail_ref[0, :, cs] = xp_ref[tm:tm + padc, :]
        acc_ref[...] += _dot((_silu(uc) * gate).astype(BF16), wd_ref[cs, :])
    y_ref[...] = _rms(acc_ref[...], fg_ref[...])


def _ffn(h, xq, mem, cin, wxo, g, wu, wv, cw, wd, fg, tm, stride, fc=D_FF // 2):
    m = h.shape[0]
    nseq, padc, _ = cin.shape
    nt = m // nseq // tm
    row = lambda w: pl.BlockSpec((tm, w), lambda s, j: (s * nt + j, 0))
    car = pl.BlockSpec((1, padc, D_FF), lambda s, j: (s, 0, 0))
    mem = list(mem or ())
    mem_specs = [pl.BlockSpec((a.shape[0] // nseq, X_WIDTH), lambda s, j: (s, 0)) for a in mem]
    return pl.pallas_call(
        functools.partial(_ffn_kernel, stride=stride, padc=padc, fc=fc, attend=bool(mem)),
        grid=(nseq, nt),
        in_specs=[row(D_MODEL), row(X_WIDTH), *mem_specs, car, _resident(wxo.shape), _resident((1, D_MODEL)),
                  _resident(wu.shape), _resident(wv.shape), _resident(cw.shape), _resident(wd.shape),
                  _resident((1, D_MODEL))],
        out_specs=[row(D_MODEL), car],
        out_shape=[jax.ShapeDtypeStruct((m, D_MODEL), F32), jax.ShapeDtypeStruct(cin.shape, F32)],
        scratch_shapes=[pltpu.VMEM((tm + padc, fc), F32), pltpu.VMEM((tm, D_MODEL), F32)],
        compiler_params=_cparams("parallel", "arbitrary"),
        name="ffn",
    )(h, xq, *mem, cin, wxo, g, wu, wv, cw, wd, fg)


def _lane_row(vec, offset=0):
    return jnp.zeros((1, LANES), F32).at[0, offset:offset + vec.shape[0]].set(vec.astype(F32))


def _prep(p):
    w_in = p["w_in"]
    o = 0
    cuts = {}
    for name, wdt in (("qkv", A_WIDTH + 2 * A_KV_WIDTH), ("d", 4 * DN_WIDTH), ("ab", 2 * DN_HEADS),
                      ("g", 2 * D_MODEL)):
        cuts[name] = w_in[:, o:o + wdt]
        o += wdt
    w = {
        "wqkv": cuts["qkv"].astype(BF16),
        "wd": cuts["d"].astype(BF16),
        "wab": jnp.pad(cuts["ab"], ((0, 0), (0, LANES - 2 * DN_HEADS))).astype(BF16),
        "wg": cuts["g"].astype(BF16),
        "norm_mix_g": p["norm_mix_g"].reshape(1, D_MODEL),
        "dn_conv_w": p["dn_conv_w"],
        "alog": _lane_row(p["dn_a_log"]),
        "dtb": _lane_row(p["dn_dt_bias"]),
        "dn_norm_g": p["dn_norm_g"].reshape(1, DN_HEAD_DIM),
        "sinks": p["sinks"].astype(F32),
        "w_br_a": p["w_br_a"].astype(BF16),
        "w_br_b": p["w_br_b"].astype(BF16),
        "w_mix_out": p["w_mix_out"].astype(BF16),
        "norm_x_g": p["norm_x_g"].reshape(1, D_MODEL),
        "w_xq": p["w_xq"].astype(BF16),
        "w_xo": p["w_xo"].astype(BF16),
        "norm_ffn_g": p["norm_ffn_g"].reshape(1, D_MODEL),
        "w_up": p["w_up"][:, :D_FF].astype(BF16),
        "w_gate": p["w_up"][:, D_FF:].astype(BF16),
        "ffn_conv_w": p["ffn_conv_w"],
        "w_down": p["w_down"].astype(BF16),
        "final_norm_g": p["final_norm_g"].reshape(1, D_MODEL),
    }
    return w


def _tile(m, pref):
    return pref if m % pref == 0 else m


def _prompt_layer(x, mem, norm_mem_g, w_xkv, w):
    bsz, seqlen, _ = x.shape
    m = bsz * seqlen
    x2 = x.reshape(m, D_MODEL)
    tm = _tile(seqlen, 512)
    tabs = _rope_tables(jnp.arange(seqlen, dtype=jnp.int32))
    q, k, v, dqkv, dz, ab, ga, gb = _proj(x2, w["norm_mix_g"], tabs, w["wqkv"], w["wd"], w["wab"], w["wg"], tm)
    yb, dn_state = _gdn_prompt(dqkv, dz, ab, w["dn_conv_w"], w["alog"], w["dtb"], w["dn_norm_g"], bsz, seqlen)
    h, hq = _merge(x2, (w["sinks"], q, k, v), yb, ga, gb, w["w_br_a"], w["w_br_b"], w["w_mix_out"],
                   w["norm_x_g"], w["w_xq"], bsz, tm)
    nm = mem.shape[1]
    mk, mv = _memkv(mem.reshape(bsz * nm, D_MODEL), norm_mem_g.reshape(1, D_MODEL), w_xkv.astype(BF16),
                    _tile(bsz * nm, 512))
    cin = jnp.zeros((bsz, SUBLANES, D_FF), F32)
    y, tail = _ffn(h, hq, (mk, mv), cin, w["w_xo"], w["norm_ffn_g"], w["w_up"], w["w_gate"], w["ffn_conv_w"],
                   w["w_down"], w["final_norm_g"], tm, 1)
    wb = min(WINDOW, seqlen)
    new = (
        k.reshape(bsz, seqlen, A_KV_HEADS, A_HEAD_DIM)[:, seqlen - wb:],
        v.reshape(bsz, seqlen, A_KV_HEADS, A_HEAD_DIM)[:, seqlen - wb:],
        dqkv.reshape(bsz, seqlen, 3 * DN_WIDTH)[:, seqlen - (DN_CONV - 1):],
        dn_state,
        mk.reshape(bsz, nm, X_HEADS, X_HEAD_DIM),
        mv.reshape(bsz, nm, X_HEADS, X_HEAD_DIM),
        tail[:, SUBLANES - (FFN_CONV - 1):],
    )
    return y.reshape(bsz, seqlen, D_MODEL), new


def _sample_layer(x, pos0, win_k, win_v, dn_buf, dn_state, mem_k, mem_v, ffn_buf, w):
    bsz, t, _ = x.shape
    m = bsz * t
    sr = SAMPLE_ROWS
    x2 = x.reshape(m, D_MODEL)
    tabs = _rope_tables(jnp.tile(pos0 + jnp.arange(t, dtype=jnp.int32), bsz))
    q, k, v, dqkv, dz, ab, ga, gb = _proj(x2, w["norm_mix_g"], tabs, w["wqkv"], w["wd"], w["wab"], w["wg"], m)

    wlen = win_k.shape[1]
    q5 = q.reshape(bsz, t, A_KV_HEADS, A_HEADS // A_KV_HEADS, A_HEAD_DIM)
    q5 = jnp.transpose(q5, (0, 2, 3, 1, 4))
    eye = jnp.eye(A_KV_HEADS, dtype=BF16)
    q8 = (q5[:, :, :, :, None, :] * eye[None, :, None, None, :, None]).reshape(bsz, A_HEADS * t, A_KV_WIDTH)
    padrows = lambda a: jnp.pad(a.reshape(bsz, t, -1), ((0, 0), (0, sr - t), (0, 0)))
    o8 = _swa_sample(q8, win_k.reshape(bsz, wlen, A_KV_WIDTH), win_v.reshape(bsz, wlen, A_KV_WIDTH),
                     padrows(k).astype(BF16), padrows(v).astype(BF16), w["sinks"], t)
    o8 = o8.reshape(bsz, A_KV_HEADS, A_HEADS // A_KV_HEADS, t, A_KV_HEADS, A_HEAD_DIM)
    o8 = jnp.stack([o8[:, g, :, :, g] for g in range(A_KV_HEADS)], axis=1)
    ya = jnp.transpose(o8, (0, 3, 1, 2, 4)).reshape(m, A_WIDTH).astype(BF16)
    new_wk = jnp.concatenate([win_k, k.reshape(bsz, t, A_KV_HEADS, A_HEAD_DIM)], axis=1)[:, -wlen:]
    new_wv = jnp.concatenate([win_v, v.reshape(bsz, t, A_KV_HEADS, A_HEAD_DIM)], axis=1)[:, -wlen:]

    nb = CHUNK // sr
    hist = jnp.concatenate([dn_buf, dqkv.reshape(bsz, t, 3 * DN_WIDTH)], axis=1)
    rec = jnp.pad(hist, ((0, 0), (0, sr - hist.shape[1]), (0, 0))).reshape(bsz // nb, CHUNK, 3 * DN_WIDTH)
    zg = jnp.pad(rec, ((0, 0), (SUBLANES - (DN_CONV - 1), DN_CONV - 1), (0, 0)))
    yb16, new_s = _gdn_sample(zg, padrows(dz).reshape(bsz * sr, DN_WIDTH), padrows(ab).reshape(bsz * sr, LANES),
                              dn_state, w["dn_conv_w"], w["alog"], w["dtb"], w["dn_norm_g"], t)
    yb = yb16.reshape(bsz, sr, DN_WIDTH)[:, :t].reshape(m, DN_WIDTH).astype(BF16)
    new_dn_buf = hist[:, -(DN_CONV - 1):]

    h, hq = _merge(x2, ya, yb, ga, gb, w["w_br_a"], w["w_br_b"], w["w_mix_out"], w["norm_x_g"], w["w_xq"], 1, m)
    nm = mem_k.shape[1]
    xo = _xattn_sample(padrows(hq), mem_k.reshape(bsz, nm * X_HEADS, X_HEAD_DIM),
                       mem_v.reshape(bsz, nm * X_HEADS, X_HEAD_DIM))
    xo = xo[:, :t]

    tmaj = lambda a: jnp.transpose(a.reshape(bsz, t, -1), (1, 0, 2)).reshape(m, -1)
    cin = jnp.transpose(ffn_buf, (1, 0, 2)).reshape(1, (FFN_CONV - 1) * bsz, D_FF)
    y, tail = _ffn(tmaj(h), tmaj(xo), None, cin, w["w_xo"], w["norm_ffn_g"], w["w_up"], w["w_gate"], w["ffn_conv_w"],
                   w["w_down"], w["final_norm_g"], m, bsz)
    y = jnp.transpose(y.reshape(t, bsz, D_MODEL), (1, 0, 2))
    new_ffn = jnp.transpose(tail.reshape(FFN_CONV - 1, bsz, D_FF), (1, 0, 2))
    return y, (new_wk, new_wv, new_dn_buf, new_s, new_ffn)


def kernel(x_prompt, x_sample, mem_prompt, cache_win_k, cache_win_v, state_dn_conv, state_dn, cache_mem_k, cache_mem_v, state_ffn_conv, norm_mix_g, w_in, dn_conv_w, dn_a_log, dn_dt_bias, dn_norm_g, attn_sinks, w_br_a, w_br_b, w_mix_out, norm_x_g, norm_mem_g, w_xq, w_xkv, w_xo, norm_ffn_g, w_up, ffn_conv_w, w_down, final_norm_g):
    p = {"norm_mix_g": norm_mix_g[0], "w_in": w_in[0], "dn_conv_w": dn_conv_w[0], "dn_a_log": dn_a_log[0],
         "dn_dt_bias": dn_dt_bias[0], "dn_norm_g": dn_norm_g[0], "sinks": attn_sinks[0], "w_br_a": w_br_a[0],
         "w_br_b": w_br_b[0], "w_mix_out": w_mix_out[0], "norm_x_g": norm_x_g[0], "w_xq": w_xq[0],
         "w_xo": w_xo[0], "norm_ffn_g": norm_ffn_g[0], "w_up": w_up[0], "ffn_conv_w": ffn_conv_w[0],
         "w_down": w_down[0], "final_norm_g": final_norm_g}
    w = _prep(p)
    yp, newp = _prompt_layer(x_prompt, mem_prompt, norm_mem_g[0], w_xkv[0], w)
    ys, news = _sample_layer(x_sample, PAST_LEN, cache_win_k[0], cache_win_v[0], state_dn_conv[0], state_dn[0],
                             cache_mem_k[0], cache_mem_v[0], state_ffn_conv[0], w)
    lead = lambda a: a[None]
    p_win_k, p_win_v, p_dn_conv, p_dn_state, p_mem_k, p_mem_v, p_ffn_conv = [lead(a) for a in newp]
    s_win_k, s_win_v, s_dn_conv, s_dn_state, s_ffn_conv = [lead(a) for a in news]
    return (yp, ys, p_win_k, p_win_v, p_dn_conv, p_dn_state, p_mem_k, p_mem_v, p_ffn_conv,
            s_win_k, s_win_v, s_dn_conv, s_dn_state, s_ffn_conv)
```

```python
import functools
import math

import jax
import jax.numpy as jnp
from jax import lax
from jax.experimental import pallas as pl
from jax.experimental.pallas import tpu as pltpu

F32 = jnp.float32
BF16 = jnp.bfloat16

D_MODEL = 1024
A_HEADS = 8
A_KV_HEADS = 2
A_HEAD_DIM = 64
A_WIDTH = 512
A_KV_WIDTH = 128
WINDOW = 128
ROT_DIM = 16
ROPE_THETA = 500000.0
DN_HEADS = 4
DN_HEAD_DIM = 128
DN_WIDTH = 512
DN_CONV = 4
X_HEADS = 4
X_HEAD_DIM = 128
X_WIDTH = 512
D_FF = 2816
FFN_CONV = 3
EPS = 1e-6
PAST_LEN = 16384

LANES = 128
SUBLANES = 8
VMEM_LIMIT = 56 * 1024 * 1024
CHUNK = 128
SAMPLE_ROWS = 16


def _cparams(*sem):
    return pltpu.CompilerParams(dimension_semantics=sem, vmem_limit_bytes=VMEM_LIMIT)


def _resident(shape):
    return pl.BlockSpec(shape, lambda *_: (0,) * len(shape), pipeline_mode=pl.Buffered(1))


def _rms(x, g):
    return x * lax.rsqrt(jnp.mean(x * x, axis=-1, keepdims=True) + EPS) * g


def _dot(a, b):
    return jnp.dot(a, b, preferred_element_type=F32)


def _dot_nt(a, b):
    return lax.dot_general(a, b, (((1,), (1,)), ((), ())), preferred_element_type=F32)


def _silu(x):
    return x * jax.nn.sigmoid(x)


def _rope(seg, c, s1, s2):
    return seg * c + pltpu.roll(seg, LANES - 8, 1) * s1 + pltpu.roll(seg, 8, 1) * s2


def _proj_kernel(x_ref, g_ref, c_ref, s1_ref, s2_ref, wqkv_ref, wd_ref, wab_ref, wg_ref,
                 q_ref, k_ref, v_ref, dqkv_ref, dz_ref, ab_ref, ga_ref, gb_ref):
    xb = _rms(x_ref[...], g_ref[...]).astype(BF16)
    c, s1, s2 = c_ref[...], s1_ref[...], s2_ref[...]
    z = _dot(xb, wqkv_ref[...])
    for i in range(A_WIDTH // LANES):
        sl = slice(i * LANES, (i + 1) * LANES)
        q_ref[:, sl] = _rope(z[:, sl], c, s1, s2).astype(BF16)
    k_ref[...] = _rope(z[:, A_WIDTH:A_WIDTH + LANES], c, s1, s2)
    v_ref[...] = z[:, A_WIDTH + LANES:]
    z = _dot(xb, wd_ref[...])
    dz_ref[...] = z[:, 3 * DN_WIDTH:]
    dqkv_ref[...] = z[:, :3 * DN_WIDTH]
    ab_ref[...] = _dot(xb, wab_ref[...])
    z = _dot(xb, wg_ref[...])
    ga_ref[...] = z[:, :D_MODEL]
    gb_ref[...] = z[:, D_MODEL:]


def _proj(x, g, tabs, wqkv, wd, wab, wg, tm):
    m = x.shape[0]
    nt = tabs[0].shape[0] // tm
    row = lambda w: pl.BlockSpec((tm, w), lambda i: (i, 0))
    tab = pl.BlockSpec((tm, LANES), lambda i: (i % nt, 0))
    widths = (A_WIDTH, LANES, LANES, 3 * DN_WIDTH, DN_WIDTH, LANES, D_MODEL, D_MODEL)
    dts = (BF16, F32, F32, F32, F32, F32, F32, F32)
    return pl.pallas_call(
        _proj_kernel,
        grid=(m // tm,),
        in_specs=[row(D_MODEL), _resident((1, D_MODEL)), tab, tab, tab,
                  _resident(wqkv.shape), _resident(wd.shape), _resident(wab.shape), _resident(wg.shape)],
        out_specs=[row(w) for w in widths],
        out_shape=[jax.ShapeDtypeStruct((m, w), d) for w, d in zip(widths, dts)],
        compiler_params=_cparams("parallel"),
        name="in_proj",
    )(x, g, *tabs, wqkv, wd, wab, wg)


def _rope_tables(pos):
    half = ROT_DIM // 2
    inv = ROPE_THETA ** (-2.0 * jnp.arange(half, dtype=F32) / ROT_DIM)
    ang = pos.astype(F32)[:, None] * inv[None, :]
    c, s = jnp.cos(ang), jnp.sin(ang)
    n = pos.shape[0]
    one = jnp.ones((n, A_HEAD_DIM - ROT_DIM), F32)
    z8 = jnp.zeros((n, half), F32)
    z48 = jnp.zeros((n, A_HEAD_DIM - ROT_DIM), F32)
    ct = jnp.concatenate([c, c, one], axis=1)
    s1 = jnp.concatenate([-s, z8, z48], axis=1)
    s2 = jnp.concatenate([z8, s, z48], axis=1)
    two = lambda t: jnp.concatenate([t, t], axis=1)
    return two(ct), two(s1), two(s2)


def _both_halves(t, lane_lo):
    r = pltpu.roll(t, A_HEAD_DIM, 1)
    return jnp.where(lane_lo, t, r), jnp.where(lane_lo, r, t)


def _sink_softmax(s, valid, sink):
    s = jnp.where(valid, s, -jnp.inf)
    m = jnp.maximum(jnp.max(s, axis=-1, keepdims=True), sink)
    p = jnp.exp(s - m)
    den = jnp.sum(p, axis=-1, keepdims=True) + jnp.exp(sink - m)
    return p * (1.0 / den)


def _swa_tile(sink_ref, q_ref, k, v, first):
    w = WINDOW
    lane_lo_k = lax.broadcasted_iota(jnp.int32, k.shape, 1) < A_HEAD_DIM
    kk = [t.astype(BF16) for t in _both_halves(k, lane_lo_k)]
    vv = [t.astype(BF16) for t in _both_halves(v, lane_lo_k)]
    lane_lo = lax.broadcasted_iota(jnp.int32, (w, LANES), 1) < A_HEAD_DIM
    row = lax.broadcasted_iota(jnp.int32, (4 * w, 2 * w), 0)
    col = lax.broadcasted_iota(jnp.int32, (4 * w, 2 * w), 1)
    d = (row & (w - 1)) + w - col
    band = (d >= 0) & (d < w)
    band_first = band & (col >= jnp.where(first, w, 0))
    hrow = lax.broadcasted_iota(jnp.int32, (4 * w, 1), 0) // w
    sinks = []
    for g in range(A_KV_HEADS):
        sink = jnp.zeros((4 * w, 1), F32)
        for j in range(4):
            sink = jnp.where(hrow == j, sink_ref[4 * g + j], sink)
        sinks.append(sink)
    zero = jnp.zeros((), BF16)
    blocks = []
    for b in range(q_ref.shape[0] // w):
        segs = []
        for g in range(A_KV_HEADS):
            parts = []
            for sgm in range(2):
                seg = q_ref[b * w:(b + 1) * w, (2 * g + sgm) * LANES:(2 * g + sgm + 1) * LANES]
                parts += [jnp.where(lane_lo, seg, zero), jnp.where(lane_lo, zero, seg)]
            qs = jnp.concatenate(parts, axis=0)
            s = _dot_nt(qs, kk[g][b * w:(b + 2) * w]) * (A_HEAD_DIM ** -0.5)
            p = _sink_softmax(s, band_first if b == 0 else band, sinks[g]).astype(BF16)
            o = _dot(p, vv[g][b * w:(b + 2) * w])
            segs += [jnp.where(lane_lo, o[(2 * sgm) * w:(2 * sgm + 1) * w],
                               o[(2 * sgm + 1) * w:(2 * sgm + 2) * w]).astype(BF16) for sgm in range(2)]
        blocks.append(jnp.concatenate(segs, axis=1))
    return jnp.concatenate(blocks, axis=0)


def _swa_sample_kernel(sink_ref, q_ref, ck_ref, cv_ref, kn_ref, vn_ref, o_ref, *, nb, t):
    w = ck_ref.shape[1]
    nq = q_ref.shape[1]
    rows = nq // A_HEADS
    r = lax.broadcasted_iota(jnp.int32, (nq, w), 0)
    c = lax.broadcasted_iota(jnp.int32, (nq, w), 1)
    tq = r % rows
    valid_c = c > tq
    rn = lax.broadcasted_iota(jnp.int32, (nq, SAMPLE_ROWS), 0) % rows
    cn = lax.broadcasted_iota(jnp.int32, (nq, SAMPLE_ROWS), 1)
    valid_n = (cn <= rn) & (cn < t)
    hrow = lax.broadcasted_iota(jnp.int32, (nq, 1), 0) // rows
    sink = jnp.zeros((nq, 1), F32)
    for h in range(A_HEADS):
        sink = jnp.where(hrow == h, sink_ref[h], sink)
    seqs = range(nb)
    scale = A_HEAD_DIM ** -0.5
    qs = [q_ref[b] for b in seqs]
    scs = [jnp.where(valid_c, _dot_nt(qs[b], ck_ref[b].astype(BF16)) * scale, -jnp.inf) for b in seqs]
    sns = [jnp.where(valid_n, _dot_nt(qs[b], kn_ref[b]) * scale, -jnp.inf) for b in seqs]
    ms = [jnp.maximum(jnp.maximum(jnp.max(sc, -1, keepdims=True), jnp.max(sn, -1, keepdims=True)), sink)
          for sc, sn in zip(scs, sns)]
    pcs = [jnp.exp(sc - m) for sc, m in zip(scs, ms)]
    pns = [jnp.exp(sn - m) for sn, m in zip(sns, ms)]
    invs = [1.0 / (jnp.sum(pc, -1, keepdims=True) + jnp.sum(pn, -1, keepdims=True) + jnp.exp(sink - m))
            for pc, pn, m in zip(pcs, pns, ms)]
    for b in seqs:
        o_ref[b] = (_dot((pcs[b] * invs[b]).astype(BF16), cv_ref[b].astype(BF16))
                    + _dot((pns[b] * invs[b]).astype(BF16), vn_ref[b]))


def _swa_sample(q8, ck, cv, kn, vn, sinks, t, nb=8):
    bsz, nq, _ = q8.shape
    w = ck.shape[1]
    blk = lambda r: pl.BlockSpec((nb, r, LANES), lambda i: (i, 0, 0))
    return pl.pallas_call(
        functools.partial(_swa_sample_kernel, nb=nb, t=t),
        grid=(bsz // nb,),
        in_specs=[pl.BlockSpec(memory_space=pltpu.SMEM), blk(nq), blk(w), blk(w),
                  blk(SAMPLE_ROWS), blk(SAMPLE_ROWS)],
        out_specs=blk(nq),
        out_shape=jax.ShapeDtypeStruct((bsz, nq, LANES), F32),
        compiler_params=_cparams("parallel"),
        name="swa_sample",
    )(sinks, q8, ck, cv, kn, vn)


def _lane_bcast(x, lane):
    return jnp.broadcast_to(x[:, lane:lane + 1], (x.shape[0], LANES))


def _cumsum_rows(x, block):
    row = lax.broadcasted_iota(jnp.int32, x.shape, 0) % block
    sh = 1
    while sh < block:
        x = x + jnp.where(row >= sh, pltpu.roll(x, sh, 0), 0.0)
        sh *= 2
    return x


def _l2n(t):
    return t * lax.rsqrt(jnp.sum(t * t, axis=-1, keepdims=True) + EPS)


def _gates(ab, alog, dtb):
    x = ab + dtb
    sp = jnp.maximum(x, 0.0) + jnp.log1p(jnp.exp(-jnp.abs(x)))
    return -jnp.exp(alog) * sp, jax.nn.sigmoid(ab)


def _merge_masks(c, top):
    rowi = lax.broadcasted_iota(jnp.int32, (c, c), 0)
    coli = lax.broadcasted_iota(jnp.int32, (c, c), 1)
    masks = []
    s = 1
    while s < top:
        rb, cb = rowi // s, coli // s
        masks.append(((rb // 2) == (cb // 2)) & ((rb % 2) == 1) & ((cb % 2) == 0))
        s *= 2
    return masks


def _each(f, *lists):
    return [f(*t) for t in zip(*lists)]


def _chunk_local(qs, ks, vs, gcols, grows, betas, tril, merges):
    c = qs[0].shape[0]
    decays = _each(lambda gc, gr: jnp.exp(jnp.minimum(gc - gr, 0.0)), gcols, grows)
    kbs = _each(lambda k, b: k * b, ks, betas)
    ms = _each(lambda q, kb, k: _dot_nt(jnp.concatenate([q, kb], axis=0).astype(BF16), k.astype(BF16)),
               qs, kbs, ks)
    qks = _each(lambda m, d: jnp.where(tril, m[:c] * d, 0.0), ms, decays)
    a = _each(lambda m, d: m[c:] * d, ms, decays)
    ns = [-jnp.where(merges[0], x, 0.0) if merges else jnp.zeros_like(x) for x in a]
    for off in merges[1:]:
        aos = [jnp.where(off, x, 0.0) for x in a]
        zs = _each(lambda ao, n: ao + _dot(ao.astype(BF16), n.astype(BF16)), aos, ns)
        ns = _each(lambda n, z: n - z - _dot(n.astype(BF16), z.astype(BF16)), ns, zs)
    egs = [jnp.exp(gc) for gc in gcols]
    rhss = _each(lambda v, b, kb, eg: jnp.concatenate([v * b, kb * eg], axis=1), vs, betas, kbs, egs)
    uws = _each(lambda r, n: r + _dot(n.astype(BF16), r.astype(BF16)), rhss, ns)
    return ([x[:, :LANES] for x in uws], [x[:, LANES:] for x in uws], qks,
            _each(lambda q, eg: q * eg, qs, egs))


def _gdn_prompt_kernel(z_ref, dz_ref, ab_ref, cw_ref, alog_ref, dtb_ref, ng_ref,
                       y_ref, sout_ref, xp_ref, s_ref, *, nc):
    c = CHUNK
    r = nc * c
    pad = SUBLANES

    @pl.when(pl.program_id(1) == 0)
    def _():
        xp_ref[0:pad, :] = jnp.zeros((pad, 3 * DN_WIDTH), F32)
        s_ref[...] = jnp.zeros_like(s_ref)

    xp_ref[pad:pad + r, :] = z_ref[...]
    cw = cw_ref[...]
    conv = xp_ref[pad:pad + r, :] * cw[DN_CONV - 1:DN_CONV]
    for j in range(DN_CONV - 1):
        o = pad - (DN_CONV - 1) + j
        conv = conv + xp_ref[o:o + r, :] * cw[j:j + 1]
    xp_ref[0:pad, :] = xp_ref[r:r + pad, :]
    conv = _silu(conv)
    g, beta = _gates(ab_ref[...], alog_ref[...], dtb_ref[...])
    rowi = lax.broadcasted_iota(jnp.int32, (c, c), 0)
    coli = lax.broadcasted_iota(jnp.int32, (c, c), 1)
    tril = rowi >= coli
    merges = _merge_masks(c, c)
    ng = ng_ref[...]
    heads = range(DN_HEADS)
    probs = [(ci, h) for ci in range(nc) for h in heads]
    rows = lambda ci: slice(ci * c, (ci + 1) * c)
    lanes = lambda part, h: slice(part * DN_WIDTH + h * LANES, part * DN_WIDTH + (h + 1) * LANES)
    gcs = [_cumsum_rows(g[rows(ci)], c) for ci in range(nc)]
    gcts = [gc.T for gc in gcs]
    qs = [_l2n(conv[rows(ci), lanes(0, h)]) * (DN_HEAD_DIM ** -0.5) for ci, h in probs]
    ks = [_l2n(conv[rows(ci), lanes(1, h)]) for ci, h in probs]
    vs = [conv[rows(ci), lanes(2, h)] for ci, h in probs]
    gcols = [_lane_bcast(gcs[ci], h) for ci, h in probs]
    grows = [gcts[ci][h:h + 1, :] for ci, h in probs]
    betas = [_lane_bcast(beta[rows(ci)], DN_HEADS + h) for ci, h in probs]
    us, ws, qks, qds = _chunk_local(qs, ks, vs, gcols, grows, betas, tril, merges)
    glasts = [gc[c - 1:c, :] for gc in gcols]
    kdts = _each(lambda k, gl, gc: (k * jnp.exp(gl - gc)).T, ks, glasts, gcols)
    wqs = _each(lambda w, qd: jnp.concatenate([w, qd], axis=0).astype(BF16), ws, qds)
    qkks = _each(lambda qk, kdt: jnp.concatenate([qk, kdt], axis=0).astype(BF16), qks, kdts)
    ss = [s_ref[h] for h in heads]
    for ci in range(nc):
        pr = [ci * DN_HEADS + h for h in heads]
        r2s = [_dot(wqs[p], s.astype(BF16)) for p, s in zip(pr, ss)]
        vnews = [us[p] - r2[:c] for p, r2 in zip(pr, r2s)]
        r3s = [_dot(qkks[p], vn.astype(BF16)) for p, vn in zip(pr, vnews)]
        ss = [s * jnp.exp(glasts[p]) + r3[c:] for p, s, r3 in zip(pr, ss, r3s)]
        for h in heads:
            o = r2s[h][c:] + r3s[h][:c]
            y_ref[rows(ci), lanes(0, h)] = (_rms(o, ng) * _silu(dz_ref[rows(ci), lanes(0, h)])).astype(BF16)
    for h in heads:
        s_ref[h] = ss[h]

    @pl.when(pl.program_id(1) == pl.num_programs(1) - 1)
    def _():
        sout_ref[0] = s_ref[...]


def _gdn_prompt(dqkv, dz, ab, cw, alog, dtb, ng, nseq, seqlen, nc=4):
    r = nc * CHUNK
    ns = seqlen // r
    row = lambda w: pl.BlockSpec((r, w), lambda b, s: (b * ns + s, 0))
    return pl.pallas_call(
        functools.partial(_gdn_prompt_kernel, nc=nc),
        grid=(nseq, ns),
        in_specs=[row(3 * DN_WIDTH), row(DN_WIDTH), row(LANES), _resident(cw.shape),
                  _resident((1, LANES)), _resident((1, LANES)), _resident((1, LANES))],
        out_specs=[row(DN_WIDTH),
                   pl.BlockSpec((1, DN_HEADS, DN_HEAD_DIM, DN_HEAD_DIM), lambda b, s: (b, 0, 0, 0))],
        out_shape=[jax.ShapeDtypeStruct((nseq * seqlen, DN_WIDTH), BF16),
                   jax.ShapeDtypeStruct((nseq, DN_HEADS, DN_HEAD_DIM, DN_HEAD_DIM), F32)],
        scratch_shapes=[pltpu.VMEM((r + SUBLANES, 3 * DN_WIDTH), F32),
                        pltpu.VMEM((DN_HEADS, DN_HEAD_DIM, DN_HEAD_DIM), F32)],
        compiler_params=_cparams("parallel", "arbitrary"),
        name="gdn_prompt",
    )(dqkv, dz, ab, cw, alog, dtb, ng)


def _gdn_sample_kernel(buf_ref, x_ref, dz_ref, ab_ref, s0_ref, cw_ref, alog_ref, dtb_ref, ng_ref,
                       y_ref, sout_ref, xp_ref, ab16_ref):
    t = x_ref.shape[1]
    c = CHUNK
    sr = SAMPLE_ROWS
    nb = c // sr
    pad = SUBLANES
    hist = DN_CONV - 1
    xp_ref[...] = jnp.zeros_like(xp_ref)
    ab16_ref[...] = jnp.zeros_like(ab16_ref)
    for b in range(nb):
        xp_ref[pad + b * sr - hist:pad + b * sr, :] = buf_ref[b]
        xp_ref[pad + b * sr:pad + b * sr + t, :] = x_ref[b]
        ab16_ref[b * sr:b * sr + t, :] = ab_ref[b]
    cw = cw_ref[...]
    conv = xp_ref[pad:pad + c, :] * cw[hist:hist + 1]
    for j in range(hist):
        conv = conv + xp_ref[pad - hist + j:pad - hist + j + c, :] * cw[j:j + 1]
    conv = _silu(conv)
    live = (lax.broadcasted_iota(jnp.int32, (c, LANES), 0) % sr) < t
    g, beta = _gates(ab16_ref[...], alog_ref[...], dtb_ref[...])
    g = jnp.where(live, g, 0.0)
    beta = jnp.where(live, beta, 0.0)
    gc = _cumsum_rows(g, sr)
    gct = gc.T
    rowi = lax.broadcasted_iota(jnp.int32, (c, c), 0)
    coli = lax.broadcasted_iota(jnp.int32, (c, c), 1)
    tril = ((rowi // sr) == (coli // sr)) & (rowi >= coli)
    merges = _merge_masks(c, pl.next_power_of_2(t))
    ng = ng_ref[...]
    rowb = lax.broadcasted_iota(jnp.int32, (c, LANES), 0) // sr
    heads = range(DN_HEADS)
    lanes = lambda part, h: slice(part * DN_WIDTH + h * LANES, part * DN_WIDTH + (h + 1) * LANES)
    ks = [_l2n(conv[:, lanes(1, h)]) for h in heads]
    gcols = [_lane_bcast(gc, h) for h in heads]
    us, ws, qks, qds = _chunk_local(
        [_l2n(conv[:, lanes(0, h)]) * (DN_HEAD_DIM ** -0.5) for h in heads], ks,
        [conv[:, lanes(2, h)] for h in heads], gcols, [gct[h:h + 1, :] for h in heads],
        [_lane_bcast(beta, DN_HEADS + h) for h in heads], tril, merges)
    for h in heads:
        hs = lanes(0, h)
        k, gcol, u, w, qk, qd = ks[h], gcols[h], us[h], ws[h], qks[h], qds[h]
        wq = jnp.concatenate([w, qd], axis=1).astype(BF16)
        vnews, qss, sbs = [], [], []
        for b in range(nb):
            rs = slice(b * sr, (b + 1) * sr)
            s = s0_ref[b, h]
            sbs.append(s)
            r2 = _dot(jnp.concatenate([wq[rs, :LANES], wq[rs, LANES:]], axis=0), s.astype(BF16))
            vnews.append(u[rs] - r2[:sr])
            qss.append(r2[sr:])
        vnew = jnp.concatenate(vnews, axis=0)
        o = _rms(jnp.concatenate(qss, axis=0) + _dot(qk.astype(BF16), vnew.astype(BF16)), ng)
        for b in range(nb):
            y_ref[b, :, hs] = o[b * sr:b * sr + t] * _silu(dz_ref[b, :, hs])
        glast = jnp.concatenate(
            [jnp.broadcast_to(gcol[b * sr + sr - 1:b * sr + sr, :], (sr, LANES)) for b in range(nb)], axis=0)
        kdt = (k * jnp.exp(glast - gcol)).T.astype(BF16)
        for b in range(nb):
            vb = jnp.where(rowb == b, vnew, 0.0).astype(BF16)
            sout_ref[b, h] = sbs[b] * jnp.exp(glast[b * sr:b * sr + 1, :]) + _dot(kdt, vb)


def _gdn_sample(buf, x, dz, ab, s0, cw, alog, dtb, ng):
    bsz, t, _ = x.shape
    nb = CHUNK // SAMPLE_ROWS
    seq = lambda a: pl.BlockSpec((nb,) + a.shape[1:], lambda i: (i,) + (0,) * (a.ndim - 1))
    return pl.pallas_call(
        _gdn_sample_kernel,
        grid=(bsz // nb,),
        in_specs=[seq(buf), seq(x), seq(dz), seq(ab), seq(s0), _resident(cw.shape),
                  _resident((1, LANES)), _resident((1, LANES)), _resident((1, LANES))],
        out_specs=[seq(dz), seq(s0)],
        out_shape=[jax.ShapeDtypeStruct(dz.shape, F32), jax.ShapeDtypeStruct(s0.shape, F32)],
        scratch_shapes=[pltpu.VMEM((CHUNK + SUBLANES, 3 * DN_WIDTH), F32), pltpu.VMEM((CHUNK, LANES), F32)],
        compiler_params=_cparams("parallel"),
        name="gdn_sample",
    )(buf, x, dz, ab, s0, cw, alog, dtb, ng)


def _merge_kernel(x_ref, *rest, attend):
    if attend:
        sink_ref, q_ref, kp_ref, kc_ref, vp_ref, vc_ref, *rest = rest
        ya = _swa_tile(sink_ref, q_ref, jnp.concatenate([kp_ref[...], kc_ref[...]], axis=0),
                       jnp.concatenate([vp_ref[...], vc_ref[...]], axis=0), pl.program_id(1) == 0)
    else:
        ya_ref, *rest = rest
        ya = ya_ref[...]
    yb_ref, ga_ref, gb_ref, wa_ref, wb_ref, wo_ref, g_ref, wq_ref, h_ref, hq_ref = rest
    mix = (jax.nn.sigmoid(ga_ref[...]) * _dot(ya.astype(BF16), wa_ref[...])
           + jax.nn.sigmoid(gb_ref[...]) * _dot(yb_ref[...].astype(BF16), wb_ref[...]))
    h = x_ref[...] + _dot(mix.astype(BF16), wo_ref[...])
    h_ref[...] = h
    hq_ref[...] = _dot(_rms(h, g_ref[...]).astype(BF16), wq_ref[...]).astype(BF16)


def _merge(x, attn, yb, ga, gb, wa, wb, wo, g, wq, nseq, tm):
    m = x.shape[0]
    nt = m // nseq // tm
    row = lambda w: pl.BlockSpec((tm, w), lambda s, j: (s * nt + j, 0))
    attend = isinstance(attn, tuple)
    if attend:
        sinks, q, k, v = attn
        per = tm // WINDOW
        prev = pl.BlockSpec((WINDOW, A_KV_WIDTH), lambda s, j: (jnp.maximum((s * nt + j) * per - 1, 0), 0))
        attn_args = [sinks, q, k, k, v, v]
        attn_specs = [pl.BlockSpec(memory_space=pltpu.SMEM), row(A_WIDTH), prev, row(A_KV_WIDTH),
                      prev, row(A_KV_WIDTH)]
    else:
        attn_args, attn_specs = [attn], [row(A_WIDTH)]
    return pl.pallas_call(
        functools.partial(_merge_kernel, attend=attend),
        grid=(nseq, nt),
        in_specs=[row(D_MODEL), *attn_specs, row(DN_WIDTH), row(D_MODEL), row(D_MODEL),
                  _resident(wa.shape), _resident(wb.shape), _resident(wo.shape),
                  _resident((1, D_MODEL)), _resident(wq.shape)],
        out_specs=[row(D_MODEL), row(X_WIDTH)],
        out_shape=[jax.ShapeDtypeStruct((m, D_MODEL), F32), jax.ShapeDtypeStruct((m, X_WIDTH), BF16)],
        compiler_params=_cparams("parallel", "parallel"),
        name="merge",
    )(x, *attn_args, yb, ga, gb, wa, wb, wo, g, wq)


def _memkv_kernel(m_ref, g_ref, w_ref, k_ref, v_ref):
    z = _dot(_rms(m_ref[...], g_ref[...]).astype(BF16), w_ref[...])
    k_ref[...] = z[:, :X_WIDTH]
    v_ref[...] = z[:, X_WIDTH:]


def _memkv(mem, g, w, tm):
    m = mem.shape[0]
    row = lambda wd: pl.BlockSpec((tm, wd), lambda i: (i, 0))
    return pl.pallas_call(
        _memkv_kernel,
        grid=(m // tm,),
        in_specs=[row(D_MODEL), _resident((1, D_MODEL)), _resident(w.shape)],
        out_specs=[row(X_WIDTH), row(X_WIDTH)],
        out_shape=[jax.ShapeDtypeStruct((m, X_WIDTH), F32)] * 2,
        compiler_params=_cparams("parallel"),
        name="mem_kv",
    )(mem, g, w)


def _softmax_rows(s):
    e = jnp.exp(s - jnp.max(s, axis=-1, keepdims=True))
    return e * (1.0 / jnp.sum(e, axis=-1, keepdims=True))


def _xattn_sample_kernel(hq_ref, mk_ref, mv_ref, o_ref, *, nb):
    rows = hq_ref.shape[1]
    nmh = mk_ref.shape[1]
    qh = lax.broadcasted_iota(jnp.int32, (X_HEADS * rows, nmh), 0) // rows
    mh = lax.broadcasted_iota(jnp.int32, (X_HEADS * rows, nmh), 1) % X_HEADS
    same = qh == mh
    seqs = range(nb)
    qs = [jnp.concatenate([hq_ref[b, :, h * LANES:(h + 1) * LANES] for h in range(X_HEADS)], axis=0)
          for b in seqs]
    ss = [_dot_nt(qs[b], mk_ref[b].astype(BF16)) * (X_HEAD_DIM ** -0.5) for b in seqs]
    ps = [_softmax_rows(jnp.where(same, s, -jnp.inf)).astype(BF16) for s in ss]
    os_ = [_dot(ps[b], mv_ref[b].astype(BF16)) for b in seqs]
    for b in seqs:
        for h in range(X_HEADS):
            o_ref[b, :, h * LANES:(h + 1) * LANES] = os_[b][h * rows:(h + 1) * rows].astype(BF16)


def _xattn_sample(hq, mk, mv, nb=8):
    bsz, rows, _ = hq.shape
    q = pl.BlockSpec((nb, rows, X_WIDTH), lambda i: (i, 0, 0))
    mem = pl.BlockSpec((nb, mk.shape[1], X_HEAD_DIM), lambda i: (i, 0, 0))
    return pl.pallas_call(
        functools.partial(_xattn_sample_kernel, nb=nb),
        grid=(bsz // nb,),
        in_specs=[q, mem, mem],
        out_specs=q,
        out_shape=jax.ShapeDtypeStruct(hq.shape, BF16),
        compiler_params=_cparams("parallel"),
        name="xattn_sample",
    )(hq, mk, mv)


def _xattn_tile(hq_ref, mk_ref, mv_ref):
    outs = []
    for h in range(X_HEADS):
        hs = slice(h * LANES, (h + 1) * LANES)
        s = _dot_nt(hq_ref[:, hs], mk_ref[:, hs].astype(BF16)) * (X_HEAD_DIM ** -0.5)
        outs.append(_dot(_softmax_rows(s).astype(BF16), mv_ref[:, hs].astype(BF16)).astype(BF16))
    return jnp.concatenate(outs, axis=1)


def _ffn_kernel(h_ref, *rest, stride, padc, fc, attend):
    if attend:
        hq_ref, mk_ref, mv_ref, *rest = rest
        xo = _xattn_tile(hq_ref, mk_ref, mv_ref)
    else:
        xo_ref, *rest = rest
        xo = xo_ref[...]
    cin_ref, wxo_ref, g_ref, wu_ref, wv_ref, cw_ref, wd_ref, fg_ref, y_ref, tail_ref, xp_ref, acc_ref = rest
    tm = h_ref.shape[0]

    @pl.when(pl.program_id(1) == 0)
    def _():
        tail_ref[...] = cin_ref[...]

    h = h_ref[...] + _dot(xo, wxo_ref[...])
    acc_ref[...] = h
    hn = _rms(h, g_ref[...]).astype(BF16)
    for ci in range(D_FF // fc):
        cs = slice(ci * fc, (ci + 1) * fc)
        u = _dot(hn, wu_ref[:, cs])
        gate = _dot(hn, wv_ref[:, cs])
        xp_ref[0:padc, :] = tail_ref[0, :, cs]
        xp_ref[padc:padc + tm, :] = u
        cw = cw_ref[:, cs]
        uc = u * cw[2:3]
        for j in range(FFN_CONV - 1):
            o = padc - (FFN_CONV - 1 - j) * stride
            uc = uc + xp_ref[o:o + tm, :] * cw[j:j + 1]
        tail_ref[0, :, cs] = xp_ref[tm:tm + padc, :]
        acc_ref[...] += _dot((_silu(uc) * gate).astype(BF16), wd_ref[cs, :])
    y_ref[...] = _rms(acc_ref[...], fg_ref[...])


def _ffn(h, xq, mem, cin, wxo, g, wu, wv, cw, wd, fg, tm, stride, fc=D_FF // 2):
    m = h.shape[0]
    nseq, padc, _ = cin.shape
    nt = m // nseq // tm
    row = lambda w: pl.BlockSpec((tm, w), lambda s, j: (s * nt + j, 0))
    car = pl.BlockSpec((1, padc, D_FF), lambda s, j: (s, 0, 0))
    mem = list(mem or ())
    mem_specs = [pl.BlockSpec((a.shape[0] // nseq, X_WIDTH), lambda s, j: (s, 0)) for a in mem]
    return pl.pallas_call(
        functools.partial(_ffn_kernel, stride=stride, padc=padc, fc=fc, attend=bool(mem)),
        grid=(nseq, nt),
        in_specs=[row(D_MODEL), row(X_WIDTH), *mem_specs, car, _resident(wxo.shape), _resident((1, D_MODEL)),
                  _resident(wu.shape), _resident(wv.shape), _resident(cw.shape), _resident(wd.shape),
                  _resident((1, D_MODEL))],
        out_specs=[row(D_MODEL), car],
        out_shape=[jax.ShapeDtypeStruct((m, D_MODEL), F32), jax.ShapeDtypeStruct(cin.shape, F32)],
        scratch_shapes=[pltpu.VMEM((tm + padc, fc), F32), pltpu.VMEM((tm, D_MODEL), F32)],
        compiler_params=_cparams("parallel", "arbitrary"),
        name="ffn",
    )(h, xq, *mem, cin, wxo, g, wu, wv, cw, wd, fg)


def _lane_row(vec, offset=0):
    return jnp.zeros((1, LANES), F32).at[0, offset:offset + vec.shape[0]].set(vec.astype(F32))


def _prep(p):
    w_in = p["w_in"]
    o = 0
    cuts = {}
    for name, wdt in (("qkv", A_WIDTH + 2 * A_KV_WIDTH), ("d", 4 * DN_WIDTH), ("ab", 2 * DN_HEADS),
                      ("g", 2 * D_MODEL)):
        cuts[name] = w_in[:, o:o + wdt]
        o += wdt
    w = {
        "wqkv": cuts["qkv"].astype(BF16),
        "wd": cuts["d"].astype(BF16),
        "wab": jnp.pad(cuts["ab"], ((0, 0), (0, LANES - 2 * DN_HEADS))).astype(BF16),
        "wg": cuts["g"].astype(BF16),
        "norm_mix_g": p["norm_mix_g"].reshape(1, D_MODEL),
        "dn_conv_w": p["dn_conv_w"],
        "alog": _lane_row(p["dn_a_log"]),
        "dtb": _lane_row(p["dn_dt_bias"]),
        "dn_norm_g": p["dn_norm_g"].reshape(1, DN_HEAD_DIM),
        "sinks": p["sinks"].astype(F32),
        "w_br_a": p["w_br_a"].astype(BF16),
        "w_br_b": p["w_br_b"].astype(BF16),
        "w_mix_out": p["w_mix_out"].astype(BF16),
        "norm_x_g": p["norm_x_g"].reshape(1, D_MODEL),
        "w_xq": p["w_xq"].astype(BF16),
        "w_xo": p["w_xo"].astype(BF16),
        "norm_ffn_g": p["norm_ffn_g"].reshape(1, D_MODEL),
        "w_up": p["w_up"][:, :D_FF].astype(BF16),
        "w_gate": p["w_up"][:, D_FF:].astype(BF16),
        "ffn_conv_w": p["ffn_conv_w"],
        "w_down": p["w_down"].astype(BF16),
        "final_norm_g": p["final_norm_g"].reshape(1, D_MODEL),
    }
    return w


def _tile(m, pref):
    return pref if m % pref == 0 else m


def _prompt_layer(x, mem, norm_mem_g, w_xkv, w):
    bsz, seqlen, _ = x.shape
    m = bsz * seqlen
    x2 = x.reshape(m, D_MODEL)
    tm = _tile(seqlen, 512)
    tabs = _rope_tables(jnp.arange(seqlen, dtype=jnp.int32))
    q, k, v, dqkv, dz, ab, ga, gb = _proj(x2, w["norm_mix_g"], tabs, w["wqkv"], w["wd"], w["wab"], w["wg"], tm)
    yb, dn_state = _gdn_prompt(dqkv, dz, ab, w["dn_conv_w"], w["alog"], w["dtb"], w["dn_norm_g"], bsz, seqlen)
    h, hq = _merge(x2, (w["sinks"], q, k, v), yb, ga, gb, w["w_br_a"], w["w_br_b"], w["w_mix_out"],
                   w["norm_x_g"], w["w_xq"], bsz, tm)
    nm = mem.shape[1]
    mk, mv = _memkv(mem.reshape(bsz * nm, D_MODEL), norm_mem_g.reshape(1, D_MODEL), w_xkv.astype(BF16),
                    _tile(bsz * nm, 512))
    cin = jnp.zeros((bsz, SUBLANES, D_FF), F32)
    y, tail = _ffn(h, hq, (mk, mv), cin, w["w_xo"], w["norm_ffn_g"], w["w_up"], w["w_gate"], w["ffn_conv_w"],
                   w["w_down"], w["final_norm_g"], tm, 1)
    wb = min(WINDOW, seqlen)
    new = (
        k.reshape(bsz, seqlen, A_KV_HEADS, A_HEAD_DIM)[:, seqlen - wb:],
        v.reshape(bsz, seqlen, A_KV_HEADS, A_HEAD_DIM)[:, seqlen - wb:],
        dqkv.reshape(bsz, seqlen, 3 * DN_WIDTH)[:, seqlen - (DN_CONV - 1):],
        dn_state,
        mk.reshape(bsz, nm, X_HEADS, X_HEAD_DIM),
        mv.reshape(bsz, nm, X_HEADS, X_HEAD_DIM),
        tail[:, SUBLANES - (FFN_CONV - 1):],
    )
    return y.reshape(bsz, seqlen, D_MODEL), new


def _sample_layer(x, pos0, win_k, win_v, dn_buf, dn_state, mem_k, mem_v, ffn_buf, w):
    bsz, t, _ = x.shape
    m = bsz * t
    sr = SAMPLE_ROWS
    x2 = x.reshape(m, D_MODEL)
    tabs = _rope_tables(jnp.tile(pos0 + jnp.arange(t, dtype=jnp.int32), bsz))
    q, k, v, dqkv, dz, ab, ga, gb = _proj(x2, w["norm_mix_g"], tabs, w["wqkv"], w["wd"], w["wab"], w["wg"], m)

    wlen = win_k.shape[1]
    q5 = q.reshape(bsz, t, A_KV_HEADS, A_HEADS // A_KV_HEADS, A_HEAD_DIM)
    q5 = jnp.transpose(q5, (0, 2, 3, 1, 4))
    eye = jnp.eye(A_KV_HEADS, dtype=BF16)
    q8 = (q5[:, :, :, :, None, :] * eye[None, :, None, None, :, None]).reshape(bsz, A_HEADS * t, A_KV_WIDTH)
    padrows = lambda a: jnp.pad(a.reshape(bsz, t, -1), ((0, 0), (0, sr - t), (0, 0)))
    o8 = _swa_sample(q8, win_k.reshape(bsz, wlen, A_KV_WIDTH), win_v.reshape(bsz, wlen, A_KV_WIDTH),
                     padrows(k).astype(BF16), padrows(v).astype(BF16), w["sinks"], t)
    o8 = o8.reshape(bsz, A_KV_HEADS, A_HEADS // A_KV_HEADS, t, A_KV_HEADS, A_HEAD_DIM)
    o8 = jnp.stack([o8[:, g, :, :, g] for g in range(A_KV_HEADS)], axis=1)
    ya = jnp.transpose(o8, (0, 3, 1, 2, 4)).reshape(m, A_WIDTH).astype(BF16)
    new_wk = jnp.concatenate([win_k, k.reshape(bsz, t, A_KV_HEADS, A_HEAD_DIM)], axis=1)[:, -wlen:]
    new_wv = jnp.concatenate([win_v, v.reshape(bsz, t, A_KV_HEADS, A_HEAD_DIM)], axis=1)[:, -wlen:]

    dqkv3 = dqkv.reshape(bsz, t, 3 * DN_WIDTH)
    yb, new_s = _gdn_sample(dn_buf, dqkv3, dz.reshape(bsz, t, DN_WIDTH), ab.reshape(bsz, t, LANES), dn_state,
                            w["dn_conv_w"], w["alog"], w["dtb"], w["dn_norm_g"])
    yb = yb.reshape(m, DN_WIDTH)
    new_dn_buf = jnp.concatenate([dn_buf, dqkv3], axis=1)[:, -(DN_CONV - 1):]

    h, hq = _merge(x2, ya, yb, ga, gb, w["w_br_a"], w["w_br_b"], w["w_mix_out"], w["norm_x_g"], w["w_xq"], 1, m)
    nm = mem_k.shape[1]
    xo = _xattn_sample(padrows(hq), mem_k.reshape(bsz, nm * X_HEADS, X_HEAD_DIM),
                       mem_v.reshape(bsz, nm * X_HEADS, X_HEAD_DIM))
    xo = xo[:, :t]

    tmaj = lambda a: jnp.transpose(a.reshape(bsz, t, -1), (1, 0, 2)).reshape(m, -1)
    cin = jnp.transpose(ffn_buf, (1, 0, 2)).reshape(1, (FFN_CONV - 1) * bsz, D_FF)
    y, tail = _ffn(tmaj(h), tmaj(xo), None, cin, w["w_xo"], w["norm_ffn_g"], w["w_up"], w["w_gate"], w["ffn_conv_w"],
                   w["w_down"], w["final_norm_g"], m, bsz)
    y = jnp.transpose(y.reshape(t, bsz, D_MODEL), (1, 0, 2))
    new_ffn = jnp.transpose(tail.reshape(FFN_CONV - 1, bsz, D_FF), (1, 0, 2))
    return y, (new_wk, new_wv, new_dn_buf, new_s, new_ffn)


def kernel(x_prompt, x_sample, mem_prompt, cache_win_k, cache_win_v, state_dn_conv, state_dn, cache_mem_k, cache_mem_v, state_ffn_conv, norm_mix_g, w_in, dn_conv_w, dn_a_log, dn_dt_bias, dn_norm_g, attn_sinks, w_br_a, w_br_b, w_mix_out, norm_x_g, norm_mem_g, w_xq, w_xkv, w_xo, norm_ffn_g, w_up, ffn_conv_w, w_down, final_norm_g):
    p = {"norm_mix_g": norm_mix_g[0], "w_in": w_in[0], "dn_conv_w": dn_conv_w[0], "dn_a_log": dn_a_log[0],
         "dn_dt_bias": dn_dt_bias[0], "dn_norm_g": dn_norm_g[0], "sinks": attn_sinks[0], "w_br_a": w_br_a[0],
         "w_br_b": w_br_b[0], "w_mix_out": w_mix_out[0], "norm_x_g": norm_x_g[0], "w_xq": w_xq[0],
         "w_xo": w_xo[0], "norm_ffn_g": norm_ffn_g[0], "w_up": w_up[0], "ffn_conv_w": ffn_conv_w[0],
         "w_down": w_down[0], "final_norm_g": final_norm_g}
    w = _prep(p)
    yp, newp = _prompt_layer(x_prompt, mem_prompt, norm_mem_g[0], w_xkv[0], w)
    ys, news = _sample_layer(x_sample, PAST_LEN, cache_win_k[0], cache_win_v[0], state_dn_conv[0], state_dn[0],
                             cache_mem_k[0], cache_mem_v[0], state_ffn_conv[0], w)
    lead = lambda a: a[None]
    p_win_k, p_win_v, p_dn_conv, p_dn_state, p_mem_k, p_mem_v, p_ffn_conv = [lead(a) for a in newp]
    s_win_k, s_win_v, s_dn_conv, s_dn_state, s_ffn_conv = [lead(a) for a in news]
    return (yp, ys, p_win_k, p_win_v, p_dn_conv, p_dn_state, p_mem_k, p_mem_v, p_ffn_conv,
            s_win_k, s_win_v, s_dn_conv, s_dn_state, s_ffn_conv)
```

```python
import functools
import math

import jax
import jax.numpy as jnp
from jax import lax
from jax.experimental import pallas as pl
from jax.experimental.pallas import tpu as pltpu

F32 = jnp.float32
BF16 = jnp.bfloat16

D_MODEL = 1024
A_HEADS = 8
A_KV_HEADS = 2
A_HEAD_DIM = 64
A_WIDTH = 512
A_KV_WIDTH = 128
WINDOW = 128
ROT_DIM = 16
ROPE_THETA = 500000.0
DN_HEADS = 4
DN_HEAD_DIM = 128
DN_WIDTH = 512
DN_CONV = 4
X_HEADS = 4
X_HEAD_DIM = 128
X_WIDTH = 512
D_FF = 2816
FFN_CONV = 3
EPS = 1e-6
PAST_LEN = 16384

LANES = 128
SUBLANES = 8
VMEM_LIMIT = 56 * 1024 * 1024
CHUNK = 128
SAMPLE_ROWS = 16
GDN_SAMPLE_ROWS = 8


def _cparams(*sem):
    return pltpu.CompilerParams(dimension_semantics=sem, vmem_limit_bytes=VMEM_LIMIT)


def _resident(shape):
    return pl.BlockSpec(shape, lambda *_: (0,) * len(shape), pipeline_mode=pl.Buffered(1))


def _rms(x, g):
    return x * lax.rsqrt(jnp.mean(x * x, axis=-1, keepdims=True) + EPS) * g


def _dot(a, b):
    return jnp.dot(a, b, preferred_element_type=F32)


def _dot_nt(a, b):
    return lax.dot_general(a, b, (((1,), (1,)), ((), ())), preferred_element_type=F32)


def _silu(x):
    return x * jax.nn.sigmoid(x)


def _rope(seg, c, s1, s2):
    return seg * c + pltpu.roll(seg, LANES - 8, 1) * s1 + pltpu.roll(seg, 8, 1) * s2


def _proj_kernel(x_ref, g_ref, c_ref, s1_ref, s2_ref, wqkv_ref, wd_ref, wab_ref, wg_ref,
                 q_ref, k_ref, v_ref, dqkv_ref, dz_ref, ab_ref, ga_ref, gb_ref, *, dqkv_row0=0):
    tm = x_ref.shape[0]
    xb = _rms(x_ref[...], g_ref[...]).astype(BF16)
    c, s1, s2 = c_ref[...], s1_ref[...], s2_ref[...]
    z = _dot(xb, wqkv_ref[...])
    for i in range(A_WIDTH // LANES):
        sl = slice(i * LANES, (i + 1) * LANES)
        q_ref[:, sl] = _rope(z[:, sl], c, s1, s2).astype(BF16)
    k_ref[...] = _rope(z[:, A_WIDTH:A_WIDTH + LANES], c, s1, s2)
    v_ref[...] = z[:, A_WIDTH + LANES:]
    z = _dot(xb, wd_ref[...])
    dz_ref[...] = z[:, 3 * DN_WIDTH:]
    dqkv_ref[dqkv_row0:dqkv_row0 + tm, :] = z[:, :3 * DN_WIDTH]
    ab_ref[...] = _dot(xb, wab_ref[...])
    z = _dot(xb, wg_ref[...])
    ga_ref[...] = z[:, :D_MODEL]
    gb_ref[...] = z[:, D_MODEL:]


def _proj(x, g, tabs, wqkv, wd, wab, wg, tm):
    m = x.shape[0]
    nt = tabs[0].shape[0] // tm
    row = lambda w: pl.BlockSpec((tm, w), lambda i: (i, 0))
    tab = pl.BlockSpec((tm, LANES), lambda i: (i % nt, 0))
    widths = (A_WIDTH, LANES, LANES, 3 * DN_WIDTH, DN_WIDTH, LANES, D_MODEL, D_MODEL)
    dts = (BF16, F32, F32, F32, F32, F32, F32, F32)
    return pl.pallas_call(
        _proj_kernel,
        grid=(m // tm,),
        in_specs=[row(D_MODEL), _resident((1, D_MODEL)), tab, tab, tab,
                  _resident(wqkv.shape), _resident(wd.shape), _resident(wab.shape), _resident(wg.shape)],
        out_specs=[row(w) for w in widths],
        out_shape=[jax.ShapeDtypeStruct((m, w), d) for w, d in zip(widths, dts)],
        compiler_params=_cparams("parallel"),
        name="in_proj",
    )(x, g, *tabs, wqkv, wd, wab, wg)


def _rope_tables(pos):
    half = ROT_DIM // 2
    inv = ROPE_THETA ** (-2.0 * jnp.arange(half, dtype=F32) / ROT_DIM)
    ang = pos.astype(F32)[:, None] * inv[None, :]
    c, s = jnp.cos(ang), jnp.sin(ang)
    n = pos.shape[0]
    one = jnp.ones((n, A_HEAD_DIM - ROT_DIM), F32)
    z8 = jnp.zeros((n, half), F32)
    z48 = jnp.zeros((n, A_HEAD_DIM - ROT_DIM), F32)
    ct = jnp.concatenate([c, c, one], axis=1)
    s1 = jnp.concatenate([-s, z8, z48], axis=1)
    s2 = jnp.concatenate([z8, s, z48], axis=1)
    two = lambda t: jnp.concatenate([t, t], axis=1)
    return two(ct), two(s1), two(s2)


def _both_halves(t, lane_lo):
    r = pltpu.roll(t, A_HEAD_DIM, 1)
    return jnp.where(lane_lo, t, r), jnp.where(lane_lo, r, t)


def _sink_softmax(s, valid, sink):
    s = jnp.where(valid, s, -jnp.inf)
    m = jnp.maximum(jnp.max(s, axis=-1, keepdims=True), sink)
    p = jnp.exp(s - m)
    den = jnp.sum(p, axis=-1, keepdims=True) + jnp.exp(sink - m)
    return p * (1.0 / den)


def _swa_tile(sink_ref, q_ref, k, v, first):
    w = WINDOW
    lane_lo_k = lax.broadcasted_iota(jnp.int32, k.shape, 1) < A_HEAD_DIM
    kk = [t.astype(BF16) for t in _both_halves(k, lane_lo_k)]
    vv = [t.astype(BF16) for t in _both_halves(v, lane_lo_k)]
    lane_lo = lax.broadcasted_iota(jnp.int32, (w, LANES), 1) < A_HEAD_DIM
    row = lax.broadcasted_iota(jnp.int32, (4 * w, 2 * w), 0)
    col = lax.broadcasted_iota(jnp.int32, (4 * w, 2 * w), 1)
    d = (row & (w - 1)) + w - col
    band = (d >= 0) & (d < w)
    band_first = band & (col >= jnp.where(first, w, 0))
    hrow = lax.broadcasted_iota(jnp.int32, (4 * w, 1), 0) // w
    sinks = []
    for g in range(A_KV_HEADS):
        sink = jnp.zeros((4 * w, 1), F32)
        for j in range(4):
            sink = jnp.where(hrow == j, sink_ref[4 * g + j], sink)
        sinks.append(sink)
    zero = jnp.zeros((), BF16)
    blocks = []
    for b in range(q_ref.shape[0] // w):
        segs = []
        for g in range(A_KV_HEADS):
            parts = []
            for sgm in range(2):
                seg = q_ref[b * w:(b + 1) * w, (2 * g + sgm) * LANES:(2 * g + sgm + 1) * LANES]
                parts += [jnp.where(lane_lo, seg, zero), jnp.where(lane_lo, zero, seg)]
            qs = jnp.concatenate(parts, axis=0)
            s = _dot_nt(qs, kk[g][b * w:(b + 2) * w]) * (A_HEAD_DIM ** -0.5)
            p = _sink_softmax(s, band_first if b == 0 else band, sinks[g]).astype(BF16)
            o = _dot(p, vv[g][b * w:(b + 2) * w])
            segs += [jnp.where(lane_lo, o[(2 * sgm) * w:(2 * sgm + 1) * w],
                               o[(2 * sgm + 1) * w:(2 * sgm + 2) * w]).astype(BF16) for sgm in range(2)]
        blocks.append(jnp.concatenate(segs, axis=1))
    return jnp.concatenate(blocks, axis=0)


def _swa_sample_kernel(sink_ref, q_ref, ck_ref, cv_ref, kn_ref, vn_ref, o_ref, *, nb, t):
    w = ck_ref.shape[1]
    nq = q_ref.shape[1]
    rows = nq // A_HEADS
    r = lax.broadcasted_iota(jnp.int32, (nq, w), 0)
    c = lax.broadcasted_iota(jnp.int32, (nq, w), 1)
    tq = r % rows
    valid_c = c > tq
    rn = lax.broadcasted_iota(jnp.int32, (nq, SAMPLE_ROWS), 0) % rows
    cn = lax.broadcasted_iota(jnp.int32, (nq, SAMPLE_ROWS), 1)
    valid_n = (cn <= rn) & (cn < t)
    hrow = lax.broadcasted_iota(jnp.int32, (nq, 1), 0) // rows
    sink = jnp.zeros((nq, 1), F32)
    for h in range(A_HEADS):
        sink = jnp.where(hrow == h, sink_ref[h], sink)
    seqs = range(nb)
    scale = A_HEAD_DIM ** -0.5
    qs = [q_ref[b] for b in seqs]
    scs = [jnp.where(valid_c, _dot_nt(qs[b], ck_ref[b].astype(BF16)) * scale, -jnp.inf) for b in seqs]
    sns = [jnp.where(valid_n, _dot_nt(qs[b], kn_ref[b]) * scale, -jnp.inf) for b in seqs]
    ms = [jnp.maximum(jnp.maximum(jnp.max(sc, -1, keepdims=True), jnp.max(sn, -1, keepdims=True)), sink)
          for sc, sn in zip(scs, sns)]
    pcs = [jnp.exp(sc - m) for sc, m in zip(scs, ms)]
    pns = [jnp.exp(sn - m) for sn, m in zip(sns, ms)]
    invs = [1.0 / (jnp.sum(pc, -1, keepdims=True) + jnp.sum(pn, -1, keepdims=True) + jnp.exp(sink - m))
            for pc, pn, m in zip(pcs, pns, ms)]
    for b in seqs:
        o_ref[b] = (_dot((pcs[b] * invs[b]).astype(BF16), cv_ref[b].astype(BF16))
                    + _dot((pns[b] * invs[b]).astype(BF16), vn_ref[b]))


def _swa_sample(q8, ck, cv, kn, vn, sinks, t, nb=8):
    bsz, nq, _ = q8.shape
    w = ck.shape[1]
    blk = lambda r: pl.BlockSpec((nb, r, LANES), lambda i: (i, 0, 0))
    return pl.pallas_call(
        functools.partial(_swa_sample_kernel, nb=nb, t=t),
        grid=(bsz // nb,),
        in_specs=[pl.BlockSpec(memory_space=pltpu.SMEM), blk(nq), blk(w), blk(w),
                  blk(SAMPLE_ROWS), blk(SAMPLE_ROWS)],
        out_specs=blk(nq),
        out_shape=jax.ShapeDtypeStruct((bsz, nq, LANES), F32),
        compiler_params=_cparams("parallel"),
        name="swa_sample",
    )(sinks, q8, ck, cv, kn, vn)


def _lane_bcast(x, lane):
    return jnp.broadcast_to(x[:, lane:lane + 1], (x.shape[0], LANES))


def _cumsum_rows(x, block):
    row = lax.broadcasted_iota(jnp.int32, x.shape, 0) % block
    sh = 1
    while sh < block:
        x = x + jnp.where(row >= sh, pltpu.roll(x, sh, 0), 0.0)
        sh *= 2
    return x


def _l2n(t):
    return t * lax.rsqrt(jnp.sum(t * t, axis=-1, keepdims=True) + EPS)


def _gates(ab, alog, dtb):
    x = ab + dtb
    sp = jnp.maximum(x, 0.0) + jnp.log1p(jnp.exp(-jnp.abs(x)))
    return -jnp.exp(alog) * sp, jax.nn.sigmoid(ab)


def _merge_masks(c, top):
    rowi = lax.broadcasted_iota(jnp.int32, (c, c), 0)
    coli = lax.broadcasted_iota(jnp.int32, (c, c), 1)
    masks = []
    s = 1
    while s < top:
        rb, cb = rowi // s, coli // s
        masks.append(((rb // 2) == (cb // 2)) & ((rb % 2) == 1) & ((cb % 2) == 0))
        s *= 2
    return masks


def _each(f, *lists):
    return [f(*t) for t in zip(*lists)]


def _chunk_local(qs, ks, vs, gcols, grows, betas, tril, merges):
    c = qs[0].shape[0]
    decays = _each(lambda gc, gr: jnp.exp(jnp.minimum(gc - gr, 0.0)), gcols, grows)
    kbs = _each(lambda k, b: k * b, ks, betas)
    ms = _each(lambda q, kb, k: _dot_nt(jnp.concatenate([q, kb], axis=0).astype(BF16), k.astype(BF16)),
               qs, kbs, ks)
    qks = _each(lambda m, d: jnp.where(tril, m[:c] * d, 0.0), ms, decays)
    a = _each(lambda m, d: m[c:] * d, ms, decays)
    ns = [-jnp.where(merges[0], x, 0.0) if merges else jnp.zeros_like(x) for x in a]
    for off in merges[1:]:
        aos = [jnp.where(off, x, 0.0) for x in a]
        zs = _each(lambda ao, n: ao + _dot(ao.astype(BF16), n.astype(BF16)), aos, ns)
        ns = _each(lambda n, z: n - z - _dot(n.astype(BF16), z.astype(BF16)), ns, zs)
    egs = [jnp.exp(gc) for gc in gcols]
    rhss = _each(lambda v, b, kb, eg: jnp.concatenate([v * b, kb * eg], axis=1), vs, betas, kbs, egs)
    uws = _each(lambda r, n: r + _dot(n.astype(BF16), r.astype(BF16)), rhss, ns)
    return ([x[:, :LANES] for x in uws], [x[:, LANES:] for x in uws], qks,
            _each(lambda q, eg: q * eg, qs, egs))


def _gdn_tile(xp_ref, dz_ref, ab_ref, cw_ref, alog_ref, dtb_ref, ng_ref, y_ref, s_ref, nc):
    c = CHUNK
    r = nc * c
    pad = SUBLANES
    cw = cw_ref[...]
    conv = xp_ref[pad:pad + r, :] * cw[DN_CONV - 1:DN_CONV]
    for j in range(DN_CONV - 1):
        o = pad - (DN_CONV - 1) + j
        conv = conv + xp_ref[o:o + r, :] * cw[j:j + 1]
    conv = _silu(conv)
    g, beta = _gates(ab_ref[...], alog_ref[...], dtb_ref[...])
    rowi = lax.broadcasted_iota(jnp.int32, (c, c), 0)
    coli = lax.broadcasted_iota(jnp.int32, (c, c), 1)
    tril = rowi >= coli
    merges = _merge_masks(c, c)
    ng = ng_ref[...]
    heads = range(DN_HEADS)
    probs = [(ci, h) for ci in range(nc) for h in heads]
    rows = lambda ci: slice(ci * c, (ci + 1) * c)
    lanes = lambda part, h: slice(part * DN_WIDTH + h * LANES, part * DN_WIDTH + (h + 1) * LANES)
    gcs = [_cumsum_rows(g[rows(ci)], c) for ci in range(nc)]
    gcts = [gc.T for gc in gcs]
    qs = [_l2n(conv[rows(ci), lanes(0, h)]) * (DN_HEAD_DIM ** -0.5) for ci, h in probs]
    ks = [_l2n(conv[rows(ci), lanes(1, h)]) for ci, h in probs]
    vs = [conv[rows(ci), lanes(2, h)] for ci, h in probs]
    gcols = [_lane_bcast(gcs[ci], h) for ci, h in probs]
    grows = [gcts[ci][h:h + 1, :] for ci, h in probs]
    betas = [_lane_bcast(beta[rows(ci)], DN_HEADS + h) for ci, h in probs]
    us, ws, qks, qds = _chunk_local(qs, ks, vs, gcols, grows, betas, tril, merges)
    glasts = [gc[c - 1:c, :] for gc in gcols]
    kdts = _each(lambda k, gl, gc: (k * jnp.exp(gl - gc)).T, ks, glasts, gcols)
    wqs = _each(lambda w, qd: jnp.concatenate([w, qd], axis=0).astype(BF16), ws, qds)
    qkks = _each(lambda qk, kdt: jnp.concatenate([qk, kdt], axis=0).astype(BF16), qks, kdts)
    ss = [s_ref[h] for h in heads]
    for ci in range(nc):
        pr = [ci * DN_HEADS + h for h in heads]
        r2s = [_dot(wqs[p], s.astype(BF16)) for p, s in zip(pr, ss)]
        vnews = [us[p] - r2[:c] for p, r2 in zip(pr, r2s)]
        r3s = [_dot(qkks[p], vn.astype(BF16)) for p, vn in zip(pr, vnews)]
        ss = [s * jnp.exp(glasts[p]) + r3[c:] for p, s, r3 in zip(pr, ss, r3s)]
        for h in heads:
            o = r2s[h][c:] + r3s[h][:c]
            y_ref[rows(ci), lanes(0, h)] = (_rms(o, ng) * _silu(dz_ref[rows(ci), lanes(0, h)])).astype(BF16)
    for h in heads:
        s_ref[h] = ss[h]


def _proj_gdn_kernel(x_ref, g_ref, c_ref, s1_ref, s2_ref, wqkv_ref, wd_ref, wab_ref, wg_ref,
                     cw_ref, alog_ref, dtb_ref, ng_ref,
                     q_ref, k_ref, v_ref, ga_ref, gb_ref, y_ref, sout_ref, tail_ref,
                     xp0_ref, dz0_ref, ab0_ref, xp1_ref, dz1_ref, ab1_ref, carry_ref, s_ref, *, nt, nc):
    i = pl.program_id(0)
    pad = SUBLANES
    r = nc * CHUNK

    @pl.when(i == 0)
    def _():
        xp1_ref[...] = jnp.zeros_like(xp1_ref)
        dz1_ref[...] = jnp.zeros_like(dz1_ref)
        ab1_ref[...] = jnp.zeros_like(ab1_ref)

    @pl.when(jnp.maximum(i - 1, 0) % nt == 0)
    def _():
        carry_ref[...] = jnp.zeros_like(carry_ref)
        s_ref[...] = jnp.zeros_like(s_ref)

    def step(xpa_ref, dza_ref, aba_ref, xpb_ref, dzb_ref, abb_ref):
        _proj_kernel(x_ref, g_ref, c_ref, s1_ref, s2_ref, wqkv_ref, wd_ref, wab_ref, wg_ref,
                     q_ref, k_ref, v_ref, xpa_ref, dza_ref, aba_ref, ga_ref, gb_ref, dqkv_row0=pad)
        xpb_ref[0:pad, :] = carry_ref[...]
        _gdn_tile(xpb_ref, dzb_ref, abb_ref, cw_ref, alog_ref, dtb_ref, ng_ref, y_ref, s_ref, nc)
        carry_ref[...] = xpb_ref[r:r + pad, :]

    @pl.when(i % 2 == 0)
    def _():
        step(xp0_ref, dz0_ref, ab0_ref, xp1_ref, dz1_ref, ab1_ref)

    @pl.when(i % 2 == 1)
    def _():
        step(xp1_ref, dz1_ref, ab1_ref, xp0_ref, dz0_ref, ab0_ref)

    tail_ref[0] = carry_ref[...]
    sout_ref[0] = s_ref[...]


def _proj_gdn(x, g, tabs, wqkv, wd, wab, wg, cw, alog, dtb, ng, nseq, tm):
    m = x.shape[0]
    nt = m // nseq // tm
    last = m // tm - 1
    nc = tm // CHUNK
    proj = lambda i: jnp.minimum(i, last)
    gdn = lambda i: jnp.maximum(i - 1, 0)
    prow = lambda w: pl.BlockSpec((tm, w), lambda i: (proj(i), 0))
    tab = pl.BlockSpec((tm, LANES), lambda i: (proj(i) % nt, 0))
    per_seq = lambda *dims: pl.BlockSpec((1,) + dims, lambda i: (gdn(i) // nt,) + (0,) * len(dims))
    state = (DN_HEADS, DN_HEAD_DIM, DN_HEAD_DIM)
    bufs = [pltpu.VMEM((tm + SUBLANES, 3 * DN_WIDTH), F32), pltpu.VMEM((tm, DN_WIDTH), F32),
            pltpu.VMEM((tm, LANES), F32)]
    return pl.pallas_call(
        functools.partial(_proj_gdn_kernel, nt=nt, nc=nc),
        grid=(m // tm + 1,),
        in_specs=[prow(D_MODEL), _resident((1, D_MODEL)), tab, tab, tab,
                  _resident(wqkv.shape), _resident(wd.shape), _resident(wab.shape), _resident(wg.shape),
                  _resident(cw.shape), _resident((1, LANES)), _resident((1, LANES)), _resident((1, LANES))],
        out_specs=[prow(A_WIDTH), prow(LANES), prow(LANES), prow(D_MODEL), prow(D_MODEL),
                   pl.BlockSpec((tm, DN_WIDTH), lambda i: (gdn(i), 0)), per_seq(*state),
                   per_seq(SUBLANES, 3 * DN_WIDTH)],
        out_shape=[jax.ShapeDtypeStruct((m, A_WIDTH), BF16), jax.ShapeDtypeStruct((m, LANES), F32),
                   jax.ShapeDtypeStruct((m, LANES), F32), jax.ShapeDtypeStruct((m, D_MODEL), F32),
                   jax.ShapeDtypeStruct((m, D_MODEL), F32), jax.ShapeDtypeStruct((m, DN_WIDTH), BF16),
                   jax.ShapeDtypeStruct((nseq,) + state, F32),
                   jax.ShapeDtypeStruct((nseq, SUBLANES, 3 * DN_WIDTH), F32)],
        scratch_shapes=bufs + bufs + [pltpu.VMEM((SUBLANES, 3 * DN_WIDTH), F32), pltpu.VMEM(state, F32)],
        compiler_params=_cparams("arbitrary"),
        name="proj_gdn",
    )(x, g, *tabs, wqkv, wd, wab, wg, cw, alog, dtb, ng)


def _gdn_sample_kernel(buf_ref, x_ref, dz_ref, ab_ref, s0_ref, cw_ref, alog_ref, dtb_ref, ng_ref,
                       y_ref, sout_ref, xp_ref, ab16_ref):
    t = x_ref.shape[1]
    c = CHUNK
    sr = GDN_SAMPLE_ROWS
    nb = c // sr
    pad = SUBLANES
    hist = DN_CONV - 1
    xp_ref[...] = jnp.zeros_like(xp_ref)
    ab16_ref[...] = jnp.zeros_like(ab16_ref)
    for b in range(nb):
        xp_ref[pad + b * sr - hist:pad + b * sr, :] = buf_ref[b]
        xp_ref[pad + b * sr:pad + b * sr + t, :] = x_ref[b]
        ab16_ref[b * sr:b * sr + t, :] = ab_ref[b]
    cw = cw_ref[...]
    conv = xp_ref[pad:pad + c, :] * cw[hist:hist + 1]
    for j in range(hist):
        conv = conv + xp_ref[pad - hist + j:pad - hist + j + c, :] * cw[j:j + 1]
    conv = _silu(conv)
    live = (lax.broadcasted_iota(jnp.int32, (c, LANES), 0) % sr) < t
    g, beta = _gates(ab16_ref[...], alog_ref[...], dtb_ref[...])
    g = jnp.where(live, g, 0.0)
    beta = jnp.where(live, beta, 0.0)
    gc = _cumsum_rows(g, sr)
    gct = gc.T
    rowi = lax.broadcasted_iota(jnp.int32, (c, c), 0)
    coli = lax.broadcasted_iota(jnp.int32, (c, c), 1)
    tril = ((rowi // sr) == (coli // sr)) & (rowi >= coli)
    merges = _merge_masks(c, pl.next_power_of_2(t))
    ng = ng_ref[...]
    rowb = lax.broadcasted_iota(jnp.int32, (c, LANES), 0) // sr
    heads = range(DN_HEADS)
    lanes = lambda part, h: slice(part * DN_WIDTH + h * LANES, part * DN_WIDTH + (h + 1) * LANES)
    ks = [_l2n(conv[:, lanes(1, h)]) for h in heads]
    gcols = [_lane_bcast(gc, h) for h in heads]
    us, ws, qks, qds = _chunk_local(
        [_l2n(conv[:, lanes(0, h)]) * (DN_HEAD_DIM ** -0.5) for h in heads], ks,
        [conv[:, lanes(2, h)] for h in heads], gcols, [gct[h:h + 1, :] for h in heads],
        [_lane_bcast(beta, DN_HEADS + h) for h in heads], tril, merges)
    seqs = range(nb)
    rows = lambda b: slice(b * sr, (b + 1) * sr)
    s0s = [[s0_ref[b, h] for b in seqs] for h in heads]
    r2s = [[_dot(jnp.concatenate([ws[h][rows(b)], qds[h][rows(b)]], axis=0).astype(BF16), s0s[h][b].astype(BF16))
            for b in seqs] for h in heads]
    vnews = [jnp.concatenate([us[h][rows(b)] - r2s[h][b][:sr] for b in seqs], axis=0) for h in heads]
    os_ = [_rms(jnp.concatenate([r2s[h][b][sr:] for b in seqs], axis=0)
                + _dot(qks[h].astype(BF16), vnews[h].astype(BF16)), ng) for h in heads]
    for h in heads:
        for b in seqs:
            y_ref[b, :, lanes(0, h)] = os_[h][b * sr:b * sr + t] * _silu(dz_ref[b, :, lanes(0, h)])
    glasts = [jnp.concatenate([jnp.broadcast_to(gcols[h][(b + 1) * sr - 1:(b + 1) * sr, :], (sr, LANES))
                               for b in seqs], axis=0) for h in heads]
    kdts = [(ks[h] * jnp.exp(glasts[h] - gcols[h])).T.astype(BF16) for h in heads]
    for h in heads:
        for b in seqs:
            vb = jnp.where(rowb == b, vnews[h], 0.0).astype(BF16)
            sout_ref[b, h] = s0s[h][b] * jnp.exp(glasts[h][b * sr:b * sr + 1, :]) + _dot(kdts[h], vb)


def _gdn_sample(buf, x, dz, ab, s0, cw, alog, dtb, ng):
    bsz, t, _ = x.shape
    assert t + DN_CONV - 1 <= GDN_SAMPLE_ROWS, "too many new tokens for one row tile per sequence"
    nb = CHUNK // GDN_SAMPLE_ROWS
    seq = lambda a: pl.BlockSpec((nb,) + a.shape[1:], lambda i: (i,) + (0,) * (a.ndim - 1))
    return pl.pallas_call(
        _gdn_sample_kernel,
        grid=(bsz // nb,),
        in_specs=[seq(buf), seq(x), seq(dz), seq(ab), seq(s0), _resident(cw.shape),
                  _resident((1, LANES)), _resident((1, LANES)), _resident((1, LANES))],
        out_specs=[seq(dz), seq(s0)],
        out_shape=[jax.ShapeDtypeStruct(dz.shape, F32), jax.ShapeDtypeStruct(s0.shape, F32)],
        scratch_shapes=[pltpu.VMEM((CHUNK + SUBLANES, 3 * DN_WIDTH), F32), pltpu.VMEM((CHUNK, LANES), F32)],
        compiler_params=_cparams("parallel"),
        name="gdn_sample",
    )(buf, x, dz, ab, s0, cw, alog, dtb, ng)


def _merge_kernel(x_ref, *rest, attend):
    if attend:
        sink_ref, q_ref, kp_ref, kc_ref, vp_ref, vc_ref, *rest = rest
        ya = _swa_tile(sink_ref, q_ref, jnp.concatenate([kp_ref[...], kc_ref[...]], axis=0),
                       jnp.concatenate([vp_ref[...], vc_ref[...]], axis=0), pl.program_id(1) == 0)
    else:
        ya_ref, *rest = rest
        ya = ya_ref[...]
    yb_ref, ga_ref, gb_ref, wa_ref, wb_ref, wo_ref, g_ref, wq_ref, h_ref, hq_ref = rest
    mix = (jax.nn.sigmoid(ga_ref[...]) * _dot(ya.astype(BF16), wa_ref[...])
           + jax.nn.sigmoid(gb_ref[...]) * _dot(yb_ref[...].astype(BF16), wb_ref[...]))
    h = x_ref[...] + _dot(mix.astype(BF16), wo_ref[...])
    h_ref[...] = h
    hq_ref[...] = _dot(_rms(h, g_ref[...]).astype(BF16), wq_ref[...]).astype(BF16)


def _merge(x, attn, yb, ga, gb, wa, wb, wo, g, wq, nseq, tm):
    m = x.shape[0]
    nt = m // nseq // tm
    row = lambda w: pl.BlockSpec((tm, w), lambda s, j: (s * nt + j, 0))
    attend = isinstance(attn, tuple)
    if attend:
        sinks, q, k, v = attn
        per = tm // WINDOW
        prev = pl.BlockSpec((WINDOW, A_KV_WIDTH), lambda s, j: (jnp.maximum((s * nt + j) * per - 1, 0), 0))
        attn_args = [sinks, q, k, k, v, v]
        attn_specs = [pl.BlockSpec(memory_space=pltpu.SMEM), row(A_WIDTH), prev, row(A_KV_WIDTH),
                      prev, row(A_KV_WIDTH)]
    else:
        attn_args, attn_specs = [attn], [row(A_WIDTH)]
    return pl.pallas_call(
        functools.partial(_merge_kernel, attend=attend),
        grid=(nseq, nt),
        in_specs=[row(D_MODEL), *attn_specs, row(DN_WIDTH), row(D_MODEL), row(D_MODEL),
                  _resident(wa.shape), _resident(wb.shape), _resident(wo.shape),
                  _resident((1, D_MODEL)), _resident(wq.shape)],
        out_specs=[row(D_MODEL), row(X_WIDTH)],
        out_shape=[jax.ShapeDtypeStruct((m, D_MODEL), F32), jax.ShapeDtypeStruct((m, X_WIDTH), BF16)],
        compiler_params=_cparams("parallel", "parallel"),
        name="merge",
    )(x, *attn_args, yb, ga, gb, wa, wb, wo, g, wq)


def _memkv_kernel(m_ref, g_ref, w_ref, k_ref, v_ref):
    z = _dot(_rms(m_ref[...], g_ref[...]).astype(BF16), w_ref[...])
    k_ref[...] = z[:, :X_WIDTH]
    v_ref[...] = z[:, X_WIDTH:]


def _memkv(mem, g, w, tm):
    m = mem.shape[0]
    row = lambda wd: pl.BlockSpec((tm, wd), lambda i: (i, 0))
    return pl.pallas_call(
        _memkv_kernel,
        grid=(m // tm,),
        in_specs=[row(D_MODEL), _resident((1, D_MODEL)), _resident(w.shape)],
        out_specs=[row(X_WIDTH), row(X_WIDTH)],
        out_shape=[jax.ShapeDtypeStruct((m, X_WIDTH), F32)] * 2,
        compiler_params=_cparams("parallel"),
        name="mem_kv",
    )(mem, g, w)


def _softmax_rows(s):
    e = jnp.exp(s - jnp.max(s, axis=-1, keepdims=True))
    return e * (1.0 / jnp.sum(e, axis=-1, keepdims=True))


def _xattn_sample_kernel(hq_ref, mk_ref, mv_ref, o_ref, *, nb):
    rows = hq_ref.shape[1]
    nmh = mk_ref.shape[1]
    qh = lax.broadcasted_iota(jnp.int32, (X_HEADS * rows, nmh), 0) // rows
    mh = lax.broadcasted_iota(jnp.int32, (X_HEADS * rows, nmh), 1) % X_HEADS
    same = qh == mh
    seqs = range(nb)
    qs = [jnp.concatenate([hq_ref[b, :, h * LANES:(h + 1) * LANES] for h in range(X_HEADS)], axis=0)
          for b in seqs]
    ss = [_dot_nt(qs[b], mk_ref[b].astype(BF16)) * (X_HEAD_DIM ** -0.5) for b in seqs]
    ps = [_softmax_rows(jnp.where(same, s, -jnp.inf)).astype(BF16) for s in ss]
    os_ = [_dot(ps[b], mv_ref[b].astype(BF16)) for b in seqs]
    for b in seqs:
        for h in range(X_HEADS):
            o_ref[b, :, h * LANES:(h + 1) * LANES] = os_[b][h * rows:(h + 1) * rows].astype(BF16)


def _xattn_sample(hq, mk, mv, nb=8):
    bsz, rows, _ = hq.shape
    q = pl.BlockSpec((nb, rows, X_WIDTH), lambda i: (i, 0, 0))
    mem = pl.BlockSpec((nb, mk.shape[1], X_HEAD_DIM), lambda i: (i, 0, 0))
    return pl.pallas_call(
        functools.partial(_xattn_sample_kernel, nb=nb),
        grid=(bsz // nb,),
        in_specs=[q, mem, mem],
        out_specs=q,
        out_shape=jax.ShapeDtypeStruct(hq.shape, BF16),
        compiler_params=_cparams("parallel"),
        name="xattn_sample",
    )(hq, mk, mv)


def _xattn_tile(hq_ref, mk_ref, mv_ref):
    outs = []
    for h in range(X_HEADS):
        hs = slice(h * LANES, (h + 1) * LANES)
        s = _dot_nt(hq_ref[:, hs], mk_ref[:, hs].astype(BF16)) * (X_HEAD_DIM ** -0.5)
        outs.append(_dot(_softmax_rows(s).astype(BF16), mv_ref[:, hs].astype(BF16)).astype(BF16))
    return jnp.concatenate(outs, axis=1)


def _ffn_kernel(h_ref, *rest, stride, padc, fc, attend):
    if attend:
        hq_ref, mk_ref, mv_ref, *rest = rest
        xo = _xattn_tile(hq_ref, mk_ref, mv_ref)
    else:
        xo_ref, *rest = rest
        xo = xo_ref[...]
    cin_ref, wxo_ref, g_ref, wu_ref, wv_ref, cw_ref, wd_ref, fg_ref, y_ref, tail_ref, xp_ref, acc_ref = rest
    tm = h_ref.shape[0]

    @pl.when(pl.program_id(1) == 0)
    def _():
        tail_ref[...] = cin_ref[...]

    h = h_ref[...] + _dot(xo, wxo_ref[...])
    acc_ref[...] = h
    hn = _rms(h, g_ref[...]).astype(BF16)
    for ci in range(D_FF // fc):
        cs = slice(ci * fc, (ci + 1) * fc)
        u = _dot(hn, wu_ref[:, cs])
        gate = _dot(hn, wv_ref[:, cs])
        xp_ref[0:padc, :] = tail_ref[0, :, cs]
        xp_ref[padc:padc + tm, :] = u
        cw = cw_ref[:, cs]
        uc = u * cw[2:3]
        for j in range(FFN_CONV - 1):
            o = padc - (FFN_CONV - 1 - j) * stride
            uc = uc + xp_ref[o:o + tm, :] * cw[j:j + 1]
        tail_ref[0, :, cs] = xp_ref[tm:tm + padc, :]
        acc_ref[...] += _dot((_silu(uc) * gate).astype(BF16), wd_ref[cs, :])
    y_ref[...] = _rms(acc_ref[...], fg_ref[...])


def _ffn(h, xq, mem, cin, wxo, g, wup, cw, wd, fg, tm, stride, fc=D_FF // 2):
    m = h.shape[0]
    half = lambda c: pl.BlockSpec((D_MODEL, D_FF), lambda *_: (0, c), pipeline_mode=pl.Buffered(1))
    nseq, padc, _ = cin.shape
    nt = m // nseq // tm
    row = lambda w: pl.BlockSpec((tm, w), lambda s, j: (s * nt + j, 0))
    car = pl.BlockSpec((1, padc, D_FF), lambda s, j: (s, 0, 0))
    mem = list(mem or ())
    mem_specs = [pl.BlockSpec((a.shape[0] // nseq, X_WIDTH), lambda s, j: (s, 0)) for a in mem]
    return pl.pallas_call(
        functools.partial(_ffn_kernel, stride=stride, padc=padc, fc=fc, attend=bool(mem)),
        grid=(nseq, nt),
        in_specs=[row(D_MODEL), row(X_WIDTH), *mem_specs, car, _resident(wxo.shape), _resident((1, D_MODEL)),
                  half(0), half(1), _resident(cw.shape), _resident(wd.shape), _resident((1, D_MODEL))],
        out_specs=[row(D_MODEL), car],
        out_shape=[jax.ShapeDtypeStruct((m, D_MODEL), F32), jax.ShapeDtypeStruct(cin.shape, F32)],
        scratch_shapes=[pltpu.VMEM((tm + padc, fc), F32), pltpu.VMEM((tm, D_MODEL), F32)],
        compiler_params=_cparams("parallel", "arbitrary"),
        name="ffn",
    )(h, xq, *mem, cin, wxo, g, wup, wup, cw, wd, fg)


def _lane_row(vec, offset=0):
    return jnp.zeros((1, LANES), F32).at[0, offset:offset + vec.shape[0]].set(vec.astype(F32))


def _prep(p):
    w_in = p["w_in"]
    o = 0
    cuts = {}
    for name, wdt in (("qkv", A_WIDTH + 2 * A_KV_WIDTH), ("d", 4 * DN_WIDTH), ("ab", 2 * DN_HEADS),
                      ("g", 2 * D_MODEL)):
        cuts[name] = w_in[:, o:o + wdt]
        o += wdt
    w = {
        "wqkv": cuts["qkv"].astype(BF16),
        "wd": cuts["d"].astype(BF16),
        "wab": jnp.pad(cuts["ab"], ((0, 0), (0, LANES - 2 * DN_HEADS))).astype(BF16),
        "wg": cuts["g"].astype(BF16),
        "norm_mix_g": p["norm_mix_g"].reshape(1, D_MODEL),
        "dn_conv_w": p["dn_conv_w"],
        "alog": _lane_row(p["dn_a_log"]),
        "dtb": _lane_row(p["dn_dt_bias"]),
        "dn_norm_g": p["dn_norm_g"].reshape(1, DN_HEAD_DIM),
        "sinks": p["sinks"].astype(F32),
        "w_br_a": p["w_br_a"].astype(BF16),
        "w_br_b": p["w_br_b"].astype(BF16),
        "w_mix_out": p["w_mix_out"].astype(BF16),
        "norm_x_g": p["norm_x_g"].reshape(1, D_MODEL),
        "w_xq": p["w_xq"].astype(BF16),
        "w_xo": p["w_xo"].astype(BF16),
        "norm_ffn_g": p["norm_ffn_g"].reshape(1, D_MODEL),
        "w_up": p["w_up"].astype(BF16),
        "ffn_conv_w": p["ffn_conv_w"],
        "w_down": p["w_down"].astype(BF16),
        "final_norm_g": p["final_norm_g"].reshape(1, D_MODEL),
    }
    return w


def _tile(m, pref):
    return pref if m % pref == 0 else m


def _prompt_layer(x, mem, norm_mem_g, w_xkv, w):
    bsz, seqlen, _ = x.shape
    m = bsz * seqlen
    x2 = x.reshape(m, D_MODEL)
    tm = _tile(seqlen, 512)
    tabs = _rope_tables(jnp.arange(seqlen, dtype=jnp.int32))
    q, k, v, ga, gb, yb, dn_state, dn_tail = _proj_gdn(
        x2, w["norm_mix_g"], tabs, w["wqkv"], w["wd"], w["wab"], w["wg"], w["dn_conv_w"], w["alog"], w["dtb"],
        w["dn_norm_g"], bsz, tm)
    h, hq = _merge(x2, (w["sinks"], q, k, v), yb, ga, gb, w["w_br_a"], w["w_br_b"], w["w_mix_out"],
                   w["norm_x_g"], w["w_xq"], bsz, tm)
    nm = mem.shape[1]
    mk, mv = _memkv(mem.reshape(bsz * nm, D_MODEL), norm_mem_g.reshape(1, D_MODEL), w_xkv.astype(BF16),
                    _tile(bsz * nm, 512))
    cin = jnp.zeros((bsz, SUBLANES, D_FF), F32)
    y, tail = _ffn(h, hq, (mk, mv), cin, w["w_xo"], w["norm_ffn_g"], w["w_up"], w["ffn_conv_w"],
                   w["w_down"], w["final_norm_g"], tm, 1)
    wb = min(WINDOW, seqlen)
    new = (
        k.reshape(bsz, seqlen, A_KV_HEADS, A_HEAD_DIM)[:, seqlen - wb:],
        v.reshape(bsz, seqlen, A_KV_HEADS, A_HEAD_DIM)[:, seqlen - wb:],
        dn_tail[:, SUBLANES - (DN_CONV - 1):],
        dn_state,
        mk.reshape(bsz, nm, X_HEADS, X_HEAD_DIM),
        mv.reshape(bsz, nm, X_HEADS, X_HEAD_DIM),
        tail[:, SUBLANES - (FFN_CONV - 1):],
    )
    return y.reshape(bsz, seqlen, D_MODEL), new


def _sample_layer(x, pos0, win_k, win_v, dn_buf, dn_state, mem_k, mem_v, ffn_buf, w):
    bsz, t, _ = x.shape
    m = bsz * t
    sr = SAMPLE_ROWS
    x2 = x.reshape(m, D_MODEL)
    tabs = _rope_tables(jnp.tile(pos0 + jnp.arange(t, dtype=jnp.int32), bsz))
    q, k, v, dqkv, dz, ab, ga, gb = _proj(x2, w["norm_mix_g"], tabs, w["wqkv"], w["wd"], w["wab"], w["wg"], m)

    wlen = win_k.shape[1]
    q5 = q.reshape(bsz, t, A_KV_HEADS, A_HEADS // A_KV_HEADS, A_HEAD_DIM)
    q5 = jnp.transpose(q5, (0, 2, 3, 1, 4))
    eye = jnp.eye(A_KV_HEADS, dtype=BF16)
    q8 = (q5[:, :, :, :, None, :] * eye[None, :, None, None, :, None]).reshape(bsz, A_HEADS * t, A_KV_WIDTH)
    padrows = lambda a: jnp.pad(a.reshape(bsz, t, -1), ((0, 0), (0, sr - t), (0, 0)))
    o8 = _swa_sample(q8, win_k.reshape(bsz, wlen, A_KV_WIDTH), win_v.reshape(bsz, wlen, A_KV_WIDTH),
                     padrows(k).astype(BF16), padrows(v).astype(BF16), w["sinks"], t)
    o8 = o8.reshape(bsz, A_KV_HEADS, A_HEADS // A_KV_HEADS, t, A_KV_HEADS, A_HEAD_DIM)
    o8 = jnp.stack([o8[:, g, :, :, g] for g in range(A_KV_HEADS)], axis=1)
    ya = jnp.transpose(o8, (0, 3, 1, 2, 4)).reshape(m, A_WIDTH).astype(BF16)
    new_wk = jnp.concatenate([win_k, k.reshape(bsz, t, A_KV_HEADS, A_HEAD_DIM)], axis=1)[:, -wlen:]
    new_wv = jnp.concatenate([win_v, v.reshape(bsz, t, A_KV_HEADS, A_HEAD_DIM)], axis=1)[:, -wlen:]

    dqkv3 = dqkv.reshape(bsz, t, 3 * DN_WIDTH)
    yb, new_s = _gdn_sample(dn_buf, dqkv3, dz.reshape(bsz, t, DN_WIDTH), ab.reshape(bsz, t, LANES), dn_state,
                            w["dn_conv_w"], w["alog"], w["dtb"], w["dn_norm_g"])
    yb = yb.reshape(m, DN_WIDTH)
    new_dn_buf = jnp.concatenate([dn_buf, dqkv3], axis=1)[:, -(DN_CONV - 1):]

    h, hq = _merge(x2, ya, yb, ga, gb, w["w_br_a"], w["w_br_b"], w["w_mix_out"], w["norm_x_g"], w["w_xq"], 1, m)
    nm = mem_k.shape[1]
    xo = _xattn_sample(padrows(hq), mem_k.reshape(bsz, nm * X_HEADS, X_HEAD_DIM),
                       mem_v.reshape(bsz, nm * X_HEADS, X_HEAD_DIM))
    xo = xo[:, :t]

    tmaj = lambda a: jnp.transpose(a.reshape(bsz, t, -1), (1, 0, 2)).reshape(m, -1)
    cin = jnp.transpose(ffn_buf, (1, 0, 2)).reshape(1, (FFN_CONV - 1) * bsz, D_FF)
    y, tail = _ffn(tmaj(h), tmaj(xo), None, cin, w["w_xo"], w["norm_ffn_g"], w["w_up"], w["ffn_conv_w"],
                   w["w_down"], w["final_norm_g"], m, bsz)
    y = jnp.transpose(y.reshape(t, bsz, D_MODEL), (1, 0, 2))
    new_ffn = jnp.transpose(tail.reshape(FFN_CONV - 1, bsz, D_FF), (1, 0, 2))
    return y, (new_wk, new_wv, new_dn_buf, new_s, new_ffn)


def kernel(x_prompt, x_sample, mem_prompt, cache_win_k, cache_win_v, state_dn_conv, state_dn, cache_mem_k, cache_mem_v, state_ffn_conv, norm_mix_g, w_in, dn_conv_w, dn_a_log, dn_dt_bias, dn_norm_g, attn_sinks, w_br_a, w_br_b, w_mix_out, norm_x_g, norm_mem_g, w_xq, w_xkv, w_xo, norm_ffn_g, w_up, ffn_conv_w, w_down, final_norm_g):
    p = {"norm_mix_g": norm_mix_g[0], "w_in": w_in[0], "dn_conv_w": dn_conv_w[0], "dn_a_log": dn_a_log[0],
         "dn_dt_bias": dn_dt_bias[0], "dn_norm_g": dn_norm_g[0], "sinks": attn_sinks[0], "w_br_a": w_br_a[0],
         "w_br_b": w_br_b[0], "w_mix_out": w_mix_out[0], "norm_x_g": norm_x_g[0], "w_xq": w_xq[0],
         "w_xo": w_xo[0], "norm_ffn_g": norm_ffn_g[0], "w_up": w_up[0], "ffn_conv_w": ffn_conv_w[0],
         "w_down": w_down[0], "final_norm_g": final_norm_g}
    w = _prep(p)
    yp, newp = _prompt_layer(x_prompt, mem_prompt, norm_mem_g[0], w_xkv[0], w)
    ys, news = _sample_layer(x_sample, PAST_LEN, cache_win_k[0], cache_win_v[0], state_dn_conv[0], state_dn[0],
                             cache_mem_k[0], cache_mem_v[0], state_ffn_conv[0], w)
    lead = lambda a: a[None]
    p_win_k, p_win_v, p_dn_conv, p_dn_state, p_mem_k, p_mem_v, p_ffn_conv = [lead(a) for a in newp]
    s_win_k, s_win_v, s_dn_conv, s_dn_state, s_ffn_conv = [lead(a) for a in news]
    return (yp, ys, p_win_k, p_win_v, p_dn_conv, p_dn_state, p_mem_k, p_mem_v, p_ffn_conv,
            s_win_k, s_win_v, s_dn_conv, s_dn_state, s_ffn_conv)
```

```python
import functools
import math

import jax
import jax.numpy as jnp
from jax import lax
from jax.experimental import pallas as pl
from jax.experimental.pallas import tpu as pltpu

F32 = jnp.float32
BF16 = jnp.bfloat16

D_MODEL = 1024
A_HEADS = 8
A_KV_HEADS = 2
A_HEAD_DIM = 64
A_WIDTH = 512
A_KV_WIDTH = 128
WINDOW = 128
ROT_DIM = 16
ROPE_THETA = 500000.0
DN_HEADS = 4
DN_HEAD_DIM = 128
DN_WIDTH = 512
DN_CONV = 4
X_HEADS = 4
X_HEAD_DIM = 128
X_WIDTH = 512
D_FF = 2816
FFN_CONV = 3
EPS = 1e-6
PAST_LEN = 16384

LANES = 128
SUBLANES = 8
VMEM_LIMIT = 56 * 1024 * 1024
CHUNK = 128
SAMPLE_ROWS = 16
GDN_SAMPLE_ROWS = 8


def _cparams(*sem):
    return pltpu.CompilerParams(dimension_semantics=sem, vmem_limit_bytes=VMEM_LIMIT)


def _resident(shape):
    return pl.BlockSpec(shape, lambda *_: (0,) * len(shape), pipeline_mode=pl.Buffered(1))


def _rms(x, g):
    return x * lax.rsqrt(jnp.mean(x * x, axis=-1, keepdims=True) + EPS) * g


def _dot(a, b):
    return jnp.dot(a, b, preferred_element_type=F32)


def _dot_nt(a, b):
    return lax.dot_general(a, b, (((1,), (1,)), ((), ())), preferred_element_type=F32)


def _silu(x):
    return x * jax.nn.sigmoid(x)


def _rope(seg, c, s1, s2):
    return seg * c + pltpu.roll(seg, LANES - 8, 1) * s1 + pltpu.roll(seg, 8, 1) * s2


PROJ_COLS = 512


def _proj_tile(x_ref, g_ref, c_ref, s1_ref, s2_ref, wqkv_ref, wd_ref, wab_ref, wg_ref,
               q_ref, k_ref, v_ref, dqkv_ref, dz_ref, ab_ref, ga_ref, gb_ref, dqkv_row0=0):
    tm = x_ref.shape[0]
    xb = _rms(x_ref[...], g_ref[...]).astype(BF16)
    yield
    c, s1, s2 = c_ref[...], s1_ref[...], s2_ref[...]
    z = _dot(xb, wqkv_ref[...])
    for i in range(A_WIDTH // LANES):
        sl = slice(i * LANES, (i + 1) * LANES)
        q_ref[:, sl] = _rope(z[:, sl], c, s1, s2).astype(BF16)
    k_ref[...] = _rope(z[:, A_WIDTH:A_WIDTH + LANES], c, s1, s2)
    v_ref[...] = z[:, A_WIDTH + LANES:]
    yield
    pc = PROJ_COLS
    for j in range(3 * DN_WIDTH // pc):
        dqkv_ref[dqkv_row0:dqkv_row0 + tm, j * pc:(j + 1) * pc] = _dot(xb, wd_ref[:, j * pc:(j + 1) * pc])
        yield
    dz_ref[...] = _dot(xb, wd_ref[:, 3 * DN_WIDTH:])
    ab_ref[...] = _dot(xb, wab_ref[...])
    yield
    for dst, base in ((ga_ref, 0), (gb_ref, D_MODEL)):
        for j in range(D_MODEL // pc):
            dst[:, j * pc:(j + 1) * pc] = _dot(xb, wg_ref[:, base + j * pc:base + (j + 1) * pc])
            yield


def _proj_kernel(*refs):
    _drain(_proj_tile(*refs))


def _proj(x, g, tabs, wqkv, wd, wab, wg, tm):
    m = x.shape[0]
    nt = tabs[0].shape[0] // tm
    row = lambda w: pl.BlockSpec((tm, w), lambda i: (i, 0))
    tab = pl.BlockSpec((tm, LANES), lambda i: (i % nt, 0))
    widths = (A_WIDTH, LANES, LANES, 3 * DN_WIDTH, DN_WIDTH, LANES, D_MODEL, D_MODEL)
    dts = (BF16, F32, F32, F32, F32, F32, F32, F32)
    return pl.pallas_call(
        _proj_kernel,
        grid=(m // tm,),
        in_specs=[row(D_MODEL), _resident((1, D_MODEL)), tab, tab, tab,
                  _resident(wqkv.shape), _resident(wd.shape), _resident(wab.shape), _resident(wg.shape)],
        out_specs=[row(w) for w in widths],
        out_shape=[jax.ShapeDtypeStruct((m, w), d) for w, d in zip(widths, dts)],
        compiler_params=_cparams("parallel"),
        name="in_proj",
    )(x, g, *tabs, wqkv, wd, wab, wg)


def _rope_tables(pos):
    half = ROT_DIM // 2
    inv = ROPE_THETA ** (-2.0 * jnp.arange(half, dtype=F32) / ROT_DIM)
    ang = pos.astype(F32)[:, None] * inv[None, :]
    c, s = jnp.cos(ang), jnp.sin(ang)
    n = pos.shape[0]
    one = jnp.ones((n, A_HEAD_DIM - ROT_DIM), F32)
    z8 = jnp.zeros((n, half), F32)
    z48 = jnp.zeros((n, A_HEAD_DIM - ROT_DIM), F32)
    ct = jnp.concatenate([c, c, one], axis=1)
    s1 = jnp.concatenate([-s, z8, z48], axis=1)
    s2 = jnp.concatenate([z8, s, z48], axis=1)
    two = lambda t: jnp.concatenate([t, t], axis=1)
    return two(ct), two(s1), two(s2)


def _both_halves(t, lane_lo):
    r = pltpu.roll(t, A_HEAD_DIM, 1)
    return jnp.where(lane_lo, t, r), jnp.where(lane_lo, r, t)


def _sink_softmax(s, valid, sink):
    s = jnp.where(valid, s, -jnp.inf)
    m = jnp.maximum(jnp.max(s, axis=-1, keepdims=True), sink)
    p = jnp.exp(s - m)
    den = jnp.sum(p, axis=-1, keepdims=True) + jnp.exp(sink - m)
    return p * (1.0 / den)


def _swa_tile(sink_ref, q_ref, k, v, first):
    w = WINDOW
    lane_lo_k = lax.broadcasted_iota(jnp.int32, k.shape, 1) < A_HEAD_DIM
    kk = [t.astype(BF16) for t in _both_halves(k, lane_lo_k)]
    vv = [t.astype(BF16) for t in _both_halves(v, lane_lo_k)]
    lane_lo = lax.broadcasted_iota(jnp.int32, (w, LANES), 1) < A_HEAD_DIM
    row = lax.broadcasted_iota(jnp.int32, (4 * w, 2 * w), 0)
    col = lax.broadcasted_iota(jnp.int32, (4 * w, 2 * w), 1)
    d = (row & (w - 1)) + w - col
    band = (d >= 0) & (d < w)
    band_first = band & (col >= jnp.where(first, w, 0))
    hrow = lax.broadcasted_iota(jnp.int32, (4 * w, 1), 0) // w
    sinks = []
    for g in range(A_KV_HEADS):
        sink = jnp.zeros((4 * w, 1), F32)
        for j in range(4):
            sink = jnp.where(hrow == j, sink_ref[4 * g + j], sink)
        sinks.append(sink)
    zero = jnp.zeros((), BF16)
    blocks = []
    for b in range(q_ref.shape[0] // w):
        segs = []
        for g in range(A_KV_HEADS):
            parts = []
            for sgm in range(2):
                seg = q_ref[b * w:(b + 1) * w, (2 * g + sgm) * LANES:(2 * g + sgm + 1) * LANES]
                parts += [jnp.where(lane_lo, seg, zero), jnp.where(lane_lo, zero, seg)]
            qs = jnp.concatenate(parts, axis=0)
            s = _dot_nt(qs, kk[g][b * w:(b + 2) * w]) * (A_HEAD_DIM ** -0.5)
            p = _sink_softmax(s, band_first if b == 0 else band, sinks[g]).astype(BF16)
            o = _dot(p, vv[g][b * w:(b + 2) * w])
            segs += [jnp.where(lane_lo, o[(2 * sgm) * w:(2 * sgm + 1) * w],
                               o[(2 * sgm + 1) * w:(2 * sgm + 2) * w]).astype(BF16) for sgm in range(2)]
        blocks.append(jnp.concatenate(segs, axis=1))
    return jnp.concatenate(blocks, axis=0)


def _swa_sample_kernel(sink_ref, q_ref, ckt_ref, cvt_ref, kn_ref, vn_ref, o_ref, *, nb, t):
    w = ckt_ref.shape[3]
    rows = q_ref.shape[1] // A_HEADS
    gq = rows * (A_HEADS // A_KV_HEADS)
    r = lax.broadcasted_iota(jnp.int32, (gq, w), 0)
    c = lax.broadcasted_iota(jnp.int32, (gq, w), 1)
    valid_c = c > r % rows
    rn = lax.broadcasted_iota(jnp.int32, (gq, kn_ref.shape[1]), 0)
    cn = lax.broadcasted_iota(jnp.int32, (gq, kn_ref.shape[1]), 1)
    tn = cn // A_KV_HEADS
    causal_n = (tn <= rn % rows) & (tn < t)
    hrow = lax.broadcasted_iota(jnp.int32, (gq, 1), 0) // rows
    valid_n, sink = [], []
    for g in range(A_KV_HEADS):
        valid_n.append(causal_n & (cn % A_KV_HEADS == g))
        sk = jnp.zeros((gq, 1), F32)
        for j in range(A_HEADS // A_KV_HEADS):
            sk = jnp.where(hrow == j, sink_ref[g * (A_HEADS // A_KV_HEADS) + j], sk)
        sink.append(sk)
    probs = [(b, g) for b in range(nb) for g in range(A_KV_HEADS)]
    scale = A_HEAD_DIM ** -0.5
    qs = [q_ref[b, g * gq:(g + 1) * gq, :] for b, g in probs]
    scs = [jnp.where(valid_c, _dot(q, ckt_ref[b, g].astype(BF16)) * scale, -jnp.inf)
           for q, (b, g) in zip(qs, probs)]
    sns = [jnp.where(valid_n[g], _dot_nt(q, kn_ref[b]) * scale, -jnp.inf) for q, (b, g) in zip(qs, probs)]
    ms = [jnp.maximum(jnp.maximum(jnp.max(sc, -1, keepdims=True), jnp.max(sn, -1, keepdims=True)), sink[g])
          for sc, sn, (b, g) in zip(scs, sns, probs)]
    pcs = [jnp.exp(sc - m) for sc, m in zip(scs, ms)]
    pns = [jnp.exp(sn - m) for sn, m in zip(sns, ms)]
    invs = [1.0 / (jnp.sum(pc, -1, keepdims=True) + jnp.sum(pn, -1, keepdims=True) + jnp.exp(sink[g] - m))
            for pc, pn, m, (b, g) in zip(pcs, pns, ms, probs)]
    for pc, pn, inv, (b, g) in zip(pcs, pns, invs, probs):
        o_ref[b, g * gq:(g + 1) * gq, :] = (_dot_nt((pc * inv).astype(BF16), cvt_ref[b, g].astype(BF16))
                                            + _dot((pn * inv).astype(BF16), vn_ref[b]))


def _swa_sample(q, ckt, cvt, kn, vn, sinks, t, nb=8):
    bsz, nq, _ = q.shape
    blk = lambda a: pl.BlockSpec((nb,) + a.shape[1:], lambda i: (i,) + (0,) * (a.ndim - 1))
    return pl.pallas_call(
        functools.partial(_swa_sample_kernel, nb=nb, t=t),
        grid=(bsz // nb,),
        in_specs=[pl.BlockSpec(memory_space=pltpu.SMEM), blk(q), blk(ckt), blk(cvt), blk(kn), blk(vn)],
        out_specs=blk(q),
        out_shape=jax.ShapeDtypeStruct(q.shape, F32),
        compiler_params=_cparams("parallel"),
        name="swa_sample",
    )(sinks, q, ckt, cvt, kn, vn)


def _lane_bcast(x, lane):
    return jnp.broadcast_to(x[:, lane:lane + 1], (x.shape[0], LANES))


def _cumsum_rows(x, block):
    c = x.shape[0]
    rowi = lax.broadcasted_iota(jnp.int32, (c, c), 0)
    coli = lax.broadcasted_iota(jnp.int32, (c, c), 1)
    ones = (((rowi // block) == (coli // block)) & (rowi >= coli)).astype(BF16)
    hi = x.astype(BF16)
    r1 = x - hi.astype(F32)
    mid = r1.astype(BF16)
    lo = (r1 - mid.astype(F32)).astype(BF16)
    s = _dot(ones, jnp.concatenate([hi, mid, lo], axis=1))
    return s[:, :LANES] + s[:, LANES:2 * LANES] + s[:, 2 * LANES:]


def _l2n(t):
    return t * lax.rsqrt(jnp.sum(t * t, axis=-1, keepdims=True) + EPS)


def _gates(ab, alog, dtb):
    x = ab + dtb
    sp = jnp.maximum(x, 0.0) + jnp.log1p(jnp.exp(-jnp.abs(x)))
    return -jnp.exp(alog) * sp, jax.nn.sigmoid(ab)


def _merge_masks(c, top):
    rowi = lax.broadcasted_iota(jnp.int32, (c, c), 0)
    coli = lax.broadcasted_iota(jnp.int32, (c, c), 1)
    masks = []
    s = 1
    while s < top:
        rb, cb = rowi // s, coli // s
        masks.append(((rb // 2) == (cb // 2)) & ((rb % 2) == 1) & ((cb % 2) == 0))
        s *= 2
    return masks


def _each(f, *lists):
    return [f(*t) for t in zip(*lists)]


def _drain(stages):
    try:
        while True:
            next(stages)
    except StopIteration as done:
        return done.value


def _interleave(*staged):
    live = list(staged)
    while live:
        for item in tuple(live):
            stages, per_turn = item
            try:
                for _ in range(per_turn):
                    next(stages)
            except StopIteration:
                live.remove(item)


def _chunk_local(qs, ks, vs, gcols, grows, betas, tril, merges):
    c = qs[0].shape[0]
    decays = _each(lambda gc, gr: jnp.exp(jnp.minimum(gc - gr, 0.0)), gcols, grows)
    kbs = _each(lambda k, b: k * b, ks, betas)
    yield
    ms = _each(lambda q, kb, k: _dot_nt(jnp.concatenate([q, kb], axis=0).astype(BF16), k.astype(BF16)),
               qs, kbs, ks)
    yield
    qks = _each(lambda m, d: jnp.where(tril, m[:c] * d, 0.0), ms, decays)
    a = _each(lambda m, d: m[c:] * d, ms, decays)
    eye = (lax.broadcasted_iota(jnp.int32, (c, c), 0) == lax.broadcasted_iota(jnp.int32, (c, c), 1)).astype(F32)
    ts = [eye - jnp.where(merges[0], x, 0.0) if merges else eye for x in a]
    yield
    for off in merges[1:]:
        tbs = [t.astype(BF16) for t in ts]
        zs = _each(lambda x, tb: _dot(jnp.where(off, x, 0.0).astype(BF16), tb), a, tbs)
        yield
        ts = _each(lambda t, tb, z: t - _dot(tb, z.astype(BF16)), ts, tbs, zs)
        yield
    ns = [t - eye for t in ts]
    egs = [jnp.exp(gc) for gc in gcols]
    rhss = _each(lambda v, b, kb, eg: jnp.concatenate([v * b, kb * eg], axis=1), vs, betas, kbs, egs)
    yield
    uws = _each(lambda r, n: r + _dot(n.astype(BF16), r.astype(BF16)), rhss, ns)
    yield
    return ([x[:, :LANES] for x in uws], [x[:, LANES:] for x in uws], qks,
            _each(lambda q, eg: q * eg, qs, egs))


def _gdn_tile(xp_ref, dz_ref, ab_ref, cw_ref, alog_ref, dtb_ref, ng_ref, y_ref, s_ref, nc):
    c = CHUNK
    pad = SUBLANES
    cw = cw_ref[...]
    g, beta = _gates(ab_ref[...], alog_ref[...], dtb_ref[...])
    rowi = lax.broadcasted_iota(jnp.int32, (c, c), 0)
    coli = lax.broadcasted_iota(jnp.int32, (c, c), 1)
    tril = rowi >= coli
    merges = _merge_masks(c, c)
    ng = ng_ref[...]
    heads = range(DN_HEADS)
    probs = [(ci, h) for ci in range(nc) for h in heads]
    rows = lambda ci: slice(ci * c, (ci + 1) * c)
    lanes = lambda part, h: slice(part * DN_WIDTH + h * LANES, part * DN_WIDTH + (h + 1) * LANES)
    yield
    qs, ks, vs, gcols, grows, betas = [], [], [], [], [], []
    for ci in range(nc):
        conv = xp_ref[pad + ci * c:pad + (ci + 1) * c, :] * cw[DN_CONV - 1:DN_CONV]
        for j in range(DN_CONV - 1):
            o = pad - (DN_CONV - 1) + j + ci * c
            conv = conv + xp_ref[o:o + c, :] * cw[j:j + 1]
        conv = _silu(conv)
        gc = _cumsum_rows(g[rows(ci)], c)
        gct = gc.T
        qs += [_l2n(conv[:, lanes(0, h)]) * (DN_HEAD_DIM ** -0.5) for h in heads]
        ks += [_l2n(conv[:, lanes(1, h)]) for h in heads]
        vs += [conv[:, lanes(2, h)] for h in heads]
        gcols += [_lane_bcast(gc, h) for h in heads]
        grows += [gct[h:h + 1, :] for h in heads]
        betas += [_lane_bcast(beta[rows(ci)], DN_HEADS + h) for h in heads]
        yield
    us, ws, qks, qds = yield from _chunk_local(qs, ks, vs, gcols, grows, betas, tril, merges)
    glasts = [gc[c - 1:c, :] for gc in gcols]
    kdts = _each(lambda k, gl, gc: (k * jnp.exp(gl - gc)).T, ks, glasts, gcols)
    wqs = _each(lambda w, qd: jnp.concatenate([w, qd], axis=0).astype(BF16), ws, qds)
    qkks = _each(lambda qk, kdt: jnp.concatenate([qk, kdt], axis=0).astype(BF16), qks, kdts)
    yield
    ss = [s_ref[h] for h in heads]
    for ci in range(nc):
        pr = [ci * DN_HEADS + h for h in heads]
        r2s = [_dot(wqs[p], s.astype(BF16)) for p, s in zip(pr, ss)]
        vnews = [us[p] - r2[:c] for p, r2 in zip(pr, r2s)]
        yield
        r3s = [_dot(qkks[p], vn.astype(BF16)) for p, vn in zip(pr, vnews)]
        ss = [s * jnp.exp(glasts[p]) + r3[c:] for p, s, r3 in zip(pr, ss, r3s)]
        for h in heads:
            o = r2s[h][c:] + r3s[h][:c]
            y_ref[rows(ci), lanes(0, h)] = (_rms(o, ng) * _silu(dz_ref[rows(ci), lanes(0, h)])).astype(BF16)
        yield
    for h in heads:
        s_ref[h] = ss[h]


def _proj_gdn_kernel(x_ref, g_ref, c_ref, s1_ref, s2_ref, wqkv_ref, wd_ref, wab_ref, wg_ref,
                     cw_ref, alog_ref, dtb_ref, ng_ref,
                     q_ref, k_ref, v_ref, ga_ref, gb_ref, y_ref, sout_ref, tail_ref,
                     xp0_ref, dz0_ref, ab0_ref, xp1_ref, dz1_ref, ab1_ref, carry_ref, s_ref, *, nt, nc):
    i = pl.program_id(0)
    pad = SUBLANES
    r = nc * CHUNK

    @pl.when(i == 0)
    def _():
        xp1_ref[...] = jnp.zeros_like(xp1_ref)
        dz1_ref[...] = jnp.zeros_like(dz1_ref)
        ab1_ref[...] = jnp.zeros_like(ab1_ref)

    @pl.when(jnp.maximum(i - 1, 0) % nt == 0)
    def _():
        carry_ref[...] = jnp.zeros_like(carry_ref)
        s_ref[...] = jnp.zeros_like(s_ref)

    def step(xpa_ref, dza_ref, aba_ref, xpb_ref, dzb_ref, abb_ref):
        xpb_ref[0:pad, :] = carry_ref[...]
        _interleave(
            (_gdn_tile(xpb_ref, dzb_ref, abb_ref, cw_ref, alog_ref, dtb_ref, ng_ref, y_ref, s_ref, nc), 1),
            (_proj_tile(x_ref, g_ref, c_ref, s1_ref, s2_ref, wqkv_ref, wd_ref, wab_ref, wg_ref,
                        q_ref, k_ref, v_ref, xpa_ref, dza_ref, aba_ref, ga_ref, gb_ref, pad), 1))
        carry_ref[...] = xpb_ref[r:r + pad, :]

    @pl.when(i % 2 == 0)
    def _():
        step(xp0_ref, dz0_ref, ab0_ref, xp1_ref, dz1_ref, ab1_ref)

    @pl.when(i % 2 == 1)
    def _():
        step(xp1_ref, dz1_ref, ab1_ref, xp0_ref, dz0_ref, ab0_ref)

    tail_ref[0] = carry_ref[...]
    sout_ref[0] = s_ref[...]


def _proj_gdn(x, g, tabs, wqkv, wd, wab, wg, cw, alog, dtb, ng, nseq, tm):
    m = x.shape[0]
    nt = m // nseq // tm
    last = m // tm - 1
    nc = tm // CHUNK
    proj = lambda i: jnp.minimum(i, last)
    gdn = lambda i: jnp.maximum(i - 1, 0)
    prow = lambda w: pl.BlockSpec((tm, w), lambda i: (proj(i), 0))
    tab = pl.BlockSpec((tm, LANES), lambda i: (proj(i) % nt, 0))
    per_seq = lambda *dims: pl.BlockSpec((1,) + dims, lambda i: (gdn(i) // nt,) + (0,) * len(dims))
    state = (DN_HEADS, DN_HEAD_DIM, DN_HEAD_DIM)
    bufs = [pltpu.VMEM((tm + SUBLANES, 3 * DN_WIDTH), F32), pltpu.VMEM((tm, DN_WIDTH), F32),
            pltpu.VMEM((tm, LANES), F32)]
    return pl.pallas_call(
        functools.partial(_proj_gdn_kernel, nt=nt, nc=nc),
        grid=(m // tm + 1,),
        in_specs=[prow(D_MODEL), _resident((1, D_MODEL)), tab, tab, tab,
                  _resident(wqkv.shape), _resident(wd.shape), _resident(wab.shape), _resident(wg.shape),
                  _resident(cw.shape), _resident((1, LANES)), _resident((1, LANES)), _resident((1, LANES))],
        out_specs=[prow(A_WIDTH), prow(LANES), prow(LANES), prow(D_MODEL), prow(D_MODEL),
                   pl.BlockSpec((tm, DN_WIDTH), lambda i: (gdn(i), 0)), per_seq(*state),
                   per_seq(SUBLANES, 3 * DN_WIDTH)],
        out_shape=[jax.ShapeDtypeStruct((m, A_WIDTH), BF16), jax.ShapeDtypeStruct((m, LANES), F32),
                   jax.ShapeDtypeStruct((m, LANES), F32), jax.ShapeDtypeStruct((m, D_MODEL), F32),
                   jax.ShapeDtypeStruct((m, D_MODEL), F32), jax.ShapeDtypeStruct((m, DN_WIDTH), BF16),
                   jax.ShapeDtypeStruct((nseq,) + state, F32),
                   jax.ShapeDtypeStruct((nseq, SUBLANES, 3 * DN_WIDTH), F32)],
        scratch_shapes=bufs + bufs + [pltpu.VMEM((SUBLANES, 3 * DN_WIDTH), F32), pltpu.VMEM(state, F32)],
        compiler_params=_cparams("arbitrary"),
        name="proj_gdn",
    )(x, g, *tabs, wqkv, wd, wab, wg, cw, alog, dtb, ng)


def _gdn_sample_kernel(buf_ref, x_ref, dz_ref, ab_ref, s0_ref, cw_ref, alog_ref, dtb_ref, ng_ref,
                       y_ref, sout_ref, xp_ref, ab16_ref):
    t = x_ref.shape[1]
    c = CHUNK
    sr = GDN_SAMPLE_ROWS
    nb = c // sr
    pad = SUBLANES
    hist = DN_CONV - 1
    xp_ref[...] = jnp.zeros_like(xp_ref)
    ab16_ref[...] = jnp.zeros_like(ab16_ref)
    for b in range(nb):
        xp_ref[pad + b * sr - hist:pad + b * sr, :] = buf_ref[b]
        xp_ref[pad + b * sr:pad + b * sr + t, :] = x_ref[b]
        ab16_ref[b * sr:b * sr + t, :] = ab_ref[b]
    cw = cw_ref[...]
    conv = xp_ref[pad:pad + c, :] * cw[hist:hist + 1]
    for j in range(hist):
        conv = conv + xp_ref[pad - hist + j:pad - hist + j + c, :] * cw[j:j + 1]
    conv = _silu(conv)
    live = (lax.broadcasted_iota(jnp.int32, (c, LANES), 0) % sr) < t
    g, beta = _gates(ab16_ref[...], alog_ref[...], dtb_ref[...])
    g = jnp.where(live, g, 0.0)
    beta = jnp.where(live, beta, 0.0)
    gc = _cumsum_rows(g, sr)
    gct = gc.T
    rowi = lax.broadcasted_iota(jnp.int32, (c, c), 0)
    coli = lax.broadcasted_iota(jnp.int32, (c, c), 1)
    tril = ((rowi // sr) == (coli // sr)) & (rowi >= coli)
    merges = _merge_masks(c, pl.next_power_of_2(t))
    ng = ng_ref[...]
    rowb = lax.broadcasted_iota(jnp.int32, (c, LANES), 0) // sr
    heads = range(DN_HEADS)
    lanes = lambda part, h: slice(part * DN_WIDTH + h * LANES, part * DN_WIDTH + (h + 1) * LANES)
    ks = [_l2n(conv[:, lanes(1, h)]) for h in heads]
    gcols = [_lane_bcast(gc, h) for h in heads]
    us, ws, qks, qds = _drain(_chunk_local(
        [_l2n(conv[:, lanes(0, h)]) * (DN_HEAD_DIM ** -0.5) for h in heads], ks,
        [conv[:, lanes(2, h)] for h in heads], gcols, [gct[h:h + 1, :] for h in heads],
        [_lane_bcast(beta, DN_HEADS + h) for h in heads], tril, merges))
    seqs = range(nb)
    rows = lambda b: slice(b * sr, (b + 1) * sr)
    s0s = [[s0_ref[b, h] for b in seqs] for h in heads]
    r2s = [[_dot(jnp.concatenate([ws[h][rows(b)], qds[h][rows(b)]], axis=0).astype(BF16), s0s[h][b].astype(BF16))
            for b in seqs] for h in heads]
    vnews = [jnp.concatenate([us[h][rows(b)] - r2s[h][b][:sr] for b in seqs], axis=0) for h in heads]
    os_ = [_rms(jnp.concatenate([r2s[h][b][sr:] for b in seqs], axis=0)
                + _dot(qks[h].astype(BF16), vnews[h].astype(BF16)), ng) for h in heads]
    for h in heads:
        for b in seqs:
            y_ref[b, :, lanes(0, h)] = os_[h][b * sr:b * sr + t] * _silu(dz_ref[b, :, lanes(0, h)])
    glasts = [jnp.concatenate([jnp.broadcast_to(gcols[h][(b + 1) * sr - 1:(b + 1) * sr, :], (sr, LANES))
                               for b in seqs], axis=0) for h in heads]
    kdts = [(ks[h] * jnp.exp(glasts[h] - gcols[h])).T.astype(BF16) for h in heads]
    for h in heads:
        for b in seqs:
            vb = jnp.where(rowb == b, vnews[h], 0.0).astype(BF16)
            sout_ref[b, h] = s0s[h][b] * jnp.exp(glasts[h][b * sr:b * sr + 1, :]) + _dot(kdts[h], vb)


def _gdn_sample(buf, x, dz, ab, s0, cw, alog, dtb, ng):
    bsz, t, _ = x.shape
    assert t + DN_CONV - 1 <= GDN_SAMPLE_ROWS, "too many new tokens for one row tile per sequence"
    nb = CHUNK // GDN_SAMPLE_ROWS
    seq = lambda a: pl.BlockSpec((nb,) + a.shape[1:], lambda i: (i,) + (0,) * (a.ndim - 1))
    return pl.pallas_call(
        _gdn_sample_kernel,
        grid=(bsz // nb,),
        in_specs=[seq(buf), seq(x), seq(dz), seq(ab), seq(s0), _resident(cw.shape),
                  _resident((1, LANES)), _resident((1, LANES)), _resident((1, LANES))],
        out_specs=[seq(dz), seq(s0)],
        out_shape=[jax.ShapeDtypeStruct(dz.shape, F32), jax.ShapeDtypeStruct(s0.shape, F32)],
        scratch_shapes=[pltpu.VMEM((CHUNK + SUBLANES, 3 * DN_WIDTH), F32), pltpu.VMEM((CHUNK, LANES), F32)],
        compiler_params=_cparams("parallel"),
        name="gdn_sample",
    )(buf, x, dz, ab, s0, cw, alog, dtb, ng)


def _merge_kernel(x_ref, *rest, attend):
    if attend:
        sink_ref, q_ref, kp_ref, kc_ref, vp_ref, vc_ref, *rest = rest
        ya = _swa_tile(sink_ref, q_ref, jnp.concatenate([kp_ref[...], kc_ref[...]], axis=0),
                       jnp.concatenate([vp_ref[...], vc_ref[...]], axis=0), pl.program_id(1) == 0)
    else:
        ya_ref, *rest = rest
        ya = ya_ref[...]
    yb_ref, ga_ref, gb_ref, wa_ref, wb_ref, wo_ref, g_ref, wq_ref, h_ref, hq_ref = rest
    mix = (jax.nn.sigmoid(ga_ref[...]) * _dot(ya.astype(BF16), wa_ref[...])
           + jax.nn.sigmoid(gb_ref[...]) * _dot(yb_ref[...].astype(BF16), wb_ref[...]))
    h = x_ref[...] + _dot(mix.astype(BF16), wo_ref[...])
    h_ref[...] = h
    hq_ref[...] = _dot(_rms(h, g_ref[...]).astype(BF16), wq_ref[...]).astype(BF16)


def _merge(x, attn, yb, ga, gb, wa, wb, wo, g, wq, nseq, tm):
    m = x.shape[0]
    nt = m // nseq // tm
    row = lambda w: pl.BlockSpec((tm, w), lambda s, j: (s * nt + j, 0))
    attend = isinstance(attn, tuple)
    if attend:
        sinks, q, k, v = attn
        per = tm // WINDOW
        prev = pl.BlockSpec((WINDOW, A_KV_WIDTH), lambda s, j: (jnp.maximum((s * nt + j) * per - 1, 0), 0))
        attn_args = [sinks, q, k, k, v, v]
        attn_specs = [pl.BlockSpec(memory_space=pltpu.SMEM), row(A_WIDTH), prev, row(A_KV_WIDTH),
                      prev, row(A_KV_WIDTH)]
    else:
        attn_args, attn_specs = [attn], [row(A_WIDTH)]
    return pl.pallas_call(
        functools.partial(_merge_kernel, attend=attend),
        grid=(nseq, nt),
        in_specs=[row(D_MODEL), *attn_specs, row(DN_WIDTH), row(D_MODEL), row(D_MODEL),
                  _resident(wa.shape), _resident(wb.shape), _resident(wo.shape),
                  _resident((1, D_MODEL)), _resident(wq.shape)],
        out_specs=[row(D_MODEL), row(X_WIDTH)],
        out_shape=[jax.ShapeDtypeStruct((m, D_MODEL), F32), jax.ShapeDtypeStruct((m, X_WIDTH), BF16)],
        compiler_params=_cparams("parallel", "parallel"),
        name="merge",
    )(x, *attn_args, yb, ga, gb, wa, wb, wo, g, wq)


def _memkv_kernel(m_ref, g_ref, w_ref, k_ref, v_ref):
    z = _dot(_rms(m_ref[...], g_ref[...]).astype(BF16), w_ref[...])
    k_ref[...] = z[:, :X_WIDTH]
    v_ref[...] = z[:, X_WIDTH:]


def _memkv(mem, g, w, tm):
    m = mem.shape[0]
    row = lambda wd: pl.BlockSpec((tm, wd), lambda i: (i, 0))
    return pl.pallas_call(
        _memkv_kernel,
        grid=(m // tm,),
        in_specs=[row(D_MODEL), _resident((1, D_MODEL)), _resident(w.shape)],
        out_specs=[row(X_WIDTH), row(X_WIDTH)],
        out_shape=[jax.ShapeDtypeStruct((m, X_WIDTH), F32)] * 2,
        compiler_params=_cparams("parallel"),
        name="mem_kv",
    )(mem, g, w)


def _softmax_rows(s):
    e = jnp.exp(s - jnp.max(s, axis=-1, keepdims=True))
    return e * (1.0 / jnp.sum(e, axis=-1, keepdims=True))


def _xattn_sample_kernel(hq_ref, mk_ref, mv_ref, o_ref, *, nb):
    rows = hq_ref.shape[1]
    nmh = mk_ref.shape[1]
    qh = lax.broadcasted_iota(jnp.int32, (X_HEADS * rows, nmh), 0) // rows
    mh = lax.broadcasted_iota(jnp.int32, (X_HEADS * rows, nmh), 1) % X_HEADS
    same = qh == mh
    seqs = range(nb)
    qs = [jnp.concatenate([hq_ref[b, :, h * LANES:(h + 1) * LANES] for h in range(X_HEADS)], axis=0)
          for b in seqs]
    ss = [_dot_nt(qs[b], mk_ref[b].astype(BF16)) * (X_HEAD_DIM ** -0.5) for b in seqs]
    ps = [_softmax_rows(jnp.where(same, s, -jnp.inf)).astype(BF16) for s in ss]
    os_ = [_dot(ps[b], mv_ref[b].astype(BF16)) for b in seqs]
    for b in seqs:
        for h in range(X_HEADS):
            o_ref[b, :, h * LANES:(h + 1) * LANES] = os_[b][h * rows:(h + 1) * rows].astype(BF16)


def _xattn_sample(hq, mk, mv, nb=8):
    bsz, rows, _ = hq.shape
    q = pl.BlockSpec((nb, rows, X_WIDTH), lambda i: (i, 0, 0))
    mem = pl.BlockSpec((nb, mk.shape[1], X_HEAD_DIM), lambda i: (i, 0, 0))
    return pl.pallas_call(
        functools.partial(_xattn_sample_kernel, nb=nb),
        grid=(bsz // nb,),
        in_specs=[q, mem, mem],
        out_specs=q,
        out_shape=jax.ShapeDtypeStruct(hq.shape, BF16),
        compiler_params=_cparams("parallel"),
        name="xattn_sample",
    )(hq, mk, mv)


def _xattn_tile(hq_ref, mk_ref, mv_ref):
    outs = []
    for h in range(X_HEADS):
        hs = slice(h * LANES, (h + 1) * LANES)
        s = _dot_nt(hq_ref[:, hs], mk_ref[:, hs].astype(BF16)) * (X_HEAD_DIM ** -0.5)
        outs.append(_dot(_softmax_rows(s).astype(BF16), mv_ref[:, hs].astype(BF16)).astype(BF16))
    return jnp.concatenate(outs, axis=1)


def _ffn_kernel(h_ref, *rest, stride, padc, fc, attend):
    if attend:
        hq_ref, mk_ref, mv_ref, *rest = rest
        xo = _xattn_tile(hq_ref, mk_ref, mv_ref)
    else:
        xo_ref, *rest = rest
        xo = xo_ref[...]
    cin_ref, wxo_ref, g_ref, wu_ref, wv_ref, cw_ref, wd_ref, fg_ref, y_ref, tail_ref, xp_ref, acc_ref = rest
    tm = h_ref.shape[0]

    @pl.when(pl.program_id(1) == 0)
    def _():
        tail_ref[...] = cin_ref[...]

    h = h_ref[...] + _dot(xo, wxo_ref[...])
    acc_ref[...] = h
    hn = _rms(h, g_ref[...]).astype(BF16)
    for ci in range(D_FF // fc):
        cs = slice(ci * fc, (ci + 1) * fc)
        u = _dot(hn, wu_ref[:, cs])
        gate = _dot(hn, wv_ref[:, cs])
        xp_ref[0:padc, :] = tail_ref[0, :, cs]
        xp_ref[padc:padc + tm, :] = u
        cw = cw_ref[:, cs]
        uc = u * cw[2:3]
        for j in range(FFN_CONV - 1):
            o = padc - (FFN_CONV - 1 - j) * stride
            uc = uc + xp_ref[o:o + tm, :] * cw[j:j + 1]
        tail_ref[0, :, cs] = xp_ref[tm:tm + padc, :]
        acc_ref[...] += _dot((_silu(uc) * gate).astype(BF16), wd_ref[cs, :])
    y_ref[...] = _rms(acc_ref[...], fg_ref[...])


def _ffn(h, xq, mem, cin, wxo, g, wup, cw, wd, fg, tm, stride, fc=D_FF // 2):
    m = h.shape[0]
    half = lambda c: pl.BlockSpec((D_MODEL, D_FF), lambda *_: (0, c), pipeline_mode=pl.Buffered(1))
    nseq, padc, _ = cin.shape
    nt = m // nseq // tm
    row = lambda w: pl.BlockSpec((tm, w), lambda s, j: (s * nt + j, 0))
    car = pl.BlockSpec((1, padc, D_FF), lambda s, j: (s, 0, 0))
    mem = list(mem or ())
    mem_specs = [pl.BlockSpec((a.shape[0] // nseq, X_WIDTH), lambda s, j: (s, 0)) for a in mem]
    return pl.pallas_call(
        functools.partial(_ffn_kernel, stride=stride, padc=padc, fc=fc, attend=bool(mem)),
        grid=(nseq, nt),
        in_specs=[row(D_MODEL), row(X_WIDTH), *mem_specs, car, _resident(wxo.shape), _resident((1, D_MODEL)),
                  half(0), half(1), _resident(cw.shape), _resident(wd.shape), _resident((1, D_MODEL))],
        out_specs=[row(D_MODEL), car],
        out_shape=[jax.ShapeDtypeStruct((m, D_MODEL), F32), jax.ShapeDtypeStruct(cin.shape, F32)],
        scratch_shapes=[pltpu.VMEM((tm + padc, fc), F32), pltpu.VMEM((tm, D_MODEL), F32)],
        compiler_params=_cparams("parallel", "arbitrary"),
        name="ffn",
    )(h, xq, *mem, cin, wxo, g, wup, wup, cw, wd, fg)


def _lane_row(vec, offset=0):
    return jnp.zeros((1, LANES), F32).at[0, offset:offset + vec.shape[0]].set(vec.astype(F32))


def _prep(p):
    w_in = p["w_in"]
    o = 0
    cuts = {}
    for name, wdt in (("qkv", A_WIDTH + 2 * A_KV_WIDTH), ("d", 4 * DN_WIDTH), ("ab", 2 * DN_HEADS),
                      ("g", 2 * D_MODEL)):
        cuts[name] = w_in[:, o:o + wdt]
        o += wdt
    w = {
        "wqkv": cuts["qkv"].astype(BF16),
        "wd": cuts["d"].astype(BF16),
        "wab": jnp.pad(cuts["ab"], ((0, 0), (0, LANES - 2 * DN_HEADS))).astype(BF16),
        "wg": cuts["g"].astype(BF16),
        "norm_mix_g": p["norm_mix_g"].reshape(1, D_MODEL),
        "dn_conv_w": p["dn_conv_w"],
        "alog": _lane_row(p["dn_a_log"]),
        "dtb": _lane_row(p["dn_dt_bias"]),
        "dn_norm_g": p["dn_norm_g"].reshape(1, DN_HEAD_DIM),
        "sinks": p["sinks"].astype(F32),
        "w_br_a": p["w_br_a"].astype(BF16),
        "w_br_b": p["w_br_b"].astype(BF16),
        "w_mix_out": p["w_mix_out"].astype(BF16),
        "norm_x_g": p["norm_x_g"].reshape(1, D_MODEL),
        "w_xq": p["w_xq"].astype(BF16),
        "w_xo": p["w_xo"].astype(BF16),
        "norm_ffn_g": p["norm_ffn_g"].reshape(1, D_MODEL),
        "w_up": p["w_up"].astype(BF16),
        "ffn_conv_w": p["ffn_conv_w"],
        "w_down": p["w_down"].astype(BF16),
        "final_norm_g": p["final_norm_g"].reshape(1, D_MODEL),
    }
    return w


def _tile(m, pref):
    return pref if m % pref == 0 else m


def _prompt_layer(x, mem, norm_mem_g, w_xkv, w):
    bsz, seqlen, _ = x.shape
    m = bsz * seqlen
    x2 = x.reshape(m, D_MODEL)
    tm = _tile(seqlen, 512)
    tabs = _rope_tables(jnp.arange(seqlen, dtype=jnp.int32))
    q, k, v, ga, gb, yb, dn_state, dn_tail = _proj_gdn(
        x2, w["norm_mix_g"], tabs, w["wqkv"], w["wd"], w["wab"], w["wg"], w["dn_conv_w"], w["alog"], w["dtb"],
        w["dn_norm_g"], bsz, tm)
    h, hq = _merge(x2, (w["sinks"], q, k, v), yb, ga, gb, w["w_br_a"], w["w_br_b"], w["w_mix_out"],
                   w["norm_x_g"], w["w_xq"], bsz, tm)
    nm = mem.shape[1]
    mk, mv = _memkv(mem.reshape(bsz * nm, D_MODEL), norm_mem_g.reshape(1, D_MODEL), w_xkv.astype(BF16),
                    _tile(bsz * nm, 512))
    cin = jnp.zeros((bsz, SUBLANES, D_FF), F32)
    y, tail = _ffn(h, hq, (mk, mv), cin, w["w_xo"], w["norm_ffn_g"], w["w_up"], w["ffn_conv_w"],
                   w["w_down"], w["final_norm_g"], tm, 1)
    wb = min(WINDOW, seqlen)
    new = (
        k.reshape(bsz, seqlen, A_KV_HEADS, A_HEAD_DIM)[:, seqlen - wb:],
        v.reshape(bsz, seqlen, A_KV_HEADS, A_HEAD_DIM)[:, seqlen - wb:],
        dn_tail[:, SUBLANES - (DN_CONV - 1):],
        dn_state,
        mk.reshape(bsz, nm, X_HEADS, X_HEAD_DIM),
        mv.reshape(bsz, nm, X_HEADS, X_HEAD_DIM),
        tail[:, SUBLANES - (FFN_CONV - 1):],
    )
    return y.reshape(bsz, seqlen, D_MODEL), new


def _sample_layer(x, pos0, win_k, win_v, dn_buf, dn_state, mem_k, mem_v, ffn_buf, w):
    bsz, t, _ = x.shape
    m = bsz * t
    sr = SAMPLE_ROWS
    x2 = x.reshape(m, D_MODEL)
    tabs = _rope_tables(jnp.tile(pos0 + jnp.arange(t, dtype=jnp.int32), bsz))
    q, k, v, dqkv, dz, ab, ga, gb = _proj(x2, w["norm_mix_g"], tabs, w["wqkv"], w["wd"], w["wab"], w["wg"], m)

    wlen = win_k.shape[1]
    qh = jnp.transpose(q.reshape(bsz, t, A_HEADS, A_HEAD_DIM), (0, 2, 1, 3)).reshape(bsz, A_HEADS * t, A_HEAD_DIM)
    padrows = lambda a: jnp.pad(a.reshape(bsz, t, -1), ((0, 0), (0, sr - t), (0, 0)))
    newrows = lambda a: jnp.pad(a.reshape(bsz, t * A_KV_HEADS, A_HEAD_DIM),
                                ((0, 0), (0, sr - t * A_KV_HEADS), (0, 0))).astype(BF16)
    kv_major = lambda a: jnp.transpose(a, (0, 2, 3, 1))
    oh = _swa_sample(qh, kv_major(win_k), kv_major(win_v), newrows(k), newrows(v), w["sinks"], t)
    ya = jnp.transpose(oh.reshape(bsz, A_HEADS, t, A_HEAD_DIM), (0, 2, 1, 3)).reshape(m, A_WIDTH)
    new_wk = jnp.concatenate([win_k, k.reshape(bsz, t, A_KV_HEADS, A_HEAD_DIM)], axis=1)[:, -wlen:]
    new_wv = jnp.concatenate([win_v, v.reshape(bsz, t, A_KV_HEADS, A_HEAD_DIM)], axis=1)[:, -wlen:]

    dqkv3 = dqkv.reshape(bsz, t, 3 * DN_WIDTH)
    yb, new_s = _gdn_sample(dn_buf, dqkv3, dz.reshape(bsz, t, DN_WIDTH), ab.reshape(bsz, t, LANES), dn_state,
                            w["dn_conv_w"], w["alog"], w["dtb"], w["dn_norm_g"])
    yb = yb.reshape(m, DN_WIDTH)
    new_dn_buf = jnp.concatenate([dn_buf, dqkv3], axis=1)[:, -(DN_CONV - 1):]

    h, hq = _merge(x2, ya, yb, ga, gb, w["w_br_a"], w["w_br_b"], w["w_mix_out"], w["norm_x_g"], w["w_xq"], 1, m)
    nm = mem_k.shape[1]
    xo = _xattn_sample(padrows(hq), mem_k.reshape(bsz, nm * X_HEADS, X_HEAD_DIM),
                       mem_v.reshape(bsz, nm * X_HEADS, X_HEAD_DIM))
    xo = xo[:, :t]

    tmaj = lambda a: jnp.transpose(a.reshape(bsz, t, -1), (1, 0, 2)).reshape(m, -1)
    cin = jnp.transpose(ffn_buf, (1, 0, 2)).reshape(1, (FFN_CONV - 1) * bsz, D_FF)
    y, tail = _ffn(tmaj(h), tmaj(xo), None, cin, w["w_xo"], w["norm_ffn_g"], w["w_up"], w["ffn_conv_w"],
                   w["w_down"], w["final_norm_g"], m, bsz)
    y = jnp.transpose(y.reshape(t, bsz, D_MODEL), (1, 0, 2))
    new_ffn = jnp.transpose(tail.reshape(FFN_CONV - 1, bsz, D_FF), (1, 0, 2))
    return y, (new_wk, new_wv, new_dn_buf, new_s, new_ffn)


def kernel(x_prompt, x_sample, mem_prompt, cache_win_k, cache_win_v, state_dn_conv, state_dn, cache_mem_k, cache_mem_v, state_ffn_conv, norm_mix_g, w_in, dn_conv_w, dn_a_log, dn_dt_bias, dn_norm_g, attn_sinks, w_br_a, w_br_b, w_mix_out, norm_x_g, norm_mem_g, w_xq, w_xkv, w_xo, norm_ffn_g, w_up, ffn_conv_w, w_down, final_norm_g):
    p = {"norm_mix_g": norm_mix_g[0], "w_in": w_in[0], "dn_conv_w": dn_conv_w[0], "dn_a_log": dn_a_log[0],
         "dn_dt_bias": dn_dt_bias[0], "dn_norm_g": dn_norm_g[0], "sinks": attn_sinks[0], "w_br_a": w_br_a[0],
         "w_br_b": w_br_b[0], "w_mix_out": w_mix_out[0], "norm_x_g": norm_x_g[0], "w_xq": w_xq[0],
         "w_xo": w_xo[0], "norm_ffn_g": norm_ffn_g[0], "w_up": w_up[0], "ffn_conv_w": ffn_conv_w[0],
         "w_down": w_down[0], "final_norm_g": final_norm_g}
    w = _prep(p)
    yp, newp = _prompt_layer(x_prompt, mem_prompt, norm_mem_g[0], w_xkv[0], w)
    ys, news = _sample_layer(x_sample, PAST_LEN, cache_win_k[0], cache_win_v[0], state_dn_conv[0], state_dn[0],
                             cache_mem_k[0], cache_mem_v[0], state_ffn_conv[0], w)
    lead = lambda a: a[None]
    p_win_k, p_win_v, p_dn_conv, p_dn_state, p_mem_k, p_mem_v, p_ffn_conv = [lead(a) for a in newp]
    s_win_k, s_win_v, s_dn_conv, s_dn_state, s_ffn_conv = [lead(a) for a in news]
    return (yp, ys, p_win_k, p_win_v, p_dn_conv, p_dn_state, p_mem_k, p_mem_v, p_ffn_conv,
            s_win_k, s_win_v, s_dn_conv, s_dn_state, s_ffn_conv)
```

```python
import functools
import math

import jax
import jax.numpy as jnp
from jax import lax
from jax.experimental import pallas as pl
from jax.experimental.pallas import tpu as pltpu

F32 = jnp.float32
BF16 = jnp.bfloat16

D_MODEL = 1024
A_HEADS = 8
A_KV_HEADS = 2
A_HEAD_DIM = 64
A_WIDTH = 512
A_KV_WIDTH = 128
WINDOW = 128
ROT_DIM = 16
ROPE_THETA = 500000.0
DN_HEADS = 4
DN_HEAD_DIM = 128
DN_WIDTH = 512
DN_CONV = 4
X_HEADS = 4
X_HEAD_DIM = 128
X_WIDTH = 512
D_FF = 2816
FFN_CONV = 3
EPS = 1e-6
PAST_LEN = 16384

LANES = 128
SUBLANES = 8
VMEM_LIMIT = 56 * 1024 * 1024
CHUNK = 128
SAMPLE_ROWS = 16
GDN_SAMPLE_ROWS = 8


def _cparams(*sem):
    return pltpu.CompilerParams(dimension_semantics=sem, vmem_limit_bytes=VMEM_LIMIT)


def _resident(shape):
    return pl.BlockSpec(shape, lambda *_: (0,) * len(shape), pipeline_mode=pl.Buffered(1))


def _rms(x, g):
    return x * lax.rsqrt(jnp.mean(x * x, axis=-1, keepdims=True) + EPS) * g


def _dot(a, b):
    return jnp.dot(a, b, preferred_element_type=F32)


def _dot_nt(a, b):
    return lax.dot_general(a, b, (((1,), (1,)), ((), ())), preferred_element_type=F32)


def _silu(x):
    return x * jax.nn.sigmoid(x)


def _rope(seg, c, s1, s2):
    return seg * c + pltpu.roll(seg, LANES - 8, 1) * s1 + pltpu.roll(seg, 8, 1) * s2


PROJ_COLS = 512


def _proj_tile(x_ref, g_ref, c_ref, s1_ref, s2_ref, wqkv_ref, wd_ref, wab_ref, wg_ref,
               q_ref, k_ref, v_ref, dqkv_ref, dz_ref, ab_ref, ga_ref, gb_ref, dqkv_row0=0):
    tm = x_ref.shape[0]
    xb = _rms(x_ref[...], g_ref[...]).astype(BF16)
    yield
    c, s1, s2 = c_ref[...], s1_ref[...], s2_ref[...]
    z = _dot(xb, wqkv_ref[...])
    for i in range(A_WIDTH // LANES):
        sl = slice(i * LANES, (i + 1) * LANES)
        q_ref[:, sl] = _rope(z[:, sl], c, s1, s2).astype(BF16)
    k_ref[...] = _rope(z[:, A_WIDTH:A_WIDTH + LANES], c, s1, s2)
    v_ref[...] = z[:, A_WIDTH + LANES:]
    yield
    pc = PROJ_COLS
    for j in range(3 * DN_WIDTH // pc):
        dqkv_ref[dqkv_row0:dqkv_row0 + tm, j * pc:(j + 1) * pc] = _dot(xb, wd_ref[:, j * pc:(j + 1) * pc])
        yield
    dz_ref[...] = _dot(xb, wd_ref[:, 3 * DN_WIDTH:])
    ab_ref[...] = _dot(xb, wab_ref[...])
    yield
    for dst, base in ((ga_ref, 0), (gb_ref, D_MODEL)):
        for j in range(D_MODEL // pc):
            dst[:, j * pc:(j + 1) * pc] = _dot(xb, wg_ref[:, base + j * pc:base + (j + 1) * pc])
            yield


def _proj_kernel(*refs):
    _drain(_proj_tile(*refs))


def _proj(x, g, tabs, wqkv, wd, wab, wg, tm):
    m = x.shape[0]
    nt = tabs[0].shape[0] // tm
    row = lambda w: pl.BlockSpec((tm, w), lambda i: (i, 0))
    tab = pl.BlockSpec((tm, LANES), lambda i: (i % nt, 0))
    widths = (A_WIDTH, LANES, LANES, 3 * DN_WIDTH, DN_WIDTH, LANES, D_MODEL, D_MODEL)
    dts = (BF16, F32, F32, F32, F32, F32, F32, F32)
    return pl.pallas_call(
        _proj_kernel,
        grid=(m // tm,),
        in_specs=[row(D_MODEL), _resident((1, D_MODEL)), tab, tab, tab,
                  _resident(wqkv.shape), _resident(wd.shape), _resident(wab.shape), _resident(wg.shape)],
        out_specs=[row(w) for w in widths],
        out_shape=[jax.ShapeDtypeStruct((m, w), d) for w, d in zip(widths, dts)],
        compiler_params=_cparams("parallel"),
        name="in_proj",
    )(x, g, *tabs, wqkv, wd, wab, wg)


def _rope_tables(pos):
    half = ROT_DIM // 2
    inv = ROPE_THETA ** (-2.0 * jnp.arange(half, dtype=F32) / ROT_DIM)
    ang = pos.astype(F32)[:, None] * inv[None, :]
    c, s = jnp.cos(ang), jnp.sin(ang)
    n = pos.shape[0]
    one = jnp.ones((n, A_HEAD_DIM - ROT_DIM), F32)
    z8 = jnp.zeros((n, half), F32)
    z48 = jnp.zeros((n, A_HEAD_DIM - ROT_DIM), F32)
    ct = jnp.concatenate([c, c, one], axis=1)
    s1 = jnp.concatenate([-s, z8, z48], axis=1)
    s2 = jnp.concatenate([z8, s, z48], axis=1)
    two = lambda t: jnp.concatenate([t, t], axis=1)
    return two(ct), two(s1), two(s2)


def _both_halves(t, lane_lo):
    r = pltpu.roll(t, A_HEAD_DIM, 1)
    return jnp.where(lane_lo, t, r), jnp.where(lane_lo, r, t)


def _sink_softmax(s, valid, sink):
    s = jnp.where(valid, s, -jnp.inf)
    m = jnp.maximum(jnp.max(s, axis=-1, keepdims=True), sink)
    p = jnp.exp(s - m)
    den = jnp.sum(p, axis=-1, keepdims=True) + jnp.exp(sink - m)
    return p * (1.0 / den)


def _swa_tile(sink_ref, q_ref, k, v, first):
    w = WINDOW
    lane_lo_k = lax.broadcasted_iota(jnp.int32, k.shape, 1) < A_HEAD_DIM
    kk = [t.astype(BF16) for t in _both_halves(k, lane_lo_k)]
    vv = [t.astype(BF16) for t in _both_halves(v, lane_lo_k)]
    lane_lo = lax.broadcasted_iota(jnp.int32, (w, LANES), 1) < A_HEAD_DIM
    row = lax.broadcasted_iota(jnp.int32, (4 * w, 2 * w), 0)
    col = lax.broadcasted_iota(jnp.int32, (4 * w, 2 * w), 1)
    d = (row & (w - 1)) + w - col
    band = (d >= 0) & (d < w)
    band_first = band & (col >= jnp.where(first, w, 0))
    hrow = lax.broadcasted_iota(jnp.int32, (4 * w, 1), 0) // w
    sinks = []
    for g in range(A_KV_HEADS):
        sink = jnp.zeros((4 * w, 1), F32)
        for j in range(4):
            sink = jnp.where(hrow == j, sink_ref[4 * g + j], sink)
        sinks.append(sink)
    zero = jnp.zeros((), BF16)
    blocks = []
    for b in range(q_ref.shape[0] // w):
        segs = []
        for g in range(A_KV_HEADS):
            parts = []
            for sgm in range(2):
                seg = q_ref[b * w:(b + 1) * w, (2 * g + sgm) * LANES:(2 * g + sgm + 1) * LANES]
                parts += [jnp.where(lane_lo, seg, zero), jnp.where(lane_lo, zero, seg)]
            qs = jnp.concatenate(parts, axis=0)
            s = _dot_nt(qs, kk[g][b * w:(b + 2) * w]) * (A_HEAD_DIM ** -0.5)
            p = _sink_softmax(s, band_first if b == 0 else band, sinks[g]).astype(BF16)
            o = _dot(p, vv[g][b * w:(b + 2) * w])
            segs += [jnp.where(lane_lo, o[(2 * sgm) * w:(2 * sgm + 1) * w],
                               o[(2 * sgm + 1) * w:(2 * sgm + 2) * w]).astype(BF16) for sgm in range(2)]
        blocks.append(jnp.concatenate(segs, axis=1))
    return jnp.concatenate(blocks, axis=0)


def _swa_sample_kernel(sink_ref, q_ref, ckt_ref, cvt_ref, kn_ref, vn_ref, o_ref, *, nb, t):
    w = ckt_ref.shape[3]
    rows = q_ref.shape[1] // A_HEADS
    gq = rows * (A_HEADS // A_KV_HEADS)
    r = lax.broadcasted_iota(jnp.int32, (gq, w), 0)
    c = lax.broadcasted_iota(jnp.int32, (gq, w), 1)
    valid_c = c > r % rows
    rn = lax.broadcasted_iota(jnp.int32, (gq, kn_ref.shape[1]), 0)
    cn = lax.broadcasted_iota(jnp.int32, (gq, kn_ref.shape[1]), 1)
    tn = cn // A_KV_HEADS
    causal_n = (tn <= rn % rows) & (tn < t)
    hrow = lax.broadcasted_iota(jnp.int32, (gq, 1), 0) // rows
    valid_n, sink = [], []
    for g in range(A_KV_HEADS):
        valid_n.append(causal_n & (cn % A_KV_HEADS == g))
        sk = jnp.zeros((gq, 1), F32)
        for j in range(A_HEADS // A_KV_HEADS):
            sk = jnp.where(hrow == j, sink_ref[g * (A_HEADS // A_KV_HEADS) + j], sk)
        sink.append(sk)
    probs = [(b, g) for b in range(nb) for g in range(A_KV_HEADS)]
    scale = A_HEAD_DIM ** -0.5
    qs = [q_ref[b, g * gq:(g + 1) * gq, :] for b, g in probs]
    scs = [jnp.where(valid_c, _dot(q, ckt_ref[b, g].astype(BF16)) * scale, -jnp.inf)
           for q, (b, g) in zip(qs, probs)]
    sns = [jnp.where(valid_n[g], _dot_nt(q, kn_ref[b]) * scale, -jnp.inf) for q, (b, g) in zip(qs, probs)]
    ms = [jnp.maximum(jnp.maximum(jnp.max(sc, -1, keepdims=True), jnp.max(sn, -1, keepdims=True)), sink[g])
          for sc, sn, (b, g) in zip(scs, sns, probs)]
    pcs = [jnp.exp(sc - m) for sc, m in zip(scs, ms)]
    pns = [jnp.exp(sn - m) for sn, m in zip(sns, ms)]
    invs = [1.0 / (jnp.sum(pc, -1, keepdims=True) + jnp.sum(pn, -1, keepdims=True) + jnp.exp(sink[g] - m))
            for pc, pn, m, (b, g) in zip(pcs, pns, ms, probs)]
    for pc, pn, inv, (b, g) in zip(pcs, pns, invs, probs):
        o_ref[b, g * gq:(g + 1) * gq, :] = (_dot_nt((pc * inv).astype(BF16), cvt_ref[b, g].astype(BF16))
                                            + _dot((pn * inv).astype(BF16), vn_ref[b]))


def _swa_sample(q, ckt, cvt, kn, vn, sinks, t, nb=8):
    bsz, nq, _ = q.shape
    blk = lambda a: pl.BlockSpec((nb,) + a.shape[1:], lambda i: (i,) + (0,) * (a.ndim - 1))
    return pl.pallas_call(
        functools.partial(_swa_sample_kernel, nb=nb, t=t),
        grid=(bsz // nb,),
        in_specs=[pl.BlockSpec(memory_space=pltpu.SMEM), blk(q), blk(ckt), blk(cvt), blk(kn), blk(vn)],
        out_specs=blk(q),
        out_shape=jax.ShapeDtypeStruct(q.shape, F32),
        compiler_params=_cparams("parallel"),
        name="swa_sample",
    )(sinks, q, ckt, cvt, kn, vn)


def _lane_bcast(x, lane):
    return jnp.broadcast_to(x[:, lane:lane + 1], (x.shape[0], LANES))


def _cumsum_rows(x, block):
    c = x.shape[0]
    rowi = lax.broadcasted_iota(jnp.int32, (c, c), 0)
    coli = lax.broadcasted_iota(jnp.int32, (c, c), 1)
    ones = (((rowi // block) == (coli // block)) & (rowi >= coli)).astype(BF16)
    hi = x.astype(BF16)
    r1 = x - hi.astype(F32)
    mid = r1.astype(BF16)
    lo = (r1 - mid.astype(F32)).astype(BF16)
    s = _dot(ones, jnp.concatenate([hi, mid, lo], axis=1))
    return s[:, :LANES] + s[:, LANES:2 * LANES] + s[:, 2 * LANES:]


def _l2n(t):
    return t * lax.rsqrt(jnp.sum(t * t, axis=-1, keepdims=True) + EPS)


def _gates(ab, alog, dtb):
    x = ab + dtb
    sp = jnp.maximum(x, 0.0) + jnp.log1p(jnp.exp(-jnp.abs(x)))
    return -jnp.exp(alog) * sp, jax.nn.sigmoid(ab)


def _merge_masks(c, top):
    rowi = lax.broadcasted_iota(jnp.int32, (c, c), 0)
    coli = lax.broadcasted_iota(jnp.int32, (c, c), 1)
    masks = []
    s = 1
    while s < top:
        rb, cb = rowi // s, coli // s
        masks.append(((rb // 2) == (cb // 2)) & ((rb % 2) == 1) & ((cb % 2) == 0))
        s *= 2
    return masks


def _each(f, *lists):
    return [f(*t) for t in zip(*lists)]


def _drain(stages):
    try:
        while True:
            next(stages)
    except StopIteration as done:
        return done.value


def _interleave(*staged):
    live = list(staged)
    while live:
        for item in tuple(live):
            stages, per_turn = item
            try:
                for _ in range(per_turn):
                    next(stages)
            except StopIteration:
                live.remove(item)


def _chunk_local(qs, ks, vs, gcols, grows, betas, tril, merges):
    c = qs[0].shape[0]
    decays = _each(lambda gc, gr: jnp.exp(jnp.minimum(gc - gr, 0.0)), gcols, grows)
    kbs = _each(lambda k, b: k * b, ks, betas)
    yield
    ms = _each(lambda q, kb, k: _dot_nt(jnp.concatenate([q, kb], axis=0).astype(BF16), k.astype(BF16)),
               qs, kbs, ks)
    yield
    qks = _each(lambda m, d: jnp.where(tril, m[:c] * d, 0.0), ms, decays)
    a = _each(lambda m, d: m[c:] * d, ms, decays)
    eye = (lax.broadcasted_iota(jnp.int32, (c, c), 0) == lax.broadcasted_iota(jnp.int32, (c, c), 1)).astype(F32)
    ts = [eye - jnp.where(merges[0], x, 0.0) if merges else eye for x in a]
    yield
    for off in merges[1:]:
        tbs = [t.astype(BF16) for t in ts]
        zs = _each(lambda x, tb: _dot(jnp.where(off, x, 0.0).astype(BF16), tb), a, tbs)
        yield
        ts = _each(lambda t, tb, z: t - _dot(tb, z.astype(BF16)), ts, tbs, zs)
        yield
    ns = [t - eye for t in ts]
    egs = [jnp.exp(gc) for gc in gcols]
    rhss = _each(lambda v, b, kb, eg: jnp.concatenate([v * b, kb * eg], axis=1), vs, betas, kbs, egs)
    yield
    uws = _each(lambda r, n: r + _dot(n.astype(BF16), r.astype(BF16)), rhss, ns)
    yield
    return ([x[:, :LANES] for x in uws], [x[:, LANES:] for x in uws], qks,
            _each(lambda q, eg: q * eg, qs, egs))


def _gdn_tile(xp_ref, dz_ref, ab_ref, cw_ref, alog_ref, dtb_ref, ng_ref, y_ref, s_ref, nc):
    c = CHUNK
    pad = SUBLANES
    cw = cw_ref[...]
    g, beta = _gates(ab_ref[...], alog_ref[...], dtb_ref[...])
    rowi = lax.broadcasted_iota(jnp.int32, (c, c), 0)
    coli = lax.broadcasted_iota(jnp.int32, (c, c), 1)
    tril = rowi >= coli
    merges = _merge_masks(c, c)
    ng = ng_ref[...]
    heads = range(DN_HEADS)
    probs = [(ci, h) for ci in range(nc) for h in heads]
    rows = lambda ci: slice(ci * c, (ci + 1) * c)
    lanes = lambda part, h: slice(part * DN_WIDTH + h * LANES, part * DN_WIDTH + (h + 1) * LANES)
    yield
    qs, ks, vs, gcols, grows, betas = [], [], [], [], [], []
    for ci in range(nc):
        conv = xp_ref[pad + ci * c:pad + (ci + 1) * c, :] * cw[DN_CONV - 1:DN_CONV]
        for j in range(DN_CONV - 1):
            o = pad - (DN_CONV - 1) + j + ci * c
            conv = conv + xp_ref[o:o + c, :] * cw[j:j + 1]
        conv = _silu(conv)
        gc = _cumsum_rows(g[rows(ci)], c)
        gct = gc.T
        qs += [_l2n(conv[:, lanes(0, h)]) * (DN_HEAD_DIM ** -0.5) for h in heads]
        ks += [_l2n(conv[:, lanes(1, h)]) for h in heads]
        vs += [conv[:, lanes(2, h)] for h in heads]
        gcols += [_lane_bcast(gc, h) for h in heads]
        grows += [gct[h:h + 1, :] for h in heads]
        betas += [_lane_bcast(beta[rows(ci)], DN_HEADS + h) for h in heads]
        yield
    us, ws, qks, qds = yield from _chunk_local(qs, ks, vs, gcols, grows, betas, tril, merges)
    glasts = [gc[c - 1:c, :] for gc in gcols]
    kdts = _each(lambda k, gl, gc: (k * jnp.exp(gl - gc)).T, ks, glasts, gcols)
    wqs = _each(lambda w, qd: jnp.concatenate([w, qd], axis=0).astype(BF16), ws, qds)
    qkks = _each(lambda qk, kdt: jnp.concatenate([qk, kdt], axis=0).astype(BF16), qks, kdts)
    yield
    ss = [s_ref[h] for h in heads]
    for ci in range(nc):
        pr = [ci * DN_HEADS + h for h in heads]
        r2s = [_dot(wqs[p], s.astype(BF16)) for p, s in zip(pr, ss)]
        vnews = [us[p] - r2[:c] for p, r2 in zip(pr, r2s)]
        yield
        r3s = [_dot(qkks[p], vn.astype(BF16)) for p, vn in zip(pr, vnews)]
        ss = [s * jnp.exp(glasts[p]) + r3[c:] for p, s, r3 in zip(pr, ss, r3s)]
        for h in heads:
            o = r2s[h][c:] + r3s[h][:c]
            y_ref[rows(ci), lanes(0, h)] = (_rms(o, ng) * _silu(dz_ref[rows(ci), lanes(0, h)])).astype(BF16)
        yield
    for h in heads:
        s_ref[h] = ss[h]


def _proj_gdn_kernel(x_ref, g_ref, c_ref, s1_ref, s2_ref, wqkv_ref, wd_ref, wab_ref, wg_ref,
                     cw_ref, alog_ref, dtb_ref, ng_ref,
                     q_ref, k_ref, v_ref, ga_ref, gb_ref, y_ref, sout_ref, tail_ref,
                     xp_ref, dz_ref, ab_ref, carry_ref, s_ref, *, nt, nc):
    i = pl.program_id(0)
    pad = SUBLANES
    r = nc * CHUNK
    slot_a = i % 2
    slot_b = 1 - slot_a

    @pl.when(i == 0)
    def _():
        xp_ref[1] = jnp.zeros(xp_ref.shape[1:], F32)
        dz_ref[1] = jnp.zeros(dz_ref.shape[1:], F32)
        ab_ref[1] = jnp.zeros(ab_ref.shape[1:], F32)

    @pl.when(jnp.maximum(i - 1, 0) % nt == 0)
    def _():
        carry_ref[...] = jnp.zeros_like(carry_ref)
        s_ref[...] = jnp.zeros_like(s_ref)

    xpb_ref = xp_ref.at[slot_b]
    xpb_ref[0:pad, :] = carry_ref[...]
    _interleave(
        (_gdn_tile(xpb_ref, dz_ref.at[slot_b], ab_ref.at[slot_b], cw_ref, alog_ref, dtb_ref, ng_ref,
                   y_ref, s_ref, nc), 1),
        (_proj_tile(x_ref, g_ref, c_ref, s1_ref, s2_ref, wqkv_ref, wd_ref, wab_ref, wg_ref, q_ref, k_ref, v_ref,
                    xp_ref.at[slot_a], dz_ref.at[slot_a], ab_ref.at[slot_a], ga_ref, gb_ref, pad), 1))
    carry_ref[...] = xpb_ref[r:r + pad, :]
    tail_ref[0] = carry_ref[...]
    sout_ref[0] = s_ref[...]


def _proj_gdn(x, g, tabs, wqkv, wd, wab, wg, cw, alog, dtb, ng, nseq, tm):
    m = x.shape[0]
    nt = m // nseq // tm
    last = m // tm - 1
    nc = tm // CHUNK
    proj = lambda i: jnp.minimum(i, last)
    gdn = lambda i: jnp.maximum(i - 1, 0)
    prow = lambda w: pl.BlockSpec((tm, w), lambda i: (proj(i), 0))
    tab = pl.BlockSpec((tm, LANES), lambda i: (proj(i) % nt, 0))
    per_seq = lambda *dims: pl.BlockSpec((1,) + dims, lambda i: (gdn(i) // nt,) + (0,) * len(dims))
    state = (DN_HEADS, DN_HEAD_DIM, DN_HEAD_DIM)
    bufs = [pltpu.VMEM((2, tm + SUBLANES, 3 * DN_WIDTH), F32), pltpu.VMEM((2, tm, DN_WIDTH), F32),
            pltpu.VMEM((2, tm, LANES), F32)]
    return pl.pallas_call(
        functools.partial(_proj_gdn_kernel, nt=nt, nc=nc),
        grid=(m // tm + 1,),
        in_specs=[prow(D_MODEL), _resident((1, D_MODEL)), tab, tab, tab,
                  _resident(wqkv.shape), _resident(wd.shape), _resident(wab.shape), _resident(wg.shape),
                  _resident(cw.shape), _resident((1, LANES)), _resident((1, LANES)), _resident((1, LANES))],
        out_specs=[prow(A_WIDTH), prow(LANES), prow(LANES), prow(D_MODEL), prow(D_MODEL),
                   pl.BlockSpec((tm, DN_WIDTH), lambda i: (gdn(i), 0)), per_seq(*state),
                   per_seq(SUBLANES, 3 * DN_WIDTH)],
        out_shape=[jax.ShapeDtypeStruct((m, A_WIDTH), BF16), jax.ShapeDtypeStruct((m, LANES), F32),
                   jax.ShapeDtypeStruct((m, LANES), F32), jax.ShapeDtypeStruct((m, D_MODEL), F32),
                   jax.ShapeDtypeStruct((m, D_MODEL), F32), jax.ShapeDtypeStruct((m, DN_WIDTH), BF16),
                   jax.ShapeDtypeStruct((nseq,) + state, F32),
                   jax.ShapeDtypeStruct((nseq, SUBLANES, 3 * DN_WIDTH), F32)],
        scratch_shapes=bufs + [pltpu.VMEM((SUBLANES, 3 * DN_WIDTH), F32), pltpu.VMEM(state, F32)],
        compiler_params=_cparams("arbitrary"),
        name="proj_gdn",
    )(x, g, *tabs, wqkv, wd, wab, wg, cw, alog, dtb, ng)


def _gdn_sample_kernel(buf_ref, x_ref, dz_ref, ab_ref, s0_ref, cw_ref, alog_ref, dtb_ref, ng_ref,
                       y_ref, sout_ref, xp_ref, ab16_ref):
    t = x_ref.shape[1]
    c = CHUNK
    sr = GDN_SAMPLE_ROWS
    nb = c // sr
    pad = SUBLANES
    hist = DN_CONV - 1
    xp_ref[...] = jnp.zeros_like(xp_ref)
    ab16_ref[...] = jnp.zeros_like(ab16_ref)
    for b in range(nb):
        xp_ref[pad + b * sr - hist:pad + b * sr, :] = buf_ref[b]
        xp_ref[pad + b * sr:pad + b * sr + t, :] = x_ref[b]
        ab16_ref[b * sr:b * sr + t, :] = ab_ref[b]
    cw = cw_ref[...]
    conv = xp_ref[pad:pad + c, :] * cw[hist:hist + 1]
    for j in range(hist):
        conv = conv + xp_ref[pad - hist + j:pad - hist + j + c, :] * cw[j:j + 1]
    conv = _silu(conv)
    live = (lax.broadcasted_iota(jnp.int32, (c, LANES), 0) % sr) < t
    g, beta = _gates(ab16_ref[...], alog_ref[...], dtb_ref[...])
    g = jnp.where(live, g, 0.0)
    beta = jnp.where(live, beta, 0.0)
    gc = _cumsum_rows(g, sr)
    gct = gc.T
    rowi = lax.broadcasted_iota(jnp.int32, (c, c), 0)
    coli = lax.broadcasted_iota(jnp.int32, (c, c), 1)
    tril = ((rowi // sr) == (coli // sr)) & (rowi >= coli)
    merges = _merge_masks(c, pl.next_power_of_2(t))
    ng = ng_ref[...]
    rowb = lax.broadcasted_iota(jnp.int32, (c, LANES), 0) // sr
    heads = range(DN_HEADS)
    lanes = lambda part, h: slice(part * DN_WIDTH + h * LANES, part * DN_WIDTH + (h + 1) * LANES)
    ks = [_l2n(conv[:, lanes(1, h)]) for h in heads]
    gcols = [_lane_bcast(gc, h) for h in heads]
    us, ws, qks, qds = _drain(_chunk_local(
        [_l2n(conv[:, lanes(0, h)]) * (DN_HEAD_DIM ** -0.5) for h in heads], ks,
        [conv[:, lanes(2, h)] for h in heads], gcols, [gct[h:h + 1, :] for h in heads],
        [_lane_bcast(beta, DN_HEADS + h) for h in heads], tril, merges))
    seqs = range(nb)
    rows = lambda b: slice(b * sr, (b + 1) * sr)
    s0s = [[s0_ref[b, h] for b in seqs] for h in heads]
    r2s = [[_dot(jnp.concatenate([ws[h][rows(b)], qds[h][rows(b)]], axis=0).astype(BF16), s0s[h][b].astype(BF16))
            for b in seqs] for h in heads]
    vnews = [jnp.concatenate([us[h][rows(b)] - r2s[h][b][:sr] for b in seqs], axis=0) for h in heads]
    os_ = [_rms(jnp.concatenate([r2s[h][b][sr:] for b in seqs], axis=0)
                + _dot(qks[h].astype(BF16), vnews[h].astype(BF16)), ng) for h in heads]
    for h in heads:
        for b in seqs:
            y_ref[b, :, lanes(0, h)] = os_[h][b * sr:b * sr + t] * _silu(dz_ref[b, :, lanes(0, h)])
    glasts = [jnp.concatenate([jnp.broadcast_to(gcols[h][(b + 1) * sr - 1:(b + 1) * sr, :], (sr, LANES))
                               for b in seqs], axis=0) for h in heads]
    kdts = [(ks[h] * jnp.exp(glasts[h] - gcols[h])).T.astype(BF16) for h in heads]
    for h in heads:
        for b in seqs:
            vb = jnp.where(rowb == b, vnews[h], 0.0).astype(BF16)
            sout_ref[b, h] = s0s[h][b] * jnp.exp(glasts[h][b * sr:b * sr + 1, :]) + _dot(kdts[h], vb)


def _gdn_sample(buf, x, dz, ab, s0, cw, alog, dtb, ng):
    bsz, t, _ = x.shape
    assert t + DN_CONV - 1 <= GDN_SAMPLE_ROWS, "too many new tokens for one row tile per sequence"
    nb = CHUNK // GDN_SAMPLE_ROWS
    seq = lambda a: pl.BlockSpec((nb,) + a.shape[1:], lambda i: (i,) + (0,) * (a.ndim - 1))
    return pl.pallas_call(
        _gdn_sample_kernel,
        grid=(bsz // nb,),
        in_specs=[seq(buf), seq(x), seq(dz), seq(ab), seq(s0), _resident(cw.shape),
                  _resident((1, LANES)), _resident((1, LANES)), _resident((1, LANES))],
        out_specs=[seq(dz), seq(s0)],
        out_shape=[jax.ShapeDtypeStruct(dz.shape, F32), jax.ShapeDtypeStruct(s0.shape, F32)],
        scratch_shapes=[pltpu.VMEM((CHUNK + SUBLANES, 3 * DN_WIDTH), F32), pltpu.VMEM((CHUNK, LANES), F32)],
        compiler_params=_cparams("parallel"),
        name="gdn_sample",
    )(buf, x, dz, ab, s0, cw, alog, dtb, ng)


def _merge_kernel(x_ref, *rest, attend):
    if attend:
        sink_ref, q_ref, kp_ref, kc_ref, vp_ref, vc_ref, *rest = rest
        ya = _swa_tile(sink_ref, q_ref, jnp.concatenate([kp_ref[...], kc_ref[...]], axis=0),
                       jnp.concatenate([vp_ref[...], vc_ref[...]], axis=0), pl.program_id(1) == 0)
    else:
        ya_ref, *rest = rest
        ya = ya_ref[...]
    yb_ref, ga_ref, gb_ref, wa_ref, wb_ref, wo_ref, g_ref, wq_ref, h_ref, hq_ref = rest
    mix = (jax.nn.sigmoid(ga_ref[...]) * _dot(ya.astype(BF16), wa_ref[...])
           + jax.nn.sigmoid(gb_ref[...]) * _dot(yb_ref[...].astype(BF16), wb_ref[...]))
    h = x_ref[...] + _dot(mix.astype(BF16), wo_ref[...])
    h_ref[...] = h
    hq_ref[...] = _dot(_rms(h, g_ref[...]).astype(BF16), wq_ref[...]).astype(BF16)


def _merge(x, attn, yb, ga, gb, wa, wb, wo, g, wq, nseq, tm):
    m = x.shape[0]
    nt = m // nseq // tm
    row = lambda w: pl.BlockSpec((tm, w), lambda s, j: (s * nt + j, 0))
    attend = isinstance(attn, tuple)
    if attend:
        sinks, q, k, v = attn
        per = tm // WINDOW
        prev = pl.BlockSpec((WINDOW, A_KV_WIDTH), lambda s, j: (jnp.maximum((s * nt + j) * per - 1, 0), 0))
        attn_args = [sinks, q, k, k, v, v]
        attn_specs = [pl.BlockSpec(memory_space=pltpu.SMEM), row(A_WIDTH), prev, row(A_KV_WIDTH),
                      prev, row(A_KV_WIDTH)]
    else:
        attn_args, attn_specs = [attn], [row(A_WIDTH)]
    return pl.pallas_call(
        functools.partial(_merge_kernel, attend=attend),
        grid=(nseq, nt),
        in_specs=[row(D_MODEL), *attn_specs, row(DN_WIDTH), row(D_MODEL), row(D_MODEL),
                  _resident(wa.shape), _resident(wb.shape), _resident(wo.shape),
                  _resident((1, D_MODEL)), _resident(wq.shape)],
        out_specs=[row(D_MODEL), row(X_WIDTH)],
        out_shape=[jax.ShapeDtypeStruct((m, D_MODEL), F32), jax.ShapeDtypeStruct((m, X_WIDTH), BF16)],
        compiler_params=_cparams("parallel", "parallel"),
        name="merge",
    )(x, *attn_args, yb, ga, gb, wa, wb, wo, g, wq)


def _memkv_kernel(m_ref, g_ref, w_ref, k_ref, v_ref):
    z = _dot(_rms(m_ref[...], g_ref[...]).astype(BF16), w_ref[...])
    k_ref[...] = z[:, :X_WIDTH]
    v_ref[...] = z[:, X_WIDTH:]


def _memkv(mem, g, w, tm):
    m = mem.shape[0]
    row = lambda wd: pl.BlockSpec((tm, wd), lambda i: (i, 0))
    return pl.pallas_call(
        _memkv_kernel,
        grid=(m // tm,),
        in_specs=[row(D_MODEL), _resident((1, D_MODEL)), _resident(w.shape)],
        out_specs=[row(X_WIDTH), row(X_WIDTH)],
        out_shape=[jax.ShapeDtypeStruct((m, X_WIDTH), F32)] * 2,
        compiler_params=_cparams("parallel"),
        name="mem_kv",
    )(mem, g, w)


def _softmax_rows(s):
    e = jnp.exp(s - jnp.max(s, axis=-1, keepdims=True))
    return e * (1.0 / jnp.sum(e, axis=-1, keepdims=True))


def _xattn_sample_kernel(hq_ref, mk_ref, mv_ref, o_ref, *, nb):
    rows = hq_ref.shape[1]
    nmh = mk_ref.shape[1]
    qh = lax.broadcasted_iota(jnp.int32, (X_HEADS * rows, nmh), 0) // rows
    mh = lax.broadcasted_iota(jnp.int32, (X_HEADS * rows, nmh), 1) % X_HEADS
    same = qh == mh
    seqs = range(nb)
    qs = [jnp.concatenate([hq_ref[b, :, h * LANES:(h + 1) * LANES] for h in range(X_HEADS)], axis=0)
          for b in seqs]
    ss = [_dot_nt(qs[b], mk_ref[b].astype(BF16)) * (X_HEAD_DIM ** -0.5) for b in seqs]
    ps = [_softmax_rows(jnp.where(same, s, -jnp.inf)).astype(BF16) for s in ss]
    os_ = [_dot(ps[b], mv_ref[b].astype(BF16)) for b in seqs]
    for b in seqs:
        for h in range(X_HEADS):
            o_ref[b, :, h * LANES:(h + 1) * LANES] = os_[b][h * rows:(h + 1) * rows].astype(BF16)


def _xattn_sample(hq, mk, mv, nb=8):
    bsz, rows, _ = hq.shape
    q = pl.BlockSpec((nb, rows, X_WIDTH), lambda i: (i, 0, 0))
    mem = pl.BlockSpec((nb, mk.shape[1], X_HEAD_DIM), lambda i: (i, 0, 0))
    return pl.pallas_call(
        functools.partial(_xattn_sample_kernel, nb=nb),
        grid=(bsz // nb,),
        in_specs=[q, mem, mem],
        out_specs=q,
        out_shape=jax.ShapeDtypeStruct(hq.shape, BF16),
        compiler_params=_cparams("parallel"),
        name="xattn_sample",
    )(hq, mk, mv)


def _xattn_tile(hq_ref, mk_ref, mv_ref):
    outs = []
    for h in range(X_HEADS):
        hs = slice(h * LANES, (h + 1) * LANES)
        s = _dot_nt(hq_ref[:, hs], mk_ref[:, hs].astype(BF16)) * (X_HEAD_DIM ** -0.5)
        outs.append(_dot(_softmax_rows(s).astype(BF16), mv_ref[:, hs].astype(BF16)).astype(BF16))
    return jnp.concatenate(outs, axis=1)


def _ffn_kernel(h_ref, *rest, stride, padc, fc, attend):
    if attend:
        hq_ref, mk_ref, mv_ref, *rest = rest
        xo = _xattn_tile(hq_ref, mk_ref, mv_ref)
    else:
        xo_ref, *rest = rest
        xo = xo_ref[...]
    cin_ref, wxo_ref, g_ref, wu_ref, wv_ref, cw_ref, wd_ref, fg_ref, y_ref, tail_ref, xp_ref, acc_ref = rest
    tm = h_ref.shape[0]

    @pl.when(pl.program_id(1) == 0)
    def _():
        tail_ref[...] = cin_ref[...]

    h = h_ref[...] + _dot(xo, wxo_ref[...])
    acc_ref[...] = h
    hn = _rms(h, g_ref[...]).astype(BF16)
    for ci in range(D_FF // fc):
        cs = slice(ci * fc, (ci + 1) * fc)
        u = _dot(hn, wu_ref[:, cs])
        gate = _dot(hn, wv_ref[:, cs])
        xp_ref[0:padc, :] = tail_ref[0, :, cs]
        xp_ref[padc:padc + tm, :] = u
        cw = cw_ref[:, cs]
        uc = u * cw[2:3]
        for j in range(FFN_CONV - 1):
            o = padc - (FFN_CONV - 1 - j) * stride
            uc = uc + xp_ref[o:o + tm, :] * cw[j:j + 1]
        tail_ref[0, :, cs] = xp_ref[tm:tm + padc, :]
        acc_ref[...] += _dot((_silu(uc) * gate).astype(BF16), wd_ref[cs, :])
    y_ref[...] = _rms(acc_ref[...], fg_ref[...])


def _ffn(h, xq, mem, cin, wxo, g, wup, cw, wd, fg, tm, stride, fc=D_FF // 2):
    m = h.shape[0]
    half = lambda c: pl.BlockSpec((D_MODEL, D_FF), lambda *_: (0, c), pipeline_mode=pl.Buffered(1))
    nseq, padc, _ = cin.shape
    nt = m // nseq // tm
    row = lambda w: pl.BlockSpec((tm, w), lambda s, j: (s * nt + j, 0))
    car = pl.BlockSpec((1, padc, D_FF), lambda s, j: (s, 0, 0))
    mem = list(mem or ())
    mem_specs = [pl.BlockSpec((a.shape[0] // nseq, X_WIDTH), lambda s, j: (s, 0)) for a in mem]
    return pl.pallas_call(
        functools.partial(_ffn_kernel, stride=stride, padc=padc, fc=fc, attend=bool(mem)),
        grid=(nseq, nt),
        in_specs=[row(D_MODEL), row(X_WIDTH), *mem_specs, car, _resident(wxo.shape), _resident((1, D_MODEL)),
                  half(0), half(1), _resident(cw.shape), _resident(wd.shape), _resident((1, D_MODEL))],
        out_specs=[row(D_MODEL), car],
        out_shape=[jax.ShapeDtypeStruct((m, D_MODEL), F32), jax.ShapeDtypeStruct(cin.shape, F32)],
        scratch_shapes=[pltpu.VMEM((tm + padc, fc), F32), pltpu.VMEM((tm, D_MODEL), F32)],
        compiler_params=_cparams("parallel", "arbitrary"),
        name="ffn",
    )(h, xq, *mem, cin, wxo, g, wup, wup, cw, wd, fg)


def _lane_row(vec, offset=0):
    return jnp.zeros((1, LANES), F32).at[0, offset:offset + vec.shape[0]].set(vec.astype(F32))


def _prep(p):
    w_in = p["w_in"]
    o = 0
    cuts = {}
    for name, wdt in (("qkv", A_WIDTH + 2 * A_KV_WIDTH), ("d", 4 * DN_WIDTH), ("ab", 2 * DN_HEADS),
                      ("g", 2 * D_MODEL)):
        cuts[name] = w_in[:, o:o + wdt]
        o += wdt
    w = {
        "wqkv": cuts["qkv"].astype(BF16),
        "wd": cuts["d"].astype(BF16),
        "wab": jnp.pad(cuts["ab"], ((0, 0), (0, LANES - 2 * DN_HEADS))).astype(BF16),
        "wg": cuts["g"].astype(BF16),
        "norm_mix_g": p["norm_mix_g"].reshape(1, D_MODEL),
        "dn_conv_w": p["dn_conv_w"],
        "alog": _lane_row(p["dn_a_log"]),
        "dtb": _lane_row(p["dn_dt_bias"]),
        "dn_norm_g": p["dn_norm_g"].reshape(1, DN_HEAD_DIM),
        "sinks": p["sinks"].astype(F32),
        "w_br_a": p["w_br_a"].astype(BF16),
        "w_br_b": p["w_br_b"].astype(BF16),
        "w_mix_out": p["w_mix_out"].astype(BF16),
        "norm_x_g": p["norm_x_g"].reshape(1, D_MODEL),
        "w_xq": p["w_xq"].astype(BF16),
        "w_xo": p["w_xo"].astype(BF16),
        "norm_ffn_g": p["norm_ffn_g"].reshape(1, D_MODEL),
        "w_up": p["w_up"].astype(BF16),
        "ffn_conv_w": p["ffn_conv_w"],
        "w_down": p["w_down"].astype(BF16),
        "final_norm_g": p["final_norm_g"].reshape(1, D_MODEL),
    }
    return w


def _tile(m, pref):
    return pref if m % pref == 0 else m


def _prompt_layer(x, mem, norm_mem_g, w_xkv, w):
    bsz, seqlen, _ = x.shape
    m = bsz * seqlen
    x2 = x.reshape(m, D_MODEL)
    tm = _tile(seqlen, 512)
    tabs = _rope_tables(jnp.arange(seqlen, dtype=jnp.int32))
    q, k, v, ga, gb, yb, dn_state, dn_tail = _proj_gdn(
        x2, w["norm_mix_g"], tabs, w["wqkv"], w["wd"], w["wab"], w["wg"], w["dn_conv_w"], w["alog"], w["dtb"],
        w["dn_norm_g"], bsz, tm)
    h, hq = _merge(x2, (w["sinks"], q, k, v), yb, ga, gb, w["w_br_a"], w["w_br_b"], w["w_mix_out"],
                   w["norm_x_g"], w["w_xq"], bsz, tm)
    nm = mem.shape[1]
    mk, mv = _memkv(mem.reshape(bsz * nm, D_MODEL), norm_mem_g.reshape(1, D_MODEL), w_xkv.astype(BF16),
                    _tile(bsz * nm, 512))
    cin = jnp.zeros((bsz, SUBLANES, D_FF), F32)
    y, tail = _ffn(h, hq, (mk, mv), cin, w["w_xo"], w["norm_ffn_g"], w["w_up"], w["ffn_conv_w"],
                   w["w_down"], w["final_norm_g"], tm, 1)
    wb = min(WINDOW, seqlen)
    new = (
        k.reshape(bsz, seqlen, A_KV_HEADS, A_HEAD_DIM)[:, seqlen - wb:],
        v.reshape(bsz, seqlen, A_KV_HEADS, A_HEAD_DIM)[:, seqlen - wb:],
        dn_tail[:, SUBLANES - (DN_CONV - 1):],
        dn_state,
        mk.reshape(bsz, nm, X_HEADS, X_HEAD_DIM),
        mv.reshape(bsz, nm, X_HEADS, X_HEAD_DIM),
        tail[:, SUBLANES - (FFN_CONV - 1):],
    )
    return y.reshape(bsz, seqlen, D_MODEL), new


def _sample_layer(x, pos0, win_k, win_v, dn_buf, dn_state, mem_k, mem_v, ffn_buf, w):
    bsz, t, _ = x.shape
    m = bsz * t
    sr = SAMPLE_ROWS
    x2 = x.reshape(m, D_MODEL)
    tabs = _rope_tables(jnp.tile(pos0 + jnp.arange(t, dtype=jnp.int32), bsz))
    q, k, v, dqkv, dz, ab, ga, gb = _proj(x2, w["norm_mix_g"], tabs, w["wqkv"], w["wd"], w["wab"], w["wg"], m)

    wlen = win_k.shape[1]
    qh = jnp.transpose(q.reshape(bsz, t, A_HEADS, A_HEAD_DIM), (0, 2, 1, 3)).reshape(bsz, A_HEADS * t, A_HEAD_DIM)
    padrows = lambda a: jnp.pad(a.reshape(bsz, t, -1), ((0, 0), (0, sr - t), (0, 0)))
    newrows = lambda a: jnp.pad(a.reshape(bsz, t * A_KV_HEADS, A_HEAD_DIM),
                                ((0, 0), (0, sr - t * A_KV_HEADS), (0, 0))).astype(BF16)
    kv_major = lambda a: jnp.transpose(a, (0, 2, 3, 1))
    oh = _swa_sample(qh, kv_major(win_k), kv_major(win_v), newrows(k), newrows(v), w["sinks"], t)
    ya = jnp.transpose(oh.reshape(bsz, A_HEADS, t, A_HEAD_DIM), (0, 2, 1, 3)).reshape(m, A_WIDTH)
    new_wk = jnp.concatenate([win_k, k.reshape(bsz, t, A_KV_HEADS, A_HEAD_DIM)], axis=1)[:, -wlen:]
    new_wv = jnp.concatenate([win_v, v.reshape(bsz, t, A_KV_HEADS, A_HEAD_DIM)], axis=1)[:, -wlen:]

    dqkv3 = dqkv.reshape(bsz, t, 3 * DN_WIDTH)
    yb, new_s = _gdn_sample(dn_buf, dqkv3, dz.reshape(bsz, t, DN_WIDTH), ab.reshape(bsz, t, LANES), dn_state,
                            w["dn_conv_w"], w["alog"], w["dtb"], w["dn_norm_g"])
    yb = yb.reshape(m, DN_WIDTH)
    new_dn_buf = jnp.concatenate([dn_buf, dqkv3], axis=1)[:, -(DN_CONV - 1):]

    h, hq = _merge(x2, ya, yb, ga, gb, w["w_br_a"], w["w_br_b"], w["w_mix_out"], w["norm_x_g"], w["w_xq"], 1, m)
    nm = mem_k.shape[1]
    xo = _xattn_sample(padrows(hq), mem_k.reshape(bsz, nm * X_HEADS, X_HEAD_DIM),
                       mem_v.reshape(bsz, nm * X_HEADS, X_HEAD_DIM))
    xo = xo[:, :t]

    tmaj = lambda a: jnp.transpose(a.reshape(bsz, t, -1), (1, 0, 2)).reshape(m, -1)
    cin = jnp.transpose(ffn_buf, (1, 0, 2)).reshape(1, (FFN_CONV - 1) * bsz, D_FF)
    y, tail = _ffn(tmaj(h), tmaj(xo), None, cin, w["w_xo"], w["norm_ffn_g"], w["w_up"], w["ffn_conv_w"],
                   w["w_down"], w["final_norm_g"], m, bsz)
    y = jnp.transpose(y.reshape(t, bsz, D_MODEL), (1, 0, 2))
    new_ffn = jnp.transpose(tail.reshape(FFN_CONV - 1, bsz, D_FF), (1, 0, 2))
    return y, (new_wk, new_wv, new_dn_buf, new_s, new_ffn)


def kernel(x_prompt, x_sample, mem_prompt, cache_win_k, cache_win_v, state_dn_conv, state_dn, cache_mem_k, cache_mem_v, state_ffn_conv, norm_mix_g, w_in, dn_conv_w, dn_a_log, dn_dt_bias, dn_norm_g, attn_sinks, w_br_a, w_br_b, w_mix_out, norm_x_g, norm_mem_g, w_xq, w_xkv, w_xo, norm_ffn_g, w_up, ffn_conv_w, w_down, final_norm_g):
    p = {"norm_mix_g": norm_mix_g[0], "w_in": w_in[0], "dn_conv_w": dn_conv_w[0], "dn_a_log": dn_a_log[0],
         "dn_dt_bias": dn_dt_bias[0], "dn_norm_g": dn_norm_g[0], "sinks": attn_sinks[0], "w_br_a": w_br_a[0],
         "w_br_b": w_br_b[0], "w_mix_out": w_mix_out[0], "norm_x_g": norm_x_g[0], "w_xq": w_xq[0],
         "w_xo": w_xo[0], "norm_ffn_g": norm_ffn_g[0], "w_up": w_up[0], "ffn_conv_w": ffn_conv_w[0],
         "w_down": w_down[0], "final_norm_g": final_norm_g}
    w = _prep(p)
    yp, newp = _prompt_layer(x_prompt, mem_prompt, norm_mem_g[0], w_xkv[0], w)
    ys, news = _sample_layer(x_sample, PAST_LEN, cache_win_k[0], cache_win_v[0], state_dn_conv[0], state_dn[0],
                             cache_mem_k[0], cache_mem_v[0], state_ffn_conv[0], w)
    lead = lambda a: a[None]
    p_win_k, p_win_v, p_dn_conv, p_dn_state, p_mem_k, p_mem_v, p_ffn_conv = [lead(a) for a in newp]
    s_win_k, s_win_v, s_dn_conv, s_dn_state, s_ffn_conv = [lead(a) for a in news]
    return (yp, ys, p_win_k, p_win_v, p_dn_conv, p_dn_state, p_mem_k, p_mem_v, p_ffn_conv,
            s_win_k, s_win_v, s_dn_conv, s_dn_state, s_ffn_conv)
```

```python
import functools
import math

import jax
import jax.numpy as jnp
from jax import lax
from jax.experimental import pallas as pl
from jax.experimental.pallas import tpu as pltpu

F32 = jnp.float32
BF16 = jnp.bfloat16

D_MODEL = 1024
A_HEADS = 8
A_KV_HEADS = 2
A_HEAD_DIM = 64
A_WIDTH = 512
A_KV_WIDTH = 128
WINDOW = 128
ROT_DIM = 16
ROPE_THETA = 500000.0
DN_HEADS = 4
DN_HEAD_DIM = 128
DN_WIDTH = 512
DN_CONV = 4
X_HEADS = 4
X_HEAD_DIM = 128
X_WIDTH = 512
D_FF = 2816
FFN_CONV = 3
EPS = 1e-6
PAST_LEN = 16384

LANES = 128
SUBLANES = 8
VMEM_LIMIT = 56 * 1024 * 1024
CHUNK = 128
SAMPLE_ROWS = 16
GDN_SAMPLE_ROWS = 8


def _cparams(*sem):
    return pltpu.CompilerParams(dimension_semantics=sem, vmem_limit_bytes=VMEM_LIMIT)


def _resident(shape):
    return pl.BlockSpec(shape, lambda *_: (0,) * len(shape), pipeline_mode=pl.Buffered(1))


def _rms(x, g):
    return x * lax.rsqrt(jnp.mean(x * x, axis=-1, keepdims=True) + EPS) * g


def _dot(a, b):
    return jnp.dot(a, b, preferred_element_type=F32)


def _dot_nt(a, b):
    return lax.dot_general(a, b, (((1,), (1,)), ((), ())), preferred_element_type=F32)


def _silu(x):
    return x * jax.nn.sigmoid(x)


def _rope(seg, c, s1, s2):
    return seg * c + pltpu.roll(seg, LANES - 8, 1) * s1 + pltpu.roll(seg, 8, 1) * s2


PROJ_COLS = 512


def _proj_tile(x_ref, g_ref, c_ref, s1_ref, s2_ref, wqkv_ref, wd_ref, wab_ref, wg_ref,
               q_ref, k_ref, v_ref, dqkv_ref, dz_ref, ab_ref, ga_ref, gb_ref, dqkv_row0=0):
    tm = x_ref.shape[0]
    xb = _rms(x_ref[...], g_ref[...]).astype(BF16)
    yield
    c, s1, s2 = c_ref[...], s1_ref[...], s2_ref[...]
    z = _dot(xb, wqkv_ref[...])
    for i in range(A_WIDTH // LANES):
        sl = slice(i * LANES, (i + 1) * LANES)
        q_ref[:, sl] = _rope(z[:, sl], c, s1, s2).astype(BF16)
    k_ref[...] = _rope(z[:, A_WIDTH:A_WIDTH + LANES], c, s1, s2)
    v_ref[...] = z[:, A_WIDTH + LANES:]
    yield
    pc = PROJ_COLS
    for j in range(3 * DN_WIDTH // pc):
        dqkv_ref[dqkv_row0:dqkv_row0 + tm, j * pc:(j + 1) * pc] = _dot(xb, wd_ref[:, j * pc:(j + 1) * pc])
        yield
    dz_ref[...] = _dot(xb, wd_ref[:, 3 * DN_WIDTH:])
    ab_ref[...] = _dot(xb, wab_ref[...])
    yield
    for dst, base in ((ga_ref, 0), (gb_ref, D_MODEL)):
        for j in range(D_MODEL // pc):
            dst[:, j * pc:(j + 1) * pc] = _dot(xb, wg_ref[:, base + j * pc:base + (j + 1) * pc])
            yield


def _proj_kernel(*refs):
    _drain(_proj_tile(*refs))


def _proj(x, g, tabs, wqkv, wd, wab, wg, tm):
    m = x.shape[0]
    nt = tabs[0].shape[0] // tm
    row = lambda w: pl.BlockSpec((tm, w), lambda i: (i, 0))
    tab = pl.BlockSpec((tm, LANES), lambda i: (i % nt, 0))
    widths = (A_WIDTH, LANES, LANES, 3 * DN_WIDTH, DN_WIDTH, LANES, D_MODEL, D_MODEL)
    dts = (BF16, F32, F32, F32, F32, F32, F32, F32)
    return pl.pallas_call(
        _proj_kernel,
        grid=(m // tm,),
        in_specs=[row(D_MODEL), _resident((1, D_MODEL)), tab, tab, tab,
                  _resident(wqkv.shape), _resident(wd.shape), _resident(wab.shape), _resident(wg.shape)],
        out_specs=[row(w) for w in widths],
        out_shape=[jax.ShapeDtypeStruct((m, w), d) for w, d in zip(widths, dts)],
        compiler_params=_cparams("parallel"),
        name="in_proj",
    )(x, g, *tabs, wqkv, wd, wab, wg)


def _rope_tables(pos):
    half = ROT_DIM // 2
    inv = ROPE_THETA ** (-2.0 * jnp.arange(half, dtype=F32) / ROT_DIM)
    d = jnp.arange(LANES, dtype=jnp.int32) % A_HEAD_DIM
    ang = pos.astype(F32)[:, None] * inv[d % half][None, :]
    c, s = jnp.cos(ang), jnp.sin(ang)
    return (jnp.where(d < ROT_DIM, c, 1.0), jnp.where(d < half, -s, 0.0),
            jnp.where((d >= half) & (d < ROT_DIM), s, 0.0))


def _both_halves(t, lane_lo):
    r = pltpu.roll(t, A_HEAD_DIM, 1)
    return jnp.where(lane_lo, t, r), jnp.where(lane_lo, r, t)


def _sink_softmax(s, valid, sink):
    s = jnp.where(valid, s, -jnp.inf)
    m = jnp.maximum(jnp.max(s, axis=-1, keepdims=True), sink)
    p = jnp.exp(s - m)
    den = jnp.sum(p, axis=-1, keepdims=True) + jnp.exp(sink - m)
    return p * (1.0 / den)


def _swa_tile(sink_ref, q_ref, k, v, first):
    w = WINDOW
    lane_lo_k = lax.broadcasted_iota(jnp.int32, k.shape, 1) < A_HEAD_DIM
    kk = [t.astype(BF16) for t in _both_halves(k, lane_lo_k)]
    vv = [t.astype(BF16) for t in _both_halves(v, lane_lo_k)]
    lane_lo = lax.broadcasted_iota(jnp.int32, (w, LANES), 1) < A_HEAD_DIM
    row = lax.broadcasted_iota(jnp.int32, (4 * w, 2 * w), 0)
    col = lax.broadcasted_iota(jnp.int32, (4 * w, 2 * w), 1)
    d = (row & (w - 1)) + w - col
    band = (d >= 0) & (d < w)
    band_first = band & (col >= jnp.where(first, w, 0))
    hrow = lax.broadcasted_iota(jnp.int32, (4 * w, 1), 0) // w
    sinks = []
    for g in range(A_KV_HEADS):
        sink = jnp.zeros((4 * w, 1), F32)
        for j in range(4):
            sink = jnp.where(hrow == j, sink_ref[4 * g + j], sink)
        sinks.append(sink)
    zero = jnp.zeros((), BF16)
    scale = jnp.asarray(A_HEAD_DIM ** -0.5, BF16)
    blocks = []
    for b in range(q_ref.shape[0] // w):
        segs = []
        for g in range(A_KV_HEADS):
            parts = []
            for sgm in range(2):
                seg = q_ref[b * w:(b + 1) * w, (2 * g + sgm) * LANES:(2 * g + sgm + 1) * LANES] * scale
                parts += [jnp.where(lane_lo, seg, zero), jnp.where(lane_lo, zero, seg)]
            qs = jnp.concatenate(parts, axis=0)
            s = _dot_nt(qs, kk[g][b * w:(b + 2) * w])
            p = _sink_softmax(s, band_first if b == 0 else band, sinks[g]).astype(BF16)
            o = _dot(p, vv[g][b * w:(b + 2) * w])
            segs += [jnp.where(lane_lo, o[(2 * sgm) * w:(2 * sgm + 1) * w],
                               o[(2 * sgm + 1) * w:(2 * sgm + 2) * w]).astype(BF16) for sgm in range(2)]
        blocks.append(jnp.concatenate(segs, axis=1))
    return jnp.concatenate(blocks, axis=0)


def _swa_sample_kernel(sink_ref, q_ref, ckt_ref, cvt_ref, kn_ref, vn_ref, o_ref, *, nb, t):
    w = ckt_ref.shape[3]
    rows = q_ref.shape[1] // A_HEADS
    gq = rows * (A_HEADS // A_KV_HEADS)
    r = lax.broadcasted_iota(jnp.int32, (gq, w), 0)
    c = lax.broadcasted_iota(jnp.int32, (gq, w), 1)
    valid_c = c > r % rows
    rn = lax.broadcasted_iota(jnp.int32, (gq, kn_ref.shape[1]), 0)
    cn = lax.broadcasted_iota(jnp.int32, (gq, kn_ref.shape[1]), 1)
    tn = cn // A_KV_HEADS
    causal_n = (tn <= rn % rows) & (tn < t)
    hrow = lax.broadcasted_iota(jnp.int32, (gq, 1), 0) // rows
    valid_n, sink = [], []
    for g in range(A_KV_HEADS):
        valid_n.append(causal_n & (cn % A_KV_HEADS == g))
        sk = jnp.zeros((gq, 1), F32)
        for j in range(A_HEADS // A_KV_HEADS):
            sk = jnp.where(hrow == j, sink_ref[g * (A_HEADS // A_KV_HEADS) + j], sk)
        sink.append(sk)
    probs = [(b, g) for b in range(nb) for g in range(A_KV_HEADS)]
    scale = A_HEAD_DIM ** -0.5
    qs = [q_ref[b, g * gq:(g + 1) * gq, :] for b, g in probs]
    scs = [jnp.where(valid_c, _dot(q, ckt_ref[b, g].astype(BF16)) * scale, -jnp.inf)
           for q, (b, g) in zip(qs, probs)]
    sns = [jnp.where(valid_n[g], _dot_nt(q, kn_ref[b]) * scale, -jnp.inf) for q, (b, g) in zip(qs, probs)]
    ms = [jnp.maximum(jnp.maximum(jnp.max(sc, -1, keepdims=True), jnp.max(sn, -1, keepdims=True)), sink[g])
          for sc, sn, (b, g) in zip(scs, sns, probs)]
    pcs = [jnp.exp(sc - m) for sc, m in zip(scs, ms)]
    pns = [jnp.exp(sn - m) for sn, m in zip(sns, ms)]
    invs = [1.0 / (jnp.sum(pc, -1, keepdims=True) + jnp.sum(pn, -1, keepdims=True) + jnp.exp(sink[g] - m))
            for pc, pn, m, (b, g) in zip(pcs, pns, ms, probs)]
    for pc, pn, inv, (b, g) in zip(pcs, pns, invs, probs):
        o_ref[b, g * gq:(g + 1) * gq, :] = (_dot_nt((pc * inv).astype(BF16), cvt_ref[b, g].astype(BF16))
                                            + _dot((pn * inv).astype(BF16), vn_ref[b]))


def _swa_sample(q, ckt, cvt, kn, vn, sinks, t, nb=8):
    bsz, nq, _ = q.shape
    blk = lambda a: pl.BlockSpec((nb,) + a.shape[1:], lambda i: (i,) + (0,) * (a.ndim - 1))
    return pl.pallas_call(
        functools.partial(_swa_sample_kernel, nb=nb, t=t),
        grid=(bsz // nb,),
        in_specs=[pl.BlockSpec(memory_space=pltpu.SMEM), blk(q), blk(ckt), blk(cvt), blk(kn), blk(vn)],
        out_specs=blk(q),
        out_shape=jax.ShapeDtypeStruct(q.shape, F32),
        compiler_params=_cparams("parallel"),
        name="swa_sample",
    )(sinks, q, ckt, cvt, kn, vn)


def _lane_bcast(x, lane):
    return jnp.broadcast_to(x[:, lane:lane + 1], (x.shape[0], LANES))


def _cumsum_rows(x, block):
    c = x.shape[0]
    rowi = lax.broadcasted_iota(jnp.int32, (c, c), 0)
    coli = lax.broadcasted_iota(jnp.int32, (c, c), 1)
    ones = (((rowi // block) == (coli // block)) & (rowi >= coli)).astype(BF16)
    hi = x.astype(BF16)
    r1 = x - hi.astype(F32)
    mid = r1.astype(BF16)
    lo = (r1 - mid.astype(F32)).astype(BF16)
    s = _dot(ones, jnp.concatenate([hi, mid, lo], axis=1))
    return s[:, :LANES] + s[:, LANES:2 * LANES] + s[:, 2 * LANES:]


def _l2n(t):
    return t * lax.rsqrt(jnp.sum(t * t, axis=-1, keepdims=True) + EPS)


def _gates(ab, alog, dtb):
    x = ab + dtb
    sp = jnp.maximum(x, 0.0) + jnp.log1p(jnp.exp(-jnp.abs(x)))
    return -jnp.exp(alog) * sp, jax.nn.sigmoid(ab)


def _merge_masks(c, top):
    rowi = lax.broadcasted_iota(jnp.int32, (c, c), 0)
    coli = lax.broadcasted_iota(jnp.int32, (c, c), 1)
    masks = []
    s = 1
    while s < top:
        rb, cb = rowi // s, coli // s
        masks.append(((rb // 2) == (cb // 2)) & ((rb % 2) == 1) & ((cb % 2) == 0))
        s *= 2
    return masks


def _each(f, *lists):
    return [f(*t) for t in zip(*lists)]


def _drain(stages):
    try:
        while True:
            next(stages)
    except StopIteration as done:
        return done.value


def _interleave(*staged):
    live = list(staged)
    while live:
        for item in tuple(live):
            stages, per_turn = item
            try:
                for _ in range(per_turn):
                    next(stages)
            except StopIteration:
                live.remove(item)


def _chunk_local(qs, ks, vs, gcols, grows, betas, tril, merges):
    c = qs[0].shape[0]
    decays = _each(lambda gc, gr: jnp.exp(jnp.minimum(gc - gr, 0.0)), gcols, grows)
    kbs = _each(lambda k, b: k * b, ks, betas)
    yield
    ms = _each(lambda q, kb, k: _dot_nt(jnp.concatenate([q, kb], axis=0).astype(BF16), k.astype(BF16)),
               qs, kbs, ks)
    yield
    qks = _each(lambda m, d: jnp.where(tril, m[:c] * d, 0.0), ms, decays)
    a = _each(lambda m, d: m[c:] * d, ms, decays)
    eye = (lax.broadcasted_iota(jnp.int32, (c, c), 0) == lax.broadcasted_iota(jnp.int32, (c, c), 1)).astype(F32)
    ts = [eye - jnp.where(merges[0], x, 0.0) if merges else eye for x in a]
    yield
    for off in merges[1:]:
        tbs = [t.astype(BF16) for t in ts]
        zs = _each(lambda x, tb: _dot(jnp.where(off, x, 0.0).astype(BF16), tb), a, tbs)
        yield
        ts = _each(lambda t, tb, z: t - _dot(tb, z.astype(BF16)), ts, tbs, zs)
        yield
    ns = [t - eye for t in ts]
    egs = [jnp.exp(gc) for gc in gcols]
    rhss = _each(lambda v, b, kb, eg: jnp.concatenate([v * b, kb * eg], axis=1), vs, betas, kbs, egs)
    yield
    uws = _each(lambda r, n: r + _dot(n.astype(BF16), r.astype(BF16)), rhss, ns)
    yield
    return ([x[:, :LANES] for x in uws], [x[:, LANES:] for x in uws], qks,
            _each(lambda q, eg: q * eg, qs, egs))


def _gdn_tile(xp_ref, dz_ref, ab_ref, cw_ref, alog_ref, dtb_ref, ng_ref, y_ref, s_ref, nc):
    c = CHUNK
    pad = SUBLANES
    cw = cw_ref[...]
    g, beta = _gates(ab_ref[...], alog_ref[...], dtb_ref[...])
    rowi = lax.broadcasted_iota(jnp.int32, (c, c), 0)
    coli = lax.broadcasted_iota(jnp.int32, (c, c), 1)
    tril = rowi >= coli
    merges = _merge_masks(c, c)
    ng = ng_ref[...]
    heads = range(DN_HEADS)
    probs = [(ci, h) for ci in range(nc) for h in heads]
    rows = lambda ci: slice(ci * c, (ci + 1) * c)
    lanes = lambda part, h: slice(part * DN_WIDTH + h * LANES, part * DN_WIDTH + (h + 1) * LANES)
    yield
    qs, ks, vs, gcols, grows, betas = [], [], [], [], [], []
    for ci in range(nc):
        conv = xp_ref[pad + ci * c:pad + (ci + 1) * c, :] * cw[DN_CONV - 1:DN_CONV]
        for j in range(DN_CONV - 1):
            o = pad - (DN_CONV - 1) + j + ci * c
            conv = conv + xp_ref[o:o + c, :] * cw[j:j + 1]
        conv = _silu(conv)
        gc = _cumsum_rows(g[rows(ci)], c)
        gct = gc.T
        qs += [_l2n(conv[:, lanes(0, h)]) * (DN_HEAD_DIM ** -0.5) for h in heads]
        ks += [_l2n(conv[:, lanes(1, h)]) for h in heads]
        vs += [conv[:, lanes(2, h)] for h in heads]
        gcols += [_lane_bcast(gc, h) for h in heads]
        grows += [gct[h:h + 1, :] for h in heads]
        betas += [_lane_bcast(beta[rows(ci)], DN_HEADS + h) for h in heads]
        yield
    us, ws, qks, qds = yield from _chunk_local(qs, ks, vs, gcols, grows, betas, tril, merges)
    glasts = [gc[c - 1:c, :] for gc in gcols]
    kdts = _each(lambda k, gl, gc: (k * jnp.exp(gl - gc)).T, ks, glasts, gcols)
    wqs = _each(lambda w, qd: jnp.concatenate([w, qd], axis=0).astype(BF16), ws, qds)
    qkks = _each(lambda qk, kdt: jnp.concatenate([qk, kdt], axis=0).astype(BF16), qks, kdts)
    yield
    ss = [s_ref[h] for h in heads]
    for ci in range(nc):
        pr = [ci * DN_HEADS + h for h in heads]
        r2s = [_dot(wqs[p], s.astype(BF16)) for p, s in zip(pr, ss)]
        vnews = [us[p] - r2[:c] for p, r2 in zip(pr, r2s)]
        yield
        r3s = [_dot(qkks[p], vn.astype(BF16)) for p, vn in zip(pr, vnews)]
        ss = [s * jnp.exp(glasts[p]) + r3[c:] for p, s, r3 in zip(pr, ss, r3s)]
        for h in heads:
            o = r2s[h][c:] + r3s[h][:c]
            y_ref[rows(ci), lanes(0, h)] = (_rms(o, ng) * _silu(dz_ref[rows(ci), lanes(0, h)])).astype(BF16)
        yield
    for h in heads:
        s_ref[h] = ss[h]


def _proj_gdn_kernel(x_ref, g_ref, c_ref, s1_ref, s2_ref, wqkv_ref, wd_ref, wab_ref, wg_ref,
                     cw_ref, alog_ref, dtb_ref, ng_ref,
                     q_ref, k_ref, v_ref, ga_ref, gb_ref, y_ref, sout_ref, tail_ref,
                     xp_ref, dz_ref, ab_ref, carry_ref, s_ref, *, nt, nc):
    i = pl.program_id(0)
    pad = SUBLANES
    r = nc * CHUNK
    slot_a = i % 2
    slot_b = 1 - slot_a

    @pl.when(i == 0)
    def _():
        xp_ref[1] = jnp.zeros(xp_ref.shape[1:], F32)
        dz_ref[1] = jnp.zeros(dz_ref.shape[1:], F32)
        ab_ref[1] = jnp.zeros(ab_ref.shape[1:], F32)

    @pl.when(jnp.maximum(i - 1, 0) % nt == 0)
    def _():
        carry_ref[...] = jnp.zeros_like(carry_ref)
        s_ref[...] = jnp.zeros_like(s_ref)

    xpb_ref = xp_ref.at[slot_b]
    xpb_ref[0:pad, :] = carry_ref[...]
    _interleave(
        (_gdn_tile(xpb_ref, dz_ref.at[slot_b], ab_ref.at[slot_b], cw_ref, alog_ref, dtb_ref, ng_ref,
                   y_ref, s_ref, nc), 1),
        (_proj_tile(x_ref, g_ref, c_ref, s1_ref, s2_ref, wqkv_ref, wd_ref, wab_ref, wg_ref, q_ref, k_ref, v_ref,
                    xp_ref.at[slot_a], dz_ref.at[slot_a], ab_ref.at[slot_a], ga_ref, gb_ref, pad), 1))
    carry_ref[...] = xpb_ref[r:r + pad, :]
    tail_ref[0] = carry_ref[...]
    sout_ref[0] = s_ref[...]


def _proj_gdn(x, g, tabs, wqkv, wd, wab, wg, cw, alog, dtb, ng, nseq, tm):
    m = x.shape[0]
    nt = m // nseq // tm
    last = m // tm - 1
    nc = tm // CHUNK
    proj = lambda i: jnp.minimum(i, last)
    gdn = lambda i: jnp.maximum(i - 1, 0)
    prow = lambda w: pl.BlockSpec((tm, w), lambda i: (proj(i), 0))
    tab = pl.BlockSpec((tm, LANES), lambda i: (proj(i) % nt, 0))
    per_seq = lambda *dims: pl.BlockSpec((1,) + dims, lambda i: (gdn(i) // nt,) + (0,) * len(dims))
    state = (DN_HEADS, DN_HEAD_DIM, DN_HEAD_DIM)
    bufs = [pltpu.VMEM((2, tm + SUBLANES, 3 * DN_WIDTH), F32), pltpu.VMEM((2, tm, DN_WIDTH), F32),
            pltpu.VMEM((2, tm, LANES), F32)]
    return pl.pallas_call(
        functools.partial(_proj_gdn_kernel, nt=nt, nc=nc),
        grid=(m // tm + 1,),
        in_specs=[prow(D_MODEL), _resident((1, D_MODEL)), tab, tab, tab,
                  _resident(wqkv.shape), _resident(wd.shape), _resident(wab.shape), _resident(wg.shape),
                  _resident(cw.shape), _resident((1, LANES)), _resident((1, LANES)), _resident((1, LANES))],
        out_specs=[prow(A_WIDTH), prow(LANES), prow(LANES), prow(D_MODEL), prow(D_MODEL),
                   pl.BlockSpec((tm, DN_WIDTH), lambda i: (gdn(i), 0)), per_seq(*state),
                   per_seq(SUBLANES, 3 * DN_WIDTH)],
        out_shape=[jax.ShapeDtypeStruct((m, A_WIDTH), BF16), jax.ShapeDtypeStruct((m, LANES), F32),
                   jax.ShapeDtypeStruct((m, LANES), F32), jax.ShapeDtypeStruct((m, D_MODEL), F32),
                   jax.ShapeDtypeStruct((m, D_MODEL), F32), jax.ShapeDtypeStruct((m, DN_WIDTH), BF16),
                   jax.ShapeDtypeStruct((nseq,) + state, F32),
                   jax.ShapeDtypeStruct((nseq, SUBLANES, 3 * DN_WIDTH), F32)],
        scratch_shapes=bufs + [pltpu.VMEM((SUBLANES, 3 * DN_WIDTH), F32), pltpu.VMEM(state, F32)],
        compiler_params=_cparams("arbitrary"),
        name="proj_gdn",
    )(x, g, *tabs, wqkv, wd, wab, wg, cw, alog, dtb, ng)


def _gdn_sample_kernel(buf_ref, x_ref, dz_ref, ab_ref, s0_ref, cw_ref, alog_ref, dtb_ref, ng_ref,
                       y_ref, sout_ref, xp_ref, ab16_ref):
    t = x_ref.shape[1]
    c = CHUNK
    sr = GDN_SAMPLE_ROWS
    nb = c // sr
    pad = SUBLANES
    hist = DN_CONV - 1
    xp_ref[...] = jnp.zeros_like(xp_ref)
    ab16_ref[...] = jnp.zeros_like(ab16_ref)
    for b in range(nb):
        xp_ref[pad + b * sr - hist:pad + b * sr, :] = buf_ref[b]
        xp_ref[pad + b * sr:pad + b * sr + t, :] = x_ref[b]
        ab16_ref[b * sr:b * sr + t, :] = ab_ref[b]
    cw = cw_ref[...]
    conv = xp_ref[pad:pad + c, :] * cw[hist:hist + 1]
    for j in range(hist):
        conv = conv + xp_ref[pad - hist + j:pad - hist + j + c, :] * cw[j:j + 1]
    conv = _silu(conv)
    live = (lax.broadcasted_iota(jnp.int32, (c, LANES), 0) % sr) < t
    g, beta = _gates(ab16_ref[...], alog_ref[...], dtb_ref[...])
    g = jnp.where(live, g, 0.0)
    beta = jnp.where(live, beta, 0.0)
    gc = _cumsum_rows(g, sr)
    gct = gc.T
    rowi = lax.broadcasted_iota(jnp.int32, (c, c), 0)
    coli = lax.broadcasted_iota(jnp.int32, (c, c), 1)
    tril = ((rowi // sr) == (coli // sr)) & (rowi >= coli)
    merges = _merge_masks(c, pl.next_power_of_2(t))
    ng = ng_ref[...]
    rowb = lax.broadcasted_iota(jnp.int32, (c, LANES), 0) // sr
    heads = range(DN_HEADS)
    lanes = lambda part, h: slice(part * DN_WIDTH + h * LANES, part * DN_WIDTH + (h + 1) * LANES)
    ks = [_l2n(conv[:, lanes(1, h)]) for h in heads]
    gcols = [_lane_bcast(gc, h) for h in heads]
    us, ws, qks, qds = _drain(_chunk_local(
        [_l2n(conv[:, lanes(0, h)]) * (DN_HEAD_DIM ** -0.5) for h in heads], ks,
        [conv[:, lanes(2, h)] for h in heads], gcols, [gct[h:h + 1, :] for h in heads],
        [_lane_bcast(beta, DN_HEADS + h) for h in heads], tril, merges))
    seqs = range(nb)
    rows = lambda b: slice(b * sr, (b + 1) * sr)
    s0s = [[s0_ref[b, h] for b in seqs] for h in heads]
    r2s = [[_dot(jnp.concatenate([ws[h][rows(b)], qds[h][rows(b)]], axis=0).astype(BF16), s0s[h][b].astype(BF16))
            for b in seqs] for h in heads]
    vnews = [jnp.concatenate([us[h][rows(b)] - r2s[h][b][:sr] for b in seqs], axis=0) for h in heads]
    os_ = [_rms(jnp.concatenate([r2s[h][b][sr:] for b in seqs], axis=0)
                + _dot(qks[h].astype(BF16), vnews[h].astype(BF16)), ng) for h in heads]
    for h in heads:
        for b in seqs:
            y_ref[b, :, lanes(0, h)] = os_[h][b * sr:b * sr + t] * _silu(dz_ref[b, :, lanes(0, h)])
    glasts = [jnp.concatenate([jnp.broadcast_to(gcols[h][(b + 1) * sr - 1:(b + 1) * sr, :], (sr, LANES))
                               for b in seqs], axis=0) for h in heads]
    kdts = [(ks[h] * jnp.exp(glasts[h] - gcols[h])).T.astype(BF16) for h in heads]
    for h in heads:
        for b in seqs:
            vb = jnp.where(rowb == b, vnews[h], 0.0).astype(BF16)
            sout_ref[b, h] = s0s[h][b] * jnp.exp(glasts[h][b * sr:b * sr + 1, :]) + _dot(kdts[h], vb)


def _gdn_sample(buf, x, dz, ab, s0, cw, alog, dtb, ng):
    bsz, t, _ = x.shape
    assert t + DN_CONV - 1 <= GDN_SAMPLE_ROWS, "too many new tokens for one row tile per sequence"
    nb = CHUNK // GDN_SAMPLE_ROWS
    seq = lambda a: pl.BlockSpec((nb,) + a.shape[1:], lambda i: (i,) + (0,) * (a.ndim - 1))
    return pl.pallas_call(
        _gdn_sample_kernel,
        grid=(bsz // nb,),
        in_specs=[seq(buf), seq(x), seq(dz), seq(ab), seq(s0), _resident(cw.shape),
                  _resident((1, LANES)), _resident((1, LANES)), _resident((1, LANES))],
        out_specs=[seq(dz), seq(s0)],
        out_shape=[jax.ShapeDtypeStruct(dz.shape, F32), jax.ShapeDtypeStruct(s0.shape, F32)],
        scratch_shapes=[pltpu.VMEM((CHUNK + SUBLANES, 3 * DN_WIDTH), F32), pltpu.VMEM((CHUNK, LANES), F32)],
        compiler_params=_cparams("parallel"),
        name="gdn_sample",
    )(buf, x, dz, ab, s0, cw, alog, dtb, ng)


def _merge_kernel(x_ref, *rest, attend):
    if attend:
        sink_ref, q_ref, kp_ref, kc_ref, vp_ref, vc_ref, *rest = rest
        ya = _swa_tile(sink_ref, q_ref, jnp.concatenate([kp_ref[...], kc_ref[...]], axis=0),
                       jnp.concatenate([vp_ref[...], vc_ref[...]], axis=0), pl.program_id(1) == 0)
    else:
        ya_ref, *rest = rest
        ya = ya_ref[...]
    yb_ref, ga_ref, gb_ref, wa_ref, wb_ref, wo_ref, g_ref, wq_ref, h_ref, hq_ref = rest
    mix = (jax.nn.sigmoid(ga_ref[...]) * _dot(ya.astype(BF16), wa_ref[...])
           + jax.nn.sigmoid(gb_ref[...]) * _dot(yb_ref[...].astype(BF16), wb_ref[...]))
    h = x_ref[...] + _dot(mix.astype(BF16), wo_ref[...])
    h_ref[...] = h
    hq_ref[...] = _dot(_rms(h, g_ref[...]).astype(BF16), wq_ref[...]).astype(BF16)


def _merge(x, attn, yb, ga, gb, wa, wb, wo, g, wq, nseq, tm):
    m = x.shape[0]
    nt = m // nseq // tm
    row = lambda w: pl.BlockSpec((tm, w), lambda s, j: (s * nt + j, 0))
    attend = isinstance(attn, tuple)
    if attend:
        sinks, q, k, v = attn
        per = tm // WINDOW
        prev = pl.BlockSpec((WINDOW, A_KV_WIDTH), lambda s, j: (jnp.maximum((s * nt + j) * per - 1, 0), 0))
        attn_args = [sinks, q, k, k, v, v]
        attn_specs = [pl.BlockSpec(memory_space=pltpu.SMEM), row(A_WIDTH), prev, row(A_KV_WIDTH),
                      prev, row(A_KV_WIDTH)]
    else:
        attn_args, attn_specs = [attn], [row(A_WIDTH)]
    return pl.pallas_call(
        functools.partial(_merge_kernel, attend=attend),
        grid=(nseq, nt),
        in_specs=[row(D_MODEL), *attn_specs, row(DN_WIDTH), row(D_MODEL), row(D_MODEL),
                  _resident(wa.shape), _resident(wb.shape), _resident(wo.shape),
                  _resident((1, D_MODEL)), _resident(wq.shape)],
        out_specs=[row(D_MODEL), row(X_WIDTH)],
        out_shape=[jax.ShapeDtypeStruct((m, D_MODEL), F32), jax.ShapeDtypeStruct((m, X_WIDTH), BF16)],
        compiler_params=_cparams("parallel", "parallel"),
        name="merge",
    )(x, *attn_args, yb, ga, gb, wa, wb, wo, g, wq)


def _memkv_kernel(m_ref, g_ref, w_ref, k_ref, v_ref, kf_ref, vf_ref):
    tm = m_ref.shape[0]
    z = _dot(_rms(m_ref[...], g_ref[...]).astype(BF16), w_ref[...])
    k_ref[...] = z[:, :X_WIDTH]
    v_ref[...] = z[:, X_WIDTH:]
    for h in range(X_HEADS):
        kf_ref[pl.ds(h, tm, stride=X_HEADS), :] = z[:, h * LANES:(h + 1) * LANES]
        vf_ref[pl.ds(h, tm, stride=X_HEADS), :] = z[:, X_WIDTH + h * LANES:X_WIDTH + (h + 1) * LANES]


def _memkv(mem, g, w, tm):
    m = mem.shape[0]
    row = lambda wd: pl.BlockSpec((tm, wd), lambda i: (i, 0))
    flat = pl.BlockSpec((tm * X_HEADS, X_HEAD_DIM), lambda i: (i, 0))
    return pl.pallas_call(
        _memkv_kernel,
        grid=(m // tm,),
        in_specs=[row(D_MODEL), _resident((1, D_MODEL)), _resident(w.shape)],
        out_specs=[row(X_WIDTH), row(X_WIDTH), flat, flat],
        out_shape=[jax.ShapeDtypeStruct((m, X_WIDTH), F32)] * 2
        + [jax.ShapeDtypeStruct((m * X_HEADS, X_HEAD_DIM), F32)] * 2,
        compiler_params=_cparams("parallel"),
        name="mem_kv",
    )(mem, g, w)


def _softmax_rows(s):
    e = jnp.exp(s - jnp.max(s, axis=-1, keepdims=True))
    return e * (1.0 / jnp.sum(e, axis=-1, keepdims=True))


def _xattn_sample_kernel(hq_ref, mk_ref, mv_ref, o_ref, *, nb):
    rows = hq_ref.shape[1]
    nmh = mk_ref.shape[1]
    qh = lax.broadcasted_iota(jnp.int32, (X_HEADS * rows, nmh), 0) // rows
    mh = lax.broadcasted_iota(jnp.int32, (X_HEADS * rows, nmh), 1) % X_HEADS
    same = qh == mh
    seqs = range(nb)
    qs = [jnp.concatenate([hq_ref[b, :, h * LANES:(h + 1) * LANES] for h in range(X_HEADS)], axis=0)
          for b in seqs]
    ss = [_dot_nt(qs[b], mk_ref[b].astype(BF16)) * (X_HEAD_DIM ** -0.5) for b in seqs]
    ps = [_softmax_rows(jnp.where(same, s, -jnp.inf)).astype(BF16) for s in ss]
    os_ = [_dot(ps[b], mv_ref[b].astype(BF16)) for b in seqs]
    for b in seqs:
        for h in range(X_HEADS):
            o_ref[b, :, h * LANES:(h + 1) * LANES] = os_[b][h * rows:(h + 1) * rows].astype(BF16)


def _xattn_sample(hq, mk, mv, nb=8):
    bsz, rows, _ = hq.shape
    q = pl.BlockSpec((nb, rows, X_WIDTH), lambda i: (i, 0, 0))
    mem = pl.BlockSpec((nb, mk.shape[1], X_HEAD_DIM), lambda i: (i, 0, 0))
    return pl.pallas_call(
        functools.partial(_xattn_sample_kernel, nb=nb),
        grid=(bsz // nb,),
        in_specs=[q, mem, mem],
        out_specs=q,
        out_shape=jax.ShapeDtypeStruct(hq.shape, BF16),
        compiler_params=_cparams("parallel"),
        name="xattn_sample",
    )(hq, mk, mv)


def _xattn_tile(hq_ref, mk_ref, mv_ref):
    outs = []
    for h in range(X_HEADS):
        hs = slice(h * LANES, (h + 1) * LANES)
        s = _dot_nt(hq_ref[:, hs], mk_ref[:, hs].astype(BF16)) * (X_HEAD_DIM ** -0.5)
        outs.append(_dot(_softmax_rows(s).astype(BF16), mv_ref[:, hs].astype(BF16)).astype(BF16))
    return jnp.concatenate(outs, axis=1)


def _ffn_kernel(h_ref, *rest, stride, padc, fc, attend):
    if attend:
        hq_ref, mk_ref, mv_ref, *rest = rest
        xo = _xattn_tile(hq_ref, mk_ref, mv_ref)
    else:
        xo_ref, *rest = rest
        xo = xo_ref[...]
    cin_ref, wxo_ref, g_ref, wu_ref, wv_ref, cw_ref, wd_ref, fg_ref, y_ref, tail_ref, xp_ref, acc_ref = rest
    tm = h_ref.shape[0]

    @pl.when(pl.program_id(1) == 0)
    def _():
        tail_ref[...] = cin_ref[...]

    h = h_ref[...] + _dot(xo, wxo_ref[...])
    acc_ref[...] = h
    hn = _rms(h, g_ref[...]).astype(BF16)
    for lo, hi in zip((0,) + fc, fc + (D_FF,)):
        cs = slice(lo, hi)
        n = hi - lo
        u = _dot(hn, wu_ref[:, cs])
        gate = _dot(hn, wv_ref[:, cs])
        xp_ref[0:padc, 0:n] = tail_ref[0, :, cs]
        xp_ref[padc:padc + tm, 0:n] = u
        cw = cw_ref[:, cs]
        uc = u * cw[2:3]
        for j in range(FFN_CONV - 1):
            o = padc - (FFN_CONV - 1 - j) * stride
            uc = uc + xp_ref[o:o + tm, 0:n] * cw[j:j + 1]
        tail_ref[0, :, cs] = xp_ref[tm:tm + padc, 0:n]
        acc_ref[...] += _dot((_silu(uc) * gate).astype(BF16), wd_ref[cs, :])
    y_ref[...] = _rms(acc_ref[...], fg_ref[...])


FFN_CUTS = (1536,)


def _ffn(h, xq, mem, cin, wxo, g, wup, cw, wd, fg, tm, stride, fc=FFN_CUTS):
    m = h.shape[0]
    half = lambda c: pl.BlockSpec((D_MODEL, D_FF), lambda *_: (0, c), pipeline_mode=pl.Buffered(1))
    nseq, padc, _ = cin.shape
    nt = m // nseq // tm
    row = lambda w: pl.BlockSpec((tm, w), lambda s, j: (s * nt + j, 0))
    car = pl.BlockSpec((1, padc, D_FF), lambda s, j: (s, 0, 0))
    mem = list(mem or ())
    mem_specs = [pl.BlockSpec((a.shape[0] // nseq, X_WIDTH), lambda s, j: (s, 0)) for a in mem]
    return pl.pallas_call(
        functools.partial(_ffn_kernel, stride=stride, padc=padc, fc=fc, attend=bool(mem)),
        grid=(nseq, nt),
        in_specs=[row(D_MODEL), row(X_WIDTH), *mem_specs, car, _resident(wxo.shape), _resident((1, D_MODEL)),
                  half(0), half(1), _resident(cw.shape), _resident(wd.shape), _resident((1, D_MODEL))],
        out_specs=[row(D_MODEL), car],
        out_shape=[jax.ShapeDtypeStruct((m, D_MODEL), F32), jax.ShapeDtypeStruct(cin.shape, F32)],
        scratch_shapes=[pltpu.VMEM((tm + padc, max(b - a for a, b in zip((0,) + fc, fc + (D_FF,)))), F32),
                        pltpu.VMEM((tm, D_MODEL), F32)],
        compiler_params=_cparams("parallel", "arbitrary"),
        name="ffn",
    )(h, xq, *mem, cin, wxo, g, wup, wup, cw, wd, fg)


def _lane_row(vec, offset=0):
    return jnp.zeros((1, LANES), F32).at[0, offset:offset + vec.shape[0]].set(vec.astype(F32))


def _prep(p):
    w_in = p["w_in"]
    o = 0
    cuts = {}
    for name, wdt in (("qkv", A_WIDTH + 2 * A_KV_WIDTH), ("d", 4 * DN_WIDTH), ("ab", 2 * DN_HEADS),
                      ("g", 2 * D_MODEL)):
        cuts[name] = w_in[:, o:o + wdt]
        o += wdt
    w = {
        "wqkv": cuts["qkv"].astype(BF16),
        "wd": cuts["d"].astype(BF16),
        "wab": jnp.pad(cuts["ab"], ((0, 0), (0, LANES - 2 * DN_HEADS))).astype(BF16),
        "wg": cuts["g"].astype(BF16),
        "norm_mix_g": p["norm_mix_g"].reshape(1, D_MODEL),
        "dn_conv_w": p["dn_conv_w"],
        "alog": _lane_row(p["dn_a_log"]),
        "dtb": _lane_row(p["dn_dt_bias"]),
        "dn_norm_g": p["dn_norm_g"].reshape(1, DN_HEAD_DIM),
        "sinks": p["sinks"].astype(F32),
        "w_br_a": p["w_br_a"].astype(BF16),
        "w_br_b": p["w_br_b"].astype(BF16),
        "w_mix_out": p["w_mix_out"].astype(BF16),
        "norm_x_g": p["norm_x_g"].reshape(1, D_MODEL),
        "w_xq": p["w_xq"].astype(BF16),
        "w_xo": p["w_xo"].astype(BF16),
        "norm_ffn_g": p["norm_ffn_g"].reshape(1, D_MODEL),
        "w_up": p["w_up"].astype(BF16),
        "ffn_conv_w": p["ffn_conv_w"],
        "w_down": p["w_down"].astype(BF16),
        "final_norm_g": p["final_norm_g"].reshape(1, D_MODEL),
    }
    return w


def _tile(m, pref):
    return pref if m % pref == 0 else m


def _prompt_layer(x, mem, norm_mem_g, w_xkv, w):
    bsz, seqlen, _ = x.shape
    m = bsz * seqlen
    x2 = x.reshape(m, D_MODEL)
    tm = _tile(seqlen, 512)
    tabs = _rope_tables(jnp.arange(seqlen, dtype=jnp.int32))
    q, k, v, ga, gb, yb, dn_state, dn_tail = _proj_gdn(
        x2, w["norm_mix_g"], tabs, w["wqkv"], w["wd"], w["wab"], w["wg"], w["dn_conv_w"], w["alog"], w["dtb"],
        w["dn_norm_g"], bsz, tm)
    h, hq = _merge(x2, (w["sinks"], q, k, v), yb, ga, gb, w["w_br_a"], w["w_br_b"], w["w_mix_out"],
                   w["norm_x_g"], w["w_xq"], bsz, tm)
    nm = mem.shape[1]
    mk, mv, mk_rows, mv_rows = _memkv(mem.reshape(bsz * nm, D_MODEL), norm_mem_g.reshape(1, D_MODEL),
                                      w_xkv.astype(BF16), _tile(bsz * nm, 512))
    cin = jnp.zeros((bsz, SUBLANES, D_FF), F32)
    y, tail = _ffn(h, hq, (mk, mv), cin, w["w_xo"], w["norm_ffn_g"], w["w_up"], w["ffn_conv_w"],
                   w["w_down"], w["final_norm_g"], tm, 1)
    wb = min(WINDOW, seqlen)
    new = (
        k.reshape(bsz, seqlen, A_KV_HEADS, A_HEAD_DIM)[:, seqlen - wb:],
        v.reshape(bsz, seqlen, A_KV_HEADS, A_HEAD_DIM)[:, seqlen - wb:],
        dn_tail[:, SUBLANES - (DN_CONV - 1):],
        dn_state,
        mk_rows.reshape(bsz, nm, X_HEADS, X_HEAD_DIM),
        mv_rows.reshape(bsz, nm, X_HEADS, X_HEAD_DIM),
        tail[:, SUBLANES - (FFN_CONV - 1):],
    )
    return y.reshape(bsz, seqlen, D_MODEL), new


def _sample_layer(x, pos0, win_k, win_v, dn_buf, dn_state, mem_k, mem_v, ffn_buf, w):
    bsz, t, _ = x.shape
    m = bsz * t
    sr = SAMPLE_ROWS
    x2 = x.reshape(m, D_MODEL)
    tabs = _rope_tables(jnp.tile(pos0 + jnp.arange(t, dtype=jnp.int32), bsz))
    q, k, v, dqkv, dz, ab, ga, gb = _proj(x2, w["norm_mix_g"], tabs, w["wqkv"], w["wd"], w["wab"], w["wg"], m)

    wlen = win_k.shape[1]
    qh = jnp.transpose(q.reshape(bsz, t, A_HEADS, A_HEAD_DIM), (0, 2, 1, 3)).reshape(bsz, A_HEADS * t, A_HEAD_DIM)
    padrows = lambda a: jnp.pad(a.reshape(bsz, t, -1), ((0, 0), (0, sr - t), (0, 0)))
    newrows = lambda a: jnp.pad(a.reshape(bsz, t * A_KV_HEADS, A_HEAD_DIM),
                                ((0, 0), (0, sr - t * A_KV_HEADS), (0, 0))).astype(BF16)
    kv_major = lambda a: jnp.transpose(a, (0, 2, 3, 1))
    oh = _swa_sample(qh, kv_major(win_k), kv_major(win_v), newrows(k), newrows(v), w["sinks"], t)
    ya = jnp.transpose(oh.reshape(bsz, A_HEADS, t, A_HEAD_DIM), (0, 2, 1, 3)).reshape(m, A_WIDTH)
    new_wk = jnp.concatenate([win_k, k.reshape(bsz, t, A_KV_HEADS, A_HEAD_DIM)], axis=1)[:, -wlen:]
    new_wv = jnp.concatenate([win_v, v.reshape(bsz, t, A_KV_HEADS, A_HEAD_DIM)], axis=1)[:, -wlen:]

    dqkv3 = dqkv.reshape(bsz, t, 3 * DN_WIDTH)
    yb, new_s = _gdn_sample(dn_buf, dqkv3, dz.reshape(bsz, t, DN_WIDTH), ab.reshape(bsz, t, LANES), dn_state,
                            w["dn_conv_w"], w["alog"], w["dtb"], w["dn_norm_g"])
    yb = yb.reshape(m, DN_WIDTH)
    new_dn_buf = jnp.concatenate([dn_buf, dqkv3], axis=1)[:, -(DN_CONV - 1):]

    h, hq = _merge(x2, ya, yb, ga, gb, w["w_br_a"], w["w_br_b"], w["w_mix_out"], w["norm_x_g"], w["w_xq"], 1, m)
    nm = mem_k.shape[1]
    xo = _xattn_sample(padrows(hq), mem_k.reshape(bsz, nm * X_HEADS, X_HEAD_DIM),
                       mem_v.reshape(bsz, nm * X_HEADS, X_HEAD_DIM))
    xo = xo[:, :t]

    tmaj = lambda a: jnp.transpose(a.reshape(bsz, t, -1), (1, 0, 2)).reshape(m, -1)
    cin = jnp.transpose(ffn_buf, (1, 0, 2)).reshape(1, (FFN_CONV - 1) * bsz, D_FF)
    y, tail = _ffn(tmaj(h), tmaj(xo), None, cin, w["w_xo"], w["norm_ffn_g"], w["w_up"], w["ffn_conv_w"],
                   w["w_down"], w["final_norm_g"], m, bsz)
    y = jnp.transpose(y.reshape(t, bsz, D_MODEL), (1, 0, 2))
    new_ffn = jnp.transpose(tail.reshape(FFN_CONV - 1, bsz, D_FF), (1, 0, 2))
    return y, (new_wk, new_wv, new_dn_buf, new_s, new_ffn)


def kernel(x_prompt, x_sample, mem_prompt, cache_win_k, cache_win_v, state_dn_conv, state_dn, cache_mem_k, cache_mem_v, state_ffn_conv, norm_mix_g, w_in, dn_conv_w, dn_a_log, dn_dt_bias, dn_norm_g, attn_sinks, w_br_a, w_br_b, w_mix_out, norm_x_g, norm_mem_g, w_xq, w_xkv, w_xo, norm_ffn_g, w_up, ffn_conv_w, w_down, final_norm_g):
    p = {"norm_mix_g": norm_mix_g[0], "w_in": w_in[0], "dn_conv_w": dn_conv_w[0], "dn_a_log": dn_a_log[0],
         "dn_dt_bias": dn_dt_bias[0], "dn_norm_g": dn_norm_g[0], "sinks": attn_sinks[0], "w_br_a": w_br_a[0],
         "w_br_b": w_br_b[0], "w_mix_out": w_mix_out[0], "norm_x_g": norm_x_g[0], "w_xq": w_xq[0],
         "w_xo": w_xo[0], "norm_ffn_g": norm_ffn_g[0], "w_up": w_up[0], "ffn_conv_w": ffn_conv_w[0],
         "w_down": w_down[0], "final_norm_g": final_norm_g}
    w = _prep(p)
    yp, newp = _prompt_layer(x_prompt, mem_prompt, norm_mem_g[0], w_xkv[0], w)
    ys, news = _sample_layer(x_sample, PAST_LEN, cache_win_k[0], cache_win_v[0], state_dn_conv[0], state_dn[0],
                             cache_mem_k[0], cache_mem_v[0], state_ffn_conv[0], w)
    lead = lambda a: a[None]
    p_win_k, p_win_v, p_dn_conv, p_dn_state, p_mem_k, p_mem_v, p_ffn_conv = [lead(a) for a in newp]
    s_win_k, s_win_v, s_dn_conv, s_dn_state, s_ffn_conv = [lead(a) for a in news]
    return (yp, ys, p_win_k, p_win_v, p_dn_conv, p_dn_state, p_mem_k, p_mem_v, p_ffn_conv,
            s_win_k, s_win_v, s_dn_conv, s_dn_state, s_ffn_conv)
```

```python
import functools
import math

import jax
import jax.numpy as jnp
from jax import lax
from jax.experimental import pallas as pl
from jax.experimental.pallas import tpu as pltpu

F32 = jnp.float32
BF16 = jnp.bfloat16

D_MODEL = 1024
A_HEADS = 8
A_KV_HEADS = 2
A_HEAD_DIM = 64
A_WIDTH = 512
A_KV_WIDTH = 128
WINDOW = 128
ROT_DIM = 16
ROPE_THETA = 500000.0
DN_HEADS = 4
DN_HEAD_DIM = 128
DN_WIDTH = 512
DN_CONV = 4
X_HEADS = 4
X_HEAD_DIM = 128
X_WIDTH = 512
D_FF = 2816
FFN_CONV = 3
EPS = 1e-6
PAST_LEN = 16384

LANES = 128
SUBLANES = 8
VMEM_LIMIT = 56 * 1024 * 1024
CHUNK = 128
SAMPLE_ROWS = 16
GDN_SAMPLE_ROWS = 8


def _cparams(*sem):
    return pltpu.CompilerParams(dimension_semantics=sem, vmem_limit_bytes=VMEM_LIMIT)


def _resident(shape):
    return pl.BlockSpec(shape, lambda *_: (0,) * len(shape), pipeline_mode=pl.Buffered(1))


def _rms(x, g):
    return x * lax.rsqrt(jnp.mean(x * x, axis=-1, keepdims=True) + EPS) * g


def _dot(a, b):
    return jnp.dot(a, b, preferred_element_type=F32)


def _dot_nt(a, b):
    return lax.dot_general(a, b, (((1,), (1,)), ((), ())), preferred_element_type=F32)


def _silu(x):
    return x * jax.nn.sigmoid(x)


def _rope(seg, c, s1, s2):
    return seg * c + pltpu.roll(seg, LANES - 8, 1) * s1 + pltpu.roll(seg, 8, 1) * s2


PROJ_COLS = 512


def _proj_tile(x_ref, g_ref, c_ref, s1_ref, s2_ref, wqkv_ref, wd_ref, wab_ref, wg_ref,
               q_ref, k_ref, v_ref, dqkv_ref, dz_ref, ab_ref, ga_ref, gb_ref, dqkv_row0=0, win_refs=None):
    tm = x_ref.shape[0]
    xb = _rms(x_ref[...], g_ref[...]).astype(BF16)
    yield
    c, s1, s2 = c_ref[...], s1_ref[...], s2_ref[...]
    z = _dot(xb, wqkv_ref[...])
    for i in range(A_WIDTH // LANES):
        sl = slice(i * LANES, (i + 1) * LANES)
        q_ref[:, sl] = _rope(z[:, sl], c, s1, s2).astype(BF16)
    k = _rope(z[:, A_WIDTH:A_WIDTH + LANES], c, s1, s2)
    k_ref[...] = k
    v_ref[...] = z[:, A_WIDTH + LANES:]
    if win_refs is not None:
        win_refs[0][0] = k[tm - WINDOW:].T
        win_refs[1][0] = z[tm - WINDOW:, A_WIDTH + LANES:].T
    yield
    pc = PROJ_COLS
    for j in range(3 * DN_WIDTH // pc):
        dqkv_ref[dqkv_row0:dqkv_row0 + tm, j * pc:(j + 1) * pc] = _dot(xb, wd_ref[:, j * pc:(j + 1) * pc])
        yield
    dz_ref[...] = _dot(xb, wd_ref[:, 3 * DN_WIDTH:])
    ab_ref[...] = _dot(xb, wab_ref[...])
    yield
    for dst, base in ((ga_ref, 0), (gb_ref, D_MODEL)):
        for j in range(D_MODEL // pc):
            dst[:, j * pc:(j + 1) * pc] = _dot(xb, wg_ref[:, base + j * pc:base + (j + 1) * pc])
            yield


def _proj_kernel(*refs):
    _drain(_proj_tile(*refs))


def _proj(x, g, tabs, wqkv, wd, wab, wg, tm):
    m = x.shape[0]
    nt = tabs[0].shape[0] // tm
    row = lambda w: pl.BlockSpec((tm, w), lambda i: (i, 0))
    tab = pl.BlockSpec((tm, LANES), lambda i: (i % nt, 0))
    widths = (A_WIDTH, LANES, LANES, 3 * DN_WIDTH, DN_WIDTH, LANES, D_MODEL, D_MODEL)
    dts = (BF16, F32, F32, F32, F32, F32, F32, F32)
    return pl.pallas_call(
        _proj_kernel,
        grid=(m // tm,),
        in_specs=[row(D_MODEL), _resident((1, D_MODEL)), tab, tab, tab,
                  _resident(wqkv.shape), _resident(wd.shape), _resident(wab.shape), _resident(wg.shape)],
        out_specs=[row(w) for w in widths],
        out_shape=[jax.ShapeDtypeStruct((m, w), d) for w, d in zip(widths, dts)],
        compiler_params=_cparams("parallel"),
        name="in_proj",
    )(x, g, *tabs, wqkv, wd, wab, wg)


def _rope_tables(pos):
    half = ROT_DIM // 2
    inv = ROPE_THETA ** (-2.0 * jnp.arange(half, dtype=F32) / ROT_DIM)
    d = jnp.arange(LANES, dtype=jnp.int32) % A_HEAD_DIM
    ang = pos.astype(F32)[:, None] * inv[d % half][None, :]
    c, s = jnp.cos(ang), jnp.sin(ang)
    return (jnp.where(d < ROT_DIM, c, 1.0), jnp.where(d < half, -s, 0.0),
            jnp.where((d >= half) & (d < ROT_DIM), s, 0.0))


def _both_halves(t, lane_lo):
    r = pltpu.roll(t, A_HEAD_DIM, 1)
    return jnp.where(lane_lo, t, r), jnp.where(lane_lo, r, t)


def _sink_softmax(s, valid, sink):
    s = jnp.where(valid, s, -jnp.inf)
    m = jnp.maximum(jnp.max(s, axis=-1, keepdims=True), sink)
    p = jnp.exp(s - m)
    den = jnp.sum(p, axis=-1, keepdims=True) + jnp.exp(sink - m)
    return p * (1.0 / den)


def _swa_tile(sink_ref, q_ref, k, v, first):
    w = WINDOW
    lane_lo_k = lax.broadcasted_iota(jnp.int32, k.shape, 1) < A_HEAD_DIM
    kk = [t.astype(BF16) for t in _both_halves(k, lane_lo_k)]
    vv = [t.astype(BF16) for t in _both_halves(v, lane_lo_k)]
    lane_lo = lax.broadcasted_iota(jnp.int32, (w, LANES), 1) < A_HEAD_DIM
    row = lax.broadcasted_iota(jnp.int32, (4 * w, 2 * w), 0)
    col = lax.broadcasted_iota(jnp.int32, (4 * w, 2 * w), 1)
    d = (row & (w - 1)) + w - col
    band = (d >= 0) & (d < w)
    band_first = band & (col >= jnp.where(first, w, 0))
    hrow = lax.broadcasted_iota(jnp.int32, (4 * w, 1), 0) // w
    sinks = []
    for g in range(A_KV_HEADS):
        sink = jnp.zeros((4 * w, 1), F32)
        for j in range(4):
            sink = jnp.where(hrow == j, sink_ref[4 * g + j], sink)
        sinks.append(sink)
    zero = jnp.zeros((), BF16)
    scale = jnp.asarray(A_HEAD_DIM ** -0.5, BF16)
    blocks = []
    for b in range(q_ref.shape[0] // w):
        segs = []
        for g in range(A_KV_HEADS):
            parts = []
            for sgm in range(2):
                seg = q_ref[b * w:(b + 1) * w, (2 * g + sgm) * LANES:(2 * g + sgm + 1) * LANES] * scale
                parts += [jnp.where(lane_lo, seg, zero), jnp.where(lane_lo, zero, seg)]
            qs = jnp.concatenate(parts, axis=0)
            s = _dot_nt(qs, kk[g][b * w:(b + 2) * w])
            p = _sink_softmax(s, band_first if b == 0 else band, sinks[g]).astype(BF16)
            o = _dot(p, vv[g][b * w:(b + 2) * w])
            segs += [jnp.where(lane_lo, o[(2 * sgm) * w:(2 * sgm + 1) * w],
                               o[(2 * sgm + 1) * w:(2 * sgm + 2) * w]).astype(BF16) for sgm in range(2)]
        blocks.append(jnp.concatenate(segs, axis=1))
    return jnp.concatenate(blocks, axis=0)


def _swa_sample_kernel(sink_ref, q_ref, ckt_ref, cvt_ref, kn_ref, vn_ref, knt_ref, vnt_ref,
                       o_ref, cko_ref, cvo_ref, *, nb, t):
    w = ckt_ref.shape[3]
    old = lax.broadcasted_iota(jnp.int32, (A_HEAD_DIM, w), 1) < w - t
    for src_ref, new_ref, dst_ref in ((ckt_ref, knt_ref, cko_ref), (cvt_ref, vnt_ref, cvo_ref)):
        for g in range(A_KV_HEADS):
            new = new_ref[0, g]
            for b in range(nb):
                dst_ref[b, g] = jnp.where(old, pltpu.roll(src_ref[b, g], w - t, 1),
                                          pltpu.roll(new, (w - t - t * b) % LANES, 1))
    rows = q_ref.shape[1] // A_HEADS
    gq = rows * (A_HEADS // A_KV_HEADS)
    r = lax.broadcasted_iota(jnp.int32, (gq, w), 0)
    c = lax.broadcasted_iota(jnp.int32, (gq, w), 1)
    valid_c = c > r % rows
    rn = lax.broadcasted_iota(jnp.int32, (gq, kn_ref.shape[1]), 0)
    cn = lax.broadcasted_iota(jnp.int32, (gq, kn_ref.shape[1]), 1)
    tn = cn // A_KV_HEADS
    causal_n = (tn <= rn % rows) & (tn < t)
    hrow = lax.broadcasted_iota(jnp.int32, (gq, 1), 0) // rows
    valid_n, sink = [], []
    for g in range(A_KV_HEADS):
        valid_n.append(causal_n & (cn % A_KV_HEADS == g))
        sk = jnp.zeros((gq, 1), F32)
        for j in range(A_HEADS // A_KV_HEADS):
            sk = jnp.where(hrow == j, sink_ref[g * (A_HEADS // A_KV_HEADS) + j], sk)
        sink.append(sk)
    probs = [(b, g) for b in range(nb) for g in range(A_KV_HEADS)]
    scale = A_HEAD_DIM ** -0.5
    qs = [q_ref[b, g * gq:(g + 1) * gq, :] for b, g in probs]
    scs = [jnp.where(valid_c, _dot(q, ckt_ref[b, g].astype(BF16)) * scale, -jnp.inf)
           for q, (b, g) in zip(qs, probs)]
    sns = [jnp.where(valid_n[g], _dot_nt(q, kn_ref[b]) * scale, -jnp.inf) for q, (b, g) in zip(qs, probs)]
    ms = [jnp.maximum(jnp.maximum(jnp.max(sc, -1, keepdims=True), jnp.max(sn, -1, keepdims=True)), sink[g])
          for sc, sn, (b, g) in zip(scs, sns, probs)]
    pcs = [jnp.exp(sc - m) for sc, m in zip(scs, ms)]
    pns = [jnp.exp(sn - m) for sn, m in zip(sns, ms)]
    invs = [1.0 / (jnp.sum(pc, -1, keepdims=True) + jnp.sum(pn, -1, keepdims=True) + jnp.exp(sink[g] - m))
            for pc, pn, m, (b, g) in zip(pcs, pns, ms, probs)]
    for pc, pn, inv, (b, g) in zip(pcs, pns, invs, probs):
        o_ref[b, g * gq:(g + 1) * gq, :] = (_dot_nt((pc * inv).astype(BF16), cvt_ref[b, g].astype(BF16))
                                            + _dot((pn * inv).astype(BF16), vn_ref[b]))


def _swa_sample(q, ckt, cvt, kn, vn, knt, vnt, sinks, t):
    bsz, nq, _ = q.shape
    nb = bsz // knt.shape[0]
    assert nb * t <= LANES and ckt.shape[3] == LANES
    blk = lambda a, n=nb: pl.BlockSpec((n,) + a.shape[1:], lambda i: (i,) + (0,) * (a.ndim - 1))
    return pl.pallas_call(
        functools.partial(_swa_sample_kernel, nb=nb, t=t),
        grid=(bsz // nb,),
        in_specs=[pl.BlockSpec(memory_space=pltpu.SMEM), blk(q), blk(ckt), blk(cvt), blk(kn), blk(vn),
                  blk(knt, 1), blk(vnt, 1)],
        out_specs=[blk(q), blk(ckt), blk(cvt)],
        out_shape=[jax.ShapeDtypeStruct(q.shape, F32), jax.ShapeDtypeStruct(ckt.shape, F32),
                   jax.ShapeDtypeStruct(cvt.shape, F32)],
        compiler_params=_cparams("parallel"),
        name="swa_sample",
    )(sinks, q, ckt, cvt, kn, vn, knt, vnt)


def _lane_bcast(x, lane):
    return jnp.broadcast_to(x[:, lane:lane + 1], (x.shape[0], LANES))


def _cumsum_rows(x, block):
    c = x.shape[0]
    rowi = lax.broadcasted_iota(jnp.int32, (c, c), 0)
    coli = lax.broadcasted_iota(jnp.int32, (c, c), 1)
    ones = (((rowi // block) == (coli // block)) & (rowi >= coli)).astype(BF16)
    hi = x.astype(BF16)
    r1 = x - hi.astype(F32)
    mid = r1.astype(BF16)
    lo = (r1 - mid.astype(F32)).astype(BF16)
    s = _dot(ones, jnp.concatenate([hi, mid, lo], axis=1))
    return s[:, :LANES] + s[:, LANES:2 * LANES] + s[:, 2 * LANES:]


def _l2n(t):
    return t * lax.rsqrt(jnp.sum(t * t, axis=-1, keepdims=True) + EPS)


def _gates(ab, alog, dtb):
    x = ab + dtb
    sp = jnp.maximum(x, 0.0) + jnp.log1p(jnp.exp(-jnp.abs(x)))
    return -jnp.exp(alog) * sp, jax.nn.sigmoid(ab)


def _merge_masks(c, top):
    rowi = lax.broadcasted_iota(jnp.int32, (c, c), 0)
    coli = lax.broadcasted_iota(jnp.int32, (c, c), 1)
    masks = []
    s = 1
    while s < top:
        rb, cb = rowi // s, coli // s
        masks.append(((rb // 2) == (cb // 2)) & ((rb % 2) == 1) & ((cb % 2) == 0))
        s *= 2
    return masks


def _each(f, *lists):
    return [f(*t) for t in zip(*lists)]


def _drain(stages):
    try:
        while True:
            next(stages)
    except StopIteration as done:
        return done.value


def _interleave(*staged):
    live = list(staged)
    while live:
        for item in tuple(live):
            stages, per_turn = item
            try:
                for _ in range(per_turn):
                    next(stages)
            except StopIteration:
                live.remove(item)


def _chunk_local(qs, ks, vs, gcols, grows, betas, tril, merges):
    c = qs[0].shape[0]
    decays = _each(lambda gc, gr: jnp.exp(jnp.minimum(gc - gr, 0.0)), gcols, grows)
    kbs = _each(lambda k, b: k * b, ks, betas)
    yield
    ms = _each(lambda q, kb, k: _dot_nt(jnp.concatenate([q, kb], axis=0).astype(BF16), k.astype(BF16)),
               qs, kbs, ks)
    yield
    qks = _each(lambda m, d: jnp.where(tril, m[:c] * d, 0.0), ms, decays)
    a = _each(lambda m, d: m[c:] * d, ms, decays)
    eye = (lax.broadcasted_iota(jnp.int32, (c, c), 0) == lax.broadcasted_iota(jnp.int32, (c, c), 1)).astype(F32)
    ts = [eye - jnp.where(merges[0], x, 0.0) if merges else eye for x in a]
    yield
    for off in merges[1:]:
        tbs = [t.astype(BF16) for t in ts]
        zs = _each(lambda x, tb: _dot(jnp.where(off, x, 0.0).astype(BF16), tb), a, tbs)
        yield
        ts = _each(lambda t, tb, z: t - _dot(tb, z.astype(BF16)), ts, tbs, zs)
        yield
    ns = [t - eye for t in ts]
    egs = [jnp.exp(gc) for gc in gcols]
    rhss = _each(lambda v, b, kb, eg: jnp.concatenate([v * b, kb * eg], axis=1), vs, betas, kbs, egs)
    yield
    uws = _each(lambda r, n: r + _dot(n.astype(BF16), r.astype(BF16)), rhss, ns)
    yield
    return ([x[:, :LANES] for x in uws], [x[:, LANES:] for x in uws], qks,
            _each(lambda q, eg: q * eg, qs, egs))


def _gdn_tile(xp_ref, dz_ref, ab_ref, cw_ref, alog_ref, dtb_ref, ng_ref, y_ref, s_ref, nc):
    c = CHUNK
    pad = SUBLANES
    cw = cw_ref[...]
    g, beta = _gates(ab_ref[...], alog_ref[...], dtb_ref[...])
    rowi = lax.broadcasted_iota(jnp.int32, (c, c), 0)
    coli = lax.broadcasted_iota(jnp.int32, (c, c), 1)
    tril = rowi >= coli
    merges = _merge_masks(c, c)
    ng = ng_ref[...]
    heads = range(DN_HEADS)
    probs = [(ci, h) for ci in range(nc) for h in heads]
    rows = lambda ci: slice(ci * c, (ci + 1) * c)
    lanes = lambda part, h: slice(part * DN_WIDTH + h * LANES, part * DN_WIDTH + (h + 1) * LANES)
    yield
    qs, ks, vs, gcols, grows, betas = [], [], [], [], [], []
    for ci in range(nc):
        conv = xp_ref[pad + ci * c:pad + (ci + 1) * c, :] * cw[DN_CONV - 1:DN_CONV]
        for j in range(DN_CONV - 1):
            o = pad - (DN_CONV - 1) + j + ci * c
            conv = conv + xp_ref[o:o + c, :] * cw[j:j + 1]
        conv = _silu(conv)
        gc = _cumsum_rows(g[rows(ci)], c)
        gct = gc.T
        qs += [_l2n(conv[:, lanes(0, h)]) * (DN_HEAD_DIM ** -0.5) for h in heads]
        ks += [_l2n(conv[:, lanes(1, h)]) for h in heads]
        vs += [conv[:, lanes(2, h)] for h in heads]
        gcols += [_lane_bcast(gc, h) for h in heads]
        grows += [gct[h:h + 1, :] for h in heads]
        betas += [_lane_bcast(beta[rows(ci)], DN_HEADS + h) for h in heads]
        yield
    us, ws, qks, qds = yield from _chunk_local(qs, ks, vs, gcols, grows, betas, tril, merges)
    glasts = [gc[c - 1:c, :] for gc in gcols]
    kdts = _each(lambda k, gl, gc: (k * jnp.exp(gl - gc)).T, ks, glasts, gcols)
    wqs = _each(lambda w, qd: jnp.concatenate([w, qd], axis=0).astype(BF16), ws, qds)
    qkks = _each(lambda qk, kdt: jnp.concatenate([qk, kdt], axis=0).astype(BF16), qks, kdts)
    yield
    ss = [s_ref[h] for h in heads]
    for ci in range(nc):
        pr = [ci * DN_HEADS + h for h in heads]
        r2s = [_dot(wqs[p], s.astype(BF16)) for p, s in zip(pr, ss)]
        vnews = [us[p] - r2[:c] for p, r2 in zip(pr, r2s)]
        yield
        r3s = [_dot(qkks[p], vn.astype(BF16)) for p, vn in zip(pr, vnews)]
        ss = [s * jnp.exp(glasts[p]) + r3[c:] for p, s, r3 in zip(pr, ss, r3s)]
        for h in heads:
            o = r2s[h][c:] + r3s[h][:c]
            y_ref[rows(ci), lanes(0, h)] = (_rms(o, ng) * _silu(dz_ref[rows(ci), lanes(0, h)])).astype(BF16)
        yield
    for h in heads:
        s_ref[h] = ss[h]


def _proj_gdn_kernel(x_ref, g_ref, c_ref, s1_ref, s2_ref, wqkv_ref, wd_ref, wab_ref, wg_ref,
                     cw_ref, alog_ref, dtb_ref, ng_ref,
                     q_ref, k_ref, v_ref, ga_ref, gb_ref, kw_ref, vw_ref, y_ref, sout_ref, tail_ref,
                     xp_ref, dz_ref, ab_ref, carry_ref, s_ref, *, nt, nc):
    i = pl.program_id(0)
    pad = SUBLANES
    r = nc * CHUNK
    slot_a = i % 2
    slot_b = 1 - slot_a

    @pl.when(i == 0)
    def _():
        xp_ref[1] = jnp.zeros(xp_ref.shape[1:], F32)
        dz_ref[1] = jnp.zeros(dz_ref.shape[1:], F32)
        ab_ref[1] = jnp.zeros(ab_ref.shape[1:], F32)

    @pl.when(jnp.maximum(i - 1, 0) % nt == 0)
    def _():
        carry_ref[...] = jnp.zeros_like(carry_ref)
        s_ref[...] = jnp.zeros_like(s_ref)

    xpb_ref = xp_ref.at[slot_b]
    xpb_ref[0:pad, :] = carry_ref[...]
    _interleave(
        (_gdn_tile(xpb_ref, dz_ref.at[slot_b], ab_ref.at[slot_b], cw_ref, alog_ref, dtb_ref, ng_ref,
                   y_ref, s_ref, nc), 1),
        (_proj_tile(x_ref, g_ref, c_ref, s1_ref, s2_ref, wqkv_ref, wd_ref, wab_ref, wg_ref, q_ref, k_ref, v_ref,
                    xp_ref.at[slot_a], dz_ref.at[slot_a], ab_ref.at[slot_a], ga_ref, gb_ref, pad,
                    (kw_ref, vw_ref)), 1))
    carry_ref[...] = xpb_ref[r:r + pad, :]
    tail_ref[0] = carry_ref[...]
    sout_ref[0] = s_ref[...]


def _proj_gdn(x, g, tabs, wqkv, wd, wab, wg, cw, alog, dtb, ng, nseq, tm):
    m = x.shape[0]
    nt = m // nseq // tm
    last = m // tm - 1
    nc = tm // CHUNK
    proj = lambda i: jnp.minimum(i, last)
    gdn = lambda i: jnp.maximum(i - 1, 0)
    prow = lambda w: pl.BlockSpec((tm, w), lambda i: (proj(i), 0))
    tab = pl.BlockSpec((tm, LANES), lambda i: (proj(i) % nt, 0))
    per_seq = lambda *dims: pl.BlockSpec((1,) + dims, lambda i: (gdn(i) // nt,) + (0,) * len(dims))
    win = pl.BlockSpec((1, A_KV_WIDTH, WINDOW), lambda i: (proj(i) // nt, 0, 0))
    win_shape = jax.ShapeDtypeStruct((nseq, A_KV_WIDTH, WINDOW), F32)
    state = (DN_HEADS, DN_HEAD_DIM, DN_HEAD_DIM)
    bufs = [pltpu.VMEM((2, tm + SUBLANES, 3 * DN_WIDTH), F32), pltpu.VMEM((2, tm, DN_WIDTH), F32),
            pltpu.VMEM((2, tm, LANES), F32)]
    return pl.pallas_call(
        functools.partial(_proj_gdn_kernel, nt=nt, nc=nc),
        grid=(m // tm + 1,),
        in_specs=[prow(D_MODEL), _resident((1, D_MODEL)), tab, tab, tab,
                  _resident(wqkv.shape), _resident(wd.shape), _resident(wab.shape), _resident(wg.shape),
                  _resident(cw.shape), _resident((1, LANES)), _resident((1, LANES)), _resident((1, LANES))],
        out_specs=[prow(A_WIDTH), prow(LANES), prow(LANES), prow(D_MODEL), prow(D_MODEL), win, win,
                   pl.BlockSpec((tm, DN_WIDTH), lambda i: (gdn(i), 0)), per_seq(*state),
                   per_seq(SUBLANES, 3 * DN_WIDTH)],
        out_shape=[jax.ShapeDtypeStruct((m, A_WIDTH), BF16), jax.ShapeDtypeStruct((m, LANES), F32),
                   jax.ShapeDtypeStruct((m, LANES), F32), jax.ShapeDtypeStruct((m, D_MODEL), F32),
                   jax.ShapeDtypeStruct((m, D_MODEL), F32), win_shape, win_shape,
                   jax.ShapeDtypeStruct((m, DN_WIDTH), BF16),
                   jax.ShapeDtypeStruct((nseq,) + state, F32),
                   jax.ShapeDtypeStruct((nseq, SUBLANES, 3 * DN_WIDTH), F32)],
        scratch_shapes=bufs + [pltpu.VMEM((SUBLANES, 3 * DN_WIDTH), F32), pltpu.VMEM(state, F32)],
        compiler_params=_cparams("arbitrary"),
        name="proj_gdn",
    )(x, g, *tabs, wqkv, wd, wab, wg, cw, alog, dtb, ng)


def _gdn_sample_kernel(buf_ref, x_ref, dz_ref, ab_ref, s0_ref, cw_ref, alog_ref, dtb_ref, ng_ref,
                       y_ref, sout_ref, xp_ref, ab16_ref):
    t = x_ref.shape[1]
    c = CHUNK
    sr = GDN_SAMPLE_ROWS
    nb = c // sr
    pad = SUBLANES
    hist = DN_CONV - 1
    xp_ref[...] = jnp.zeros_like(xp_ref)
    ab16_ref[...] = jnp.zeros_like(ab16_ref)
    for b in range(nb):
        xp_ref[pad + b * sr - hist:pad + b * sr, :] = buf_ref[b]
        xp_ref[pad + b * sr:pad + b * sr + t, :] = x_ref[b]
        ab16_ref[b * sr:b * sr + t, :] = ab_ref[b]
    cw = cw_ref[...]
    conv = xp_ref[pad:pad + c, :] * cw[hist:hist + 1]
    for j in range(hist):
        conv = conv + xp_ref[pad - hist + j:pad - hist + j + c, :] * cw[j:j + 1]
    conv = _silu(conv)
    live = (lax.broadcasted_iota(jnp.int32, (c, LANES), 0) % sr) < t
    g, beta = _gates(ab16_ref[...], alog_ref[...], dtb_ref[...])
    g = jnp.where(live, g, 0.0)
    beta = jnp.where(live, beta, 0.0)
    gc = _cumsum_rows(g, sr)
    gct = gc.T
    rowi = lax.broadcasted_iota(jnp.int32, (c, c), 0)
    coli = lax.broadcasted_iota(jnp.int32, (c, c), 1)
    tril = ((rowi // sr) == (coli // sr)) & (rowi >= coli)
    merges = _merge_masks(c, pl.next_power_of_2(t))
    ng = ng_ref[...]
    rowb = lax.broadcasted_iota(jnp.int32, (c, LANES), 0) // sr
    heads = range(DN_HEADS)
    lanes = lambda part, h: slice(part * DN_WIDTH + h * LANES, part * DN_WIDTH + (h + 1) * LANES)
    ks = [_l2n(conv[:, lanes(1, h)]) for h in heads]
    gcols = [_lane_bcast(gc, h) for h in heads]
    us, ws, qks, qds = _drain(_chunk_local(
        [_l2n(conv[:, lanes(0, h)]) * (DN_HEAD_DIM ** -0.5) for h in heads], ks,
        [conv[:, lanes(2, h)] for h in heads], gcols, [gct[h:h + 1, :] for h in heads],
        [_lane_bcast(beta, DN_HEADS + h) for h in heads], tril, merges))
    seqs = range(nb)
    rows = lambda b: slice(b * sr, (b + 1) * sr)
    s0s = [[s0_ref[b, h] for b in seqs] for h in heads]
    r2s = [[_dot(jnp.concatenate([ws[h][rows(b)], qds[h][rows(b)]], axis=0).astype(BF16), s0s[h][b].astype(BF16))
            for b in seqs] for h in heads]
    vnews = [jnp.concatenate([us[h][rows(b)] - r2s[h][b][:sr] for b in seqs], axis=0) for h in heads]
    os_ = [_rms(jnp.concatenate([r2s[h][b][sr:] for b in seqs], axis=0)
                + _dot(qks[h].astype(BF16), vnews[h].astype(BF16)), ng) for h in heads]
    for h in heads:
        for b in seqs:
            y_ref[b, :, lanes(0, h)] = os_[h][b * sr:b * sr + t] * _silu(dz_ref[b, :, lanes(0, h)])
    glasts = [jnp.concatenate([jnp.broadcast_to(gcols[h][(b + 1) * sr - 1:(b + 1) * sr, :], (sr, LANES))
                               for b in seqs], axis=0) for h in heads]
    kdts = [(ks[h] * jnp.exp(glasts[h] - gcols[h])).T.astype(BF16) for h in heads]
    for h in heads:
        for b in seqs:
            vb = jnp.where(rowb == b, vnews[h], 0.0).astype(BF16)
            sout_ref[b, h] = s0s[h][b] * jnp.exp(glasts[h][b * sr:b * sr + 1, :]) + _dot(kdts[h], vb)


def _gdn_sample(buf, x, dz, ab, s0, cw, alog, dtb, ng):
    bsz, t, _ = x.shape
    assert t + DN_CONV - 1 <= GDN_SAMPLE_ROWS, "too many new tokens for one row tile per sequence"
    nb = CHUNK // GDN_SAMPLE_ROWS
    seq = lambda a: pl.BlockSpec((nb,) + a.shape[1:], lambda i: (i,) + (0,) * (a.ndim - 1))
    return pl.pallas_call(
        _gdn_sample_kernel,
        grid=(bsz // nb,),
        in_specs=[seq(buf), seq(x), seq(dz), seq(ab), seq(s0), _resident(cw.shape),
                  _resident((1, LANES)), _resident((1, LANES)), _resident((1, LANES))],
        out_specs=[seq(dz), seq(s0)],
        out_shape=[jax.ShapeDtypeStruct(dz.shape, F32), jax.ShapeDtypeStruct(s0.shape, F32)],
        scratch_shapes=[pltpu.VMEM((CHUNK + SUBLANES, 3 * DN_WIDTH), F32), pltpu.VMEM((CHUNK, LANES), F32)],
        compiler_params=_cparams("parallel"),
        name="gdn_sample",
    )(buf, x, dz, ab, s0, cw, alog, dtb, ng)


def _merge_kernel(x_ref, *rest, attend):
    if attend:
        sink_ref, q_ref, kp_ref, kc_ref, vp_ref, vc_ref, *rest = rest
        ya = _swa_tile(sink_ref, q_ref, jnp.concatenate([kp_ref[...], kc_ref[...]], axis=0),
                       jnp.concatenate([vp_ref[...], vc_ref[...]], axis=0), pl.program_id(1) == 0)
    else:
        ya_ref, *rest = rest
        ya = ya_ref[...]
    yb_ref, ga_ref, gb_ref, wa_ref, wb_ref, wo_ref, g_ref, wq_ref, h_ref, hq_ref = rest
    mix = (jax.nn.sigmoid(ga_ref[...]) * _dot(ya.astype(BF16), wa_ref[...])
           + jax.nn.sigmoid(gb_ref[...]) * _dot(yb_ref[...].astype(BF16), wb_ref[...]))
    h = x_ref[...] + _dot(mix.astype(BF16), wo_ref[...])
    h_ref[...] = h
    hq_ref[...] = _dot(_rms(h, g_ref[...]).astype(BF16), wq_ref[...]).astype(BF16)


def _merge(x, attn, yb, ga, gb, wa, wb, wo, g, wq, nseq, tm):
    m = x.shape[0]
    nt = m // nseq // tm
    row = lambda w: pl.BlockSpec((tm, w), lambda s, j: (s * nt + j, 0))
    attend = isinstance(attn, tuple)
    if attend:
        sinks, q, k, v = attn
        per = tm // WINDOW
        prev = pl.BlockSpec((WINDOW, A_KV_WIDTH), lambda s, j: (jnp.maximum((s * nt + j) * per - 1, 0), 0))
        attn_args = [sinks, q, k, k, v, v]
        attn_specs = [pl.BlockSpec(memory_space=pltpu.SMEM), row(A_WIDTH), prev, row(A_KV_WIDTH),
                      prev, row(A_KV_WIDTH)]
    else:
        attn_args, attn_specs = [attn], [row(A_WIDTH)]
    return pl.pallas_call(
        functools.partial(_merge_kernel, attend=attend),
        grid=(nseq, nt),
        in_specs=[row(D_MODEL), *attn_specs, row(DN_WIDTH), row(D_MODEL), row(D_MODEL),
                  _resident(wa.shape), _resident(wb.shape), _resident(wo.shape),
                  _resident((1, D_MODEL)), _resident(wq.shape)],
        out_specs=[row(D_MODEL), row(X_WIDTH)],
        out_shape=[jax.ShapeDtypeStruct((m, D_MODEL), F32), jax.ShapeDtypeStruct((m, X_WIDTH), BF16)],
        compiler_params=_cparams("parallel", "parallel"),
        name="merge",
    )(x, *attn_args, yb, ga, gb, wa, wb, wo, g, wq)


def _memkv_kernel(m_ref, g_ref, w_ref, k_ref, v_ref, kf_ref, vf_ref):
    tm = m_ref.shape[0]
    z = _dot(_rms(m_ref[...], g_ref[...]).astype(BF16), w_ref[...])
    k_ref[...] = z[:, :X_WIDTH]
    v_ref[...] = z[:, X_WIDTH:]
    for h in range(X_HEADS):
        kf_ref[pl.ds(h, tm, stride=X_HEADS), :] = z[:, h * LANES:(h + 1) * LANES]
        vf_ref[pl.ds(h, tm, stride=X_HEADS), :] = z[:, X_WIDTH + h * LANES:X_WIDTH + (h + 1) * LANES]


def _memkv(mem, g, w, tm):
    m = mem.shape[0]
    row = lambda wd: pl.BlockSpec((tm, wd), lambda i: (i, 0))
    flat = pl.BlockSpec((tm * X_HEADS, X_HEAD_DIM), lambda i: (i, 0))
    return pl.pallas_call(
        _memkv_kernel,
        grid=(m // tm,),
        in_specs=[row(D_MODEL), _resident((1, D_MODEL)), _resident(w.shape)],
        out_specs=[row(X_WIDTH), row(X_WIDTH), flat, flat],
        out_shape=[jax.ShapeDtypeStruct((m, X_WIDTH), F32)] * 2
        + [jax.ShapeDtypeStruct((m * X_HEADS, X_HEAD_DIM), F32)] * 2,
        compiler_params=_cparams("parallel"),
        name="mem_kv",
    )(mem, g, w)


def _softmax_rows(s):
    e = jnp.exp(s - jnp.max(s, axis=-1, keepdims=True))
    return e * (1.0 / jnp.sum(e, axis=-1, keepdims=True))


def _xattn_sample_kernel(hq_ref, mk_ref, mv_ref, o_ref, *, nb):
    rows = hq_ref.shape[1]
    nmh = mk_ref.shape[1]
    qh = lax.broadcasted_iota(jnp.int32, (X_HEADS * rows, nmh), 0) // rows
    mh = lax.broadcasted_iota(jnp.int32, (X_HEADS * rows, nmh), 1) % X_HEADS
    same = qh == mh
    seqs = range(nb)
    qs = [jnp.concatenate([hq_ref[b, :, h * LANES:(h + 1) * LANES] for h in range(X_HEADS)], axis=0)
          for b in seqs]
    ss = [_dot_nt(qs[b], mk_ref[b].astype(BF16)) * (X_HEAD_DIM ** -0.5) for b in seqs]
    ps = [_softmax_rows(jnp.where(same, s, -jnp.inf)).astype(BF16) for s in ss]
    os_ = [_dot(ps[b], mv_ref[b].astype(BF16)) for b in seqs]
    for b in seqs:
        for h in range(X_HEADS):
            o_ref[b, :, h * LANES:(h + 1) * LANES] = os_[b][h * rows:(h + 1) * rows].astype(BF16)


def _xattn_sample(hq, mk, mv, nb=8):
    bsz, rows, _ = hq.shape
    q = pl.BlockSpec((nb, rows, X_WIDTH), lambda i: (i, 0, 0))
    mem = pl.BlockSpec((nb, mk.shape[1], X_HEAD_DIM), lambda i: (i, 0, 0))
    return pl.pallas_call(
        functools.partial(_xattn_sample_kernel, nb=nb),
        grid=(bsz // nb,),
        in_specs=[q, mem, mem],
        out_specs=q,
        out_shape=jax.ShapeDtypeStruct(hq.shape, BF16),
        compiler_params=_cparams("parallel"),
        name="xattn_sample",
    )(hq, mk, mv)


def _xattn_tile(hq_ref, mk_ref, mv_ref):
    outs = []
    for h in range(X_HEADS):
        hs = slice(h * LANES, (h + 1) * LANES)
        s = _dot_nt(hq_ref[:, hs], mk_ref[:, hs].astype(BF16)) * (X_HEAD_DIM ** -0.5)
        outs.append(_dot(_softmax_rows(s).astype(BF16), mv_ref[:, hs].astype(BF16)).astype(BF16))
    return jnp.concatenate(outs, axis=1)


def _ffn_kernel(h_ref, *rest, stride, padc, fc, attend):
    if attend:
        hq_ref, mk_ref, mv_ref, *rest = rest
        xo = _xattn_tile(hq_ref, mk_ref, mv_ref)
    else:
        xo_ref, *rest = rest
        xo = xo_ref[...]
    cin_ref, wxo_ref, g_ref, wu_ref, wv_ref, cw_ref, wd_ref, fg_ref, y_ref, tail_ref, xp_ref, acc_ref = rest
    tm = h_ref.shape[0]

    @pl.when(pl.program_id(1) == 0)
    def _():
        tail_ref[...] = cin_ref[...]

    h = h_ref[...] + _dot(xo, wxo_ref[...])
    acc_ref[...] = h
    hn = _rms(h, g_ref[...]).astype(BF16)
    for lo, hi in zip((0,) + fc, fc + (D_FF,)):
        cs = slice(lo, hi)
        n = hi - lo
        u = _dot(hn, wu_ref[:, cs])
        gate = _dot(hn, wv_ref[:, cs])
        xp_ref[0:padc, 0:n] = tail_ref[0, :, cs]
        xp_ref[padc:padc + tm, 0:n] = u
        cw = cw_ref[:, cs]
        uc = u * cw[2:3]
        for j in range(FFN_CONV - 1):
            o = padc - (FFN_CONV - 1 - j) * stride
            uc = uc + xp_ref[o:o + tm, 0:n] * cw[j:j + 1]
        tail_ref[0, :, cs] = xp_ref[tm:tm + padc, 0:n]
        acc_ref[...] += _dot((_silu(uc) * gate).astype(BF16), wd_ref[cs, :])
    y_ref[...] = _rms(acc_ref[...], fg_ref[...])


FFN_CUTS = (1536,)


def _ffn(h, xq, mem, cin, wxo, g, wup, cw, wd, fg, tm, stride, fc=FFN_CUTS):
    m = h.shape[0]
    half = lambda c: pl.BlockSpec((D_MODEL, D_FF), lambda *_: (0, c), pipeline_mode=pl.Buffered(1))
    nseq, padc, _ = cin.shape
    nt = m // nseq // tm
    row = lambda w: pl.BlockSpec((tm, w), lambda s, j: (s * nt + j, 0))
    car = pl.BlockSpec((1, padc, D_FF), lambda s, j: (s, 0, 0))
    mem = list(mem or ())
    mem_specs = [pl.BlockSpec((a.shape[0] // nseq, X_WIDTH), lambda s, j: (s, 0)) for a in mem]
    return pl.pallas_call(
        functools.partial(_ffn_kernel, stride=stride, padc=padc, fc=fc, attend=bool(mem)),
        grid=(nseq, nt),
        in_specs=[row(D_MODEL), row(X_WIDTH), *mem_specs, car, _resident(wxo.shape), _resident((1, D_MODEL)),
                  half(0), half(1), _resident(cw.shape), _resident(wd.shape), _resident((1, D_MODEL))],
        out_specs=[row(D_MODEL), car],
        out_shape=[jax.ShapeDtypeStruct((m, D_MODEL), F32), jax.ShapeDtypeStruct(cin.shape, F32)],
        scratch_shapes=[pltpu.VMEM((tm + padc, max(b - a for a, b in zip((0,) + fc, fc + (D_FF,)))), F32),
                        pltpu.VMEM((tm, D_MODEL), F32)],
        compiler_params=_cparams("parallel", "arbitrary"),
        name="ffn",
    )(h, xq, *mem, cin, wxo, g, wup, wup, cw, wd, fg)


def _lane_row(vec, offset=0):
    return jnp.zeros((1, LANES), F32).at[0, offset:offset + vec.shape[0]].set(vec.astype(F32))


def _prep(p):
    w_in = p["w_in"]
    o = 0
    cuts = {}
    for name, wdt in (("qkv", A_WIDTH + 2 * A_KV_WIDTH), ("d", 4 * DN_WIDTH), ("ab", 2 * DN_HEADS),
                      ("g", 2 * D_MODEL)):
        cuts[name] = w_in[:, o:o + wdt]
        o += wdt
    w = {
        "wqkv": cuts["qkv"].astype(BF16),
        "wd": cuts["d"].astype(BF16),
        "wab": jnp.pad(cuts["ab"], ((0, 0), (0, LANES - 2 * DN_HEADS))).astype(BF16),
        "wg": cuts["g"].astype(BF16),
        "norm_mix_g": p["norm_mix_g"].reshape(1, D_MODEL),
        "dn_conv_w": p["dn_conv_w"],
        "alog": _lane_row(p["dn_a_log"]),
        "dtb": _lane_row(p["dn_dt_bias"]),
        "dn_norm_g": p["dn_norm_g"].reshape(1, DN_HEAD_DIM),
        "sinks": p["sinks"].astype(F32),
        "w_br_a": p["w_br_a"].astype(BF16),
        "w_br_b": p["w_br_b"].astype(BF16),
        "w_mix_out": p["w_mix_out"].astype(BF16),
        "norm_x_g": p["norm_x_g"].reshape(1, D_MODEL),
        "w_xq": p["w_xq"].astype(BF16),
        "w_xo": p["w_xo"].astype(BF16),
        "norm_ffn_g": p["norm_ffn_g"].reshape(1, D_MODEL),
        "w_up": p["w_up"].astype(BF16),
        "ffn_conv_w": p["ffn_conv_w"],
        "w_down": p["w_down"].astype(BF16),
        "final_norm_g": p["final_norm_g"].reshape(1, D_MODEL),
    }
    return w


def _tile(m, pref):
    return pref if m % pref == 0 else m


def _prompt_layer(x, mem, norm_mem_g, w_xkv, w):
    bsz, seqlen, _ = x.shape
    m = bsz * seqlen
    x2 = x.reshape(m, D_MODEL)
    tm = _tile(seqlen, 512)
    tabs = _rope_tables(jnp.arange(seqlen, dtype=jnp.int32))
    assert seqlen >= WINDOW, "the window outputs are taken from a full last window"
    q, k, v, ga, gb, kwin, vwin, yb, dn_state, dn_tail = _proj_gdn(
        x2, w["norm_mix_g"], tabs, w["wqkv"], w["wd"], w["wab"], w["wg"], w["dn_conv_w"], w["alog"], w["dtb"],
        w["dn_norm_g"], bsz, tm)
    h, hq = _merge(x2, (w["sinks"], q, k, v), yb, ga, gb, w["w_br_a"], w["w_br_b"], w["w_mix_out"],
                   w["norm_x_g"], w["w_xq"], bsz, tm)
    nm = mem.shape[1]
    mk, mv, mk_rows, mv_rows = _memkv(mem.reshape(bsz * nm, D_MODEL), norm_mem_g.reshape(1, D_MODEL),
                                      w_xkv.astype(BF16), _tile(bsz * nm, 512))
    cin = jnp.zeros((bsz, SUBLANES, D_FF), F32)
    y, tail = _ffn(h, hq, (mk, mv), cin, w["w_xo"], w["norm_ffn_g"], w["w_up"], w["ffn_conv_w"],
                   w["w_down"], w["final_norm_g"], tm, 1)
    win = lambda a: jnp.transpose(a.reshape(bsz, A_KV_HEADS, A_HEAD_DIM, WINDOW), (0, 3, 1, 2))
    new = (
        win(kwin),
        win(vwin),
        dn_tail[:, SUBLANES - (DN_CONV - 1):],
        dn_state,
        mk_rows.reshape(bsz, nm, X_HEADS, X_HEAD_DIM),
        mv_rows.reshape(bsz, nm, X_HEADS, X_HEAD_DIM),
        tail[:, SUBLANES - (FFN_CONV - 1):],
    )
    return y.reshape(bsz, seqlen, D_MODEL), new


def _sample_layer(x, pos0, win_k, win_v, dn_buf, dn_state, mem_k, mem_v, ffn_buf, w):
    bsz, t, _ = x.shape
    m = bsz * t
    sr = SAMPLE_ROWS
    x2 = x.reshape(m, D_MODEL)
    tabs = _rope_tables(jnp.tile(pos0 + jnp.arange(t, dtype=jnp.int32), bsz))
    q, k, v, dqkv, dz, ab, ga, gb = _proj(x2, w["norm_mix_g"], tabs, w["wqkv"], w["wd"], w["wab"], w["wg"], m)

    wlen = win_k.shape[1]
    qh = jnp.transpose(q.reshape(bsz, t, A_HEADS, A_HEAD_DIM), (0, 2, 1, 3)).reshape(bsz, A_HEADS * t, A_HEAD_DIM)
    padrows = lambda a: jnp.pad(a.reshape(bsz, t, -1), ((0, 0), (0, sr - t), (0, 0)))
    newrows = lambda a: jnp.pad(a.reshape(bsz, t * A_KV_HEADS, A_HEAD_DIM),
                                ((0, 0), (0, sr - t * A_KV_HEADS), (0, 0))).astype(BF16)
    kv_major = lambda a: jnp.transpose(a, (0, 2, 3, 1))
    nbw = min(bsz, LANES // t)

    def new_t(a):
        a = jnp.transpose(a.reshape(bsz // nbw, nbw * t, A_KV_HEADS, A_HEAD_DIM), (0, 2, 3, 1))
        return jnp.pad(a, ((0, 0), (0, 0), (0, 0), (0, LANES - nbw * t)))

    oh, new_wk, new_wv = _swa_sample(qh, kv_major(win_k), kv_major(win_v), newrows(k), newrows(v),
                                     new_t(k), new_t(v), w["sinks"], t)
    ya = jnp.transpose(oh.reshape(bsz, A_HEADS, t, A_HEAD_DIM), (0, 2, 1, 3)).reshape(m, A_WIDTH)
    new_wk, new_wv = (jnp.transpose(a, (0, 3, 1, 2)) for a in (new_wk, new_wv))

    dqkv3 = dqkv.reshape(bsz, t, 3 * DN_WIDTH)
    yb, new_s = _gdn_sample(dn_buf, dqkv3, dz.reshape(bsz, t, DN_WIDTH), ab.reshape(bsz, t, LANES), dn_state,
                            w["dn_conv_w"], w["alog"], w["dtb"], w["dn_norm_g"])
    yb = yb.reshape(m, DN_WIDTH)
    new_dn_buf = jnp.concatenate([dn_buf, dqkv3], axis=1)[:, -(DN_CONV - 1):]

    h, hq = _merge(x2, ya, yb, ga, gb, w["w_br_a"], w["w_br_b"], w["w_mix_out"], w["norm_x_g"], w["w_xq"], 1, m)
    nm = mem_k.shape[1]
    xo = _xattn_sample(padrows(hq), mem_k.reshape(bsz, nm * X_HEADS, X_HEAD_DIM),
                       mem_v.reshape(bsz, nm * X_HEADS, X_HEAD_DIM))
    xo = xo[:, :t]

    tmaj = lambda a: jnp.transpose(a.reshape(bsz, t, -1), (1, 0, 2)).reshape(m, -1)
    cin = jnp.transpose(ffn_buf, (1, 0, 2)).reshape(1, (FFN_CONV - 1) * bsz, D_FF)
    y, tail = _ffn(tmaj(h), tmaj(xo), None, cin, w["w_xo"], w["norm_ffn_g"], w["w_up"], w["ffn_conv_w"],
                   w["w_down"], w["final_norm_g"], m, bsz)
    y = jnp.transpose(y.reshape(t, bsz, D_MODEL), (1, 0, 2))
    new_ffn = jnp.transpose(tail.reshape(FFN_CONV - 1, bsz, D_FF), (1, 0, 2))
    return y, (new_wk, new_wv, new_dn_buf, new_s, new_ffn)


def kernel(x_prompt, x_sample, mem_prompt, cache_win_k, cache_win_v, state_dn_conv, state_dn, cache_mem_k, cache_mem_v, state_ffn_conv, norm_mix_g, w_in, dn_conv_w, dn_a_log, dn_dt_bias, dn_norm_g, attn_sinks, w_br_a, w_br_b, w_mix_out, norm_x_g, norm_mem_g, w_xq, w_xkv, w_xo, norm_ffn_g, w_up, ffn_conv_w, w_down, final_norm_g):
    p = {"norm_mix_g": norm_mix_g[0], "w_in": w_in[0], "dn_conv_w": dn_conv_w[0], "dn_a_log": dn_a_log[0],
         "dn_dt_bias": dn_dt_bias[0], "dn_norm_g": dn_norm_g[0], "sinks": attn_sinks[0], "w_br_a": w_br_a[0],
         "w_br_b": w_br_b[0], "w_mix_out": w_mix_out[0], "norm_x_g": norm_x_g[0], "w_xq": w_xq[0],
         "w_xo": w_xo[0], "norm_ffn_g": norm_ffn_g[0], "w_up": w_up[0], "ffn_conv_w": ffn_conv_w[0],
         "w_down": w_down[0], "final_norm_g": final_norm_g}
    w = _prep(p)
    yp, newp = _prompt_layer(x_prompt, mem_prompt, norm_mem_g[0], w_xkv[0], w)
    ys, news = _sample_layer(x_sample, PAST_LEN, cache_win_k[0], cache_win_v[0], state_dn_conv[0], state_dn[0],
                             cache_mem_k[0], cache_mem_v[0], state_ffn_conv[0], w)
    lead = lambda a: a[None]
    p_win_k, p_win_v, p_dn_conv, p_dn_state, p_mem_k, p_mem_v, p_ffn_conv = [lead(a) for a in newp]
    s_win_k, s_win_v, s_dn_conv, s_dn_state, s_ffn_conv = [lead(a) for a in news]
    return (yp, ys, p_win_k, p_win_v, p_dn_conv, p_dn_state, p_mem_k, p_mem_v, p_ffn_conv,
            s_win_k, s_win_v, s_dn_conv, s_dn_state, s_ffn_conv)
```

```python
import functools
import math

import jax
import jax.numpy as jnp
from jax import lax
from jax.experimental import pallas as pl
from jax.experimental.pallas import tpu as pltpu

F32 = jnp.float32
BF16 = jnp.bfloat16

D_MODEL = 1024
A_HEADS = 8
A_KV_HEADS = 2
A_HEAD_DIM = 64
A_WIDTH = 512
A_KV_WIDTH = 128
WINDOW = 128
ROT_DIM = 16
ROPE_THETA = 500000.0
DN_HEADS = 4
DN_HEAD_DIM = 128
DN_WIDTH = 512
DN_CONV = 4
X_HEADS = 4
X_HEAD_DIM = 128
X_WIDTH = 512
D_FF = 2816
FFN_CONV = 3
EPS = 1e-6
PAST_LEN = 16384

LANES = 128
SUBLANES = 8
VMEM_LIMIT = 56 * 1024 * 1024
CHUNK = 128
SAMPLE_ROWS = 16
GDN_SAMPLE_ROWS = 8


def _cparams(*sem):
    return pltpu.CompilerParams(dimension_semantics=sem, vmem_limit_bytes=VMEM_LIMIT)


def _resident(shape):
    return pl.BlockSpec(shape, lambda *_: (0,) * len(shape), pipeline_mode=pl.Buffered(1))


def _rms(x, g):
    return x * lax.rsqrt(jnp.mean(x * x, axis=-1, keepdims=True) + EPS) * g


def _dot(a, b):
    return jnp.dot(a, b, preferred_element_type=F32)


def _dot_nt(a, b):
    return lax.dot_general(a, b, (((1,), (1,)), ((), ())), preferred_element_type=F32)


def _silu(x):
    return x * jax.nn.sigmoid(x)


def _rope(seg, c, s1, s2):
    return seg * c + pltpu.roll(seg, LANES - 8, 1) * s1 + pltpu.roll(seg, 8, 1) * s2


PROJ_COLS = 512


def _proj_tile(x_ref, g_ref, c_ref, s1_ref, s2_ref, wqkv_ref, wd_ref, wab_ref, wg_ref,
               q_ref, k_ref, v_ref, dqkv_ref, dz_ref, ab_ref, ga_ref, gb_ref, dqkv_row0=0, win_refs=None):
    tm = x_ref.shape[0]
    xb = _rms(x_ref[...], g_ref[...]).astype(BF16)
    yield
    c, s1, s2 = c_ref[...], s1_ref[...], s2_ref[...]
    z = _dot(xb, wqkv_ref[...])
    for i in range(A_WIDTH // LANES):
        sl = slice(i * LANES, (i + 1) * LANES)
        q_ref[:, sl] = _rope(z[:, sl], c, s1, s2).astype(BF16)
    k = _rope(z[:, A_WIDTH:A_WIDTH + LANES], c, s1, s2)
    k_ref[...] = k
    v_ref[...] = z[:, A_WIDTH + LANES:]
    if win_refs is not None:
        win_refs[0][0] = k[tm - WINDOW:].T
        win_refs[1][0] = z[tm - WINDOW:, A_WIDTH + LANES:].T
    yield
    pc = PROJ_COLS
    for j in range(3 * DN_WIDTH // pc):
        dqkv_ref[dqkv_row0:dqkv_row0 + tm, j * pc:(j + 1) * pc] = _dot(xb, wd_ref[:, j * pc:(j + 1) * pc])
        yield
    dz_ref[...] = _dot(xb, wd_ref[:, 3 * DN_WIDTH:])
    ab_ref[...] = _dot(xb, wab_ref[...])
    yield
    for dst, base in ((ga_ref, 0), (gb_ref, D_MODEL)):
        for j in range(D_MODEL // pc):
            dst[:, j * pc:(j + 1) * pc] = _dot(xb, wg_ref[:, base + j * pc:base + (j + 1) * pc])
            yield


def _proj_kernel(*refs):
    _drain(_proj_tile(*refs))


def _proj(x, g, tabs, wqkv, wd, wab, wg, tm):
    m = x.shape[0]
    nt = tabs[0].shape[0] // tm
    row = lambda w: pl.BlockSpec((tm, w), lambda i: (i, 0))
    tab = pl.BlockSpec((tm, LANES), lambda i: (i % nt, 0))
    widths = (A_WIDTH, LANES, LANES, 3 * DN_WIDTH, DN_WIDTH, LANES, D_MODEL, D_MODEL)
    dts = (BF16, F32, F32, F32, F32, F32, F32, F32)
    return pl.pallas_call(
        _proj_kernel,
        grid=(m // tm,),
        in_specs=[row(D_MODEL), _resident((1, D_MODEL)), tab, tab, tab,
                  _resident(wqkv.shape), _resident(wd.shape), _resident(wab.shape), _resident(wg.shape)],
        out_specs=[row(w) for w in widths],
        out_shape=[jax.ShapeDtypeStruct((m, w), d) for w, d in zip(widths, dts)],
        compiler_params=_cparams("parallel"),
        name="in_proj",
    )(x, g, *tabs, wqkv, wd, wab, wg)


def _rope_tables(pos):
    half = ROT_DIM // 2
    d = jnp.arange(LANES, dtype=jnp.int32) % A_HEAD_DIM
    inv = ROPE_THETA ** (-2.0 * (d % half).astype(F32) / ROT_DIM)
    ang = pos.astype(F32)[:, None] * inv[None, :]
    c, s = jnp.cos(ang), jnp.sin(ang)
    return (jnp.where(d < ROT_DIM, c, 1.0), jnp.where(d < half, -s, 0.0),
            jnp.where((d >= half) & (d < ROT_DIM), s, 0.0))


def _both_halves(t, lane_lo):
    r = pltpu.roll(t, A_HEAD_DIM, 1)
    return jnp.where(lane_lo, t, r), jnp.where(lane_lo, r, t)


def _sink_softmax(s, valid, sink):
    s = jnp.where(valid, s, -jnp.inf)
    m = jnp.maximum(jnp.max(s, axis=-1, keepdims=True), sink)
    p = jnp.exp(s - m)
    den = jnp.sum(p, axis=-1, keepdims=True) + jnp.exp(sink - m)
    return p * (1.0 / den)


def _swa_tile(sink_ref, q_ref, k, v, first):
    w = WINDOW
    lane_lo_k = lax.broadcasted_iota(jnp.int32, k.shape, 1) < A_HEAD_DIM
    kk = [t.astype(BF16) for t in _both_halves(k, lane_lo_k)]
    vv = [t.astype(BF16) for t in _both_halves(v, lane_lo_k)]
    lane_lo = lax.broadcasted_iota(jnp.int32, (w, LANES), 1) < A_HEAD_DIM
    row = lax.broadcasted_iota(jnp.int32, (4 * w, 2 * w), 0)
    col = lax.broadcasted_iota(jnp.int32, (4 * w, 2 * w), 1)
    d = (row & (w - 1)) + w - col
    band = (d >= 0) & (d < w)
    band_first = band & (col >= jnp.where(first, w, 0))
    hrow = lax.broadcasted_iota(jnp.int32, (4 * w, 1), 0) // w
    sinks = []
    for g in range(A_KV_HEADS):
        sink = jnp.zeros((4 * w, 1), F32)
        for j in range(4):
            sink = jnp.where(hrow == j, sink_ref[4 * g + j], sink)
        sinks.append(sink)
    zero = jnp.zeros((), BF16)
    scale = jnp.asarray(A_HEAD_DIM ** -0.5, BF16)
    blocks = []
    for b in range(q_ref.shape[0] // w):
        segs = []
        for g in range(A_KV_HEADS):
            parts = []
            for sgm in range(2):
                seg = q_ref[b * w:(b + 1) * w, (2 * g + sgm) * LANES:(2 * g + sgm + 1) * LANES] * scale
                parts += [jnp.where(lane_lo, seg, zero), jnp.where(lane_lo, zero, seg)]
            qs = jnp.concatenate(parts, axis=0)
            s = _dot_nt(qs, kk[g][b * w:(b + 2) * w])
            p = _sink_softmax(s, band_first if b == 0 else band, sinks[g]).astype(BF16)
            o = _dot(p, vv[g][b * w:(b + 2) * w])
            segs += [jnp.where(lane_lo, o[(2 * sgm) * w:(2 * sgm + 1) * w],
                               o[(2 * sgm + 1) * w:(2 * sgm + 2) * w]).astype(BF16) for sgm in range(2)]
        blocks.append(jnp.concatenate(segs, axis=1))
    return jnp.concatenate(blocks, axis=0)


def _swa_sample_kernel(sink_ref, q_ref, ckt_ref, cvt_ref, kn_ref, vn_ref, knt_ref, vnt_ref,
                       o_ref, cko_ref, cvo_ref, *, nb, t):
    w = ckt_ref.shape[3]
    old = lax.broadcasted_iota(jnp.int32, (A_HEAD_DIM, w), 1) < w - t
    for src_ref, new_ref, dst_ref in ((ckt_ref, knt_ref, cko_ref), (cvt_ref, vnt_ref, cvo_ref)):
        for g in range(A_KV_HEADS):
            new = new_ref[0, g]
            for b in range(nb):
                dst_ref[b, g] = jnp.where(old, pltpu.roll(src_ref[b, g], w - t, 1),
                                          pltpu.roll(new, (w - t - t * b) % LANES, 1))
    rows = q_ref.shape[1] // A_HEADS
    gq = rows * (A_HEADS // A_KV_HEADS)
    r = lax.broadcasted_iota(jnp.int32, (gq, w), 0)
    c = lax.broadcasted_iota(jnp.int32, (gq, w), 1)
    valid_c = c > r % rows
    rn = lax.broadcasted_iota(jnp.int32, (gq, kn_ref.shape[1]), 0)
    cn = lax.broadcasted_iota(jnp.int32, (gq, kn_ref.shape[1]), 1)
    tn = cn // A_KV_HEADS
    causal_n = (tn <= rn % rows) & (tn < t)
    hrow = lax.broadcasted_iota(jnp.int32, (gq, 1), 0) // rows
    valid_n, sink = [], []
    for g in range(A_KV_HEADS):
        valid_n.append(causal_n & (cn % A_KV_HEADS == g))
        sk = jnp.zeros((gq, 1), F32)
        for j in range(A_HEADS // A_KV_HEADS):
            sk = jnp.where(hrow == j, sink_ref[g * (A_HEADS // A_KV_HEADS) + j], sk)
        sink.append(sk)
    probs = [(b, g) for b in range(nb) for g in range(A_KV_HEADS)]
    scale = A_HEAD_DIM ** -0.5
    qs = [q_ref[b, g * gq:(g + 1) * gq, :] for b, g in probs]
    scs = [jnp.where(valid_c, _dot(q, ckt_ref[b, g].astype(BF16)) * scale, -jnp.inf)
           for q, (b, g) in zip(qs, probs)]
    sns = [jnp.where(valid_n[g], _dot_nt(q, kn_ref[b]) * scale, -jnp.inf) for q, (b, g) in zip(qs, probs)]
    ms = [jnp.maximum(jnp.maximum(jnp.max(sc, -1, keepdims=True), jnp.max(sn, -1, keepdims=True)), sink[g])
          for sc, sn, (b, g) in zip(scs, sns, probs)]
    pcs = [jnp.exp(sc - m) for sc, m in zip(scs, ms)]
    pns = [jnp.exp(sn - m) for sn, m in zip(sns, ms)]
    invs = [1.0 / (jnp.sum(pc, -1, keepdims=True) + jnp.sum(pn, -1, keepdims=True) + jnp.exp(sink[g] - m))
            for pc, pn, m, (b, g) in zip(pcs, pns, ms, probs)]
    for pc, pn, inv, (b, g) in zip(pcs, pns, invs, probs):
        o_ref[b, g * gq:(g + 1) * gq, :] = (_dot_nt((pc * inv).astype(BF16), cvt_ref[b, g].astype(BF16))
                                            + _dot((pn * inv).astype(BF16), vn_ref[b]))


def _swa_sample(q, ckt, cvt, kn, vn, knt, vnt, sinks, t):
    bsz, nq, _ = q.shape
    nb = bsz // knt.shape[0]
    assert nb * t <= LANES and ckt.shape[3] == LANES
    blk = lambda a, n=nb: pl.BlockSpec((n,) + a.shape[1:], lambda i: (i,) + (0,) * (a.ndim - 1))
    return pl.pallas_call(
        functools.partial(_swa_sample_kernel, nb=nb, t=t),
        grid=(bsz // nb,),
        in_specs=[pl.BlockSpec(memory_space=pltpu.SMEM), blk(q), blk(ckt), blk(cvt), blk(kn), blk(vn),
                  blk(knt, 1), blk(vnt, 1)],
        out_specs=[blk(q), blk(ckt), blk(cvt)],
        out_shape=[jax.ShapeDtypeStruct(q.shape, F32), jax.ShapeDtypeStruct(ckt.shape, F32),
                   jax.ShapeDtypeStruct(cvt.shape, F32)],
        compiler_params=_cparams("parallel"),
        name="swa_sample",
    )(sinks, q, ckt, cvt, kn, vn, knt, vnt)


def _lane_bcast(x, lane):
    return jnp.broadcast_to(x[:, lane:lane + 1], (x.shape[0], LANES))


def _cumsum_rows(x, block):
    c = x.shape[0]
    rowi = lax.broadcasted_iota(jnp.int32, (c, c), 0)
    coli = lax.broadcasted_iota(jnp.int32, (c, c), 1)
    ones = (((rowi // block) == (coli // block)) & (rowi >= coli)).astype(BF16)
    hi = x.astype(BF16)
    r1 = x - hi.astype(F32)
    mid = r1.astype(BF16)
    lo = (r1 - mid.astype(F32)).astype(BF16)
    s = _dot(ones, jnp.concatenate([hi, mid, lo], axis=1))
    return s[:, :LANES] + s[:, LANES:2 * LANES] + s[:, 2 * LANES:]


def _l2n(t):
    return t * lax.rsqrt(jnp.sum(t * t, axis=-1, keepdims=True) + EPS)


def _gates(ab, alog, dtb):
    x = ab + dtb
    sp = jnp.maximum(x, 0.0) + jnp.log1p(jnp.exp(-jnp.abs(x)))
    return -jnp.exp(alog) * sp, jax.nn.sigmoid(ab)


def _merge_masks(c, top):
    rowi = lax.broadcasted_iota(jnp.int32, (c, c), 0)
    coli = lax.broadcasted_iota(jnp.int32, (c, c), 1)
    masks = []
    s = 1
    while s < top:
        rb, cb = rowi // s, coli // s
        masks.append(((rb // 2) == (cb // 2)) & ((rb % 2) == 1) & ((cb % 2) == 0))
        s *= 2
    return masks


def _each(f, *lists):
    return [f(*t) for t in zip(*lists)]


def _drain(stages):
    try:
        while True:
            next(stages)
    except StopIteration as done:
        return done.value


def _interleave(*staged):
    live = list(staged)
    while live:
        for item in tuple(live):
            stages, per_turn = item
            try:
                for _ in range(per_turn):
                    next(stages)
            except StopIteration:
                live.remove(item)


def _chunk_local(qs, ks, vs, gcols, grows, betas, tril, merges):
    c = qs[0].shape[0]
    decays = _each(lambda gc, gr: jnp.exp(jnp.minimum(gc - gr, 0.0)), gcols, grows)
    kbs = _each(lambda k, b: k * b, ks, betas)
    yield
    ms = _each(lambda q, kb, k: _dot_nt(jnp.concatenate([q, kb], axis=0).astype(BF16), k.astype(BF16)),
               qs, kbs, ks)
    yield
    qks = _each(lambda m, d: jnp.where(tril, m[:c] * d, 0.0), ms, decays)
    a = _each(lambda m, d: m[c:] * d, ms, decays)
    eye = (lax.broadcasted_iota(jnp.int32, (c, c), 0) == lax.broadcasted_iota(jnp.int32, (c, c), 1)).astype(F32)
    ts = [eye - jnp.where(merges[0], x, 0.0) if merges else eye for x in a]
    yield
    for off in merges[1:]:
        tbs = [t.astype(BF16) for t in ts]
        zs = _each(lambda x, tb: _dot(jnp.where(off, x, 0.0).astype(BF16), tb), a, tbs)
        yield
        ts = _each(lambda t, tb, z: t - _dot(tb, z.astype(BF16)), ts, tbs, zs)
        yield
    ns = [t - eye for t in ts]
    egs = [jnp.exp(gc) for gc in gcols]
    rhss = _each(lambda v, b, kb, eg: jnp.concatenate([v * b, kb * eg], axis=1), vs, betas, kbs, egs)
    yield
    uws = _each(lambda r, n: r + _dot(n.astype(BF16), r.astype(BF16)), rhss, ns)
    yield
    return ([x[:, :LANES] for x in uws], [x[:, LANES:] for x in uws], qks,
            _each(lambda q, eg: q * eg, qs, egs))


def _gdn_tile(xp_ref, dz_ref, ab_ref, cw_ref, alog_ref, dtb_ref, ng_ref, y_ref, s_ref, nc):
    c = CHUNK
    pad = SUBLANES
    cw = cw_ref[...]
    g, beta = _gates(ab_ref[...], alog_ref[...], dtb_ref[...])
    rowi = lax.broadcasted_iota(jnp.int32, (c, c), 0)
    coli = lax.broadcasted_iota(jnp.int32, (c, c), 1)
    tril = rowi >= coli
    merges = _merge_masks(c, c)
    ng = ng_ref[...]
    heads = range(DN_HEADS)
    probs = [(ci, h) for ci in range(nc) for h in heads]
    rows = lambda ci: slice(ci * c, (ci + 1) * c)
    lanes = lambda part, h: slice(part * DN_WIDTH + h * LANES, part * DN_WIDTH + (h + 1) * LANES)
    yield
    qs, ks, vs, gcols, grows, betas = [], [], [], [], [], []
    for ci in range(nc):
        conv = xp_ref[pad + ci * c:pad + (ci + 1) * c, :] * cw[DN_CONV - 1:DN_CONV]
        for j in range(DN_CONV - 1):
            o = pad - (DN_CONV - 1) + j + ci * c
            conv = conv + xp_ref[o:o + c, :] * cw[j:j + 1]
        conv = _silu(conv)
        gc = _cumsum_rows(g[rows(ci)], c)
        gct = gc.T
        qs += [_l2n(conv[:, lanes(0, h)]) * (DN_HEAD_DIM ** -0.5) for h in heads]
        ks += [_l2n(conv[:, lanes(1, h)]) for h in heads]
        vs += [conv[:, lanes(2, h)] for h in heads]
        gcols += [_lane_bcast(gc, h) for h in heads]
        grows += [gct[h:h + 1, :] for h in heads]
        betas += [_lane_bcast(beta[rows(ci)], DN_HEADS + h) for h in heads]
        yield
    us, ws, qks, qds = yield from _chunk_local(qs, ks, vs, gcols, grows, betas, tril, merges)
    glasts = [gc[c - 1:c, :] for gc in gcols]
    kdts = _each(lambda k, gl, gc: (k * jnp.exp(gl - gc)).T, ks, glasts, gcols)
    wqs = _each(lambda w, qd: jnp.concatenate([w, qd], axis=0).astype(BF16), ws, qds)
    qkks = _each(lambda qk, kdt: jnp.concatenate([qk, kdt], axis=0).astype(BF16), qks, kdts)
    yield
    ss = [s_ref[h] for h in heads]
    for ci in range(nc):
        pr = [ci * DN_HEADS + h for h in heads]
        r2s = [_dot(wqs[p], s.astype(BF16)) for p, s in zip(pr, ss)]
        vnews = [us[p] - r2[:c] for p, r2 in zip(pr, r2s)]
        yield
        r3s = [_dot(qkks[p], vn.astype(BF16)) for p, vn in zip(pr, vnews)]
        ss = [s * jnp.exp(glasts[p]) + r3[c:] for p, s, r3 in zip(pr, ss, r3s)]
        for h in heads:
            o = r2s[h][c:] + r3s[h][:c]
            y_ref[rows(ci), lanes(0, h)] = (_rms(o, ng) * _silu(dz_ref[rows(ci), lanes(0, h)])).astype(BF16)
        yield
    for h in heads:
        s_ref[h] = ss[h]


def _proj_gdn_kernel(x_ref, g_ref, c_ref, s1_ref, s2_ref, wqkv_ref, wd_ref, wab_ref, wg_ref,
                     cw_ref, alog_ref, dtb_ref, ng_ref,
                     q_ref, k_ref, v_ref, ga_ref, gb_ref, kw_ref, vw_ref, y_ref, sout_ref, tail_ref,
                     xp_ref, dz_ref, ab_ref, carry_ref, s_ref, *, nt, nc):
    i = pl.program_id(0)
    pad = SUBLANES
    r = nc * CHUNK
    slot_a = i % 2
    slot_b = 1 - slot_a

    @pl.when(i == 0)
    def _():
        xp_ref[1] = jnp.zeros(xp_ref.shape[1:], F32)
        dz_ref[1] = jnp.zeros(dz_ref.shape[1:], F32)
        ab_ref[1] = jnp.zeros(ab_ref.shape[1:], F32)

    @pl.when(jnp.maximum(i - 1, 0) % nt == 0)
    def _():
        carry_ref[...] = jnp.zeros_like(carry_ref)
        s_ref[...] = jnp.zeros_like(s_ref)

    xpb_ref = xp_ref.at[slot_b]
    xpb_ref[0:pad, :] = carry_ref[...]
    _interleave(
        (_gdn_tile(xpb_ref, dz_ref.at[slot_b], ab_ref.at[slot_b], cw_ref, alog_ref, dtb_ref, ng_ref,
                   y_ref, s_ref, nc), 1),
        (_proj_tile(x_ref, g_ref, c_ref, s1_ref, s2_ref, wqkv_ref, wd_ref, wab_ref, wg_ref, q_ref, k_ref, v_ref,
                    xp_ref.at[slot_a], dz_ref.at[slot_a], ab_ref.at[slot_a], ga_ref, gb_ref, pad,
                    (kw_ref, vw_ref)), 1))
    carry_ref[...] = xpb_ref[r:r + pad, :]
    tail_ref[0] = carry_ref[...]
    sout_ref[0] = s_ref[...]


def _proj_gdn(x, g, tabs, wqkv, wd, wab, wg, cw, alog, dtb, ng, nseq, tm):
    m = x.shape[0]
    nt = m // nseq // tm
    last = m // tm - 1
    nc = tm // CHUNK
    proj = lambda i: jnp.minimum(i, last)
    gdn = lambda i: jnp.maximum(i - 1, 0)
    prow = lambda w: pl.BlockSpec((tm, w), lambda i: (proj(i), 0))
    tab = pl.BlockSpec((tm, LANES), lambda i: (proj(i) % nt, 0))
    per_seq = lambda *dims: pl.BlockSpec((1,) + dims, lambda i: (gdn(i) // nt,) + (0,) * len(dims))
    win = pl.BlockSpec((1, A_KV_WIDTH, WINDOW), lambda i: (proj(i) // nt, 0, 0))
    win_shape = jax.ShapeDtypeStruct((nseq, A_KV_WIDTH, WINDOW), F32)
    state = (DN_HEADS, DN_HEAD_DIM, DN_HEAD_DIM)
    bufs = [pltpu.VMEM((2, tm + SUBLANES, 3 * DN_WIDTH), F32), pltpu.VMEM((2, tm, DN_WIDTH), F32),
            pltpu.VMEM((2, tm, LANES), F32)]
    return pl.pallas_call(
        functools.partial(_proj_gdn_kernel, nt=nt, nc=nc),
        grid=(m // tm + 1,),
        in_specs=[prow(D_MODEL), _resident((1, D_MODEL)), tab, tab, tab,
                  _resident(wqkv.shape), _resident(wd.shape), _resident(wab.shape), _resident(wg.shape),
                  _resident(cw.shape), _resident((1, LANES)), _resident((1, LANES)), _resident((1, LANES))],
        out_specs=[prow(A_WIDTH), prow(LANES), prow(LANES), prow(D_MODEL), prow(D_MODEL), win, win,
                   pl.BlockSpec((tm, DN_WIDTH), lambda i: (gdn(i), 0)), per_seq(*state),
                   per_seq(SUBLANES, 3 * DN_WIDTH)],
        out_shape=[jax.ShapeDtypeStruct((m, A_WIDTH), BF16), jax.ShapeDtypeStruct((m, LANES), F32),
                   jax.ShapeDtypeStruct((m, LANES), F32), jax.ShapeDtypeStruct((m, D_MODEL), F32),
                   jax.ShapeDtypeStruct((m, D_MODEL), F32), win_shape, win_shape,
                   jax.ShapeDtypeStruct((m, DN_WIDTH), BF16),
                   jax.ShapeDtypeStruct((nseq,) + state, F32),
                   jax.ShapeDtypeStruct((nseq, SUBLANES, 3 * DN_WIDTH), F32)],
        scratch_shapes=bufs + [pltpu.VMEM((SUBLANES, 3 * DN_WIDTH), F32), pltpu.VMEM(state, F32)],
        compiler_params=_cparams("arbitrary"),
        name="proj_gdn",
    )(x, g, *tabs, wqkv, wd, wab, wg, cw, alog, dtb, ng)


def _gdn_sample_kernel(buf_ref, x_ref, dz_ref, ab_ref, s0_ref, cw_ref, alog_ref, dtb_ref, ng_ref,
                       y_ref, sout_ref, xp_ref, ab16_ref):
    t = x_ref.shape[1]
    c = CHUNK
    sr = GDN_SAMPLE_ROWS
    nb = c // sr
    pad = SUBLANES
    hist = DN_CONV - 1
    xp_ref[...] = jnp.zeros_like(xp_ref)
    ab16_ref[...] = jnp.zeros_like(ab16_ref)
    for b in range(nb):
        xp_ref[pad + b * sr - hist:pad + b * sr, :] = buf_ref[b]
        xp_ref[pad + b * sr:pad + b * sr + t, :] = x_ref[b]
        ab16_ref[b * sr:b * sr + t, :] = ab_ref[b]
    cw = cw_ref[...]
    conv = xp_ref[pad:pad + c, :] * cw[hist:hist + 1]
    for j in range(hist):
        conv = conv + xp_ref[pad - hist + j:pad - hist + j + c, :] * cw[j:j + 1]
    conv = _silu(conv)
    live = (lax.broadcasted_iota(jnp.int32, (c, LANES), 0) % sr) < t
    g, beta = _gates(ab16_ref[...], alog_ref[...], dtb_ref[...])
    g = jnp.where(live, g, 0.0)
    beta = jnp.where(live, beta, 0.0)
    gc = _cumsum_rows(g, sr)
    gct = gc.T
    rowi = lax.broadcasted_iota(jnp.int32, (c, c), 0)
    coli = lax.broadcasted_iota(jnp.int32, (c, c), 1)
    tril = ((rowi // sr) == (coli // sr)) & (rowi >= coli)
    merges = _merge_masks(c, pl.next_power_of_2(t))
    ng = ng_ref[...]
    rowb = lax.broadcasted_iota(jnp.int32, (c, LANES), 0) // sr
    heads = range(DN_HEADS)
    lanes = lambda part, h: slice(part * DN_WIDTH + h * LANES, part * DN_WIDTH + (h + 1) * LANES)
    ks = [_l2n(conv[:, lanes(1, h)]) for h in heads]
    gcols = [_lane_bcast(gc, h) for h in heads]
    us, ws, qks, qds = _drain(_chunk_local(
        [_l2n(conv[:, lanes(0, h)]) * (DN_HEAD_DIM ** -0.5) for h in heads], ks,
        [conv[:, lanes(2, h)] for h in heads], gcols, [gct[h:h + 1, :] for h in heads],
        [_lane_bcast(beta, DN_HEADS + h) for h in heads], tril, merges))
    seqs = range(nb)
    rows = lambda b: slice(b * sr, (b + 1) * sr)
    s0s = [[s0_ref[b, h] for b in seqs] for h in heads]
    r2s = [[_dot(jnp.concatenate([ws[h][rows(b)], qds[h][rows(b)]], axis=0).astype(BF16), s0s[h][b].astype(BF16))
            for b in seqs] for h in heads]
    vnews = [jnp.concatenate([us[h][rows(b)] - r2s[h][b][:sr] for b in seqs], axis=0) for h in heads]
    os_ = [_rms(jnp.concatenate([r2s[h][b][sr:] for b in seqs], axis=0)
                + _dot(qks[h].astype(BF16), vnews[h].astype(BF16)), ng) for h in heads]
    for h in heads:
        for b in seqs:
            y_ref[b, :, lanes(0, h)] = os_[h][b * sr:b * sr + t] * _silu(dz_ref[b, :, lanes(0, h)])
    glasts = [jnp.concatenate([jnp.broadcast_to(gcols[h][(b + 1) * sr - 1:(b + 1) * sr, :], (sr, LANES))
                               for b in seqs], axis=0) for h in heads]
    kdts = [(ks[h] * jnp.exp(glasts[h] - gcols[h])).T.astype(BF16) for h in heads]
    for h in heads:
        for b in seqs:
            vb = jnp.where(rowb == b, vnews[h], 0.0).astype(BF16)
            sout_ref[b, h] = s0s[h][b] * jnp.exp(glasts[h][b * sr:b * sr + 1, :]) + _dot(kdts[h], vb)


def _gdn_sample(buf, x, dz, ab, s0, cw, alog, dtb, ng):
    bsz, t, _ = x.shape
    assert t + DN_CONV - 1 <= GDN_SAMPLE_ROWS, "too many new tokens for one row tile per sequence"
    nb = CHUNK // GDN_SAMPLE_ROWS
    seq = lambda a: pl.BlockSpec((nb,) + a.shape[1:], lambda i: (i,) + (0,) * (a.ndim - 1))
    return pl.pallas_call(
        _gdn_sample_kernel,
        grid=(bsz // nb,),
        in_specs=[seq(buf), seq(x), seq(dz), seq(ab), seq(s0), _resident(cw.shape),
                  _resident((1, LANES)), _resident((1, LANES)), _resident((1, LANES))],
        out_specs=[seq(dz), seq(s0)],
        out_shape=[jax.ShapeDtypeStruct(dz.shape, F32), jax.ShapeDtypeStruct(s0.shape, F32)],
        scratch_shapes=[pltpu.VMEM((CHUNK + SUBLANES, 3 * DN_WIDTH), F32), pltpu.VMEM((CHUNK, LANES), F32)],
        compiler_params=_cparams("parallel"),
        name="gdn_sample",
    )(buf, x, dz, ab, s0, cw, alog, dtb, ng)


def _merge_kernel(x_ref, *rest, attend):
    if attend:
        sink_ref, q_ref, kp_ref, kc_ref, vp_ref, vc_ref, *rest = rest
        ya = _swa_tile(sink_ref, q_ref, jnp.concatenate([kp_ref[...], kc_ref[...]], axis=0),
                       jnp.concatenate([vp_ref[...], vc_ref[...]], axis=0), pl.program_id(1) == 0)
    else:
        ya_ref, *rest = rest
        ya = ya_ref[...]
    yb_ref, ga_ref, gb_ref, wa_ref, wb_ref, wo_ref, g_ref, wq_ref, h_ref, hq_ref = rest
    mix = (jax.nn.sigmoid(ga_ref[...]) * _dot(ya.astype(BF16), wa_ref[...])
           + jax.nn.sigmoid(gb_ref[...]) * _dot(yb_ref[...].astype(BF16), wb_ref[...]))
    h = x_ref[...] + _dot(mix.astype(BF16), wo_ref[...])
    h_ref[...] = h
    hq_ref[...] = _dot(_rms(h, g_ref[...]).astype(BF16), wq_ref[...]).astype(BF16)


def _merge(x, attn, yb, ga, gb, wa, wb, wo, g, wq, nseq, tm):
    m = x.shape[0]
    nt = m // nseq // tm
    row = lambda w: pl.BlockSpec((tm, w), lambda s, j: (s * nt + j, 0))
    attend = isinstance(attn, tuple)
    if attend:
        sinks, q, k, v = attn
        per = tm // WINDOW
        prev = pl.BlockSpec((WINDOW, A_KV_WIDTH), lambda s, j: (jnp.maximum((s * nt + j) * per - 1, 0), 0))
        attn_args = [sinks, q, k, k, v, v]
        attn_specs = [pl.BlockSpec(memory_space=pltpu.SMEM), row(A_WIDTH), prev, row(A_KV_WIDTH),
                      prev, row(A_KV_WIDTH)]
    else:
        attn_args, attn_specs = [attn], [row(A_WIDTH)]
    return pl.pallas_call(
        functools.partial(_merge_kernel, attend=attend),
        grid=(nseq, nt),
        in_specs=[row(D_MODEL), *attn_specs, row(DN_WIDTH), row(D_MODEL), row(D_MODEL),
                  _resident(wa.shape), _resident(wb.shape), _resident(wo.shape),
                  _resident((1, D_MODEL)), _resident(wq.shape)],
        out_specs=[row(D_MODEL), row(X_WIDTH)],
        out_shape=[jax.ShapeDtypeStruct((m, D_MODEL), F32), jax.ShapeDtypeStruct((m, X_WIDTH), BF16)],
        compiler_params=_cparams("parallel", "parallel"),
        name="merge",
    )(x, *attn_args, yb, ga, gb, wa, wb, wo, g, wq)


def _memkv_kernel(m_ref, g_ref, w_ref, k_ref, v_ref, kf_ref, vf_ref):
    tm = m_ref.shape[0]
    z = _dot(_rms(m_ref[...], g_ref[...]).astype(BF16), w_ref[...])
    k_ref[...] = z[:, :X_WIDTH]
    v_ref[...] = z[:, X_WIDTH:]
    for h in range(X_HEADS):
        kf_ref[pl.ds(h, tm, stride=X_HEADS), :] = z[:, h * LANES:(h + 1) * LANES]
        vf_ref[pl.ds(h, tm, stride=X_HEADS), :] = z[:, X_WIDTH + h * LANES:X_WIDTH + (h + 1) * LANES]


def _memkv(mem, g, w, tm):
    m = mem.shape[0]
    row = lambda wd: pl.BlockSpec((tm, wd), lambda i: (i, 0))
    flat = pl.BlockSpec((tm * X_HEADS, X_HEAD_DIM), lambda i: (i, 0))
    return pl.pallas_call(
        _memkv_kernel,
        grid=(m // tm,),
        in_specs=[row(D_MODEL), _resident((1, D_MODEL)), _resident(w.shape)],
        out_specs=[row(X_WIDTH), row(X_WIDTH), flat, flat],
        out_shape=[jax.ShapeDtypeStruct((m, X_WIDTH), F32)] * 2
        + [jax.ShapeDtypeStruct((m * X_HEADS, X_HEAD_DIM), F32)] * 2,
        compiler_params=_cparams("parallel"),
        name="mem_kv",
    )(mem, g, w)


def _softmax_rows(s):
    e = jnp.exp(s - jnp.max(s, axis=-1, keepdims=True))
    return e * (1.0 / jnp.sum(e, axis=-1, keepdims=True))


def _xattn_sample_kernel(hq_ref, mk_ref, mv_ref, o_ref, *, nb):
    rows = hq_ref.shape[1]
    nmh = mk_ref.shape[1]
    qh = lax.broadcasted_iota(jnp.int32, (X_HEADS * rows, nmh), 0) // rows
    mh = lax.broadcasted_iota(jnp.int32, (X_HEADS * rows, nmh), 1) % X_HEADS
    same = qh == mh
    seqs = range(nb)
    qs = [jnp.concatenate([hq_ref[b, :, h * LANES:(h + 1) * LANES] for h in range(X_HEADS)], axis=0)
          for b in seqs]
    ss = [_dot_nt(qs[b], mk_ref[b].astype(BF16)) * (X_HEAD_DIM ** -0.5) for b in seqs]
    ps = [_softmax_rows(jnp.where(same, s, -jnp.inf)).astype(BF16) for s in ss]
    os_ = [_dot(ps[b], mv_ref[b].astype(BF16)) for b in seqs]
    for b in seqs:
        for h in range(X_HEADS):
            o_ref[b, :, h * LANES:(h + 1) * LANES] = os_[b][h * rows:(h + 1) * rows].astype(BF16)


def _xattn_sample(hq, mk, mv, nb=8):
    bsz, rows, _ = hq.shape
    q = pl.BlockSpec((nb, rows, X_WIDTH), lambda i: (i, 0, 0))
    mem = pl.BlockSpec((nb, mk.shape[1], X_HEAD_DIM), lambda i: (i, 0, 0))
    return pl.pallas_call(
        functools.partial(_xattn_sample_kernel, nb=nb),
        grid=(bsz // nb,),
        in_specs=[q, mem, mem],
        out_specs=q,
        out_shape=jax.ShapeDtypeStruct(hq.shape, BF16),
        compiler_params=_cparams("parallel"),
        name="xattn_sample",
    )(hq, mk, mv)


def _xattn_tile(hq_ref, mk_ref, mv_ref):
    outs = []
    for h in range(X_HEADS):
        hs = slice(h * LANES, (h + 1) * LANES)
        s = _dot_nt(hq_ref[:, hs], mk_ref[:, hs].astype(BF16)) * (X_HEAD_DIM ** -0.5)
        outs.append(_dot(_softmax_rows(s).astype(BF16), mv_ref[:, hs].astype(BF16)).astype(BF16))
    return jnp.concatenate(outs, axis=1)


def _ffn_kernel(h_ref, *rest, stride, padc, fc, attend):
    if attend:
        hq_ref, mk_ref, mv_ref, *rest = rest
        xo = _xattn_tile(hq_ref, mk_ref, mv_ref)
    else:
        xo_ref, *rest = rest
        xo = xo_ref[...]
    cin_ref, wxo_ref, g_ref, wu_ref, wv_ref, cw_ref, wd_ref, fg_ref, y_ref, tail_ref, xp_ref, acc_ref = rest
    tm = h_ref.shape[0]

    @pl.when(pl.program_id(1) == 0)
    def _():
        tail_ref[...] = cin_ref[...]

    h = h_ref[...] + _dot(xo, wxo_ref[...])
    acc_ref[...] = h
    hn = _rms(h, g_ref[...]).astype(BF16)
    for lo, hi in zip((0,) + fc, fc + (D_FF,)):
        cs = slice(lo, hi)
        n = hi - lo
        u = _dot(hn, wu_ref[:, cs])
        gate = _dot(hn, wv_ref[:, cs])
        xp_ref[0:padc, 0:n] = tail_ref[0, :, cs]
        xp_ref[padc:padc + tm, 0:n] = u
        cw = cw_ref[:, cs]
        uc = u * cw[2:3]
        for j in range(FFN_CONV - 1):
            o = padc - (FFN_CONV - 1 - j) * stride
            uc = uc + xp_ref[o:o + tm, 0:n] * cw[j:j + 1]
        tail_ref[0, :, cs] = xp_ref[tm:tm + padc, 0:n]
        acc_ref[...] += _dot((_silu(uc) * gate).astype(BF16), wd_ref[cs, :])
    y_ref[...] = _rms(acc_ref[...], fg_ref[...])


FFN_CUTS = (1536,)


def _ffn(h, xq, mem, cin, wxo, g, wup, cw, wd, fg, tm, stride, fc=FFN_CUTS):
    m = h.shape[0]
    half = lambda c: pl.BlockSpec((D_MODEL, D_FF), lambda *_: (0, c), pipeline_mode=pl.Buffered(1))
    nseq, padc, _ = cin.shape
    nt = m // nseq // tm
    row = lambda w: pl.BlockSpec((tm, w), lambda s, j: (s * nt + j, 0))
    car = pl.BlockSpec((1, padc, D_FF), lambda s, j: (s, 0, 0))
    mem = list(mem or ())
    mem_specs = [pl.BlockSpec((a.shape[0] // nseq, X_WIDTH), lambda s, j: (s, 0)) for a in mem]
    return pl.pallas_call(
        functools.partial(_ffn_kernel, stride=stride, padc=padc, fc=fc, attend=bool(mem)),
        grid=(nseq, nt),
        in_specs=[row(D_MODEL), row(X_WIDTH), *mem_specs, car, _resident(wxo.shape), _resident((1, D_MODEL)),
                  half(0), half(1), _resident(cw.shape), _resident(wd.shape), _resident((1, D_MODEL))],
        out_specs=[row(D_MODEL), car],
        out_shape=[jax.ShapeDtypeStruct((m, D_MODEL), F32), jax.ShapeDtypeStruct(cin.shape, F32)],
        scratch_shapes=[pltpu.VMEM((tm + padc, max(b - a for a, b in zip((0,) + fc, fc + (D_FF,)))), F32),
                        pltpu.VMEM((tm, D_MODEL), F32)],
        compiler_params=_cparams("parallel", "arbitrary"),
        name="ffn",
    )(h, xq, *mem, cin, wxo, g, wup, wup, cw, wd, fg)


def _lane_row(vec, offset=0):
    return jnp.zeros((1, LANES), F32).at[0, offset:offset + vec.shape[0]].set(vec.astype(F32))


W_IN_QKV = A_WIDTH + 2 * A_KV_WIDTH
W_IN_D = W_IN_QKV + 4 * DN_WIDTH
W_IN_AB = W_IN_D + 2 * DN_HEADS
W_IN_G = W_IN_AB + 2 * D_MODEL


def _split_w_in_kernel(wt_ref, qkv_ref, d_ref, ab_ref, g_ref):
    tk = wt_ref.shape[1]
    qkv_ref[...] = wt_ref[:W_IN_QKV, :].T.astype(BF16)
    d_ref[...] = wt_ref[W_IN_QKV:W_IN_D, :].T.astype(BF16)
    ab = jnp.concatenate([wt_ref[W_IN_D:W_IN_AB, :], jnp.zeros((LANES - (W_IN_AB - W_IN_D), tk), F32)], axis=0)
    ab_ref[...] = ab.T.astype(BF16)
    g_ref[...] = wt_ref[W_IN_AB:W_IN_G, :].T.astype(BF16)


def _split_w_in(w_in_t, tk=256):
    k = w_in_t.shape[1]
    widths = (W_IN_QKV, W_IN_D - W_IN_QKV, LANES, W_IN_G - W_IN_AB)
    return pl.pallas_call(
        _split_w_in_kernel,
        grid=(k // tk,),
        in_specs=[pl.BlockSpec((w_in_t.shape[0], tk), lambda i: (0, i))],
        out_specs=[pl.BlockSpec((tk, wd), lambda i: (i, 0)) for wd in widths],
        out_shape=[jax.ShapeDtypeStruct((k, wd), BF16) for wd in widths],
        compiler_params=_cparams("parallel"),
        name="split_w_in",
    )(w_in_t)


def _prep(p):
    wqkv, wd, wab, wg = _split_w_in(p["w_in"].T)
    w = {
        "wqkv": wqkv,
        "wd": wd,
        "wab": wab,
        "wg": wg,
        "norm_mix_g": p["norm_mix_g"].reshape(1, D_MODEL),
        "dn_conv_w": p["dn_conv_w"],
        "alog": _lane_row(p["dn_a_log"]),
        "dtb": _lane_row(p["dn_dt_bias"]),
        "dn_norm_g": p["dn_norm_g"].reshape(1, DN_HEAD_DIM),
        "sinks": p["sinks"].astype(F32),
        "w_br_a": p["w_br_a"].astype(BF16),
        "w_br_b": p["w_br_b"].astype(BF16),
        "w_mix_out": p["w_mix_out"].astype(BF16),
        "norm_x_g": p["norm_x_g"].reshape(1, D_MODEL),
        "w_xq": p["w_xq"].astype(BF16),
        "w_xo": p["w_xo"].astype(BF16),
        "norm_ffn_g": p["norm_ffn_g"].reshape(1, D_MODEL),
        "w_up": p["w_up"].astype(BF16),
        "ffn_conv_w": p["ffn_conv_w"],
        "w_down": p["w_down"].astype(BF16),
        "final_norm_g": p["final_norm_g"].reshape(1, D_MODEL),
    }
    return w


def _tile(m, pref):
    return pref if m % pref == 0 else m


def _prompt_layer(x, mem, norm_mem_g, w_xkv, w):
    bsz, seqlen, _ = x.shape
    m = bsz * seqlen
    x2 = x.reshape(m, D_MODEL)
    tm = _tile(seqlen, 512)
    tabs = _rope_tables(jnp.arange(seqlen, dtype=jnp.int32))
    assert seqlen >= WINDOW, "the window outputs are taken from a full last window"
    q, k, v, ga, gb, kwin, vwin, yb, dn_state, dn_tail = _proj_gdn(
        x2, w["norm_mix_g"], tabs, w["wqkv"], w["wd"], w["wab"], w["wg"], w["dn_conv_w"], w["alog"], w["dtb"],
        w["dn_norm_g"], bsz, tm)
    h, hq = _merge(x2, (w["sinks"], q, k, v), yb, ga, gb, w["w_br_a"], w["w_br_b"], w["w_mix_out"],
                   w["norm_x_g"], w["w_xq"], bsz, tm)
    nm = mem.shape[1]
    mk, mv, mk_rows, mv_rows = _memkv(mem.reshape(bsz * nm, D_MODEL), norm_mem_g.reshape(1, D_MODEL),
                                      w_xkv.astype(BF16), _tile(bsz * nm, 512))
    cin = jnp.zeros((bsz, SUBLANES, D_FF), F32)
    y, tail = _ffn(h, hq, (mk, mv), cin, w["w_xo"], w["norm_ffn_g"], w["w_up"], w["ffn_conv_w"],
                   w["w_down"], w["final_norm_g"], tm, 1)
    win = lambda a: jnp.transpose(a.reshape(bsz, A_KV_HEADS, A_HEAD_DIM, WINDOW), (0, 3, 1, 2))
    new = (
        win(kwin),
        win(vwin),
        dn_tail[:, SUBLANES - (DN_CONV - 1):],
        dn_state,
        mk_rows.reshape(bsz, nm, X_HEADS, X_HEAD_DIM),
        mv_rows.reshape(bsz, nm, X_HEADS, X_HEAD_DIM),
        tail[:, SUBLANES - (FFN_CONV - 1):],
    )
    return y.reshape(bsz, seqlen, D_MODEL), new


def _sample_layer(x, pos0, win_k, win_v, dn_buf, dn_state, mem_k, mem_v, ffn_buf, w):
    bsz, t, _ = x.shape
    m = bsz * t
    sr = SAMPLE_ROWS
    x2 = x.reshape(m, D_MODEL)
    tabs = _rope_tables(jnp.tile(pos0 + jnp.arange(t, dtype=jnp.int32), bsz))
    q, k, v, dqkv, dz, ab, ga, gb = _proj(x2, w["norm_mix_g"], tabs, w["wqkv"], w["wd"], w["wab"], w["wg"], m)

    wlen = win_k.shape[1]
    qh = jnp.transpose(q.reshape(bsz, t, A_HEADS, A_HEAD_DIM), (0, 2, 1, 3)).reshape(bsz, A_HEADS * t, A_HEAD_DIM)
    padrows = lambda a: jnp.pad(a.reshape(bsz, t, -1), ((0, 0), (0, sr - t), (0, 0)))
    newrows = lambda a: jnp.pad(a.reshape(bsz, t * A_KV_HEADS, A_HEAD_DIM),
                                ((0, 0), (0, sr - t * A_KV_HEADS), (0, 0))).astype(BF16)
    kv_major = lambda a: jnp.transpose(a, (0, 2, 3, 1))
    nbw = min(bsz, LANES // t)

    def new_t(a):
        a = jnp.transpose(a.reshape(bsz // nbw, nbw * t, A_KV_HEADS, A_HEAD_DIM), (0, 2, 3, 1))
        return jnp.pad(a, ((0, 0), (0, 0), (0, 0), (0, LANES - nbw * t)))

    oh, new_wk, new_wv = _swa_sample(qh, kv_major(win_k), kv_major(win_v), newrows(k), newrows(v),
                                     new_t(k), new_t(v), w["sinks"], t)
    ya = jnp.transpose(oh.reshape(bsz, A_HEADS, t, A_HEAD_DIM), (0, 2, 1, 3)).reshape(m, A_WIDTH)
    new_wk, new_wv = (jnp.transpose(a, (0, 3, 1, 2)) for a in (new_wk, new_wv))

    dqkv3 = dqkv.reshape(bsz, t, 3 * DN_WIDTH)
    yb, new_s = _gdn_sample(dn_buf, dqkv3, dz.reshape(bsz, t, DN_WIDTH), ab.reshape(bsz, t, LANES), dn_state,
                            w["dn_conv_w"], w["alog"], w["dtb"], w["dn_norm_g"])
    yb = yb.reshape(m, DN_WIDTH)
    new_dn_buf = jnp.concatenate([dn_buf, dqkv3], axis=1)[:, -(DN_CONV - 1):]

    h, hq = _merge(x2, ya, yb, ga, gb, w["w_br_a"], w["w_br_b"], w["w_mix_out"], w["norm_x_g"], w["w_xq"], 1, m)
    nm = mem_k.shape[1]
    xo = _xattn_sample(padrows(hq), mem_k.reshape(bsz, nm * X_HEADS, X_HEAD_DIM),
                       mem_v.reshape(bsz, nm * X_HEADS, X_HEAD_DIM))
    xo = xo[:, :t]

    tmaj = lambda a: jnp.transpose(a.reshape(bsz, t, -1), (1, 0, 2)).reshape(m, -1)
    cin = jnp.transpose(ffn_buf, (1, 0, 2)).reshape(1, (FFN_CONV - 1) * bsz, D_FF)
    y, tail = _ffn(tmaj(h), tmaj(xo), None, cin, w["w_xo"], w["norm_ffn_g"], w["w_up"], w["ffn_conv_w"],
                   w["w_down"], w["final_norm_g"], m, bsz)
    y = jnp.transpose(y.reshape(t, bsz, D_MODEL), (1, 0, 2))
    new_ffn = jnp.transpose(tail.reshape(FFN_CONV - 1, bsz, D_FF), (1, 0, 2))
    return y, (new_wk, new_wv, new_dn_buf, new_s, new_ffn)


def kernel(x_prompt, x_sample, mem_prompt, cache_win_k, cache_win_v, state_dn_conv, state_dn, cache_mem_k, cache_mem_v, state_ffn_conv, norm_mix_g, w_in, dn_conv_w, dn_a_log, dn_dt_bias, dn_norm_g, attn_sinks, w_br_a, w_br_b, w_mix_out, norm_x_g, norm_mem_g, w_xq, w_xkv, w_xo, norm_ffn_g, w_up, ffn_conv_w, w_down, final_norm_g):
    p = {"norm_mix_g": norm_mix_g[0], "w_in": w_in[0], "dn_conv_w": dn_conv_w[0], "dn_a_log": dn_a_log[0],
         "dn_dt_bias": dn_dt_bias[0], "dn_norm_g": dn_norm_g[0], "sinks": attn_sinks[0], "w_br_a": w_br_a[0],
         "w_br_b": w_br_b[0], "w_mix_out": w_mix_out[0], "norm_x_g": norm_x_g[0], "w_xq": w_xq[0],
         "w_xo": w_xo[0], "norm_ffn_g": norm_ffn_g[0], "w_up": w_up[0], "ffn_conv_w": ffn_conv_w[0],
         "w_down": w_down[0], "final_norm_g": final_norm_g}
    w = _prep(p)
    yp, newp = _prompt_layer(x_prompt, mem_prompt, norm_mem_g[0], w_xkv[0], w)
    ys, news = _sample_layer(x_sample, PAST_LEN, cache_win_k[0], cache_win_v[0], state_dn_conv[0], state_dn[0],
                             cache_mem_k[0], cache_mem_v[0], state_ffn_conv[0], w)
    lead = lambda a: a[None]
    p_win_k, p_win_v, p_dn_conv, p_dn_state, p_mem_k, p_mem_v, p_ffn_conv = [lead(a) for a in newp]
    s_win_k, s_win_v, s_dn_conv, s_dn_state, s_ffn_conv = [lead(a) for a in news]
    return (yp, ys, p_win_k, p_win_v, p_dn_conv, p_dn_state, p_mem_k, p_mem_v, p_ffn_conv,
            s_win_k, s_win_v, s_dn_conv, s_dn_state, s_ffn_conv)
```

```python
import functools
import math

import jax
import jax.numpy as jnp
from jax import lax
from jax.experimental import pallas as pl
from jax.experimental.pallas import tpu as pltpu

F32 = jnp.float32
BF16 = jnp.bfloat16

D_MODEL = 1024
A_HEADS = 8
A_KV_HEADS = 2
A_HEAD_DIM = 64
A_WIDTH = 512
A_KV_WIDTH = 128
WINDOW = 128
ROT_DIM = 16
ROPE_THETA = 500000.0
DN_HEADS = 4
DN_HEAD_DIM = 128
DN_WIDTH = 512
DN_CONV = 4
X_HEADS = 4
X_HEAD_DIM = 128
X_WIDTH = 512
D_FF = 2816
FFN_CONV = 3
EPS = 1e-6
PAST_LEN = 16384

LANES = 128
SUBLANES = 8
VMEM_LIMIT = 56 * 1024 * 1024
CHUNK = 128
SAMPLE_ROWS = 16
GDN_SAMPLE_ROWS = 8


def _cparams(*sem):
    return pltpu.CompilerParams(dimension_semantics=sem, vmem_limit_bytes=VMEM_LIMIT)


def _resident(shape):
    return pl.BlockSpec(shape, lambda *_: (0,) * len(shape), pipeline_mode=pl.Buffered(1))


def _rms(x, g):
    return x * lax.rsqrt(jnp.mean(x * x, axis=-1, keepdims=True) + EPS) * g


def _dot(a, b):
    return jnp.dot(a, b, preferred_element_type=F32)


def _dot_nt(a, b):
    return lax.dot_general(a, b, (((1,), (1,)), ((), ())), preferred_element_type=F32)


def _silu(x):
    return x * jax.nn.sigmoid(x)


def _rope(seg, c, s1, s2):
    return seg * c + pltpu.roll(seg, LANES - 8, 1) * s1 + pltpu.roll(seg, 8, 1) * s2


PROJ_COLS = 512


def _proj_tile(x_ref, g_ref, c_ref, s1_ref, s2_ref, wqkv_ref, wd_ref, wab_ref, wg_ref,
               q_ref, k_ref, v_ref, dqkv_ref, dz_ref, ab_ref, ga_ref, gb_ref, dqkv_row0=0, win_refs=None):
    tm = x_ref.shape[0]
    xb = _rms(x_ref[...], g_ref[...]).astype(BF16)
    yield
    c, s1, s2 = c_ref[...], s1_ref[...], s2_ref[...]
    z = _dot(xb, wqkv_ref[...])
    for i in range(A_WIDTH // LANES):
        sl = slice(i * LANES, (i + 1) * LANES)
        q_ref[:, sl] = _rope(z[:, sl], c, s1, s2).astype(BF16)
    k = _rope(z[:, A_WIDTH:A_WIDTH + LANES], c, s1, s2)
    k_ref[...] = k
    v_ref[...] = z[:, A_WIDTH + LANES:]
    if win_refs is not None:
        win_refs[0][0] = k[tm - WINDOW:].T
        win_refs[1][0] = z[tm - WINDOW:, A_WIDTH + LANES:].T
    yield
    pc = PROJ_COLS
    for j in range(3 * DN_WIDTH // pc):
        dqkv_ref[dqkv_row0:dqkv_row0 + tm, j * pc:(j + 1) * pc] = _dot(xb, wd_ref[:, j * pc:(j + 1) * pc])
        yield
    dz_ref[...] = _dot(xb, wd_ref[:, 3 * DN_WIDTH:])
    ab_ref[...] = _dot(xb, wab_ref[...])
    yield
    for dst, base in ((ga_ref, 0), (gb_ref, D_MODEL)):
        for j in range(D_MODEL // pc):
            dst[:, j * pc:(j + 1) * pc] = _dot(xb, wg_ref[:, base + j * pc:base + (j + 1) * pc])
            yield


def _proj_kernel(*refs):
    _drain(_proj_tile(*refs))


def _proj(x, g, tabs, wqkv, wd, wab, wg, tm):
    m = x.shape[0]
    nt = tabs[0].shape[0] // tm
    row = lambda w: pl.BlockSpec((tm, w), lambda i: (i, 0))
    tab = pl.BlockSpec((tm, LANES), lambda i: (i % nt, 0))
    widths = (A_WIDTH, LANES, LANES, 3 * DN_WIDTH, DN_WIDTH, LANES, D_MODEL, D_MODEL)
    dts = (BF16, F32, F32, F32, F32, F32, F32, F32)
    return pl.pallas_call(
        _proj_kernel,
        grid=(m // tm,),
        in_specs=[row(D_MODEL), _resident((1, D_MODEL)), tab, tab, tab,
                  _resident(wqkv.shape), _resident(wd.shape), _resident(wab.shape), _resident(wg.shape)],
        out_specs=[row(w) for w in widths],
        out_shape=[jax.ShapeDtypeStruct((m, w), d) for w, d in zip(widths, dts)],
        compiler_params=_cparams("parallel"),
        name="in_proj",
    )(x, g, *tabs, wqkv, wd, wab, wg)


def _rope_tables(pos):
    half = ROT_DIM // 2
    d = jnp.arange(LANES, dtype=jnp.int32) % A_HEAD_DIM
    inv = ROPE_THETA ** (-2.0 * (d % half).astype(F32) / ROT_DIM)
    ang = pos.astype(F32)[:, None] * inv[None, :]
    c, s = jnp.cos(ang), jnp.sin(ang)
    return (jnp.where(d < ROT_DIM, c, 1.0), jnp.where(d < half, -s, 0.0),
            jnp.where((d >= half) & (d < ROT_DIM), s, 0.0))


def _both_halves(t, lane_lo):
    r = pltpu.roll(t, A_HEAD_DIM, 1)
    return jnp.where(lane_lo, t, r), jnp.where(lane_lo, r, t)


def _sink_softmax(s, valid, sink):
    s = jnp.where(valid, s, -jnp.inf)
    m = jnp.maximum(jnp.max(s, axis=-1, keepdims=True), sink)
    p = jnp.exp(s - m)
    den = jnp.sum(p, axis=-1, keepdims=True) + jnp.exp(sink - m)
    return p * (1.0 / den)


def _swa_tile(sink_ref, q_ref, k, v, first):
    w = WINDOW
    lane_lo_k = lax.broadcasted_iota(jnp.int32, k.shape, 1) < A_HEAD_DIM
    kk = [t.astype(BF16) for t in _both_halves(k, lane_lo_k)]
    vv = [t.astype(BF16) for t in _both_halves(v, lane_lo_k)]
    lane_lo = lax.broadcasted_iota(jnp.int32, (w, LANES), 1) < A_HEAD_DIM
    row = lax.broadcasted_iota(jnp.int32, (4 * w, 2 * w), 0)
    col = lax.broadcasted_iota(jnp.int32, (4 * w, 2 * w), 1)
    d = (row & (w - 1)) + w - col
    band = (d >= 0) & (d < w)
    band_first = band & (col >= jnp.where(first, w, 0))
    hrow = lax.broadcasted_iota(jnp.int32, (4 * w, 1), 0) // w
    sinks = []
    for g in range(A_KV_HEADS):
        sink = jnp.zeros((4 * w, 1), F32)
        for j in range(4):
            sink = jnp.where(hrow == j, sink_ref[4 * g + j], sink)
        sinks.append(sink)
    zero = jnp.zeros((), BF16)
    scale = jnp.asarray(A_HEAD_DIM ** -0.5, BF16)
    nblk = q_ref.shape[0] // w
    probs = [(b, g) for b in range(nblk) for g in range(A_KV_HEADS)]

    def queries(b, g):
        parts = []
        for sgm in range(2):
            seg = q_ref[b * w:(b + 1) * w, (2 * g + sgm) * LANES:(2 * g + sgm + 1) * LANES] * scale
            parts += [jnp.where(lane_lo, seg, zero), jnp.where(lane_lo, zero, seg)]
        return jnp.concatenate(parts, axis=0)

    ss = [_dot_nt(queries(b, g), kk[g][b * w:(b + 2) * w]) for b, g in probs]
    yield
    ps = [_sink_softmax(s, band_first if b == 0 else band, sinks[g]).astype(BF16) for s, (b, g) in zip(ss, probs)]
    yield
    os_ = [_dot(p, vv[g][b * w:(b + 2) * w]) for p, (b, g) in zip(ps, probs)]
    yield
    segs = [[jnp.where(lane_lo, o[(2 * sgm) * w:(2 * sgm + 1) * w], o[(2 * sgm + 1) * w:(2 * sgm + 2) * w]
                       ).astype(BF16) for sgm in range(2)] for o in os_]
    return jnp.concatenate([jnp.concatenate(segs[A_KV_HEADS * b] + segs[A_KV_HEADS * b + 1], axis=1)
                            for b in range(nblk)], axis=0)


def _swa_sample_kernel(sink_ref, q_ref, ckt_ref, cvt_ref, kn_ref, vn_ref, knt_ref, vnt_ref,
                       o_ref, cko_ref, cvo_ref, *, nb, t):
    w = ckt_ref.shape[3]
    old = lax.broadcasted_iota(jnp.int32, (A_HEAD_DIM, w), 1) < w - t
    for src_ref, new_ref, dst_ref in ((ckt_ref, knt_ref, cko_ref), (cvt_ref, vnt_ref, cvo_ref)):
        for g in range(A_KV_HEADS):
            new = new_ref[0, g]
            for b in range(nb):
                dst_ref[b, g] = jnp.where(old, pltpu.roll(src_ref[b, g], w - t, 1),
                                          pltpu.roll(new, (w - t - t * b) % LANES, 1))
    rows = q_ref.shape[1] // A_HEADS
    gq = rows * (A_HEADS // A_KV_HEADS)
    r = lax.broadcasted_iota(jnp.int32, (gq, w), 0)
    c = lax.broadcasted_iota(jnp.int32, (gq, w), 1)
    valid_c = c > r % rows
    rn = lax.broadcasted_iota(jnp.int32, (gq, kn_ref.shape[1]), 0)
    cn = lax.broadcasted_iota(jnp.int32, (gq, kn_ref.shape[1]), 1)
    tn = cn // A_KV_HEADS
    causal_n = (tn <= rn % rows) & (tn < t)
    hrow = lax.broadcasted_iota(jnp.int32, (gq, 1), 0) // rows
    valid_n, sink = [], []
    for g in range(A_KV_HEADS):
        valid_n.append(causal_n & (cn % A_KV_HEADS == g))
        sk = jnp.zeros((gq, 1), F32)
        for j in range(A_HEADS // A_KV_HEADS):
            sk = jnp.where(hrow == j, sink_ref[g * (A_HEADS // A_KV_HEADS) + j], sk)
        sink.append(sk)
    probs = [(b, g) for b in range(nb) for g in range(A_KV_HEADS)]
    scale = A_HEAD_DIM ** -0.5
    qs = [q_ref[b, g * gq:(g + 1) * gq, :] for b, g in probs]
    scs = [jnp.where(valid_c, _dot(q, ckt_ref[b, g].astype(BF16)) * scale, -jnp.inf)
           for q, (b, g) in zip(qs, probs)]
    sns = [jnp.where(valid_n[g], _dot_nt(q, kn_ref[b]) * scale, -jnp.inf) for q, (b, g) in zip(qs, probs)]
    ms = [jnp.maximum(jnp.maximum(jnp.max(sc, -1, keepdims=True), jnp.max(sn, -1, keepdims=True)), sink[g])
          for sc, sn, (b, g) in zip(scs, sns, probs)]
    pcs = [jnp.exp(sc - m) for sc, m in zip(scs, ms)]
    pns = [jnp.exp(sn - m) for sn, m in zip(sns, ms)]
    invs = [1.0 / (jnp.sum(pc, -1, keepdims=True) + jnp.sum(pn, -1, keepdims=True) + jnp.exp(sink[g] - m))
            for pc, pn, m, (b, g) in zip(pcs, pns, ms, probs)]
    for pc, pn, inv, (b, g) in zip(pcs, pns, invs, probs):
        o_ref[b, g * gq:(g + 1) * gq, :] = (_dot_nt((pc * inv).astype(BF16), cvt_ref[b, g].astype(BF16))
                                            + _dot((pn * inv).astype(BF16), vn_ref[b]))


def _swa_sample(q, ckt, cvt, kn, vn, knt, vnt, sinks, t):
    bsz, nq, _ = q.shape
    nb = bsz // knt.shape[0]
    assert nb * t <= LANES and ckt.shape[3] == LANES
    blk = lambda a, n=nb: pl.BlockSpec((n,) + a.shape[1:], lambda i: (i,) + (0,) * (a.ndim - 1))
    return pl.pallas_call(
        functools.partial(_swa_sample_kernel, nb=nb, t=t),
        grid=(bsz // nb,),
        in_specs=[pl.BlockSpec(memory_space=pltpu.SMEM), blk(q), blk(ckt), blk(cvt), blk(kn), blk(vn),
                  blk(knt, 1), blk(vnt, 1)],
        out_specs=[blk(q), blk(ckt), blk(cvt)],
        out_shape=[jax.ShapeDtypeStruct(q.shape, F32), jax.ShapeDtypeStruct(ckt.shape, F32),
                   jax.ShapeDtypeStruct(cvt.shape, F32)],
        compiler_params=_cparams("parallel"),
        name="swa_sample",
    )(sinks, q, ckt, cvt, kn, vn, knt, vnt)


def _lane_bcast(x, lane):
    return jnp.broadcast_to(x[:, lane:lane + 1], (x.shape[0], LANES))


def _cumsum_rows(x, block):
    c = x.shape[0]
    rowi = lax.broadcasted_iota(jnp.int32, (c, c), 0)
    coli = lax.broadcasted_iota(jnp.int32, (c, c), 1)
    ones = (((rowi // block) == (coli // block)) & (rowi >= coli)).astype(BF16)
    hi = x.astype(BF16)
    r1 = x - hi.astype(F32)
    mid = r1.astype(BF16)
    lo = (r1 - mid.astype(F32)).astype(BF16)
    s = _dot(ones, jnp.concatenate([hi, mid, lo], axis=1))
    return s[:, :LANES] + s[:, LANES:2 * LANES] + s[:, 2 * LANES:]


def _l2n(t):
    return t * lax.rsqrt(jnp.sum(t * t, axis=-1, keepdims=True) + EPS)


def _gates(ab, alog, dtb):
    x = ab + dtb
    sp = jnp.maximum(x, 0.0) + jnp.log1p(jnp.exp(-jnp.abs(x)))
    return -jnp.exp(alog) * sp, jax.nn.sigmoid(ab)


def _merge_masks(c, top):
    rowi = lax.broadcasted_iota(jnp.int32, (c, c), 0)
    coli = lax.broadcasted_iota(jnp.int32, (c, c), 1)
    masks = []
    s = 1
    while s < top:
        rb, cb = rowi // s, coli // s
        masks.append(((rb // 2) == (cb // 2)) & ((rb % 2) == 1) & ((cb % 2) == 0))
        s *= 2
    return masks


def _each(f, *lists):
    return [f(*t) for t in zip(*lists)]


def _drain(stages):
    try:
        while True:
            next(stages)
    except StopIteration as done:
        return done.value


def _interleave(*staged):
    live = list(staged)
    values = {}
    while live:
        for item in tuple(live):
            stages, per_turn = item
            try:
                for _ in range(per_turn):
                    next(stages)
            except StopIteration as done:
                values[id(stages)] = done.value
                live.remove(item)
    return [values[id(stages)] for stages, _ in staged]


def _chunk_local(qs, ks, vs, gcols, grows, betas, tril, merges):
    c = qs[0].shape[0]
    decays = _each(lambda gc, gr: jnp.exp(jnp.minimum(gc - gr, 0.0)), gcols, grows)
    kbs = _each(lambda k, b: k * b, ks, betas)
    yield
    ms = _each(lambda q, kb, k: _dot_nt(jnp.concatenate([q, kb], axis=0).astype(BF16), k.astype(BF16)),
               qs, kbs, ks)
    yield
    qks = _each(lambda m, d: jnp.where(tril, m[:c] * d, 0.0), ms, decays)
    a = _each(lambda m, d: m[c:] * d, ms, decays)
    eye = (lax.broadcasted_iota(jnp.int32, (c, c), 0) == lax.broadcasted_iota(jnp.int32, (c, c), 1)).astype(F32)
    ts = [eye - jnp.where(merges[0], x, 0.0) if merges else eye for x in a]
    yield
    for off in merges[1:]:
        tbs = [t.astype(BF16) for t in ts]
        zs = _each(lambda x, tb: _dot(jnp.where(off, x, 0.0).astype(BF16), tb), a, tbs)
        yield
        ts = _each(lambda t, tb, z: t - _dot(tb, z.astype(BF16)), ts, tbs, zs)
        yield
    ns = [t - eye for t in ts]
    egs = [jnp.exp(gc) for gc in gcols]
    rhss = _each(lambda v, b, kb, eg: jnp.concatenate([v * b, kb * eg], axis=1), vs, betas, kbs, egs)
    yield
    uws = _each(lambda r, n: r + _dot(n.astype(BF16), r.astype(BF16)), rhss, ns)
    yield
    return ([x[:, :LANES] for x in uws], [x[:, LANES:] for x in uws], qks,
            _each(lambda q, eg: q * eg, qs, egs))


def _gdn_tile(xp_ref, dz_ref, ab_ref, cw_ref, alog_ref, dtb_ref, ng_ref, y_ref, s_ref, nc):
    c = CHUNK
    pad = SUBLANES
    cw = cw_ref[...]
    g, beta = _gates(ab_ref[...], alog_ref[...], dtb_ref[...])
    rowi = lax.broadcasted_iota(jnp.int32, (c, c), 0)
    coli = lax.broadcasted_iota(jnp.int32, (c, c), 1)
    tril = rowi >= coli
    merges = _merge_masks(c, c)
    ng = ng_ref[...]
    heads = range(DN_HEADS)
    probs = [(ci, h) for ci in range(nc) for h in heads]
    rows = lambda ci: slice(ci * c, (ci + 1) * c)
    lanes = lambda part, h: slice(part * DN_WIDTH + h * LANES, part * DN_WIDTH + (h + 1) * LANES)
    yield
    qs, ks, vs, gcols, grows, betas = [], [], [], [], [], []
    for ci in range(nc):
        conv = xp_ref[pad + ci * c:pad + (ci + 1) * c, :] * cw[DN_CONV - 1:DN_CONV]
        for j in range(DN_CONV - 1):
            o = pad - (DN_CONV - 1) + j + ci * c
            conv = conv + xp_ref[o:o + c, :] * cw[j:j + 1]
        conv = _silu(conv)
        gc = _cumsum_rows(g[rows(ci)], c)
        gct = gc.T
        qs += [_l2n(conv[:, lanes(0, h)]) * (DN_HEAD_DIM ** -0.5) for h in heads]
        ks += [_l2n(conv[:, lanes(1, h)]) for h in heads]
        vs += [conv[:, lanes(2, h)] for h in heads]
        gcols += [_lane_bcast(gc, h) for h in heads]
        grows += [gct[h:h + 1, :] for h in heads]
        betas += [_lane_bcast(beta[rows(ci)], DN_HEADS + h) for h in heads]
        yield
    us, ws, qks, qds = yield from _chunk_local(qs, ks, vs, gcols, grows, betas, tril, merges)
    glasts = [gc[c - 1:c, :] for gc in gcols]
    kdts = _each(lambda k, gl, gc: (k * jnp.exp(gl - gc)).T, ks, glasts, gcols)
    wqs = _each(lambda w, qd: jnp.concatenate([w, qd], axis=0).astype(BF16), ws, qds)
    qkks = _each(lambda qk, kdt: jnp.concatenate([qk, kdt], axis=0).astype(BF16), qks, kdts)
    yield
    ss = [s_ref[h] for h in heads]
    for ci in range(nc):
        pr = [ci * DN_HEADS + h for h in heads]
        r2s = [_dot(wqs[p], s.astype(BF16)) for p, s in zip(pr, ss)]
        vnews = [us[p] - r2[:c] for p, r2 in zip(pr, r2s)]
        yield
        r3s = [_dot(qkks[p], vn.astype(BF16)) for p, vn in zip(pr, vnews)]
        ss = [s * jnp.exp(glasts[p]) + r3[c:] for p, s, r3 in zip(pr, ss, r3s)]
        for h in heads:
            o = r2s[h][c:] + r3s[h][:c]
            y_ref[rows(ci), lanes(0, h)] = (_rms(o, ng) * _silu(dz_ref[rows(ci), lanes(0, h)])).astype(BF16)
        yield
    for h in heads:
        s_ref[h] = ss[h]


def _proj_gdn_kernel(x_ref, g_ref, c_ref, s1_ref, s2_ref, wqkv_ref, wd_ref, wab_ref, wg_ref,
                     cw_ref, alog_ref, dtb_ref, ng_ref,
                     q_ref, k_ref, v_ref, ga_ref, gb_ref, kw_ref, vw_ref, y_ref, sout_ref, tail_ref,
                     xp_ref, dz_ref, ab_ref, carry_ref, s_ref, *, nt, nc):
    i = pl.program_id(0)
    pad = SUBLANES
    r = nc * CHUNK
    slot_a = i % 2
    slot_b = 1 - slot_a

    @pl.when(i == 0)
    def _():
        xp_ref[1] = jnp.zeros(xp_ref.shape[1:], F32)
        dz_ref[1] = jnp.zeros(dz_ref.shape[1:], F32)
        ab_ref[1] = jnp.zeros(ab_ref.shape[1:], F32)

    @pl.when(jnp.maximum(i - 1, 0) % nt == 0)
    def _():
        carry_ref[...] = jnp.zeros_like(carry_ref)
        s_ref[...] = jnp.zeros_like(s_ref)

    xpb_ref = xp_ref.at[slot_b]
    xpb_ref[0:pad, :] = carry_ref[...]
    _interleave(
        (_gdn_tile(xpb_ref, dz_ref.at[slot_b], ab_ref.at[slot_b], cw_ref, alog_ref, dtb_ref, ng_ref,
                   y_ref, s_ref, nc), 1),
        (_proj_tile(x_ref, g_ref, c_ref, s1_ref, s2_ref, wqkv_ref, wd_ref, wab_ref, wg_ref, q_ref, k_ref, v_ref,
                    xp_ref.at[slot_a], dz_ref.at[slot_a], ab_ref.at[slot_a], ga_ref, gb_ref, pad,
                    (kw_ref, vw_ref)), 1))
    carry_ref[...] = xpb_ref[r:r + pad, :]
    tail_ref[0] = carry_ref[...]
    sout_ref[0] = s_ref[...]


def _proj_gdn(x, g, tabs, wqkv, wd, wab, wg, cw, alog, dtb, ng, nseq, tm):
    m = x.shape[0]
    nt = m // nseq // tm
    last = m // tm - 1
    nc = tm // CHUNK
    proj = lambda i: jnp.minimum(i, last)
    gdn = lambda i: jnp.maximum(i - 1, 0)
    prow = lambda w: pl.BlockSpec((tm, w), lambda i: (proj(i), 0))
    tab = pl.BlockSpec((tm, LANES), lambda i: (proj(i) % nt, 0))
    per_seq = lambda *dims: pl.BlockSpec((1,) + dims, lambda i: (gdn(i) // nt,) + (0,) * len(dims))
    win = pl.BlockSpec((1, A_KV_WIDTH, WINDOW), lambda i: (proj(i) // nt, 0, 0))
    win_shape = jax.ShapeDtypeStruct((nseq, A_KV_WIDTH, WINDOW), F32)
    state = (DN_HEADS, DN_HEAD_DIM, DN_HEAD_DIM)
    bufs = [pltpu.VMEM((2, tm + SUBLANES, 3 * DN_WIDTH), F32), pltpu.VMEM((2, tm, DN_WIDTH), F32),
            pltpu.VMEM((2, tm, LANES), F32)]
    return pl.pallas_call(
        functools.partial(_proj_gdn_kernel, nt=nt, nc=nc),
        grid=(m // tm + 1,),
        in_specs=[prow(D_MODEL), _resident((1, D_MODEL)), tab, tab, tab,
                  _resident(wqkv.shape), _resident(wd.shape), _resident(wab.shape), _resident(wg.shape),
                  _resident(cw.shape), _resident((1, LANES)), _resident((1, LANES)), _resident((1, LANES))],
        out_specs=[prow(A_WIDTH), prow(LANES), prow(LANES), prow(D_MODEL), prow(D_MODEL), win, win,
                   pl.BlockSpec((tm, DN_WIDTH), lambda i: (gdn(i), 0)), per_seq(*state),
                   per_seq(SUBLANES, 3 * DN_WIDTH)],
        out_shape=[jax.ShapeDtypeStruct((m, A_WIDTH), BF16), jax.ShapeDtypeStruct((m, LANES), F32),
                   jax.ShapeDtypeStruct((m, LANES), F32), jax.ShapeDtypeStruct((m, D_MODEL), F32),
                   jax.ShapeDtypeStruct((m, D_MODEL), F32), win_shape, win_shape,
                   jax.ShapeDtypeStruct((m, DN_WIDTH), BF16),
                   jax.ShapeDtypeStruct((nseq,) + state, F32),
                   jax.ShapeDtypeStruct((nseq, SUBLANES, 3 * DN_WIDTH), F32)],
        scratch_shapes=bufs + [pltpu.VMEM((SUBLANES, 3 * DN_WIDTH), F32), pltpu.VMEM(state, F32)],
        compiler_params=_cparams("arbitrary"),
        name="proj_gdn",
    )(x, g, *tabs, wqkv, wd, wab, wg, cw, alog, dtb, ng)


def _gdn_sample_kernel(buf_ref, x_ref, dz_ref, ab_ref, s0_ref, cw_ref, alog_ref, dtb_ref, ng_ref,
                       y_ref, sout_ref, xp_ref, ab16_ref):
    t = x_ref.shape[1]
    c = CHUNK
    sr = GDN_SAMPLE_ROWS
    nb = c // sr
    pad = SUBLANES
    hist = DN_CONV - 1
    xp_ref[...] = jnp.zeros_like(xp_ref)
    ab16_ref[...] = jnp.zeros_like(ab16_ref)
    for b in range(nb):
        xp_ref[pad + b * sr - hist:pad + b * sr, :] = buf_ref[b]
        xp_ref[pad + b * sr:pad + b * sr + t, :] = x_ref[b]
        ab16_ref[b * sr:b * sr + t, :] = ab_ref[b]
    cw = cw_ref[...]
    conv = xp_ref[pad:pad + c, :] * cw[hist:hist + 1]
    for j in range(hist):
        conv = conv + xp_ref[pad - hist + j:pad - hist + j + c, :] * cw[j:j + 1]
    conv = _silu(conv)
    live = (lax.broadcasted_iota(jnp.int32, (c, LANES), 0) % sr) < t
    g, beta = _gates(ab16_ref[...], alog_ref[...], dtb_ref[...])
    g = jnp.where(live, g, 0.0)
    beta = jnp.where(live, beta, 0.0)
    gc = _cumsum_rows(g, sr)
    gct = gc.T
    rowi = lax.broadcasted_iota(jnp.int32, (c, c), 0)
    coli = lax.broadcasted_iota(jnp.int32, (c, c), 1)
    tril = ((rowi // sr) == (coli // sr)) & (rowi >= coli)
    merges = _merge_masks(c, pl.next_power_of_2(t))
    ng = ng_ref[...]
    rowb = lax.broadcasted_iota(jnp.int32, (c, LANES), 0) // sr
    heads = range(DN_HEADS)
    lanes = lambda part, h: slice(part * DN_WIDTH + h * LANES, part * DN_WIDTH + (h + 1) * LANES)
    ks = [_l2n(conv[:, lanes(1, h)]) for h in heads]
    gcols = [_lane_bcast(gc, h) for h in heads]
    us, ws, qks, qds = _drain(_chunk_local(
        [_l2n(conv[:, lanes(0, h)]) * (DN_HEAD_DIM ** -0.5) for h in heads], ks,
        [conv[:, lanes(2, h)] for h in heads], gcols, [gct[h:h + 1, :] for h in heads],
        [_lane_bcast(beta, DN_HEADS + h) for h in heads], tril, merges))
    seqs = range(nb)
    rows = lambda b: slice(b * sr, (b + 1) * sr)
    s0s = [[s0_ref[b, h] for b in seqs] for h in heads]
    r2s = [[_dot(jnp.concatenate([ws[h][rows(b)], qds[h][rows(b)]], axis=0).astype(BF16), s0s[h][b].astype(BF16))
            for b in seqs] for h in heads]
    vnews = [jnp.concatenate([us[h][rows(b)] - r2s[h][b][:sr] for b in seqs], axis=0) for h in heads]
    os_ = [_rms(jnp.concatenate([r2s[h][b][sr:] for b in seqs], axis=0)
                + _dot(qks[h].astype(BF16), vnews[h].astype(BF16)), ng) for h in heads]
    for h in heads:
        for b in seqs:
            y_ref[b, :, lanes(0, h)] = os_[h][b * sr:b * sr + t] * _silu(dz_ref[b, :, lanes(0, h)])
    glasts = [jnp.concatenate([jnp.broadcast_to(gcols[h][(b + 1) * sr - 1:(b + 1) * sr, :], (sr, LANES))
                               for b in seqs], axis=0) for h in heads]
    kdts = [(ks[h] * jnp.exp(glasts[h] - gcols[h])).T.astype(BF16) for h in heads]
    for h in heads:
        for b in seqs:
            vb = jnp.where(rowb == b, vnews[h], 0.0).astype(BF16)
            sout_ref[b, h] = s0s[h][b] * jnp.exp(glasts[h][b * sr:b * sr + 1, :]) + _dot(kdts[h], vb)


def _gdn_sample(buf, x, dz, ab, s0, cw, alog, dtb, ng):
    bsz, t, _ = x.shape
    assert t + DN_CONV - 1 <= GDN_SAMPLE_ROWS, "too many new tokens for one row tile per sequence"
    nb = CHUNK // GDN_SAMPLE_ROWS
    seq = lambda a: pl.BlockSpec((nb,) + a.shape[1:], lambda i: (i,) + (0,) * (a.ndim - 1))
    return pl.pallas_call(
        _gdn_sample_kernel,
        grid=(bsz // nb,),
        in_specs=[seq(buf), seq(x), seq(dz), seq(ab), seq(s0), _resident(cw.shape),
                  _resident((1, LANES)), _resident((1, LANES)), _resident((1, LANES))],
        out_specs=[seq(dz), seq(s0)],
        out_shape=[jax.ShapeDtypeStruct(dz.shape, F32), jax.ShapeDtypeStruct(s0.shape, F32)],
        scratch_shapes=[pltpu.VMEM((CHUNK + SUBLANES, 3 * DN_WIDTH), F32), pltpu.VMEM((CHUNK, LANES), F32)],
        compiler_params=_cparams("parallel"),
        name="gdn_sample",
    )(buf, x, dz, ab, s0, cw, alog, dtb, ng)


def _merge_kernel(x_ref, *rest, attend):
    if attend:
        sink_ref, q_ref, kp_ref, kc_ref, vp_ref, vc_ref, *rest = rest
    else:
        ya_ref, *rest = rest
    yb_ref, ga_ref, gb_ref, wa_ref, wb_ref, wo_ref, g_ref, wq_ref, h_ref, hq_ref = rest

    def deltanet_half():
        yb = _dot(yb_ref[...].astype(BF16), wb_ref[...])
        yield
        gate_a = jax.nn.sigmoid(ga_ref[...])
        yield
        return gate_a, jax.nn.sigmoid(gb_ref[...]) * yb

    if attend:
        ya, (gate_a, mix_b) = _interleave(
            (_swa_tile(sink_ref, q_ref, jnp.concatenate([kp_ref[...], kc_ref[...]], axis=0),
                       jnp.concatenate([vp_ref[...], vc_ref[...]], axis=0), pl.program_id(1) == 0), 1),
            (deltanet_half(), 1))
    else:
        ya = ya_ref[...]
        gate_a, mix_b = _drain(deltanet_half())
    mix = gate_a * _dot(ya.astype(BF16), wa_ref[...]) + mix_b
    h = x_ref[...] + _dot(mix.astype(BF16), wo_ref[...])
    h_ref[...] = h
    hq_ref[...] = _dot(_rms(h, g_ref[...]).astype(BF16), wq_ref[...]).astype(BF16)


def _merge(x, attn, yb, ga, gb, wa, wb, wo, g, wq, nseq, tm):
    m = x.shape[0]
    nt = m // nseq // tm
    row = lambda w: pl.BlockSpec((tm, w), lambda s, j: (s * nt + j, 0))
    attend = isinstance(attn, tuple)
    if attend:
        sinks, q, k, v = attn
        per = tm // WINDOW
        prev = pl.BlockSpec((WINDOW, A_KV_WIDTH), lambda s, j: (jnp.maximum((s * nt + j) * per - 1, 0), 0))
        attn_args = [sinks, q, k, k, v, v]
        attn_specs = [pl.BlockSpec(memory_space=pltpu.SMEM), row(A_WIDTH), prev, row(A_KV_WIDTH),
                      prev, row(A_KV_WIDTH)]
    else:
        attn_args, attn_specs = [attn], [row(A_WIDTH)]
    return pl.pallas_call(
        functools.partial(_merge_kernel, attend=attend),
        grid=(nseq, nt),
        in_specs=[row(D_MODEL), *attn_specs, row(DN_WIDTH), row(D_MODEL), row(D_MODEL),
                  _resident(wa.shape), _resident(wb.shape), _resident(wo.shape),
                  _resident((1, D_MODEL)), _resident(wq.shape)],
        out_specs=[row(D_MODEL), row(X_WIDTH)],
        out_shape=[jax.ShapeDtypeStruct((m, D_MODEL), F32), jax.ShapeDtypeStruct((m, X_WIDTH), BF16)],
        compiler_params=_cparams("parallel", "parallel"),
        name="merge",
    )(x, *attn_args, yb, ga, gb, wa, wb, wo, g, wq)


def _memkv_kernel(m_ref, g_ref, w_ref, k_ref, v_ref, kf_ref, vf_ref):
    tm = m_ref.shape[0]
    z = _dot(_rms(m_ref[...], g_ref[...]).astype(BF16), w_ref[...])
    k_ref[...] = z[:, :X_WIDTH]
    v_ref[...] = z[:, X_WIDTH:]
    for h in range(X_HEADS):
        kf_ref[pl.ds(h, tm, stride=X_HEADS), :] = z[:, h * LANES:(h + 1) * LANES]
        vf_ref[pl.ds(h, tm, stride=X_HEADS), :] = z[:, X_WIDTH + h * LANES:X_WIDTH + (h + 1) * LANES]


def _memkv(mem, g, w, tm):
    m = mem.shape[0]
    row = lambda wd: pl.BlockSpec((tm, wd), lambda i: (i, 0))
    flat = pl.BlockSpec((tm * X_HEADS, X_HEAD_DIM), lambda i: (i, 0))
    return pl.pallas_call(
        _memkv_kernel,
        grid=(m // tm,),
        in_specs=[row(D_MODEL), _resident((1, D_MODEL)), _resident(w.shape)],
        out_specs=[row(X_WIDTH), row(X_WIDTH), flat, flat],
        out_shape=[jax.ShapeDtypeStruct((m, X_WIDTH), F32)] * 2
        + [jax.ShapeDtypeStruct((m * X_HEADS, X_HEAD_DIM), F32)] * 2,
        compiler_params=_cparams("parallel"),
        name="mem_kv",
    )(mem, g, w)


def _softmax_rows(s):
    e = jnp.exp(s - jnp.max(s, axis=-1, keepdims=True))
    return e * (1.0 / jnp.sum(e, axis=-1, keepdims=True))


def _xattn_sample_kernel(hq_ref, mk_ref, mv_ref, o_ref, *, nb):
    rows = hq_ref.shape[1]
    nmh = mk_ref.shape[1]
    qh = lax.broadcasted_iota(jnp.int32, (X_HEADS * rows, nmh), 0) // rows
    mh = lax.broadcasted_iota(jnp.int32, (X_HEADS * rows, nmh), 1) % X_HEADS
    same = qh == mh
    seqs = range(nb)
    qs = [jnp.concatenate([hq_ref[b, :, h * LANES:(h + 1) * LANES] for h in range(X_HEADS)], axis=0)
          for b in seqs]
    ss = [_dot_nt(qs[b], mk_ref[b].astype(BF16)) * (X_HEAD_DIM ** -0.5) for b in seqs]
    ps = [_softmax_rows(jnp.where(same, s, -jnp.inf)).astype(BF16) for s in ss]
    os_ = [_dot(ps[b], mv_ref[b].astype(BF16)) for b in seqs]
    for b in seqs:
        for h in range(X_HEADS):
            o_ref[b, :, h * LANES:(h + 1) * LANES] = os_[b][h * rows:(h + 1) * rows].astype(BF16)


def _xattn_sample(hq, mk, mv, nb=8):
    bsz, rows, _ = hq.shape
    q = pl.BlockSpec((nb, rows, X_WIDTH), lambda i: (i, 0, 0))
    mem = pl.BlockSpec((nb, mk.shape[1], X_HEAD_DIM), lambda i: (i, 0, 0))
    return pl.pallas_call(
        functools.partial(_xattn_sample_kernel, nb=nb),
        grid=(bsz // nb,),
        in_specs=[q, mem, mem],
        out_specs=q,
        out_shape=jax.ShapeDtypeStruct(hq.shape, BF16),
        compiler_params=_cparams("parallel"),
        name="xattn_sample",
    )(hq, mk, mv)


def _xattn_tile(hq_ref, mk_ref, mv_ref):
    outs = []
    for h in range(X_HEADS):
        hs = slice(h * LANES, (h + 1) * LANES)
        s = _dot_nt(hq_ref[:, hs], mk_ref[:, hs].astype(BF16)) * (X_HEAD_DIM ** -0.5)
        outs.append(_dot(_softmax_rows(s).astype(BF16), mv_ref[:, hs].astype(BF16)).astype(BF16))
    return jnp.concatenate(outs, axis=1)


def _ffn_kernel(h_ref, *rest, stride, padc, fc, attend):
    if attend:
        hq_ref, mk_ref, mv_ref, *rest = rest
        xo = _xattn_tile(hq_ref, mk_ref, mv_ref)
    else:
        xo_ref, *rest = rest
        xo = xo_ref[...]
    cin_ref, wxo_ref, g_ref, wu_ref, wv_ref, cw_ref, wd_ref, fg_ref, y_ref, tail_ref, xp_ref, acc_ref = rest
    tm = h_ref.shape[0]

    @pl.when(pl.program_id(1) == 0)
    def _():
        tail_ref[...] = cin_ref[...]

    h = h_ref[...] + _dot(xo, wxo_ref[...])
    acc_ref[...] = h
    hn = _rms(h, g_ref[...]).astype(BF16)
    for lo, hi in zip((0,) + fc, fc + (D_FF,)):
        cs = slice(lo, hi)
        n = hi - lo
        u = _dot(hn, wu_ref[:, cs])
        gate = _dot(hn, wv_ref[:, cs])
        xp_ref[0:padc, 0:n] = tail_ref[0, :, cs]
        xp_ref[padc:padc + tm, 0:n] = u
        cw = cw_ref[:, cs]
        uc = u * cw[2:3]
        for j in range(FFN_CONV - 1):
            o = padc - (FFN_CONV - 1 - j) * stride
            uc = uc + xp_ref[o:o + tm, 0:n] * cw[j:j + 1]
        tail_ref[0, :, cs] = xp_ref[tm:tm + padc, 0:n]
        acc_ref[...] += _dot((_silu(uc) * gate).astype(BF16), wd_ref[cs, :])
    y_ref[...] = _rms(acc_ref[...], fg_ref[...])


FFN_CUTS = (1536,)


def _ffn(h, xq, mem, cin, wxo, g, wup, cw, wd, fg, tm, stride, fc=FFN_CUTS):
    m = h.shape[0]
    half = lambda c: pl.BlockSpec((D_MODEL, D_FF), lambda *_: (0, c), pipeline_mode=pl.Buffered(1))
    nseq, padc, _ = cin.shape
    nt = m // nseq // tm
    row = lambda w: pl.BlockSpec((tm, w), lambda s, j: (s * nt + j, 0))
    car = pl.BlockSpec((1, padc, D_FF), lambda s, j: (s, 0, 0))
    mem = list(mem or ())
    mem_specs = [pl.BlockSpec((a.shape[0] // nseq, X_WIDTH), lambda s, j: (s, 0)) for a in mem]
    return pl.pallas_call(
        functools.partial(_ffn_kernel, stride=stride, padc=padc, fc=fc, attend=bool(mem)),
        grid=(nseq, nt),
        in_specs=[row(D_MODEL), row(X_WIDTH), *mem_specs, car, _resident(wxo.shape), _resident((1, D_MODEL)),
                  half(0), half(1), _resident(cw.shape), _resident(wd.shape), _resident((1, D_MODEL))],
        out_specs=[row(D_MODEL), car],
        out_shape=[jax.ShapeDtypeStruct((m, D_MODEL), F32), jax.ShapeDtypeStruct(cin.shape, F32)],
        scratch_shapes=[pltpu.VMEM((tm + padc, max(b - a for a, b in zip((0,) + fc, fc + (D_FF,)))), F32),
                        pltpu.VMEM((tm, D_MODEL), F32)],
        compiler_params=_cparams("parallel", "arbitrary"),
        name="ffn",
    )(h, xq, *mem, cin, wxo, g, wup, wup, cw, wd, fg)


def _lane_row(vec, offset=0):
    return jnp.zeros((1, LANES), F32).at[0, offset:offset + vec.shape[0]].set(vec.astype(F32))


W_IN_QKV = A_WIDTH + 2 * A_KV_WIDTH
W_IN_D = W_IN_QKV + 4 * DN_WIDTH
W_IN_AB = W_IN_D + 2 * DN_HEADS
W_IN_G = W_IN_AB + 2 * D_MODEL


def _split_w_in_kernel(wt_ref, qkv_ref, d_ref, ab_ref, g_ref):
    tk = wt_ref.shape[1]
    qkv_ref[...] = wt_ref[:W_IN_QKV, :].T.astype(BF16)
    d_ref[...] = wt_ref[W_IN_QKV:W_IN_D, :].T.astype(BF16)
    ab = jnp.concatenate([wt_ref[W_IN_D:W_IN_AB, :], jnp.zeros((LANES - (W_IN_AB - W_IN_D), tk), F32)], axis=0)
    ab_ref[...] = ab.T.astype(BF16)
    g_ref[...] = wt_ref[W_IN_AB:W_IN_G, :].T.astype(BF16)


def _split_w_in(w_in_t, tk=256):
    k = w_in_t.shape[1]
    widths = (W_IN_QKV, W_IN_D - W_IN_QKV, LANES, W_IN_G - W_IN_AB)
    return pl.pallas_call(
        _split_w_in_kernel,
        grid=(k // tk,),
        in_specs=[pl.BlockSpec((w_in_t.shape[0], tk), lambda i: (0, i))],
        out_specs=[pl.BlockSpec((tk, wd), lambda i: (i, 0)) for wd in widths],
        out_shape=[jax.ShapeDtypeStruct((k, wd), BF16) for wd in widths],
        compiler_params=_cparams("parallel"),
        name="split_w_in",
    )(w_in_t)


def _prep(p):
    wqkv, wd, wab, wg = _split_w_in(p["w_in"].T)
    w = {
        "wqkv": wqkv,
        "wd": wd,
        "wab": wab,
        "wg": wg,
        "norm_mix_g": p["norm_mix_g"].reshape(1, D_MODEL),
        "dn_conv_w": p["dn_conv_w"],
        "alog": _lane_row(p["dn_a_log"]),
        "dtb": _lane_row(p["dn_dt_bias"]),
        "dn_norm_g": p["dn_norm_g"].reshape(1, DN_HEAD_DIM),
        "sinks": p["sinks"].astype(F32),
        "w_br_a": p["w_br_a"].astype(BF16),
        "w_br_b": p["w_br_b"].astype(BF16),
        "w_mix_out": p["w_mix_out"].astype(BF16),
        "norm_x_g": p["norm_x_g"].reshape(1, D_MODEL),
        "w_xq": p["w_xq"].astype(BF16),
        "w_xo": p["w_xo"].astype(BF16),
        "norm_ffn_g": p["norm_ffn_g"].reshape(1, D_MODEL),
        "w_up": p["w_up"].astype(BF16),
        "ffn_conv_w": p["ffn_conv_w"],
        "w_down": p["w_down"].astype(BF16),
        "final_norm_g": p["final_norm_g"].reshape(1, D_MODEL),
    }
    return w


def _tile(m, pref):
    return pref if m % pref == 0 else m


def _prompt_layer(x, mem, norm_mem_g, w_xkv, w):
    bsz, seqlen, _ = x.shape
    m = bsz * seqlen
    x2 = x.reshape(m, D_MODEL)
    tm = _tile(seqlen, 512)
    tabs = _rope_tables(jnp.arange(seqlen, dtype=jnp.int32))
    assert seqlen >= WINDOW, "the window outputs are taken from a full last window"
    q, k, v, ga, gb, kwin, vwin, yb, dn_state, dn_tail = _proj_gdn(
        x2, w["norm_mix_g"], tabs, w["wqkv"], w["wd"], w["wab"], w["wg"], w["dn_conv_w"], w["alog"], w["dtb"],
        w["dn_norm_g"], bsz, tm)
    h, hq = _merge(x2, (w["sinks"], q, k, v), yb, ga, gb, w["w_br_a"], w["w_br_b"], w["w_mix_out"],
                   w["norm_x_g"], w["w_xq"], bsz, tm)
    nm = mem.shape[1]
    mk, mv, mk_rows, mv_rows = _memkv(mem.reshape(bsz * nm, D_MODEL), norm_mem_g.reshape(1, D_MODEL),
                                      w_xkv.astype(BF16), _tile(bsz * nm, 512))
    cin = jnp.zeros((bsz, SUBLANES, D_FF), F32)
    y, tail = _ffn(h, hq, (mk, mv), cin, w["w_xo"], w["norm_ffn_g"], w["w_up"], w["ffn_conv_w"],
                   w["w_down"], w["final_norm_g"], tm, 1)
    win = lambda a: jnp.transpose(a.reshape(bsz, A_KV_HEADS, A_HEAD_DIM, WINDOW), (0, 3, 1, 2))
    new = (
        win(kwin),
        win(vwin),
        dn_tail[:, SUBLANES - (DN_CONV - 1):],
        dn_state,
        mk_rows.reshape(bsz, nm, X_HEADS, X_HEAD_DIM),
        mv_rows.reshape(bsz, nm, X_HEADS, X_HEAD_DIM),
        tail[:, SUBLANES - (FFN_CONV - 1):],
    )
    return y.reshape(bsz, seqlen, D_MODEL), new


def _sample_layer(x, pos0, win_k, win_v, dn_buf, dn_state, mem_k, mem_v, ffn_buf, w):
    bsz, t, _ = x.shape
    m = bsz * t
    sr = SAMPLE_ROWS
    x2 = x.reshape(m, D_MODEL)
    tabs = _rope_tables(jnp.tile(pos0 + jnp.arange(t, dtype=jnp.int32), bsz))
    q, k, v, dqkv, dz, ab, ga, gb = _proj(x2, w["norm_mix_g"], tabs, w["wqkv"], w["wd"], w["wab"], w["wg"], m)

    wlen = win_k.shape[1]
    qh = jnp.transpose(q.reshape(bsz, t, A_HEADS, A_HEAD_DIM), (0, 2, 1, 3)).reshape(bsz, A_HEADS * t, A_HEAD_DIM)
    padrows = lambda a: jnp.pad(a.reshape(bsz, t, -1), ((0, 0), (0, sr - t), (0, 0)))
    newrows = lambda a: jnp.pad(a.reshape(bsz, t * A_KV_HEADS, A_HEAD_DIM),
                                ((0, 0), (0, sr - t * A_KV_HEADS), (0, 0))).astype(BF16)
    kv_major = lambda a: jnp.transpose(a, (0, 2, 3, 1))
    nbw = min(bsz, LANES // t)

    def new_t(a):
        a = jnp.transpose(a.reshape(bsz // nbw, nbw * t, A_KV_HEADS, A_HEAD_DIM), (0, 2, 3, 1))
        return jnp.pad(a, ((0, 0), (0, 0), (0, 0), (0, LANES - nbw * t)))

    oh, new_wk, new_wv = _swa_sample(qh, kv_major(win_k), kv_major(win_v), newrows(k), newrows(v),
                                     new_t(k), new_t(v), w["sinks"], t)
    ya = jnp.transpose(oh.reshape(bsz, A_HEADS, t, A_HEAD_DIM), (0, 2, 1, 3)).reshape(m, A_WIDTH)
    new_wk, new_wv = (jnp.transpose(a, (0, 3, 1, 2)) for a in (new_wk, new_wv))

    dqkv3 = dqkv.reshape(bsz, t, 3 * DN_WIDTH)
    yb, new_s = _gdn_sample(dn_buf, dqkv3, dz.reshape(bsz, t, DN_WIDTH), ab.reshape(bsz, t, LANES), dn_state,
                            w["dn_conv_w"], w["alog"], w["dtb"], w["dn_norm_g"])
    yb = yb.reshape(m, DN_WIDTH)
    new_dn_buf = jnp.concatenate([dn_buf, dqkv3], axis=1)[:, -(DN_CONV - 1):]

    h, hq = _merge(x2, ya, yb, ga, gb, w["w_br_a"], w["w_br_b"], w["w_mix_out"], w["norm_x_g"], w["w_xq"], 1, m)
    nm = mem_k.shape[1]
    xo = _xattn_sample(padrows(hq), mem_k.reshape(bsz, nm * X_HEADS, X_HEAD_DIM),
                       mem_v.reshape(bsz, nm * X_HEADS, X_HEAD_DIM))
    xo = xo[:, :t]

    tmaj = lambda a: jnp.transpose(a.reshape(bsz, t, -1), (1, 0, 2)).reshape(m, -1)
    cin = jnp.transpose(ffn_buf, (1, 0, 2)).reshape(1, (FFN_CONV - 1) * bsz, D_FF)
    y, tail = _ffn(tmaj(h), tmaj(xo), None, cin, w["w_xo"], w["norm_ffn_g"], w["w_up"], w["ffn_conv_w"],
                   w["w_down"], w["final_norm_g"], m, bsz)
    y = jnp.transpose(y.reshape(t, bsz, D_MODEL), (1, 0, 2))
    new_ffn = jnp.transpose(tail.reshape(FFN_CONV - 1, bsz, D_FF), (1, 0, 2))
    return y, (new_wk, new_wv, new_dn_buf, new_s, new_ffn)


def kernel(x_prompt, x_sample, mem_prompt, cache_win_k, cache_win_v, state_dn_conv, state_dn, cache_mem_k, cache_mem_v, state_ffn_conv, norm_mix_g, w_in, dn_conv_w, dn_a_log, dn_dt_bias, dn_norm_g, attn_sinks, w_br_a, w_br_b, w_mix_out, norm_x_g, norm_mem_g, w_xq, w_xkv, w_xo, norm_ffn_g, w_up, ffn_conv_w, w_down, final_norm_g):
    p = {"norm_mix_g": norm_mix_g[0], "w_in": w_in[0], "dn_conv_w": dn_conv_w[0], "dn_a_log": dn_a_log[0],
         "dn_dt_bias": dn_dt_bias[0], "dn_norm_g": dn_norm_g[0], "sinks": attn_sinks[0], "w_br_a": w_br_a[0],
         "w_br_b": w_br_b[0], "w_mix_out": w_mix_out[0], "norm_x_g": norm_x_g[0], "w_xq": w_xq[0],
         "w_xo": w_xo[0], "norm_ffn_g": norm_ffn_g[0], "w_up": w_up[0], "ffn_conv_w": ffn_conv_w[0],
         "w_down": w_down[0], "final_norm_g": final_norm_g}
    w = _prep(p)
    yp, newp = _prompt_layer(x_prompt, mem_prompt, norm_mem_g[0], w_xkv[0], w)
    ys, news = _sample_layer(x_sample, PAST_LEN, cache_win_k[0], cache_win_v[0], state_dn_conv[0], state_dn[0],
                             cache_mem_k[0], cache_mem_v[0], state_ffn_conv[0], w)
    lead = lambda a: a[None]
    p_win_k, p_win_v, p_dn_conv, p_dn_state, p_mem_k, p_mem_v, p_ffn_conv = [lead(a) for a in newp]
    s_win_k, s_win_v, s_dn_conv, s_dn_state, s_ffn_conv = [lead(a) for a in news]
    return (yp, ys, p_win_k, p_win_v, p_dn_conv, p_dn_state, p_mem_k, p_mem_v, p_ffn_conv,
            s_win_k, s_win_v, s_dn_conv, s_dn_state, s_ffn_conv)
```

```python
import functools

import jax
import jax.numpy as jnp
from jax import lax
from jax.experimental import pallas as pl
from jax.experimental.pallas import tpu as pltpu

F32 = jnp.float32
BF16 = jnp.bfloat16

D_MODEL = 1024
A_HEADS = 8
A_KV_HEADS = 2
A_HEAD_DIM = 64
A_WIDTH = 512
A_KV_WIDTH = 128
WINDOW = 128
ROT_DIM = 16
ROPE_THETA = 500000.0
DN_HEADS = 4
DN_HEAD_DIM = 128
DN_WIDTH = 512
DN_CONV = 4
X_HEADS = 4
X_HEAD_DIM = 128
X_WIDTH = 512
D_FF = 2816
FFN_CONV = 3
EPS = 1e-6
PAST_LEN = 16384

LANES = 128
SUBLANES = 8
VMEM_LIMIT = 56 * 1024 * 1024
ROW_TILE = 512
CHUNK = 128
SAMPLE_ROWS = 16
GDN_SAMPLE_ROWS = 8


def _cparams(*sem):
    return pltpu.CompilerParams(dimension_semantics=sem, vmem_limit_bytes=VMEM_LIMIT)


def _resident(shape):
    return pl.BlockSpec(shape, lambda *_: (0,) * len(shape), pipeline_mode=pl.Buffered(1))


def _rms(x, g):
    return x * lax.rsqrt(jnp.mean(x * x, axis=-1, keepdims=True) + EPS) * g


def _dot(a, b):
    return jnp.dot(a, b, preferred_element_type=F32)


def _dot_nt(a, b):
    return lax.dot_general(a, b, (((1,), (1,)), ((), ())), preferred_element_type=F32)


def _silu(x):
    return x * jax.nn.sigmoid(x)


def _rope(seg, c, s1, s2):
    return seg * c + pltpu.roll(seg, LANES - 8, 1) * s1 + pltpu.roll(seg, 8, 1) * s2


PROJ_COLS = 512
assert (3 * DN_WIDTH) % PROJ_COLS == 0 and D_MODEL % PROJ_COLS == 0
GATE_DT = BF16


def _proj_tile(x_ref, g_ref, c_ref, s1_ref, s2_ref, wqkv_ref, wd_ref, wab_ref, wg_ref,
               q_ref, k_ref, v_ref, dqkv_ref, dz_ref, ab_ref, ga_ref, gb_ref, dqkv_row0=0, win_refs=None):
    tm = x_ref.shape[0]
    xb = _rms(x_ref[...], g_ref[...]).astype(BF16)
    yield
    c, s1, s2 = c_ref[...], s1_ref[...], s2_ref[...]
    z = _dot(xb, wqkv_ref[...])
    for i in range(A_WIDTH // LANES):
        sl = slice(i * LANES, (i + 1) * LANES)
        q_ref[:, sl] = _rope(z[:, sl], c, s1, s2).astype(BF16)
    k = _rope(z[:, A_WIDTH:A_WIDTH + LANES], c, s1, s2)
    k_ref[...] = k
    v_ref[...] = z[:, A_WIDTH + LANES:]
    if win_refs is not None:
        win_refs[0][0] = k[tm - WINDOW:].T
        win_refs[1][0] = z[tm - WINDOW:, A_WIDTH + LANES:].T
    yield
    pc = PROJ_COLS
    for j in range(3 * DN_WIDTH // pc):
        dqkv_ref[dqkv_row0:dqkv_row0 + tm, j * pc:(j + 1) * pc] = _dot(xb, wd_ref[:, j * pc:(j + 1) * pc])
        yield
    dz_ref[...] = _dot(xb, wd_ref[:, 3 * DN_WIDTH:])
    ab_ref[...] = _dot(xb, wab_ref[...])
    yield
    for dst, base in ((ga_ref, 0), (gb_ref, D_MODEL)):
        for j in range(D_MODEL // pc):
            dst[:, j * pc:(j + 1) * pc] = _dot(xb, wg_ref[:, base + j * pc:base + (j + 1) * pc]).astype(dst.dtype)
            yield


def _proj_kernel(*refs):
    _drain(_proj_tile(*refs))


def _proj(x, g, tabs, wqkv, wd, wab, wg, tm):
    m = x.shape[0]
    nt = tabs[0].shape[0] // tm
    row = lambda w: pl.BlockSpec((tm, w), lambda i: (i, 0))
    tab = pl.BlockSpec((tm, LANES), lambda i: (i % nt, 0))
    widths = (A_WIDTH, LANES, LANES, 3 * DN_WIDTH, DN_WIDTH, LANES, D_MODEL, D_MODEL)
    dts = (BF16, F32, F32, F32, F32, F32, GATE_DT, GATE_DT)
    return pl.pallas_call(
        _proj_kernel,
        grid=(m // tm,),
        in_specs=[row(D_MODEL), _resident((1, D_MODEL)), tab, tab, tab,
                  _resident(wqkv.shape), _resident(wd.shape), _resident(wab.shape), _resident(wg.shape)],
        out_specs=[row(w) for w in widths],
        out_shape=[jax.ShapeDtypeStruct((m, w), d) for w, d in zip(widths, dts)],
        compiler_params=_cparams("parallel"),
        name="in_proj",
    )(x, g, *tabs, wqkv, wd, wab, wg)


def _rope_tables(pos):
    half = ROT_DIM // 2
    d = jnp.arange(LANES, dtype=jnp.int32) % A_HEAD_DIM
    inv = ROPE_THETA ** (-2.0 * (d % half).astype(F32) / ROT_DIM)
    ang = pos.astype(F32)[:, None] * inv[None, :]
    c, s = jnp.cos(ang), jnp.sin(ang)
    return (jnp.where(d < ROT_DIM, c, 1.0), jnp.where(d < half, -s, 0.0),
            jnp.where((d >= half) & (d < ROT_DIM), s, 0.0))


def _both_halves(t, lane_lo):
    r = pltpu.roll(t, A_HEAD_DIM, 1)
    return jnp.where(lane_lo, t, r), jnp.where(lane_lo, r, t)


def _sink_softmax(s, valid, sink):
    s = jnp.where(valid, s, -jnp.inf)
    m = jnp.maximum(jnp.max(s, axis=-1, keepdims=True), sink)
    p = jnp.exp(s - m)
    den = jnp.sum(p, axis=-1, keepdims=True) + jnp.exp(sink - m)
    return p * (1.0 / den)


def _swa_tile(sink_ref, q_ref, k, v, first):
    w = WINDOW
    lane_lo_k = lax.broadcasted_iota(jnp.int32, k.shape, 1) < A_HEAD_DIM
    kk = [t.astype(BF16) for t in _both_halves(k, lane_lo_k)]
    vv = [t.astype(BF16) for t in _both_halves(v, lane_lo_k)]
    lane_lo = lax.broadcasted_iota(jnp.int32, (w, LANES), 1) < A_HEAD_DIM
    row = lax.broadcasted_iota(jnp.int32, (4 * w, 2 * w), 0)
    col = lax.broadcasted_iota(jnp.int32, (4 * w, 2 * w), 1)
    d = (row & (w - 1)) + w - col
    band = (d >= 0) & (d < w)
    band_first = band & (col >= jnp.where(first, w, 0))
    hrow = lax.broadcasted_iota(jnp.int32, (4 * w, 1), 0) // w
    sinks = []
    for g in range(A_KV_HEADS):
        sink = jnp.zeros((4 * w, 1), F32)
        for j in range(4):
            sink = jnp.where(hrow == j, sink_ref[4 * g + j], sink)
        sinks.append(sink)
    zero = jnp.zeros((), BF16)
    scale = jnp.asarray(A_HEAD_DIM ** -0.5, BF16)
    nblk = q_ref.shape[0] // w
    probs = [(b, g) for b in range(nblk) for g in range(A_KV_HEADS)]

    def queries(b, g):
        parts = []
        for sgm in range(2):
            seg = q_ref[b * w:(b + 1) * w, (2 * g + sgm) * LANES:(2 * g + sgm + 1) * LANES] * scale
            parts += [jnp.where(lane_lo, seg, zero), jnp.where(lane_lo, zero, seg)]
        return jnp.concatenate(parts, axis=0)

    ss = [_dot_nt(queries(b, g), kk[g][b * w:(b + 2) * w]) for b, g in probs]
    yield
    ps = [_sink_softmax(s, band_first if b == 0 else band, sinks[g]).astype(BF16) for s, (b, g) in zip(ss, probs)]
    yield
    os_ = [_dot(p, vv[g][b * w:(b + 2) * w]) for p, (b, g) in zip(ps, probs)]
    yield
    segs = [[jnp.where(lane_lo, o[(2 * sgm) * w:(2 * sgm + 1) * w], o[(2 * sgm + 1) * w:(2 * sgm + 2) * w]
                       ).astype(BF16) for sgm in range(2)] for o in os_]
    return jnp.concatenate([jnp.concatenate(segs[A_KV_HEADS * b] + segs[A_KV_HEADS * b + 1], axis=1)
                            for b in range(nblk)], axis=0)


def _swa_sample_kernel(sink_ref, q_ref, ckt_ref, cvt_ref, kn_ref, vn_ref, knt_ref, vnt_ref,
                       o_ref, cko_ref, cvo_ref, *, nb, t):
    w = ckt_ref.shape[3]
    old = lax.broadcasted_iota(jnp.int32, (A_HEAD_DIM, w), 1) < w - t
    for src_ref, new_ref, dst_ref in ((ckt_ref, knt_ref, cko_ref), (cvt_ref, vnt_ref, cvo_ref)):
        for g in range(A_KV_HEADS):
            new = new_ref[0, g]
            for b in range(nb):
                dst_ref[b, g] = jnp.where(old, pltpu.roll(src_ref[b, g], w - t, 1),
                                          pltpu.roll(new, (w - t - t * b) % LANES, 1))
    rows = q_ref.shape[1] // A_HEADS
    gq = rows * (A_HEADS // A_KV_HEADS)
    r = lax.broadcasted_iota(jnp.int32, (gq, w), 0)
    c = lax.broadcasted_iota(jnp.int32, (gq, w), 1)
    valid_c = c > r % rows
    rn = lax.broadcasted_iota(jnp.int32, (gq, kn_ref.shape[1]), 0)
    cn = lax.broadcasted_iota(jnp.int32, (gq, kn_ref.shape[1]), 1)
    tn = cn // A_KV_HEADS
    causal_n = (tn <= rn % rows) & (tn < t)
    hrow = lax.broadcasted_iota(jnp.int32, (gq, 1), 0) // rows
    valid_n, sink = [], []
    for g in range(A_KV_HEADS):
        valid_n.append(causal_n & (cn % A_KV_HEADS == g))
        sk = jnp.zeros((gq, 1), F32)
        for j in range(A_HEADS // A_KV_HEADS):
            sk = jnp.where(hrow == j, sink_ref[g * (A_HEADS // A_KV_HEADS) + j], sk)
        sink.append(sk)
    probs = [(b, g) for b in range(nb) for g in range(A_KV_HEADS)]
    scale = A_HEAD_DIM ** -0.5
    qs = [q_ref[b, g * gq:(g + 1) * gq, :] for b, g in probs]
    scs = [jnp.where(valid_c, _dot(q, ckt_ref[b, g].astype(BF16)) * scale, -jnp.inf)
           for q, (b, g) in zip(qs, probs)]
    sns = [jnp.where(valid_n[g], _dot_nt(q, kn_ref[b]) * scale, -jnp.inf) for q, (b, g) in zip(qs, probs)]
    ms = [jnp.maximum(jnp.maximum(jnp.max(sc, -1, keepdims=True), jnp.max(sn, -1, keepdims=True)), sink[g])
          for sc, sn, (b, g) in zip(scs, sns, probs)]
    pcs = [jnp.exp(sc - m) for sc, m in zip(scs, ms)]
    pns = [jnp.exp(sn - m) for sn, m in zip(sns, ms)]
    invs = [1.0 / (jnp.sum(pc, -1, keepdims=True) + jnp.sum(pn, -1, keepdims=True) + jnp.exp(sink[g] - m))
            for pc, pn, m, (b, g) in zip(pcs, pns, ms, probs)]
    for pc, pn, inv, (b, g) in zip(pcs, pns, invs, probs):
        o_ref[b, g * gq:(g + 1) * gq, :] = (_dot_nt((pc * inv).astype(BF16), cvt_ref[b, g].astype(BF16))
                                            + _dot((pn * inv).astype(BF16), vn_ref[b]))


def _swa_sample(q, ckt, cvt, kn, vn, knt, vnt, sinks, t):
    bsz, nq, _ = q.shape
    nb = bsz // knt.shape[0]
    assert nb * t <= LANES and ckt.shape[3] == LANES
    blk = lambda a, n=nb: pl.BlockSpec((n,) + a.shape[1:], lambda i: (i,) + (0,) * (a.ndim - 1))
    return pl.pallas_call(
        functools.partial(_swa_sample_kernel, nb=nb, t=t),
        grid=(bsz // nb,),
        in_specs=[pl.BlockSpec(memory_space=pltpu.SMEM), blk(q), blk(ckt), blk(cvt), blk(kn), blk(vn),
                  blk(knt, 1), blk(vnt, 1)],
        out_specs=[blk(q), blk(ckt), blk(cvt)],
        out_shape=[jax.ShapeDtypeStruct(q.shape, F32), jax.ShapeDtypeStruct(ckt.shape, F32),
                   jax.ShapeDtypeStruct(cvt.shape, F32)],
        compiler_params=_cparams("parallel"),
        name="swa_sample",
    )(sinks, q, ckt, cvt, kn, vn, knt, vnt)


def _lane_bcast(x, lane):
    return jnp.broadcast_to(x[:, lane:lane + 1], (x.shape[0], LANES))


def _cumsum_rows(x, block):
    c = x.shape[0]
    rowi = lax.broadcasted_iota(jnp.int32, (c, c), 0)
    coli = lax.broadcasted_iota(jnp.int32, (c, c), 1)
    ones = (((rowi // block) == (coli // block)) & (rowi >= coli)).astype(BF16)
    hi = x.astype(BF16)
    r1 = x - hi.astype(F32)
    mid = r1.astype(BF16)
    lo = (r1 - mid.astype(F32)).astype(BF16)
    s = _dot(ones, jnp.concatenate([hi, mid, lo], axis=1))
    return s[:, :LANES] + s[:, LANES:2 * LANES] + s[:, 2 * LANES:]


def _l2n(t):
    return t * lax.rsqrt(jnp.sum(t * t, axis=-1, keepdims=True) + EPS)


def _gates(ab, alog, dtb):
    x = ab + dtb
    sp = jnp.maximum(x, 0.0) + jnp.log1p(jnp.exp(-jnp.abs(x)))
    return -jnp.exp(alog) * sp, jax.nn.sigmoid(ab)


def _merge_masks(c, top):
    rowi = lax.broadcasted_iota(jnp.int32, (c, c), 0)
    coli = lax.broadcasted_iota(jnp.int32, (c, c), 1)
    masks = []
    s = 1
    while s < top:
        rb, cb = rowi // s, coli // s
        masks.append(((rb // 2) == (cb // 2)) & ((rb % 2) == 1) & ((cb % 2) == 0))
        s *= 2
    return masks


def _each(f, *lists):
    return [f(*t) for t in zip(*lists)]


def _drain(stages):
    try:
        while True:
            next(stages)
    except StopIteration as done:
        return done.value


def _interleave(*staged):
    live = list(staged)
    values = {}
    while live:
        for item in tuple(live):
            stages, per_turn = item
            try:
                for _ in range(per_turn):
                    next(stages)
            except StopIteration as done:
                values[id(stages)] = done.value
                live.remove(item)
    return [values[id(stages)] for stages, _ in staged]


def _chunk_local(qs, ks, vs, gcols, grows, betas, tril, merges):
    c = qs[0].shape[0]
    decays = _each(lambda gc, gr: jnp.exp(jnp.minimum(gc - gr, 0.0)), gcols, grows)
    kbs = _each(lambda k, b: k * b, ks, betas)
    yield
    ms = _each(lambda q, kb, k: _dot_nt(jnp.concatenate([q, kb], axis=0).astype(BF16), k.astype(BF16)),
               qs, kbs, ks)
    yield
    qks = _each(lambda m, d: jnp.where(tril, m[:c] * d, 0.0), ms, decays)
    a = _each(lambda m, d: m[c:] * d, ms, decays)
    eye = (lax.broadcasted_iota(jnp.int32, (c, c), 0) == lax.broadcasted_iota(jnp.int32, (c, c), 1)).astype(F32)
    ts = [eye - jnp.where(merges[0], x, 0.0) if merges else eye for x in a]
    yield
    for off in merges[1:]:
        tbs = [t.astype(BF16) for t in ts]
        zs = _each(lambda x, tb: _dot(jnp.where(off, x, 0.0).astype(BF16), tb), a, tbs)
        yield
        ts = _each(lambda t, tb, z: t - _dot(tb, z.astype(BF16)), ts, tbs, zs)
        yield
    ns = [t - eye for t in ts]
    egs = [jnp.exp(gc) for gc in gcols]
    rhss = _each(lambda v, b, kb, eg: jnp.concatenate([v * b, kb * eg], axis=1), vs, betas, kbs, egs)
    yield
    uws = _each(lambda r, n: r + _dot(n.astype(BF16), r.astype(BF16)), rhss, ns)
    yield
    return ([x[:, :LANES] for x in uws], [x[:, LANES:] for x in uws], qks,
            _each(lambda q, eg: q * eg, qs, egs))


def _gdn_tile(xp_ref, dz_ref, ab_ref, cw_ref, alog_ref, dtb_ref, ng_ref, y_ref, s_ref, nc):
    c = CHUNK
    pad = SUBLANES
    cw = cw_ref[...]
    g, beta = _gates(ab_ref[...], alog_ref[...], dtb_ref[...])
    rowi = lax.broadcasted_iota(jnp.int32, (c, c), 0)
    coli = lax.broadcasted_iota(jnp.int32, (c, c), 1)
    tril = rowi >= coli
    merges = _merge_masks(c, c)
    ng = ng_ref[...]
    heads = range(DN_HEADS)
    probs = [(ci, h) for ci in range(nc) for h in heads]
    rows = lambda ci: slice(ci * c, (ci + 1) * c)
    lanes = lambda part, h: slice(part * DN_WIDTH + h * LANES, part * DN_WIDTH + (h + 1) * LANES)
    yield
    qs, ks, vs, gcols, grows, betas = [], [], [], [], [], []
    for ci in range(nc):
        conv = xp_ref[pad + ci * c:pad + (ci + 1) * c, :] * cw[DN_CONV - 1:DN_CONV]
        for j in range(DN_CONV - 1):
            o = pad - (DN_CONV - 1) + j + ci * c
            conv = conv + xp_ref[o:o + c, :] * cw[j:j + 1]
        conv = _silu(conv)
        gc = _cumsum_rows(g[rows(ci)], c)
        gct = gc.T
        qs += [_l2n(conv[:, lanes(0, h)]) * (DN_HEAD_DIM ** -0.5) for h in heads]
        ks += [_l2n(conv[:, lanes(1, h)]) for h in heads]
        vs += [conv[:, lanes(2, h)] for h in heads]
        gcols += [_lane_bcast(gc, h) for h in heads]
        grows += [gct[h:h + 1, :] for h in heads]
        betas += [_lane_bcast(beta[rows(ci)], DN_HEADS + h) for h in heads]
        yield
    us, ws, qks, qds = yield from _chunk_local(qs, ks, vs, gcols, grows, betas, tril, merges)
    glasts = [gc[c - 1:c, :] for gc in gcols]
    kdts = _each(lambda k, gl, gc: (k * jnp.exp(gl - gc)).T, ks, glasts, gcols)
    wqs = _each(lambda w, qd: jnp.concatenate([w, qd], axis=0).astype(BF16), ws, qds)
    qkks = _each(lambda qk, kdt: jnp.concatenate([qk, kdt], axis=0).astype(BF16), qks, kdts)
    yield
    ss = [s_ref[h] for h in heads]
    for ci in range(nc):
        pr = [ci * DN_HEADS + h for h in heads]
        r2s = [_dot(wqs[p], s.astype(BF16)) for p, s in zip(pr, ss)]
        vnews = [us[p] - r2[:c] for p, r2 in zip(pr, r2s)]
        yield
        r3s = [_dot(qkks[p], vn.astype(BF16)) for p, vn in zip(pr, vnews)]
        ss = [s * jnp.exp(glasts[p]) + r3[c:] for p, s, r3 in zip(pr, ss, r3s)]
        for h in heads:
            o = r2s[h][c:] + r3s[h][:c]
            y_ref[rows(ci), lanes(0, h)] = (_rms(o, ng) * _silu(dz_ref[rows(ci), lanes(0, h)])).astype(BF16)
        yield
    for h in heads:
        s_ref[h] = ss[h]


def _proj_gdn_kernel(x_ref, g_ref, c_ref, s1_ref, s2_ref, wqkv_ref, wd_ref, wab_ref, wg_ref,
                     cw_ref, alog_ref, dtb_ref, ng_ref,
                     q_ref, k_ref, v_ref, ga_ref, gb_ref, kw_ref, vw_ref, y_ref, sout_ref, tail_ref,
                     xp_ref, dz_ref, ab_ref, carry_ref, s_ref, *, nt, nc):
    i = pl.program_id(0)
    pad = SUBLANES
    r = nc * CHUNK
    slot_a = i % 2
    slot_b = 1 - slot_a

    @pl.when(i == 0)
    def _():
        xp_ref[1] = jnp.zeros(xp_ref.shape[1:], F32)
        dz_ref[1] = jnp.zeros(dz_ref.shape[1:], F32)
        ab_ref[1] = jnp.zeros(ab_ref.shape[1:], F32)

    @pl.when(jnp.maximum(i - 1, 0) % nt == 0)
    def _():
        carry_ref[...] = jnp.zeros_like(carry_ref)
        s_ref[...] = jnp.zeros_like(s_ref)

    xpb_ref = xp_ref.at[slot_b]
    xpb_ref[0:pad, :] = carry_ref[...]
    _interleave(
        (_gdn_tile(xpb_ref, dz_ref.at[slot_b], ab_ref.at[slot_b], cw_ref, alog_ref, dtb_ref, ng_ref,
                   y_ref, s_ref, nc), 1),
        (_proj_tile(x_ref, g_ref, c_ref, s1_ref, s2_ref, wqkv_ref, wd_ref, wab_ref, wg_ref, q_ref, k_ref, v_ref,
                    xp_ref.at[slot_a], dz_ref.at[slot_a], ab_ref.at[slot_a], ga_ref, gb_ref, pad,
                    (kw_ref, vw_ref)), 1))
    carry_ref[...] = xpb_ref[r:r + pad, :]
    tail_ref[0] = carry_ref[...]
    sout_ref[0] = s_ref[...]


def _proj_gdn(x, g, tabs, wqkv, wd, wab, wg, cw, alog, dtb, ng, nseq, tm):
    m = x.shape[0]
    nt = m // nseq // tm
    last = m // tm - 1
    nc = tm // CHUNK
    proj = lambda i: jnp.minimum(i, last)
    gdn = lambda i: jnp.maximum(i - 1, 0)
    prow = lambda w: pl.BlockSpec((tm, w), lambda i: (proj(i), 0))
    tab = pl.BlockSpec((tm, LANES), lambda i: (proj(i) % nt, 0))
    per_seq = lambda *dims: pl.BlockSpec((1,) + dims, lambda i: (gdn(i) // nt,) + (0,) * len(dims))
    win = pl.BlockSpec((1, A_KV_WIDTH, WINDOW), lambda i: (proj(i) // nt, 0, 0))
    win_shape = jax.ShapeDtypeStruct((nseq, A_KV_WIDTH, WINDOW), F32)
    state = (DN_HEADS, DN_HEAD_DIM, DN_HEAD_DIM)
    bufs = [pltpu.VMEM((2, tm + SUBLANES, 3 * DN_WIDTH), F32), pltpu.VMEM((2, tm, DN_WIDTH), F32),
            pltpu.VMEM((2, tm, LANES), F32)]
    return pl.pallas_call(
        functools.partial(_proj_gdn_kernel, nt=nt, nc=nc),
        grid=(m // tm + 1,),
        in_specs=[prow(D_MODEL), _resident((1, D_MODEL)), tab, tab, tab,
                  _resident(wqkv.shape), _resident(wd.shape), _resident(wab.shape), _resident(wg.shape),
                  _resident(cw.shape), _resident((1, LANES)), _resident((1, LANES)), _resident((1, LANES))],
        out_specs=[prow(A_WIDTH), prow(LANES), prow(LANES), prow(D_MODEL), prow(D_MODEL), win, win,
                   pl.BlockSpec((tm, DN_WIDTH), lambda i: (gdn(i), 0)), per_seq(*state),
                   per_seq(SUBLANES, 3 * DN_WIDTH)],
        out_shape=[jax.ShapeDtypeStruct((m, A_WIDTH), BF16), jax.ShapeDtypeStruct((m, LANES), F32),
                   jax.ShapeDtypeStruct((m, LANES), F32), jax.ShapeDtypeStruct((m, D_MODEL), GATE_DT),
                   jax.ShapeDtypeStruct((m, D_MODEL), GATE_DT), win_shape, win_shape,
                   jax.ShapeDtypeStruct((m, DN_WIDTH), BF16),
                   jax.ShapeDtypeStruct((nseq,) + state, F32),
                   jax.ShapeDtypeStruct((nseq, SUBLANES, 3 * DN_WIDTH), F32)],
        scratch_shapes=bufs + [pltpu.VMEM((SUBLANES, 3 * DN_WIDTH), F32), pltpu.VMEM(state, F32)],
        compiler_params=_cparams("arbitrary"),
        name="proj_gdn",
    )(x, g, *tabs, wqkv, wd, wab, wg, cw, alog, dtb, ng)


def _gdn_sample_kernel(buf_ref, x_ref, dz_ref, ab_ref, s0_ref, cw_ref, alog_ref, dtb_ref, ng_ref,
                       y_ref, sout_ref, xp_ref, ab16_ref):
    t = x_ref.shape[1]
    c = CHUNK
    sr = GDN_SAMPLE_ROWS
    nb = c // sr
    pad = SUBLANES
    hist = DN_CONV - 1
    xp_ref[...] = jnp.zeros_like(xp_ref)
    ab16_ref[...] = jnp.zeros_like(ab16_ref)
    for b in range(nb):
        xp_ref[pad + b * sr - hist:pad + b * sr, :] = buf_ref[b]
        xp_ref[pad + b * sr:pad + b * sr + t, :] = x_ref[b]
        ab16_ref[b * sr:b * sr + t, :] = ab_ref[b]
    cw = cw_ref[...]
    conv = xp_ref[pad:pad + c, :] * cw[hist:hist + 1]
    for j in range(hist):
        conv = conv + xp_ref[pad - hist + j:pad - hist + j + c, :] * cw[j:j + 1]
    conv = _silu(conv)
    live = (lax.broadcasted_iota(jnp.int32, (c, LANES), 0) % sr) < t
    g, beta = _gates(ab16_ref[...], alog_ref[...], dtb_ref[...])
    g = jnp.where(live, g, 0.0)
    beta = jnp.where(live, beta, 0.0)
    gc = _cumsum_rows(g, sr)
    gct = gc.T
    rowi = lax.broadcasted_iota(jnp.int32, (c, c), 0)
    coli = lax.broadcasted_iota(jnp.int32, (c, c), 1)
    tril = ((rowi // sr) == (coli // sr)) & (rowi >= coli)
    merges = _merge_masks(c, pl.next_power_of_2(t))
    ng = ng_ref[...]
    rowb = lax.broadcasted_iota(jnp.int32, (c, LANES), 0) // sr
    heads = range(DN_HEADS)
    lanes = lambda part, h: slice(part * DN_WIDTH + h * LANES, part * DN_WIDTH + (h + 1) * LANES)
    ks = [_l2n(conv[:, lanes(1, h)]) for h in heads]
    gcols = [_lane_bcast(gc, h) for h in heads]
    us, ws, qks, qds = _drain(_chunk_local(
        [_l2n(conv[:, lanes(0, h)]) * (DN_HEAD_DIM ** -0.5) for h in heads], ks,
        [conv[:, lanes(2, h)] for h in heads], gcols, [gct[h:h + 1, :] for h in heads],
        [_lane_bcast(beta, DN_HEADS + h) for h in heads], tril, merges))
    seqs = range(nb)
    rows = lambda b: slice(b * sr, (b + 1) * sr)
    s0s = [[s0_ref[b, h] for b in seqs] for h in heads]
    r2s = [[_dot(jnp.concatenate([ws[h][rows(b)], qds[h][rows(b)]], axis=0).astype(BF16), s0s[h][b].astype(BF16))
            for b in seqs] for h in heads]
    vnews = [jnp.concatenate([us[h][rows(b)] - r2s[h][b][:sr] for b in seqs], axis=0) for h in heads]
    os_ = [_rms(jnp.concatenate([r2s[h][b][sr:] for b in seqs], axis=0)
                + _dot(qks[h].astype(BF16), vnews[h].astype(BF16)), ng) for h in heads]
    for h in heads:
        for b in seqs:
            y_ref[b, :, lanes(0, h)] = os_[h][b * sr:b * sr + t] * _silu(dz_ref[b, :, lanes(0, h)])
    glasts = [jnp.concatenate([jnp.broadcast_to(gcols[h][(b + 1) * sr - 1:(b + 1) * sr, :], (sr, LANES))
                               for b in seqs], axis=0) for h in heads]
    kdts = [(ks[h] * jnp.exp(glasts[h] - gcols[h])).T.astype(BF16) for h in heads]
    for h in heads:
        for b in seqs:
            vb = jnp.where(rowb == b, vnews[h], 0.0).astype(BF16)
            sout_ref[b, h] = s0s[h][b] * jnp.exp(glasts[h][b * sr:b * sr + 1, :]) + _dot(kdts[h], vb)


def _gdn_sample(buf, x, dz, ab, s0, cw, alog, dtb, ng):
    bsz, t, _ = x.shape
    assert t + DN_CONV - 1 <= GDN_SAMPLE_ROWS, "too many new tokens for one row tile per sequence"
    nb = CHUNK // GDN_SAMPLE_ROWS
    seq = lambda a: pl.BlockSpec((nb,) + a.shape[1:], lambda i: (i,) + (0,) * (a.ndim - 1))
    return pl.pallas_call(
        _gdn_sample_kernel,
        grid=(bsz // nb,),
        in_specs=[seq(buf), seq(x), seq(dz), seq(ab), seq(s0), _resident(cw.shape),
                  _resident((1, LANES)), _resident((1, LANES)), _resident((1, LANES))],
        out_specs=[seq(dz), seq(s0)],
        out_shape=[jax.ShapeDtypeStruct(dz.shape, F32), jax.ShapeDtypeStruct(s0.shape, F32)],
        scratch_shapes=[pltpu.VMEM((CHUNK + SUBLANES, 3 * DN_WIDTH), F32), pltpu.VMEM((CHUNK, LANES), F32)],
        compiler_params=_cparams("parallel"),
        name="gdn_sample",
    )(buf, x, dz, ab, s0, cw, alog, dtb, ng)


def _merge_kernel(x_ref, *rest, attend):
    if attend:
        sink_ref, q_ref, kp_ref, kc_ref, vp_ref, vc_ref, *rest = rest
    else:
        ya_ref, *rest = rest
    yb_ref, ga_ref, gb_ref, wa_ref, wb_ref, wo_ref, g_ref, wq_ref, h_ref, hq_ref = rest

    def deltanet_half():
        yb = _dot(yb_ref[...].astype(BF16), wb_ref[...])
        yield
        gate_a = jax.nn.sigmoid(ga_ref[...].astype(F32))
        yield
        return gate_a, jax.nn.sigmoid(gb_ref[...].astype(F32)) * yb

    if attend:
        ya, (gate_a, mix_b) = _interleave(
            (_swa_tile(sink_ref, q_ref, jnp.concatenate([kp_ref[...], kc_ref[...]], axis=0),
                       jnp.concatenate([vp_ref[...], vc_ref[...]], axis=0), pl.program_id(1) == 0), 1),
            (deltanet_half(), 1))
    else:
        ya = ya_ref[...]
        gate_a, mix_b = _drain(deltanet_half())
    mix = gate_a * _dot(ya.astype(BF16), wa_ref[...]) + mix_b
    h = x_ref[...] + _dot(mix.astype(BF16), wo_ref[...])
    h_ref[...] = h
    hq_ref[...] = _dot(_rms(h, g_ref[...]).astype(BF16), wq_ref[...]).astype(BF16)


def _merge(x, attn, yb, ga, gb, wa, wb, wo, g, wq, nseq, tm):
    m = x.shape[0]
    nt = m // nseq // tm
    row = lambda w: pl.BlockSpec((tm, w), lambda s, j: (s * nt + j, 0))
    attend = isinstance(attn, tuple)
    if attend:
        sinks, q, k, v = attn
        per = tm // WINDOW
        prev = pl.BlockSpec((WINDOW, A_KV_WIDTH), lambda s, j: (jnp.maximum((s * nt + j) * per - 1, 0), 0))
        attn_args = [sinks, q, k, k, v, v]
        attn_specs = [pl.BlockSpec(memory_space=pltpu.SMEM), row(A_WIDTH), prev, row(A_KV_WIDTH),
                      prev, row(A_KV_WIDTH)]
    else:
        attn_args, attn_specs = [attn], [row(A_WIDTH)]
    return pl.pallas_call(
        functools.partial(_merge_kernel, attend=attend),
        grid=(nseq, nt),
        in_specs=[row(D_MODEL), *attn_specs, row(DN_WIDTH), row(D_MODEL), row(D_MODEL),
                  _resident(wa.shape), _resident(wb.shape), _resident(wo.shape),
                  _resident((1, D_MODEL)), _resident(wq.shape)],
        out_specs=[row(D_MODEL), row(X_WIDTH)],
        out_shape=[jax.ShapeDtypeStruct((m, D_MODEL), F32), jax.ShapeDtypeStruct((m, X_WIDTH), BF16)],
        compiler_params=_cparams("parallel", "parallel"),
        name="merge",
    )(x, *attn_args, yb, ga, gb, wa, wb, wo, g, wq)


def _memkv_kernel(m_ref, g_ref, w_ref, k_ref, v_ref, kf_ref, vf_ref):
    tm = m_ref.shape[0]
    z = _dot(_rms(m_ref[...], g_ref[...]).astype(BF16), w_ref[...])
    k_ref[...] = z[:, :X_WIDTH]
    v_ref[...] = z[:, X_WIDTH:]
    for h in range(X_HEADS):
        kf_ref[pl.ds(h, tm, stride=X_HEADS), :] = z[:, h * LANES:(h + 1) * LANES]
        vf_ref[pl.ds(h, tm, stride=X_HEADS), :] = z[:, X_WIDTH + h * LANES:X_WIDTH + (h + 1) * LANES]


def _memkv(mem, g, w, tm):
    m = mem.shape[0]
    row = lambda wd: pl.BlockSpec((tm, wd), lambda i: (i, 0))
    flat = pl.BlockSpec((tm * X_HEADS, X_HEAD_DIM), lambda i: (i, 0))
    return pl.pallas_call(
        _memkv_kernel,
        grid=(m // tm,),
        in_specs=[row(D_MODEL), _resident((1, D_MODEL)), _resident(w.shape)],
        out_specs=[row(X_WIDTH), row(X_WIDTH), flat, flat],
        out_shape=[jax.ShapeDtypeStruct((m, X_WIDTH), F32)] * 2
        + [jax.ShapeDtypeStruct((m * X_HEADS, X_HEAD_DIM), F32)] * 2,
        compiler_params=_cparams("parallel"),
        name="mem_kv",
    )(mem, g, w)


def _softmax_rows(s):
    e = jnp.exp(s - jnp.max(s, axis=-1, keepdims=True))
    return e * (1.0 / jnp.sum(e, axis=-1, keepdims=True))


def _xattn_sample_kernel(hq_ref, mk_ref, mv_ref, o_ref, *, nb):
    rows = hq_ref.shape[1]
    nmh = mk_ref.shape[1]
    qh = lax.broadcasted_iota(jnp.int32, (X_HEADS * rows, nmh), 0) // rows
    mh = lax.broadcasted_iota(jnp.int32, (X_HEADS * rows, nmh), 1) % X_HEADS
    same = qh == mh
    seqs = range(nb)
    qs = [jnp.concatenate([hq_ref[b, :, h * LANES:(h + 1) * LANES] for h in range(X_HEADS)], axis=0)
          for b in seqs]
    ss = [_dot_nt(qs[b], mk_ref[b].astype(BF16)) * (X_HEAD_DIM ** -0.5) for b in seqs]
    ps = [_softmax_rows(jnp.where(same, s, -jnp.inf)).astype(BF16) for s in ss]
    os_ = [_dot(ps[b], mv_ref[b].astype(BF16)) for b in seqs]
    for b in seqs:
        for h in range(X_HEADS):
            o_ref[b, :, h * LANES:(h + 1) * LANES] = os_[b][h * rows:(h + 1) * rows].astype(BF16)


def _xattn_sample(hq, mk, mv, nb=8):
    bsz, rows, _ = hq.shape
    q = pl.BlockSpec((nb, rows, X_WIDTH), lambda i: (i, 0, 0))
    mem = pl.BlockSpec((nb, mk.shape[1], X_HEAD_DIM), lambda i: (i, 0, 0))
    return pl.pallas_call(
        functools.partial(_xattn_sample_kernel, nb=nb),
        grid=(bsz // nb,),
        in_specs=[q, mem, mem],
        out_specs=q,
        out_shape=jax.ShapeDtypeStruct(hq.shape, BF16),
        compiler_params=_cparams("parallel"),
        name="xattn_sample",
    )(hq, mk, mv)


def _xattn_tile(hq_ref, mk_ref, mv_ref):
    outs = []
    for h in range(X_HEADS):
        hs = slice(h * LANES, (h + 1) * LANES)
        s = _dot_nt(hq_ref[:, hs], mk_ref[:, hs].astype(BF16)) * (X_HEAD_DIM ** -0.5)
        outs.append(_dot(_softmax_rows(s).astype(BF16), mv_ref[:, hs].astype(BF16)).astype(BF16))
    return jnp.concatenate(outs, axis=1)


def _ffn_kernel(h_ref, *rest, stride, padc, fc, attend):
    if attend:
        hq_ref, mk_ref, mv_ref, *rest = rest
        xo = _xattn_tile(hq_ref, mk_ref, mv_ref)
    else:
        xo_ref, *rest = rest
        xo = xo_ref[...]
    cin_ref, wxo_ref, g_ref, wu_ref, wv_ref, cw_ref, wd_ref, fg_ref, y_ref, tail_ref, xp_ref, acc_ref = rest
    tm = h_ref.shape[0]

    @pl.when(pl.program_id(1) == 0)
    def _():
        tail_ref[...] = cin_ref[...]

    h = h_ref[...] + _dot(xo, wxo_ref[...])
    acc_ref[...] = h
    hn = _rms(h, g_ref[...]).astype(BF16)
    for lo, hi in zip((0,) + fc, fc + (D_FF,)):
        cs = slice(lo, hi)
        n = hi - lo
        u = _dot(hn, wu_ref[:, cs])
        gate = _dot(hn, wv_ref[:, cs])
        xp_ref[0:padc, 0:n] = tail_ref[0, :, cs]
        xp_ref[padc:padc + tm, 0:n] = u
        cw = cw_ref[:, cs]
        uc = u * cw[2:3]
        for j in range(FFN_CONV - 1):
            o = padc - (FFN_CONV - 1 - j) * stride
            uc = uc + xp_ref[o:o + tm, 0:n] * cw[j:j + 1]
        tail_ref[0, :, cs] = xp_ref[tm:tm + padc, 0:n]
        acc_ref[...] += _dot((_silu(uc) * gate).astype(BF16), wd_ref[cs, :])
    y_ref[...] = _rms(acc_ref[...], fg_ref[...])


FFN_CUTS = ()


def _ffn(h, xq, mem, cin, wxo, g, wup, cw, wd, fg, tm, stride, fc=FFN_CUTS):
    m = h.shape[0]
    half = lambda c: pl.BlockSpec((D_MODEL, D_FF), lambda *_: (0, c), pipeline_mode=pl.Buffered(1))
    nseq, padc, _ = cin.shape
    nt = m // nseq // tm
    row = lambda w: pl.BlockSpec((tm, w), lambda s, j: (s * nt + j, 0))
    car = pl.BlockSpec((1, padc, D_FF), lambda s, j: (s, 0, 0))
    mem = list(mem or ())
    mem_specs = [pl.BlockSpec((a.shape[0] // nseq, X_WIDTH), lambda s, j: (s, 0)) for a in mem]
    return pl.pallas_call(
        functools.partial(_ffn_kernel, stride=stride, padc=padc, fc=fc, attend=bool(mem)),
        grid=(nseq, nt),
        in_specs=[row(D_MODEL), row(X_WIDTH), *mem_specs, car, _resident(wxo.shape), _resident((1, D_MODEL)),
                  half(0), half(1), _resident(cw.shape), _resident(wd.shape), _resident((1, D_MODEL))],
        out_specs=[row(D_MODEL), car],
        out_shape=[jax.ShapeDtypeStruct((m, D_MODEL), F32), jax.ShapeDtypeStruct(cin.shape, F32)],
        scratch_shapes=[pltpu.VMEM((tm + padc, max(b - a for a, b in zip((0,) + fc, fc + (D_FF,)))), F32),
                        pltpu.VMEM((tm, D_MODEL), F32)],
        compiler_params=_cparams("parallel", "arbitrary"),
        name="ffn",
    )(h, xq, *mem, cin, wxo, g, wup, wup, cw, wd, fg)


def _lane_row(vec, offset=0):
    return jnp.zeros((1, LANES), F32).at[0, offset:offset + vec.shape[0]].set(vec.astype(F32))


W_IN_QKV = A_WIDTH + 2 * A_KV_WIDTH
W_IN_D = W_IN_QKV + 4 * DN_WIDTH
W_IN_AB = W_IN_D + 2 * DN_HEADS
W_IN_G = W_IN_AB + 2 * D_MODEL


def _split_w_in_kernel(wt_ref, qkv_ref, d_ref, ab_ref, g_ref):
    tk = wt_ref.shape[1]
    qkv_ref[...] = wt_ref[:W_IN_QKV, :].T.astype(BF16)
    d_ref[...] = wt_ref[W_IN_QKV:W_IN_D, :].T.astype(BF16)
    ab = jnp.concatenate([wt_ref[W_IN_D:W_IN_AB, :], jnp.zeros((LANES - (W_IN_AB - W_IN_D), tk), F32)], axis=0)
    ab_ref[...] = ab.T.astype(BF16)
    g_ref[...] = wt_ref[W_IN_AB:W_IN_G, :].T.astype(BF16)


def _split_w_in(w_in_t, tk=256):
    k = w_in_t.shape[1]
    widths = (W_IN_QKV, W_IN_D - W_IN_QKV, LANES, W_IN_G - W_IN_AB)
    return pl.pallas_call(
        _split_w_in_kernel,
        grid=(k // tk,),
        in_specs=[pl.BlockSpec((w_in_t.shape[0], tk), lambda i: (0, i))],
        out_specs=[pl.BlockSpec((tk, wd), lambda i: (i, 0)) for wd in widths],
        out_shape=[jax.ShapeDtypeStruct((k, wd), BF16) for wd in widths],
        compiler_params=_cparams("parallel"),
        name="split_w_in",
    )(w_in_t)


def _prep(p):
    wqkv, wd, wab, wg = _split_w_in(p["w_in"].T)
    w = {
        "wqkv": wqkv,
        "wd": wd,
        "wab": wab,
        "wg": wg,
        "norm_mix_g": p["norm_mix_g"].reshape(1, D_MODEL),
        "dn_conv_w": p["dn_conv_w"],
        "alog": _lane_row(p["dn_a_log"]),
        "dtb": _lane_row(p["dn_dt_bias"]),
        "dn_norm_g": p["dn_norm_g"].reshape(1, DN_HEAD_DIM),
        "sinks": p["sinks"].astype(F32),
        "w_br_a": p["w_br_a"].astype(BF16),
        "w_br_b": p["w_br_b"].astype(BF16),
        "w_mix_out": p["w_mix_out"].astype(BF16),
        "norm_x_g": p["norm_x_g"].reshape(1, D_MODEL),
        "w_xq": p["w_xq"].astype(BF16),
        "w_xo": p["w_xo"].astype(BF16),
        "norm_ffn_g": p["norm_ffn_g"].reshape(1, D_MODEL),
        "w_up": p["w_up"].astype(BF16),
        "ffn_conv_w": p["ffn_conv_w"],
        "w_down": p["w_down"].astype(BF16),
        "final_norm_g": p["final_norm_g"].reshape(1, D_MODEL),
    }
    return w


def _tile(m, pref):
    return pref if m % pref == 0 else m


def _prompt_layer(x, mem, norm_mem_g, w_xkv, w):
    bsz, seqlen, _ = x.shape
    m = bsz * seqlen
    x2 = x.reshape(m, D_MODEL)
    tm = _tile(seqlen, ROW_TILE)
    tabs = _rope_tables(jnp.arange(seqlen, dtype=jnp.int32))
    assert seqlen >= WINDOW, "the window outputs are taken from a full last window"
    q, k, v, ga, gb, kwin, vwin, yb, dn_state, dn_tail = _proj_gdn(
        x2, w["norm_mix_g"], tabs, w["wqkv"], w["wd"], w["wab"], w["wg"], w["dn_conv_w"], w["alog"], w["dtb"],
        w["dn_norm_g"], bsz, tm)
    h, hq = _merge(x2, (w["sinks"], q, k, v), yb, ga, gb, w["w_br_a"], w["w_br_b"], w["w_mix_out"],
                   w["norm_x_g"], w["w_xq"], bsz, tm)
    nm = mem.shape[1]
    mk, mv, mk_rows, mv_rows = _memkv(mem.reshape(bsz * nm, D_MODEL), norm_mem_g.reshape(1, D_MODEL),
                                      w_xkv.astype(BF16), _tile(bsz * nm, ROW_TILE))
    cin = jnp.zeros((bsz, SUBLANES, D_FF), F32)
    y, tail = _ffn(h, hq, (mk, mv), cin, w["w_xo"], w["norm_ffn_g"], w["w_up"], w["ffn_conv_w"],
                   w["w_down"], w["final_norm_g"], tm, 1)
    win = lambda a: jnp.transpose(a.reshape(bsz, A_KV_HEADS, A_HEAD_DIM, WINDOW), (0, 3, 1, 2))
    new = (
        win(kwin),
        win(vwin),
        dn_tail[:, SUBLANES - (DN_CONV - 1):],
        dn_state,
        mk_rows.reshape(bsz, nm, X_HEADS, X_HEAD_DIM),
        mv_rows.reshape(bsz, nm, X_HEADS, X_HEAD_DIM),
        tail[:, SUBLANES - (FFN_CONV - 1):],
    )
    return y.reshape(bsz, seqlen, D_MODEL), new


def _sample_layer(x, pos0, win_k, win_v, dn_buf, dn_state, mem_k, mem_v, ffn_buf, w):
    bsz, t, _ = x.shape
    m = bsz * t
    sr = SAMPLE_ROWS
    x2 = x.reshape(m, D_MODEL)
    tabs = _rope_tables(jnp.tile(pos0 + jnp.arange(t, dtype=jnp.int32), bsz))
    q, k, v, dqkv, dz, ab, ga, gb = _proj(x2, w["norm_mix_g"], tabs, w["wqkv"], w["wd"], w["wab"], w["wg"], m)

    wlen = win_k.shape[1]
    qh = jnp.transpose(q.reshape(bsz, t, A_HEADS, A_HEAD_DIM), (0, 2, 1, 3)).reshape(bsz, A_HEADS * t, A_HEAD_DIM)
    padrows = lambda a: jnp.pad(a.reshape(bsz, t, -1), ((0, 0), (0, sr - t), (0, 0)))
    newrows = lambda a: jnp.pad(a.reshape(bsz, t * A_KV_HEADS, A_HEAD_DIM),
                                ((0, 0), (0, sr - t * A_KV_HEADS), (0, 0))).astype(BF16)
    kv_major = lambda a: jnp.transpose(a, (0, 2, 3, 1))
    nbw = min(bsz, LANES // t)

    def new_t(a):
        a = jnp.transpose(a.reshape(bsz // nbw, nbw * t, A_KV_HEADS, A_HEAD_DIM), (0, 2, 3, 1))
        return jnp.pad(a, ((0, 0), (0, 0), (0, 0), (0, LANES - nbw * t)))

    oh, new_wk, new_wv = _swa_sample(qh, kv_major(win_k), kv_major(win_v), newrows(k), newrows(v),
                                     new_t(k), new_t(v), w["sinks"], t)
    ya = jnp.transpose(oh.reshape(bsz, A_HEADS, t, A_HEAD_DIM), (0, 2, 1, 3)).reshape(m, A_WIDTH)
    new_wk, new_wv = (jnp.transpose(a, (0, 3, 1, 2)) for a in (new_wk, new_wv))

    dqkv3 = dqkv.reshape(bsz, t, 3 * DN_WIDTH)
    yb, new_s = _gdn_sample(dn_buf, dqkv3, dz.reshape(bsz, t, DN_WIDTH), ab.reshape(bsz, t, LANES), dn_state,
                            w["dn_conv_w"], w["alog"], w["dtb"], w["dn_norm_g"])
    yb = yb.reshape(m, DN_WIDTH)
    new_dn_buf = jnp.concatenate([dn_buf, dqkv3], axis=1)[:, -(DN_CONV - 1):]

    h, hq = _merge(x2, ya, yb, ga, gb, w["w_br_a"], w["w_br_b"], w["w_mix_out"], w["norm_x_g"], w["w_xq"], 1, m)
    nm = mem_k.shape[1]
    xo = _xattn_sample(padrows(hq), mem_k.reshape(bsz, nm * X_HEADS, X_HEAD_DIM),
                       mem_v.reshape(bsz, nm * X_HEADS, X_HEAD_DIM))
    xo = xo[:, :t]

    tmaj = lambda a: jnp.transpose(a.reshape(bsz, t, -1), (1, 0, 2)).reshape(m, -1)
    cin = jnp.transpose(ffn_buf, (1, 0, 2)).reshape(1, (FFN_CONV - 1) * bsz, D_FF)
    y, tail = _ffn(tmaj(h), tmaj(xo), None, cin, w["w_xo"], w["norm_ffn_g"], w["w_up"], w["ffn_conv_w"],
                   w["w_down"], w["final_norm_g"], m, bsz)
    y = jnp.transpose(y.reshape(t, bsz, D_MODEL), (1, 0, 2))
    new_ffn = jnp.transpose(tail.reshape(FFN_CONV - 1, bsz, D_FF), (1, 0, 2))
    return y, (new_wk, new_wv, new_dn_buf, new_s, new_ffn)


def kernel(x_prompt, x_sample, mem_prompt, cache_win_k, cache_win_v, state_dn_conv, state_dn, cache_mem_k, cache_mem_v, state_ffn_conv, norm_mix_g, w_in, dn_conv_w, dn_a_log, dn_dt_bias, dn_norm_g, attn_sinks, w_br_a, w_br_b, w_mix_out, norm_x_g, norm_mem_g, w_xq, w_xkv, w_xo, norm_ffn_g, w_up, ffn_conv_w, w_down, final_norm_g):
    p = {"norm_mix_g": norm_mix_g[0], "w_in": w_in[0], "dn_conv_w": dn_conv_w[0], "dn_a_log": dn_a_log[0],
         "dn_dt_bias": dn_dt_bias[0], "dn_norm_g": dn_norm_g[0], "sinks": attn_sinks[0], "w_br_a": w_br_a[0],
         "w_br_b": w_br_b[0], "w_mix_out": w_mix_out[0], "norm_x_g": norm_x_g[0], "w_xq": w_xq[0],
         "w_xo": w_xo[0], "norm_ffn_g": norm_ffn_g[0], "w_up": w_up[0], "ffn_conv_w": ffn_conv_w[0],
         "w_down": w_down[0], "final_norm_g": final_norm_g}
    w = _prep(p)
    yp, newp = _prompt_layer(x_prompt, mem_prompt, norm_mem_g[0], w_xkv[0], w)
    ys, news = _sample_layer(x_sample, PAST_LEN, cache_win_k[0], cache_win_v[0], state_dn_conv[0], state_dn[0],
                             cache_mem_k[0], cache_mem_v[0], state_ffn_conv[0], w)
    lead = lambda a: a[None]
    p_win_k, p_win_v, p_dn_conv, p_dn_state, p_mem_k, p_mem_v, p_ffn_conv = [lead(a) for a in newp]
    s_win_k, s_win_v, s_dn_conv, s_dn_state, s_ffn_conv = [lead(a) for a in news]
    return (yp, ys, p_win_k, p_win_v, p_dn_conv, p_dn_state, p_mem_k, p_mem_v, p_ffn_conv,
            s_win_k, s_win_v, s_dn_conv, s_dn_state, s_ffn_conv)
```

```python
import functools

import jax
import jax.numpy as jnp
from jax import lax
from jax.experimental import pallas as pl
from jax.experimental.pallas import tpu as pltpu

F32 = jnp.float32
BF16 = jnp.bfloat16

D_MODEL = 1024
A_HEADS = 8
A_KV_HEADS = 2
A_HEAD_DIM = 64
A_WIDTH = 512
A_KV_WIDTH = 128
WINDOW = 128
ROT_DIM = 16
ROPE_THETA = 500000.0
DN_HEADS = 4
DN_HEAD_DIM = 128
DN_WIDTH = 512
DN_CONV = 4
X_HEADS = 4
X_HEAD_DIM = 128
X_WIDTH = 512
D_FF = 2816
FFN_CONV = 3
EPS = 1e-6
PAST_LEN = 16384

LANES = 128
SUBLANES = 8
VMEM_LIMIT = 56 * 1024 * 1024
ROW_TILE = 512
CHUNK = 128
SAMPLE_ROWS = 16
GDN_SAMPLE_ROWS = 8


def _cparams(*sem):
    return pltpu.CompilerParams(dimension_semantics=sem, vmem_limit_bytes=VMEM_LIMIT)


def _resident(shape):
    return pl.BlockSpec(shape, lambda *_: (0,) * len(shape), pipeline_mode=pl.Buffered(1))


def _rms(x, g):
    return x * lax.rsqrt(jnp.mean(x * x, axis=-1, keepdims=True) + EPS) * g


def _dot(a, b):
    return jnp.dot(a, b, preferred_element_type=F32)


def _dot_nt(a, b):
    return lax.dot_general(a, b, (((1,), (1,)), ((), ())), preferred_element_type=F32)


def _silu(x):
    return x * jax.nn.sigmoid(x)


def _rope(seg, c, s1, s2):
    return seg * c + pltpu.roll(seg, LANES - 8, 1) * s1 + pltpu.roll(seg, 8, 1) * s2


PROJ_COLS = 512
assert (3 * DN_WIDTH) % PROJ_COLS == 0 and D_MODEL % PROJ_COLS == 0


def _proj_tile(x_ref, g_ref, c_ref, s1_ref, s2_ref, wqkv_ref, wd_ref, wab_ref, wg_ref,
               q_ref, k_ref, v_ref, dqkv_ref, dz_ref, ab_ref, ga_ref, gb_ref, dqkv_row0=0, win_refs=None):
    tm = x_ref.shape[0]
    xb = _rms(x_ref[...], g_ref[...]).astype(BF16)
    yield
    c, s1, s2 = c_ref[...], s1_ref[...], s2_ref[...]
    z = _dot(xb, wqkv_ref[...])
    for i in range(A_WIDTH // LANES):
        sl = slice(i * LANES, (i + 1) * LANES)
        q_ref[:, sl] = _rope(z[:, sl], c, s1, s2).astype(BF16)
    k = _rope(z[:, A_WIDTH:A_WIDTH + LANES], c, s1, s2)
    k_ref[...] = k
    v_ref[...] = z[:, A_WIDTH + LANES:]
    if win_refs is not None:
        win_refs[0][0] = k[tm - WINDOW:].T
        win_refs[1][0] = z[tm - WINDOW:, A_WIDTH + LANES:].T
    yield
    pc = PROJ_COLS
    for j in range(3 * DN_WIDTH // pc):
        dqkv_ref[dqkv_row0:dqkv_row0 + tm, j * pc:(j + 1) * pc] = _dot(xb, wd_ref[:, j * pc:(j + 1) * pc])
        yield
    dz_ref[...] = _dot(xb, wd_ref[:, 3 * DN_WIDTH:])
    ab_ref[...] = _dot(xb, wab_ref[...])
    yield
    for dst, base in ((ga_ref, 0), (gb_ref, D_MODEL)):
        for j in range(D_MODEL // pc):
            dst[:, j * pc:(j + 1) * pc] = _dot(xb, wg_ref[:, base + j * pc:base + (j + 1) * pc])
            yield


def _proj_kernel(*refs):
    _drain(_proj_tile(*refs))


def _proj(x, g, tabs, wqkv, wd, wab, wg, tm):
    m = x.shape[0]
    nt = tabs[0].shape[0] // tm
    row = lambda w: pl.BlockSpec((tm, w), lambda i: (i, 0))
    tab = pl.BlockSpec((tm, LANES), lambda i: (i % nt, 0))
    widths = (A_WIDTH, LANES, LANES, 3 * DN_WIDTH, DN_WIDTH, LANES, D_MODEL, D_MODEL)
    dts = (BF16, F32, F32, F32, F32, F32, F32, F32)
    return pl.pallas_call(
        _proj_kernel,
        grid=(m // tm,),
        in_specs=[row(D_MODEL), _resident((1, D_MODEL)), tab, tab, tab,
                  _resident(wqkv.shape), _resident(wd.shape), _resident(wab.shape), _resident(wg.shape)],
        out_specs=[row(w) for w in widths],
        out_shape=[jax.ShapeDtypeStruct((m, w), d) for w, d in zip(widths, dts)],
        compiler_params=_cparams("parallel"),
        name="in_proj",
    )(x, g, *tabs, wqkv, wd, wab, wg)


def _rope_tables(pos):
    half = ROT_DIM // 2
    d = jnp.arange(LANES, dtype=jnp.int32) % A_HEAD_DIM
    inv = ROPE_THETA ** (-2.0 * (d % half).astype(F32) / ROT_DIM)
    ang = pos.astype(F32)[:, None] * inv[None, :]
    c, s = jnp.cos(ang), jnp.sin(ang)
    return (jnp.where(d < ROT_DIM, c, 1.0), jnp.where(d < half, -s, 0.0),
            jnp.where((d >= half) & (d < ROT_DIM), s, 0.0))


def _both_halves(t, lane_lo):
    r = pltpu.roll(t, A_HEAD_DIM, 1)
    return jnp.where(lane_lo, t, r), jnp.where(lane_lo, r, t)


def _sink_softmax(s, valid, sink):
    s = jnp.where(valid, s, -jnp.inf)
    m = jnp.maximum(jnp.max(s, axis=-1, keepdims=True), sink)
    p = jnp.exp(s - m)
    den = jnp.sum(p, axis=-1, keepdims=True) + jnp.exp(sink - m)
    return p * (1.0 / den)


def _swa_tile(sink_ref, q_ref, k, v, first):
    w = WINDOW
    lane_lo_k = lax.broadcasted_iota(jnp.int32, k.shape, 1) < A_HEAD_DIM
    kk = [t.astype(BF16) for t in _both_halves(k, lane_lo_k)]
    vv = [t.astype(BF16) for t in _both_halves(v, lane_lo_k)]
    lane_lo = lax.broadcasted_iota(jnp.int32, (w, LANES), 1) < A_HEAD_DIM
    row = lax.broadcasted_iota(jnp.int32, (4 * w, 2 * w), 0)
    col = lax.broadcasted_iota(jnp.int32, (4 * w, 2 * w), 1)
    d = (row & (w - 1)) + w - col
    band = (d >= 0) & (d < w)
    band_first = band & (col >= jnp.where(first, w, 0))
    hrow = lax.broadcasted_iota(jnp.int32, (4 * w, 1), 0) // w
    sinks = []
    for g in range(A_KV_HEADS):
        sink = jnp.zeros((4 * w, 1), F32)
        for j in range(4):
            sink = jnp.where(hrow == j, sink_ref[4 * g + j], sink)
        sinks.append(sink)
    zero = jnp.zeros((), BF16)
    scale = jnp.asarray(A_HEAD_DIM ** -0.5, BF16)
    nblk = q_ref.shape[0] // w
    probs = [(b, g) for b in range(nblk) for g in range(A_KV_HEADS)]

    def queries(b, g):
        parts = []
        for sgm in range(2):
            seg = q_ref[b * w:(b + 1) * w, (2 * g + sgm) * LANES:(2 * g + sgm + 1) * LANES] * scale
            parts += [jnp.where(lane_lo, seg, zero), jnp.where(lane_lo, zero, seg)]
        return jnp.concatenate(parts, axis=0)

    ss = [_dot_nt(queries(b, g), kk[g][b * w:(b + 2) * w]) for b, g in probs]
    yield
    ps = [_sink_softmax(s, band_first if b == 0 else band, sinks[g]).astype(BF16) for s, (b, g) in zip(ss, probs)]
    yield
    os_ = [_dot(p, vv[g][b * w:(b + 2) * w]) for p, (b, g) in zip(ps, probs)]
    yield
    segs = [[jnp.where(lane_lo, o[(2 * sgm) * w:(2 * sgm + 1) * w], o[(2 * sgm + 1) * w:(2 * sgm + 2) * w]
                       ).astype(BF16) for sgm in range(2)] for o in os_]
    return jnp.concatenate([jnp.concatenate(segs[A_KV_HEADS * b] + segs[A_KV_HEADS * b + 1], axis=1)
                            for b in range(nblk)], axis=0)


def _swa_sample_kernel(sink_ref, q_ref, ckt_ref, cvt_ref, knt_ref, vnt_ref, o_ref, cko_ref, cvo_ref, *, nb, t):
    w = ckt_ref.shape[3]
    old = lax.broadcasted_iota(jnp.int32, (A_HEAD_DIM, w), 1) < w - t
    for src_ref, new_ref, dst_ref in ((ckt_ref, knt_ref, cko_ref), (cvt_ref, vnt_ref, cvo_ref)):
        for g in range(A_KV_HEADS):
            new = new_ref[0, g]
            for b in range(nb):
                dst_ref[b, g] = jnp.where(old, pltpu.roll(src_ref[b, g], w - t, 1),
                                          pltpu.roll(new, (w - t - t * b) % LANES, 1))
    rows = q_ref.shape[1] // A_HEADS
    gq = rows * (A_HEADS // A_KV_HEADS)
    r = lax.broadcasted_iota(jnp.int32, (gq, w), 0)
    c = lax.broadcasted_iota(jnp.int32, (gq, w), 1)
    valid_c = c > r % rows
    rn = lax.broadcasted_iota(jnp.int32, (gq, LANES), 0)
    cn = lax.broadcasted_iota(jnp.int32, (gq, LANES), 1)
    causal_n = cn % t <= rn % rows
    valid_n = [causal_n & (cn // t == b) for b in range(nb)]
    hrow = lax.broadcasted_iota(jnp.int32, (gq, 1), 0) // rows
    sink = []
    for g in range(A_KV_HEADS):
        sk = jnp.zeros((gq, 1), F32)
        for j in range(A_HEADS // A_KV_HEADS):
            sk = jnp.where(hrow == j, sink_ref[g * (A_HEADS // A_KV_HEADS) + j], sk)
        sink.append(sk)
    probs = [(b, g) for b in range(nb) for g in range(A_KV_HEADS)]
    scale = A_HEAD_DIM ** -0.5
    qs = [q_ref[b, g * gq:(g + 1) * gq, :] for b, g in probs]
    scs = [jnp.where(valid_c, _dot(q, ckt_ref[b, g].astype(BF16)) * scale, -jnp.inf)
           for q, (b, g) in zip(qs, probs)]
    knt = [knt_ref[0, g].astype(BF16) for g in range(A_KV_HEADS)]
    vnt = [vnt_ref[0, g].astype(BF16) for g in range(A_KV_HEADS)]
    sns = [jnp.where(valid_n[b], _dot(q, knt[g]) * scale, -jnp.inf) for q, (b, g) in zip(qs, probs)]
    ms = [jnp.maximum(jnp.maximum(jnp.max(sc, -1, keepdims=True), jnp.max(sn, -1, keepdims=True)), sink[g])
          for sc, sn, (b, g) in zip(scs, sns, probs)]
    pcs = [jnp.exp(sc - m) for sc, m in zip(scs, ms)]
    pns = [jnp.exp(sn - m) for sn, m in zip(sns, ms)]
    invs = [1.0 / (jnp.sum(pc, -1, keepdims=True) + jnp.sum(pn, -1, keepdims=True) + jnp.exp(sink[g] - m))
            for pc, pn, m, (b, g) in zip(pcs, pns, ms, probs)]
    for pc, pn, inv, (b, g) in zip(pcs, pns, invs, probs):
        o_ref[b, g * gq:(g + 1) * gq, :] = (_dot_nt((pc * inv).astype(BF16), cvt_ref[b, g].astype(BF16))
                                            + _dot_nt((pn * inv).astype(BF16), vnt[g]))


def _swa_sample(q, ckt, cvt, knt, vnt, sinks, t):
    bsz, nq, _ = q.shape
    nb = bsz // knt.shape[0]
    assert nb * t <= LANES and ckt.shape[3] == LANES
    blk = lambda a, n=nb: pl.BlockSpec((n,) + a.shape[1:], lambda i: (i,) + (0,) * (a.ndim - 1))
    return pl.pallas_call(
        functools.partial(_swa_sample_kernel, nb=nb, t=t),
        grid=(bsz // nb,),
        in_specs=[pl.BlockSpec(memory_space=pltpu.SMEM), blk(q), blk(ckt), blk(cvt), blk(knt, 1), blk(vnt, 1)],
        out_specs=[blk(q), blk(ckt), blk(cvt)],
        out_shape=[jax.ShapeDtypeStruct(q.shape, F32), jax.ShapeDtypeStruct(ckt.shape, F32),
                   jax.ShapeDtypeStruct(cvt.shape, F32)],
        compiler_params=_cparams("parallel"),
        name="swa_sample",
    )(sinks, q, ckt, cvt, knt, vnt)


def _lane_bcast(x, lane):
    return jnp.broadcast_to(x[:, lane:lane + 1], (x.shape[0], LANES))


def _cumsum_rows(x, block):
    c = x.shape[0]
    rowi = lax.broadcasted_iota(jnp.int32, (c, c), 0)
    coli = lax.broadcasted_iota(jnp.int32, (c, c), 1)
    ones = (((rowi // block) == (coli // block)) & (rowi >= coli)).astype(BF16)
    hi = x.astype(BF16)
    r1 = x - hi.astype(F32)
    mid = r1.astype(BF16)
    lo = (r1 - mid.astype(F32)).astype(BF16)
    s = _dot(ones, jnp.concatenate([hi, mid, lo], axis=1))
    return s[:, :LANES] + s[:, LANES:2 * LANES] + s[:, 2 * LANES:]


def _l2n(t):
    return t * lax.rsqrt(jnp.sum(t * t, axis=-1, keepdims=True) + EPS)


def _gates(ab, alog, dtb):
    x = ab + dtb
    sp = jnp.maximum(x, 0.0) + jnp.log1p(jnp.exp(-jnp.abs(x)))
    return -jnp.exp(alog) * sp, jax.nn.sigmoid(ab)


def _merge_masks(c, top):
    rowi = lax.broadcasted_iota(jnp.int32, (c, c), 0)
    coli = lax.broadcasted_iota(jnp.int32, (c, c), 1)
    masks = []
    s = 1
    while s < top:
        rb, cb = rowi // s, coli // s
        masks.append(((rb // 2) == (cb // 2)) & ((rb % 2) == 1) & ((cb % 2) == 0))
        s *= 2
    return masks


def _each(f, *lists):
    return [f(*t) for t in zip(*lists)]


def _drain(stages):
    try:
        while True:
            next(stages)
    except StopIteration as done:
        return done.value


def _interleave(*staged):
    live = list(staged)
    values = {}
    while live:
        for item in tuple(live):
            stages, per_turn = item
            try:
                for _ in range(per_turn):
                    next(stages)
            except StopIteration as done:
                values[id(stages)] = done.value
                live.remove(item)
    return [values[id(stages)] for stages, _ in staged]


def _chunk_local(qs, ks, vs, gcols, grows, betas, tril, merges):
    c = qs[0].shape[0]
    decays = _each(lambda gc, gr: jnp.exp(jnp.minimum(gc - gr, 0.0)), gcols, grows)
    kbs = _each(lambda k, b: k * b, ks, betas)
    yield
    ms = _each(lambda q, kb, k: _dot_nt(jnp.concatenate([q, kb], axis=0).astype(BF16), k.astype(BF16)),
               qs, kbs, ks)
    yield
    qks = _each(lambda m, d: jnp.where(tril, m[:c] * d, 0.0), ms, decays)
    a = _each(lambda m, d: m[c:] * d, ms, decays)
    eye = (lax.broadcasted_iota(jnp.int32, (c, c), 0) == lax.broadcasted_iota(jnp.int32, (c, c), 1)).astype(F32)
    ts = [eye - jnp.where(merges[0], x, 0.0) if merges else eye for x in a]
    yield
    for off in merges[1:]:
        tbs = [t.astype(BF16) for t in ts]
        zs = _each(lambda x, tb: _dot(jnp.where(off, x, 0.0).astype(BF16), tb), a, tbs)
        yield
        ts = _each(lambda t, tb, z: t - _dot(tb, z.astype(BF16)), ts, tbs, zs)
        yield
    ns = [t - eye for t in ts]
    egs = [jnp.exp(gc) for gc in gcols]
    rhss = _each(lambda v, b, kb, eg: jnp.concatenate([v * b, kb * eg], axis=1), vs, betas, kbs, egs)
    yield
    uws = _each(lambda r, n: r + _dot(n.astype(BF16), r.astype(BF16)), rhss, ns)
    yield
    return ([x[:, :LANES] for x in uws], [x[:, LANES:] for x in uws], qks,
            _each(lambda q, eg: q * eg, qs, egs))


def _gdn_tile(xp_ref, dz_ref, ab_ref, cw_ref, alog_ref, dtb_ref, ng_ref, y_ref, s_ref, nc):
    c = CHUNK
    pad = SUBLANES
    cw = cw_ref[...]
    g, beta = _gates(ab_ref[...], alog_ref[...], dtb_ref[...])
    rowi = lax.broadcasted_iota(jnp.int32, (c, c), 0)
    coli = lax.broadcasted_iota(jnp.int32, (c, c), 1)
    tril = rowi >= coli
    merges = _merge_masks(c, c)
    ng = ng_ref[...]
    heads = range(DN_HEADS)
    probs = [(ci, h) for ci in range(nc) for h in heads]
    rows = lambda ci: slice(ci * c, (ci + 1) * c)
    lanes = lambda part, h: slice(part * DN_WIDTH + h * LANES, part * DN_WIDTH + (h + 1) * LANES)
    yield
    qs, ks, vs, gcols, grows, betas = [], [], [], [], [], []
    for ci in range(nc):
        conv = xp_ref[pad + ci * c:pad + (ci + 1) * c, :] * cw[DN_CONV - 1:DN_CONV]
        for j in range(DN_CONV - 1):
            o = pad - (DN_CONV - 1) + j + ci * c
            conv = conv + xp_ref[o:o + c, :] * cw[j:j + 1]
        conv = _silu(conv)
        gc = _cumsum_rows(g[rows(ci)], c)
        gct = gc.T
        qs += [_l2n(conv[:, lanes(0, h)]) * (DN_HEAD_DIM ** -0.5) for h in heads]
        ks += [_l2n(conv[:, lanes(1, h)]) for h in heads]
        vs += [conv[:, lanes(2, h)] for h in heads]
        gcols += [_lane_bcast(gc, h) for h in heads]
        grows += [gct[h:h + 1, :] for h in heads]
        betas += [_lane_bcast(beta[rows(ci)], DN_HEADS + h) for h in heads]
        yield
    us, ws, qks, qds = yield from _chunk_local(qs, ks, vs, gcols, grows, betas, tril, merges)
    glasts = [gc[c - 1:c, :] for gc in gcols]
    kdts = _each(lambda k, gl, gc: (k * jnp.exp(gl - gc)).T, ks, glasts, gcols)
    wqs = _each(lambda w, qd: jnp.concatenate([w, qd], axis=0).astype(BF16), ws, qds)
    qkks = _each(lambda qk, kdt: jnp.concatenate([qk, kdt], axis=0).astype(BF16), qks, kdts)
    yield
    ss = [s_ref[h] for h in heads]
    for ci in range(nc):
        pr = [ci * DN_HEADS + h for h in heads]
        r2s = [_dot(wqs[p], s.astype(BF16)) for p, s in zip(pr, ss)]
        vnews = [us[p] - r2[:c] for p, r2 in zip(pr, r2s)]
        yield
        r3s = [_dot(qkks[p], vn.astype(BF16)) for p, vn in zip(pr, vnews)]
        ss = [s * jnp.exp(glasts[p]) + r3[c:] for p, s, r3 in zip(pr, ss, r3s)]
        for h in heads:
            o = r2s[h][c:] + r3s[h][:c]
            y_ref[rows(ci), lanes(0, h)] = (_rms(o, ng) * _silu(dz_ref[rows(ci), lanes(0, h)])).astype(BF16)
        yield
    for h in heads:
        s_ref[h] = ss[h]


def _proj_gdn_kernel(x_ref, g_ref, c_ref, s1_ref, s2_ref, wqkv_ref, wd_ref, wab_ref, wg_ref,
                     cw_ref, alog_ref, dtb_ref, ng_ref,
                     q_ref, k_ref, v_ref, ga_ref, gb_ref, kw_ref, vw_ref, y_ref, sout_ref, tail_ref,
                     xp_ref, dz_ref, ab_ref, carry_ref, s_ref, *, nt, nc):
    i = pl.program_id(0)
    pad = SUBLANES
    r = nc * CHUNK
    slot_a = i % 2
    slot_b = 1 - slot_a

    @pl.when(i == 0)
    def _():
        xp_ref[1] = jnp.zeros(xp_ref.shape[1:], F32)
        dz_ref[1] = jnp.zeros(dz_ref.shape[1:], F32)
        ab_ref[1] = jnp.zeros(ab_ref.shape[1:], F32)

    @pl.when(jnp.maximum(i - 1, 0) % nt == 0)
    def _():
        carry_ref[...] = jnp.zeros_like(carry_ref)
        s_ref[...] = jnp.zeros_like(s_ref)

    xpb_ref = xp_ref.at[slot_b]
    xpb_ref[0:pad, :] = carry_ref[...]
    _interleave(
        (_gdn_tile(xpb_ref, dz_ref.at[slot_b], ab_ref.at[slot_b], cw_ref, alog_ref, dtb_ref, ng_ref,
                   y_ref, s_ref, nc), 1),
        (_proj_tile(x_ref, g_ref, c_ref, s1_ref, s2_ref, wqkv_ref, wd_ref, wab_ref, wg_ref, q_ref, k_ref, v_ref,
                    xp_ref.at[slot_a], dz_ref.at[slot_a], ab_ref.at[slot_a], ga_ref, gb_ref, pad,
                    (kw_ref, vw_ref)), 1))
    carry_ref[...] = xpb_ref[r:r + pad, :]
    tail_ref[0] = carry_ref[...]
    sout_ref[0] = s_ref[...]


def _proj_gdn(x, g, tabs, wqkv, wd, wab, wg, cw, alog, dtb, ng, nseq, tm):
    m = x.shape[0]
    nt = m // nseq // tm
    last = m // tm - 1
    nc = tm // CHUNK
    proj = lambda i: jnp.minimum(i, last)
    gdn = lambda i: jnp.maximum(i - 1, 0)
    prow = lambda w: pl.BlockSpec((tm, w), lambda i: (proj(i), 0))
    tab = pl.BlockSpec((tm, LANES), lambda i: (proj(i) % nt, 0))
    per_seq = lambda *dims: pl.BlockSpec((1,) + dims, lambda i: (gdn(i) // nt,) + (0,) * len(dims))
    win = pl.BlockSpec((1, A_KV_WIDTH, WINDOW), lambda i: (proj(i) // nt, 0, 0))
    win_shape = jax.ShapeDtypeStruct((nseq, A_KV_WIDTH, WINDOW), F32)
    state = (DN_HEADS, DN_HEAD_DIM, DN_HEAD_DIM)
    bufs = [pltpu.VMEM((2, tm + SUBLANES, 3 * DN_WIDTH), F32), pltpu.VMEM((2, tm, DN_WIDTH), F32),
            pltpu.VMEM((2, tm, LANES), F32)]
    return pl.pallas_call(
        functools.partial(_proj_gdn_kernel, nt=nt, nc=nc),
        grid=(m // tm + 1,),
        in_specs=[prow(D_MODEL), _resident((1, D_MODEL)), tab, tab, tab,
                  _resident(wqkv.shape), _resident(wd.shape), _resident(wab.shape), _resident(wg.shape),
                  _resident(cw.shape), _resident((1, LANES)), _resident((1, LANES)), _resident((1, LANES))],
        out_specs=[prow(A_WIDTH), prow(LANES), prow(LANES), prow(D_MODEL), prow(D_MODEL), win, win,
                   pl.BlockSpec((tm, DN_WIDTH), lambda i: (gdn(i), 0)), per_seq(*state),
                   per_seq(SUBLANES, 3 * DN_WIDTH)],
        out_shape=[jax.ShapeDtypeStruct((m, A_WIDTH), BF16), jax.ShapeDtypeStruct((m, LANES), F32),
                   jax.ShapeDtypeStruct((m, LANES), F32), jax.ShapeDtypeStruct((m, D_MODEL), F32),
                   jax.ShapeDtypeStruct((m, D_MODEL), F32), win_shape, win_shape,
                   jax.ShapeDtypeStruct((m, DN_WIDTH), BF16),
                   jax.ShapeDtypeStruct((nseq,) + state, F32),
                   jax.ShapeDtypeStruct((nseq, SUBLANES, 3 * DN_WIDTH), F32)],
        scratch_shapes=bufs + [pltpu.VMEM((SUBLANES, 3 * DN_WIDTH), F32), pltpu.VMEM(state, F32)],
        compiler_params=_cparams("arbitrary"),
        name="proj_gdn",
    )(x, g, *tabs, wqkv, wd, wab, wg, cw, alog, dtb, ng)


def _gdn_sample_kernel(buf_ref, x_ref, dz_ref, ab_ref, s0_ref, cw_ref, alog_ref, dtb_ref, ng_ref,
                       y_ref, sout_ref, bufo_ref, xp_ref, ab16_ref):
    t = x_ref.shape[1]
    c = CHUNK
    sr = GDN_SAMPLE_ROWS
    nb = c // sr
    pad = SUBLANES
    hist = DN_CONV - 1
    xp_ref[...] = jnp.zeros_like(xp_ref)
    ab16_ref[...] = jnp.zeros_like(ab16_ref)
    for b in range(nb):
        for j in range(hist):
            xp_ref[pad + b * sr - hist + j:pad + b * sr - hist + j + 1, :] = buf_ref[j, b:b + 1, :]
        xp_ref[pad + b * sr:pad + b * sr + t, :] = x_ref[b]
        ab16_ref[b * sr:b * sr + t, :] = ab_ref[b]
    for b in range(nb):
        for j in range(hist):
            o = pad + b * sr - hist + t + j
            bufo_ref[j, b:b + 1, :] = xp_ref[o:o + 1, :]
    cw = cw_ref[...]
    conv = xp_ref[pad:pad + c, :] * cw[hist:hist + 1]
    for j in range(hist):
        conv = conv + xp_ref[pad - hist + j:pad - hist + j + c, :] * cw[j:j + 1]
    conv = _silu(conv)
    live = (lax.broadcasted_iota(jnp.int32, (c, LANES), 0) % sr) < t
    g, beta = _gates(ab16_ref[...], alog_ref[...], dtb_ref[...])
    g = jnp.where(live, g, 0.0)
    beta = jnp.where(live, beta, 0.0)
    gc = _cumsum_rows(g, sr)
    gct = gc.T
    rowi = lax.broadcasted_iota(jnp.int32, (c, c), 0)
    coli = lax.broadcasted_iota(jnp.int32, (c, c), 1)
    tril = ((rowi // sr) == (coli // sr)) & (rowi >= coli)
    merges = _merge_masks(c, pl.next_power_of_2(t))
    ng = ng_ref[...]
    rowb = lax.broadcasted_iota(jnp.int32, (c, LANES), 0) // sr
    heads = range(DN_HEADS)
    lanes = lambda part, h: slice(part * DN_WIDTH + h * LANES, part * DN_WIDTH + (h + 1) * LANES)
    ks = [_l2n(conv[:, lanes(1, h)]) for h in heads]
    gcols = [_lane_bcast(gc, h) for h in heads]
    us, ws, qks, qds = _drain(_chunk_local(
        [_l2n(conv[:, lanes(0, h)]) * (DN_HEAD_DIM ** -0.5) for h in heads], ks,
        [conv[:, lanes(2, h)] for h in heads], gcols, [gct[h:h + 1, :] for h in heads],
        [_lane_bcast(beta, DN_HEADS + h) for h in heads], tril, merges))
    seqs = range(nb)
    rows = lambda b: slice(b * sr, (b + 1) * sr)
    s0s = [[s0_ref[b, h] for b in seqs] for h in heads]
    r2s = [[_dot(jnp.concatenate([ws[h][rows(b)], qds[h][rows(b)]], axis=0).astype(BF16), s0s[h][b].astype(BF16))
            for b in seqs] for h in heads]
    vnews = [jnp.concatenate([us[h][rows(b)] - r2s[h][b][:sr] for b in seqs], axis=0) for h in heads]
    os_ = [_rms(jnp.concatenate([r2s[h][b][sr:] for b in seqs], axis=0)
                + _dot(qks[h].astype(BF16), vnews[h].astype(BF16)), ng) for h in heads]
    for h in heads:
        for b in seqs:
            y_ref[b, :, lanes(0, h)] = os_[h][b * sr:b * sr + t] * _silu(dz_ref[b, :, lanes(0, h)])
    glasts = [jnp.concatenate([jnp.broadcast_to(gcols[h][(b + 1) * sr - 1:(b + 1) * sr, :], (sr, LANES))
                               for b in seqs], axis=0) for h in heads]
    kdts = [(ks[h] * jnp.exp(glasts[h] - gcols[h])).T.astype(BF16) for h in heads]
    for h in heads:
        for b in seqs:
            vb = jnp.where(rowb == b, vnews[h], 0.0).astype(BF16)
            sout_ref[b, h] = s0s[h][b] * jnp.exp(glasts[h][b * sr:b * sr + 1, :]) + _dot(kdts[h], vb)


def _gdn_sample(buf, x, dz, ab, s0, cw, alog, dtb, ng):
    bsz, t, _ = x.shape
    assert t + DN_CONV - 1 <= GDN_SAMPLE_ROWS, "too many new tokens for one row tile per sequence"
    nb = CHUNK // GDN_SAMPLE_ROWS
    seq = lambda a: pl.BlockSpec((nb,) + a.shape[1:], lambda i: (i,) + (0,) * (a.ndim - 1))
    hist = pl.BlockSpec((buf.shape[0], nb, buf.shape[2]), lambda i: (0, i, 0))
    return pl.pallas_call(
        _gdn_sample_kernel,
        grid=(bsz // nb,),
        in_specs=[hist, seq(x), seq(dz), seq(ab), seq(s0), _resident(cw.shape),
                  _resident((1, LANES)), _resident((1, LANES)), _resident((1, LANES))],
        out_specs=[seq(dz), seq(s0), hist],
        out_shape=[jax.ShapeDtypeStruct(dz.shape, F32), jax.ShapeDtypeStruct(s0.shape, F32),
                   jax.ShapeDtypeStruct(buf.shape, F32)],
        scratch_shapes=[pltpu.VMEM((CHUNK + SUBLANES, 3 * DN_WIDTH), F32), pltpu.VMEM((CHUNK, LANES), F32)],
        compiler_params=_cparams("parallel"),
        name="gdn_sample",
    )(buf, x, dz, ab, s0, cw, alog, dtb, ng)


def _merge_kernel(x_ref, *rest, attend):
    if attend:
        sink_ref, q_ref, kp_ref, kc_ref, vp_ref, vc_ref, *rest = rest
    else:
        ya_ref, *rest = rest
    yb_ref, ga_ref, gb_ref, wa_ref, wb_ref, wo_ref, g_ref, wq_ref, h_ref, hq_ref = rest

    def deltanet_half():
        yb = _dot(yb_ref[...].astype(BF16), wb_ref[...])
        yield
        gate_a = jax.nn.sigmoid(ga_ref[...])
        yield
        return gate_a, jax.nn.sigmoid(gb_ref[...]) * yb

    if attend:
        ya, (gate_a, mix_b) = _interleave(
            (_swa_tile(sink_ref, q_ref, jnp.concatenate([kp_ref[...], kc_ref[...]], axis=0),
                       jnp.concatenate([vp_ref[...], vc_ref[...]], axis=0), pl.program_id(1) == 0), 1),
            (deltanet_half(), 1))
    else:
        ya = ya_ref[...]
        gate_a, mix_b = _drain(deltanet_half())
    mix = gate_a * _dot(ya.astype(BF16), wa_ref[...]) + mix_b
    h = x_ref[...] + _dot(mix.astype(BF16), wo_ref[...])
    h_ref[...] = h
    hq_ref[...] = _dot(_rms(h, g_ref[...]).astype(BF16), wq_ref[...]).astype(BF16)


def _merge(x, attn, yb, ga, gb, wa, wb, wo, g, wq, nseq, tm):
    m = x.shape[0]
    nt = m // nseq // tm
    row = lambda w: pl.BlockSpec((tm, w), lambda s, j: (s * nt + j, 0))
    attend = isinstance(attn, tuple)
    if attend:
        sinks, q, k, v = attn
        per = tm // WINDOW
        prev = pl.BlockSpec((WINDOW, A_KV_WIDTH), lambda s, j: (jnp.maximum((s * nt + j) * per - 1, 0), 0))
        attn_args = [sinks, q, k, k, v, v]
        attn_specs = [pl.BlockSpec(memory_space=pltpu.SMEM), row(A_WIDTH), prev, row(A_KV_WIDTH),
                      prev, row(A_KV_WIDTH)]
    else:
        attn_args, attn_specs = [attn], [row(A_WIDTH)]
    return pl.pallas_call(
        functools.partial(_merge_kernel, attend=attend),
        grid=(nseq, nt),
        in_specs=[row(D_MODEL), *attn_specs, row(DN_WIDTH), row(D_MODEL), row(D_MODEL),
                  _resident(wa.shape), _resident(wb.shape), _resident(wo.shape),
                  _resident((1, D_MODEL)), _resident(wq.shape)],
        out_specs=[row(D_MODEL), row(X_WIDTH)],
        out_shape=[jax.ShapeDtypeStruct((m, D_MODEL), F32), jax.ShapeDtypeStruct((m, X_WIDTH), BF16)],
        compiler_params=_cparams("parallel", "parallel"),
        name="merge",
    )(x, *attn_args, yb, ga, gb, wa, wb, wo, g, wq)


def _memkv_kernel(m_ref, g_ref, w_ref, k_ref, v_ref, kf_ref, vf_ref):
    tm = m_ref.shape[0]
    z = _dot(_rms(m_ref[...], g_ref[...]).astype(BF16), w_ref[...])
    k_ref[...] = z[:, :X_WIDTH]
    v_ref[...] = z[:, X_WIDTH:]
    for h in range(X_HEADS):
        kf_ref[pl.ds(h, tm, stride=X_HEADS), :] = z[:, h * LANES:(h + 1) * LANES]
        vf_ref[pl.ds(h, tm, stride=X_HEADS), :] = z[:, X_WIDTH + h * LANES:X_WIDTH + (h + 1) * LANES]


def _memkv(mem, g, w, tm):
    m = mem.shape[0]
    row = lambda wd: pl.BlockSpec((tm, wd), lambda i: (i, 0))
    flat = pl.BlockSpec((tm * X_HEADS, X_HEAD_DIM), lambda i: (i, 0))
    return pl.pallas_call(
        _memkv_kernel,
        grid=(m // tm,),
        in_specs=[row(D_MODEL), _resident((1, D_MODEL)), _resident(w.shape)],
        out_specs=[row(X_WIDTH), row(X_WIDTH), flat, flat],
        out_shape=[jax.ShapeDtypeStruct((m, X_WIDTH), F32)] * 2
        + [jax.ShapeDtypeStruct((m * X_HEADS, X_HEAD_DIM), F32)] * 2,
        compiler_params=_cparams("parallel"),
        name="mem_kv",
    )(mem, g, w)


def _softmax_rows(s):
    e = jnp.exp(s - jnp.max(s, axis=-1, keepdims=True))
    return e * (1.0 / jnp.sum(e, axis=-1, keepdims=True))


def _xattn_sample_kernel(hq_ref, mk_ref, mv_ref, o_ref, *, nb):
    rows = hq_ref.shape[1]
    nmh = mk_ref.shape[1]
    qh = lax.broadcasted_iota(jnp.int32, (X_HEADS * rows, nmh), 0) // rows
    mh = lax.broadcasted_iota(jnp.int32, (X_HEADS * rows, nmh), 1) % X_HEADS
    same = qh == mh
    seqs = range(nb)
    qs = [jnp.concatenate([hq_ref[b, :, h * LANES:(h + 1) * LANES] for h in range(X_HEADS)], axis=0)
          for b in seqs]
    ss = [_dot_nt(qs[b], mk_ref[b].astype(BF16)) * (X_HEAD_DIM ** -0.5) for b in seqs]
    ps = [_softmax_rows(jnp.where(same, s, -jnp.inf)).astype(BF16) for s in ss]
    os_ = [_dot(ps[b], mv_ref[b].astype(BF16)) for b in seqs]
    for b in seqs:
        for h in range(X_HEADS):
            o_ref[b, :, h * LANES:(h + 1) * LANES] = os_[b][h * rows:(h + 1) * rows].astype(BF16)


def _xattn_sample(hq, mk, mv, nb=8):
    bsz, rows, _ = hq.shape
    q = pl.BlockSpec((nb, rows, X_WIDTH), lambda i: (i, 0, 0))
    mem = pl.BlockSpec((nb, mk.shape[1], X_HEAD_DIM), lambda i: (i, 0, 0))
    return pl.pallas_call(
        functools.partial(_xattn_sample_kernel, nb=nb),
        grid=(bsz // nb,),
        in_specs=[q, mem, mem],
        out_specs=q,
        out_shape=jax.ShapeDtypeStruct(hq.shape, BF16),
        compiler_params=_cparams("parallel"),
        name="xattn_sample",
    )(hq, mk, mv)


def _xattn_tile(hq_ref, mk_ref, mv_ref):
    heads = [slice(h * LANES, (h + 1) * LANES) for h in range(X_HEADS)]
    ss = [_dot_nt(hq_ref[:, hs], mk_ref[:, hs].astype(BF16)) * (X_HEAD_DIM ** -0.5) for hs in heads]
    ps = [_softmax_rows(s).astype(BF16) for s in ss]
    return jnp.concatenate([_dot(p, mv_ref[:, hs].astype(BF16)).astype(BF16) for p, hs in zip(ps, heads)], axis=1)


def _ffn_kernel(h_ref, *rest, stride, padc, fc, attend):
    if attend:
        hq_ref, mk_ref, mv_ref, *rest = rest
        xo = _xattn_tile(hq_ref, mk_ref, mv_ref)
    else:
        xo_ref, *rest = rest
        xo = xo_ref[...]
    cin_ref, wxo_ref, g_ref, wu_ref, wv_ref, cw_ref, wd_ref, fg_ref, y_ref, tail_ref, xp_ref, acc_ref = rest
    tm = h_ref.shape[0]

    @pl.when(pl.program_id(1) == 0)
    def _():
        tail_ref[...] = cin_ref[...]

    h = h_ref[...] + _dot(xo, wxo_ref[...])
    acc_ref[...] = h
    hn = _rms(h, g_ref[...]).astype(BF16)
    for lo, hi in zip((0,) + fc, fc + (D_FF,)):
        cs = slice(lo, hi)
        n = hi - lo
        u = _dot(hn, wu_ref[:, cs])
        gate = _dot(hn, wv_ref[:, cs])
        xp_ref[0:padc, 0:n] = tail_ref[0, :, cs]
        xp_ref[padc:padc + tm, 0:n] = u
        cw = cw_ref[:, cs]
        uc = u * cw[2:3]
        for j in range(FFN_CONV - 1):
            o = padc - (FFN_CONV - 1 - j) * stride
            uc = uc + xp_ref[o:o + tm, 0:n] * cw[j:j + 1]
        tail_ref[0, :, cs] = xp_ref[tm:tm + padc, 0:n]
        acc_ref[...] += _dot((_silu(uc) * gate).astype(BF16), wd_ref[cs, :])
    y_ref[...] = _rms(acc_ref[...], fg_ref[...])


FFN_CUTS = ()


def _ffn(h, xq, mem, cin, wxo, g, wup, cw, wd, fg, tm, stride, fc=FFN_CUTS):
    m = h.shape[0]
    half = lambda c: pl.BlockSpec((D_MODEL, D_FF), lambda *_: (0, c), pipeline_mode=pl.Buffered(1))
    nseq, padc, _ = cin.shape
    nt = m // nseq // tm
    row = lambda w: pl.BlockSpec((tm, w), lambda s, j: (s * nt + j, 0))
    car = pl.BlockSpec((1, padc, D_FF), lambda s, j: (s, 0, 0))
    mem = list(mem or ())
    mem_specs = [pl.BlockSpec((a.shape[0] // nseq, X_WIDTH), lambda s, j: (s, 0)) for a in mem]
    return pl.pallas_call(
        functools.partial(_ffn_kernel, stride=stride, padc=padc, fc=fc, attend=bool(mem)),
        grid=(nseq, nt),
        in_specs=[row(D_MODEL), row(X_WIDTH), *mem_specs, car, _resident(wxo.shape), _resident((1, D_MODEL)),
                  half(0), half(1), _resident(cw.shape), _resident(wd.shape), _resident((1, D_MODEL))],
        out_specs=[row(D_MODEL), car],
        out_shape=[jax.ShapeDtypeStruct((m, D_MODEL), F32), jax.ShapeDtypeStruct(cin.shape, F32)],
        scratch_shapes=[pltpu.VMEM((tm + padc, max(b - a for a, b in zip((0,) + fc, fc + (D_FF,)))), F32),
                        pltpu.VMEM((tm, D_MODEL), F32)],
        compiler_params=_cparams("parallel", "arbitrary"),
        name="ffn",
    )(h, xq, *mem, cin, wxo, g, wup, wup, cw, wd, fg)


def _lane_row(vec, offset=0):
    return jnp.zeros((1, LANES), F32).at[0, offset:offset + vec.shape[0]].set(vec.astype(F32))


W_IN_QKV = A_WIDTH + 2 * A_KV_WIDTH
W_IN_D = W_IN_QKV + 4 * DN_WIDTH
W_IN_AB = W_IN_D + 2 * DN_HEADS
W_IN_G = W_IN_AB + 2 * D_MODEL


def _split_w_in_kernel(wt_ref, qkv_ref, d_ref, ab_ref, g_ref):
    tk = wt_ref.shape[1]
    qkv_ref[...] = wt_ref[:W_IN_QKV, :].T.astype(BF16)
    d_ref[...] = wt_ref[W_IN_QKV:W_IN_D, :].T.astype(BF16)
    ab = jnp.concatenate([wt_ref[W_IN_D:W_IN_AB, :], jnp.zeros((LANES - (W_IN_AB - W_IN_D), tk), F32)], axis=0)
    ab_ref[...] = ab.T.astype(BF16)
    g_ref[...] = wt_ref[W_IN_AB:W_IN_G, :].T.astype(BF16)


def _split_w_in(w_in_t, tk=256):
    k = w_in_t.shape[1]
    widths = (W_IN_QKV, W_IN_D - W_IN_QKV, LANES, W_IN_G - W_IN_AB)
    return pl.pallas_call(
        _split_w_in_kernel,
        grid=(k // tk,),
        in_specs=[pl.BlockSpec((w_in_t.shape[0], tk), lambda i: (0, i))],
        out_specs=[pl.BlockSpec((tk, wd), lambda i: (i, 0)) for wd in widths],
        out_shape=[jax.ShapeDtypeStruct((k, wd), BF16) for wd in widths],
        compiler_params=_cparams("parallel"),
        name="split_w_in",
    )(w_in_t)


def _prep(p):
    wqkv, wd, wab, wg = _split_w_in(p["w_in"].T)
    w = {
        "wqkv": wqkv,
        "wd": wd,
        "wab": wab,
        "wg": wg,
        "norm_mix_g": p["norm_mix_g"].reshape(1, D_MODEL),
        "dn_conv_w": p["dn_conv_w"],
        "alog": _lane_row(p["dn_a_log"]),
        "dtb": _lane_row(p["dn_dt_bias"]),
        "dn_norm_g": p["dn_norm_g"].reshape(1, DN_HEAD_DIM),
        "sinks": p["sinks"].astype(F32),
        "w_br_a": p["w_br_a"].astype(BF16),
        "w_br_b": p["w_br_b"].astype(BF16),
        "w_mix_out": p["w_mix_out"].astype(BF16),
        "norm_x_g": p["norm_x_g"].reshape(1, D_MODEL),
        "w_xq": p["w_xq"].astype(BF16),
        "w_xo": p["w_xo"].astype(BF16),
        "norm_ffn_g": p["norm_ffn_g"].reshape(1, D_MODEL),
        "w_up": p["w_up"].astype(BF16),
        "ffn_conv_w": p["ffn_conv_w"],
        "w_down": p["w_down"].astype(BF16),
        "final_norm_g": p["final_norm_g"].reshape(1, D_MODEL),
    }
    return w


def _tile(m, pref):
    return pref if m % pref == 0 else m


def _prompt_layer(x, mem, norm_mem_g, w_xkv, w):
    bsz, seqlen, _ = x.shape
    m = bsz * seqlen
    x2 = x.reshape(m, D_MODEL)
    tm = _tile(seqlen, ROW_TILE)
    tabs = _rope_tables(jnp.arange(seqlen, dtype=jnp.int32))
    assert seqlen >= WINDOW, "the window outputs are taken from a full last window"
    q, k, v, ga, gb, kwin, vwin, yb, dn_state, dn_tail = _proj_gdn(
        x2, w["norm_mix_g"], tabs, w["wqkv"], w["wd"], w["wab"], w["wg"], w["dn_conv_w"], w["alog"], w["dtb"],
        w["dn_norm_g"], bsz, tm)
    h, hq = _merge(x2, (w["sinks"], q, k, v), yb, ga, gb, w["w_br_a"], w["w_br_b"], w["w_mix_out"],
                   w["norm_x_g"], w["w_xq"], bsz, tm)
    nm = mem.shape[1]
    mk, mv, mk_rows, mv_rows = _memkv(mem.reshape(bsz * nm, D_MODEL), norm_mem_g.reshape(1, D_MODEL),
                                      w_xkv.astype(BF16), _tile(bsz * nm, ROW_TILE))
    cin = jnp.zeros((bsz, SUBLANES, D_FF), F32)
    y, tail = _ffn(h, hq, (mk, mv), cin, w["w_xo"], w["norm_ffn_g"], w["w_up"], w["ffn_conv_w"],
                   w["w_down"], w["final_norm_g"], tm, 1)
    win = lambda a: jnp.transpose(a.reshape(bsz, A_KV_HEADS, A_HEAD_DIM, WINDOW), (0, 3, 1, 2))
    new = (
        win(kwin),
        win(vwin),
        dn_tail[:, SUBLANES - (DN_CONV - 1):],
        dn_state,
        mk_rows.reshape(bsz, nm, X_HEADS, X_HEAD_DIM),
        mv_rows.reshape(bsz, nm, X_HEADS, X_HEAD_DIM),
        tail[:, SUBLANES - (FFN_CONV - 1):],
    )
    return y.reshape(bsz, seqlen, D_MODEL), new


def _sample_layer(x, pos0, win_k, win_v, dn_buf, dn_state, mem_k, mem_v, ffn_buf, w):
    bsz, t, _ = x.shape
    m = bsz * t
    sr = SAMPLE_ROWS
    x2 = x.reshape(m, D_MODEL)
    tabs = _rope_tables(jnp.tile(pos0 + jnp.arange(t, dtype=jnp.int32), bsz))
    q, k, v, dqkv, dz, ab, ga, gb = _proj(x2, w["norm_mix_g"], tabs, w["wqkv"], w["wd"], w["wab"], w["wg"], m)

    qh = jnp.transpose(q.reshape(bsz, t, A_HEADS, A_HEAD_DIM), (0, 2, 1, 3)).reshape(bsz, A_HEADS * t, A_HEAD_DIM)
    padrows = lambda a: jnp.pad(a.reshape(bsz, t, -1), ((0, 0), (0, sr - t), (0, 0)))
    kv_major = lambda a: jnp.transpose(a, (0, 2, 3, 1))
    nbw = min(bsz, LANES // t)

    def new_t(a):
        a = jnp.transpose(a.reshape(bsz // nbw, nbw * t, A_KV_HEADS, A_HEAD_DIM), (0, 2, 3, 1))
        return jnp.pad(a, ((0, 0), (0, 0), (0, 0), (0, LANES - nbw * t)))

    oh, new_wk, new_wv = _swa_sample(qh, kv_major(win_k), kv_major(win_v), new_t(k), new_t(v), w["sinks"], t)
    ya = jnp.transpose(oh.reshape(bsz, A_HEADS, t, A_HEAD_DIM), (0, 2, 1, 3)).reshape(m, A_WIDTH)
    new_wk, new_wv = (jnp.transpose(a, (0, 3, 1, 2)) for a in (new_wk, new_wv))

    yb, new_s, new_dn_buf = _gdn_sample(
        jnp.transpose(dn_buf, (1, 0, 2)), dqkv.reshape(bsz, t, 3 * DN_WIDTH), dz.reshape(bsz, t, DN_WIDTH),
        ab.reshape(bsz, t, LANES), dn_state, w["dn_conv_w"], w["alog"], w["dtb"], w["dn_norm_g"])
    yb = yb.reshape(m, DN_WIDTH)
    new_dn_buf = jnp.transpose(new_dn_buf, (1, 0, 2))

    h, hq = _merge(x2, ya, yb, ga, gb, w["w_br_a"], w["w_br_b"], w["w_mix_out"], w["norm_x_g"], w["w_xq"], 1, m)
    nm = mem_k.shape[1]
    xo = _xattn_sample(padrows(hq), mem_k.reshape(bsz, nm * X_HEADS, X_HEAD_DIM),
                       mem_v.reshape(bsz, nm * X_HEADS, X_HEAD_DIM))
    xo = xo[:, :t]

    tmaj = lambda a: jnp.transpose(a.reshape(bsz, t, -1), (1, 0, 2)).reshape(m, -1)
    cin = jnp.transpose(ffn_buf, (1, 0, 2)).reshape(1, (FFN_CONV - 1) * bsz, D_FF)
    y, tail = _ffn(tmaj(h), tmaj(xo), None, cin, w["w_xo"], w["norm_ffn_g"], w["w_up"], w["ffn_conv_w"],
                   w["w_down"], w["final_norm_g"], m, bsz)
    y = jnp.transpose(y.reshape(t, bsz, D_MODEL), (1, 0, 2))
    new_ffn = jnp.transpose(tail.reshape(FFN_CONV - 1, bsz, D_FF), (1, 0, 2))
    return y, (new_wk, new_wv, new_dn_buf, new_s, new_ffn)


def kernel(x_prompt, x_sample, mem_prompt, cache_win_k, cache_win_v, state_dn_conv, state_dn, cache_mem_k, cache_mem_v, state_ffn_conv, norm_mix_g, w_in, dn_conv_w, dn_a_log, dn_dt_bias, dn_norm_g, attn_sinks, w_br_a, w_br_b, w_mix_out, norm_x_g, norm_mem_g, w_xq, w_xkv, w_xo, norm_ffn_g, w_up, ffn_conv_w, w_down, final_norm_g):
    p = {"norm_mix_g": norm_mix_g[0], "w_in": w_in[0], "dn_conv_w": dn_conv_w[0], "dn_a_log": dn_a_log[0],
         "dn_dt_bias": dn_dt_bias[0], "dn_norm_g": dn_norm_g[0], "sinks": attn_sinks[0], "w_br_a": w_br_a[0],
         "w_br_b": w_br_b[0], "w_mix_out": w_mix_out[0], "norm_x_g": norm_x_g[0], "w_xq": w_xq[0],
         "w_xo": w_xo[0], "norm_ffn_g": norm_ffn_g[0], "w_up": w_up[0], "ffn_conv_w": ffn_conv_w[0],
         "w_down": w_down[0], "final_norm_g": final_norm_g}
    w = _prep(p)
    yp, newp = _prompt_layer(x_prompt, mem_prompt, norm_mem_g[0], w_xkv[0], w)
    ys, news = _sample_layer(x_sample, PAST_LEN, cache_win_k[0], cache_win_v[0], state_dn_conv[0], state_dn[0],
                             cache_mem_k[0], cache_mem_v[0], state_ffn_conv[0], w)
    lead = lambda a: a[None]
    p_win_k, p_win_v, p_dn_conv, p_dn_state, p_mem_k, p_mem_v, p_ffn_conv = [lead(a) for a in newp]
    s_win_k, s_win_v, s_dn_conv, s_dn_state, s_ffn_conv = [lead(a) for a in news]
    return (yp, ys, p_win_k, p_win_v, p_dn_conv, p_dn_state, p_mem_k, p_mem_v, p_ffn_conv,
            s_win_k, s_win_v, s_dn_conv, s_dn_state, s_ffn_conv)
```

```python
import functools

import jax
import jax.numpy as jnp
from jax import lax
from jax.experimental import pallas as pl
from jax.experimental.pallas import tpu as pltpu

F32 = jnp.float32
BF16 = jnp.bfloat16

D_MODEL = 1024
A_HEADS = 8
A_KV_HEADS = 2
A_HEAD_DIM = 64
A_WIDTH = 512
A_KV_WIDTH = 128
WINDOW = 128
ROT_DIM = 16
ROPE_THETA = 500000.0
DN_HEADS = 4
DN_HEAD_DIM = 128
DN_WIDTH = 512
DN_CONV = 4
X_HEADS = 4
X_HEAD_DIM = 128
X_WIDTH = 512
D_FF = 2816
FFN_CONV = 3
EPS = 1e-6
PAST_LEN = 16384

LANES = 128
SUBLANES = 8
VMEM_LIMIT = 56 * 1024 * 1024
ROW_TILE = 512
CHUNK = 128
SAMPLE_ROWS = 16
GDN_SAMPLE_ROWS = 8


def _cparams(*sem):
    return pltpu.CompilerParams(dimension_semantics=sem, vmem_limit_bytes=VMEM_LIMIT)


def _resident(shape):
    return pl.BlockSpec(shape, lambda *_: (0,) * len(shape), pipeline_mode=pl.Buffered(1))


def _rms(x, g):
    return x * lax.rsqrt(jnp.mean(x * x, axis=-1, keepdims=True) + EPS) * g


def _dot(a, b):
    return jnp.dot(a, b, preferred_element_type=F32)


def _dot_nt(a, b):
    return lax.dot_general(a, b, (((1,), (1,)), ((), ())), preferred_element_type=F32)


def _silu(x):
    return x * jax.nn.sigmoid(x)


def _rope(seg, c, s1, s2):
    return seg * c + pltpu.roll(seg, LANES - 8, 1) * s1 + pltpu.roll(seg, 8, 1) * s2


PROJ_COLS = 512
assert (3 * DN_WIDTH) % PROJ_COLS == 0 and D_MODEL % PROJ_COLS == 0


def _proj_tile(x_ref, g_ref, c_ref, s1_ref, s2_ref, wqkv_ref, wd_ref, wab_ref, wg_ref,
               q_ref, k_ref, v_ref, dqkv_ref, dz_ref, ab_ref, ga_ref, gb_ref, dqkv_row0=0, win_refs=None):
    tm = x_ref.shape[0]
    xb = _rms(x_ref[...], g_ref[...]).astype(BF16)
    yield
    c, s1, s2 = c_ref[...], s1_ref[...], s2_ref[...]
    z = _dot(xb, wqkv_ref[...])
    for i in range(A_WIDTH // LANES):
        sl = slice(i * LANES, (i + 1) * LANES)
        q_ref[:, sl] = _rope(z[:, sl], c, s1, s2).astype(BF16)
    k = _rope(z[:, A_WIDTH:A_WIDTH + LANES], c, s1, s2)
    k_ref[...] = k
    v_ref[...] = z[:, A_WIDTH + LANES:]
    if win_refs is not None:
        win_refs[0][0] = k[tm - WINDOW:].T
        win_refs[1][0] = z[tm - WINDOW:, A_WIDTH + LANES:].T
    yield
    pc = PROJ_COLS
    for j in range(3 * DN_WIDTH // pc):
        dqkv_ref[dqkv_row0:dqkv_row0 + tm, j * pc:(j + 1) * pc] = _dot(xb, wd_ref[:, j * pc:(j + 1) * pc])
        yield
    dz_ref[...] = _dot(xb, wd_ref[:, 3 * DN_WIDTH:])
    ab_ref[...] = _dot(xb, wab_ref[...])
    yield
    for dst, base in ((ga_ref, 0), (gb_ref, D_MODEL)):
        for j in range(D_MODEL // pc):
            dst[:, j * pc:(j + 1) * pc] = _dot(xb, wg_ref[:, base + j * pc:base + (j + 1) * pc])
            yield


def _proj_kernel(*refs):
    _drain(_proj_tile(*refs))


def _proj(x, g, tabs, wqkv, wd, wab, wg, tm):
    m = x.shape[0]
    nt = tabs[0].shape[0] // tm
    row = lambda w: pl.BlockSpec((tm, w), lambda i: (i, 0))
    tab = pl.BlockSpec((tm, LANES), lambda i: (i % nt, 0))
    widths = (A_WIDTH, LANES, LANES, 3 * DN_WIDTH, DN_WIDTH, LANES, D_MODEL, D_MODEL)
    dts = (BF16, F32, F32, F32, F32, F32, F32, F32)
    return pl.pallas_call(
        _proj_kernel,
        grid=(m // tm,),
        in_specs=[row(D_MODEL), _resident((1, D_MODEL)), tab, tab, tab,
                  _resident(wqkv.shape), _resident(wd.shape), _resident(wab.shape), _resident(wg.shape)],
        out_specs=[row(w) for w in widths],
        out_shape=[jax.ShapeDtypeStruct((m, w), d) for w, d in zip(widths, dts)],
        compiler_params=_cparams("parallel"),
        name="in_proj",
    )(x, g, *tabs, wqkv, wd, wab, wg)


def _rope_tables(pos):
    half = ROT_DIM // 2
    d = jnp.arange(LANES, dtype=jnp.int32) % A_HEAD_DIM
    inv = ROPE_THETA ** (-2.0 * (d % half).astype(F32) / ROT_DIM)
    ang = pos.astype(F32)[:, None] * inv[None, :]
    c, s = jnp.cos(ang), jnp.sin(ang)
    return (jnp.where(d < ROT_DIM, c, 1.0), jnp.where(d < half, -s, 0.0),
            jnp.where((d >= half) & (d < ROT_DIM), s, 0.0))


def _both_halves(t, lane_lo):
    r = pltpu.roll(t, A_HEAD_DIM, 1)
    return jnp.where(lane_lo, t, r), jnp.where(lane_lo, r, t)


def _sink_softmax(s, valid, sink):
    s = jnp.where(valid, s, -jnp.inf)
    m = jnp.maximum(jnp.max(s, axis=-1, keepdims=True), sink)
    p = jnp.exp(s - m)
    den = jnp.sum(p, axis=-1, keepdims=True) + jnp.exp(sink - m)
    return p * (1.0 / den)


def _swa_tile(sink_ref, q_ref, k, v, first):
    w = WINDOW
    lane_lo_k = lax.broadcasted_iota(jnp.int32, k.shape, 1) < A_HEAD_DIM
    kk = [t.astype(BF16) for t in _both_halves(k, lane_lo_k)]
    vv = [t.astype(BF16) for t in _both_halves(v, lane_lo_k)]
    lane_lo = lax.broadcasted_iota(jnp.int32, (w, LANES), 1) < A_HEAD_DIM
    row = lax.broadcasted_iota(jnp.int32, (4 * w, 2 * w), 0)
    col = lax.broadcasted_iota(jnp.int32, (4 * w, 2 * w), 1)
    d = (row & (w - 1)) + w - col
    band = (d >= 0) & (d < w)
    band_first = band & (col >= jnp.where(first, w, 0))
    hrow = lax.broadcasted_iota(jnp.int32, (4 * w, 1), 0) // w
    sinks = []
    for g in range(A_KV_HEADS):
        sink = jnp.zeros((4 * w, 1), F32)
        for j in range(4):
            sink = jnp.where(hrow == j, sink_ref[4 * g + j], sink)
        sinks.append(sink)
    zero = jnp.zeros((), BF16)
    scale = jnp.asarray(A_HEAD_DIM ** -0.5, BF16)
    nblk = q_ref.shape[0] // w
    probs = [(b, g) for b in range(nblk) for g in range(A_KV_HEADS)]

    def queries(b, g):
        parts = []
        for sgm in range(2):
            seg = q_ref[b * w:(b + 1) * w, (2 * g + sgm) * LANES:(2 * g + sgm + 1) * LANES] * scale
            parts += [jnp.where(lane_lo, seg, zero), jnp.where(lane_lo, zero, seg)]
        return jnp.concatenate(parts, axis=0)

    ss = [_dot_nt(queries(b, g), kk[g][b * w:(b + 2) * w]) for b, g in probs]
    yield
    ps = [_sink_softmax(s, band_first if b == 0 else band, sinks[g]).astype(BF16) for s, (b, g) in zip(ss, probs)]
    yield
    os_ = [_dot(p, vv[g][b * w:(b + 2) * w]) for p, (b, g) in zip(ps, probs)]
    yield
    segs = [[jnp.where(lane_lo, o[(2 * sgm) * w:(2 * sgm + 1) * w], o[(2 * sgm + 1) * w:(2 * sgm + 2) * w]
                       ).astype(BF16) for sgm in range(2)] for o in os_]
    return jnp.concatenate([jnp.concatenate(segs[A_KV_HEADS * b] + segs[A_KV_HEADS * b + 1], axis=1)
                            for b in range(nblk)], axis=0)


def _swa_sample_kernel(sink_ref, q_ref, ckt_ref, cvt_ref, knt_ref, vnt_ref, o_ref, cko_ref, cvo_ref, *, nb, t):
    w = ckt_ref.shape[3]
    old = lax.broadcasted_iota(jnp.int32, (A_HEAD_DIM, w), 1) < w - t
    for src_ref, new_ref, dst_ref in ((ckt_ref, knt_ref, cko_ref), (cvt_ref, vnt_ref, cvo_ref)):
        for g in range(A_KV_HEADS):
            new = new_ref[0, g]
            for b in range(nb):
                dst_ref[b, g] = jnp.where(old, pltpu.roll(src_ref[b, g], w - t, 1),
                                          pltpu.roll(new, (w - t - t * b) % LANES, 1))
    rows = q_ref.shape[1] // A_HEADS
    gq = rows * (A_HEADS // A_KV_HEADS)
    r = lax.broadcasted_iota(jnp.int32, (gq, w), 0)
    c = lax.broadcasted_iota(jnp.int32, (gq, w), 1)
    valid_c = c > r % rows
    rn = lax.broadcasted_iota(jnp.int32, (gq, LANES), 0)
    cn = lax.broadcasted_iota(jnp.int32, (gq, LANES), 1)
    causal_n = cn % t <= rn % rows
    valid_n = [causal_n & (cn // t == b) for b in range(nb)]
    hrow = lax.broadcasted_iota(jnp.int32, (gq, 1), 0) // rows
    sink = []
    for g in range(A_KV_HEADS):
        sk = jnp.zeros((gq, 1), F32)
        for j in range(A_HEADS // A_KV_HEADS):
            sk = jnp.where(hrow == j, sink_ref[g * (A_HEADS // A_KV_HEADS) + j], sk)
        sink.append(sk)
    probs = [(b, g) for b in range(nb) for g in range(A_KV_HEADS)]
    scale = A_HEAD_DIM ** -0.5
    qs = [q_ref[b, g * gq:(g + 1) * gq, :] for b, g in probs]
    scs = [jnp.where(valid_c, _dot(q, ckt_ref[b, g].astype(BF16)) * scale, -jnp.inf)
           for q, (b, g) in zip(qs, probs)]
    knt = [knt_ref[0, g].astype(BF16) for g in range(A_KV_HEADS)]
    vnt = [vnt_ref[0, g].astype(BF16) for g in range(A_KV_HEADS)]
    sns = [jnp.where(valid_n[b], _dot(q, knt[g]) * scale, -jnp.inf) for q, (b, g) in zip(qs, probs)]
    ms = [jnp.maximum(jnp.maximum(jnp.max(sc, -1, keepdims=True), jnp.max(sn, -1, keepdims=True)), sink[g])
          for sc, sn, (b, g) in zip(scs, sns, probs)]
    pcs = [jnp.exp(sc - m) for sc, m in zip(scs, ms)]
    pns = [jnp.exp(sn - m) for sn, m in zip(sns, ms)]
    invs = [1.0 / (jnp.sum(pc, -1, keepdims=True) + jnp.sum(pn, -1, keepdims=True) + jnp.exp(sink[g] - m))
            for pc, pn, m, (b, g) in zip(pcs, pns, ms, probs)]
    for pc, pn, inv, (b, g) in zip(pcs, pns, invs, probs):
        o_ref[b, g * gq:(g + 1) * gq, :] = (_dot_nt((pc * inv).astype(BF16), cvt_ref[b, g].astype(BF16))
                                            + _dot_nt((pn * inv).astype(BF16), vnt[g]))


def _swa_sample(q, ckt, cvt, knt, vnt, sinks, t):
    bsz, nq, _ = q.shape
    nb = bsz // knt.shape[0]
    assert nb * t <= LANES and ckt.shape[3] == LANES
    blk = lambda a, n=nb: pl.BlockSpec((n,) + a.shape[1:], lambda i: (i,) + (0,) * (a.ndim - 1))
    return pl.pallas_call(
        functools.partial(_swa_sample_kernel, nb=nb, t=t),
        grid=(bsz // nb,),
        in_specs=[pl.BlockSpec(memory_space=pltpu.SMEM), blk(q), blk(ckt), blk(cvt), blk(knt, 1), blk(vnt, 1)],
        out_specs=[blk(q), blk(ckt), blk(cvt)],
        out_shape=[jax.ShapeDtypeStruct(q.shape, F32), jax.ShapeDtypeStruct(ckt.shape, F32),
                   jax.ShapeDtypeStruct(cvt.shape, F32)],
        compiler_params=_cparams("parallel"),
        name="swa_sample",
    )(sinks, q, ckt, cvt, knt, vnt)


def _lane_bcast(x, lane):
    return jnp.broadcast_to(x[:, lane:lane + 1], (x.shape[0], LANES))


def _cumsum_rows(x, block):
    c = x.shape[0]
    rowi = lax.broadcasted_iota(jnp.int32, (c, c), 0)
    coli = lax.broadcasted_iota(jnp.int32, (c, c), 1)
    ones = (((rowi // block) == (coli // block)) & (rowi >= coli)).astype(BF16)
    hi = x.astype(BF16)
    r1 = x - hi.astype(F32)
    mid = r1.astype(BF16)
    lo = (r1 - mid.astype(F32)).astype(BF16)
    s = _dot(ones, jnp.concatenate([hi, mid, lo], axis=1))
    return s[:, :LANES] + s[:, LANES:2 * LANES] + s[:, 2 * LANES:]


def _l2n(t):
    return t * lax.rsqrt(jnp.sum(t * t, axis=-1, keepdims=True) + EPS)


def _gates(ab, alog, dtb):
    x = ab + dtb
    sp = jnp.maximum(x, 0.0) + jnp.log1p(jnp.exp(-jnp.abs(x)))
    return -jnp.exp(alog) * sp, jax.nn.sigmoid(ab)


def _merge_masks(c, top):
    rowi = lax.broadcasted_iota(jnp.int32, (c, c), 0)
    coli = lax.broadcasted_iota(jnp.int32, (c, c), 1)
    masks = []
    s = 1
    while s < top:
        rb, cb = rowi // s, coli // s
        masks.append(((rb // 2) == (cb // 2)) & ((rb % 2) == 1) & ((cb % 2) == 0))
        s *= 2
    return masks


def _each(f, *lists):
    return [f(*t) for t in zip(*lists)]


def _drain(stages):
    try:
        while True:
            next(stages)
    except StopIteration as done:
        return done.value


def _interleave(*staged):
    live = list(staged)
    values = {}
    while live:
        for item in tuple(live):
            stages, per_turn = item
            try:
                for _ in range(per_turn):
                    next(stages)
            except StopIteration as done:
                values[id(stages)] = done.value
                live.remove(item)
    return [values[id(stages)] for stages, _ in staged]


def _chunk_local(qs, ks, vs, gcols, grows, betas, tril, merges):
    c = qs[0].shape[0]
    decays = _each(lambda gc, gr: jnp.exp(jnp.minimum(gc - gr, 0.0)), gcols, grows)
    kbs = _each(lambda k, b: k * b, ks, betas)
    yield
    kts = [k.T for k in ks]
    ms = _each(lambda q, kb, kt: _dot(jnp.concatenate([q, kb], axis=0).astype(BF16), kt.astype(BF16)),
               qs, kbs, kts)
    yield
    qks = _each(lambda m, d: jnp.where(tril, m[:c] * d, 0.0), ms, decays)
    a = _each(lambda m, d: m[c:] * d, ms, decays)
    eye = (lax.broadcasted_iota(jnp.int32, (c, c), 0) == lax.broadcasted_iota(jnp.int32, (c, c), 1)).astype(F32)
    ts = [eye - jnp.where(merges[0], x, 0.0) if merges else eye for x in a]
    yield
    for off in merges[1:]:
        tbs = [t.astype(BF16) for t in ts]
        zs = _each(lambda x, tb: _dot(jnp.where(off, x, 0.0).astype(BF16), tb), a, tbs)
        yield
        ts = _each(lambda t, tb, z: t - _dot(tb, z.astype(BF16)), ts, tbs, zs)
        yield
    ns = [t - eye for t in ts]
    egs = [jnp.exp(gc) for gc in gcols]
    rhss = _each(lambda v, b, kb, eg: jnp.concatenate([v * b, kb * eg], axis=1), vs, betas, kbs, egs)
    yield
    uws = _each(lambda r, n: r + _dot(n.astype(BF16), r.astype(BF16)), rhss, ns)
    yield
    return ([x[:, :LANES] for x in uws], [x[:, LANES:] for x in uws], qks,
            _each(lambda q, eg: q * eg, qs, egs), kts)


def _gdn_tile(xp_ref, dz_ref, ab_ref, cw_ref, alog_ref, dtb_ref, ng_ref, y_ref, s_ref, nc):
    c = CHUNK
    pad = SUBLANES
    cw = cw_ref[...]
    g, beta = _gates(ab_ref[...], alog_ref[...], dtb_ref[...])
    rowi = lax.broadcasted_iota(jnp.int32, (c, c), 0)
    coli = lax.broadcasted_iota(jnp.int32, (c, c), 1)
    tril = rowi >= coli
    merges = _merge_masks(c, c)
    ng = ng_ref[...]
    heads = range(DN_HEADS)
    probs = [(ci, h) for ci in range(nc) for h in heads]
    rows = lambda ci: slice(ci * c, (ci + 1) * c)
    lanes = lambda part, h: slice(part * DN_WIDTH + h * LANES, part * DN_WIDTH + (h + 1) * LANES)
    yield
    qs, ks, vs, gcols, grows, betas = [], [], [], [], [], []
    for ci in range(nc):
        conv = xp_ref[pad + ci * c:pad + (ci + 1) * c, :] * cw[DN_CONV - 1:DN_CONV]
        for j in range(DN_CONV - 1):
            o = pad - (DN_CONV - 1) + j + ci * c
            conv = conv + xp_ref[o:o + c, :] * cw[j:j + 1]
        conv = _silu(conv)
        gc = _cumsum_rows(g[rows(ci)], c)
        gct = gc.T
        qs += [_l2n(conv[:, lanes(0, h)]) * (DN_HEAD_DIM ** -0.5) for h in heads]
        ks += [_l2n(conv[:, lanes(1, h)]) for h in heads]
        vs += [conv[:, lanes(2, h)] for h in heads]
        gcols += [_lane_bcast(gc, h) for h in heads]
        grows += [gct[h:h + 1, :] for h in heads]
        betas += [_lane_bcast(beta[rows(ci)], DN_HEADS + h) for h in heads]
        yield
    us, ws, qks, qds, kts = yield from _chunk_local(qs, ks, vs, gcols, grows, betas, tril, merges)
    glasts = [gc[c - 1:c, :] for gc in gcols]
    kdts = _each(lambda kt, gr: kt * jnp.exp(gr[:, c - 1:c] - gr), kts, grows)
    wqs = _each(lambda w, qd: jnp.concatenate([w, qd], axis=0).astype(BF16), ws, qds)
    qkks = _each(lambda qk, kdt: jnp.concatenate([qk, kdt], axis=0).astype(BF16), qks, kdts)
    yield
    ss = [s_ref[h] for h in heads]
    for ci in range(nc):
        pr = [ci * DN_HEADS + h for h in heads]
        r2s = [_dot(wqs[p], s.astype(BF16)) for p, s in zip(pr, ss)]
        vnews = [us[p] - r2[:c] for p, r2 in zip(pr, r2s)]
        yield
        r3s = [_dot(qkks[p], vn.astype(BF16)) for p, vn in zip(pr, vnews)]
        ss = [s * jnp.exp(glasts[p]) + r3[c:] for p, s, r3 in zip(pr, ss, r3s)]
        for h in heads:
            o = r2s[h][c:] + r3s[h][:c]
            y_ref[rows(ci), lanes(0, h)] = (_rms(o, ng) * _silu(dz_ref[rows(ci), lanes(0, h)])).astype(BF16)
        yield
    for h in heads:
        s_ref[h] = ss[h]


def _proj_gdn_kernel(x_ref, g_ref, c_ref, s1_ref, s2_ref, wqkv_ref, wd_ref, wab_ref, wg_ref,
                     cw_ref, alog_ref, dtb_ref, ng_ref,
                     q_ref, k_ref, v_ref, ga_ref, gb_ref, kw_ref, vw_ref, y_ref, sout_ref, tail_ref,
                     xp_ref, dz_ref, ab_ref, carry_ref, s_ref, *, nt, nc):
    i = pl.program_id(0)
    pad = SUBLANES
    r = nc * CHUNK
    slot_a = i % 2
    slot_b = 1 - slot_a

    @pl.when(i == 0)
    def _():
        xp_ref[1] = jnp.zeros(xp_ref.shape[1:], F32)
        dz_ref[1] = jnp.zeros(dz_ref.shape[1:], F32)
        ab_ref[1] = jnp.zeros(ab_ref.shape[1:], F32)

    @pl.when(jnp.maximum(i - 1, 0) % nt == 0)
    def _():
        carry_ref[...] = jnp.zeros_like(carry_ref)
        s_ref[...] = jnp.zeros_like(s_ref)

    xpb_ref = xp_ref.at[slot_b]
    xpb_ref[0:pad, :] = carry_ref[...]
    _interleave(
        (_gdn_tile(xpb_ref, dz_ref.at[slot_b], ab_ref.at[slot_b], cw_ref, alog_ref, dtb_ref, ng_ref,
                   y_ref, s_ref, nc), 1),
        (_proj_tile(x_ref, g_ref, c_ref, s1_ref, s2_ref, wqkv_ref, wd_ref, wab_ref, wg_ref, q_ref, k_ref, v_ref,
                    xp_ref.at[slot_a], dz_ref.at[slot_a], ab_ref.at[slot_a], ga_ref, gb_ref, pad,
                    (kw_ref, vw_ref)), 1))
    carry_ref[...] = xpb_ref[r:r + pad, :]
    tail_ref[0] = carry_ref[...]
    sout_ref[0] = s_ref[...]


def _proj_gdn(x, g, tabs, wqkv, wd, wab, wg, cw, alog, dtb, ng, nseq, tm):
    m = x.shape[0]
    nt = m // nseq // tm
    last = m // tm - 1
    nc = tm // CHUNK
    proj = lambda i: jnp.minimum(i, last)
    gdn = lambda i: jnp.maximum(i - 1, 0)
    prow = lambda w: pl.BlockSpec((tm, w), lambda i: (proj(i), 0))
    tab = pl.BlockSpec((tm, LANES), lambda i: (proj(i) % nt, 0))
    per_seq = lambda *dims: pl.BlockSpec((1,) + dims, lambda i: (gdn(i) // nt,) + (0,) * len(dims))
    win = pl.BlockSpec((1, A_KV_WIDTH, WINDOW), lambda i: (proj(i) // nt, 0, 0))
    win_shape = jax.ShapeDtypeStruct((nseq, A_KV_WIDTH, WINDOW), F32)
    state = (DN_HEADS, DN_HEAD_DIM, DN_HEAD_DIM)
    bufs = [pltpu.VMEM((2, tm + SUBLANES, 3 * DN_WIDTH), F32), pltpu.VMEM((2, tm, DN_WIDTH), F32),
            pltpu.VMEM((2, tm, LANES), F32)]
    return pl.pallas_call(
        functools.partial(_proj_gdn_kernel, nt=nt, nc=nc),
        grid=(m // tm + 1,),
        in_specs=[prow(D_MODEL), _resident((1, D_MODEL)), tab, tab, tab,
                  _resident(wqkv.shape), _resident(wd.shape), _resident(wab.shape), _resident(wg.shape),
                  _resident(cw.shape), _resident((1, LANES)), _resident((1, LANES)), _resident((1, LANES))],
        out_specs=[prow(A_WIDTH), prow(LANES), prow(LANES), prow(D_MODEL), prow(D_MODEL), win, win,
                   pl.BlockSpec((tm, DN_WIDTH), lambda i: (gdn(i), 0)), per_seq(*state),
                   per_seq(SUBLANES, 3 * DN_WIDTH)],
        out_shape=[jax.ShapeDtypeStruct((m, A_WIDTH), BF16), jax.ShapeDtypeStruct((m, LANES), F32),
                   jax.ShapeDtypeStruct((m, LANES), F32), jax.ShapeDtypeStruct((m, D_MODEL), F32),
                   jax.ShapeDtypeStruct((m, D_MODEL), F32), win_shape, win_shape,
                   jax.ShapeDtypeStruct((m, DN_WIDTH), BF16),
                   jax.ShapeDtypeStruct((nseq,) + state, F32),
                   jax.ShapeDtypeStruct((nseq, SUBLANES, 3 * DN_WIDTH), F32)],
        scratch_shapes=bufs + [pltpu.VMEM((SUBLANES, 3 * DN_WIDTH), F32), pltpu.VMEM(state, F32)],
        compiler_params=_cparams("arbitrary"),
        name="proj_gdn",
    )(x, g, *tabs, wqkv, wd, wab, wg, cw, alog, dtb, ng)


def _gdn_sample_kernel(buf_ref, x_ref, dz_ref, ab_ref, s0_ref, cw_ref, alog_ref, dtb_ref, ng_ref,
                       y_ref, sout_ref, bufo_ref, xp_ref, ab16_ref):
    t = x_ref.shape[1]
    c = CHUNK
    sr = GDN_SAMPLE_ROWS
    nb = c // sr
    pad = SUBLANES
    hist = DN_CONV - 1
    xp_ref[...] = jnp.zeros_like(xp_ref)
    ab16_ref[...] = jnp.zeros_like(ab16_ref)
    for b in range(nb):
        for j in range(hist):
            xp_ref[pad + b * sr - hist + j:pad + b * sr - hist + j + 1, :] = buf_ref[j, b:b + 1, :]
        xp_ref[pad + b * sr:pad + b * sr + t, :] = x_ref[b]
        ab16_ref[b * sr:b * sr + t, :] = ab_ref[b]
    for b in range(nb):
        for j in range(hist):
            o = pad + b * sr - hist + t + j
            bufo_ref[j, b:b + 1, :] = xp_ref[o:o + 1, :]
    cw = cw_ref[...]
    conv = xp_ref[pad:pad + c, :] * cw[hist:hist + 1]
    for j in range(hist):
        conv = conv + xp_ref[pad - hist + j:pad - hist + j + c, :] * cw[j:j + 1]
    conv = _silu(conv)
    live = (lax.broadcasted_iota(jnp.int32, (c, LANES), 0) % sr) < t
    g, beta = _gates(ab16_ref[...], alog_ref[...], dtb_ref[...])
    g = jnp.where(live, g, 0.0)
    beta = jnp.where(live, beta, 0.0)
    gc = _cumsum_rows(g, sr)
    gct = gc.T
    rowi = lax.broadcasted_iota(jnp.int32, (c, c), 0)
    coli = lax.broadcasted_iota(jnp.int32, (c, c), 1)
    tril = ((rowi // sr) == (coli // sr)) & (rowi >= coli)
    merges = _merge_masks(c, pl.next_power_of_2(t))
    ng = ng_ref[...]
    rowb = lax.broadcasted_iota(jnp.int32, (c, LANES), 0) // sr
    heads = range(DN_HEADS)
    lanes = lambda part, h: slice(part * DN_WIDTH + h * LANES, part * DN_WIDTH + (h + 1) * LANES)
    ks = [_l2n(conv[:, lanes(1, h)]) for h in heads]
    gcols = [_lane_bcast(gc, h) for h in heads]
    us, ws, qks, qds, _ = _drain(_chunk_local(
        [_l2n(conv[:, lanes(0, h)]) * (DN_HEAD_DIM ** -0.5) for h in heads], ks,
        [conv[:, lanes(2, h)] for h in heads], gcols, [gct[h:h + 1, :] for h in heads],
        [_lane_bcast(beta, DN_HEADS + h) for h in heads], tril, merges))
    seqs = range(nb)
    rows = lambda b: slice(b * sr, (b + 1) * sr)
    s0s = [[s0_ref[b, h] for b in seqs] for h in heads]
    r2s = [[_dot(jnp.concatenate([ws[h][rows(b)], qds[h][rows(b)]], axis=0).astype(BF16), s0s[h][b].astype(BF16))
            for b in seqs] for h in heads]
    vnews = [jnp.concatenate([us[h][rows(b)] - r2s[h][b][:sr] for b in seqs], axis=0) for h in heads]
    os_ = [_rms(jnp.concatenate([r2s[h][b][sr:] for b in seqs], axis=0)
                + _dot(qks[h].astype(BF16), vnews[h].astype(BF16)), ng) for h in heads]
    for h in heads:
        for b in seqs:
            y_ref[b, :, lanes(0, h)] = os_[h][b * sr:b * sr + t] * _silu(dz_ref[b, :, lanes(0, h)])
    glasts = [jnp.concatenate([jnp.broadcast_to(gcols[h][(b + 1) * sr - 1:(b + 1) * sr, :], (sr, LANES))
                               for b in seqs], axis=0) for h in heads]
    kdts = [(ks[h] * jnp.exp(glasts[h] - gcols[h])).T.astype(BF16) for h in heads]
    for h in heads:
        for b in seqs:
            vb = jnp.where(rowb == b, vnews[h], 0.0).astype(BF16)
            sout_ref[b, h] = s0s[h][b] * jnp.exp(glasts[h][b * sr:b * sr + 1, :]) + _dot(kdts[h], vb)


def _gdn_sample(buf, x, dz, ab, s0, cw, alog, dtb, ng):
    bsz, t, _ = x.shape
    assert t + DN_CONV - 1 <= GDN_SAMPLE_ROWS, "too many new tokens for one row tile per sequence"
    nb = CHUNK // GDN_SAMPLE_ROWS
    seq = lambda a: pl.BlockSpec((nb,) + a.shape[1:], lambda i: (i,) + (0,) * (a.ndim - 1))
    hist = pl.BlockSpec((buf.shape[0], nb, buf.shape[2]), lambda i: (0, i, 0))
    return pl.pallas_call(
        _gdn_sample_kernel,
        grid=(bsz // nb,),
        in_specs=[hist, seq(x), seq(dz), seq(ab), seq(s0), _resident(cw.shape),
                  _resident((1, LANES)), _resident((1, LANES)), _resident((1, LANES))],
        out_specs=[seq(dz), seq(s0), hist],
        out_shape=[jax.ShapeDtypeStruct(dz.shape, F32), jax.ShapeDtypeStruct(s0.shape, F32),
                   jax.ShapeDtypeStruct(buf.shape, F32)],
        scratch_shapes=[pltpu.VMEM((CHUNK + SUBLANES, 3 * DN_WIDTH), F32), pltpu.VMEM((CHUNK, LANES), F32)],
        compiler_params=_cparams("parallel"),
        name="gdn_sample",
    )(buf, x, dz, ab, s0, cw, alog, dtb, ng)


def _merge_kernel(x_ref, *rest, attend):
    if attend:
        sink_ref, q_ref, kp_ref, kc_ref, vp_ref, vc_ref, *rest = rest
    else:
        ya_ref, *rest = rest
    yb_ref, ga_ref, gb_ref, wa_ref, wb_ref, wo_ref, g_ref, wq_ref, h_ref, hq_ref = rest

    def deltanet_half():
        yb = _dot(yb_ref[...].astype(BF16), wb_ref[...])
        yield
        gate_a = jax.nn.sigmoid(ga_ref[...])
        yield
        return gate_a, jax.nn.sigmoid(gb_ref[...]) * yb

    if attend:
        ya, (gate_a, mix_b) = _interleave(
            (_swa_tile(sink_ref, q_ref, jnp.concatenate([kp_ref[...], kc_ref[...]], axis=0),
                       jnp.concatenate([vp_ref[...], vc_ref[...]], axis=0), pl.program_id(1) == 0), 1),
            (deltanet_half(), 1))
    else:
        ya = ya_ref[...]
        gate_a, mix_b = _drain(deltanet_half())
    mix = gate_a * _dot(ya.astype(BF16), wa_ref[...]) + mix_b
    h = x_ref[...] + _dot(mix.astype(BF16), wo_ref[...])
    h_ref[...] = h
    hq_ref[...] = _dot(_rms(h, g_ref[...]).astype(BF16), wq_ref[...]).astype(BF16)


def _merge(x, attn, yb, ga, gb, wa, wb, wo, g, wq, nseq, tm):
    m = x.shape[0]
    nt = m // nseq // tm
    row = lambda w: pl.BlockSpec((tm, w), lambda s, j: (s * nt + j, 0))
    attend = isinstance(attn, tuple)
    if attend:
        sinks, q, k, v = attn
        per = tm // WINDOW
        prev = pl.BlockSpec((WINDOW, A_KV_WIDTH), lambda s, j: (jnp.maximum((s * nt + j) * per - 1, 0), 0))
        attn_args = [sinks, q, k, k, v, v]
        attn_specs = [pl.BlockSpec(memory_space=pltpu.SMEM), row(A_WIDTH), prev, row(A_KV_WIDTH),
                      prev, row(A_KV_WIDTH)]
    else:
        attn_args, attn_specs = [attn], [row(A_WIDTH)]
    return pl.pallas_call(
        functools.partial(_merge_kernel, attend=attend),
        grid=(nseq, nt),
        in_specs=[row(D_MODEL), *attn_specs, row(DN_WIDTH), row(D_MODEL), row(D_MODEL),
                  _resident(wa.shape), _resident(wb.shape), _resident(wo.shape),
                  _resident((1, D_MODEL)), _resident(wq.shape)],
        out_specs=[row(D_MODEL), row(X_WIDTH)],
        out_shape=[jax.ShapeDtypeStruct((m, D_MODEL), F32), jax.ShapeDtypeStruct((m, X_WIDTH), BF16)],
        compiler_params=_cparams("parallel", "parallel"),
        name="merge",
    )(x, *attn_args, yb, ga, gb, wa, wb, wo, g, wq)


def _memkv_kernel(m_ref, g_ref, w_ref, k_ref, v_ref, kf_ref, vf_ref):
    tm = m_ref.shape[0]
    z = _dot(_rms(m_ref[...], g_ref[...]).astype(BF16), w_ref[...])
    k_ref[...] = z[:, :X_WIDTH]
    v_ref[...] = z[:, X_WIDTH:]
    for h in range(X_HEADS):
        kf_ref[pl.ds(h, tm, stride=X_HEADS), :] = z[:, h * LANES:(h + 1) * LANES]
        vf_ref[pl.ds(h, tm, stride=X_HEADS), :] = z[:, X_WIDTH + h * LANES:X_WIDTH + (h + 1) * LANES]


def _memkv(mem, g, w, tm):
    m = mem.shape[0]
    row = lambda wd: pl.BlockSpec((tm, wd), lambda i: (i, 0))
    flat = pl.BlockSpec((tm * X_HEADS, X_HEAD_DIM), lambda i: (i, 0))
    return pl.pallas_call(
        _memkv_kernel,
        grid=(m // tm,),
        in_specs=[row(D_MODEL), _resident((1, D_MODEL)), _resident(w.shape)],
        out_specs=[row(X_WIDTH), row(X_WIDTH), flat, flat],
        out_shape=[jax.ShapeDtypeStruct((m, X_WIDTH), F32)] * 2
        + [jax.ShapeDtypeStruct((m * X_HEADS, X_HEAD_DIM), F32)] * 2,
        compiler_params=_cparams("parallel"),
        name="mem_kv",
    )(mem, g, w)


def _softmax_rows(s):
    e = jnp.exp(s - jnp.max(s, axis=-1, keepdims=True))
    return e * (1.0 / jnp.sum(e, axis=-1, keepdims=True))


def _xattn_sample_kernel(hq_ref, mk_ref, mv_ref, o_ref, *, nb):
    rows = hq_ref.shape[1]
    nmh = mk_ref.shape[1]
    qh = lax.broadcasted_iota(jnp.int32, (X_HEADS * rows, nmh), 0) // rows
    mh = lax.broadcasted_iota(jnp.int32, (X_HEADS * rows, nmh), 1) % X_HEADS
    same = qh == mh
    seqs = range(nb)
    qs = [jnp.concatenate([hq_ref[b, :, h * LANES:(h + 1) * LANES] for h in range(X_HEADS)], axis=0)
          for b in seqs]
    ss = [_dot_nt(qs[b], mk_ref[b].astype(BF16)) * (X_HEAD_DIM ** -0.5) for b in seqs]
    ps = [_softmax_rows(jnp.where(same, s, -jnp.inf)).astype(BF16) for s in ss]
    os_ = [_dot(ps[b], mv_ref[b].astype(BF16)) for b in seqs]
    for b in seqs:
        for h in range(X_HEADS):
            o_ref[b, :, h * LANES:(h + 1) * LANES] = os_[b][h * rows:(h + 1) * rows].astype(BF16)


def _xattn_sample(hq, mk, mv, nb=8):
    bsz, rows, _ = hq.shape
    q = pl.BlockSpec((nb, rows, X_WIDTH), lambda i: (i, 0, 0))
    mem = pl.BlockSpec((nb, mk.shape[1], X_HEAD_DIM), lambda i: (i, 0, 0))
    return pl.pallas_call(
        functools.partial(_xattn_sample_kernel, nb=nb),
        grid=(bsz // nb,),
        in_specs=[q, mem, mem],
        out_specs=q,
        out_shape=jax.ShapeDtypeStruct(hq.shape, BF16),
        compiler_params=_cparams("parallel"),
        name="xattn_sample",
    )(hq, mk, mv)


def _xattn_tile(hq_ref, mk_ref, mv_ref):
    heads = [slice(h * LANES, (h + 1) * LANES) for h in range(X_HEADS)]
    ss = [_dot_nt(hq_ref[:, hs], mk_ref[:, hs].astype(BF16)) * (X_HEAD_DIM ** -0.5) for hs in heads]
    ps = [_softmax_rows(s).astype(BF16) for s in ss]
    return jnp.concatenate([_dot(p, mv_ref[:, hs].astype(BF16)).astype(BF16) for p, hs in zip(ps, heads)], axis=1)


def _ffn_kernel(h_ref, *rest, stride, padc, fc, attend):
    if attend:
        hq_ref, mk_ref, mv_ref, *rest = rest
        xo = _xattn_tile(hq_ref, mk_ref, mv_ref)
    else:
        xo_ref, *rest = rest
        xo = xo_ref[...]
    cin_ref, wxo_ref, g_ref, wu_ref, wv_ref, cw_ref, wd_ref, fg_ref, y_ref, tail_ref, xp_ref, acc_ref = rest
    tm = h_ref.shape[0]

    @pl.when(pl.program_id(1) == 0)
    def _():
        tail_ref[...] = cin_ref[...]

    h = h_ref[...] + _dot(xo, wxo_ref[...])
    acc_ref[...] = h
    hn = _rms(h, g_ref[...]).astype(BF16)
    for lo, hi in zip((0,) + fc, fc + (D_FF,)):
        cs = slice(lo, hi)
        n = hi - lo
        u = _dot(hn, wu_ref[:, cs])
        gate = _dot(hn, wv_ref[:, cs])
        xp_ref[0:padc, 0:n] = tail_ref[0, :, cs]
        xp_ref[padc:padc + tm, 0:n] = u
        cw = cw_ref[:, cs]
        uc = u * cw[2:3]
        for j in range(FFN_CONV - 1):
            o = padc - (FFN_CONV - 1 - j) * stride
            uc = uc + xp_ref[o:o + tm, 0:n] * cw[j:j + 1]
        tail_ref[0, :, cs] = xp_ref[tm:tm + padc, 0:n]
        acc_ref[...] += _dot((_silu(uc) * gate).astype(BF16), wd_ref[cs, :])
    y_ref[...] = _rms(acc_ref[...], fg_ref[...])


FFN_CUTS = ()


def _ffn(h, xq, mem, cin, wxo, g, wup, cw, wd, fg, tm, stride, fc=FFN_CUTS):
    m = h.shape[0]
    half = lambda c: pl.BlockSpec((D_MODEL, D_FF), lambda *_: (0, c), pipeline_mode=pl.Buffered(1))
    nseq, padc, _ = cin.shape
    nt = m // nseq // tm
    row = lambda w: pl.BlockSpec((tm, w), lambda s, j: (s * nt + j, 0))
    car = pl.BlockSpec((1, padc, D_FF), lambda s, j: (s, 0, 0))
    mem = list(mem or ())
    mem_specs = [pl.BlockSpec((a.shape[0] // nseq, X_WIDTH), lambda s, j: (s, 0)) for a in mem]
    return pl.pallas_call(
        functools.partial(_ffn_kernel, stride=stride, padc=padc, fc=fc, attend=bool(mem)),
        grid=(nseq, nt),
        in_specs=[row(D_MODEL), row(X_WIDTH), *mem_specs, car, _resident(wxo.shape), _resident((1, D_MODEL)),
                  half(0), half(1), _resident(cw.shape), _resident(wd.shape), _resident((1, D_MODEL))],
        out_specs=[row(D_MODEL), car],
        out_shape=[jax.ShapeDtypeStruct((m, D_MODEL), F32), jax.ShapeDtypeStruct(cin.shape, F32)],
        scratch_shapes=[pltpu.VMEM((tm + padc, max(b - a for a, b in zip((0,) + fc, fc + (D_FF,)))), F32),
                        pltpu.VMEM((tm, D_MODEL), F32)],
        compiler_params=_cparams("parallel", "arbitrary"),
        name="ffn",
    )(h, xq, *mem, cin, wxo, g, wup, wup, cw, wd, fg)


def _lane_row(vec, offset=0):
    return jnp.zeros((1, LANES), F32).at[0, offset:offset + vec.shape[0]].set(vec.astype(F32))


W_IN_QKV = A_WIDTH + 2 * A_KV_WIDTH
W_IN_D = W_IN_QKV + 4 * DN_WIDTH
W_IN_AB = W_IN_D + 2 * DN_HEADS
W_IN_G = W_IN_AB + 2 * D_MODEL


def _split_w_in_kernel(wt_ref, qkv_ref, d_ref, ab_ref, g_ref):
    tk = wt_ref.shape[1]
    qkv_ref[...] = wt_ref[:W_IN_QKV, :].T.astype(BF16)
    d_ref[...] = wt_ref[W_IN_QKV:W_IN_D, :].T.astype(BF16)
    ab = jnp.concatenate([wt_ref[W_IN_D:W_IN_AB, :], jnp.zeros((LANES - (W_IN_AB - W_IN_D), tk), F32)], axis=0)
    ab_ref[...] = ab.T.astype(BF16)
    g_ref[...] = wt_ref[W_IN_AB:W_IN_G, :].T.astype(BF16)


def _split_w_in(w_in_t, tk=256):
    k = w_in_t.shape[1]
    widths = (W_IN_QKV, W_IN_D - W_IN_QKV, LANES, W_IN_G - W_IN_AB)
    return pl.pallas_call(
        _split_w_in_kernel,
        grid=(k // tk,),
        in_specs=[pl.BlockSpec((w_in_t.shape[0], tk), lambda i: (0, i))],
        out_specs=[pl.BlockSpec((tk, wd), lambda i: (i, 0)) for wd in widths],
        out_shape=[jax.ShapeDtypeStruct((k, wd), BF16) for wd in widths],
        compiler_params=_cparams("parallel"),
        name="split_w_in",
    )(w_in_t)


def _prep(p):
    wqkv, wd, wab, wg = _split_w_in(p["w_in"].T)
    w = {
        "wqkv": wqkv,
        "wd": wd,
        "wab": wab,
        "wg": wg,
        "norm_mix_g": p["norm_mix_g"].reshape(1, D_MODEL),
        "dn_conv_w": p["dn_conv_w"],
        "alog": _lane_row(p["dn_a_log"]),
        "dtb": _lane_row(p["dn_dt_bias"]),
        "dn_norm_g": p["dn_norm_g"].reshape(1, DN_HEAD_DIM),
        "sinks": p["sinks"].astype(F32),
        "w_br_a": p["w_br_a"].astype(BF16),
        "w_br_b": p["w_br_b"].astype(BF16),
        "w_mix_out": p["w_mix_out"].astype(BF16),
        "norm_x_g": p["norm_x_g"].reshape(1, D_MODEL),
        "w_xq": p["w_xq"].astype(BF16),
        "w_xo": p["w_xo"].astype(BF16),
        "norm_ffn_g": p["norm_ffn_g"].reshape(1, D_MODEL),
        "w_up": p["w_up"].astype(BF16),
        "ffn_conv_w": p["ffn_conv_w"],
        "w_down": p["w_down"].astype(BF16),
        "final_norm_g": p["final_norm_g"].reshape(1, D_MODEL),
    }
    return w


def _tile(m, pref):
    return pref if m % pref == 0 else m


def _prompt_layer(x, mem, norm_mem_g, w_xkv, w):
    bsz, seqlen, _ = x.shape
    m = bsz * seqlen
    x2 = x.reshape(m, D_MODEL)
    tm = _tile(seqlen, ROW_TILE)
    tabs = _rope_tables(jnp.arange(seqlen, dtype=jnp.int32))
    assert seqlen >= WINDOW, "the window outputs are taken from a full last window"
    q, k, v, ga, gb, kwin, vwin, yb, dn_state, dn_tail = _proj_gdn(
        x2, w["norm_mix_g"], tabs, w["wqkv"], w["wd"], w["wab"], w["wg"], w["dn_conv_w"], w["alog"], w["dtb"],
        w["dn_norm_g"], bsz, tm)
    h, hq = _merge(x2, (w["sinks"], q, k, v), yb, ga, gb, w["w_br_a"], w["w_br_b"], w["w_mix_out"],
                   w["norm_x_g"], w["w_xq"], bsz, tm)
    nm = mem.shape[1]
    mk, mv, mk_rows, mv_rows = _memkv(mem.reshape(bsz * nm, D_MODEL), norm_mem_g.reshape(1, D_MODEL),
                                      w_xkv.astype(BF16), _tile(bsz * nm, ROW_TILE))
    cin = jnp.zeros((bsz, SUBLANES, D_FF), F32)
    y, tail = _ffn(h, hq, (mk, mv), cin, w["w_xo"], w["norm_ffn_g"], w["w_up"], w["ffn_conv_w"],
                   w["w_down"], w["final_norm_g"], tm, 1)
    win = lambda a: jnp.transpose(a.reshape(bsz, A_KV_HEADS, A_HEAD_DIM, WINDOW), (0, 3, 1, 2))
    new = (
        win(kwin),
        win(vwin),
        dn_tail[:, SUBLANES - (DN_CONV - 1):],
        dn_state,
        mk_rows.reshape(bsz, nm, X_HEADS, X_HEAD_DIM),
        mv_rows.reshape(bsz, nm, X_HEADS, X_HEAD_DIM),
        tail[:, SUBLANES - (FFN_CONV - 1):],
    )
    return y.reshape(bsz, seqlen, D_MODEL), new


def _sample_layer(x, pos0, win_k, win_v, dn_buf, dn_state, mem_k, mem_v, ffn_buf, w):
    bsz, t, _ = x.shape
    m = bsz * t
    sr = SAMPLE_ROWS
    x2 = x.reshape(m, D_MODEL)
    tabs = _rope_tables(jnp.tile(pos0 + jnp.arange(t, dtype=jnp.int32), bsz))
    q, k, v, dqkv, dz, ab, ga, gb = _proj(x2, w["norm_mix_g"], tabs, w["wqkv"], w["wd"], w["wab"], w["wg"], m)

    qh = jnp.transpose(q.reshape(bsz, t, A_HEADS, A_HEAD_DIM), (0, 2, 1, 3)).reshape(bsz, A_HEADS * t, A_HEAD_DIM)
    padrows = lambda a: jnp.pad(a.reshape(bsz, t, -1), ((0, 0), (0, sr - t), (0, 0)))
    kv_major = lambda a: jnp.transpose(a, (0, 2, 3, 1))
    nbw = min(bsz, LANES // t)

    def new_t(a):
        a = jnp.transpose(a.reshape(bsz // nbw, nbw * t, A_KV_HEADS, A_HEAD_DIM), (0, 2, 3, 1))
        return jnp.pad(a, ((0, 0), (0, 0), (0, 0), (0, LANES - nbw * t)))

    oh, new_wk, new_wv = _swa_sample(qh, kv_major(win_k), kv_major(win_v), new_t(k), new_t(v), w["sinks"], t)
    ya = jnp.transpose(oh.reshape(bsz, A_HEADS, t, A_HEAD_DIM), (0, 2, 1, 3)).reshape(m, A_WIDTH)
    new_wk, new_wv = (jnp.transpose(a, (0, 3, 1, 2)) for a in (new_wk, new_wv))

    yb, new_s, new_dn_buf = _gdn_sample(
        jnp.transpose(dn_buf, (1, 0, 2)), dqkv.reshape(bsz, t, 3 * DN_WIDTH), dz.reshape(bsz, t, DN_WIDTH),
        ab.reshape(bsz, t, LANES), dn_state, w["dn_conv_w"], w["alog"], w["dtb"], w["dn_norm_g"])
    yb = yb.reshape(m, DN_WIDTH)
    new_dn_buf = jnp.transpose(new_dn_buf, (1, 0, 2))

    h, hq = _merge(x2, ya, yb, ga, gb, w["w_br_a"], w["w_br_b"], w["w_mix_out"], w["norm_x_g"], w["w_xq"], 1, m)
    nm = mem_k.shape[1]
    xo = _xattn_sample(padrows(hq), mem_k.reshape(bsz, nm * X_HEADS, X_HEAD_DIM),
                       mem_v.reshape(bsz, nm * X_HEADS, X_HEAD_DIM))
    xo = xo[:, :t]

    tmaj = lambda a: jnp.transpose(a.reshape(bsz, t, -1), (1, 0, 2)).reshape(m, -1)
    cin = jnp.transpose(ffn_buf, (1, 0, 2)).reshape(1, (FFN_CONV - 1) * bsz, D_FF)
    y, tail = _ffn(tmaj(h), tmaj(xo), None, cin, w["w_xo"], w["norm_ffn_g"], w["w_up"], w["ffn_conv_w"],
                   w["w_down"], w["final_norm_g"], m, bsz)
    y = jnp.transpose(y.reshape(t, bsz, D_MODEL), (1, 0, 2))
    new_ffn = jnp.transpose(tail.reshape(FFN_CONV - 1, bsz, D_FF), (1, 0, 2))
    return y, (new_wk, new_wv, new_dn_buf, new_s, new_ffn)


def kernel(x_prompt, x_sample, mem_prompt, cache_win_k, cache_win_v, state_dn_conv, state_dn, cache_mem_k, cache_mem_v, state_ffn_conv, norm_mix_g, w_in, dn_conv_w, dn_a_log, dn_dt_bias, dn_norm_g, attn_sinks, w_br_a, w_br_b, w_mix_out, norm_x_g, norm_mem_g, w_xq, w_xkv, w_xo, norm_ffn_g, w_up, ffn_conv_w, w_down, final_norm_g):
    p = {"norm_mix_g": norm_mix_g[0], "w_in": w_in[0], "dn_conv_w": dn_conv_w[0], "dn_a_log": dn_a_log[0],
         "dn_dt_bias": dn_dt_bias[0], "dn_norm_g": dn_norm_g[0], "sinks": attn_sinks[0], "w_br_a": w_br_a[0],
         "w_br_b": w_br_b[0], "w_mix_out": w_mix_out[0], "norm_x_g": norm_x_g[0], "w_xq": w_xq[0],
         "w_xo": w_xo[0], "norm_ffn_g": norm_ffn_g[0], "w_up": w_up[0], "ffn_conv_w": ffn_conv_w[0],
         "w_down": w_down[0], "final_norm_g": final_norm_g}
    w = _prep(p)
    yp, newp = _prompt_layer(x_prompt, mem_prompt, norm_mem_g[0], w_xkv[0], w)
    ys, news = _sample_layer(x_sample, PAST_LEN, cache_win_k[0], cache_win_v[0], state_dn_conv[0], state_dn[0],
                             cache_mem_k[0], cache_mem_v[0], state_ffn_conv[0], w)
    lead = lambda a: a[None]
    p_win_k, p_win_v, p_dn_conv, p_dn_state, p_mem_k, p_mem_v, p_ffn_conv = [lead(a) for a in newp]
    s_win_k, s_win_v, s_dn_conv, s_dn_state, s_ffn_conv = [lead(a) for a in news]
    return (yp, ys, p_win_k, p_win_v, p_dn_conv, p_dn_state, p_mem_k, p_mem_v, p_ffn_conv,
            s_win_k, s_win_v, s_dn_conv, s_dn_state, s_ffn_conv)
```

```python
import functools

import jax
import jax.numpy as jnp
from jax import lax
from jax.experimental import pallas as pl
from jax.experimental.pallas import tpu as pltpu

F32 = jnp.float32
BF16 = jnp.bfloat16

D_MODEL = 1024
A_HEADS = 8
A_KV_HEADS = 2
A_HEAD_DIM = 64
A_WIDTH = 512
A_KV_WIDTH = 128
WINDOW = 128
ROT_DIM = 16
ROPE_THETA = 500000.0
DN_HEADS = 4
DN_HEAD_DIM = 128
DN_WIDTH = 512
DN_CONV = 4
X_HEADS = 4
X_HEAD_DIM = 128
X_WIDTH = 512
D_FF = 2816
FFN_CONV = 3
EPS = 1e-6
PAST_LEN = 16384

LANES = 128
SUBLANES = 8
VMEM_LIMIT = 56 * 1024 * 1024
ROW_TILE = 512
CHUNK = 128
SAMPLE_ROWS = 16
GDN_SAMPLE_ROWS = 8


def _cparams(*sem):
    return pltpu.CompilerParams(dimension_semantics=sem, vmem_limit_bytes=VMEM_LIMIT)


def _resident(shape):
    return pl.BlockSpec(shape, lambda *_: (0,) * len(shape), pipeline_mode=pl.Buffered(1))


def _rms(x, g):
    return x * lax.rsqrt(jnp.mean(x * x, axis=-1, keepdims=True) + EPS) * g


def _dot(a, b):
    return jnp.dot(a, b, preferred_element_type=F32)


def _dot_nt(a, b):
    return lax.dot_general(a, b, (((1,), (1,)), ((), ())), preferred_element_type=F32)


def _silu(x):
    return x * jax.nn.sigmoid(x)


def _rope(seg, c, s1, s2):
    return seg * c + pltpu.roll(seg, LANES - 8, 1) * s1 + pltpu.roll(seg, 8, 1) * s2


PROJ_COLS = 512
assert (3 * DN_WIDTH) % PROJ_COLS == 0 and D_MODEL % PROJ_COLS == 0


def _proj_tile(x_ref, g_ref, c_ref, s1_ref, s2_ref, wqkv_ref, wd_ref, wab_ref, wg_ref,
               q_ref, k_ref, v_ref, dqkv_ref, dz_ref, ab_ref, ga_ref, gb_ref, dqkv_row0=0, win_refs=None):
    tm = x_ref.shape[0]
    xb = _rms(x_ref[...], g_ref[...]).astype(BF16)
    yield
    c, s1, s2 = c_ref[...], s1_ref[...], s2_ref[...]
    z = _dot(xb, wqkv_ref[...])
    for i in range(A_WIDTH // LANES):
        sl = slice(i * LANES, (i + 1) * LANES)
        q_ref[:, sl] = _rope(z[:, sl], c, s1, s2).astype(BF16)
    k = _rope(z[:, A_WIDTH:A_WIDTH + LANES], c, s1, s2)
    k_ref[...] = k
    v_ref[...] = z[:, A_WIDTH + LANES:]
    if win_refs is not None:
        win_refs[0][0] = k[tm - WINDOW:].T
        win_refs[1][0] = z[tm - WINDOW:, A_WIDTH + LANES:].T
    yield
    pc = PROJ_COLS
    for j in range(3 * DN_WIDTH // pc):
        dqkv_ref[dqkv_row0:dqkv_row0 + tm, j * pc:(j + 1) * pc] = _dot(xb, wd_ref[:, j * pc:(j + 1) * pc])
        yield
    dz_ref[...] = _dot(xb, wd_ref[:, 3 * DN_WIDTH:])
    ab_ref[...] = _dot(xb, wab_ref[...])
    yield
    for dst, base in ((ga_ref, 0), (gb_ref, D_MODEL)):
        for j in range(D_MODEL // pc):
            dst[:, j * pc:(j + 1) * pc] = _dot(xb, wg_ref[:, base + j * pc:base + (j + 1) * pc])
            yield


def _proj_kernel(*refs):
    _drain(_proj_tile(*refs))


def _proj(x, g, tabs, wqkv, wd, wab, wg, tm):
    m = x.shape[0]
    nt = tabs[0].shape[0] // tm
    row = lambda w: pl.BlockSpec((tm, w), lambda i: (i, 0))
    tab = pl.BlockSpec((tm, LANES), lambda i: (i % nt, 0))
    widths = (A_WIDTH, LANES, LANES, 3 * DN_WIDTH, DN_WIDTH, LANES, D_MODEL, D_MODEL)
    dts = (BF16, F32, F32, F32, F32, F32, F32, F32)
    return pl.pallas_call(
        _proj_kernel,
        grid=(m // tm,),
        in_specs=[row(D_MODEL), _resident((1, D_MODEL)), tab, tab, tab,
                  _resident(wqkv.shape), _resident(wd.shape), _resident(wab.shape), _resident(wg.shape)],
        out_specs=[row(w) for w in widths],
        out_shape=[jax.ShapeDtypeStruct((m, w), d) for w, d in zip(widths, dts)],
        compiler_params=_cparams("parallel"),
        name="in_proj",
    )(x, g, *tabs, wqkv, wd, wab, wg)


def _rope_tables(pos):
    half = ROT_DIM // 2
    d = jnp.arange(LANES, dtype=jnp.int32) % A_HEAD_DIM
    inv = ROPE_THETA ** (-2.0 * (d % half).astype(F32) / ROT_DIM)
    ang = pos.astype(F32)[:, None] * inv[None, :]
    c, s = jnp.cos(ang), jnp.sin(ang)
    return (jnp.where(d < ROT_DIM, c, 1.0), jnp.where(d < half, -s, 0.0),
            jnp.where((d >= half) & (d < ROT_DIM), s, 0.0))


def _both_halves(t, lane_lo):
    r = pltpu.roll(t, A_HEAD_DIM, 1)
    return jnp.where(lane_lo, t, r), jnp.where(lane_lo, r, t)


def _sink_softmax(s, valid, sink):
    s = jnp.where(valid, s, -jnp.inf)
    m = jnp.maximum(jnp.max(s, axis=-1, keepdims=True), sink)
    p = jnp.exp(s - m)
    den = jnp.sum(p, axis=-1, keepdims=True) + jnp.exp(sink - m)
    return p * (1.0 / den)


def _swa_tile(sink_ref, q_ref, k, v, first):
    w = WINDOW
    lane_lo_k = lax.broadcasted_iota(jnp.int32, k.shape, 1) < A_HEAD_DIM
    kkt = [t.T.astype(BF16) for t in _both_halves(k, lane_lo_k)]
    vv = [t.astype(BF16) for t in _both_halves(v, lane_lo_k)]
    lane_lo = lax.broadcasted_iota(jnp.int32, (w, LANES), 1) < A_HEAD_DIM
    row = lax.broadcasted_iota(jnp.int32, (4 * w, 2 * w), 0)
    col = lax.broadcasted_iota(jnp.int32, (4 * w, 2 * w), 1)
    d = (row & (w - 1)) + w - col
    band = (d >= 0) & (d < w)
    band_first = band & (col >= jnp.where(first, w, 0))
    hrow = lax.broadcasted_iota(jnp.int32, (4 * w, 1), 0) // w
    sinks = []
    for g in range(A_KV_HEADS):
        sink = jnp.zeros((4 * w, 1), F32)
        for j in range(4):
            sink = jnp.where(hrow == j, sink_ref[4 * g + j], sink)
        sinks.append(sink)
    zero = jnp.zeros((), BF16)
    scale = jnp.asarray(A_HEAD_DIM ** -0.5, BF16)
    nblk = q_ref.shape[0] // w
    probs = [(b, g) for b in range(nblk) for g in range(A_KV_HEADS)]

    def queries(b, g):
        parts = []
        for sgm in range(2):
            seg = q_ref[b * w:(b + 1) * w, (2 * g + sgm) * LANES:(2 * g + sgm + 1) * LANES] * scale
            parts += [jnp.where(lane_lo, seg, zero), jnp.where(lane_lo, zero, seg)]
        return jnp.concatenate(parts, axis=0)

    ss = [_dot(queries(b, g), kkt[g][:, b * w:(b + 2) * w]) for b, g in probs]
    yield
    ps = [_sink_softmax(s, band_first if b == 0 else band, sinks[g]).astype(BF16) for s, (b, g) in zip(ss, probs)]
    yield
    os_ = [_dot(p, vv[g][b * w:(b + 2) * w]) for p, (b, g) in zip(ps, probs)]
    yield
    segs = [[jnp.where(lane_lo, o[(2 * sgm) * w:(2 * sgm + 1) * w], o[(2 * sgm + 1) * w:(2 * sgm + 2) * w]
                       ).astype(BF16) for sgm in range(2)] for o in os_]
    return jnp.concatenate([jnp.concatenate(segs[A_KV_HEADS * b] + segs[A_KV_HEADS * b + 1], axis=1)
                            for b in range(nblk)], axis=0)


def _swa_sample_kernel(sink_ref, q_ref, ckt_ref, cvt_ref, knt_ref, vnt_ref, o_ref, cko_ref, cvo_ref, *, nb, t):
    w = ckt_ref.shape[3]
    old = lax.broadcasted_iota(jnp.int32, (A_HEAD_DIM, w), 1) < w - t
    for src_ref, new_ref, dst_ref in ((ckt_ref, knt_ref, cko_ref), (cvt_ref, vnt_ref, cvo_ref)):
        for g in range(A_KV_HEADS):
            new = new_ref[0, g]
            for b in range(nb):
                dst_ref[b, g] = jnp.where(old, pltpu.roll(src_ref[b, g], w - t, 1),
                                          pltpu.roll(new, (w - t - t * b) % LANES, 1))
    rows = q_ref.shape[1] // A_HEADS
    gq = rows * (A_HEADS // A_KV_HEADS)
    r = lax.broadcasted_iota(jnp.int32, (gq, w), 0)
    c = lax.broadcasted_iota(jnp.int32, (gq, w), 1)
    valid_c = c > r % rows
    rn = lax.broadcasted_iota(jnp.int32, (gq, LANES), 0)
    cn = lax.broadcasted_iota(jnp.int32, (gq, LANES), 1)
    causal_n = cn % t <= rn % rows
    valid_n = [causal_n & (cn // t == b) for b in range(nb)]
    hrow = lax.broadcasted_iota(jnp.int32, (gq, 1), 0) // rows
    sink = []
    for g in range(A_KV_HEADS):
        sk = jnp.zeros((gq, 1), F32)
        for j in range(A_HEADS // A_KV_HEADS):
            sk = jnp.where(hrow == j, sink_ref[g * (A_HEADS // A_KV_HEADS) + j], sk)
        sink.append(sk)
    probs = [(b, g) for b in range(nb) for g in range(A_KV_HEADS)]
    scale = A_HEAD_DIM ** -0.5
    qs = [q_ref[b, g * gq:(g + 1) * gq, :] for b, g in probs]
    scs = [jnp.where(valid_c, _dot(q, ckt_ref[b, g].astype(BF16)) * scale, -jnp.inf)
           for q, (b, g) in zip(qs, probs)]
    knt = [knt_ref[0, g].astype(BF16) for g in range(A_KV_HEADS)]
    vnt = [vnt_ref[0, g].astype(BF16) for g in range(A_KV_HEADS)]
    sns = [jnp.where(valid_n[b], _dot(q, knt[g]) * scale, -jnp.inf) for q, (b, g) in zip(qs, probs)]
    ms = [jnp.maximum(jnp.maximum(jnp.max(sc, -1, keepdims=True), jnp.max(sn, -1, keepdims=True)), sink[g])
          for sc, sn, (b, g) in zip(scs, sns, probs)]
    pcs = [jnp.exp(sc - m) for sc, m in zip(scs, ms)]
    pns = [jnp.exp(sn - m) for sn, m in zip(sns, ms)]
    invs = [1.0 / (jnp.sum(pc, -1, keepdims=True) + jnp.sum(pn, -1, keepdims=True) + jnp.exp(sink[g] - m))
            for pc, pn, m, (b, g) in zip(pcs, pns, ms, probs)]
    for pc, pn, inv, (b, g) in zip(pcs, pns, invs, probs):
        o_ref[b, g * gq:(g + 1) * gq, :] = (_dot_nt((pc * inv).astype(BF16), cvt_ref[b, g].astype(BF16))
                                            + _dot_nt((pn * inv).astype(BF16), vnt[g]))


def _swa_sample(q, ckt, cvt, knt, vnt, sinks, t):
    bsz, nq, _ = q.shape
    nb = bsz // knt.shape[0]
    assert nb * t <= LANES and ckt.shape[3] == LANES
    blk = lambda a, n=nb: pl.BlockSpec((n,) + a.shape[1:], lambda i: (i,) + (0,) * (a.ndim - 1))
    return pl.pallas_call(
        functools.partial(_swa_sample_kernel, nb=nb, t=t),
        grid=(bsz // nb,),
        in_specs=[pl.BlockSpec(memory_space=pltpu.SMEM), blk(q), blk(ckt), blk(cvt), blk(knt, 1), blk(vnt, 1)],
        out_specs=[blk(q), blk(ckt), blk(cvt)],
        out_shape=[jax.ShapeDtypeStruct(q.shape, F32), jax.ShapeDtypeStruct(ckt.shape, F32),
                   jax.ShapeDtypeStruct(cvt.shape, F32)],
        compiler_params=_cparams("parallel"),
        name="swa_sample",
    )(sinks, q, ckt, cvt, knt, vnt)


def _lane_bcast(x, lane):
    return jnp.broadcast_to(x[:, lane:lane + 1], (x.shape[0], LANES))


def _cumsum_rows(x, block):
    c = x.shape[0]
    rowi = lax.broadcasted_iota(jnp.int32, (c, c), 0)
    coli = lax.broadcasted_iota(jnp.int32, (c, c), 1)
    ones = (((rowi // block) == (coli // block)) & (rowi >= coli)).astype(BF16)
    hi = x.astype(BF16)
    r1 = x - hi.astype(F32)
    mid = r1.astype(BF16)
    lo = (r1 - mid.astype(F32)).astype(BF16)
    s = _dot(ones, jnp.concatenate([hi, mid, lo], axis=1))
    return s[:, :LANES] + s[:, LANES:2 * LANES] + s[:, 2 * LANES:]


def _l2n(t):
    return t * lax.rsqrt(jnp.sum(t * t, axis=-1, keepdims=True) + EPS)


def _gates(ab, alog, dtb):
    x = ab + dtb
    sp = jnp.maximum(x, 0.0) + jnp.log1p(jnp.exp(-jnp.abs(x)))
    return -jnp.exp(alog) * sp, jax.nn.sigmoid(ab)


def _merge_masks(c, top):
    rowi = lax.broadcasted_iota(jnp.int32, (c, c), 0)
    coli = lax.broadcasted_iota(jnp.int32, (c, c), 1)
    masks = []
    s = 1
    while s < top:
        rb, cb = rowi // s, coli // s
        masks.append(((rb // 2) == (cb // 2)) & ((rb % 2) == 1) & ((cb % 2) == 0))
        s *= 2
    return masks


def _each(f, *lists):
    return [f(*t) for t in zip(*lists)]


def _drain(stages):
    try:
        while True:
            next(stages)
    except StopIteration as done:
        return done.value


def _interleave(*staged):
    live = list(staged)
    values = {}
    while live:
        for item in tuple(live):
            stages, per_turn = item
            try:
                for _ in range(per_turn):
                    next(stages)
            except StopIteration as done:
                values[id(stages)] = done.value
                live.remove(item)
    return [values[id(stages)] for stages, _ in staged]


def _chunk_local(qs, ks, vs, gcols, grows, betas, tril, merges):
    c = qs[0].shape[0]
    decays = _each(lambda gc, gr: jnp.exp(jnp.minimum(gc - gr, 0.0)), gcols, grows)
    kbs = _each(lambda k, b: k * b, ks, betas)
    yield
    kts = [k.T for k in ks]
    ms = _each(lambda q, kb, kt: _dot(jnp.concatenate([q, kb], axis=0).astype(BF16), kt.astype(BF16)),
               qs, kbs, kts)
    yield
    qks = _each(lambda m, d: jnp.where(tril, m[:c] * d, 0.0), ms, decays)
    a = _each(lambda m, d: m[c:] * d, ms, decays)
    eye = (lax.broadcasted_iota(jnp.int32, (c, c), 0) == lax.broadcasted_iota(jnp.int32, (c, c), 1)).astype(F32)
    ts = [eye - jnp.where(merges[0], x, 0.0) if merges else eye for x in a]
    yield
    for off in merges[1:]:
        tbs = [t.astype(BF16) for t in ts]
        zs = _each(lambda x, tb: _dot(jnp.where(off, x, 0.0).astype(BF16), tb), a, tbs)
        yield
        ts = _each(lambda t, tb, z: t - _dot(tb, z.astype(BF16)), ts, tbs, zs)
        yield
    ns = [t - eye for t in ts]
    egs = [jnp.exp(gc) for gc in gcols]
    rhss = _each(lambda v, b, kb, eg: jnp.concatenate([v * b, kb * eg], axis=1), vs, betas, kbs, egs)
    yield
    uws = _each(lambda r, n: r + _dot(n.astype(BF16), r.astype(BF16)), rhss, ns)
    yield
    return ([x[:, :LANES] for x in uws], [x[:, LANES:] for x in uws], qks,
            _each(lambda q, eg: q * eg, qs, egs), kts)


def _gdn_tile(xp_ref, dz_ref, ab_ref, cw_ref, alog_ref, dtb_ref, ng_ref, y_ref, s_ref, nc):
    c = CHUNK
    pad = SUBLANES
    cw = cw_ref[...]
    g, beta = _gates(ab_ref[...], alog_ref[...], dtb_ref[...])
    rowi = lax.broadcasted_iota(jnp.int32, (c, c), 0)
    coli = lax.broadcasted_iota(jnp.int32, (c, c), 1)
    tril = rowi >= coli
    merges = _merge_masks(c, c)
    ng = ng_ref[...]
    heads = range(DN_HEADS)
    probs = [(ci, h) for ci in range(nc) for h in heads]
    rows = lambda ci: slice(ci * c, (ci + 1) * c)
    lanes = lambda part, h: slice(part * DN_WIDTH + h * LANES, part * DN_WIDTH + (h + 1) * LANES)
    yield
    qs, ks, vs, gcols, grows, betas = [], [], [], [], [], []
    for ci in range(nc):
        conv = xp_ref[pad + ci * c:pad + (ci + 1) * c, :] * cw[DN_CONV - 1:DN_CONV]
        for j in range(DN_CONV - 1):
            o = pad - (DN_CONV - 1) + j + ci * c
            conv = conv + xp_ref[o:o + c, :] * cw[j:j + 1]
        conv = _silu(conv)
        gc = _cumsum_rows(g[rows(ci)], c)
        gct = gc.T
        qs += [_l2n(conv[:, lanes(0, h)]) * (DN_HEAD_DIM ** -0.5) for h in heads]
        ks += [_l2n(conv[:, lanes(1, h)]) for h in heads]
        vs += [conv[:, lanes(2, h)] for h in heads]
        gcols += [_lane_bcast(gc, h) for h in heads]
        grows += [gct[h:h + 1, :] for h in heads]
        betas += [_lane_bcast(beta[rows(ci)], DN_HEADS + h) for h in heads]
        yield
    us, ws, qks, qds, kts = yield from _chunk_local(qs, ks, vs, gcols, grows, betas, tril, merges)
    glasts = [gc[c - 1:c, :] for gc in gcols]
    kdts = _each(lambda kt, gr: kt * jnp.exp(gr[:, c - 1:c] - gr), kts, grows)
    wqs = _each(lambda w, qd: jnp.concatenate([w, qd], axis=0).astype(BF16), ws, qds)
    qkks = _each(lambda qk, kdt: jnp.concatenate([qk, kdt], axis=0).astype(BF16), qks, kdts)
    yield
    ss = [s_ref[h] for h in heads]
    for ci in range(nc):
        pr = [ci * DN_HEADS + h for h in heads]
        r2s = [_dot(wqs[p], s.astype(BF16)) for p, s in zip(pr, ss)]
        vnews = [us[p] - r2[:c] for p, r2 in zip(pr, r2s)]
        yield
        r3s = [_dot(qkks[p], vn.astype(BF16)) for p, vn in zip(pr, vnews)]
        ss = [s * jnp.exp(glasts[p]) + r3[c:] for p, s, r3 in zip(pr, ss, r3s)]
        for h in heads:
            o = r2s[h][c:] + r3s[h][:c]
            y_ref[rows(ci), lanes(0, h)] = (_rms(o, ng) * _silu(dz_ref[rows(ci), lanes(0, h)])).astype(BF16)
        yield
    for h in heads:
        s_ref[h] = ss[h]


def _proj_gdn_kernel(x_ref, g_ref, c_ref, s1_ref, s2_ref, wqkv_ref, wd_ref, wab_ref, wg_ref,
                     cw_ref, alog_ref, dtb_ref, ng_ref,
                     q_ref, k_ref, v_ref, ga_ref, gb_ref, kw_ref, vw_ref, y_ref, sout_ref, tail_ref,
                     xp_ref, dz_ref, ab_ref, carry_ref, s_ref, *, nt, nc):
    i = pl.program_id(0)
    pad = SUBLANES
    r = nc * CHUNK
    slot_a = i % 2
    slot_b = 1 - slot_a

    @pl.when(i == 0)
    def _():
        xp_ref[1] = jnp.zeros(xp_ref.shape[1:], F32)
        dz_ref[1] = jnp.zeros(dz_ref.shape[1:], F32)
        ab_ref[1] = jnp.zeros(ab_ref.shape[1:], F32)

    @pl.when(jnp.maximum(i - 1, 0) % nt == 0)
    def _():
        carry_ref[...] = jnp.zeros_like(carry_ref)
        s_ref[...] = jnp.zeros_like(s_ref)

    xpb_ref = xp_ref.at[slot_b]
    xpb_ref[0:pad, :] = carry_ref[...]
    _interleave(
        (_gdn_tile(xpb_ref, dz_ref.at[slot_b], ab_ref.at[slot_b], cw_ref, alog_ref, dtb_ref, ng_ref,
                   y_ref, s_ref, nc), 1),
        (_proj_tile(x_ref, g_ref, c_ref, s1_ref, s2_ref, wqkv_ref, wd_ref, wab_ref, wg_ref, q_ref, k_ref, v_ref,
                    xp_ref.at[slot_a], dz_ref.at[slot_a], ab_ref.at[slot_a], ga_ref, gb_ref, pad,
                    (kw_ref, vw_ref)), 1))
    carry_ref[...] = xpb_ref[r:r + pad, :]
    tail_ref[0] = carry_ref[...]
    sout_ref[0] = s_ref[...]


def _proj_gdn(x, g, tabs, wqkv, wd, wab, wg, cw, alog, dtb, ng, nseq, tm):
    m = x.shape[0]
    nt = m // nseq // tm
    last = m // tm - 1
    nc = tm // CHUNK
    proj = lambda i: jnp.minimum(i, last)
    gdn = lambda i: jnp.maximum(i - 1, 0)
    prow = lambda w: pl.BlockSpec((tm, w), lambda i: (proj(i), 0))
    tab = pl.BlockSpec((tm, LANES), lambda i: (proj(i) % nt, 0))
    per_seq = lambda *dims: pl.BlockSpec((1,) + dims, lambda i: (gdn(i) // nt,) + (0,) * len(dims))
    win = pl.BlockSpec((1, A_KV_WIDTH, WINDOW), lambda i: (proj(i) // nt, 0, 0))
    win_shape = jax.ShapeDtypeStruct((nseq, A_KV_WIDTH, WINDOW), F32)
    state = (DN_HEADS, DN_HEAD_DIM, DN_HEAD_DIM)
    bufs = [pltpu.VMEM((2, tm + SUBLANES, 3 * DN_WIDTH), F32), pltpu.VMEM((2, tm, DN_WIDTH), F32),
            pltpu.VMEM((2, tm, LANES), F32)]
    return pl.pallas_call(
        functools.partial(_proj_gdn_kernel, nt=nt, nc=nc),
        grid=(m // tm + 1,),
        in_specs=[prow(D_MODEL), _resident((1, D_MODEL)), tab, tab, tab,
                  _resident(wqkv.shape), _resident(wd.shape), _resident(wab.shape), _resident(wg.shape),
                  _resident(cw.shape), _resident((1, LANES)), _resident((1, LANES)), _resident((1, LANES))],
        out_specs=[prow(A_WIDTH), prow(LANES), prow(LANES), prow(D_MODEL), prow(D_MODEL), win, win,
                   pl.BlockSpec((tm, DN_WIDTH), lambda i: (gdn(i), 0)), per_seq(*state),
                   per_seq(SUBLANES, 3 * DN_WIDTH)],
        out_shape=[jax.ShapeDtypeStruct((m, A_WIDTH), BF16), jax.ShapeDtypeStruct((m, LANES), F32),
                   jax.ShapeDtypeStruct((m, LANES), F32), jax.ShapeDtypeStruct((m, D_MODEL), F32),
                   jax.ShapeDtypeStruct((m, D_MODEL), F32), win_shape, win_shape,
                   jax.ShapeDtypeStruct((m, DN_WIDTH), BF16),
                   jax.ShapeDtypeStruct((nseq,) + state, F32),
                   jax.ShapeDtypeStruct((nseq, SUBLANES, 3 * DN_WIDTH), F32)],
        scratch_shapes=bufs + [pltpu.VMEM((SUBLANES, 3 * DN_WIDTH), F32), pltpu.VMEM(state, F32)],
        compiler_params=_cparams("arbitrary"),
        name="proj_gdn",
    )(x, g, *tabs, wqkv, wd, wab, wg, cw, alog, dtb, ng)


def _gdn_sample_kernel(buf_ref, x_ref, dz_ref, ab_ref, s0_ref, cw_ref, alog_ref, dtb_ref, ng_ref,
                       y_ref, sout_ref, bufo_ref, xp_ref, ab16_ref):
    t = x_ref.shape[1]
    c = CHUNK
    sr = GDN_SAMPLE_ROWS
    nb = c // sr
    pad = SUBLANES
    hist = DN_CONV - 1
    xp_ref[...] = jnp.zeros_like(xp_ref)
    ab16_ref[...] = jnp.zeros_like(ab16_ref)
    for b in range(nb):
        for j in range(hist):
            xp_ref[pad + b * sr - hist + j:pad + b * sr - hist + j + 1, :] = buf_ref[j, b:b + 1, :]
        xp_ref[pad + b * sr:pad + b * sr + t, :] = x_ref[b]
        ab16_ref[b * sr:b * sr + t, :] = ab_ref[b]
    for b in range(nb):
        for j in range(hist):
            o = pad + b * sr - hist + t + j
            bufo_ref[j, b:b + 1, :] = xp_ref[o:o + 1, :]
    cw = cw_ref[...]
    conv = xp_ref[pad:pad + c, :] * cw[hist:hist + 1]
    for j in range(hist):
        conv = conv + xp_ref[pad - hist + j:pad - hist + j + c, :] * cw[j:j + 1]
    conv = _silu(conv)
    live = (lax.broadcasted_iota(jnp.int32, (c, LANES), 0) % sr) < t
    g, beta = _gates(ab16_ref[...], alog_ref[...], dtb_ref[...])
    g = jnp.where(live, g, 0.0)
    beta = jnp.where(live, beta, 0.0)
    gc = _cumsum_rows(g, sr)
    gct = gc.T
    rowi = lax.broadcasted_iota(jnp.int32, (c, c), 0)
    coli = lax.broadcasted_iota(jnp.int32, (c, c), 1)
    tril = ((rowi // sr) == (coli // sr)) & (rowi >= coli)
    merges = _merge_masks(c, pl.next_power_of_2(t))
    ng = ng_ref[...]
    rowb = lax.broadcasted_iota(jnp.int32, (c, LANES), 0) // sr
    heads = range(DN_HEADS)
    lanes = lambda part, h: slice(part * DN_WIDTH + h * LANES, part * DN_WIDTH + (h + 1) * LANES)
    ks = [_l2n(conv[:, lanes(1, h)]) for h in heads]
    gcols = [_lane_bcast(gc, h) for h in heads]
    us, ws, qks, qds, _ = _drain(_chunk_local(
        [_l2n(conv[:, lanes(0, h)]) * (DN_HEAD_DIM ** -0.5) for h in heads], ks,
        [conv[:, lanes(2, h)] for h in heads], gcols, [gct[h:h + 1, :] for h in heads],
        [_lane_bcast(beta, DN_HEADS + h) for h in heads], tril, merges))
    seqs = range(nb)
    rows = lambda b: slice(b * sr, (b + 1) * sr)
    s0s = [[s0_ref[b, h] for b in seqs] for h in heads]
    r2s = [[_dot(jnp.concatenate([ws[h][rows(b)], qds[h][rows(b)]], axis=0).astype(BF16), s0s[h][b].astype(BF16))
            for b in seqs] for h in heads]
    vnews = [jnp.concatenate([us[h][rows(b)] - r2s[h][b][:sr] for b in seqs], axis=0) for h in heads]
    os_ = [_rms(jnp.concatenate([r2s[h][b][sr:] for b in seqs], axis=0)
                + _dot(qks[h].astype(BF16), vnews[h].astype(BF16)), ng) for h in heads]
    for h in heads:
        for b in seqs:
            y_ref[b, :, lanes(0, h)] = os_[h][b * sr:b * sr + t] * _silu(dz_ref[b, :, lanes(0, h)])
    glasts = [jnp.concatenate([jnp.broadcast_to(gcols[h][(b + 1) * sr - 1:(b + 1) * sr, :], (sr, LANES))
                               for b in seqs], axis=0) for h in heads]
    kdts = [(ks[h] * jnp.exp(glasts[h] - gcols[h])).T.astype(BF16) for h in heads]
    for h in heads:
        for b in seqs:
            vb = jnp.where(rowb == b, vnews[h], 0.0).astype(BF16)
            sout_ref[b, h] = s0s[h][b] * jnp.exp(glasts[h][b * sr:b * sr + 1, :]) + _dot(kdts[h], vb)


def _gdn_sample(buf, x, dz, ab, s0, cw, alog, dtb, ng):
    bsz, t, _ = x.shape
    assert t + DN_CONV - 1 <= GDN_SAMPLE_ROWS, "too many new tokens for one row tile per sequence"
    nb = CHUNK // GDN_SAMPLE_ROWS
    seq = lambda a: pl.BlockSpec((nb,) + a.shape[1:], lambda i: (i,) + (0,) * (a.ndim - 1))
    hist = pl.BlockSpec((buf.shape[0], nb, buf.shape[2]), lambda i: (0, i, 0))
    return pl.pallas_call(
        _gdn_sample_kernel,
        grid=(bsz // nb,),
        in_specs=[hist, seq(x), seq(dz), seq(ab), seq(s0), _resident(cw.shape),
                  _resident((1, LANES)), _resident((1, LANES)), _resident((1, LANES))],
        out_specs=[seq(dz), seq(s0), hist],
        out_shape=[jax.ShapeDtypeStruct(dz.shape, F32), jax.ShapeDtypeStruct(s0.shape, F32),
                   jax.ShapeDtypeStruct(buf.shape, F32)],
        scratch_shapes=[pltpu.VMEM((CHUNK + SUBLANES, 3 * DN_WIDTH), F32), pltpu.VMEM((CHUNK, LANES), F32)],
        compiler_params=_cparams("parallel"),
        name="gdn_sample",
    )(buf, x, dz, ab, s0, cw, alog, dtb, ng)


def _merge_kernel(x_ref, *rest, attend):
    if attend:
        sink_ref, q_ref, kp_ref, kc_ref, vp_ref, vc_ref, *rest = rest
    else:
        ya_ref, *rest = rest
    yb_ref, ga_ref, gb_ref, wa_ref, wb_ref, wo_ref, g_ref, wq_ref, h_ref, hq_ref = rest

    def deltanet_half():
        yb = _dot(yb_ref[...].astype(BF16), wb_ref[...])
        yield
        gate_a = jax.nn.sigmoid(ga_ref[...])
        yield
        return gate_a, jax.nn.sigmoid(gb_ref[...]) * yb

    if attend:
        ya, (gate_a, mix_b) = _interleave(
            (_swa_tile(sink_ref, q_ref, jnp.concatenate([kp_ref[...], kc_ref[...]], axis=0),
                       jnp.concatenate([vp_ref[...], vc_ref[...]], axis=0), pl.program_id(1) == 0), 1),
            (deltanet_half(), 1))
    else:
        ya = ya_ref[...]
        gate_a, mix_b = _drain(deltanet_half())
    mix = gate_a * _dot(ya.astype(BF16), wa_ref[...]) + mix_b
    h = x_ref[...] + _dot(mix.astype(BF16), wo_ref[...])
    h_ref[...] = h
    hq_ref[...] = _dot(_rms(h, g_ref[...]).astype(BF16), wq_ref[...]).astype(BF16)


def _merge(x, attn, yb, ga, gb, wa, wb, wo, g, wq, nseq, tm):
    m = x.shape[0]
    nt = m // nseq // tm
    row = lambda w: pl.BlockSpec((tm, w), lambda s, j: (s * nt + j, 0))
    attend = isinstance(attn, tuple)
    if attend:
        sinks, q, k, v = attn
        per = tm // WINDOW
        prev = pl.BlockSpec((WINDOW, A_KV_WIDTH), lambda s, j: (jnp.maximum((s * nt + j) * per - 1, 0), 0))
        attn_args = [sinks, q, k, k, v, v]
        attn_specs = [pl.BlockSpec(memory_space=pltpu.SMEM), row(A_WIDTH), prev, row(A_KV_WIDTH),
                      prev, row(A_KV_WIDTH)]
    else:
        attn_args, attn_specs = [attn], [row(A_WIDTH)]
    return pl.pallas_call(
        functools.partial(_merge_kernel, attend=attend),
        grid=(nseq, nt),
        in_specs=[row(D_MODEL), *attn_specs, row(DN_WIDTH), row(D_MODEL), row(D_MODEL),
                  _resident(wa.shape), _resident(wb.shape), _resident(wo.shape),
                  _resident((1, D_MODEL)), _resident(wq.shape)],
        out_specs=[row(D_MODEL), row(X_WIDTH)],
        out_shape=[jax.ShapeDtypeStruct((m, D_MODEL), F32), jax.ShapeDtypeStruct((m, X_WIDTH), BF16)],
        compiler_params=_cparams("parallel", "parallel"),
        name="merge",
    )(x, *attn_args, yb, ga, gb, wa, wb, wo, g, wq)


def _memkv_kernel(m_ref, g_ref, w_ref, k_ref, v_ref, kf_ref, vf_ref):
    tm = m_ref.shape[0]
    z = _dot(_rms(m_ref[...], g_ref[...]).astype(BF16), w_ref[...])
    k_ref[...] = z[:, :X_WIDTH]
    v_ref[...] = z[:, X_WIDTH:]
    for h in range(X_HEADS):
        kf_ref[pl.ds(h, tm, stride=X_HEADS), :] = z[:, h * LANES:(h + 1) * LANES]
        vf_ref[pl.ds(h, tm, stride=X_HEADS), :] = z[:, X_WIDTH + h * LANES:X_WIDTH + (h + 1) * LANES]


def _memkv(mem, g, w, tm):
    m = mem.shape[0]
    row = lambda wd: pl.BlockSpec((tm, wd), lambda i: (i, 0))
    flat = pl.BlockSpec((tm * X_HEADS, X_HEAD_DIM), lambda i: (i, 0))
    return pl.pallas_call(
        _memkv_kernel,
        grid=(m // tm,),
        in_specs=[row(D_MODEL), _resident((1, D_MODEL)), _resident(w.shape)],
        out_specs=[row(X_WIDTH), row(X_WIDTH), flat, flat],
        out_shape=[jax.ShapeDtypeStruct((m, X_WIDTH), F32)] * 2
        + [jax.ShapeDtypeStruct((m * X_HEADS, X_HEAD_DIM), F32)] * 2,
        compiler_params=_cparams("parallel"),
        name="mem_kv",
    )(mem, g, w)


def _softmax_rows(s):
    e = jnp.exp(s - jnp.max(s, axis=-1, keepdims=True))
    return e * (1.0 / jnp.sum(e, axis=-1, keepdims=True))


def _xattn_sample_kernel(hq_ref, mk_ref, mv_ref, o_ref, *, nb):
    rows = hq_ref.shape[1]
    nmh = mk_ref.shape[1]
    qh = lax.broadcasted_iota(jnp.int32, (X_HEADS * rows, nmh), 0) // rows
    mh = lax.broadcasted_iota(jnp.int32, (X_HEADS * rows, nmh), 1) % X_HEADS
    same = qh == mh
    seqs = range(nb)
    qs = [jnp.concatenate([hq_ref[b, :, h * LANES:(h + 1) * LANES] for h in range(X_HEADS)], axis=0)
          for b in seqs]
    ss = [_dot_nt(qs[b], mk_ref[b].astype(BF16)) * (X_HEAD_DIM ** -0.5) for b in seqs]
    ps = [_softmax_rows(jnp.where(same, s, -jnp.inf)).astype(BF16) for s in ss]
    os_ = [_dot(ps[b], mv_ref[b].astype(BF16)) for b in seqs]
    for b in seqs:
        for h in range(X_HEADS):
            o_ref[b, :, h * LANES:(h + 1) * LANES] = os_[b][h * rows:(h + 1) * rows].astype(BF16)


def _xattn_sample(hq, mk, mv, nb=8):
    bsz, rows, _ = hq.shape
    q = pl.BlockSpec((nb, rows, X_WIDTH), lambda i: (i, 0, 0))
    mem = pl.BlockSpec((nb, mk.shape[1], X_HEAD_DIM), lambda i: (i, 0, 0))
    return pl.pallas_call(
        functools.partial(_xattn_sample_kernel, nb=nb),
        grid=(bsz // nb,),
        in_specs=[q, mem, mem],
        out_specs=q,
        out_shape=jax.ShapeDtypeStruct(hq.shape, BF16),
        compiler_params=_cparams("parallel"),
        name="xattn_sample",
    )(hq, mk, mv)


def _xattn_tile(hq_ref, mk_ref, mv_ref):
    heads = [slice(h * LANES, (h + 1) * LANES) for h in range(X_HEADS)]
    ss = [_dot(hq_ref[:, hs], mk_ref[:, hs].T.astype(BF16)) * (X_HEAD_DIM ** -0.5) for hs in heads]
    ps = [_softmax_rows(s).astype(BF16) for s in ss]
    return jnp.concatenate([_dot(p, mv_ref[:, hs].astype(BF16)).astype(BF16) for p, hs in zip(ps, heads)], axis=1)


def _ffn_kernel(h_ref, *rest, stride, padc, fc, attend):
    if attend:
        hq_ref, mk_ref, mv_ref, *rest = rest
        xo = _xattn_tile(hq_ref, mk_ref, mv_ref)
    else:
        xo_ref, *rest = rest
        xo = xo_ref[...]
    cin_ref, wxo_ref, g_ref, wu_ref, wv_ref, cw_ref, wd_ref, fg_ref, y_ref, tail_ref, xp_ref, acc_ref = rest
    tm = h_ref.shape[0]

    @pl.when(pl.program_id(1) == 0)
    def _():
        tail_ref[...] = cin_ref[...]

    h = h_ref[...] + _dot(xo, wxo_ref[...])
    acc_ref[...] = h
    hn = _rms(h, g_ref[...]).astype(BF16)
    for lo, hi in zip((0,) + fc, fc + (D_FF,)):
        cs = slice(lo, hi)
        n = hi - lo
        u = _dot(hn, wu_ref[:, cs])
        gate = _dot(hn, wv_ref[:, cs])
        xp_ref[0:padc, 0:n] = tail_ref[0, :, cs]
        xp_ref[padc:padc + tm, 0:n] = u
        cw = cw_ref[:, cs]
        uc = u * cw[2:3]
        for j in range(FFN_CONV - 1):
            o = padc - (FFN_CONV - 1 - j) * stride
            uc = uc + xp_ref[o:o + tm, 0:n] * cw[j:j + 1]
        tail_ref[0, :, cs] = xp_ref[tm:tm + padc, 0:n]
        acc_ref[...] += _dot((_silu(uc) * gate).astype(BF16), wd_ref[cs, :])
    y_ref[...] = _rms(acc_ref[...], fg_ref[...])


FFN_CUTS = ()


def _ffn(h, xq, mem, cin, wxo, g, wup, cw, wd, fg, tm, stride, fc=FFN_CUTS):
    m = h.shape[0]
    half = lambda c: pl.BlockSpec((D_MODEL, D_FF), lambda *_: (0, c), pipeline_mode=pl.Buffered(1))
    nseq, padc, _ = cin.shape
    nt = m // nseq // tm
    row = lambda w: pl.BlockSpec((tm, w), lambda s, j: (s * nt + j, 0))
    car = pl.BlockSpec((1, padc, D_FF), lambda s, j: (s, 0, 0))
    mem = list(mem or ())
    mem_specs = [pl.BlockSpec((a.shape[0] // nseq, X_WIDTH), lambda s, j: (s, 0)) for a in mem]
    return pl.pallas_call(
        functools.partial(_ffn_kernel, stride=stride, padc=padc, fc=fc, attend=bool(mem)),
        grid=(nseq, nt),
        in_specs=[row(D_MODEL), row(X_WIDTH), *mem_specs, car, _resident(wxo.shape), _resident((1, D_MODEL)),
                  half(0), half(1), _resident(cw.shape), _resident(wd.shape), _resident((1, D_MODEL))],
        out_specs=[row(D_MODEL), car],
        out_shape=[jax.ShapeDtypeStruct((m, D_MODEL), F32), jax.ShapeDtypeStruct(cin.shape, F32)],
        scratch_shapes=[pltpu.VMEM((tm + padc, max(b - a for a, b in zip((0,) + fc, fc + (D_FF,)))), F32),
                        pltpu.VMEM((tm, D_MODEL), F32)],
        compiler_params=_cparams("parallel", "arbitrary"),
        name="ffn",
    )(h, xq, *mem, cin, wxo, g, wup, wup, cw, wd, fg)


def _lane_row(vec, offset=0):
    return jnp.zeros((1, LANES), F32).at[0, offset:offset + vec.shape[0]].set(vec.astype(F32))


W_IN_QKV = A_WIDTH + 2 * A_KV_WIDTH
W_IN_D = W_IN_QKV + 4 * DN_WIDTH
W_IN_AB = W_IN_D + 2 * DN_HEADS
W_IN_G = W_IN_AB + 2 * D_MODEL


def _split_w_in_kernel(wt_ref, qkv_ref, d_ref, ab_ref, g_ref):
    tk = wt_ref.shape[1]
    qkv_ref[...] = wt_ref[:W_IN_QKV, :].T.astype(BF16)
    d_ref[...] = wt_ref[W_IN_QKV:W_IN_D, :].T.astype(BF16)
    ab = jnp.concatenate([wt_ref[W_IN_D:W_IN_AB, :], jnp.zeros((LANES - (W_IN_AB - W_IN_D), tk), F32)], axis=0)
    ab_ref[...] = ab.T.astype(BF16)
    g_ref[...] = wt_ref[W_IN_AB:W_IN_G, :].T.astype(BF16)


def _split_w_in(w_in_t, tk=256):
    k = w_in_t.shape[1]
    widths = (W_IN_QKV, W_IN_D - W_IN_QKV, LANES, W_IN_G - W_IN_AB)
    return pl.pallas_call(
        _split_w_in_kernel,
        grid=(k // tk,),
        in_specs=[pl.BlockSpec((w_in_t.shape[0], tk), lambda i: (0, i))],
        out_specs=[pl.BlockSpec((tk, wd), lambda i: (i, 0)) for wd in widths],
        out_shape=[jax.ShapeDtypeStruct((k, wd), BF16) for wd in widths],
        compiler_params=_cparams("parallel"),
        name="split_w_in",
    )(w_in_t)


def _prep(p):
    wqkv, wd, wab, wg = _split_w_in(p["w_in"].T)
    w = {
        "wqkv": wqkv,
        "wd": wd,
        "wab": wab,
        "wg": wg,
        "norm_mix_g": p["norm_mix_g"].reshape(1, D_MODEL),
        "dn_conv_w": p["dn_conv_w"],
        "alog": _lane_row(p["dn_a_log"]),
        "dtb": _lane_row(p["dn_dt_bias"]),
        "dn_norm_g": p["dn_norm_g"].reshape(1, DN_HEAD_DIM),
        "sinks": p["sinks"].astype(F32),
        "w_br_a": p["w_br_a"].astype(BF16),
        "w_br_b": p["w_br_b"].astype(BF16),
        "w_mix_out": p["w_mix_out"].astype(BF16),
        "norm_x_g": p["norm_x_g"].reshape(1, D_MODEL),
        "w_xq": p["w_xq"].astype(BF16),
        "w_xo": p["w_xo"].astype(BF16),
        "norm_ffn_g": p["norm_ffn_g"].reshape(1, D_MODEL),
        "w_up": p["w_up"].astype(BF16),
        "ffn_conv_w": p["ffn_conv_w"],
        "w_down": p["w_down"].astype(BF16),
        "final_norm_g": p["final_norm_g"].reshape(1, D_MODEL),
    }
    return w


def _tile(m, pref):
    return pref if m % pref == 0 else m


def _prompt_layer(x, mem, norm_mem_g, w_xkv, w):
    bsz, seqlen, _ = x.shape
    m = bsz * seqlen
    x2 = x.reshape(m, D_MODEL)
    tm = _tile(seqlen, ROW_TILE)
    tabs = _rope_tables(jnp.arange(seqlen, dtype=jnp.int32))
    assert seqlen >= WINDOW, "the window outputs are taken from a full last window"
    q, k, v, ga, gb, kwin, vwin, yb, dn_state, dn_tail = _proj_gdn(
        x2, w["norm_mix_g"], tabs, w["wqkv"], w["wd"], w["wab"], w["wg"], w["dn_conv_w"], w["alog"], w["dtb"],
        w["dn_norm_g"], bsz, tm)
    h, hq = _merge(x2, (w["sinks"], q, k, v), yb, ga, gb, w["w_br_a"], w["w_br_b"], w["w_mix_out"],
                   w["norm_x_g"], w["w_xq"], bsz, tm)
    nm = mem.shape[1]
    mk, mv, mk_rows, mv_rows = _memkv(mem.reshape(bsz * nm, D_MODEL), norm_mem_g.reshape(1, D_MODEL),
                                      w_xkv.astype(BF16), _tile(bsz * nm, ROW_TILE))
    cin = jnp.zeros((bsz, SUBLANES, D_FF), F32)
    y, tail = _ffn(h, hq, (mk, mv), cin, w["w_xo"], w["norm_ffn_g"], w["w_up"], w["ffn_conv_w"],
                   w["w_down"], w["final_norm_g"], tm, 1)
    win = lambda a: jnp.transpose(a.reshape(bsz, A_KV_HEADS, A_HEAD_DIM, WINDOW), (0, 3, 1, 2))
    new = (
        win(kwin),
        win(vwin),
        dn_tail[:, SUBLANES - (DN_CONV - 1):],
        dn_state,
        mk_rows.reshape(bsz, nm, X_HEADS, X_HEAD_DIM),
        mv_rows.reshape(bsz, nm, X_HEADS, X_HEAD_DIM),
        tail[:, SUBLANES - (FFN_CONV - 1):],
    )
    return y.reshape(bsz, seqlen, D_MODEL), new


def _sample_layer(x, pos0, win_k, win_v, dn_buf, dn_state, mem_k, mem_v, ffn_buf, w):
    bsz, t, _ = x.shape
    m = bsz * t
    sr = SAMPLE_ROWS
    x2 = x.reshape(m, D_MODEL)
    tabs = _rope_tables(jnp.tile(pos0 + jnp.arange(t, dtype=jnp.int32), bsz))
    q, k, v, dqkv, dz, ab, ga, gb = _proj(x2, w["norm_mix_g"], tabs, w["wqkv"], w["wd"], w["wab"], w["wg"], m)

    qh = jnp.transpose(q.reshape(bsz, t, A_HEADS, A_HEAD_DIM), (0, 2, 1, 3)).reshape(bsz, A_HEADS * t, A_HEAD_DIM)
    padrows = lambda a: jnp.pad(a.reshape(bsz, t, -1), ((0, 0), (0, sr - t), (0, 0)))
    kv_major = lambda a: jnp.transpose(a, (0, 2, 3, 1))
    nbw = min(bsz, LANES // t)

    def new_t(a):
        a = jnp.transpose(a.reshape(bsz // nbw, nbw * t, A_KV_HEADS, A_HEAD_DIM), (0, 2, 3, 1))
        return jnp.pad(a, ((0, 0), (0, 0), (0, 0), (0, LANES - nbw * t)))

    oh, new_wk, new_wv = _swa_sample(qh, kv_major(win_k), kv_major(win_v), new_t(k), new_t(v), w["sinks"], t)
    ya = jnp.transpose(oh.reshape(bsz, A_HEADS, t, A_HEAD_DIM), (0, 2, 1, 3)).reshape(m, A_WIDTH)
    new_wk, new_wv = (jnp.transpose(a, (0, 3, 1, 2)) for a in (new_wk, new_wv))

    yb, new_s, new_dn_buf = _gdn_sample(
        jnp.transpose(dn_buf, (1, 0, 2)), dqkv.reshape(bsz, t, 3 * DN_WIDTH), dz.reshape(bsz, t, DN_WIDTH),
        ab.reshape(bsz, t, LANES), dn_state, w["dn_conv_w"], w["alog"], w["dtb"], w["dn_norm_g"])
    yb = yb.reshape(m, DN_WIDTH)
    new_dn_buf = jnp.transpose(new_dn_buf, (1, 0, 2))

    h, hq = _merge(x2, ya, yb, ga, gb, w["w_br_a"], w["w_br_b"], w["w_mix_out"], w["norm_x_g"], w["w_xq"], 1, m)
    nm = mem_k.shape[1]
    xo = _xattn_sample(padrows(hq), mem_k.reshape(bsz, nm * X_HEADS, X_HEAD_DIM),
                       mem_v.reshape(bsz, nm * X_HEADS, X_HEAD_DIM))
    xo = xo[:, :t]

    tmaj = lambda a: jnp.transpose(a.reshape(bsz, t, -1), (1, 0, 2)).reshape(m, -1)
    cin = jnp.transpose(ffn_buf, (1, 0, 2)).reshape(1, (FFN_CONV - 1) * bsz, D_FF)
    y, tail = _ffn(tmaj(h), tmaj(xo), None, cin, w["w_xo"], w["norm_ffn_g"], w["w_up"], w["ffn_conv_w"],
                   w["w_down"], w["final_norm_g"], m, bsz)
    y = jnp.transpose(y.reshape(t, bsz, D_MODEL), (1, 0, 2))
    new_ffn = jnp.transpose(tail.reshape(FFN_CONV - 1, bsz, D_FF), (1, 0, 2))
    return y, (new_wk, new_wv, new_dn_buf, new_s, new_ffn)


def kernel(x_prompt, x_sample, mem_prompt, cache_win_k, cache_win_v, state_dn_conv, state_dn, cache_mem_k, cache_mem_v, state_ffn_conv, norm_mix_g, w_in, dn_conv_w, dn_a_log, dn_dt_bias, dn_norm_g, attn_sinks, w_br_a, w_br_b, w_mix_out, norm_x_g, norm_mem_g, w_xq, w_xkv, w_xo, norm_ffn_g, w_up, ffn_conv_w, w_down, final_norm_g):
    p = {"norm_mix_g": norm_mix_g[0], "w_in": w_in[0], "dn_conv_w": dn_conv_w[0], "dn_a_log": dn_a_log[0],
         "dn_dt_bias": dn_dt_bias[0], "dn_norm_g": dn_norm_g[0], "sinks": attn_sinks[0], "w_br_a": w_br_a[0],
         "w_br_b": w_br_b[0], "w_mix_out": w_mix_out[0], "norm_x_g": norm_x_g[0], "w_xq": w_xq[0],
         "w_xo": w_xo[0], "norm_ffn_g": norm_ffn_g[0], "w_up": w_up[0], "ffn_conv_w": ffn_conv_w[0],
         "w_down": w_down[0], "final_norm_g": final_norm_g}
    w = _prep(p)
    yp, newp = _prompt_layer(x_prompt, mem_prompt, norm_mem_g[0], w_xkv[0], w)
    ys, news = _sample_layer(x_sample, PAST_LEN, cache_win_k[0], cache_win_v[0], state_dn_conv[0], state_dn[0],
                             cache_mem_k[0], cache_mem_v[0], state_ffn_conv[0], w)
    lead = lambda a: a[None]
    p_win_k, p_win_v, p_dn_conv, p_dn_state, p_mem_k, p_mem_v, p_ffn_conv = [lead(a) for a in newp]
    s_win_k, s_win_v, s_dn_conv, s_dn_state, s_ffn_conv = [lead(a) for a in news]
    return (yp, ys, p_win_k, p_win_v, p_dn_conv, p_dn_state, p_mem_k, p_mem_v, p_ffn_conv,
            s_win_k, s_win_v, s_dn_conv, s_dn_state, s_ffn_conv)
```

```python
import functools

import jax
import jax.numpy as jnp
from jax import lax
from jax.experimental import pallas as pl
from jax.experimental.pallas import tpu as pltpu

F32 = jnp.float32
BF16 = jnp.bfloat16

D_MODEL = 1024
A_HEADS = 8
A_KV_HEADS = 2
A_HEAD_DIM = 64
A_WIDTH = 512
A_KV_WIDTH = 128
WINDOW = 128
ROT_DIM = 16
ROPE_THETA = 500000.0
DN_HEADS = 4
DN_HEAD_DIM = 128
DN_WIDTH = 512
DN_CONV = 4
X_HEADS = 4
X_HEAD_DIM = 128
X_WIDTH = 512
D_FF = 2816
FFN_CONV = 3
EPS = 1e-6
PAST_LEN = 16384

LANES = 128
SUBLANES = 8
VMEM_LIMIT = 56 * 1024 * 1024
ROW_TILE = 512
CHUNK = 128
SAMPLE_ROWS = 16
GDN_SAMPLE_ROWS = 8


def _cparams(*sem):
    return pltpu.CompilerParams(dimension_semantics=sem, vmem_limit_bytes=VMEM_LIMIT)


def _resident(shape):
    return pl.BlockSpec(shape, lambda *_: (0,) * len(shape), pipeline_mode=pl.Buffered(1))


def _rms(x, g):
    return x * lax.rsqrt(jnp.mean(x * x, axis=-1, keepdims=True) + EPS) * g


def _dot(a, b):
    return jnp.dot(a, b, preferred_element_type=F32)


def _dot_nt(a, b):
    return lax.dot_general(a, b, (((1,), (1,)), ((), ())), preferred_element_type=F32)


def _silu(x):
    return x * jax.nn.sigmoid(x)


def _rope(seg, c, s1, s2):
    return seg * c + pltpu.roll(seg, LANES - 8, 1) * s1 + pltpu.roll(seg, 8, 1) * s2


PROJ_COLS = 512
assert (3 * DN_WIDTH) % PROJ_COLS == 0 and D_MODEL % PROJ_COLS == 0


def _proj_tile(x_ref, g_ref, c_ref, s1_ref, s2_ref, wqkv_ref, wd_ref, wab_ref, wg_ref,
               q_ref, k_ref, v_ref, dqkv_ref, dz_ref, ab_ref, ga_ref, gb_ref, dqkv_row0=0, win_refs=None):
    tm = x_ref.shape[0]
    xb = _rms(x_ref[...], g_ref[...]).astype(BF16)
    yield
    c, s1, s2 = c_ref[...], s1_ref[...], s2_ref[...]
    z = _dot(xb, wqkv_ref[...])
    for i in range(A_WIDTH // LANES):
        sl = slice(i * LANES, (i + 1) * LANES)
        q_ref[:, sl] = _rope(z[:, sl], c, s1, s2).astype(BF16)
    k = _rope(z[:, A_WIDTH:A_WIDTH + LANES], c, s1, s2)
    k_ref[...] = k
    v_ref[...] = z[:, A_WIDTH + LANES:]
    if win_refs is not None:
        win_refs[0][0] = k[tm - WINDOW:].T
        win_refs[1][0] = z[tm - WINDOW:, A_WIDTH + LANES:].T
    yield
    pc = PROJ_COLS
    for j in range(3 * DN_WIDTH // pc):
        dqkv_ref[dqkv_row0:dqkv_row0 + tm, j * pc:(j + 1) * pc] = _dot(xb, wd_ref[:, j * pc:(j + 1) * pc])
        yield
    dz_ref[...] = _dot(xb, wd_ref[:, 3 * DN_WIDTH:])
    ab_ref[...] = _dot(xb, wab_ref[...])
    yield
    for dst, base in ((ga_ref, 0), (gb_ref, D_MODEL)):
        for j in range(D_MODEL // pc):
            dst[:, j * pc:(j + 1) * pc] = _dot(xb, wg_ref[:, base + j * pc:base + (j + 1) * pc])
            yield


def _proj_kernel(*refs):
    _drain(_proj_tile(*refs))


def _proj(x, g, tabs, wqkv, wd, wab, wg, tm):
    m = x.shape[0]
    nt = tabs[0].shape[0] // tm
    row = lambda w: pl.BlockSpec((tm, w), lambda i: (i, 0))
    tab = pl.BlockSpec((tm, LANES), lambda i: (i % nt, 0))
    widths = (A_WIDTH, LANES, LANES, 3 * DN_WIDTH, DN_WIDTH, LANES, D_MODEL, D_MODEL)
    dts = (BF16, F32, F32, F32, F32, F32, F32, F32)
    return pl.pallas_call(
        _proj_kernel,
        grid=(m // tm,),
        in_specs=[row(D_MODEL), _resident((1, D_MODEL)), tab, tab, tab,
                  _resident(wqkv.shape), _resident(wd.shape), _resident(wab.shape), _resident(wg.shape)],
        out_specs=[row(w) for w in widths],
        out_shape=[jax.ShapeDtypeStruct((m, w), d) for w, d in zip(widths, dts)],
        compiler_params=_cparams("parallel"),
        name="in_proj",
    )(x, g, *tabs, wqkv, wd, wab, wg)


def _rope_tables(pos):
    half = ROT_DIM // 2
    d = jnp.arange(LANES, dtype=jnp.int32) % A_HEAD_DIM
    inv = ROPE_THETA ** (-2.0 * (d % half).astype(F32) / ROT_DIM)
    ang = pos.astype(F32)[:, None] * inv[None, :]
    c, s = jnp.cos(ang), jnp.sin(ang)
    return (jnp.where(d < ROT_DIM, c, 1.0), jnp.where(d < half, -s, 0.0),
            jnp.where((d >= half) & (d < ROT_DIM), s, 0.0))


def _both_halves(t, lane_lo):
    r = pltpu.roll(t, A_HEAD_DIM, 1)
    return jnp.where(lane_lo, t, r), jnp.where(lane_lo, r, t)


def _band_softmax(s, lower, prev_bias, sink):
    w = s.shape[1] // 2
    prev = s[:, :w] if prev_bias is None else s[:, :w] + prev_bias
    live = jnp.where(lower, s[:, w:], prev)
    m = jnp.maximum(jnp.max(live, axis=-1, keepdims=True), sink)
    p = jnp.exp(live - m)
    p = p * (1.0 / (jnp.sum(p, axis=-1, keepdims=True) + jnp.exp(sink - m)))
    return jnp.concatenate([jnp.where(lower, 0.0, p), jnp.where(lower, p, 0.0)], axis=1).astype(BF16)


def _swa_tile(sink_ref, q_ref, k, v, first):
    w = WINDOW
    lane_lo_k = lax.broadcasted_iota(jnp.int32, k.shape, 1) < A_HEAD_DIM
    kk = [t.astype(BF16) for t in _both_halves(k, lane_lo_k)]
    vv = [t.astype(BF16) for t in _both_halves(v, lane_lo_k)]
    lane_lo = lax.broadcasted_iota(jnp.int32, (w, LANES), 1) < A_HEAD_DIM
    row = lax.broadcasted_iota(jnp.int32, (4 * w, w), 0)
    col = lax.broadcasted_iota(jnp.int32, (4 * w, w), 1)
    lower = col <= (row & (w - 1))
    first_bias = jnp.where(first, -jnp.inf, 0.0)
    hrow = lax.broadcasted_iota(jnp.int32, (4 * w, 1), 0) // w
    sinks = []
    for g in range(A_KV_HEADS):
        sink = jnp.zeros((4 * w, 1), F32)
        for j in range(4):
            sink = jnp.where(hrow == j, sink_ref[4 * g + j], sink)
        sinks.append(sink)
    zero = jnp.zeros((), BF16)
    scale = jnp.asarray(A_HEAD_DIM ** -0.5, BF16)
    nblk = q_ref.shape[0] // w
    probs = [(b, g) for b in range(nblk) for g in range(A_KV_HEADS)]

    def queries(b, g):
        parts = []
        for sgm in range(2):
            seg = q_ref[b * w:(b + 1) * w, (2 * g + sgm) * LANES:(2 * g + sgm + 1) * LANES] * scale
            parts += [jnp.where(lane_lo, seg, zero), jnp.where(lane_lo, zero, seg)]
        return jnp.concatenate(parts, axis=0)

    ss = [_dot_nt(queries(b, g), kk[g][b * w:(b + 2) * w]) for b, g in probs]
    yield
    ps = [_band_softmax(s, lower, first_bias if b == 0 else None, sinks[g]) for s, (b, g) in zip(ss, probs)]
    yield
    os_ = [_dot(p, vv[g][b * w:(b + 2) * w]) for p, (b, g) in zip(ps, probs)]
    yield
    segs = [[jnp.where(lane_lo, o[(2 * sgm) * w:(2 * sgm + 1) * w], o[(2 * sgm + 1) * w:(2 * sgm + 2) * w]
                       ).astype(BF16) for sgm in range(2)] for o in os_]
    return jnp.concatenate([jnp.concatenate(segs[A_KV_HEADS * b] + segs[A_KV_HEADS * b + 1], axis=1)
                            for b in range(nblk)], axis=0)


def _swa_sample_kernel(sink_ref, q_ref, ckt_ref, cvt_ref, knt_ref, vnt_ref, o_ref, cko_ref, cvo_ref, *, nb, t):
    w = ckt_ref.shape[3]
    old = lax.broadcasted_iota(jnp.int32, (A_HEAD_DIM, w), 1) < w - t
    for src_ref, new_ref, dst_ref in ((ckt_ref, knt_ref, cko_ref), (cvt_ref, vnt_ref, cvo_ref)):
        for g in range(A_KV_HEADS):
            new = new_ref[0, g]
            for b in range(nb):
                dst_ref[b, g] = jnp.where(old, pltpu.roll(src_ref[b, g], w - t, 1),
                                          pltpu.roll(new, (w - t - t * b) % LANES, 1))
    rows = q_ref.shape[1] // A_HEADS
    gq = rows * (A_HEADS // A_KV_HEADS)
    r = lax.broadcasted_iota(jnp.int32, (gq, w), 0)
    c = lax.broadcasted_iota(jnp.int32, (gq, w), 1)
    valid_c = c > r % rows
    rn = lax.broadcasted_iota(jnp.int32, (gq, LANES), 0)
    cn = lax.broadcasted_iota(jnp.int32, (gq, LANES), 1)
    causal_n = cn % t <= rn % rows
    valid_n = [causal_n & (cn // t == b) for b in range(nb)]
    hrow = lax.broadcasted_iota(jnp.int32, (gq, 1), 0) // rows
    sink = []
    for g in range(A_KV_HEADS):
        sk = jnp.zeros((gq, 1), F32)
        for j in range(A_HEADS // A_KV_HEADS):
            sk = jnp.where(hrow == j, sink_ref[g * (A_HEADS // A_KV_HEADS) + j], sk)
        sink.append(sk)
    probs = [(b, g) for b in range(nb) for g in range(A_KV_HEADS)]
    scale = A_HEAD_DIM ** -0.5
    qs = [q_ref[b, g * gq:(g + 1) * gq, :] for b, g in probs]
    scs = [jnp.where(valid_c, _dot(q, ckt_ref[b, g].astype(BF16)) * scale, -jnp.inf)
           for q, (b, g) in zip(qs, probs)]
    knt = [knt_ref[0, g].astype(BF16) for g in range(A_KV_HEADS)]
    vnt = [vnt_ref[0, g].astype(BF16) for g in range(A_KV_HEADS)]
    sns = [jnp.where(valid_n[b], _dot(q, knt[g]) * scale, -jnp.inf) for q, (b, g) in zip(qs, probs)]
    ms = [jnp.maximum(jnp.maximum(jnp.max(sc, -1, keepdims=True), jnp.max(sn, -1, keepdims=True)), sink[g])
          for sc, sn, (b, g) in zip(scs, sns, probs)]
    pcs = [jnp.exp(sc - m) for sc, m in zip(scs, ms)]
    pns = [jnp.exp(sn - m) for sn, m in zip(sns, ms)]
    invs = [1.0 / (jnp.sum(pc, -1, keepdims=True) + jnp.sum(pn, -1, keepdims=True) + jnp.exp(sink[g] - m))
            for pc, pn, m, (b, g) in zip(pcs, pns, ms, probs)]
    for pc, pn, inv, (b, g) in zip(pcs, pns, invs, probs):
        o_ref[b, g * gq:(g + 1) * gq, :] = (_dot_nt((pc * inv).astype(BF16), cvt_ref[b, g].astype(BF16))
                                            + _dot_nt((pn * inv).astype(BF16), vnt[g]))


def _swa_sample(q, ckt, cvt, knt, vnt, sinks, t):
    bsz, nq, _ = q.shape
    nb = bsz // knt.shape[0]
    assert nb * t <= LANES and ckt.shape[3] == LANES
    blk = lambda a, n=nb: pl.BlockSpec((n,) + a.shape[1:], lambda i: (i,) + (0,) * (a.ndim - 1))
    return pl.pallas_call(
        functools.partial(_swa_sample_kernel, nb=nb, t=t),
        grid=(bsz // nb,),
        in_specs=[pl.BlockSpec(memory_space=pltpu.SMEM), blk(q), blk(ckt), blk(cvt), blk(knt, 1), blk(vnt, 1)],
        out_specs=[blk(q), blk(ckt), blk(cvt)],
        out_shape=[jax.ShapeDtypeStruct(q.shape, F32), jax.ShapeDtypeStruct(ckt.shape, F32),
                   jax.ShapeDtypeStruct(cvt.shape, F32)],
        compiler_params=_cparams("parallel"),
        name="swa_sample",
    )(sinks, q, ckt, cvt, knt, vnt)


def _lane_bcast(x, lane):
    return jnp.broadcast_to(x[:, lane:lane + 1], (x.shape[0], LANES))


def _cumsum_rows(x, block):
    c = x.shape[0]
    rowi = lax.broadcasted_iota(jnp.int32, (c, c), 0)
    coli = lax.broadcasted_iota(jnp.int32, (c, c), 1)
    ones = (((rowi // block) == (coli // block)) & (rowi >= coli)).astype(BF16)
    hi = x.astype(BF16)
    r1 = x - hi.astype(F32)
    mid = r1.astype(BF16)
    lo = (r1 - mid.astype(F32)).astype(BF16)
    s = _dot(ones, jnp.concatenate([hi, mid, lo], axis=1))
    return s[:, :LANES] + s[:, LANES:2 * LANES] + s[:, 2 * LANES:]


def _l2n(t):
    return t * lax.rsqrt(jnp.sum(t * t, axis=-1, keepdims=True) + EPS)


def _gates(ab, alog, dtb):
    x = ab + dtb
    sp = jnp.maximum(x, 0.0) + jnp.log1p(jnp.exp(-jnp.abs(x)))
    return -jnp.exp(alog) * sp, jax.nn.sigmoid(ab)


def _merge_masks(c, top):
    rowi = lax.broadcasted_iota(jnp.int32, (c, c), 0)
    coli = lax.broadcasted_iota(jnp.int32, (c, c), 1)
    masks = []
    s = 1
    while s < top:
        rb, cb = rowi // s, coli // s
        masks.append(((rb // 2) == (cb // 2)) & ((rb % 2) == 1) & ((cb % 2) == 0))
        s *= 2
    return masks


def _each(f, *lists):
    return [f(*t) for t in zip(*lists)]


def _drain(stages):
    try:
        while True:
            next(stages)
    except StopIteration as done:
        return done.value


def _interleave(*staged):
    live = list(staged)
    values = {}
    while live:
        for item in tuple(live):
            stages, per_turn = item
            try:
                for _ in range(per_turn):
                    next(stages)
            except StopIteration as done:
                values[id(stages)] = done.value
                live.remove(item)
    return [values[id(stages)] for stages, _ in staged]


def _chunk_local(qs, ks, vs, gcols, grows, betas, tril, merges):
    c = qs[0].shape[0]
    decays = _each(lambda gc, gr: jnp.exp(jnp.minimum(gc - gr, 0.0)), gcols, grows)
    kbs = _each(lambda k, b: k * b, ks, betas)
    yield
    kts = [k.T for k in ks]
    ms = _each(lambda q, kb, kt: _dot(jnp.concatenate([q, kb], axis=0).astype(BF16), kt.astype(BF16)),
               qs, kbs, kts)
    yield
    qks = _each(lambda m, d: jnp.where(tril, m[:c] * d, 0.0), ms, decays)
    a = _each(lambda m, d: m[c:] * d, ms, decays)
    eye = (lax.broadcasted_iota(jnp.int32, (c, c), 0) == lax.broadcasted_iota(jnp.int32, (c, c), 1)).astype(F32)
    ts = [eye - jnp.where(merges[0], x, 0.0) if merges else eye for x in a]
    yield
    for off in merges[1:]:
        tbs = [t.astype(BF16) for t in ts]
        zs = _each(lambda x, tb: _dot(jnp.where(off, x, 0.0).astype(BF16), tb), a, tbs)
        yield
        ts = _each(lambda t, tb, z: t - _dot(tb, z.astype(BF16)), ts, tbs, zs)
        yield
    ns = [t - eye for t in ts]
    egs = [jnp.exp(gc) for gc in gcols]
    rhss = _each(lambda v, b, kb, eg: jnp.concatenate([v * b, kb * eg], axis=1), vs, betas, kbs, egs)
    yield
    uws = _each(lambda r, n: r + _dot(n.astype(BF16), r.astype(BF16)), rhss, ns)
    yield
    return ([x[:, :LANES] for x in uws], [x[:, LANES:] for x in uws], qks,
            _each(lambda q, eg: q * eg, qs, egs), kts)


def _gdn_tile(xp_ref, dz_ref, ab_ref, cw_ref, alog_ref, dtb_ref, ng_ref, y_ref, s_ref, nc):
    c = CHUNK
    pad = SUBLANES
    cw = cw_ref[...]
    g, beta = _gates(ab_ref[...], alog_ref[...], dtb_ref[...])
    rowi = lax.broadcasted_iota(jnp.int32, (c, c), 0)
    coli = lax.broadcasted_iota(jnp.int32, (c, c), 1)
    tril = rowi >= coli
    merges = _merge_masks(c, c)
    ng = ng_ref[...]
    heads = range(DN_HEADS)
    probs = [(ci, h) for ci in range(nc) for h in heads]
    rows = lambda ci: slice(ci * c, (ci + 1) * c)
    lanes = lambda part, h: slice(part * DN_WIDTH + h * LANES, part * DN_WIDTH + (h + 1) * LANES)
    yield
    qs, ks, vs, gcols, grows, betas = [], [], [], [], [], []
    for ci in range(nc):
        conv = xp_ref[pad + ci * c:pad + (ci + 1) * c, :] * cw[DN_CONV - 1:DN_CONV]
        for j in range(DN_CONV - 1):
            o = pad - (DN_CONV - 1) + j + ci * c
            conv = conv + xp_ref[o:o + c, :] * cw[j:j + 1]
        conv = _silu(conv)
        gc = _cumsum_rows(g[rows(ci)], c)
        gct = gc.T
        qs += [_l2n(conv[:, lanes(0, h)]) * (DN_HEAD_DIM ** -0.5) for h in heads]
        ks += [_l2n(conv[:, lanes(1, h)]) for h in heads]
        vs += [conv[:, lanes(2, h)] for h in heads]
        gcols += [_lane_bcast(gc, h) for h in heads]
        grows += [gct[h:h + 1, :] for h in heads]
        betas += [_lane_bcast(beta[rows(ci)], DN_HEADS + h) for h in heads]
        yield
    us, ws, qks, qds, kts = yield from _chunk_local(qs, ks, vs, gcols, grows, betas, tril, merges)
    glasts = [gc[c - 1:c, :] for gc in gcols]
    kdts = _each(lambda kt, gr: kt * jnp.exp(gr[:, c - 1:c] - gr), kts, grows)
    wqs = _each(lambda w, qd: jnp.concatenate([w, qd], axis=0).astype(BF16), ws, qds)
    qkks = _each(lambda qk, kdt: jnp.concatenate([qk, kdt], axis=0).astype(BF16), qks, kdts)
    yield
    ss = [s_ref[h] for h in heads]
    for ci in range(nc):
        pr = [ci * DN_HEADS + h for h in heads]
        r2s = [_dot(wqs[p], s.astype(BF16)) for p, s in zip(pr, ss)]
        vnews = [us[p] - r2[:c] for p, r2 in zip(pr, r2s)]
        yield
        r3s = [_dot(qkks[p], vn.astype(BF16)) for p, vn in zip(pr, vnews)]
        ss = [s * jnp.exp(glasts[p]) + r3[c:] for p, s, r3 in zip(pr, ss, r3s)]
        for h in heads:
            o = r2s[h][c:] + r3s[h][:c]
            y_ref[rows(ci), lanes(0, h)] = (_rms(o, ng) * _silu(dz_ref[rows(ci), lanes(0, h)])).astype(BF16)
        yield
    for h in heads:
        s_ref[h] = ss[h]


def _proj_gdn_kernel(x_ref, g_ref, c_ref, s1_ref, s2_ref, wqkv_ref, wd_ref, wab_ref, wg_ref,
                     cw_ref, alog_ref, dtb_ref, ng_ref,
                     q_ref, k_ref, v_ref, ga_ref, gb_ref, kw_ref, vw_ref, y_ref, sout_ref, tail_ref,
                     xp_ref, dz_ref, ab_ref, carry_ref, s_ref, *, nt, nc):
    i = pl.program_id(0)
    pad = SUBLANES
    r = nc * CHUNK
    slot_a = i % 2
    slot_b = 1 - slot_a

    @pl.when(i == 0)
    def _():
        xp_ref[1] = jnp.zeros(xp_ref.shape[1:], F32)
        dz_ref[1] = jnp.zeros(dz_ref.shape[1:], F32)
        ab_ref[1] = jnp.zeros(ab_ref.shape[1:], F32)

    @pl.when(jnp.maximum(i - 1, 0) % nt == 0)
    def _():
        carry_ref[...] = jnp.zeros_like(carry_ref)
        s_ref[...] = jnp.zeros_like(s_ref)

    xpb_ref = xp_ref.at[slot_b]
    xpb_ref[0:pad, :] = carry_ref[...]
    _interleave(
        (_gdn_tile(xpb_ref, dz_ref.at[slot_b], ab_ref.at[slot_b], cw_ref, alog_ref, dtb_ref, ng_ref,
                   y_ref, s_ref, nc), 1),
        (_proj_tile(x_ref, g_ref, c_ref, s1_ref, s2_ref, wqkv_ref, wd_ref, wab_ref, wg_ref, q_ref, k_ref, v_ref,
                    xp_ref.at[slot_a], dz_ref.at[slot_a], ab_ref.at[slot_a], ga_ref, gb_ref, pad,
                    (kw_ref, vw_ref)), 1))
    carry_ref[...] = xpb_ref[r:r + pad, :]
    tail_ref[0] = carry_ref[...]
    sout_ref[0] = s_ref[...]


def _proj_gdn(x, g, tabs, wqkv, wd, wab, wg, cw, alog, dtb, ng, nseq, tm):
    m = x.shape[0]
    nt = m // nseq // tm
    last = m // tm - 1
    nc = tm // CHUNK
    proj = lambda i: jnp.minimum(i, last)
    gdn = lambda i: jnp.maximum(i - 1, 0)
    prow = lambda w: pl.BlockSpec((tm, w), lambda i: (proj(i), 0))
    tab = pl.BlockSpec((tm, LANES), lambda i: (proj(i) % nt, 0))
    per_seq = lambda *dims: pl.BlockSpec((1,) + dims, lambda i: (gdn(i) // nt,) + (0,) * len(dims))
    win = pl.BlockSpec((1, A_KV_WIDTH, WINDOW), lambda i: (proj(i) // nt, 0, 0))
    win_shape = jax.ShapeDtypeStruct((nseq, A_KV_WIDTH, WINDOW), F32)
    state = (DN_HEADS, DN_HEAD_DIM, DN_HEAD_DIM)
    bufs = [pltpu.VMEM((2, tm + SUBLANES, 3 * DN_WIDTH), F32), pltpu.VMEM((2, tm, DN_WIDTH), F32),
            pltpu.VMEM((2, tm, LANES), F32)]
    return pl.pallas_call(
        functools.partial(_proj_gdn_kernel, nt=nt, nc=nc),
        grid=(m // tm + 1,),
        in_specs=[prow(D_MODEL), _resident((1, D_MODEL)), tab, tab, tab,
                  _resident(wqkv.shape), _resident(wd.shape), _resident(wab.shape), _resident(wg.shape),
                  _resident(cw.shape), _resident((1, LANES)), _resident((1, LANES)), _resident((1, LANES))],
        out_specs=[prow(A_WIDTH), prow(LANES), prow(LANES), prow(D_MODEL), prow(D_MODEL), win, win,
                   pl.BlockSpec((tm, DN_WIDTH), lambda i: (gdn(i), 0)), per_seq(*state),
                   per_seq(SUBLANES, 3 * DN_WIDTH)],
        out_shape=[jax.ShapeDtypeStruct((m, A_WIDTH), BF16), jax.ShapeDtypeStruct((m, LANES), F32),
                   jax.ShapeDtypeStruct((m, LANES), F32), jax.ShapeDtypeStruct((m, D_MODEL), F32),
                   jax.ShapeDtypeStruct((m, D_MODEL), F32), win_shape, win_shape,
                   jax.ShapeDtypeStruct((m, DN_WIDTH), BF16),
                   jax.ShapeDtypeStruct((nseq,) + state, F32),
                   jax.ShapeDtypeStruct((nseq, SUBLANES, 3 * DN_WIDTH), F32)],
        scratch_shapes=bufs + [pltpu.VMEM((SUBLANES, 3 * DN_WIDTH), F32), pltpu.VMEM(state, F32)],
        compiler_params=_cparams("arbitrary"),
        name="proj_gdn",
    )(x, g, *tabs, wqkv, wd, wab, wg, cw, alog, dtb, ng)


def _gdn_sample_kernel(buf_ref, x_ref, dz_ref, ab_ref, s0_ref, cw_ref, alog_ref, dtb_ref, ng_ref,
                       y_ref, sout_ref, bufo_ref, xp_ref, ab16_ref):
    t = x_ref.shape[1]
    c = CHUNK
    sr = GDN_SAMPLE_ROWS
    nb = c // sr
    pad = SUBLANES
    hist = DN_CONV - 1
    xp_ref[...] = jnp.zeros_like(xp_ref)
    ab16_ref[...] = jnp.zeros_like(ab16_ref)
    for b in range(nb):
        for j in range(hist):
            xp_ref[pad + b * sr - hist + j:pad + b * sr - hist + j + 1, :] = buf_ref[j, b:b + 1, :]
        xp_ref[pad + b * sr:pad + b * sr + t, :] = x_ref[b]
        ab16_ref[b * sr:b * sr + t, :] = ab_ref[b]
    for b in range(nb):
        for j in range(hist):
            o = pad + b * sr - hist + t + j
            bufo_ref[j, b:b + 1, :] = xp_ref[o:o + 1, :]
    cw = cw_ref[...]
    conv = xp_ref[pad:pad + c, :] * cw[hist:hist + 1]
    for j in range(hist):
        conv = conv + xp_ref[pad - hist + j:pad - hist + j + c, :] * cw[j:j + 1]
    conv = _silu(conv)
    live = (lax.broadcasted_iota(jnp.int32, (c, LANES), 0) % sr) < t
    g, beta = _gates(ab16_ref[...], alog_ref[...], dtb_ref[...])
    g = jnp.where(live, g, 0.0)
    beta = jnp.where(live, beta, 0.0)
    gc = _cumsum_rows(g, sr)
    gct = gc.T
    rowi = lax.broadcasted_iota(jnp.int32, (c, c), 0)
    coli = lax.broadcasted_iota(jnp.int32, (c, c), 1)
    tril = ((rowi // sr) == (coli // sr)) & (rowi >= coli)
    merges = _merge_masks(c, pl.next_power_of_2(t))
    ng = ng_ref[...]
    rowb = lax.broadcasted_iota(jnp.int32, (c, LANES), 0) // sr
    heads = range(DN_HEADS)
    lanes = lambda part, h: slice(part * DN_WIDTH + h * LANES, part * DN_WIDTH + (h + 1) * LANES)
    ks = [_l2n(conv[:, lanes(1, h)]) for h in heads]
    gcols = [_lane_bcast(gc, h) for h in heads]
    us, ws, qks, qds, _ = _drain(_chunk_local(
        [_l2n(conv[:, lanes(0, h)]) * (DN_HEAD_DIM ** -0.5) for h in heads], ks,
        [conv[:, lanes(2, h)] for h in heads], gcols, [gct[h:h + 1, :] for h in heads],
        [_lane_bcast(beta, DN_HEADS + h) for h in heads], tril, merges))
    seqs = range(nb)
    rows = lambda b: slice(b * sr, (b + 1) * sr)
    s0s = [[s0_ref[b, h] for b in seqs] for h in heads]
    r2s = [[_dot(jnp.concatenate([ws[h][rows(b)], qds[h][rows(b)]], axis=0).astype(BF16), s0s[h][b].astype(BF16))
            for b in seqs] for h in heads]
    vnews = [jnp.concatenate([us[h][rows(b)] - r2s[h][b][:sr] for b in seqs], axis=0) for h in heads]
    os_ = [_rms(jnp.concatenate([r2s[h][b][sr:] for b in seqs], axis=0)
                + _dot(qks[h].astype(BF16), vnews[h].astype(BF16)), ng) for h in heads]
    for h in heads:
        for b in seqs:
            y_ref[b, :, lanes(0, h)] = os_[h][b * sr:b * sr + t] * _silu(dz_ref[b, :, lanes(0, h)])
    glasts = [jnp.concatenate([jnp.broadcast_to(gcols[h][(b + 1) * sr - 1:(b + 1) * sr, :], (sr, LANES))
                               for b in seqs], axis=0) for h in heads]
    kdts = [(ks[h] * jnp.exp(glasts[h] - gcols[h])).T.astype(BF16) for h in heads]
    for h in heads:
        for b in seqs:
            vb = jnp.where(rowb == b, vnews[h], 0.0).astype(BF16)
            sout_ref[b, h] = s0s[h][b] * jnp.exp(glasts[h][b * sr:b * sr + 1, :]) + _dot(kdts[h], vb)


def _gdn_sample(buf, x, dz, ab, s0, cw, alog, dtb, ng):
    bsz, t, _ = x.shape
    assert t + DN_CONV - 1 <= GDN_SAMPLE_ROWS, "too many new tokens for one row tile per sequence"
    nb = CHUNK // GDN_SAMPLE_ROWS
    seq = lambda a: pl.BlockSpec((nb,) + a.shape[1:], lambda i: (i,) + (0,) * (a.ndim - 1))
    hist = pl.BlockSpec((buf.shape[0], nb, buf.shape[2]), lambda i: (0, i, 0))
    return pl.pallas_call(
        _gdn_sample_kernel,
        grid=(bsz // nb,),
        in_specs=[hist, seq(x), seq(dz), seq(ab), seq(s0), _resident(cw.shape),
                  _resident((1, LANES)), _resident((1, LANES)), _resident((1, LANES))],
        out_specs=[seq(dz), seq(s0), hist],
        out_shape=[jax.ShapeDtypeStruct(dz.shape, F32), jax.ShapeDtypeStruct(s0.shape, F32),
                   jax.ShapeDtypeStruct(buf.shape, F32)],
        scratch_shapes=[pltpu.VMEM((CHUNK + SUBLANES, 3 * DN_WIDTH), F32), pltpu.VMEM((CHUNK, LANES), F32)],
        compiler_params=_cparams("parallel"),
        name="gdn_sample",
    )(buf, x, dz, ab, s0, cw, alog, dtb, ng)


def _merge_kernel(x_ref, *rest, attend):
    if attend:
        sink_ref, q_ref, kp_ref, kc_ref, vp_ref, vc_ref, *rest = rest
    else:
        ya_ref, *rest = rest
    yb_ref, ga_ref, gb_ref, wa_ref, wb_ref, wo_ref, g_ref, wq_ref, h_ref, hq_ref = rest

    def deltanet_half():
        yb = _dot(yb_ref[...].astype(BF16), wb_ref[...])
        yield
        gate_a = jax.nn.sigmoid(ga_ref[...])
        yield
        return gate_a, jax.nn.sigmoid(gb_ref[...]) * yb

    if attend:
        ya, (gate_a, mix_b) = _interleave(
            (_swa_tile(sink_ref, q_ref, jnp.concatenate([kp_ref[...], kc_ref[...]], axis=0),
                       jnp.concatenate([vp_ref[...], vc_ref[...]], axis=0), pl.program_id(1) == 0), 1),
            (deltanet_half(), 1))
    else:
        ya = ya_ref[...]
        gate_a, mix_b = _drain(deltanet_half())
    mix = gate_a * _dot(ya.astype(BF16), wa_ref[...]) + mix_b
    h = x_ref[...] + _dot(mix.astype(BF16), wo_ref[...])
    h_ref[...] = h
    hq_ref[...] = _dot(_rms(h, g_ref[...]).astype(BF16), wq_ref[...]).astype(BF16)


def _merge(x, attn, yb, ga, gb, wa, wb, wo, g, wq, nseq, tm):
    m = x.shape[0]
    nt = m // nseq // tm
    row = lambda w: pl.BlockSpec((tm, w), lambda s, j: (s * nt + j, 0))
    attend = isinstance(attn, tuple)
    if attend:
        sinks, q, k, v = attn
        per = tm // WINDOW
        prev = pl.BlockSpec((WINDOW, A_KV_WIDTH), lambda s, j: (jnp.maximum((s * nt + j) * per - 1, 0), 0))
        attn_args = [sinks, q, k, k, v, v]
        attn_specs = [pl.BlockSpec(memory_space=pltpu.SMEM), row(A_WIDTH), prev, row(A_KV_WIDTH),
                      prev, row(A_KV_WIDTH)]
    else:
        attn_args, attn_specs = [attn], [row(A_WIDTH)]
    return pl.pallas_call(
        functools.partial(_merge_kernel, attend=attend),
        grid=(nseq, nt),
        in_specs=[row(D_MODEL), *attn_specs, row(DN_WIDTH), row(D_MODEL), row(D_MODEL),
                  _resident(wa.shape), _resident(wb.shape), _resident(wo.shape),
                  _resident((1, D_MODEL)), _resident(wq.shape)],
        out_specs=[row(D_MODEL), row(X_WIDTH)],
        out_shape=[jax.ShapeDtypeStruct((m, D_MODEL), F32), jax.ShapeDtypeStruct((m, X_WIDTH), BF16)],
        compiler_params=_cparams("parallel", "parallel"),
        name="merge",
    )(x, *attn_args, yb, ga, gb, wa, wb, wo, g, wq)


def _memkv_kernel(m_ref, g_ref, w_ref, k_ref, v_ref, kf_ref, vf_ref):
    tm = m_ref.shape[0]
    z = _dot(_rms(m_ref[...], g_ref[...]).astype(BF16), w_ref[...])
    k_ref[...] = z[:, :X_WIDTH]
    v_ref[...] = z[:, X_WIDTH:]
    for h in range(X_HEADS):
        kf_ref[pl.ds(h, tm, stride=X_HEADS), :] = z[:, h * LANES:(h + 1) * LANES]
        vf_ref[pl.ds(h, tm, stride=X_HEADS), :] = z[:, X_WIDTH + h * LANES:X_WIDTH + (h + 1) * LANES]


def _memkv(mem, g, w, tm):
    m = mem.shape[0]
    row = lambda wd: pl.BlockSpec((tm, wd), lambda i: (i, 0))
    flat = pl.BlockSpec((tm * X_HEADS, X_HEAD_DIM), lambda i: (i, 0))
    return pl.pallas_call(
        _memkv_kernel,
        grid=(m // tm,),
        in_specs=[row(D_MODEL), _resident((1, D_MODEL)), _resident(w.shape)],
        out_specs=[row(X_WIDTH), row(X_WIDTH), flat, flat],
        out_shape=[jax.ShapeDtypeStruct((m, X_WIDTH), F32)] * 2
        + [jax.ShapeDtypeStruct((m * X_HEADS, X_HEAD_DIM), F32)] * 2,
        compiler_params=_cparams("parallel"),
        name="mem_kv",
    )(mem, g, w)


def _softmax_rows(s):
    e = jnp.exp(s - jnp.max(s, axis=-1, keepdims=True))
    return e * (1.0 / jnp.sum(e, axis=-1, keepdims=True))


def _xattn_sample_kernel(hq_ref, mk_ref, mv_ref, o_ref, *, nb):
    rows = hq_ref.shape[1]
    nmh = mk_ref.shape[1]
    qh = lax.broadcasted_iota(jnp.int32, (X_HEADS * rows, nmh), 0) // rows
    mh = lax.broadcasted_iota(jnp.int32, (X_HEADS * rows, nmh), 1) % X_HEADS
    same = qh == mh
    seqs = range(nb)
    qs = [jnp.concatenate([hq_ref[b, :, h * LANES:(h + 1) * LANES] for h in range(X_HEADS)], axis=0)
          for b in seqs]
    ss = [_dot_nt(qs[b], mk_ref[b].astype(BF16)) * (X_HEAD_DIM ** -0.5) for b in seqs]
    ps = [_softmax_rows(jnp.where(same, s, -jnp.inf)).astype(BF16) for s in ss]
    os_ = [_dot(ps[b], mv_ref[b].astype(BF16)) for b in seqs]
    for b in seqs:
        for h in range(X_HEADS):
            o_ref[b, :, h * LANES:(h + 1) * LANES] = os_[b][h * rows:(h + 1) * rows].astype(BF16)


def _xattn_sample(hq, mk, mv, nb=8):
    bsz, rows, _ = hq.shape
    q = pl.BlockSpec((nb, rows, X_WIDTH), lambda i: (i, 0, 0))
    mem = pl.BlockSpec((nb, mk.shape[1], X_HEAD_DIM), lambda i: (i, 0, 0))
    return pl.pallas_call(
        functools.partial(_xattn_sample_kernel, nb=nb),
        grid=(bsz // nb,),
        in_specs=[q, mem, mem],
        out_specs=q,
        out_shape=jax.ShapeDtypeStruct(hq.shape, BF16),
        compiler_params=_cparams("parallel"),
        name="xattn_sample",
    )(hq, mk, mv)


def _xattn_tile(hq_ref, mk_ref, mv_ref):
    heads = [slice(h * LANES, (h + 1) * LANES) for h in range(X_HEADS)]
    ss = [_dot_nt(hq_ref[:, hs], mk_ref[:, hs].astype(BF16)) * (X_HEAD_DIM ** -0.5) for hs in heads]
    ps = [_softmax_rows(s).astype(BF16) for s in ss]
    return jnp.concatenate([_dot(p, mv_ref[:, hs].astype(BF16)).astype(BF16) for p, hs in zip(ps, heads)], axis=1)


def _ffn_kernel(h_ref, *rest, stride, padc, fc, attend):
    if attend:
        hq_ref, mk_ref, mv_ref, *rest = rest
        xo = _xattn_tile(hq_ref, mk_ref, mv_ref)
    else:
        xo_ref, *rest = rest
        xo = xo_ref[...]
    cin_ref, wxo_ref, g_ref, wu_ref, wv_ref, cw_ref, wd_ref, fg_ref, y_ref, tail_ref, xp_ref, acc_ref = rest
    tm = h_ref.shape[0]

    @pl.when(pl.program_id(1) == 0)
    def _():
        tail_ref[...] = cin_ref[...]

    h = h_ref[...] + _dot(xo, wxo_ref[...])
    acc_ref[...] = h
    hn = _rms(h, g_ref[...]).astype(BF16)
    for lo, hi in zip((0,) + fc, fc + (D_FF,)):
        cs = slice(lo, hi)
        n = hi - lo
        u = _dot(hn, wu_ref[:, cs])
        gate = _dot(hn, wv_ref[:, cs])
        xp_ref[0:padc, 0:n] = tail_ref[0, :, cs]
        xp_ref[padc:padc + tm, 0:n] = u
        cw = cw_ref[:, cs]
        uc = u * cw[2:3]
        for j in range(FFN_CONV - 1):
            o = padc - (FFN_CONV - 1 - j) * stride
            uc = uc + xp_ref[o:o + tm, 0:n] * cw[j:j + 1]
        tail_ref[0, :, cs] = xp_ref[tm:tm + padc, 0:n]
        acc_ref[...] += _dot((_silu(uc) * gate).astype(BF16), wd_ref[cs, :])
    y_ref[...] = _rms(acc_ref[...], fg_ref[...])


FFN_CUTS = ()


def _ffn(h, xq, mem, cin, wxo, g, wup, cw, wd, fg, tm, stride, fc=FFN_CUTS):
    m = h.shape[0]
    half = lambda c: pl.BlockSpec((D_MODEL, D_FF), lambda *_: (0, c), pipeline_mode=pl.Buffered(1))
    nseq, padc, _ = cin.shape
    nt = m // nseq // tm
    row = lambda w: pl.BlockSpec((tm, w), lambda s, j: (s * nt + j, 0))
    car = pl.BlockSpec((1, padc, D_FF), lambda s, j: (s, 0, 0))
    mem = list(mem or ())
    mem_specs = [pl.BlockSpec((a.shape[0] // nseq, X_WIDTH), lambda s, j: (s, 0)) for a in mem]
    return pl.pallas_call(
        functools.partial(_ffn_kernel, stride=stride, padc=padc, fc=fc, attend=bool(mem)),
        grid=(nseq, nt),
        in_specs=[row(D_MODEL), row(X_WIDTH), *mem_specs, car, _resident(wxo.shape), _resident((1, D_MODEL)),
                  half(0), half(1), _resident(cw.shape), _resident(wd.shape), _resident((1, D_MODEL))],
        out_specs=[row(D_MODEL), car],
        out_shape=[jax.ShapeDtypeStruct((m, D_MODEL), F32), jax.ShapeDtypeStruct(cin.shape, F32)],
        scratch_shapes=[pltpu.VMEM((tm + padc, max(b - a for a, b in zip((0,) + fc, fc + (D_FF,)))), F32),
                        pltpu.VMEM((tm, D_MODEL), F32)],
        compiler_params=_cparams("parallel", "arbitrary"),
        name="ffn",
    )(h, xq, *mem, cin, wxo, g, wup, wup, cw, wd, fg)


def _lane_row(vec, offset=0):
    return jnp.zeros((1, LANES), F32).at[0, offset:offset + vec.shape[0]].set(vec.astype(F32))


W_IN_QKV = A_WIDTH + 2 * A_KV_WIDTH
W_IN_D = W_IN_QKV + 4 * DN_WIDTH
W_IN_AB = W_IN_D + 2 * DN_HEADS
W_IN_G = W_IN_AB + 2 * D_MODEL


def _split_w_in_kernel(wt_ref, qkv_ref, d_ref, ab_ref, g_ref):
    tk = wt_ref.shape[1]
    qkv_ref[...] = wt_ref[:W_IN_QKV, :].T.astype(BF16)
    d_ref[...] = wt_ref[W_IN_QKV:W_IN_D, :].T.astype(BF16)
    ab = jnp.concatenate([wt_ref[W_IN_D:W_IN_AB, :], jnp.zeros((LANES - (W_IN_AB - W_IN_D), tk), F32)], axis=0)
    ab_ref[...] = ab.T.astype(BF16)
    g_ref[...] = wt_ref[W_IN_AB:W_IN_G, :].T.astype(BF16)


def _split_w_in(w_in_t, tk=256):
    k = w_in_t.shape[1]
    widths = (W_IN_QKV, W_IN_D - W_IN_QKV, LANES, W_IN_G - W_IN_AB)
    return pl.pallas_call(
        _split_w_in_kernel,
        grid=(k // tk,),
        in_specs=[pl.BlockSpec((w_in_t.shape[0], tk), lambda i: (0, i))],
        out_specs=[pl.BlockSpec((tk, wd), lambda i: (i, 0)) for wd in widths],
        out_shape=[jax.ShapeDtypeStruct((k, wd), BF16) for wd in widths],
        compiler_params=_cparams("parallel"),
        name="split_w_in",
    )(w_in_t)


def _prep(p):
    wqkv, wd, wab, wg = _split_w_in(p["w_in"].T)
    w = {
        "wqkv": wqkv,
        "wd": wd,
        "wab": wab,
        "wg": wg,
        "norm_mix_g": p["norm_mix_g"].reshape(1, D_MODEL),
        "dn_conv_w": p["dn_conv_w"],
        "alog": _lane_row(p["dn_a_log"]),
        "dtb": _lane_row(p["dn_dt_bias"]),
        "dn_norm_g": p["dn_norm_g"].reshape(1, DN_HEAD_DIM),
        "sinks": p["sinks"].astype(F32),
        "w_br_a": p["w_br_a"].astype(BF16),
        "w_br_b": p["w_br_b"].astype(BF16),
        "w_mix_out": p["w_mix_out"].astype(BF16),
        "norm_x_g": p["norm_x_g"].reshape(1, D_MODEL),
        "w_xq": p["w_xq"].astype(BF16),
        "w_xo": p["w_xo"].astype(BF16),
        "norm_ffn_g": p["norm_ffn_g"].reshape(1, D_MODEL),
        "w_up": p["w_up"].astype(BF16),
        "ffn_conv_w": p["ffn_conv_w"],
        "w_down": p["w_down"].astype(BF16),
        "final_norm_g": p["final_norm_g"].reshape(1, D_MODEL),
    }
    return w


def _tile(m, pref):
    return pref if m % pref == 0 else m


def _prompt_layer(x, mem, norm_mem_g, w_xkv, w):
    bsz, seqlen, _ = x.shape
    m = bsz * seqlen
    x2 = x.reshape(m, D_MODEL)
    tm = _tile(seqlen, ROW_TILE)
    tabs = _rope_tables(jnp.arange(seqlen, dtype=jnp.int32))
    assert seqlen >= WINDOW, "the window outputs are taken from a full last window"
    q, k, v, ga, gb, kwin, vwin, yb, dn_state, dn_tail = _proj_gdn(
        x2, w["norm_mix_g"], tabs, w["wqkv"], w["wd"], w["wab"], w["wg"], w["dn_conv_w"], w["alog"], w["dtb"],
        w["dn_norm_g"], bsz, tm)
    h, hq = _merge(x2, (w["sinks"], q, k, v), yb, ga, gb, w["w_br_a"], w["w_br_b"], w["w_mix_out"],
                   w["norm_x_g"], w["w_xq"], bsz, tm)
    nm = mem.shape[1]
    mk, mv, mk_rows, mv_rows = _memkv(mem.reshape(bsz * nm, D_MODEL), norm_mem_g.reshape(1, D_MODEL),
                                      w_xkv.astype(BF16), _tile(bsz * nm, ROW_TILE))
    cin = jnp.zeros((bsz, SUBLANES, D_FF), F32)
    y, tail = _ffn(h, hq, (mk, mv), cin, w["w_xo"], w["norm_ffn_g"], w["w_up"], w["ffn_conv_w"],
                   w["w_down"], w["final_norm_g"], tm, 1)
    win = lambda a: jnp.transpose(a.reshape(bsz, A_KV_HEADS, A_HEAD_DIM, WINDOW), (0, 3, 1, 2))
    new = (
        win(kwin),
        win(vwin),
        dn_tail[:, SUBLANES - (DN_CONV - 1):],
        dn_state,
        mk_rows.reshape(bsz, nm, X_HEADS, X_HEAD_DIM),
        mv_rows.reshape(bsz, nm, X_HEADS, X_HEAD_DIM),
        tail[:, SUBLANES - (FFN_CONV - 1):],
    )
    return y.reshape(bsz, seqlen, D_MODEL), new


def _sample_layer(x, pos0, win_k, win_v, dn_buf, dn_state, mem_k, mem_v, ffn_buf, w):
    bsz, t, _ = x.shape
    m = bsz * t
    sr = SAMPLE_ROWS
    x2 = x.reshape(m, D_MODEL)
    tabs = _rope_tables(jnp.tile(pos0 + jnp.arange(t, dtype=jnp.int32), bsz))
    q, k, v, dqkv, dz, ab, ga, gb = _proj(x2, w["norm_mix_g"], tabs, w["wqkv"], w["wd"], w["wab"], w["wg"], m)

    qh = jnp.transpose(q.reshape(bsz, t, A_HEADS, A_HEAD_DIM), (0, 2, 1, 3)).reshape(bsz, A_HEADS * t, A_HEAD_DIM)
    padrows = lambda a: jnp.pad(a.reshape(bsz, t, -1), ((0, 0), (0, sr - t), (0, 0)))
    kv_major = lambda a: jnp.transpose(a, (0, 2, 3, 1))
    nbw = min(bsz, LANES // t)

    def new_t(a):
        a = jnp.transpose(a.reshape(bsz // nbw, nbw * t, A_KV_HEADS, A_HEAD_DIM), (0, 2, 3, 1))
        return jnp.pad(a, ((0, 0), (0, 0), (0, 0), (0, LANES - nbw * t)))

    oh, new_wk, new_wv = _swa_sample(qh, kv_major(win_k), kv_major(win_v), new_t(k), new_t(v), w["sinks"], t)
    ya = jnp.transpose(oh.reshape(bsz, A_HEADS, t, A_HEAD_DIM), (0, 2, 1, 3)).reshape(m, A_WIDTH)
    new_wk, new_wv = (jnp.transpose(a, (0, 3, 1, 2)) for a in (new_wk, new_wv))

    yb, new_s, new_dn_buf = _gdn_sample(
        jnp.transpose(dn_buf, (1, 0, 2)), dqkv.reshape(bsz, t, 3 * DN_WIDTH), dz.reshape(bsz, t, DN_WIDTH),
        ab.reshape(bsz, t, LANES), dn_state, w["dn_conv_w"], w["alog"], w["dtb"], w["dn_norm_g"])
    yb = yb.reshape(m, DN_WIDTH)
    new_dn_buf = jnp.transpose(new_dn_buf, (1, 0, 2))

    h, hq = _merge(x2, ya, yb, ga, gb, w["w_br_a"], w["w_br_b"], w["w_mix_out"], w["norm_x_g"], w["w_xq"], 1, m)
    nm = mem_k.shape[1]
    xo = _xattn_sample(padrows(hq), mem_k.reshape(bsz, nm * X_HEADS, X_HEAD_DIM),
                       mem_v.reshape(bsz, nm * X_HEADS, X_HEAD_DIM))
    xo = xo[:, :t]

    tmaj = lambda a: jnp.transpose(a.reshape(bsz, t, -1), (1, 0, 2)).reshape(m, -1)
    cin = jnp.transpose(ffn_buf, (1, 0, 2)).reshape(1, (FFN_CONV - 1) * bsz, D_FF)
    y, tail = _ffn(tmaj(h), tmaj(xo), None, cin, w["w_xo"], w["norm_ffn_g"], w["w_up"], w["ffn_conv_w"],
                   w["w_down"], w["final_norm_g"], m, bsz)
    y = jnp.transpose(y.reshape(t, bsz, D_MODEL), (1, 0, 2))
    new_ffn = jnp.transpose(tail.reshape(FFN_CONV - 1, bsz, D_FF), (1, 0, 2))
    return y, (new_wk, new_wv, new_dn_buf, new_s, new_ffn)


def kernel(x_prompt, x_sample, mem_prompt, cache_win_k, cache_win_v, state_dn_conv, state_dn, cache_mem_k, cache_mem_v, state_ffn_conv, norm_mix_g, w_in, dn_conv_w, dn_a_log, dn_dt_bias, dn_norm_g, attn_sinks, w_br_a, w_br_b, w_mix_out, norm_x_g, norm_mem_g, w_xq, w_xkv, w_xo, norm_ffn_g, w_up, ffn_conv_w, w_down, final_norm_g):
    p = {"norm_mix_g": norm_mix_g[0], "w_in": w_in[0], "dn_conv_w": dn_conv_w[0], "dn_a_log": dn_a_log[0],
         "dn_dt_bias": dn_dt_bias[0], "dn_norm_g": dn_norm_g[0], "sinks": attn_sinks[0], "w_br_a": w_br_a[0],
         "w_br_b": w_br_b[0], "w_mix_out": w_mix_out[0], "norm_x_g": norm_x_g[0], "w_xq": w_xq[0],
         "w_xo": w_xo[0], "norm_ffn_g": norm_ffn_g[0], "w_up": w_up[0], "ffn_conv_w": ffn_conv_w[0],
         "w_down": w_down[0], "final_norm_g": final_norm_g}
    w = _prep(p)
    yp, newp = _prompt_layer(x_prompt, mem_prompt, norm_mem_g[0], w_xkv[0], w)
    ys, news = _sample_layer(x_sample, PAST_LEN, cache_win_k[0], cache_win_v[0], state_dn_conv[0], state_dn[0],
                             cache_mem_k[0], cache_mem_v[0], state_ffn_conv[0], w)
    lead = lambda a: a[None]
    p_win_k, p_win_v, p_dn_conv, p_dn_state, p_mem_k, p_mem_v, p_ffn_conv = [lead(a) for a in newp]
    s_win_k, s_win_v, s_dn_conv, s_dn_state, s_ffn_conv = [lead(a) for a in news]
    return (yp, ys, p_win_k, p_win_v, p_dn_conv, p_dn_state, p_mem_k, p_mem_v, p_ffn_conv,
            s_win_k, s_win_v, s_dn_conv, s_dn_state, s_ffn_conv)
```

```python
import functools

import jax
import jax.numpy as jnp
from jax import lax
from jax.experimental import pallas as pl
from jax.experimental.pallas import tpu as pltpu

F32 = jnp.float32
BF16 = jnp.bfloat16

D_MODEL = 1024
A_HEADS = 8
A_KV_HEADS = 2
A_HEAD_DIM = 64
A_WIDTH = 512
A_KV_WIDTH = 128
WINDOW = 128
ROT_DIM = 16
ROPE_THETA = 500000.0
DN_HEADS = 4
DN_HEAD_DIM = 128
DN_WIDTH = 512
DN_CONV = 4
X_HEADS = 4
X_HEAD_DIM = 128
X_WIDTH = 512
D_FF = 2816
FFN_CONV = 3
EPS = 1e-6
PAST_LEN = 16384

LANES = 128
SUBLANES = 8
VMEM_LIMIT = 56 * 1024 * 1024
ROW_TILE = 512
CHUNK = 128
SAMPLE_ROWS = 16
GDN_SAMPLE_ROWS = 8


def _cparams(*sem):
    return pltpu.CompilerParams(dimension_semantics=sem, vmem_limit_bytes=VMEM_LIMIT)


def _resident(shape):
    return pl.BlockSpec(shape, lambda *_: (0,) * len(shape), pipeline_mode=pl.Buffered(1))


def _rms(x, g):
    return x * lax.rsqrt(jnp.mean(x * x, axis=-1, keepdims=True) + EPS) * g


def _dot(a, b):
    return jnp.dot(a, b, preferred_element_type=F32)


def _dot_nt(a, b):
    return lax.dot_general(a, b, (((1,), (1,)), ((), ())), preferred_element_type=F32)


def _silu(x):
    return x * jax.nn.sigmoid(x)


def _rope(seg, c, s1, s2):
    return seg * c + pltpu.roll(seg, LANES - 8, 1) * s1 + pltpu.roll(seg, 8, 1) * s2


PROJ_COLS = 512
assert (3 * DN_WIDTH) % PROJ_COLS == 0 and D_MODEL % PROJ_COLS == 0


def _proj_tile(x_ref, g_ref, c_ref, s1_ref, s2_ref, wqkv_ref, wd_ref, wab_ref, wg_ref,
               q_ref, k_ref, v_ref, dqkv_ref, dz_ref, ab_ref, ga_ref, gb_ref, dqkv_row0=0, win_refs=None):
    tm = x_ref.shape[0]
    xb = _rms(x_ref[...], g_ref[...]).astype(BF16)
    yield
    c, s1, s2 = c_ref[...], s1_ref[...], s2_ref[...]
    z = _dot(xb, wqkv_ref[...])
    for i in range(A_WIDTH // LANES):
        sl = slice(i * LANES, (i + 1) * LANES)
        q_ref[:, sl] = _rope(z[:, sl], c, s1, s2).astype(BF16)
    k = _rope(z[:, A_WIDTH:A_WIDTH + LANES], c, s1, s2)
    k_ref[...] = k
    v_ref[...] = z[:, A_WIDTH + LANES:]
    if win_refs is not None:
        win_refs[0][0] = k[tm - WINDOW:].T
        win_refs[1][0] = z[tm - WINDOW:, A_WIDTH + LANES:].T
    yield
    pc = PROJ_COLS
    for j in range(3 * DN_WIDTH // pc):
        dqkv_ref[dqkv_row0:dqkv_row0 + tm, j * pc:(j + 1) * pc] = _dot(xb, wd_ref[:, j * pc:(j + 1) * pc])
        yield
    dz_ref[...] = _dot(xb, wd_ref[:, 3 * DN_WIDTH:])
    ab_ref[...] = _dot(xb, wab_ref[...])
    yield
    for dst, base in ((ga_ref, 0), (gb_ref, D_MODEL)):
        for j in range(D_MODEL // pc):
            dst[:, j * pc:(j + 1) * pc] = _dot(xb, wg_ref[:, base + j * pc:base + (j + 1) * pc])
            yield


def _proj_kernel(*refs):
    _drain(_proj_tile(*refs))


def _proj(x, g, tabs, wqkv, wd, wab, wg, tm):
    m = x.shape[0]
    nt = tabs[0].shape[0] // tm
    row = lambda w: pl.BlockSpec((tm, w), lambda i: (i, 0))
    tab = pl.BlockSpec((tm, LANES), lambda i: (i % nt, 0))
    widths = (A_WIDTH, LANES, LANES, 3 * DN_WIDTH, DN_WIDTH, LANES, D_MODEL, D_MODEL)
    dts = (BF16, F32, F32, F32, F32, F32, F32, F32)
    return pl.pallas_call(
        _proj_kernel,
        grid=(m // tm,),
        in_specs=[row(D_MODEL), _resident((1, D_MODEL)), tab, tab, tab,
                  _resident(wqkv.shape), _resident(wd.shape), _resident(wab.shape), _resident(wg.shape)],
        out_specs=[row(w) for w in widths],
        out_shape=[jax.ShapeDtypeStruct((m, w), d) for w, d in zip(widths, dts)],
        compiler_params=_cparams("parallel"),
        name="in_proj",
    )(x, g, *tabs, wqkv, wd, wab, wg)


def _rope_tables(pos):
    half = ROT_DIM // 2
    d = jnp.arange(LANES, dtype=jnp.int32) % A_HEAD_DIM
    inv = ROPE_THETA ** (-2.0 * (d % half).astype(F32) / ROT_DIM)
    ang = pos.astype(F32)[:, None] * inv[None, :]
    c, s = jnp.cos(ang), jnp.sin(ang)
    return (jnp.where(d < ROT_DIM, c, 1.0), jnp.where(d < half, -s, 0.0),
            jnp.where((d >= half) & (d < ROT_DIM), s, 0.0))


def _both_halves(t, lane_lo):
    r = pltpu.roll(t, A_HEAD_DIM, 1)
    return jnp.where(lane_lo, t, r), jnp.where(lane_lo, r, t)


def _band_softmax(s, lower, prev_bias, sink):
    w = s.shape[1] // 2
    prev = s[:, :w] if prev_bias is None else s[:, :w] + prev_bias
    live = jnp.where(lower, s[:, w:], prev)
    m = jnp.maximum(jnp.max(live, axis=-1, keepdims=True), sink)
    p = jnp.exp(live - m)
    p = p * (1.0 / (jnp.sum(p, axis=-1, keepdims=True) + jnp.exp(sink - m)))
    return jnp.concatenate([jnp.where(lower, 0.0, p), jnp.where(lower, p, 0.0)], axis=1).astype(BF16)


def _swa_tile(sink_ref, q_ref, k, v, first):
    w = WINDOW
    lane_lo_k = lax.broadcasted_iota(jnp.int32, k.shape, 1) < A_HEAD_DIM
    kk = [t.astype(BF16) for t in _both_halves(k, lane_lo_k)]
    vv = [t.astype(BF16) for t in _both_halves(v, lane_lo_k)]
    lane_lo = lax.broadcasted_iota(jnp.int32, (w, LANES), 1) < A_HEAD_DIM
    row = lax.broadcasted_iota(jnp.int32, (4 * w, w), 0)
    col = lax.broadcasted_iota(jnp.int32, (4 * w, w), 1)
    lower = col <= (row & (w - 1))
    first_bias = jnp.where(first, -jnp.inf, 0.0)
    hrow = lax.broadcasted_iota(jnp.int32, (4 * w, 1), 0) // w
    sinks = []
    for g in range(A_KV_HEADS):
        sink = jnp.zeros((4 * w, 1), F32)
        for j in range(4):
            sink = jnp.where(hrow == j, sink_ref[4 * g + j], sink)
        sinks.append(sink)
    zero = jnp.zeros((), BF16)
    scale = jnp.asarray(A_HEAD_DIM ** -0.5, BF16)
    nblk = q_ref.shape[0] // w
    probs = [(b, g) for b in range(nblk) for g in range(A_KV_HEADS)]

    def queries(b, g):
        parts = []
        for sgm in range(2):
            seg = q_ref[b * w:(b + 1) * w, (2 * g + sgm) * LANES:(2 * g + sgm + 1) * LANES] * scale
            parts += [jnp.where(lane_lo, seg, zero), jnp.where(lane_lo, zero, seg)]
        return jnp.concatenate(parts, axis=0)

    ss = [_dot_nt(queries(b, g), kk[g][b * w:(b + 2) * w]) for b, g in probs]
    yield
    ps = [_band_softmax(s, lower, first_bias if b == 0 else None, sinks[g]) for s, (b, g) in zip(ss, probs)]
    yield
    os_ = [_dot(p, vv[g][b * w:(b + 2) * w]) for p, (b, g) in zip(ps, probs)]
    yield
    segs = [[jnp.where(lane_lo, o[(2 * sgm) * w:(2 * sgm + 1) * w], o[(2 * sgm + 1) * w:(2 * sgm + 2) * w]
                       ).astype(BF16) for sgm in range(2)] for o in os_]
    return jnp.concatenate([jnp.concatenate(segs[A_KV_HEADS * b] + segs[A_KV_HEADS * b + 1], axis=1)
                            for b in range(nblk)], axis=0)


def _swa_sample_kernel(sink_ref, q_ref, ckt_ref, cvt_ref, knt_ref, vnt_ref, o_ref, cko_ref, cvo_ref, *, nb, t):
    w = ckt_ref.shape[3]
    old = lax.broadcasted_iota(jnp.int32, (A_HEAD_DIM, w), 1) < w - t
    for src_ref, new_ref, dst_ref in ((ckt_ref, knt_ref, cko_ref), (cvt_ref, vnt_ref, cvo_ref)):
        for g in range(A_KV_HEADS):
            new = new_ref[0, g]
            for b in range(nb):
                dst_ref[b, g] = jnp.where(old, pltpu.roll(src_ref[b, g], w - t, 1),
                                          pltpu.roll(new, (w - t - t * b) % LANES, 1))
    rows = q_ref.shape[1] // A_HEADS
    gq = rows * (A_HEADS // A_KV_HEADS)
    r = lax.broadcasted_iota(jnp.int32, (gq, w), 0)
    c = lax.broadcasted_iota(jnp.int32, (gq, w), 1)
    valid_c = c > r % rows
    rn = lax.broadcasted_iota(jnp.int32, (gq, LANES), 0)
    cn = lax.broadcasted_iota(jnp.int32, (gq, LANES), 1)
    causal_n = cn % t <= rn % rows
    valid_n = [causal_n & (cn // t == b) for b in range(nb)]
    hrow = lax.broadcasted_iota(jnp.int32, (gq, 1), 0) // rows
    sink = []
    for g in range(A_KV_HEADS):
        sk = jnp.zeros((gq, 1), F32)
        for j in range(A_HEADS // A_KV_HEADS):
            sk = jnp.where(hrow == j, sink_ref[g * (A_HEADS // A_KV_HEADS) + j], sk)
        sink.append(sk)
    probs = [(b, g) for b in range(nb) for g in range(A_KV_HEADS)]
    scale = A_HEAD_DIM ** -0.5
    qs = [q_ref[b, g * gq:(g + 1) * gq, :] for b, g in probs]
    scs = [jnp.where(valid_c, _dot(q, ckt_ref[b, g].astype(BF16)) * scale, -jnp.inf)
           for q, (b, g) in zip(qs, probs)]
    knt = [knt_ref[0, g].astype(BF16) for g in range(A_KV_HEADS)]
    vnt = [vnt_ref[0, g].astype(BF16) for g in range(A_KV_HEADS)]
    sns = [jnp.where(valid_n[b], _dot(q, knt[g]) * scale, -jnp.inf) for q, (b, g) in zip(qs, probs)]
    ms = [jnp.maximum(jnp.maximum(jnp.max(sc, -1, keepdims=True), jnp.max(sn, -1, keepdims=True)), sink[g])
          for sc, sn, (b, g) in zip(scs, sns, probs)]
    pcs = [jnp.exp(sc - m) for sc, m in zip(scs, ms)]
    pns = [jnp.exp(sn - m) for sn, m in zip(sns, ms)]
    invs = [1.0 / (jnp.sum(pc, -1, keepdims=True) + jnp.sum(pn, -1, keepdims=True) + jnp.exp(sink[g] - m))
            for pc, pn, m, (b, g) in zip(pcs, pns, ms, probs)]
    for pc, pn, inv, (b, g) in zip(pcs, pns, invs, probs):
        o_ref[b, g * gq:(g + 1) * gq, :] = (_dot_nt((pc * inv).astype(BF16), cvt_ref[b, g].astype(BF16))
                                            + _dot_nt((pn * inv).astype(BF16), vnt[g]))


def _swa_sample(q, ckt, cvt, knt, vnt, sinks, t):
    bsz, nq, _ = q.shape
    nb = bsz // knt.shape[0]
    assert nb * t <= LANES and ckt.shape[3] == LANES
    blk = lambda a, n=nb: pl.BlockSpec((n,) + a.shape[1:], lambda i: (i,) + (0,) * (a.ndim - 1))
    return pl.pallas_call(
        functools.partial(_swa_sample_kernel, nb=nb, t=t),
        grid=(bsz // nb,),
        in_specs=[pl.BlockSpec(memory_space=pltpu.SMEM), blk(q), blk(ckt), blk(cvt), blk(knt, 1), blk(vnt, 1)],
        out_specs=[blk(q), blk(ckt), blk(cvt)],
        out_shape=[jax.ShapeDtypeStruct(q.shape, F32), jax.ShapeDtypeStruct(ckt.shape, F32),
                   jax.ShapeDtypeStruct(cvt.shape, F32)],
        compiler_params=_cparams("parallel"),
        name="swa_sample",
    )(sinks, q, ckt, cvt, knt, vnt)


def _lane_bcast(x, lane):
    return jnp.broadcast_to(x[:, lane:lane + 1], (x.shape[0], LANES))


def _cumsum_rows(x, block):
    c = x.shape[0]
    rowi = lax.broadcasted_iota(jnp.int32, (c, c), 0)
    coli = lax.broadcasted_iota(jnp.int32, (c, c), 1)
    ones = (((rowi // block) == (coli // block)) & (rowi >= coli)).astype(BF16)
    hi = x.astype(BF16)
    r1 = x - hi.astype(F32)
    mid = r1.astype(BF16)
    lo = (r1 - mid.astype(F32)).astype(BF16)
    s = _dot(ones, jnp.concatenate([hi, mid, lo], axis=1))
    return s[:, :LANES] + s[:, LANES:2 * LANES] + s[:, 2 * LANES:]


def _l2n(t):
    return t * lax.rsqrt(jnp.sum(t * t, axis=-1, keepdims=True) + EPS)


def _gates(ab, alog, dtb):
    x = ab + dtb
    sp = jnp.maximum(x, 0.0) + jnp.log1p(jnp.exp(-jnp.abs(x)))
    return -jnp.exp(alog) * sp, jax.nn.sigmoid(ab)


def _merge_masks(c, top):
    rowi = lax.broadcasted_iota(jnp.int32, (c, c), 0)
    coli = lax.broadcasted_iota(jnp.int32, (c, c), 1)
    masks = []
    s = 1
    while s < top:
        rb, cb = rowi // s, coli // s
        masks.append(((rb // 2) == (cb // 2)) & ((rb % 2) == 1) & ((cb % 2) == 0))
        s *= 2
    return masks


def _each(f, *lists):
    return [f(*t) for t in zip(*lists)]


def _drain(stages):
    try:
        while True:
            next(stages)
    except StopIteration as done:
        return done.value


def _interleave(*staged):
    live = list(staged)
    values = {}
    while live:
        for item in tuple(live):
            stages, per_turn = item
            try:
                for _ in range(per_turn):
                    next(stages)
            except StopIteration as done:
                values[id(stages)] = done.value
                live.remove(item)
    return [values[id(stages)] for stages, _ in staged]


def _chunk_local(qs, ks, vs, gcols, grows, betas, tril, merges):
    c = qs[0].shape[0]
    decays = _each(lambda gc, gr: jnp.exp(jnp.minimum(gc - gr, 0.0)), gcols, grows)
    kbs = _each(lambda k, b: k * b, ks, betas)
    yield
    kts = [k.T for k in ks]
    ms = _each(lambda q, kb, kt: _dot(jnp.concatenate([q, kb], axis=0).astype(BF16), kt.astype(BF16)),
               qs, kbs, kts)
    yield
    qks = _each(lambda m, d: jnp.where(tril, m[:c] * d, 0.0), ms, decays)
    a = _each(lambda m, d: m[c:] * d, ms, decays)
    eye = (lax.broadcasted_iota(jnp.int32, (c, c), 0) == lax.broadcasted_iota(jnp.int32, (c, c), 1)).astype(F32)
    ts = [eye - jnp.where(merges[0], x, 0.0) if merges else eye for x in a]
    yield
    for off in merges[1:]:
        tbs = [t.astype(BF16) for t in ts]
        zs = _each(lambda x, tb: _dot(jnp.where(off, x, 0.0).astype(BF16), tb), a, tbs)
        yield
        ts = _each(lambda t, tb, z: t - _dot(tb, z.astype(BF16)), ts, tbs, zs)
        yield
    ns = [t - eye for t in ts]
    egs = [jnp.exp(gc) for gc in gcols]
    rhss = _each(lambda v, b, kb, eg: jnp.concatenate([v * b, kb * eg], axis=1), vs, betas, kbs, egs)
    yield
    uws = _each(lambda r, n: r + _dot(n.astype(BF16), r.astype(BF16)), rhss, ns)
    yield
    return ([x[:, :LANES] for x in uws], [x[:, LANES:] for x in uws], qks,
            _each(lambda q, eg: q * eg, qs, egs), kts)


def _gdn_tile(xp_ref, dz_ref, ab_ref, cw_ref, alog_ref, dtb_ref, ng_ref, y_ref, s_ref, nc):
    c = CHUNK
    pad = SUBLANES
    cw = cw_ref[...]
    g, beta = _gates(ab_ref[...], alog_ref[...], dtb_ref[...])
    rowi = lax.broadcasted_iota(jnp.int32, (c, c), 0)
    coli = lax.broadcasted_iota(jnp.int32, (c, c), 1)
    tril = rowi >= coli
    merges = _merge_masks(c, c)
    ng = ng_ref[...]
    heads = range(DN_HEADS)
    probs = [(ci, h) for ci in range(nc) for h in heads]
    rows = lambda ci: slice(ci * c, (ci + 1) * c)
    lanes = lambda part, h: slice(part * DN_WIDTH + h * LANES, part * DN_WIDTH + (h + 1) * LANES)
    yield
    qs, ks, vs, gcols, grows, betas = [], [], [], [], [], []
    for ci in range(nc):
        conv = xp_ref[pad + ci * c:pad + (ci + 1) * c, :] * cw[DN_CONV - 1:DN_CONV]
        for j in range(DN_CONV - 1):
            o = pad - (DN_CONV - 1) + j + ci * c
            conv = conv + xp_ref[o:o + c, :] * cw[j:j + 1]
        conv = _silu(conv)
        gc = _cumsum_rows(g[rows(ci)], c)
        gct = gc.T
        qs += [_l2n(conv[:, lanes(0, h)]) * (DN_HEAD_DIM ** -0.5) for h in heads]
        ks += [_l2n(conv[:, lanes(1, h)]) for h in heads]
        vs += [conv[:, lanes(2, h)] for h in heads]
        gcols += [_lane_bcast(gc, h) for h in heads]
        grows += [gct[h:h + 1, :] for h in heads]
        betas += [_lane_bcast(beta[rows(ci)], DN_HEADS + h) for h in heads]
        yield
    us, ws, qks, qds, kts = yield from _chunk_local(qs, ks, vs, gcols, grows, betas, tril, merges)
    glasts = [gc[c - 1:c, :] for gc in gcols]
    kdts = _each(lambda kt, gr: kt * jnp.exp(gr[:, c - 1:c] - gr), kts, grows)
    wqs = _each(lambda w, qd: jnp.concatenate([w, qd], axis=0).astype(BF16), ws, qds)
    qkks = _each(lambda qk, kdt: jnp.concatenate([qk, kdt], axis=0).astype(BF16), qks, kdts)
    yield
    ss = [s_ref[h] for h in heads]
    for ci in range(nc):
        pr = [ci * DN_HEADS + h for h in heads]
        r2s = [_dot(wqs[p], s.astype(BF16)) for p, s in zip(pr, ss)]
        vnews = [us[p] - r2[:c] for p, r2 in zip(pr, r2s)]
        yield
        r3s = [_dot(qkks[p], vn.astype(BF16)) for p, vn in zip(pr, vnews)]
        ss = [s * jnp.exp(glasts[p]) + r3[c:] for p, s, r3 in zip(pr, ss, r3s)]
        for h in heads:
            o = r2s[h][c:] + r3s[h][:c]
            y_ref[rows(ci), lanes(0, h)] = (_rms(o, ng) * _silu(dz_ref[rows(ci), lanes(0, h)])).astype(BF16)
        yield
    for h in heads:
        s_ref[h] = ss[h]


def _proj_gdn_kernel(x_ref, g_ref, c_ref, s1_ref, s2_ref, wqkv_ref, wd_ref, wab_ref, wg_ref,
                     cw_ref, alog_ref, dtb_ref, ng_ref,
                     q_ref, k_ref, v_ref, ga_ref, gb_ref, kw_ref, vw_ref, y_ref, sout_ref, tail_ref,
                     xp_ref, dz_ref, ab_ref, carry_ref, s_ref, *, nt, nc):
    i = pl.program_id(0)
    pad = SUBLANES
    r = nc * CHUNK
    slot_a = i % 2
    slot_b = 1 - slot_a

    @pl.when(i == 0)
    def _():
        xp_ref[1] = jnp.zeros(xp_ref.shape[1:], F32)
        dz_ref[1] = jnp.zeros(dz_ref.shape[1:], F32)
        ab_ref[1] = jnp.zeros(ab_ref.shape[1:], F32)

    @pl.when(jnp.maximum(i - 1, 0) % nt == 0)
    def _():
        carry_ref[...] = jnp.zeros_like(carry_ref)
        s_ref[...] = jnp.zeros_like(s_ref)

    xpb_ref = xp_ref.at[slot_b]
    xpb_ref[0:pad, :] = carry_ref[...]
    _interleave(
        (_gdn_tile(xpb_ref, dz_ref.at[slot_b], ab_ref.at[slot_b], cw_ref, alog_ref, dtb_ref, ng_ref,
                   y_ref, s_ref, nc), 1),
        (_proj_tile(x_ref, g_ref, c_ref, s1_ref, s2_ref, wqkv_ref, wd_ref, wab_ref, wg_ref, q_ref, k_ref, v_ref,
                    xp_ref.at[slot_a], dz_ref.at[slot_a], ab_ref.at[slot_a], ga_ref, gb_ref, pad,
                    (kw_ref, vw_ref)), 1))
    carry_ref[...] = xpb_ref[r:r + pad, :]
    tail_ref[0] = carry_ref[...]
    sout_ref[0] = s_ref[...]


def _proj_gdn(x, g, tabs, wqkv, wd, wab, wg, cw, alog, dtb, ng, nseq, tm):
    m = x.shape[0]
    nt = m // nseq // tm
    last = m // tm - 1
    nc = tm // CHUNK
    proj = lambda i: jnp.minimum(i, last)
    gdn = lambda i: jnp.maximum(i - 1, 0)
    prow = lambda w: pl.BlockSpec((tm, w), lambda i: (proj(i), 0))
    tab = pl.BlockSpec((tm, LANES), lambda i: (proj(i) % nt, 0))
    per_seq = lambda *dims: pl.BlockSpec((1,) + dims, lambda i: (gdn(i) // nt,) + (0,) * len(dims))
    win = pl.BlockSpec((1, A_KV_WIDTH, WINDOW), lambda i: (proj(i) // nt, 0, 0))
    win_shape = jax.ShapeDtypeStruct((nseq, A_KV_WIDTH, WINDOW), F32)
    state = (DN_HEADS, DN_HEAD_DIM, DN_HEAD_DIM)
    bufs = [pltpu.VMEM((2, tm + SUBLANES, 3 * DN_WIDTH), F32), pltpu.VMEM((2, tm, DN_WIDTH), F32),
            pltpu.VMEM((2, tm, LANES), F32)]
    return pl.pallas_call(
        functools.partial(_proj_gdn_kernel, nt=nt, nc=nc),
        grid=(m // tm + 1,),
        in_specs=[prow(D_MODEL), _resident((1, D_MODEL)), tab, tab, tab,
                  _resident(wqkv.shape), _resident(wd.shape), _resident(wab.shape), _resident(wg.shape),
                  _resident(cw.shape), _resident((1, LANES)), _resident((1, LANES)), _resident((1, LANES))],
        out_specs=[prow(A_WIDTH), prow(LANES), prow(LANES), prow(D_MODEL), prow(D_MODEL), win, win,
                   pl.BlockSpec((tm, DN_WIDTH), lambda i: (gdn(i), 0)), per_seq(*state),
                   per_seq(SUBLANES, 3 * DN_WIDTH)],
        out_shape=[jax.ShapeDtypeStruct((m, A_WIDTH), BF16), jax.ShapeDtypeStruct((m, LANES), F32),
                   jax.ShapeDtypeStruct((m, LANES), F32), jax.ShapeDtypeStruct((m, D_MODEL), F32),
                   jax.ShapeDtypeStruct((m, D_MODEL), F32), win_shape, win_shape,
                   jax.ShapeDtypeStruct((m, DN_WIDTH), BF16),
                   jax.ShapeDtypeStruct((nseq,) + state, F32),
                   jax.ShapeDtypeStruct((nseq, SUBLANES, 3 * DN_WIDTH), F32)],
        scratch_shapes=bufs + [pltpu.VMEM((SUBLANES, 3 * DN_WIDTH), F32), pltpu.VMEM(state, F32)],
        compiler_params=_cparams("arbitrary"),
        name="proj_gdn",
    )(x, g, *tabs, wqkv, wd, wab, wg, cw, alog, dtb, ng)


def _gdn_sample_kernel(buf_ref, x_ref, dz_ref, ab_ref, s0_ref, cw_ref, alog_ref, dtb_ref, ng_ref,
                       y_ref, sout_ref, bufo_ref, xp_ref, ab16_ref):
    t = x_ref.shape[1]
    c = CHUNK
    sr = GDN_SAMPLE_ROWS
    nb = c // sr
    pad = SUBLANES
    hist = DN_CONV - 1
    xp_ref[...] = jnp.zeros_like(xp_ref)
    ab16_ref[...] = jnp.zeros_like(ab16_ref)
    for b in range(nb):
        for j in range(hist):
            xp_ref[pad + b * sr - hist + j:pad + b * sr - hist + j + 1, :] = buf_ref[j, b:b + 1, :]
        xp_ref[pad + b * sr:pad + b * sr + t, :] = x_ref[b]
        ab16_ref[b * sr:b * sr + t, :] = ab_ref[b]
    for b in range(nb):
        for j in range(hist):
            o = pad + b * sr - hist + t + j
            bufo_ref[j, b:b + 1, :] = xp_ref[o:o + 1, :]
    cw = cw_ref[...]
    conv = xp_ref[pad:pad + c, :] * cw[hist:hist + 1]
    for j in range(hist):
        conv = conv + xp_ref[pad - hist + j:pad - hist + j + c, :] * cw[j:j + 1]
    conv = _silu(conv)
    live = (lax.broadcasted_iota(jnp.int32, (c, LANES), 0) % sr) < t
    g, beta = _gates(ab16_ref[...], alog_ref[...], dtb_ref[...])
    g = jnp.where(live, g, 0.0)
    beta = jnp.where(live, beta, 0.0)
    gc = _cumsum_rows(g, sr)
    gct = gc.T
    rowi = lax.broadcasted_iota(jnp.int32, (c, c), 0)
    coli = lax.broadcasted_iota(jnp.int32, (c, c), 1)
    tril = ((rowi // sr) == (coli // sr)) & (rowi >= coli)
    merges = _merge_masks(c, pl.next_power_of_2(t))
    ng = ng_ref[...]
    rowb = lax.broadcasted_iota(jnp.int32, (c, LANES), 0) // sr
    heads = range(DN_HEADS)
    lanes = lambda part, h: slice(part * DN_WIDTH + h * LANES, part * DN_WIDTH + (h + 1) * LANES)
    ks = [_l2n(conv[:, lanes(1, h)]) for h in heads]
    gcols = [_lane_bcast(gc, h) for h in heads]
    us, ws, qks, qds, _ = _drain(_chunk_local(
        [_l2n(conv[:, lanes(0, h)]) * (DN_HEAD_DIM ** -0.5) for h in heads], ks,
        [conv[:, lanes(2, h)] for h in heads], gcols, [gct[h:h + 1, :] for h in heads],
        [_lane_bcast(beta, DN_HEADS + h) for h in heads], tril, merges))
    seqs = range(nb)
    rows = lambda b: slice(b * sr, (b + 1) * sr)
    s0s = [[s0_ref[b, h] for b in seqs] for h in heads]
    r2s = [[_dot(jnp.concatenate([ws[h][rows(b)], qds[h][rows(b)]], axis=0).astype(BF16), s0s[h][b].astype(BF16))
            for b in seqs] for h in heads]
    vnews = [jnp.concatenate([us[h][rows(b)] - r2s[h][b][:sr] for b in seqs], axis=0) for h in heads]
    os_ = [_rms(jnp.concatenate([r2s[h][b][sr:] for b in seqs], axis=0)
                + _dot(qks[h].astype(BF16), vnews[h].astype(BF16)), ng) for h in heads]
    for h in heads:
        for b in seqs:
            y_ref[b, :, lanes(0, h)] = os_[h][b * sr:b * sr + t] * _silu(dz_ref[b, :, lanes(0, h)])
    glasts = [jnp.concatenate([jnp.broadcast_to(gcols[h][(b + 1) * sr - 1:(b + 1) * sr, :], (sr, LANES))
                               for b in seqs], axis=0) for h in heads]
    kdts = [(ks[h] * jnp.exp(glasts[h] - gcols[h])).T.astype(BF16) for h in heads]
    for h in heads:
        for b in seqs:
            vb = jnp.where(rowb == b, vnews[h], 0.0).astype(BF16)
            sout_ref[b, h] = s0s[h][b] * jnp.exp(glasts[h][b * sr:b * sr + 1, :]) + _dot(kdts[h], vb)


def _gdn_sample(buf, x, dz, ab, s0, cw, alog, dtb, ng):
    bsz, t, _ = x.shape
    assert t + DN_CONV - 1 <= GDN_SAMPLE_ROWS, "too many new tokens for one row tile per sequence"
    nb = CHUNK // GDN_SAMPLE_ROWS
    seq = lambda a: pl.BlockSpec((nb,) + a.shape[1:], lambda i: (i,) + (0,) * (a.ndim - 1))
    hist = pl.BlockSpec((buf.shape[0], nb, buf.shape[2]), lambda i: (0, i, 0))
    return pl.pallas_call(
        _gdn_sample_kernel,
        grid=(bsz // nb,),
        in_specs=[hist, seq(x), seq(dz), seq(ab), seq(s0), _resident(cw.shape),
                  _resident((1, LANES)), _resident((1, LANES)), _resident((1, LANES))],
        out_specs=[seq(dz), seq(s0), hist],
        out_shape=[jax.ShapeDtypeStruct(dz.shape, F32), jax.ShapeDtypeStruct(s0.shape, F32),
                   jax.ShapeDtypeStruct(buf.shape, F32)],
        scratch_shapes=[pltpu.VMEM((CHUNK + SUBLANES, 3 * DN_WIDTH), F32), pltpu.VMEM((CHUNK, LANES), F32)],
        compiler_params=_cparams("parallel"),
        name="gdn_sample",
    )(buf, x, dz, ab, s0, cw, alog, dtb, ng)


def _merge_kernel(x_ref, *rest, attend):
    if attend:
        sink_ref, q_ref, kp_ref, kc_ref, vp_ref, vc_ref, *rest = rest
    else:
        ya_ref, *rest = rest
    yb_ref, ga_ref, gb_ref, wa_ref, wb_ref, wo_ref, g_ref, wq_ref, h_ref, hq_ref = rest

    def deltanet_half():
        yb = _dot(yb_ref[...].astype(BF16), wb_ref[...])
        yield
        gate_a = jax.nn.sigmoid(ga_ref[...])
        yield
        return gate_a, jax.nn.sigmoid(gb_ref[...]) * yb

    if attend:
        ya, (gate_a, mix_b) = _interleave(
            (_swa_tile(sink_ref, q_ref, jnp.concatenate([kp_ref[...], kc_ref[...]], axis=0),
                       jnp.concatenate([vp_ref[...], vc_ref[...]], axis=0), pl.program_id(1) == 0), 1),
            (deltanet_half(), 1))
    else:
        ya = ya_ref[...]
        gate_a, mix_b = _drain(deltanet_half())
    mix = gate_a * _dot(ya.astype(BF16), wa_ref[...]) + mix_b
    h = x_ref[...] + _dot(mix.astype(BF16), wo_ref[...])
    h_ref[...] = h
    hq_ref[...] = _dot(_rms(h, g_ref[...]).astype(BF16), wq_ref[...]).astype(BF16)


def _merge(x, attn, yb, ga, gb, wa, wb, wo, g, wq, nseq, tm):
    m = x.shape[0]
    nt = m // nseq // tm
    row = lambda w: pl.BlockSpec((tm, w), lambda s, j: (s * nt + j, 0))
    attend = isinstance(attn, tuple)
    if attend:
        sinks, q, k, v = attn
        per = tm // WINDOW
        prev = pl.BlockSpec((WINDOW, A_KV_WIDTH), lambda s, j: (jnp.maximum((s * nt + j) * per - 1, 0), 0))
        attn_args = [sinks, q, k, k, v, v]
        attn_specs = [pl.BlockSpec(memory_space=pltpu.SMEM), row(A_WIDTH), prev, row(A_KV_WIDTH),
                      prev, row(A_KV_WIDTH)]
    else:
        attn_args, attn_specs = [attn], [row(A_WIDTH)]
    return pl.pallas_call(
        functools.partial(_merge_kernel, attend=attend),
        grid=(nseq, nt),
        in_specs=[row(D_MODEL), *attn_specs, row(DN_WIDTH), row(D_MODEL), row(D_MODEL),
                  _resident(wa.shape), _resident(wb.shape), _resident(wo.shape),
                  _resident((1, D_MODEL)), _resident(wq.shape)],
        out_specs=[row(D_MODEL), row(X_WIDTH)],
        out_shape=[jax.ShapeDtypeStruct((m, D_MODEL), F32), jax.ShapeDtypeStruct((m, X_WIDTH), BF16)],
        compiler_params=_cparams("parallel", "parallel"),
        name="merge",
    )(x, *attn_args, yb, ga, gb, wa, wb, wo, g, wq)


def _memkv_kernel(m_ref, g_ref, w_ref, k_ref, v_ref, kf_ref, vf_ref):
    tm = m_ref.shape[0]
    z = _dot(_rms(m_ref[...], g_ref[...]).astype(BF16), w_ref[...])
    k_ref[...] = z[:, :X_WIDTH]
    v_ref[...] = z[:, X_WIDTH:]
    for h in range(X_HEADS):
        kf_ref[pl.ds(h, tm, stride=X_HEADS), :] = z[:, h * LANES:(h + 1) * LANES]
        vf_ref[pl.ds(h, tm, stride=X_HEADS), :] = z[:, X_WIDTH + h * LANES:X_WIDTH + (h + 1) * LANES]


def _memkv(mem, g, w, tm):
    m = mem.shape[0]
    row = lambda wd: pl.BlockSpec((tm, wd), lambda i: (i, 0))
    flat = pl.BlockSpec((tm * X_HEADS, X_HEAD_DIM), lambda i: (i, 0))
    return pl.pallas_call(
        _memkv_kernel,
        grid=(m // tm,),
        in_specs=[row(D_MODEL), _resident((1, D_MODEL)), _resident(w.shape)],
        out_specs=[row(X_WIDTH), row(X_WIDTH), flat, flat],
        out_shape=[jax.ShapeDtypeStruct((m, X_WIDTH), F32)] * 2
        + [jax.ShapeDtypeStruct((m * X_HEADS, X_HEAD_DIM), F32)] * 2,
        compiler_params=_cparams("parallel"),
        name="mem_kv",
    )(mem, g, w)


def _softmax_rows(s):
    e = jnp.exp(s - jnp.max(s, axis=-1, keepdims=True))
    return e * (1.0 / jnp.sum(e, axis=-1, keepdims=True))


def _xattn_sample_kernel(hq_ref, mk_ref, mv_ref, o_ref, *, nb):
    rows = hq_ref.shape[1]
    nmh = mk_ref.shape[1]
    qh = lax.broadcasted_iota(jnp.int32, (X_HEADS * rows, nmh), 0) // rows
    mh = lax.broadcasted_iota(jnp.int32, (X_HEADS * rows, nmh), 1) % X_HEADS
    same = qh == mh
    seqs = range(nb)
    qs = [jnp.concatenate([hq_ref[b, :, h * LANES:(h + 1) * LANES] for h in range(X_HEADS)], axis=0)
          for b in seqs]
    ss = [_dot_nt(qs[b], mk_ref[b].astype(BF16)) * (X_HEAD_DIM ** -0.5) for b in seqs]
    ps = [_softmax_rows(jnp.where(same, s, -jnp.inf)).astype(BF16) for s in ss]
    os_ = [_dot(ps[b], mv_ref[b].astype(BF16)) for b in seqs]
    for b in seqs:
        for h in range(X_HEADS):
            o_ref[b, :, h * LANES:(h + 1) * LANES] = os_[b][h * rows:(h + 1) * rows].astype(BF16)


def _xattn_sample(hq, mk, mv, nb=8):
    bsz, rows, _ = hq.shape
    q = pl.BlockSpec((nb, rows, X_WIDTH), lambda i: (i, 0, 0))
    mem = pl.BlockSpec((nb, mk.shape[1], X_HEAD_DIM), lambda i: (i, 0, 0))
    return pl.pallas_call(
        functools.partial(_xattn_sample_kernel, nb=nb),
        grid=(bsz // nb,),
        in_specs=[q, mem, mem],
        out_specs=q,
        out_shape=jax.ShapeDtypeStruct(hq.shape, BF16),
        compiler_params=_cparams("parallel"),
        name="xattn_sample",
    )(hq, mk, mv)


def _xattn_tile(hq_ref, mk_ref, mv_ref):
    heads = [slice(h * LANES, (h + 1) * LANES) for h in range(X_HEADS)]
    ss = [_dot_nt(hq_ref[:, hs], mk_ref[:, hs].astype(BF16)) * (X_HEAD_DIM ** -0.5) for hs in heads]
    ps = [_softmax_rows(s).astype(BF16) for s in ss]
    return jnp.concatenate([_dot(p, mv_ref[:, hs].astype(BF16)).astype(BF16) for p, hs in zip(ps, heads)], axis=1)


def _ffn_kernel(h_ref, *rest, stride, padc, fc, attend):
    if attend:
        hq_ref, mk_ref, mv_ref, *rest = rest
        xo = _xattn_tile(hq_ref, mk_ref, mv_ref)
    else:
        xo_ref, *rest = rest
        xo = xo_ref[...]
    cin_ref, wxo_ref, g_ref, wu_ref, wv_ref, cw_ref, wd_ref, fg_ref, y_ref, tail_ref, xp_ref, acc_ref = rest
    tm = h_ref.shape[0]

    @pl.when(pl.program_id(1) == 0)
    def _():
        tail_ref[...] = cin_ref[...]

    h = h_ref[...] + _dot(xo, wxo_ref[...])
    acc_ref[...] = h
    hn = _rms(h, g_ref[...]).astype(BF16)
    for lo, hi in zip((0,) + fc, fc + (D_FF,)):
        cs = slice(lo, hi)
        n = hi - lo
        u = _dot(hn, wu_ref[:, cs])
        gate = _dot(hn, wv_ref[:, cs])
        xp_ref[0:padc, 0:n] = tail_ref[0, :, cs]
        xp_ref[padc:padc + tm, 0:n] = u
        cw = cw_ref[:, cs]
        uc = u * cw[2:3]
        for j in range(FFN_CONV - 1):
            o = padc - (FFN_CONV - 1 - j) * stride
            uc = uc + xp_ref[o:o + tm, 0:n] * cw[j:j + 1]
        tail_ref[0, :, cs] = xp_ref[tm:tm + padc, 0:n]
        acc_ref[...] += _dot((_silu(uc) * gate).astype(BF16), wd_ref[cs, :])
    y_ref[...] = _rms(acc_ref[...], fg_ref[...])


FFN_CUTS = ()


def _ffn(h, xq, mem, cin, wxo, g, wup, cw, wd, fg, tm, stride, fc=FFN_CUTS):
    m = h.shape[0]
    half = lambda c: pl.BlockSpec((D_MODEL, D_FF), lambda *_: (0, c), pipeline_mode=pl.Buffered(1))
    nseq, padc, _ = cin.shape
    nt = m // nseq // tm
    row = lambda w: pl.BlockSpec((tm, w), lambda s, j: (s * nt + j, 0))
    car = pl.BlockSpec((1, padc, D_FF), lambda s, j: (s, 0, 0))
    mem = list(mem or ())
    mem_specs = [pl.BlockSpec((a.shape[0] // nseq, X_WIDTH), lambda s, j: (s, 0)) for a in mem]
    return pl.pallas_call(
        functools.partial(_ffn_kernel, stride=stride, padc=padc, fc=fc, attend=bool(mem)),
        grid=(nseq, nt),
        in_specs=[row(D_MODEL), row(X_WIDTH), *mem_specs, car, _resident(wxo.shape), _resident((1, D_MODEL)),
                  half(0), half(1), _resident(cw.shape), _resident(wd.shape), _resident((1, D_MODEL))],
        out_specs=[row(D_MODEL), car],
        out_shape=[jax.ShapeDtypeStruct((m, D_MODEL), F32), jax.ShapeDtypeStruct(cin.shape, F32)],
        scratch_shapes=[pltpu.VMEM((tm + padc, max(b - a for a, b in zip((0,) + fc, fc + (D_FF,)))), F32),
                        pltpu.VMEM((tm, D_MODEL), F32)],
        compiler_params=_cparams("parallel", "arbitrary"),
        name="ffn",
    )(h, xq, *mem, cin, wxo, g, wup, wup, cw, wd, fg)


def _lane_row(vec, offset=0):
    return jnp.zeros((1, LANES), F32).at[0, offset:offset + vec.shape[0]].set(vec.astype(F32))


W_IN_QKV = A_WIDTH + 2 * A_KV_WIDTH
W_IN_D = W_IN_QKV + 4 * DN_WIDTH
W_IN_AB = W_IN_D + 2 * DN_HEADS
W_IN_G = W_IN_AB + 2 * D_MODEL


def _split_w_in_kernel(wt_ref, qkv_ref, d_ref, ab_ref, g_ref):
    tk = wt_ref.shape[1]
    qkv_ref[...] = wt_ref[:W_IN_QKV, :].T.astype(BF16)
    d_ref[...] = wt_ref[W_IN_QKV:W_IN_D, :].T.astype(BF16)
    ab = jnp.concatenate([wt_ref[W_IN_D:W_IN_AB, :], jnp.zeros((LANES - (W_IN_AB - W_IN_D), tk), F32)], axis=0)
    ab_ref[...] = ab.T.astype(BF16)
    g_ref[...] = wt_ref[W_IN_AB:W_IN_G, :].T.astype(BF16)


def _split_w_in(w_in_t, tk=256):
    k = w_in_t.shape[1]
    widths = (W_IN_QKV, W_IN_D - W_IN_QKV, LANES, W_IN_G - W_IN_AB)
    return pl.pallas_call(
        _split_w_in_kernel,
        grid=(k // tk,),
        in_specs=[pl.BlockSpec((w_in_t.shape[0], tk), lambda i: (0, i))],
        out_specs=[pl.BlockSpec((tk, wd), lambda i: (i, 0)) for wd in widths],
        out_shape=[jax.ShapeDtypeStruct((k, wd), BF16) for wd in widths],
        compiler_params=_cparams("parallel"),
        name="split_w_in",
    )(w_in_t)


def _prep(p):
    wqkv, wd, wab, wg = _split_w_in(p["w_in"].T)
    w = {
        "wqkv": wqkv,
        "wd": wd,
        "wab": wab,
        "wg": wg,
        "norm_mix_g": p["norm_mix_g"].reshape(1, D_MODEL),
        "dn_conv_w": p["dn_conv_w"],
        "alog": _lane_row(p["dn_a_log"]),
        "dtb": _lane_row(p["dn_dt_bias"]),
        "dn_norm_g": p["dn_norm_g"].reshape(1, DN_HEAD_DIM),
        "sinks": p["sinks"].astype(F32),
        "w_br_a": p["w_br_a"].astype(BF16),
        "w_br_b": p["w_br_b"].astype(BF16),
        "w_mix_out": p["w_mix_out"].astype(BF16),
        "norm_x_g": p["norm_x_g"].reshape(1, D_MODEL),
        "w_xq": p["w_xq"].astype(BF16),
        "w_xo": p["w_xo"].astype(BF16),
        "norm_ffn_g": p["norm_ffn_g"].reshape(1, D_MODEL),
        "w_up": p["w_up"].astype(BF16),
        "ffn_conv_w": p["ffn_conv_w"],
        "w_down": p["w_down"].astype(BF16),
        "final_norm_g": p["final_norm_g"].reshape(1, D_MODEL),
    }
    return w


def _tile(m, pref):
    return pref if m % pref == 0 else m


def _prompt_layer(x, mem, norm_mem_g, w_xkv, w):
    bsz, seqlen, _ = x.shape
    m = bsz * seqlen
    x2 = x.reshape(m, D_MODEL)
    tm = _tile(seqlen, ROW_TILE)
    tabs = _rope_tables(jnp.arange(seqlen, dtype=jnp.int32))
    assert seqlen >= WINDOW, "the window outputs are taken from a full last window"
    q, k, v, ga, gb, kwin, vwin, yb, dn_state, dn_tail = _proj_gdn(
        x2, w["norm_mix_g"], tabs, w["wqkv"], w["wd"], w["wab"], w["wg"], w["dn_conv_w"], w["alog"], w["dtb"],
        w["dn_norm_g"], bsz, tm)
    h, hq = _merge(x2, (w["sinks"], q, k, v), yb, ga, gb, w["w_br_a"], w["w_br_b"], w["w_mix_out"],
                   w["norm_x_g"], w["w_xq"], bsz, tm)
    nm = mem.shape[1]
    mk, mv, mk_rows, mv_rows = _memkv(mem.reshape(bsz * nm, D_MODEL), norm_mem_g.reshape(1, D_MODEL),
                                      w_xkv.astype(BF16), _tile(bsz * nm, ROW_TILE))
    cin = jnp.zeros((bsz, SUBLANES, D_FF), F32)
    y, tail = _ffn(h, hq, (mk, mv), cin, w["w_xo"], w["norm_ffn_g"], w["w_up"], w["ffn_conv_w"],
                   w["w_down"], w["final_norm_g"], tm, 1)
    win = lambda a: jnp.transpose(a.reshape(bsz, A_KV_HEADS, A_HEAD_DIM, WINDOW), (0, 3, 1, 2))
    new = (
        win(kwin),
        win(vwin),
        dn_tail[:, SUBLANES - (DN_CONV - 1):],
        dn_state,
        mk_rows.reshape(bsz, nm, X_HEADS, X_HEAD_DIM),
        mv_rows.reshape(bsz, nm, X_HEADS, X_HEAD_DIM),
        tail[:, SUBLANES - (FFN_CONV - 1):],
    )
    return y.reshape(bsz, seqlen, D_MODEL), new


def _sample_layer(x, pos0, win_k, win_v, dn_buf, dn_state, mem_k, mem_v, ffn_buf, w):
    bsz, t, _ = x.shape
    m = bsz * t
    sr = SAMPLE_ROWS
    x2 = x.reshape(m, D_MODEL)
    tabs = _rope_tables(jnp.tile(pos0 + jnp.arange(t, dtype=jnp.int32), bsz))
    tm = m // 2 if t % 2 == 0 else m
    q, k, v, dqkv, dz, ab, ga, gb = _proj(x2, w["norm_mix_g"], tabs, w["wqkv"], w["wd"], w["wab"], w["wg"], tm)

    qh = jnp.transpose(q.reshape(bsz, t, A_HEADS, A_HEAD_DIM), (0, 2, 1, 3)).reshape(bsz, A_HEADS * t, A_HEAD_DIM)
    padrows = lambda a: jnp.pad(a.reshape(bsz, t, -1), ((0, 0), (0, sr - t), (0, 0)))
    kv_major = lambda a: jnp.transpose(a, (0, 2, 3, 1))
    nbw = min(bsz, LANES // t)

    def new_t(a):
        a = jnp.transpose(a.reshape(bsz // nbw, nbw * t, A_KV_HEADS, A_HEAD_DIM), (0, 2, 3, 1))
        return jnp.pad(a, ((0, 0), (0, 0), (0, 0), (0, LANES - nbw * t)))

    oh, new_wk, new_wv = _swa_sample(qh, kv_major(win_k), kv_major(win_v), new_t(k), new_t(v), w["sinks"], t)
    ya = jnp.transpose(oh.reshape(bsz, A_HEADS, t, A_HEAD_DIM), (0, 2, 1, 3)).reshape(m, A_WIDTH)
    new_wk, new_wv = (jnp.transpose(a, (0, 3, 1, 2)) for a in (new_wk, new_wv))

    yb, new_s, new_dn_buf = _gdn_sample(
        jnp.transpose(dn_buf, (1, 0, 2)), dqkv.reshape(bsz, t, 3 * DN_WIDTH), dz.reshape(bsz, t, DN_WIDTH),
        ab.reshape(bsz, t, LANES), dn_state, w["dn_conv_w"], w["alog"], w["dtb"], w["dn_norm_g"])
    yb = yb.reshape(m, DN_WIDTH)
    new_dn_buf = jnp.transpose(new_dn_buf, (1, 0, 2))

    h, hq = _merge(x2, ya, yb, ga, gb, w["w_br_a"], w["w_br_b"], w["w_mix_out"], w["norm_x_g"], w["w_xq"], 1, tm)
    nm = mem_k.shape[1]
    xo = _xattn_sample(padrows(hq), mem_k.reshape(bsz, nm * X_HEADS, X_HEAD_DIM),
                       mem_v.reshape(bsz, nm * X_HEADS, X_HEAD_DIM))
    xo = xo[:, :t]

    tmaj = lambda a: jnp.transpose(a.reshape(bsz, t, -1), (1, 0, 2)).reshape(m, -1)
    cin = jnp.transpose(ffn_buf, (1, 0, 2)).reshape(1, (FFN_CONV - 1) * bsz, D_FF)
    y, tail = _ffn(tmaj(h), tmaj(xo), None, cin, w["w_xo"], w["norm_ffn_g"], w["w_up"], w["ffn_conv_w"],
                   w["w_down"], w["final_norm_g"], tm, bsz)
    y = jnp.transpose(y.reshape(t, bsz, D_MODEL), (1, 0, 2))
    new_ffn = jnp.transpose(tail.reshape(FFN_CONV - 1, bsz, D_FF), (1, 0, 2))
    return y, (new_wk, new_wv, new_dn_buf, new_s, new_ffn)


def kernel(x_prompt, x_sample, mem_prompt, cache_win_k, cache_win_v, state_dn_conv, state_dn, cache_mem_k, cache_mem_v, state_ffn_conv, norm_mix_g, w_in, dn_conv_w, dn_a_log, dn_dt_bias, dn_norm_g, attn_sinks, w_br_a, w_br_b, w_mix_out, norm_x_g, norm_mem_g, w_xq, w_xkv, w_xo, norm_ffn_g, w_up, ffn_conv_w, w_down, final_norm_g):
    p = {"norm_mix_g": norm_mix_g[0], "w_in": w_in[0], "dn_conv_w": dn_conv_w[0], "dn_a_log": dn_a_log[0],
         "dn_dt_bias": dn_dt_bias[0], "dn_norm_g": dn_norm_g[0], "sinks": attn_sinks[0], "w_br_a": w_br_a[0],
         "w_br_b": w_br_b[0], "w_mix_out": w_mix_out[0], "norm_x_g": norm_x_g[0], "w_xq": w_xq[0],
         "w_xo": w_xo[0], "norm_ffn_g": norm_ffn_g[0], "w_up": w_up[0], "ffn_conv_w": ffn_conv_w[0],
         "w_down": w_down[0], "final_norm_g": final_norm_g}
    w = _prep(p)
    yp, newp = _prompt_layer(x_prompt, mem_prompt, norm_mem_g[0], w_xkv[0], w)
    ys, news = _sample_layer(x_sample, PAST_LEN, cache_win_k[0], cache_win_v[0], state_dn_conv[0], state_dn[0],
                             cache_mem_k[0], cache_mem_v[0], state_ffn_conv[0], w)
    lead = lambda a: a[None]
    p_win_k, p_win_v, p_dn_conv, p_dn_state, p_mem_k, p_mem_v, p_ffn_conv = [lead(a) for a in newp]
    s_win_k, s_win_v, s_dn_conv, s_dn_state, s_ffn_conv = [lead(a) for a in news]
    return (yp, ys, p_win_k, p_win_v, p_dn_conv, p_dn_state, p_mem_k, p_mem_v, p_ffn_conv,
            s_win_k, s_win_v, s_dn_conv, s_dn_state, s_ffn_conv)
```

```python
import functools

import jax
import jax.numpy as jnp
from jax import lax
from jax.experimental import pallas as pl
from jax.experimental.pallas import tpu as pltpu

F32 = jnp.float32
BF16 = jnp.bfloat16

D_MODEL = 1024
A_HEADS = 8
A_KV_HEADS = 2
A_HEAD_DIM = 64
A_WIDTH = 512
A_KV_WIDTH = 128
WINDOW = 128
ROT_DIM = 16
ROPE_THETA = 500000.0
DN_HEADS = 4
DN_HEAD_DIM = 128
DN_WIDTH = 512
DN_CONV = 4
X_HEADS = 4
X_HEAD_DIM = 128
X_WIDTH = 512
D_FF = 2816
FFN_CONV = 3
EPS = 1e-6
PAST_LEN = 16384

LANES = 128
SUBLANES = 8
VMEM_LIMIT = 56 * 1024 * 1024
ROW_TILE = 512
CHUNK = 128
SAMPLE_ROWS = 16
GDN_SAMPLE_ROWS = 8


def _cparams(*sem):
    return pltpu.CompilerParams(dimension_semantics=sem, vmem_limit_bytes=VMEM_LIMIT)


def _resident(shape):
    return pl.BlockSpec(shape, lambda *_: (0,) * len(shape), pipeline_mode=pl.Buffered(1))


def _rms(x, g):
    return x * lax.rsqrt(jnp.mean(x * x, axis=-1, keepdims=True) + EPS) * g


def _dot(a, b):
    return jnp.dot(a, b, preferred_element_type=F32)


def _dot_nt(a, b):
    return lax.dot_general(a, b, (((1,), (1,)), ((), ())), preferred_element_type=F32)


def _silu(x):
    return x * jax.nn.sigmoid(x)


def _rope(seg, c, s1, s2):
    return seg * c + pltpu.roll(seg, LANES - 8, 1) * s1 + pltpu.roll(seg, 8, 1) * s2


PROJ_COLS = 512
assert (3 * DN_WIDTH) % PROJ_COLS == 0 and D_MODEL % PROJ_COLS == 0


def _proj_tile(x_ref, g_ref, c_ref, s1_ref, s2_ref, wqkv_ref, wd_ref, wab_ref, wg_ref,
               q_ref, k_ref, v_ref, dqkv_ref, dz_ref, ab_ref, ga_ref, gb_ref, dqkv_row0=0, win_refs=None):
    tm = x_ref.shape[0]
    xb = _rms(x_ref[...], g_ref[...]).astype(BF16)
    yield
    c, s1, s2 = c_ref[...], s1_ref[...], s2_ref[...]
    z = _dot(xb, wqkv_ref[...])
    for i in range(A_WIDTH // LANES):
        sl = slice(i * LANES, (i + 1) * LANES)
        q_ref[:, sl] = _rope(z[:, sl], c, s1, s2).astype(BF16)
    k = _rope(z[:, A_WIDTH:A_WIDTH + LANES], c, s1, s2)
    k_ref[...] = k
    v_ref[...] = z[:, A_WIDTH + LANES:]
    if win_refs is not None:
        win_refs[0][0] = k[tm - WINDOW:].T
        win_refs[1][0] = z[tm - WINDOW:, A_WIDTH + LANES:].T
    yield
    pc = PROJ_COLS
    for j in range(3 * DN_WIDTH // pc):
        dqkv_ref[dqkv_row0:dqkv_row0 + tm, j * pc:(j + 1) * pc] = _dot(xb, wd_ref[:, j * pc:(j + 1) * pc])
        yield
    dz_ref[...] = _dot(xb, wd_ref[:, 3 * DN_WIDTH:])
    ab_ref[...] = _dot(xb, wab_ref[...])
    yield
    for dst, base in ((ga_ref, 0), (gb_ref, D_MODEL)):
        for j in range(D_MODEL // pc):
            dst[:, j * pc:(j + 1) * pc] = _dot(xb, wg_ref[:, base + j * pc:base + (j + 1) * pc])
            yield


def _proj_kernel(*refs):
    _drain(_proj_tile(*refs))


def _proj(x, g, tabs, wqkv, wd, wab, wg, tm):
    m = x.shape[0]
    nt = tabs[0].shape[0] // tm
    row = lambda w: pl.BlockSpec((tm, w), lambda i: (i, 0))
    tab = pl.BlockSpec((tm, LANES), lambda i: (i % nt, 0))
    widths = (A_WIDTH, LANES, LANES, 3 * DN_WIDTH, DN_WIDTH, LANES, D_MODEL, D_MODEL)
    dts = (BF16, F32, F32, F32, F32, F32, F32, F32)
    return pl.pallas_call(
        _proj_kernel,
        grid=(m // tm,),
        in_specs=[row(D_MODEL), _resident((1, D_MODEL)), tab, tab, tab,
                  _resident(wqkv.shape), _resident(wd.shape), _resident(wab.shape), _resident(wg.shape)],
        out_specs=[row(w) for w in widths],
        out_shape=[jax.ShapeDtypeStruct((m, w), d) for w, d in zip(widths, dts)],
        compiler_params=_cparams("parallel"),
        name="in_proj",
    )(x, g, *tabs, wqkv, wd, wab, wg)


def _rope_tables(pos):
    half = ROT_DIM // 2
    d = jnp.arange(LANES, dtype=jnp.int32) % A_HEAD_DIM
    inv = ROPE_THETA ** (-2.0 * (d % half).astype(F32) / ROT_DIM)
    ang = pos.astype(F32)[:, None] * inv[None, :]
    c, s = jnp.cos(ang), jnp.sin(ang)
    return (jnp.where(d < ROT_DIM, c, 1.0), jnp.where(d < half, -s, 0.0),
            jnp.where((d >= half) & (d < ROT_DIM), s, 0.0))


def _both_halves(t, lane_lo):
    r = pltpu.roll(t, A_HEAD_DIM, 1)
    return jnp.where(lane_lo, t, r), jnp.where(lane_lo, r, t)


def _band_softmax(s, lower, prev_bias, sink):
    w = s.shape[1] // 2
    prev = s[:, :w] if prev_bias is None else s[:, :w] + prev_bias
    live = jnp.where(lower, s[:, w:], prev)
    m = jnp.maximum(jnp.max(live, axis=-1, keepdims=True), sink)
    p = jnp.exp(live - m)
    p = p * (1.0 / (jnp.sum(p, axis=-1, keepdims=True) + jnp.exp(sink - m)))
    return jnp.concatenate([jnp.where(lower, 0.0, p), jnp.where(lower, p, 0.0)], axis=1).astype(BF16)


def _swa_tile(sink_ref, q_ref, k, v, first):
    w = WINDOW
    lane_lo_k = lax.broadcasted_iota(jnp.int32, k.shape, 1) < A_HEAD_DIM
    kk = [t.astype(BF16) for t in _both_halves(k, lane_lo_k)]
    vv = [t.astype(BF16) for t in _both_halves(v, lane_lo_k)]
    lane_lo = lax.broadcasted_iota(jnp.int32, (w, LANES), 1) < A_HEAD_DIM
    row = lax.broadcasted_iota(jnp.int32, (4 * w, w), 0)
    col = lax.broadcasted_iota(jnp.int32, (4 * w, w), 1)
    lower = col <= (row & (w - 1))
    first_bias = jnp.where(first, -jnp.inf, 0.0)
    hrow = lax.broadcasted_iota(jnp.int32, (4 * w, 1), 0) // w
    sinks = []
    for g in range(A_KV_HEADS):
        sink = jnp.zeros((4 * w, 1), F32)
        for j in range(4):
            sink = jnp.where(hrow == j, sink_ref[4 * g + j], sink)
        sinks.append(sink)
    zero = jnp.zeros((), BF16)
    scale = jnp.asarray(A_HEAD_DIM ** -0.5, BF16)
    nblk = q_ref.shape[0] // w
    probs = [(b, g) for b in range(nblk) for g in range(A_KV_HEADS)]

    def queries(b, g):
        parts = []
        for sgm in range(2):
            seg = q_ref[b * w:(b + 1) * w, (2 * g + sgm) * LANES:(2 * g + sgm + 1) * LANES] * scale
            parts += [jnp.where(lane_lo, seg, zero), jnp.where(lane_lo, zero, seg)]
        return jnp.concatenate(parts, axis=0)

    ss = [_dot_nt(queries(b, g), kk[g][b * w:(b + 2) * w]) for b, g in probs]
    yield
    ps = [_band_softmax(s, lower, first_bias if b == 0 else None, sinks[g]) for s, (b, g) in zip(ss, probs)]
    yield
    os_ = [_dot(p, vv[g][b * w:(b + 2) * w]) for p, (b, g) in zip(ps, probs)]
    yield
    segs = [[jnp.where(lane_lo, o[(2 * sgm) * w:(2 * sgm + 1) * w], o[(2 * sgm + 1) * w:(2 * sgm + 2) * w]
                       ).astype(BF16) for sgm in range(2)] for o in os_]
    return jnp.concatenate([jnp.concatenate(segs[A_KV_HEADS * b] + segs[A_KV_HEADS * b + 1], axis=1)
                            for b in range(nblk)], axis=0)


def _swa_sample_kernel(sink_ref, q_ref, ckt_ref, cvt_ref, knt_ref, vnt_ref, o_ref, cko_ref, cvo_ref, *, nb, t):
    w = ckt_ref.shape[3]
    old = lax.broadcasted_iota(jnp.int32, (A_HEAD_DIM, w), 1) < w - t
    for src_ref, new_ref, dst_ref in ((ckt_ref, knt_ref, cko_ref), (cvt_ref, vnt_ref, cvo_ref)):
        for g in range(A_KV_HEADS):
            new = new_ref[0, g]
            for b in range(nb):
                dst_ref[b, g] = jnp.where(old, pltpu.roll(src_ref[b, g], w - t, 1),
                                          pltpu.roll(new, (w - t - t * b) % LANES, 1))
    rows = q_ref.shape[1] // A_HEADS
    gq = rows * (A_HEADS // A_KV_HEADS)
    r = lax.broadcasted_iota(jnp.int32, (gq, w), 0)
    c = lax.broadcasted_iota(jnp.int32, (gq, w), 1)
    valid_c = c > r % rows
    rn = lax.broadcasted_iota(jnp.int32, (gq, LANES), 0)
    cn = lax.broadcasted_iota(jnp.int32, (gq, LANES), 1)
    causal_n = cn % t <= rn % rows
    valid_n = [causal_n & (cn // t == b) for b in range(nb)]
    hrow = lax.broadcasted_iota(jnp.int32, (gq, 1), 0) // rows
    sink = []
    for g in range(A_KV_HEADS):
        sk = jnp.zeros((gq, 1), F32)
        for j in range(A_HEADS // A_KV_HEADS):
            sk = jnp.where(hrow == j, sink_ref[g * (A_HEADS // A_KV_HEADS) + j], sk)
        sink.append(sk)
    probs = [(b, g) for b in range(nb) for g in range(A_KV_HEADS)]
    scale = A_HEAD_DIM ** -0.5
    qs = [q_ref[b, g * gq:(g + 1) * gq, :] for b, g in probs]
    scs = [jnp.where(valid_c, _dot(q, ckt_ref[b, g].astype(BF16)) * scale, -jnp.inf)
           for q, (b, g) in zip(qs, probs)]
    knt = [knt_ref[0, g].astype(BF16) for g in range(A_KV_HEADS)]
    vnt = [vnt_ref[0, g].astype(BF16) for g in range(A_KV_HEADS)]
    sns = [jnp.where(valid_n[b], _dot(q, knt[g]) * scale, -jnp.inf) for q, (b, g) in zip(qs, probs)]
    ms = [jnp.maximum(jnp.maximum(jnp.max(sc, -1, keepdims=True), jnp.max(sn, -1, keepdims=True)), sink[g])
          for sc, sn, (b, g) in zip(scs, sns, probs)]
    pcs = [jnp.exp(sc - m) for sc, m in zip(scs, ms)]
    pns = [jnp.exp(sn - m) for sn, m in zip(sns, ms)]
    invs = [1.0 / (jnp.sum(pc, -1, keepdims=True) + jnp.sum(pn, -1, keepdims=True) + jnp.exp(sink[g] - m))
            for pc, pn, m, (b, g) in zip(pcs, pns, ms, probs)]
    for pc, pn, inv, (b, g) in zip(pcs, pns, invs, probs):
        o_ref[b, g * gq:(g + 1) * gq, :] = (_dot_nt((pc * inv).astype(BF16), cvt_ref[b, g].astype(BF16))
                                            + _dot_nt((pn * inv).astype(BF16), vnt[g]))


def _swa_sample(q, ckt, cvt, knt, vnt, sinks, t):
    bsz, nq, _ = q.shape
    nb = bsz // knt.shape[0]
    assert nb * t <= LANES and ckt.shape[3] == LANES
    blk = lambda a, n=nb: pl.BlockSpec((n,) + a.shape[1:], lambda i: (i,) + (0,) * (a.ndim - 1))
    return pl.pallas_call(
        functools.partial(_swa_sample_kernel, nb=nb, t=t),
        grid=(bsz // nb,),
        in_specs=[pl.BlockSpec(memory_space=pltpu.SMEM), blk(q), blk(ckt), blk(cvt), blk(knt, 1), blk(vnt, 1)],
        out_specs=[blk(q), blk(ckt), blk(cvt)],
        out_shape=[jax.ShapeDtypeStruct(q.shape, F32), jax.ShapeDtypeStruct(ckt.shape, F32),
                   jax.ShapeDtypeStruct(cvt.shape, F32)],
        compiler_params=_cparams("parallel"),
        name="swa_sample",
    )(sinks, q, ckt, cvt, knt, vnt)


def _lane_bcast(x, lane):
    return jnp.broadcast_to(x[:, lane:lane + 1], (x.shape[0], LANES))


def _cumsum_rows(x, block):
    c = x.shape[0]
    rowi = lax.broadcasted_iota(jnp.int32, (c, c), 0)
    coli = lax.broadcasted_iota(jnp.int32, (c, c), 1)
    ones = (((rowi // block) == (coli // block)) & (rowi >= coli)).astype(BF16)
    hi = x.astype(BF16)
    r1 = x - hi.astype(F32)
    mid = r1.astype(BF16)
    lo = (r1 - mid.astype(F32)).astype(BF16)
    s = _dot(ones, jnp.concatenate([hi, mid, lo], axis=1))
    return s[:, :LANES] + s[:, LANES:2 * LANES] + s[:, 2 * LANES:]


def _l2n(t):
    return t * lax.rsqrt(jnp.sum(t * t, axis=-1, keepdims=True) + EPS)


def _gates(ab, alog, dtb):
    x = ab + dtb
    sp = jnp.maximum(x, 0.0) + jnp.log1p(jnp.exp(-jnp.abs(x)))
    return -jnp.exp(alog) * sp, jax.nn.sigmoid(ab)


def _merge_masks(c, top):
    rowi = lax.broadcasted_iota(jnp.int32, (c, c), 0)
    coli = lax.broadcasted_iota(jnp.int32, (c, c), 1)
    masks = []
    s = 1
    while s < top:
        rb, cb = rowi // s, coli // s
        masks.append(((rb // 2) == (cb // 2)) & ((rb % 2) == 1) & ((cb % 2) == 0))
        s *= 2
    return masks


def _each(f, *lists):
    return [f(*t) for t in zip(*lists)]


def _drain(stages):
    try:
        while True:
            next(stages)
    except StopIteration as done:
        return done.value


def _interleave(*staged):
    live = list(staged)
    values = {}
    while live:
        for item in tuple(live):
            stages, per_turn = item
            try:
                for _ in range(per_turn):
                    next(stages)
            except StopIteration as done:
                values[id(stages)] = done.value
                live.remove(item)
    return [values[id(stages)] for stages, _ in staged]


def _chunk_local(qs, ks, vs, gcols, grows, betas, tril, merges):
    c = qs[0].shape[0]
    decays = _each(lambda gc, gr: jnp.exp(jnp.minimum(gc - gr, 0.0)), gcols, grows)
    kbs = _each(lambda k, b: k * b, ks, betas)
    yield
    kts = [k.T for k in ks]
    ms = _each(lambda q, kb, kt: _dot(jnp.concatenate([q, kb], axis=0).astype(BF16), kt.astype(BF16)),
               qs, kbs, kts)
    yield
    qks = _each(lambda m, d: jnp.where(tril, m[:c] * d, 0.0), ms, decays)
    a = _each(lambda m, d: m[c:] * d, ms, decays)
    eye = (lax.broadcasted_iota(jnp.int32, (c, c), 0) == lax.broadcasted_iota(jnp.int32, (c, c), 1)).astype(F32)
    ts = [eye - jnp.where(merges[0], x, 0.0) if merges else eye for x in a]
    yield
    for off in merges[1:]:
        tbs = [t.astype(BF16) for t in ts]
        zs = _each(lambda x, tb: _dot(jnp.where(off, x, 0.0).astype(BF16), tb), a, tbs)
        yield
        ts = _each(lambda t, tb, z: t - _dot(tb, z.astype(BF16)), ts, tbs, zs)
        yield
    ns = [t - eye for t in ts]
    egs = [jnp.exp(gc) for gc in gcols]
    rhss = _each(lambda v, b, kb, eg: jnp.concatenate([v * b, kb * eg], axis=1), vs, betas, kbs, egs)
    yield
    uws = _each(lambda r, n: r + _dot(n.astype(BF16), r.astype(BF16)), rhss, ns)
    yield
    return ([x[:, :LANES] for x in uws], [x[:, LANES:] for x in uws], qks,
            _each(lambda q, eg: q * eg, qs, egs), kts)


def _gdn_tile(xp_ref, dz_ref, ab_ref, cw_ref, alog_ref, dtb_ref, ng_ref, y_ref, s_ref, nc):
    c = CHUNK
    pad = SUBLANES
    cw = cw_ref[...]
    g, beta = _gates(ab_ref[...], alog_ref[...], dtb_ref[...])
    rowi = lax.broadcasted_iota(jnp.int32, (c, c), 0)
    coli = lax.broadcasted_iota(jnp.int32, (c, c), 1)
    tril = rowi >= coli
    merges = _merge_masks(c, c)
    ng = ng_ref[...]
    heads = range(DN_HEADS)
    probs = [(ci, h) for ci in range(nc) for h in heads]
    rows = lambda ci: slice(ci * c, (ci + 1) * c)
    lanes = lambda part, h: slice(part * DN_WIDTH + h * LANES, part * DN_WIDTH + (h + 1) * LANES)
    yield
    qs, ks, vs, gcols, grows, betas = [], [], [], [], [], []
    for ci in range(nc):
        conv = xp_ref[pad + ci * c:pad + (ci + 1) * c, :] * cw[DN_CONV - 1:DN_CONV]
        for j in range(DN_CONV - 1):
            o = pad - (DN_CONV - 1) + j + ci * c
            conv = conv + xp_ref[o:o + c, :] * cw[j:j + 1]
        conv = _silu(conv)
        gc = _cumsum_rows(g[rows(ci)], c)
        gct = gc.T
        qs += [_l2n(conv[:, lanes(0, h)]) * (DN_HEAD_DIM ** -0.5) for h in heads]
        ks += [_l2n(conv[:, lanes(1, h)]) for h in heads]
        vs += [conv[:, lanes(2, h)] for h in heads]
        gcols += [_lane_bcast(gc, h) for h in heads]
        grows += [gct[h:h + 1, :] for h in heads]
        betas += [_lane_bcast(beta[rows(ci)], DN_HEADS + h) for h in heads]
        yield
    us, ws, qks, qds, kts = yield from _chunk_local(qs, ks, vs, gcols, grows, betas, tril, merges)
    glasts = [gc[c - 1:c, :] for gc in gcols]
    kdts = _each(lambda kt, gr: kt * jnp.exp(gr[:, c - 1:c] - gr), kts, grows)
    wqs = _each(lambda w, qd: jnp.concatenate([w, qd], axis=0).astype(BF16), ws, qds)
    qkks = _each(lambda qk, kdt: jnp.concatenate([qk, kdt], axis=0).astype(BF16), qks, kdts)
    yield
    ss = [s_ref[h] for h in heads]
    for ci in range(nc):
        pr = [ci * DN_HEADS + h for h in heads]
        r2s = [_dot(wqs[p], s.astype(BF16)) for p, s in zip(pr, ss)]
        vnews = [us[p] - r2[:c] for p, r2 in zip(pr, r2s)]
        yield
        r3s = [_dot(qkks[p], vn.astype(BF16)) for p, vn in zip(pr, vnews)]
        ss = [s * jnp.exp(glasts[p]) + r3[c:] for p, s, r3 in zip(pr, ss, r3s)]
        for h in heads:
            o = r2s[h][c:] + r3s[h][:c]
            y_ref[rows(ci), lanes(0, h)] = (_rms(o, ng) * _silu(dz_ref[rows(ci), lanes(0, h)])).astype(BF16)
        yield
    for h in heads:
        s_ref[h] = ss[h]


def _proj_gdn_kernel(x_ref, g_ref, c_ref, s1_ref, s2_ref, wqkv_ref, wd_ref, wab_ref, wg_ref,
                     cw_ref, alog_ref, dtb_ref, ng_ref,
                     q_ref, k_ref, v_ref, ga_ref, gb_ref, kw_ref, vw_ref, y_ref, sout_ref, tail_ref,
                     xp_ref, dz_ref, ab_ref, carry_ref, s_ref, *, nt, nc):
    i = pl.program_id(0)
    pad = SUBLANES
    r = nc * CHUNK
    slot_a = i % 2
    slot_b = 1 - slot_a

    @pl.when(i == 0)
    def _():
        xp_ref[1] = jnp.zeros(xp_ref.shape[1:], F32)
        dz_ref[1] = jnp.zeros(dz_ref.shape[1:], F32)
        ab_ref[1] = jnp.zeros(ab_ref.shape[1:], F32)

    @pl.when(jnp.maximum(i - 1, 0) % nt == 0)
    def _():
        carry_ref[...] = jnp.zeros_like(carry_ref)
        s_ref[...] = jnp.zeros_like(s_ref)

    xpb_ref = xp_ref.at[slot_b]
    xpb_ref[0:pad, :] = carry_ref[...]
    _interleave(
        (_gdn_tile(xpb_ref, dz_ref.at[slot_b], ab_ref.at[slot_b], cw_ref, alog_ref, dtb_ref, ng_ref,
                   y_ref, s_ref, nc), 1),
        (_proj_tile(x_ref, g_ref, c_ref, s1_ref, s2_ref, wqkv_ref, wd_ref, wab_ref, wg_ref, q_ref, k_ref, v_ref,
                    xp_ref.at[slot_a], dz_ref.at[slot_a], ab_ref.at[slot_a], ga_ref, gb_ref, pad,
                    (kw_ref, vw_ref)), 1))
    carry_ref[...] = xpb_ref[r:r + pad, :]
    tail_ref[0] = carry_ref[...]
    sout_ref[0] = s_ref[...]


def _proj_gdn(x, g, tabs, wqkv, wd, wab, wg, cw, alog, dtb, ng, nseq, tm):
    m = x.shape[0]
    nt = m // nseq // tm
    last = m // tm - 1
    nc = tm // CHUNK
    proj = lambda i: jnp.minimum(i, last)
    gdn = lambda i: jnp.maximum(i - 1, 0)
    prow = lambda w: pl.BlockSpec((tm, w), lambda i: (proj(i), 0))
    tab = pl.BlockSpec((tm, LANES), lambda i: (proj(i) % nt, 0))
    per_seq = lambda *dims: pl.BlockSpec((1,) + dims, lambda i: (gdn(i) // nt,) + (0,) * len(dims))
    win = pl.BlockSpec((1, A_KV_WIDTH, WINDOW), lambda i: (proj(i) // nt, 0, 0))
    win_shape = jax.ShapeDtypeStruct((nseq, A_KV_WIDTH, WINDOW), F32)
    state = (DN_HEADS, DN_HEAD_DIM, DN_HEAD_DIM)
    bufs = [pltpu.VMEM((2, tm + SUBLANES, 3 * DN_WIDTH), F32), pltpu.VMEM((2, tm, DN_WIDTH), F32),
            pltpu.VMEM((2, tm, LANES), F32)]
    return pl.pallas_call(
        functools.partial(_proj_gdn_kernel, nt=nt, nc=nc),
        grid=(m // tm + 1,),
        in_specs=[prow(D_MODEL), _resident((1, D_MODEL)), tab, tab, tab,
                  _resident(wqkv.shape), _resident(wd.shape), _resident(wab.shape), _resident(wg.shape),
                  _resident(cw.shape), _resident((1, LANES)), _resident((1, LANES)), _resident((1, LANES))],
        out_specs=[prow(A_WIDTH), prow(LANES), prow(LANES), prow(D_MODEL), prow(D_MODEL), win, win,
                   pl.BlockSpec((tm, DN_WIDTH), lambda i: (gdn(i), 0)), per_seq(*state),
                   per_seq(SUBLANES, 3 * DN_WIDTH)],
        out_shape=[jax.ShapeDtypeStruct((m, A_WIDTH), BF16), jax.ShapeDtypeStruct((m, LANES), F32),
                   jax.ShapeDtypeStruct((m, LANES), F32), jax.ShapeDtypeStruct((m, D_MODEL), F32),
                   jax.ShapeDtypeStruct((m, D_MODEL), F32), win_shape, win_shape,
                   jax.ShapeDtypeStruct((m, DN_WIDTH), BF16),
                   jax.ShapeDtypeStruct((nseq,) + state, F32),
                   jax.ShapeDtypeStruct((nseq, SUBLANES, 3 * DN_WIDTH), F32)],
        scratch_shapes=bufs + [pltpu.VMEM((SUBLANES, 3 * DN_WIDTH), F32), pltpu.VMEM(state, F32)],
        compiler_params=_cparams("arbitrary"),
        name="proj_gdn",
    )(x, g, *tabs, wqkv, wd, wab, wg, cw, alog, dtb, ng)


def _gdn_sample_kernel(buf_ref, x_ref, dz_ref, ab_ref, s0_ref, cw_ref, alog_ref, dtb_ref, ng_ref,
                       y_ref, sout_ref, bufo_ref, xp_ref, ab16_ref):
    t = x_ref.shape[1]
    c = CHUNK
    sr = GDN_SAMPLE_ROWS
    nb = c // sr
    pad = SUBLANES
    hist = DN_CONV - 1
    xp_ref[...] = jnp.zeros_like(xp_ref)
    ab16_ref[...] = jnp.zeros_like(ab16_ref)
    for b in range(nb):
        for j in range(hist):
            xp_ref[pad + b * sr - hist + j:pad + b * sr - hist + j + 1, :] = buf_ref[j, b:b + 1, :]
        xp_ref[pad + b * sr:pad + b * sr + t, :] = x_ref[b]
        ab16_ref[b * sr:b * sr + t, :] = ab_ref[b]
    for b in range(nb):
        for j in range(hist):
            o = pad + b * sr - hist + t + j
            bufo_ref[j, b:b + 1, :] = xp_ref[o:o + 1, :]
    cw = cw_ref[...]
    conv = xp_ref[pad:pad + c, :] * cw[hist:hist + 1]
    for j in range(hist):
        conv = conv + xp_ref[pad - hist + j:pad - hist + j + c, :] * cw[j:j + 1]
    conv = _silu(conv)
    live = (lax.broadcasted_iota(jnp.int32, (c, LANES), 0) % sr) < t
    g, beta = _gates(ab16_ref[...], alog_ref[...], dtb_ref[...])
    g = jnp.where(live, g, 0.0)
    beta = jnp.where(live, beta, 0.0)
    gc = _cumsum_rows(g, sr)
    gct = gc.T
    rowi = lax.broadcasted_iota(jnp.int32, (c, c), 0)
    coli = lax.broadcasted_iota(jnp.int32, (c, c), 1)
    tril = ((rowi // sr) == (coli // sr)) & (rowi >= coli)
    merges = _merge_masks(c, pl.next_power_of_2(t))
    ng = ng_ref[...]
    rowb = lax.broadcasted_iota(jnp.int32, (c, LANES), 0) // sr
    heads = range(DN_HEADS)
    lanes = lambda part, h: slice(part * DN_WIDTH + h * LANES, part * DN_WIDTH + (h + 1) * LANES)
    ks = [_l2n(conv[:, lanes(1, h)]) for h in heads]
    gcols = [_lane_bcast(gc, h) for h in heads]
    us, ws, qks, qds, _ = _drain(_chunk_local(
        [_l2n(conv[:, lanes(0, h)]) * (DN_HEAD_DIM ** -0.5) for h in heads], ks,
        [conv[:, lanes(2, h)] for h in heads], gcols, [gct[h:h + 1, :] for h in heads],
        [_lane_bcast(beta, DN_HEADS + h) for h in heads], tril, merges))
    seqs = range(nb)
    rows = lambda b: slice(b * sr, (b + 1) * sr)
    s0s = [[s0_ref[b, h] for b in seqs] for h in heads]
    r2s = [[_dot(jnp.concatenate([ws[h][rows(b)], qds[h][rows(b)]], axis=0).astype(BF16), s0s[h][b].astype(BF16))
            for b in seqs] for h in heads]
    vnews = [jnp.concatenate([us[h][rows(b)] - r2s[h][b][:sr] for b in seqs], axis=0) for h in heads]
    os_ = [_rms(jnp.concatenate([r2s[h][b][sr:] for b in seqs], axis=0)
                + _dot(qks[h].astype(BF16), vnews[h].astype(BF16)), ng) for h in heads]
    for h in heads:
        for b in seqs:
            y_ref[b, :, lanes(0, h)] = os_[h][b * sr:b * sr + t] * _silu(dz_ref[b, :, lanes(0, h)])
    glasts = [jnp.concatenate([jnp.broadcast_to(gcols[h][(b + 1) * sr - 1:(b + 1) * sr, :], (sr, LANES))
                               for b in seqs], axis=0) for h in heads]
    kdts = [(ks[h] * jnp.exp(glasts[h] - gcols[h])).T.astype(BF16) for h in heads]
    for h in heads:
        for b in seqs:
            vb = jnp.where(rowb == b, vnews[h], 0.0).astype(BF16)
            sout_ref[b, h] = s0s[h][b] * jnp.exp(glasts[h][b * sr:b * sr + 1, :]) + _dot(kdts[h], vb)


def _gdn_sample(buf, x, dz, ab, s0, cw, alog, dtb, ng):
    bsz, t, _ = x.shape
    assert t + DN_CONV - 1 <= GDN_SAMPLE_ROWS, "too many new tokens for one row tile per sequence"
    nb = CHUNK // GDN_SAMPLE_ROWS
    seq = lambda a: pl.BlockSpec((nb,) + a.shape[1:], lambda i: (i,) + (0,) * (a.ndim - 1))
    hist = pl.BlockSpec((buf.shape[0], nb, buf.shape[2]), lambda i: (0, i, 0))
    return pl.pallas_call(
        _gdn_sample_kernel,
        grid=(bsz // nb,),
        in_specs=[hist, seq(x), seq(dz), seq(ab), seq(s0), _resident(cw.shape),
                  _resident((1, LANES)), _resident((1, LANES)), _resident((1, LANES))],
        out_specs=[seq(dz), seq(s0), hist],
        out_shape=[jax.ShapeDtypeStruct(dz.shape, F32), jax.ShapeDtypeStruct(s0.shape, F32),
                   jax.ShapeDtypeStruct(buf.shape, F32)],
        scratch_shapes=[pltpu.VMEM((CHUNK + SUBLANES, 3 * DN_WIDTH), F32), pltpu.VMEM((CHUNK, LANES), F32)],
        compiler_params=_cparams("parallel"),
        name="gdn_sample",
    )(buf, x, dz, ab, s0, cw, alog, dtb, ng)


def _merge_kernel(x_ref, *rest, attend):
    if attend:
        sink_ref, q_ref, kp_ref, kc_ref, vp_ref, vc_ref, *rest = rest
    else:
        ya_ref, *rest = rest
    yb_ref, ga_ref, gb_ref, wa_ref, wb_ref, wo_ref, g_ref, wq_ref, h_ref, hq_ref = rest

    def deltanet_half():
        yb = _dot(yb_ref[...].astype(BF16), wb_ref[...])
        yield
        gate_a = jax.nn.sigmoid(ga_ref[...])
        yield
        return gate_a, jax.nn.sigmoid(gb_ref[...]) * yb

    if attend:
        ya, (gate_a, mix_b) = _interleave(
            (_swa_tile(sink_ref, q_ref, jnp.concatenate([kp_ref[...], kc_ref[...]], axis=0),
                       jnp.concatenate([vp_ref[...], vc_ref[...]], axis=0), pl.program_id(1) == 0), 1),
            (deltanet_half(), 1))
    else:
        ya = ya_ref[...]
        gate_a, mix_b = _drain(deltanet_half())
    mix = gate_a * _dot(ya.astype(BF16), wa_ref[...]) + mix_b
    h = x_ref[...] + _dot(mix.astype(BF16), wo_ref[...])
    h_ref[...] = h
    hq_ref[...] = _dot(_rms(h, g_ref[...]).astype(BF16), wq_ref[...]).astype(BF16)


def _merge(x, attn, yb, ga, gb, wa, wb, wo, g, wq, nseq, tm):
    m = x.shape[0]
    nt = m // nseq // tm
    row = lambda w: pl.BlockSpec((tm, w), lambda s, j: (s * nt + j, 0))
    attend = isinstance(attn, tuple)
    if attend:
        sinks, q, k, v = attn
        per = tm // WINDOW
        prev = pl.BlockSpec((WINDOW, A_KV_WIDTH), lambda s, j: (jnp.maximum((s * nt + j) * per - 1, 0), 0))
        attn_args = [sinks, q, k, k, v, v]
        attn_specs = [pl.BlockSpec(memory_space=pltpu.SMEM), row(A_WIDTH), prev, row(A_KV_WIDTH),
                      prev, row(A_KV_WIDTH)]
    else:
        attn_args, attn_specs = [attn], [row(A_WIDTH)]
    return pl.pallas_call(
        functools.partial(_merge_kernel, attend=attend),
        grid=(nseq, nt),
        in_specs=[row(D_MODEL), *attn_specs, row(DN_WIDTH), row(D_MODEL), row(D_MODEL),
                  _resident(wa.shape), _resident(wb.shape), _resident(wo.shape),
                  _resident((1, D_MODEL)), _resident(wq.shape)],
        out_specs=[row(D_MODEL), row(X_WIDTH)],
        out_shape=[jax.ShapeDtypeStruct((m, D_MODEL), F32), jax.ShapeDtypeStruct((m, X_WIDTH), BF16)],
        compiler_params=_cparams("parallel", "parallel"),
        name="merge",
    )(x, *attn_args, yb, ga, gb, wa, wb, wo, g, wq)


def _memkv_kernel(m_ref, g_ref, w_ref, k_ref, v_ref, kf_ref, vf_ref):
    tm = m_ref.shape[0]
    z = _dot(_rms(m_ref[...], g_ref[...]).astype(BF16), w_ref[...])
    k_ref[...] = z[:, :X_WIDTH]
    v_ref[...] = z[:, X_WIDTH:]
    for h in range(X_HEADS):
        kf_ref[pl.ds(h, tm, stride=X_HEADS), :] = z[:, h * LANES:(h + 1) * LANES]
        vf_ref[pl.ds(h, tm, stride=X_HEADS), :] = z[:, X_WIDTH + h * LANES:X_WIDTH + (h + 1) * LANES]


def _memkv(mem, g, w, tm):
    m = mem.shape[0]
    row = lambda wd: pl.BlockSpec((tm, wd), lambda i: (i, 0))
    flat = pl.BlockSpec((tm * X_HEADS, X_HEAD_DIM), lambda i: (i, 0))
    return pl.pallas_call(
        _memkv_kernel,
        grid=(m // tm,),
        in_specs=[row(D_MODEL), _resident((1, D_MODEL)), _resident(w.shape)],
        out_specs=[row(X_WIDTH), row(X_WIDTH), flat, flat],
        out_shape=[jax.ShapeDtypeStruct((m, X_WIDTH), F32)] * 2
        + [jax.ShapeDtypeStruct((m * X_HEADS, X_HEAD_DIM), F32)] * 2,
        compiler_params=_cparams("parallel"),
        name="mem_kv",
    )(mem, g, w)


def _softmax_rows(s):
    e = jnp.exp(s - jnp.max(s, axis=-1, keepdims=True))
    return e * (1.0 / jnp.sum(e, axis=-1, keepdims=True))


def _xattn_sample_kernel(hq_ref, mk_ref, mv_ref, o_ref, *, nb):
    rows = hq_ref.shape[1]
    nmh = mk_ref.shape[1]
    qh = lax.broadcasted_iota(jnp.int32, (X_HEADS * rows, nmh), 0) // rows
    mh = lax.broadcasted_iota(jnp.int32, (X_HEADS * rows, nmh), 1) % X_HEADS
    same = qh == mh
    seqs = range(nb)
    qs = [jnp.concatenate([hq_ref[b, :, h * LANES:(h + 1) * LANES] for h in range(X_HEADS)], axis=0)
          for b in seqs]
    ss = [_dot_nt(qs[b], mk_ref[b].astype(BF16)) * (X_HEAD_DIM ** -0.5) for b in seqs]
    ps = [_softmax_rows(jnp.where(same, s, -jnp.inf)).astype(BF16) for s in ss]
    os_ = [_dot(ps[b], mv_ref[b].astype(BF16)) for b in seqs]
    for b in seqs:
        for h in range(X_HEADS):
            o_ref[b, :, h * LANES:(h + 1) * LANES] = os_[b][h * rows:(h + 1) * rows].astype(BF16)


def _xattn_sample(hq, mk, mv, nb=8):
    bsz, rows, _ = hq.shape
    q = pl.BlockSpec((nb, rows, X_WIDTH), lambda i: (i, 0, 0))
    mem = pl.BlockSpec((nb, mk.shape[1], X_HEAD_DIM), lambda i: (i, 0, 0))
    return pl.pallas_call(
        functools.partial(_xattn_sample_kernel, nb=nb),
        grid=(bsz // nb,),
        in_specs=[q, mem, mem],
        out_specs=q,
        out_shape=jax.ShapeDtypeStruct(hq.shape, BF16),
        compiler_params=_cparams("parallel"),
        name="xattn_sample",
    )(hq, mk, mv)


def _xattn_tile(hq_ref, mk_ref, mv_ref):
    heads = [slice(h * LANES, (h + 1) * LANES) for h in range(X_HEADS)]
    ss = [_dot_nt(hq_ref[:, hs], mk_ref[:, hs].astype(BF16)) * (X_HEAD_DIM ** -0.5) for hs in heads]
    ps = [_softmax_rows(s).astype(BF16) for s in ss]
    return jnp.concatenate([_dot(p, mv_ref[:, hs].astype(BF16)).astype(BF16) for p, hs in zip(ps, heads)], axis=1)


def _ffn_kernel(h_ref, *rest, stride, padc, fc, attend):
    if attend:
        hq_ref, mk_ref, mv_ref, *rest = rest
        xo = _xattn_tile(hq_ref, mk_ref, mv_ref)
    else:
        xo_ref, *rest = rest
        xo = xo_ref[...]
    cin_ref, wxo_ref, g_ref, wu_ref, wv_ref, cw_ref, wd_ref, fg_ref, y_ref, tail_ref, xp_ref, acc_ref = rest
    tm = h_ref.shape[0]

    @pl.when(pl.program_id(1) == 0)
    def _():
        tail_ref[...] = cin_ref[...]

    h = h_ref[...] + _dot(xo, wxo_ref[...])
    acc_ref[...] = h
    hn = _rms(h, g_ref[...]).astype(BF16)
    for lo, hi in zip((0,) + fc, fc + (D_FF,)):
        cs = slice(lo, hi)
        n = hi - lo
        u = _dot(hn, wu_ref[:, cs])
        gate = _dot(hn, wv_ref[:, cs])
        xp_ref[0:padc, 0:n] = tail_ref[0, :, cs]
        xp_ref[padc:padc + tm, 0:n] = u
        cw = cw_ref[:, cs]
        uc = u * cw[2:3]
        for j in range(FFN_CONV - 1):
            o = padc - (FFN_CONV - 1 - j) * stride
            uc = uc + xp_ref[o:o + tm, 0:n] * cw[j:j + 1]
        tail_ref[0, :, cs] = xp_ref[tm:tm + padc, 0:n]
        acc_ref[...] += _dot((_silu(uc) * gate).astype(BF16), wd_ref[cs, :])
    y_ref[...] = _rms(acc_ref[...], fg_ref[...])


FFN_CUTS = ()


def _ffn(h, xq, mem, cin, wxo, g, wup, cw, wd, fg, tm, stride, fc=FFN_CUTS):
    m = h.shape[0]
    half = lambda c: pl.BlockSpec((D_MODEL, D_FF), lambda *_: (0, c), pipeline_mode=pl.Buffered(1))
    nseq, padc, _ = cin.shape
    nt = m // nseq // tm
    row = lambda w: pl.BlockSpec((tm, w), lambda s, j: (s * nt + j, 0))
    car = pl.BlockSpec((1, padc, D_FF), lambda s, j: (s, 0, 0))
    mem = list(mem or ())
    mem_specs = [pl.BlockSpec((a.shape[0] // nseq, X_WIDTH), lambda s, j: (s, 0)) for a in mem]
    return pl.pallas_call(
        functools.partial(_ffn_kernel, stride=stride, padc=padc, fc=fc, attend=bool(mem)),
        grid=(nseq, nt),
        in_specs=[row(D_MODEL), row(X_WIDTH), *mem_specs, car, _resident(wxo.shape), _resident((1, D_MODEL)),
                  half(0), half(1), _resident(cw.shape), _resident(wd.shape), _resident((1, D_MODEL))],
        out_specs=[row(D_MODEL), car],
        out_shape=[jax.ShapeDtypeStruct((m, D_MODEL), F32), jax.ShapeDtypeStruct(cin.shape, F32)],
        scratch_shapes=[pltpu.VMEM((tm + padc, max(b - a for a, b in zip((0,) + fc, fc + (D_FF,)))), F32),
                        pltpu.VMEM((tm, D_MODEL), F32)],
        compiler_params=_cparams("parallel", "arbitrary"),
        name="ffn",
    )(h, xq, *mem, cin, wxo, g, wup, wup, cw, wd, fg)


def _lane_row(vec, offset=0):
    return jnp.zeros((1, LANES), F32).at[0, offset:offset + vec.shape[0]].set(vec.astype(F32))


W_IN_QKV = A_WIDTH + 2 * A_KV_WIDTH
W_IN_D = W_IN_QKV + 4 * DN_WIDTH
W_IN_AB = W_IN_D + 2 * DN_HEADS
W_IN_G = W_IN_AB + 2 * D_MODEL


def _split_w_in_kernel(wt_ref, qkv_ref, d_ref, ab_ref, g_ref):
    tk = wt_ref.shape[1]
    qkv_ref[...] = wt_ref[:W_IN_QKV, :].T.astype(BF16)
    d_ref[...] = wt_ref[W_IN_QKV:W_IN_D, :].T.astype(BF16)
    ab = jnp.concatenate([wt_ref[W_IN_D:W_IN_AB, :], jnp.zeros((LANES - (W_IN_AB - W_IN_D), tk), F32)], axis=0)
    ab_ref[...] = ab.T.astype(BF16)
    g_ref[...] = wt_ref[W_IN_AB:W_IN_G, :].T.astype(BF16)


def _split_w_in(w_in_t, tk=256):
    k = w_in_t.shape[1]
    widths = (W_IN_QKV, W_IN_D - W_IN_QKV, LANES, W_IN_G - W_IN_AB)
    return pl.pallas_call(
        _split_w_in_kernel,
        grid=(k // tk,),
        in_specs=[pl.BlockSpec((w_in_t.shape[0], tk), lambda i: (0, i))],
        out_specs=[pl.BlockSpec((tk, wd), lambda i: (i, 0)) for wd in widths],
        out_shape=[jax.ShapeDtypeStruct((k, wd), BF16) for wd in widths],
        compiler_params=_cparams("parallel"),
        name="split_w_in",
    )(w_in_t)


def _prep(p):
    wqkv, wd, wab, wg = _split_w_in(p["w_in"].T)
    w = {
        "wqkv": wqkv,
        "wd": wd,
        "wab": wab,
        "wg": wg,
        "norm_mix_g": p["norm_mix_g"].reshape(1, D_MODEL),
        "dn_conv_w": p["dn_conv_w"],
        "alog": _lane_row(p["dn_a_log"]),
        "dtb": _lane_row(p["dn_dt_bias"]),
        "dn_norm_g": p["dn_norm_g"].reshape(1, DN_HEAD_DIM),
        "sinks": p["sinks"].astype(F32),
        "w_br_a": p["w_br_a"].astype(BF16),
        "w_br_b": p["w_br_b"].astype(BF16),
        "w_mix_out": p["w_mix_out"].astype(BF16),
        "norm_x_g": p["norm_x_g"].reshape(1, D_MODEL),
        "w_xq": p["w_xq"].astype(BF16),
        "w_xo": p["w_xo"].astype(BF16),
        "norm_ffn_g": p["norm_ffn_g"].reshape(1, D_MODEL),
        "w_up": p["w_up"].astype(BF16),
        "ffn_conv_w": p["ffn_conv_w"],
        "w_down": p["w_down"].astype(BF16),
        "final_norm_g": p["final_norm_g"].reshape(1, D_MODEL),
    }
    return w


def _tile(m, pref):
    return pref if m % pref == 0 else m


def _prompt_layer(x, mem, norm_mem_g, w_xkv, w):
    bsz, seqlen, _ = x.shape
    m = bsz * seqlen
    x2 = x.reshape(m, D_MODEL)
    tm = _tile(seqlen, ROW_TILE)
    tabs = _rope_tables(jnp.arange(seqlen, dtype=jnp.int32))
    assert seqlen >= WINDOW, "the window outputs are taken from a full last window"
    q, k, v, ga, gb, kwin, vwin, yb, dn_state, dn_tail = _proj_gdn(
        x2, w["norm_mix_g"], tabs, w["wqkv"], w["wd"], w["wab"], w["wg"], w["dn_conv_w"], w["alog"], w["dtb"],
        w["dn_norm_g"], bsz, tm)
    h, hq = _merge(x2, (w["sinks"], q, k, v), yb, ga, gb, w["w_br_a"], w["w_br_b"], w["w_mix_out"],
                   w["norm_x_g"], w["w_xq"], bsz, tm)
    nm = mem.shape[1]
    mk, mv, mk_rows, mv_rows = _memkv(mem.reshape(bsz * nm, D_MODEL), norm_mem_g.reshape(1, D_MODEL),
                                      w_xkv.astype(BF16), _tile(bsz * nm, ROW_TILE))
    cin = jnp.zeros((bsz, SUBLANES, D_FF), F32)
    y, tail = _ffn(h, hq, (mk, mv), cin, w["w_xo"], w["norm_ffn_g"], w["w_up"], w["ffn_conv_w"],
                   w["w_down"], w["final_norm_g"], tm, 1)
    win = lambda a: jnp.transpose(a.reshape(bsz, A_KV_HEADS, A_HEAD_DIM, WINDOW), (0, 3, 1, 2))
    new = (
        win(kwin),
        win(vwin),
        dn_tail[:, SUBLANES - (DN_CONV - 1):],
        dn_state,
        mk_rows.reshape(bsz, nm, X_HEADS, X_HEAD_DIM),
        mv_rows.reshape(bsz, nm, X_HEADS, X_HEAD_DIM),
        tail[:, SUBLANES - (FFN_CONV - 1):],
    )
    return y.reshape(bsz, seqlen, D_MODEL), new


def _sample_layer(x, pos0, win_k, win_v, dn_buf, dn_state, mem_k, mem_v, ffn_buf, w):
    bsz, t, _ = x.shape
    m = bsz * t
    sr = SAMPLE_ROWS
    x2 = x.reshape(m, D_MODEL)
    tabs = _rope_tables(jnp.tile(pos0 + jnp.arange(t, dtype=jnp.int32), bsz))
    q, k, v, dqkv, dz, ab, ga, gb = _proj(x2, w["norm_mix_g"], tabs, w["wqkv"], w["wd"], w["wab"], w["wg"], m)

    qh = jnp.transpose(q.reshape(bsz, t, A_HEADS, A_HEAD_DIM), (0, 2, 1, 3)).reshape(bsz, A_HEADS * t, A_HEAD_DIM)
    padrows = lambda a: jnp.pad(a.reshape(bsz, t, -1), ((0, 0), (0, sr - t), (0, 0)))
    kv_major = lambda a: jnp.transpose(a, (0, 2, 3, 1))
    nbw = min(bsz, LANES // t, 16)

    def new_t(a):
        a = jnp.transpose(a.reshape(bsz // nbw, nbw * t, A_KV_HEADS, A_HEAD_DIM), (0, 2, 3, 1))
        return jnp.pad(a, ((0, 0), (0, 0), (0, 0), (0, LANES - nbw * t)))

    oh, new_wk, new_wv = _swa_sample(qh, kv_major(win_k), kv_major(win_v), new_t(k), new_t(v), w["sinks"], t)
    ya = jnp.transpose(oh.reshape(bsz, A_HEADS, t, A_HEAD_DIM), (0, 2, 1, 3)).reshape(m, A_WIDTH)
    new_wk, new_wv = (jnp.transpose(a, (0, 3, 1, 2)) for a in (new_wk, new_wv))

    yb, new_s, new_dn_buf = _gdn_sample(
        jnp.transpose(dn_buf, (1, 0, 2)), dqkv.reshape(bsz, t, 3 * DN_WIDTH), dz.reshape(bsz, t, DN_WIDTH),
        ab.reshape(bsz, t, LANES), dn_state, w["dn_conv_w"], w["alog"], w["dtb"], w["dn_norm_g"])
    yb = yb.reshape(m, DN_WIDTH)
    new_dn_buf = jnp.transpose(new_dn_buf, (1, 0, 2))

    h, hq = _merge(x2, ya, yb, ga, gb, w["w_br_a"], w["w_br_b"], w["w_mix_out"], w["norm_x_g"], w["w_xq"], 1, m)
    nm = mem_k.shape[1]
    xo = _xattn_sample(padrows(hq), mem_k.reshape(bsz, nm * X_HEADS, X_HEAD_DIM),
                       mem_v.reshape(bsz, nm * X_HEADS, X_HEAD_DIM))
    xo = xo[:, :t]

    tmaj = lambda a: jnp.transpose(a.reshape(bsz, t, -1), (1, 0, 2)).reshape(m, -1)
    cin = jnp.transpose(ffn_buf, (1, 0, 2)).reshape(1, (FFN_CONV - 1) * bsz, D_FF)
    y, tail = _ffn(tmaj(h), tmaj(xo), None, cin, w["w_xo"], w["norm_ffn_g"], w["w_up"], w["ffn_conv_w"],
                   w["w_down"], w["final_norm_g"], m, bsz)
    y = jnp.transpose(y.reshape(t, bsz, D_MODEL), (1, 0, 2))
    new_ffn = jnp.transpose(tail.reshape(FFN_CONV - 1, bsz, D_FF), (1, 0, 2))
    return y, (new_wk, new_wv, new_dn_buf, new_s, new_ffn)


def kernel(x_prompt, x_sample, mem_prompt, cache_win_k, cache_win_v, state_dn_conv, state_dn, cache_mem_k, cache_mem_v, state_ffn_conv, norm_mix_g, w_in, dn_conv_w, dn_a_log, dn_dt_bias, dn_norm_g, attn_sinks, w_br_a, w_br_b, w_mix_out, norm_x_g, norm_mem_g, w_xq, w_xkv, w_xo, norm_ffn_g, w_up, ffn_conv_w, w_down, final_norm_g):
    p = {"norm_mix_g": norm_mix_g[0], "w_in": w_in[0], "dn_conv_w": dn_conv_w[0], "dn_a_log": dn_a_log[0],
         "dn_dt_bias": dn_dt_bias[0], "dn_norm_g": dn_norm_g[0], "sinks": attn_sinks[0], "w_br_a": w_br_a[0],
         "w_br_b": w_br_b[0], "w_mix_out": w_mix_out[0], "norm_x_g": norm_x_g[0], "w_xq": w_xq[0],
         "w_xo": w_xo[0], "norm_ffn_g": norm_ffn_g[0], "w_up": w_up[0], "ffn_conv_w": ffn_conv_w[0],
         "w_down": w_down[0], "final_norm_g": final_norm_g}
    w = _prep(p)
    yp, newp = _prompt_layer(x_prompt, mem_prompt, norm_mem_g[0], w_xkv[0], w)
    ys, news = _sample_layer(x_sample, PAST_LEN, cache_win_k[0], cache_win_v[0], state_dn_conv[0], state_dn[0],
                             cache_mem_k[0], cache_mem_v[0], state_ffn_conv[0], w)
    lead = lambda a: a[None]
    p_win_k, p_win_v, p_dn_conv, p_dn_state, p_mem_k, p_mem_v, p_ffn_conv = [lead(a) for a in newp]
    s_win_k, s_win_v, s_dn_conv, s_dn_state, s_ffn_conv = [lead(a) for a in news]
    return (yp, ys, p_win_k, p_win_v, p_dn_conv, p_dn_state, p_mem_k, p_mem_v, p_ffn_conv,
            s_win_k, s_win_v, s_dn_conv, s_dn_state, s_ffn_conv)
```
